```python
import math
import jax, jax.numpy as jnp
from jax import lax
import numpy as np


D_MODEL = 1024
BATCH = 8
SEQ = 4096
DEPTH = 4

D_ATTN = D_MODEL // 2
HEAD_DIM = 64
N_HEADS = D_ATTN // HEAD_DIM
ROPE_DIM = HEAD_DIM // 4
ROPE_THETA = 500000.0
DILATED_PATTERNS = ((128, 1), (512, 4), (2048, 16))
D_SSM = D_MODEL - D_ATTN
SSM_GROUP = 16
N_SSM_GROUPS = D_SSM // SSM_GROUP
SSM_STATE = 64
DT_MIN = 0.001
DT_MAX = 0.1
D_MIX = D_ATTN + D_SSM
D_IN_PROJ = 3 * D_ATTN + D_SSM
D_FF = 128 * (-(-(8 * D_MODEL // 3) // 128))
PLE_DIM = 256
NORM_EPS = 1e-6

kernel_name = 'hybrid_s5_dilated_macaron_block'


def rms_norm(x, g):
    xf = x.astype(jnp.float32)
    y = xf * lax.rsqrt(jnp.mean(xf * xf, axis=-1, keepdims=True) + NORM_EPS)
    return (y * g.astype(jnp.float32)).astype(x.dtype)


def swiglu(x, w_gate, w_up, w_down):
    return (jax.nn.silu(x @ w_gate) * (x @ w_up)) @ w_down


def partial_rotary(t, positions):
    half = ROPE_DIM // 2
    inv_freq = ROPE_THETA ** (-jnp.arange(half, dtype=jnp.float32) * (2.0 / ROPE_DIM))
    ang = positions.astype(jnp.float32)[:, :, None, None] * inv_freq
    cos, sin = jnp.cos(ang), jnp.sin(ang)
    tf = t.astype(jnp.float32)
    t1, t2, rest = tf[..., :half], tf[..., half:ROPE_DIM], tf[..., ROPE_DIM:]
    out = jnp.concatenate([t1 * cos - t2 * sin, t2 * cos + t1 * sin, rest], axis=-1)
    return out.astype(t.dtype)


def dilated_band_attention(q, k, v, window, dilation):
    b_, s_, h_, dh = q.shape
    band = window // dilation
    n_str = s_ // dilation
    nb = -(-n_str // band)
    lp = nb * band
    nrows = b_ * dilation

    def to_blocks(t):
        t = t.reshape(b_, n_str, dilation, h_, dh).transpose(0, 2, 1, 3, 4).reshape(nrows, n_str, h_, dh)
        t = jnp.pad(t, ((0, 0), (0, lp - n_str), (0, 0), (0, 0)))
        return t.reshape(nrows, nb, band, h_, dh)

    def with_prev(t):
        prev = jnp.pad(t[:, :-1], ((0, 0), (1, 0), (0, 0), (0, 0), (0, 0)))
        return jnp.concatenate([prev, t], axis=2)

    qb = to_blocks(q)
    kc = with_prev(to_blocks(k))
    vc = with_prev(to_blocks(v))
    scores = jnp.einsum('nbqhd,nbkhd->nbhqk', qb, kc).astype(jnp.float32) * (HEAD_DIM ** -0.5)
    qi = jnp.arange(band)[:, None]
    kj = jnp.arange(2 * band)[None, :]
    dist = qi + band - kj
    band_ok = (dist >= 0) & (dist <= band)
    blk = jnp.arange(nb)[:, None, None]
    mask = band_ok[None] & ((blk > 0) | (kj >= band)[None])
    scores = jnp.where(mask[None, :, None], scores, -jnp.inf)
    m = jnp.max(scores, axis=-1, keepdims=True)
    e = jnp.exp(scores - m)
    den = jnp.sum(e, axis=-1, keepdims=True)
    probs = (e / den).astype(v.dtype)
    out = jnp.einsum('nbhqk,nbkhd->nbqhd', probs, vc).astype(jnp.float32)
    lse = (m + jnp.log(den))[..., 0]
    out = out.reshape(nrows, lp, h_, dh)[:, :n_str]
    out = out.reshape(b_, dilation, n_str, h_, dh).transpose(0, 2, 1, 3, 4).reshape(b_, s_, h_, dh)
    lse = lse.transpose(0, 1, 3, 2).reshape(nrows, lp, h_)[:, :n_str]
    lse = lse.reshape(b_, dilation, n_str, h_).transpose(0, 2, 1, 3).reshape(b_, s_, h_)
    return out, lse


def dilated_mixture_attention(q, k, v):
    outs, lses = [], []
    for window, dilation in DILATED_PATTERNS:
        o, l = dilated_band_attention(q, k, v, window, dilation)
        outs.append(o)
        lses.append(l)
    wts = jax.nn.softmax(jnp.stack(lses, axis=0), axis=0)
    out = jnp.sum(wts[..., None] * jnp.stack(outs, axis=0), axis=0)
    b_, s_ = q.shape[0], q.shape[1]
    return out.reshape(b_, s_, D_ATTN).astype(q.dtype)


def s5_mixer(u, lam_re, lam_im, log_dt, b_re, b_im, c_re, c_im, d_skip, w_glu, b_glu):
    b_, s_ = u.shape[0], u.shape[1]
    uf = u.astype(jnp.float32).reshape(b_, s_, N_SSM_GROUPS, SSM_GROUP)
    lam = lax.complex(lam_re.astype(jnp.float32), lam_im.astype(jnp.float32))
    dt = jnp.exp(log_dt.astype(jnp.float32))[:, None]
    lam_bar = jnp.exp(lam * dt)
    b_mat = lax.complex(b_re.astype(jnp.float32), b_im.astype(jnp.float32))
    b_bar = ((lam_bar - 1.0) / lam)[..., None] * b_mat
    bu = jnp.einsum('bsgh,gph->bsgp', uf.astype(jnp.complex64), b_bar)
    a = jnp.broadcast_to(lam_bar, bu.shape)

    def combine(left, right):
        a_l, x_l = left
        a_r, x_r = right
        return a_r * a_l, a_r * x_l + x_r

    _, states = lax.associative_scan(combine, (a, bu), axis=1)
    c_mat = lax.complex(c_re.astype(jnp.float32), c_im.astype(jnp.float32))
    y = jnp.real(jnp.einsum('bsgp,ghp->bsgh', states, c_mat)) + d_skip.astype(jnp.float32) * uf
    y = jax.nn.gelu(y).reshape(b_, s_, D_SSM)
    y = y * jax.nn.sigmoid(y @ w_glu.astype(jnp.float32) + b_glu.astype(jnp.float32))
    return y.astype(u.dtype)


def _fwd_setup_inputs(seed: int = 0) -> dict:
    key = jax.random.key(seed)
    ks = jax.random.split(key, 40)
    f32 = jnp.float32

    def nrm(k, shape, fan_in):
        return jax.random.normal(k, shape, f32) * (fan_in ** -0.5)

    def gain(k, shape):
        return 1.0 + 0.05 * jax.random.normal(k, shape, f32)

    lam_im_base = jnp.pi * jnp.arange(SSM_STATE, dtype=f32)
    return {
        'x': jax.random.normal(ks[0], (BATCH, SEQ, D_MODEL), f32),
        'p': jax.random.normal(ks[1], (DEPTH, BATCH, SEQ, PLE_DIM), f32),
        'positions': (jax.random.randint(ks[2], (BATCH, 1), 0, 1024, dtype=jnp.int32)
                      + jnp.arange(SEQ, dtype=jnp.int32)[None, :]),
        'ffn1_pre_g': gain(ks[3], (DEPTH, D_MODEL)),
        'ffn1_w_gate': nrm(ks[4], (DEPTH, D_MODEL, D_FF), D_MODEL),
        'ffn1_w_up': nrm(ks[5], (DEPTH, D_MODEL, D_FF), D_MODEL),
        'ffn1_w_down': nrm(ks[6], (DEPTH, D_FF, D_MODEL), D_FF),
        'ffn1_post_g': gain(ks[7], (DEPTH, D_MODEL)),
        'mix_pre_g': gain(ks[8], (DEPTH, D_MODEL)),
        'w_in': nrm(ks[9], (DEPTH, D_MODEL, D_IN_PROJ), D_MODEL),
        'attn_norm_g': gain(ks[10], (DEPTH, D_ATTN)),
        'ssm_lam_re': -0.5 + 0.01 * jax.random.normal(ks[11], (DEPTH, N_SSM_GROUPS, SSM_STATE), f32),
        'ssm_lam_im': lam_im_base + 0.01 * jax.random.normal(ks[12], (DEPTH, N_SSM_GROUPS, SSM_STATE), f32),
        'ssm_log_dt': jax.random.uniform(ks[13], (DEPTH, N_SSM_GROUPS), f32,
                                         minval=math.log(DT_MIN), maxval=math.log(DT_MAX)),
        'ssm_b_re': nrm(ks[14], (DEPTH, N_SSM_GROUPS, SSM_STATE, SSM_GROUP), 2 * SSM_GROUP),
        'ssm_b_im': nrm(ks[15], (DEPTH, N_SSM_GROUPS, SSM_STATE, SSM_GROUP), 2 * SSM_GROUP),
        'ssm_c_re': nrm(ks[16], (DEPTH, N_SSM_GROUPS, SSM_GROUP, SSM_STATE), 2 * SSM_STATE),
        'ssm_c_im': nrm(ks[17], (DEPTH, N_SSM_GROUPS, SSM_GROUP, SSM_STATE), 2 * SSM_STATE),
        'ssm_d': jax.random.normal(ks[18], (DEPTH, N_SSM_GROUPS, SSM_GROUP), f32),
        'ssm_w_glu': nrm(ks[19], (DEPTH, D_SSM, D_SSM), D_SSM),
        'ssm_b_glu': 0.01 * jax.random.normal(ks[20], (DEPTH, D_SSM), f32),
        'ssm_norm_g': gain(ks[21], (DEPTH, D_SSM)),
        'w_out': nrm(ks[22], (DEPTH, D_MIX, D_MODEL), D_MIX),
        'mix_post_g': gain(ks[23], (DEPTH, D_MODEL)),
        'ffn2_pre_g': gain(ks[24], (DEPTH, D_MODEL)),
        'ffn2_w_gate': nrm(ks[25], (DEPTH, D_MODEL, D_FF), D_MODEL),
        'ffn2_w_up': nrm(ks[26], (DEPTH, D_MODEL, D_FF), D_MODEL),
        'ffn2_w_down': nrm(ks[27], (DEPTH, D_FF, D_MODEL), D_FF),
        'ffn2_post_g': gain(ks[28], (DEPTH, D_MODEL)),
        'ple_w_up': nrm(ks[29], (DEPTH, PLE_DIM, D_MODEL), PLE_DIM),
        'ple_w_gate': nrm(ks[30], (DEPTH, D_MODEL, D_MODEL), D_MODEL),
        'ple_post_g': gain(ks[31], (DEPTH, D_MODEL)),
    }


def _fwd_reference(x, p, positions,
              ffn1_pre_g, ffn1_w_gate, ffn1_w_up, ffn1_w_down, ffn1_post_g,
              mix_pre_g, w_in, attn_norm_g,
              ssm_lam_re, ssm_lam_im, ssm_log_dt, ssm_b_re, ssm_b_im, ssm_c_re, ssm_c_im,
              ssm_d, ssm_w_glu, ssm_b_glu, ssm_norm_g, w_out, mix_post_g,
              ffn2_pre_g, ffn2_w_gate, ffn2_w_up, ffn2_w_down, ffn2_post_g,
              ple_w_up, ple_w_gate, ple_post_g):
    b_, s_ = x.shape[0], x.shape[1]
    h = x
    for i in range(DEPTH):
        f = swiglu(rms_norm(h, ffn1_pre_g[i]), ffn1_w_gate[i], ffn1_w_up[i], ffn1_w_down[i])
        h = h + 0.5 * rms_norm(f, ffn1_post_g[i])

        a_in = rms_norm(h, mix_pre_g[i])
        proj = a_in @ w_in[i]
        q, k, v, u = jnp.split(proj, [D_ATTN, 2 * D_ATTN, 3 * D_ATTN], axis=-1)
        q = partial_rotary(q.reshape(b_, s_, N_HEADS, HEAD_DIM), positions)
        k = partial_rotary(k.reshape(b_, s_, N_HEADS, HEAD_DIM), positions)
        v = v.reshape(b_, s_, N_HEADS, HEAD_DIM)
        attn = dilated_mixture_attention(q, k, v)
        ssm = s5_mixer(u, ssm_lam_re[i], ssm_lam_im[i], ssm_log_dt[i], ssm_b_re[i], ssm_b_im[i],
                       ssm_c_re[i], ssm_c_im[i], ssm_d[i], ssm_w_glu[i], ssm_b_glu[i])
        mixed = jnp.concatenate([rms_norm(attn, attn_norm_g[i]), rms_norm(ssm, ssm_norm_g[i])], axis=-1)
        h = h + rms_norm(mixed @ w_out[i], mix_post_g[i])

        f = swiglu(rms_norm(h, ffn2_pre_g[i]), ffn2_w_gate[i], ffn2_w_up[i], ffn2_w_down[i])
        h = h + 0.5 * rms_norm(f, ffn2_post_g[i])

        ple = (p[i] @ ple_w_up[i]) * jax.nn.sigmoid(h @ ple_w_gate[i])
        h = h + rms_norm(ple, ple_post_g[i])
    return h


import jax as _jax
import jax.numpy as _jnp

TWIN_FORMAT = 'train_step'
FWD_PARAMS = ['x', 'p', 'positions', 'ffn1_pre_g', 'ffn1_w_gate', 'ffn1_w_up', 'ffn1_w_down', 'ffn1_post_g', 'mix_pre_g', 'w_in', 'attn_norm_g', 'ssm_lam_re', 'ssm_lam_im', 'ssm_log_dt', 'ssm_b_re', 'ssm_b_im', 'ssm_c_re', 'ssm_c_im', 'ssm_d', 'ssm_w_glu', 'ssm_b_glu', 'ssm_norm_g', 'w_out', 'mix_post_g', 'ffn2_pre_g', 'ffn2_w_gate', 'ffn2_w_up', 'ffn2_w_down', 'ffn2_post_g', 'ple_w_up', 'ple_w_gate', 'ple_post_g']
TWIN_WEIGHTS = ['ffn1_pre_g', 'ffn1_w_gate', 'ffn1_w_up', 'ffn1_w_down', 'ffn1_post_g', 'mix_pre_g', 'w_in', 'attn_norm_g', 'ssm_lam_re', 'ssm_lam_im', 'ssm_log_dt', 'ssm_b_re', 'ssm_b_im', 'ssm_c_re', 'ssm_c_im', 'ssm_d', 'ssm_w_glu', 'ssm_b_glu', 'ssm_norm_g', 'w_out', 'mix_post_g', 'ffn2_pre_g', 'ffn2_w_gate', 'ffn2_w_up', 'ffn2_w_down', 'ffn2_post_g', 'ple_w_up', 'ple_w_gate', 'ple_post_g']
TWIN_DIFF_INPUT = 'x'
TWIN_INPUTS = ['x', 'p', 'positions', 'ffn1_pre_g', 'ffn1_w_gate', 'ffn1_w_up', 'ffn1_w_down', 'ffn1_post_g', 'mix_pre_g', 'w_in', 'attn_norm_g', 'ssm_lam_re', 'ssm_lam_im', 'ssm_log_dt', 'ssm_b_re', 'ssm_b_im', 'ssm_c_re', 'ssm_c_im', 'ssm_d', 'ssm_w_glu', 'ssm_b_glu', 'ssm_norm_g', 'w_out', 'mix_post_g', 'ffn2_pre_g', 'ffn2_w_gate', 'ffn2_w_up', 'ffn2_w_down', 'ffn2_post_g', 'ple_w_up', 'ple_w_gate', 'ple_post_g', 'loss_target', 'm_ffn1_pre_g', 'm_ffn1_w_gate', 'm_ffn1_w_up', 'm_ffn1_w_down', 'm_ffn1_post_g', 'm_mix_pre_g', 'm_w_in', 'm_attn_norm_g', 'm_ssm_lam_re', 'm_ssm_lam_im', 'm_ssm_log_dt', 'm_ssm_b_re', 'm_ssm_b_im', 'm_ssm_c_re', 'm_ssm_c_im', 'm_ssm_d', 'm_ssm_w_glu', 'm_ssm_b_glu', 'm_ssm_norm_g', 'm_w_out', 'm_mix_post_g', 'm_ffn2_pre_g', 'm_ffn2_w_gate', 'm_ffn2_w_up', 'm_ffn2_w_down', 'm_ffn2_post_g', 'm_ple_w_up', 'm_ple_w_gate', 'm_ple_post_g', 'v_ffn1_pre_g', 'v_ffn1_w_gate', 'v_ffn1_w_up', 'v_ffn1_w_down', 'v_ffn1_post_g', 'v_mix_pre_g', 'v_w_in', 'v_attn_norm_g', 'v_ssm_lam_re', 'v_ssm_lam_im', 'v_ssm_log_dt', 'v_ssm_b_re', 'v_ssm_b_im', 'v_ssm_c_re', 'v_ssm_c_im', 'v_ssm_d', 'v_ssm_w_glu', 'v_ssm_b_glu', 'v_ssm_norm_g', 'v_w_out', 'v_mix_post_g', 'v_ffn2_pre_g', 'v_ffn2_w_gate', 'v_ffn2_w_up', 'v_ffn2_w_down', 'v_ffn2_post_g', 'v_ple_w_up', 'v_ple_w_gate', 'v_ple_post_g']
TWIN_OUTPUTS = ['loss', 'grad_x', 'grad_ffn1_pre_g', 'grad_ffn1_w_gate', 'grad_ffn1_w_up', 'grad_ffn1_w_down', 'grad_ffn1_post_g', 'grad_mix_pre_g', 'grad_w_in', 'grad_attn_norm_g', 'grad_ssm_lam_re', 'grad_ssm_lam_im', 'grad_ssm_log_dt', 'grad_ssm_b_re', 'grad_ssm_b_im', 'grad_ssm_c_re', 'grad_ssm_c_im', 'grad_ssm_d', 'grad_ssm_w_glu', 'grad_ssm_b_glu', 'grad_ssm_norm_g', 'grad_w_out', 'grad_mix_post_g', 'grad_ffn2_pre_g', 'grad_ffn2_w_gate', 'grad_ffn2_w_up', 'grad_ffn2_w_down', 'grad_ffn2_post_g', 'grad_ple_w_up', 'grad_ple_w_gate', 'grad_ple_post_g', 'delta_ffn1_pre_g', 'delta_ffn1_w_gate', 'delta_ffn1_w_up', 'delta_ffn1_w_down', 'delta_ffn1_post_g', 'delta_mix_pre_g', 'delta_w_in', 'delta_attn_norm_g', 'delta_ssm_lam_re', 'delta_ssm_lam_im', 'delta_ssm_log_dt', 'delta_ssm_b_re', 'delta_ssm_b_im', 'delta_ssm_c_re', 'delta_ssm_c_im', 'delta_ssm_d', 'delta_ssm_w_glu', 'delta_ssm_b_glu', 'delta_ssm_norm_g', 'delta_w_out', 'delta_mix_post_g', 'delta_ffn2_pre_g', 'delta_ffn2_w_gate', 'delta_ffn2_w_up', 'delta_ffn2_w_down', 'delta_ffn2_post_g', 'delta_ple_w_up', 'delta_ple_w_gate', 'delta_ple_post_g', 'new_m_ffn1_pre_g', 'new_m_ffn1_w_gate', 'new_m_ffn1_w_up', 'new_m_ffn1_w_down', 'new_m_ffn1_post_g', 'new_m_mix_pre_g', 'new_m_w_in', 'new_m_attn_norm_g', 'new_m_ssm_lam_re', 'new_m_ssm_lam_im', 'new_m_ssm_log_dt', 'new_m_ssm_b_re', 'new_m_ssm_b_im', 'new_m_ssm_c_re', 'new_m_ssm_c_im', 'new_m_ssm_d', 'new_m_ssm_w_glu', 'new_m_ssm_b_glu', 'new_m_ssm_norm_g', 'new_m_w_out', 'new_m_mix_post_g', 'new_m_ffn2_pre_g', 'new_m_ffn2_w_gate', 'new_m_ffn2_w_up', 'new_m_ffn2_w_down', 'new_m_ffn2_post_g', 'new_m_ple_w_up', 'new_m_ple_w_gate', 'new_m_ple_post_g', 'new_v_ffn1_pre_g', 'new_v_ffn1_w_gate', 'new_v_ffn1_w_up', 'new_v_ffn1_w_down', 'new_v_ffn1_post_g', 'new_v_mix_pre_g', 'new_v_w_in', 'new_v_attn_norm_g', 'new_v_ssm_lam_re', 'new_v_ssm_lam_im', 'new_v_ssm_log_dt', 'new_v_ssm_b_re', 'new_v_ssm_b_im', 'new_v_ssm_c_re', 'new_v_ssm_c_im', 'new_v_ssm_d', 'new_v_ssm_w_glu', 'new_v_ssm_b_glu', 'new_v_ssm_norm_g', 'new_v_w_out', 'new_v_mix_post_g', 'new_v_ffn2_pre_g', 'new_v_ffn2_w_gate', 'new_v_ffn2_w_up', 'new_v_ffn2_w_down', 'new_v_ffn2_post_g', 'new_v_ple_w_up', 'new_v_ple_w_gate', 'new_v_ple_post_g']
TWIN_LEAF_KINDS = {'loss': 'loss', 'grad_x': 'grad_x', 'grad_ffn1_pre_g': 'grad_w', 'grad_ffn1_w_gate': 'grad_w', 'grad_ffn1_w_up': 'grad_w', 'grad_ffn1_w_down': 'grad_w', 'grad_ffn1_post_g': 'grad_w', 'grad_mix_pre_g': 'grad_w', 'grad_w_in': 'grad_w', 'grad_attn_norm_g': 'grad_w', 'grad_ssm_lam_re': 'grad_w', 'grad_ssm_lam_im': 'grad_w', 'grad_ssm_log_dt': 'grad_w', 'grad_ssm_b_re': 'grad_w', 'grad_ssm_b_im': 'grad_w', 'grad_ssm_c_re': 'grad_w', 'grad_ssm_c_im': 'grad_w', 'grad_ssm_d': 'grad_w', 'grad_ssm_w_glu': 'grad_w', 'grad_ssm_b_glu': 'grad_w', 'grad_ssm_norm_g': 'grad_w', 'grad_w_out': 'grad_w', 'grad_mix_post_g': 'grad_w', 'grad_ffn2_pre_g': 'grad_w', 'grad_ffn2_w_gate': 'grad_w', 'grad_ffn2_w_up': 'grad_w', 'grad_ffn2_w_down': 'grad_w', 'grad_ffn2_post_g': 'grad_w', 'grad_ple_w_up': 'grad_w', 'grad_ple_w_gate': 'grad_w', 'grad_ple_post_g': 'grad_w', 'delta_ffn1_pre_g': 'delta_w', 'delta_ffn1_w_gate': 'delta_w', 'delta_ffn1_w_up': 'delta_w', 'delta_ffn1_w_down': 'delta_w', 'delta_ffn1_post_g': 'delta_w', 'delta_mix_pre_g': 'delta_w', 'delta_w_in': 'delta_w', 'delta_attn_norm_g': 'delta_w', 'delta_ssm_lam_re': 'delta_w', 'delta_ssm_lam_im': 'delta_w', 'delta_ssm_log_dt': 'delta_w', 'delta_ssm_b_re': 'delta_w', 'delta_ssm_b_im': 'delta_w', 'delta_ssm_c_re': 'delta_w', 'delta_ssm_c_im': 'delta_w', 'delta_ssm_d': 'delta_w', 'delta_ssm_w_glu': 'delta_w', 'delta_ssm_b_glu': 'delta_w', 'delta_ssm_norm_g': 'delta_w', 'delta_w_out': 'delta_w', 'delta_mix_post_g': 'delta_w', 'delta_ffn2_pre_g': 'delta_w', 'delta_ffn2_w_gate': 'delta_w', 'delta_ffn2_w_up': 'delta_w', 'delta_ffn2_w_down': 'delta_w', 'delta_ffn2_post_g': 'delta_w', 'delta_ple_w_up': 'delta_w', 'delta_ple_w_gate': 'delta_w', 'delta_ple_post_g': 'delta_w', 'new_m_ffn1_pre_g': 'new_m', 'new_m_ffn1_w_gate': 'new_m', 'new_m_ffn1_w_up': 'new_m', 'new_m_ffn1_w_down': 'new_m', 'new_m_ffn1_post_g': 'new_m', 'new_m_mix_pre_g': 'new_m', 'new_m_w_in': 'new_m', 'new_m_attn_norm_g': 'new_m', 'new_m_ssm_lam_re': 'new_m', 'new_m_ssm_lam_im': 'new_m', 'new_m_ssm_log_dt': 'new_m', 'new_m_ssm_b_re': 'new_m', 'new_m_ssm_b_im': 'new_m', 'new_m_ssm_c_re': 'new_m', 'new_m_ssm_c_im': 'new_m', 'new_m_ssm_d': 'new_m', 'new_m_ssm_w_glu': 'new_m', 'new_m_ssm_b_glu': 'new_m', 'new_m_ssm_norm_g': 'new_m', 'new_m_w_out': 'new_m', 'new_m_mix_post_g': 'new_m', 'new_m_ffn2_pre_g': 'new_m', 'new_m_ffn2_w_gate': 'new_m', 'new_m_ffn2_w_up': 'new_m', 'new_m_ffn2_w_down': 'new_m', 'new_m_ffn2_post_g': 'new_m', 'new_m_ple_w_up': 'new_m', 'new_m_ple_w_gate': 'new_m', 'new_m_ple_post_g': 'new_m', 'new_v_ffn1_pre_g': 'new_v', 'new_v_ffn1_w_gate': 'new_v', 'new_v_ffn1_w_up': 'new_v', 'new_v_ffn1_w_down': 'new_v', 'new_v_ffn1_post_g': 'new_v', 'new_v_mix_pre_g': 'new_v', 'new_v_w_in': 'new_v', 'new_v_attn_norm_g': 'new_v', 'new_v_ssm_lam_re': 'new_v', 'new_v_ssm_lam_im': 'new_v', 'new_v_ssm_log_dt': 'new_v', 'new_v_ssm_b_re': 'new_v', 'new_v_ssm_b_im': 'new_v', 'new_v_ssm_c_re': 'new_v', 'new_v_ssm_c_im': 'new_v', 'new_v_ssm_d': 'new_v', 'new_v_ssm_w_glu': 'new_v', 'new_v_ssm_b_glu': 'new_v', 'new_v_ssm_norm_g': 'new_v', 'new_v_w_out': 'new_v', 'new_v_mix_post_g': 'new_v', 'new_v_ffn2_pre_g': 'new_v', 'new_v_ffn2_w_gate': 'new_v', 'new_v_ffn2_w_up': 'new_v', 'new_v_ffn2_w_down': 'new_v', 'new_v_ffn2_post_g': 'new_v', 'new_v_ple_w_up': 'new_v', 'new_v_ple_w_gate': 'new_v', 'new_v_ple_post_g': 'new_v'}


def _forward(args):
    return _fwd_reference(*[args[k] for k in FWD_PARAMS])


def _output_shape():
    out = _jax.eval_shape(lambda: _forward(_fwd_setup_inputs(0)))
    return out.shape, out.dtype

N_MICROBATCH = 1
ADAM_LR = 0.001
ADAM_B1 = 0.9
ADAM_B2 = 0.999
ADAM_EPS = 1e-08
ADAM_WD = 0.01
ADAM_STEP = 10
PER_EXAMPLE_BATCH_AXIS = {'x': 0, 'p': 1, 'positions': 0, 'loss_target': 0}
SHARED_INPUTS = []
_WEIGHT_DTYPES = {'ffn1_pre_g': _jnp.float32, 'ffn1_w_gate': _jnp.float32, 'ffn1_w_up': _jnp.float32, 'ffn1_w_down': _jnp.float32, 'ffn1_post_g': _jnp.float32, 'mix_pre_g': _jnp.float32, 'w_in': _jnp.float32, 'attn_norm_g': _jnp.float32, 'ssm_lam_re': _jnp.float32, 'ssm_lam_im': _jnp.float32, 'ssm_log_dt': _jnp.float32, 'ssm_b_re': _jnp.float32, 'ssm_b_im': _jnp.float32, 'ssm_c_re': _jnp.float32, 'ssm_c_im': _jnp.float32, 'ssm_d': _jnp.float32, 'ssm_w_glu': _jnp.float32, 'ssm_b_glu': _jnp.float32, 'ssm_norm_g': _jnp.float32, 'w_out': _jnp.float32, 'mix_post_g': _jnp.float32, 'ffn2_pre_g': _jnp.float32, 'ffn2_w_gate': _jnp.float32, 'ffn2_w_up': _jnp.float32, 'ffn2_w_down': _jnp.float32, 'ffn2_post_g': _jnp.float32, 'ple_w_up': _jnp.float32, 'ple_w_gate': _jnp.float32, 'ple_post_g': _jnp.float32}
MOMENT_SCALE = {'ffn1_pre_g': 7.015144e+00, 'ffn1_w_gate': 1.899932e+00, 'ffn1_w_up': 2.378075e+00, 'ffn1_w_down': 3.822783e+00, 'ffn1_post_g': 8.840517e+00, 'mix_pre_g': 1.873744e+01, 'w_in': 1.338010e+01, 'attn_norm_g': 2.434129e+01, 'ssm_lam_re': 3.416182e-01, 'ssm_lam_im': 5.193857e-01, 'ssm_log_dt': 4.087935e+01, 'ssm_b_re': 4.056774e-01, 'ssm_b_im': 3.935669e-01, 'ssm_c_re': 6.783462e-01, 'ssm_c_im': 6.465389e-01, 'ssm_d': 2.528969e+01, 'ssm_w_glu': 3.769786e+00, 'ssm_b_glu': 1.004667e+01, 'ssm_norm_g': 2.232150e+01, 'w_out': 2.389404e+01, 'mix_post_g': 4.218835e+01, 'ffn2_pre_g': 4.693208e+00, 'ffn2_w_gate': 1.344994e+00, 'ffn2_w_up': 2.379743e+00, 'ffn2_w_down': 4.012240e+00, 'ffn2_post_g': 8.906546e+00, 'ple_w_up': 1.024832e+00, 'ple_w_gate': 5.709066e-01, 'ple_post_g': 3.317319e+01}


def _to_microbatches(a, axis):
    t = _jnp.moveaxis(a, axis, 0)
    t = t.reshape((N_MICROBATCH, t.shape[0] // N_MICROBATCH) + t.shape[1:])
    return _jnp.moveaxis(t, 1, axis + 1)


def setup_inputs(seed: int = 0) -> dict:
    inp = _fwd_setup_inputs(seed)
    key = _jax.random.fold_in(_jax.random.key(seed), 7919)
    shape, _ = _output_shape()
    out = dict(inp)
    out["loss_target"] = _jax.random.normal(_jax.random.fold_in(key, 0), shape, _jnp.float32)
    for i, name in enumerate(TWIN_WEIGHTS):
        w = inp[name].astype(_jnp.float32)
        if MOMENT_SCALE is None:
            s = _jnp.sqrt(_jnp.mean(_jnp.square(w)) + 1e-30)
        else:
            s = MOMENT_SCALE[name]
        km, kv = _jax.random.split(_jax.random.fold_in(key, i + 1))
        out[name] = w
        out["m_" + name] = s * _jax.random.normal(km, w.shape, _jnp.float32)
        out["v_" + name] = (s * s) * _jax.random.uniform(kv, w.shape, _jnp.float32, 0.5, 1.5)
    if N_MICROBATCH > 1:
        for name, axis in PER_EXAMPLE_BATCH_AXIS.items():
            out[name] = _to_microbatches(out[name], axis)
    return {'x': out['x'], 'p': out['p'], 'positions': out['positions'], 'ffn1_pre_g': out['ffn1_pre_g'], 'ffn1_w_gate': out['ffn1_w_gate'], 'ffn1_w_up': out['ffn1_w_up'], 'ffn1_w_down': out['ffn1_w_down'], 'ffn1_post_g': out['ffn1_post_g'], 'mix_pre_g': out['mix_pre_g'], 'w_in': out['w_in'], 'attn_norm_g': out['attn_norm_g'], 'ssm_lam_re': out['ssm_lam_re'], 'ssm_lam_im': out['ssm_lam_im'], 'ssm_log_dt': out['ssm_log_dt'], 'ssm_b_re': out['ssm_b_re'], 'ssm_b_im': out['ssm_b_im'], 'ssm_c_re': out['ssm_c_re'], 'ssm_c_im': out['ssm_c_im'], 'ssm_d': out['ssm_d'], 'ssm_w_glu': out['ssm_w_glu'], 'ssm_b_glu': out['ssm_b_glu'], 'ssm_norm_g': out['ssm_norm_g'], 'w_out': out['w_out'], 'mix_post_g': out['mix_post_g'], 'ffn2_pre_g': out['ffn2_pre_g'], 'ffn2_w_gate': out['ffn2_w_gate'], 'ffn2_w_up': out['ffn2_w_up'], 'ffn2_w_down': out['ffn2_w_down'], 'ffn2_post_g': out['ffn2_post_g'], 'ple_w_up': out['ple_w_up'], 'ple_w_gate': out['ple_w_gate'], 'ple_post_g': out['ple_post_g'], 'loss_target': out['loss_target'], 'm_ffn1_pre_g': out['m_ffn1_pre_g'], 'm_ffn1_w_gate': out['m_ffn1_w_gate'], 'm_ffn1_w_up': out['m_ffn1_w_up'], 'm_ffn1_w_down': out['m_ffn1_w_down'], 'm_ffn1_post_g': out['m_ffn1_post_g'], 'm_mix_pre_g': out['m_mix_pre_g'], 'm_w_in': out['m_w_in'], 'm_attn_norm_g': out['m_attn_norm_g'], 'm_ssm_lam_re': out['m_ssm_lam_re'], 'm_ssm_lam_im': out['m_ssm_lam_im'], 'm_ssm_log_dt': out['m_ssm_log_dt'], 'm_ssm_b_re': out['m_ssm_b_re'], 'm_ssm_b_im': out['m_ssm_b_im'], 'm_ssm_c_re': out['m_ssm_c_re'], 'm_ssm_c_im': out['m_ssm_c_im'], 'm_ssm_d': out['m_ssm_d'], 'm_ssm_w_glu': out['m_ssm_w_glu'], 'm_ssm_b_glu': out['m_ssm_b_glu'], 'm_ssm_norm_g': out['m_ssm_norm_g'], 'm_w_out': out['m_w_out'], 'm_mix_post_g': out['m_mix_post_g'], 'm_ffn2_pre_g': out['m_ffn2_pre_g'], 'm_ffn2_w_gate': out['m_ffn2_w_gate'], 'm_ffn2_w_up': out['m_ffn2_w_up'], 'm_ffn2_w_down': out['m_ffn2_w_down'], 'm_ffn2_post_g': out['m_ffn2_post_g'], 'm_ple_w_up': out['m_ple_w_up'], 'm_ple_w_gate': out['m_ple_w_gate'], 'm_ple_post_g': out['m_ple_post_g'], 'v_ffn1_pre_g': out['v_ffn1_pre_g'], 'v_ffn1_w_gate': out['v_ffn1_w_gate'], 'v_ffn1_w_up': out['v_ffn1_w_up'], 'v_ffn1_w_down': out['v_ffn1_w_down'], 'v_ffn1_post_g': out['v_ffn1_post_g'], 'v_mix_pre_g': out['v_mix_pre_g'], 'v_w_in': out['v_w_in'], 'v_attn_norm_g': out['v_attn_norm_g'], 'v_ssm_lam_re': out['v_ssm_lam_re'], 'v_ssm_lam_im': out['v_ssm_lam_im'], 'v_ssm_log_dt': out['v_ssm_log_dt'], 'v_ssm_b_re': out['v_ssm_b_re'], 'v_ssm_b_im': out['v_ssm_b_im'], 'v_ssm_c_re': out['v_ssm_c_re'], 'v_ssm_c_im': out['v_ssm_c_im'], 'v_ssm_d': out['v_ssm_d'], 'v_ssm_w_glu': out['v_ssm_w_glu'], 'v_ssm_b_glu': out['v_ssm_b_glu'], 'v_ssm_norm_g': out['v_ssm_norm_g'], 'v_w_out': out['v_w_out'], 'v_mix_post_g': out['v_mix_post_g'], 'v_ffn2_pre_g': out['v_ffn2_pre_g'], 'v_ffn2_w_gate': out['v_ffn2_w_gate'], 'v_ffn2_w_up': out['v_ffn2_w_up'], 'v_ffn2_w_down': out['v_ffn2_w_down'], 'v_ffn2_post_g': out['v_ffn2_post_g'], 'v_ple_w_up': out['v_ple_w_up'], 'v_ple_w_gate': out['v_ple_w_gate'], 'v_ple_post_g': out['v_ple_post_g']}


def _loss(weights, diff, rest, loss_target):
    with _jax.named_scope("forward"):
        args = {**rest, TWIN_DIFF_INPUT: diff, **{k: w.astype(_WEIGHT_DTYPES[k]) for k, w in weights.items()}}
        y = _forward(args)
    with _jax.named_scope("loss_head"):
        err = _jnp.square(y.astype(_jnp.float32) - loss_target)
        return 0.5 * _jnp.sum(_jnp.mean(err, axis=-1)) if err.ndim else 0.5 * err


def _adamw(w, g, m, v):
    m = ADAM_B1 * m + (1.0 - ADAM_B1) * g
    v = ADAM_B2 * v + (1.0 - ADAM_B2) * _jnp.square(g)
    m_hat = m / (1.0 - ADAM_B1 ** ADAM_STEP)
    v_hat = v / (1.0 - ADAM_B2 ** ADAM_STEP)
    delta = -ADAM_LR * (m_hat / (_jnp.sqrt(v_hat) + ADAM_EPS) + ADAM_WD * w)
    return delta, m, v


def reference(x, p, positions, ffn1_pre_g, ffn1_w_gate, ffn1_w_up, ffn1_w_down, ffn1_post_g, mix_pre_g, w_in, attn_norm_g, ssm_lam_re, ssm_lam_im, ssm_log_dt, ssm_b_re, ssm_b_im, ssm_c_re, ssm_c_im, ssm_d, ssm_w_glu, ssm_b_glu, ssm_norm_g, w_out, mix_post_g, ffn2_pre_g, ffn2_w_gate, ffn2_w_up, ffn2_w_down, ffn2_post_g, ple_w_up, ple_w_gate, ple_post_g, loss_target, m_ffn1_pre_g, m_ffn1_w_gate, m_ffn1_w_up, m_ffn1_w_down, m_ffn1_post_g, m_mix_pre_g, m_w_in, m_attn_norm_g, m_ssm_lam_re, m_ssm_lam_im, m_ssm_log_dt, m_ssm_b_re, m_ssm_b_im, m_ssm_c_re, m_ssm_c_im, m_ssm_d, m_ssm_w_glu, m_ssm_b_glu, m_ssm_norm_g, m_w_out, m_mix_post_g, m_ffn2_pre_g, m_ffn2_w_gate, m_ffn2_w_up, m_ffn2_w_down, m_ffn2_post_g, m_ple_w_up, m_ple_w_gate, m_ple_post_g, v_ffn1_pre_g, v_ffn1_w_gate, v_ffn1_w_up, v_ffn1_w_down, v_ffn1_post_g, v_mix_pre_g, v_w_in, v_attn_norm_g, v_ssm_lam_re, v_ssm_lam_im, v_ssm_log_dt, v_ssm_b_re, v_ssm_b_im, v_ssm_c_re, v_ssm_c_im, v_ssm_d, v_ssm_w_glu, v_ssm_b_glu, v_ssm_norm_g, v_w_out, v_mix_post_g, v_ffn2_pre_g, v_ffn2_w_gate, v_ffn2_w_up, v_ffn2_w_down, v_ffn2_post_g, v_ple_w_up, v_ple_w_gate, v_ple_post_g):
    given = dict(x=x, p=p, positions=positions, ffn1_pre_g=ffn1_pre_g, ffn1_w_gate=ffn1_w_gate, ffn1_w_up=ffn1_w_up, ffn1_w_down=ffn1_w_down, ffn1_post_g=ffn1_post_g, mix_pre_g=mix_pre_g, w_in=w_in, attn_norm_g=attn_norm_g, ssm_lam_re=ssm_lam_re, ssm_lam_im=ssm_lam_im, ssm_log_dt=ssm_log_dt, ssm_b_re=ssm_b_re, ssm_b_im=ssm_b_im, ssm_c_re=ssm_c_re, ssm_c_im=ssm_c_im, ssm_d=ssm_d, ssm_w_glu=ssm_w_glu, ssm_b_glu=ssm_b_glu, ssm_norm_g=ssm_norm_g, w_out=w_out, mix_post_g=mix_post_g, ffn2_pre_g=ffn2_pre_g, ffn2_w_gate=ffn2_w_gate, ffn2_w_up=ffn2_w_up, ffn2_w_down=ffn2_w_down, ffn2_post_g=ffn2_post_g, ple_w_up=ple_w_up, ple_w_gate=ple_w_gate, ple_post_g=ple_post_g, loss_target=loss_target, m_ffn1_pre_g=m_ffn1_pre_g, m_ffn1_w_gate=m_ffn1_w_gate, m_ffn1_w_up=m_ffn1_w_up, m_ffn1_w_down=m_ffn1_w_down, m_ffn1_post_g=m_ffn1_post_g, m_mix_pre_g=m_mix_pre_g, m_w_in=m_w_in, m_attn_norm_g=m_attn_norm_g, m_ssm_lam_re=m_ssm_lam_re, m_ssm_lam_im=m_ssm_lam_im, m_ssm_log_dt=m_ssm_log_dt, m_ssm_b_re=m_ssm_b_re, m_ssm_b_im=m_ssm_b_im, m_ssm_c_re=m_ssm_c_re, m_ssm_c_im=m_ssm_c_im, m_ssm_d=m_ssm_d, m_ssm_w_glu=m_ssm_w_glu, m_ssm_b_glu=m_ssm_b_glu, m_ssm_norm_g=m_ssm_norm_g, m_w_out=m_w_out, m_mix_post_g=m_mix_post_g, m_ffn2_pre_g=m_ffn2_pre_g, m_ffn2_w_gate=m_ffn2_w_gate, m_ffn2_w_up=m_ffn2_w_up, m_ffn2_w_down=m_ffn2_w_down, m_ffn2_post_g=m_ffn2_post_g, m_ple_w_up=m_ple_w_up, m_ple_w_gate=m_ple_w_gate, m_ple_post_g=m_ple_post_g, v_ffn1_pre_g=v_ffn1_pre_g, v_ffn1_w_gate=v_ffn1_w_gate, v_ffn1_w_up=v_ffn1_w_up, v_ffn1_w_down=v_ffn1_w_down, v_ffn1_post_g=v_ffn1_post_g, v_mix_pre_g=v_mix_pre_g, v_w_in=v_w_in, v_attn_norm_g=v_attn_norm_g, v_ssm_lam_re=v_ssm_lam_re, v_ssm_lam_im=v_ssm_lam_im, v_ssm_log_dt=v_ssm_log_dt, v_ssm_b_re=v_ssm_b_re, v_ssm_b_im=v_ssm_b_im, v_ssm_c_re=v_ssm_c_re, v_ssm_c_im=v_ssm_c_im, v_ssm_d=v_ssm_d, v_ssm_w_glu=v_ssm_w_glu, v_ssm_b_glu=v_ssm_b_glu, v_ssm_norm_g=v_ssm_norm_g, v_w_out=v_w_out, v_mix_post_g=v_mix_post_g, v_ffn2_pre_g=v_ffn2_pre_g, v_ffn2_w_gate=v_ffn2_w_gate, v_ffn2_w_up=v_ffn2_w_up, v_ffn2_w_down=v_ffn2_w_down, v_ffn2_post_g=v_ffn2_post_g, v_ple_w_up=v_ple_w_up, v_ple_w_gate=v_ple_w_gate, v_ple_post_g=v_ple_post_g)
    weights = {n: given[n] for n in TWIN_WEIGHTS}
    shared = {n: given[n] for n in SHARED_INPUTS}
    per_example = {n: given[n] for n in ['x', 'p', 'positions']}
    grad_fn = _jax.value_and_grad(_loss, argnums=(0, 1))

    def one_microbatch(ex, loss_target):
        ex = dict(ex)
        diff = ex.pop(TWIN_DIFF_INPUT)
        return grad_fn(weights, diff, {**shared, **ex}, loss_target)

    if N_MICROBATCH == 1:
        loss, (grad_w, grad_x) = one_microbatch(per_example, given["loss_target"])
    else:
        def body(carry, xs):
            loss_sum, grad_sum = carry
            l_k, (gw_k, gx_k) = one_microbatch(xs[0], xs[1])
            with _jax.named_scope("update"):
                return (loss_sum + l_k, _jax.tree.map(_jnp.add, grad_sum, gw_k)), gx_k

        init = (_jnp.zeros((), _jnp.float32), _jax.tree.map(_jnp.zeros_like, weights))
        (loss, grad_w), grad_x = _jax.lax.scan(body, init, (per_example, given["loss_target"]))
    with _jax.named_scope("update"):
        delta_w, new_m, new_v = {}, {}, {}
        for n in TWIN_WEIGHTS:
            delta_w[n], new_m[n], new_v[n] = _adamw(weights[n], grad_w[n], given["m_" + n], given["v_" + n])
    return (loss, grad_x, *[grad_w[n] for n in TWIN_WEIGHTS], *[delta_w[n] for n in TWIN_WEIGHTS],
            *[new_m[n] for n in TWIN_WEIGHTS], *[new_v[n] for n in TWIN_WEIGHTS])
```

```python
import functools
import math

import jax
import jax.numpy as jnp
from jax import lax
from jax.experimental import pallas as pl
from jax.experimental.pallas import tpu as pltpu

F32 = jnp.float32
BF16 = jnp.bfloat16

N_DEV = 8
D_MODEL = 1024
D_FF = 2816
FF_SHARD = D_FF // N_DEV
D_ATTN = 512
D_SSM = 512
HEAD_DIM = 64
N_HEADS = 8
ROPE_DIM = 16
ROPE_THETA = 500000.0
DILATIONS = (1, 4, 16)
BAND = 128
N_GROUPS = 32
SSM_GROUP = 16
SSM_STATE = 64
N_STATE = N_GROUPS * SSM_STATE
PLE_DIM = 256
NORM_EPS = 1e-6
ADAM_LR, ADAM_B1, ADAM_B2, ADAM_EPS, ADAM_WD, ADAM_STEP = 0.001, 0.9, 0.999, 1e-08, 0.01, 10

VMEM_LIMIT_BYTES = 52 * 1024 * 1024
TM = 512
MESH_AXES = ("x", "y", "c")

WEIGHTS = ['ffn1_pre_g', 'ffn1_w_gate', 'ffn1_w_up', 'ffn1_w_down', 'ffn1_post_g', 'mix_pre_g', 'w_in', 'attn_norm_g',
           'ssm_lam_re', 'ssm_lam_im', 'ssm_log_dt', 'ssm_b_re', 'ssm_b_im', 'ssm_c_re', 'ssm_c_im', 'ssm_d',
           'ssm_w_glu', 'ssm_b_glu', 'ssm_norm_g', 'w_out', 'mix_post_g', 'ffn2_pre_g', 'ffn2_w_gate', 'ffn2_w_up',
           'ffn2_w_down', 'ffn2_post_g', 'ple_w_up', 'ple_w_gate', 'ple_post_g']
SHARDED = ['ffn1_w_gate', 'ffn1_w_up', 'ffn1_w_down', 'w_in', 'ssm_w_glu', 'w_out', 'ffn2_w_gate', 'ffn2_w_up',
           'ffn2_w_down', 'ple_w_up', 'ple_w_gate']
SMALL = [n for n in WEIGHTS if n not in SHARDED]
SMALL_PAD = 1024


def _params(n_axes):
    return pltpu.CompilerParams(dimension_semantics=("arbitrary",) * n_axes, vmem_limit_bytes=VMEM_LIMIT_BYTES)


_DIMS = {"nn": (((1,), (0,)), ((), ())), "nt": (((1,), (1,)), ((), ())), "tn": (((0,), (0,)), ((), ()))}


def _dot(a, b, mode):
    return lax.dot_general(a.astype(BF16), b.astype(BF16), _DIMS[mode], preferred_element_type=F32)


def _store(out_refs, vals, kinds, first):
    for ref, val, kind in zip(out_refs, vals, kinds):
        if kind == "row":
            ref[...] = val.astype(ref.dtype)
        else:
            @pl.when(first)
            def _(ref=ref, val=val):
                ref[...] = val.astype(ref.dtype)

            @pl.when(jnp.logical_not(first))
            def _(ref=ref, val=val):
                ref[...] += val.astype(ref.dtype)


def _mm(name, grid, a, b, mode, outs, extras=(), epilogue=None, acc_shape=None):
    nk = grid[2]
    ne, no = len(extras), len(outs)
    kinds = [o[4] for o in outs]
    assert all(k == "row" for k in kinds) or grid[1] == 1

    def body(*refs):
        a_ref, b_ref = refs[0], refs[1]
        ex_refs = refs[2:2 + ne]
        out_refs = refs[2 + ne:2 + ne + no]
        part = _dot(a_ref[...], b_ref[...], mode)
        first = pl.program_id(0) == 0

        def finish(acc):
            vals = epilogue(acc, *[r[...] for r in ex_refs]) if epilogue is not None else (acc,)
            _store(out_refs, vals, kinds, first)

        if nk == 1:
            finish(part)
        else:
            acc_ref = refs[-1]
            k = pl.program_id(2)

            @pl.when(k == 0)
            def _():
                acc_ref[...] = part

            @pl.when(k > 0)
            def _():
                acc_ref[...] += part

            @pl.when(k == nk - 1)
            def _():
                finish(acc_ref[...])

    ins = (a, b) + tuple(extras)
    res = pl.pallas_call(
        body,
        name=name,
        grid=grid,
        in_specs=[pl.BlockSpec(blk, imap) for (_, blk, imap) in ins],
        out_specs=[pl.BlockSpec(o[2], o[3]) for o in outs],
        out_shape=[jax.ShapeDtypeStruct(o[0], o[1]) for o in outs],
        scratch_shapes=[pltpu.VMEM(acc_shape, F32)] if nk > 1 else [],
        compiler_params=_params(3),
    )(*[x[0] for x in ins])
    return res


def _rowwise(name, n_rows, tm, ins, outs, fn):
    kinds = [o[4] for o in outs]
    ni = len(ins)

    def body(*refs):
        vals = fn(*[r[...] for r in refs[:ni]])
        _store(refs[ni:], vals, kinds, pl.program_id(0) == 0)

    return pl.pallas_call(
        body,
        name=name,
        grid=(n_rows // tm,),
        in_specs=[pl.BlockSpec(blk, imap) for (_, blk, imap) in ins],
        out_specs=[pl.BlockSpec(o[2], o[3]) for o in outs],
        out_shape=[jax.ShapeDtypeStruct(o[0], o[1]) for o in outs],
        compiler_params=_params(1),
    )(*[x[0] for x in ins])


def _rows(arr, tm, col=0, width=None):
    width = arr.shape[1] if width is None else width
    return (arr, (tm, width), lambda i, *_: (i, col))


def _whole(arr):
    nd = arr.ndim
    return (arr, arr.shape, lambda *_: (0,) * nd)


def _row_out(n_rows, width, dtype, tm, col=0, total=None):
    return ((n_rows, width if total is None else total), dtype, (tm, width), lambda i, *_: (i, col), "row")


def _col_out(width):
    return ((1, width), F32, (1, width), lambda *_: (0, 0), "colsum")


def _rms(x, g):
    return x * lax.rsqrt(jnp.mean(x * x, axis=-1, keepdims=True) + NORM_EPS) * g


def _sigmoid(x):
    return 1.0 / (1.0 + jnp.exp(-x))


def _gelu(x):
    return 0.5 * x * (1.0 + jnp.tanh(0.7978845608028654 * (x + 0.044715 * x * x * x)))


def _ffn_fwd(tag, s_len, h, n, g_a, slot, g_b1, g_post, g_next):
    nt = s_len // TM
    hid = (N_DEV, s_len, FF_SHARD)
    hid_blk = (None, TM, FF_SHARD)
    a_spec = (n, (TM, D_MODEL), lambda i, j, k: (i, 0))

    def w_spec(which):
        return (g_a, (None, None, D_MODEL, FF_SHARD), lambda i, j, k: (j, which, 0, 0))

    (up,) = _mm(f"ffn_up", (nt, N_DEV, 1), a_spec, w_spec(2 * slot + 1), "nn",
                [(hid, BF16, hid_blk, lambda i, j, k: (j, i, 0), "row")])

    def gate_epi(acc, up_blk):
        act = acc * _sigmoid(acc) * up_blk.astype(F32)
        return acc, act

    gate, act = _mm(f"ffn_gate", (nt, N_DEV, 1), a_spec, w_spec(2 * slot), "nn",
                    [(hid, BF16, hid_blk, lambda i, j, k: (j, i, 0), "row"),
                     (hid, BF16, hid_blk, lambda i, j, k: (j, i, 0), "row")],
                    extras=[(up, hid_blk, lambda i, j, k: (j, i, 0))], epilogue=gate_epi)

    def down_epi(acc, h_blk, gp, gn):
        h_new = h_blk + 0.5 * _rms(acc, gp)
        return acc, h_new, _rms(h_new, gn)

    f, h_new, n_next = _mm(
        f"ffn_down", (nt, 1, N_DEV),
        (act, hid_blk, lambda i, j, k: (k, i, 0)),
        (g_b1, (None, None, FF_SHARD, D_MODEL), lambda i, j, k: (k, slot, 0, 0)), "nn",
        [_row_out(s_len, D_MODEL, F32, TM), _row_out(s_len, D_MODEL, F32, TM), _row_out(s_len, D_MODEL, BF16, TM)],
        extras=[_rows(h, TM), _whole(g_post), _whole(g_next)], epilogue=down_epi, acc_shape=(TM, D_MODEL))
    return h_new, n_next, dict(n=n, gate=gate, up=up, act=act, f=f, h=h)


def _ffn_bwd(s_len, saved, df, g_a, slot, g_b1, final_epi, final_extras, final_outs):
    nt = s_len // TM
    hid = (N_DEV, s_len, FF_SHARD)
    hid_blk = (None, TM, FF_SHARD)
    hid_map = lambda i, j, k: (j, i, 0)

    def act_epi(acc, gate_blk, up_blk):
        gate = gate_blk.astype(F32)
        up = up_blk.astype(F32)
        sg = _sigmoid(gate)
        dgate = acc * up * sg * (1.0 + gate * (1.0 - sg))
        dup = acc * gate * sg
        return dgate, dup

    dgate, dup = _mm(
        "ffn_bwd_act", (nt, N_DEV, 1), (df, (TM, D_MODEL), lambda i, j, k: (i, 0)),
        (g_b1, (None, None, FF_SHARD, D_MODEL), lambda i, j, k: (j, slot, 0, 0)), "nt",
        [(hid, BF16, hid_blk, hid_map, "row"), (hid, BF16, hid_blk, hid_map, "row")],
        extras=[(saved["gate"], hid_blk, hid_map), (saved["up"], hid_blk, hid_map)], epilogue=act_epi)

    (d_w_down,) = _mm(
        "ffn_bwd_wdown", (N_DEV, 1, nt), (saved["act"], hid_blk, lambda i, j, k: (i, k, 0)),
        (df, (TM, D_MODEL), lambda i, j, k: (k, 0)), "tn",
        [((N_DEV, FF_SHARD, D_MODEL), BF16, (None, FF_SHARD, D_MODEL), lambda i, j, k: (i, 0, 0), "row")],
        acc_shape=(FF_SHARD, D_MODEL))

    def w_grad(name, dhid):
        (dw,) = _mm(
            name, (1, N_DEV, nt), (saved["n"], (TM, D_MODEL), lambda i, j, k: (k, 0)),
            (dhid, hid_blk, lambda i, j, k: (j, k, 0)), "tn",
            [((N_DEV, D_MODEL, FF_SHARD), BF16, (None, D_MODEL, FF_SHARD), lambda i, j, k: (j, 0, 0), "row")],
            acc_shape=(D_MODEL, FF_SHARD))
        return dw

    d_w_gate = w_grad("ffn_bwd_wgate", dgate)
    d_w_up = w_grad("ffn_bwd_wup", dup)

    def w_spec(which):
        return (g_a, (None, None, D_MODEL, FF_SHARD), lambda i, j, k: (k, which, 0, 0))

    (dn_gate,) = _mm(
        "ffn_bwd_dn_gate", (nt, 1, N_DEV), (dgate, hid_blk, lambda i, j, k: (k, i, 0)), w_spec(2 * slot), "nt",
        [_row_out(s_len, D_MODEL, F32, TM)], acc_shape=(TM, D_MODEL))

    def epi(acc, part, *rest):
        return final_epi(acc + part, *rest)

    outs = _mm(
        "ffn_bwd_dn_up", (nt, 1, N_DEV), (dup, hid_blk, lambda i, j, k: (k, i, 0)), w_spec(2 * slot + 1), "nt",
        final_outs, extras=[_rows(dn_gate, TM)] + list(final_extras), epilogue=epi, acc_shape=(TM, D_MODEL))
    return d_w_gate, d_w_up, d_w_down, outs


def _band_mask(first_block):
    qi = lax.broadcasted_iota(jnp.int32, (BAND, 2 * BAND), 0)
    kj = lax.broadcasted_iota(jnp.int32, (BAND, 2 * BAND), 1)
    ok = (kj >= qi) & (kj <= qi + BAND)
    return ok & (jnp.logical_not(first_block) | (kj >= BAND))


def _attn_views(s_len, d):
    n_str = s_len // d
    nb = n_str // BAND
    blk = (BAND, D_ATTN)
    cur = lambda r, b: (b, r)
    prev = lambda r, b: (jnp.maximum(b - 1, 0), r)
    return n_str, nb, blk, cur, prev


def _attn_fwd(s_len, q, k, v, d):
    n_str, nb, blk, cur, prev = _attn_views(s_len, d)
    view = lambda t: t.reshape(n_str, d * D_ATTN)

    def body(q_ref, kp_ref, kc_ref, vp_ref, vc_ref, o_ref, lse_ref):
        mask = _band_mask(pl.program_id(1) == 0)
        qq = q_ref[...]
        kk = jnp.concatenate([kp_ref[...], kc_ref[...]], axis=0)
        vv = jnp.concatenate([vp_ref[...], vc_ref[...]], axis=0)
        for h in range(N_HEADS):
            sl = slice(h * HEAD_DIM, (h + 1) * HEAD_DIM)
            s = _dot(qq[:, sl], kk[:, sl], "nt") * (HEAD_DIM ** -0.5)
            s = jnp.where(mask, s, -1e30)
            m = jnp.max(s, axis=-1, keepdims=True)
            e = jnp.exp(s - m)
            den = jnp.sum(e, axis=-1, keepdims=True)
            o_ref[:, sl] = _dot(e / den, vv[:, sl], "nn")
            lse_ref[:, sl] = jnp.broadcast_to(m + jnp.log(den), (BAND, HEAD_DIM))

    o, lse = pl.pallas_call(
        body,
        name=f"attn_fwd_d{d}",
        grid=(d, nb),
        in_specs=[pl.BlockSpec(blk, cur), pl.BlockSpec(blk, prev), pl.BlockSpec(blk, cur),
                  pl.BlockSpec(blk, prev), pl.BlockSpec(blk, cur)],
        out_specs=[pl.BlockSpec(blk, cur), pl.BlockSpec(blk, cur)],
        out_shape=[jax.ShapeDtypeStruct((n_str, d * D_ATTN), F32)] * 2,
        compiler_params=_params(2),
    )(view(q), view(k), view(k), view(v), view(v))
    return o.reshape(s_len, D_ATTN), lse.reshape(s_len, D_ATTN)


def _attn_bwd(s_len, q, k, v, dattn, attn, lse, d):
    n_str, nb, blk, cur, prev = _attn_views(s_len, d)
    view = lambda t: t.reshape(n_str, d * D_ATTN)

    def body(q_ref, kp_ref, kc_ref, vp_ref, vc_ref, da_ref, at_ref, lse_ref, dq_ref, dka_ref, dkb_ref, dva_ref, dvb_ref):
        mask = _band_mask(pl.program_id(1) == 0)
        qq = q_ref[...]
        kk = jnp.concatenate([kp_ref[...], kc_ref[...]], axis=0)
        vv = jnp.concatenate([vp_ref[...], vc_ref[...]], axis=0)
        da = da_ref[...]
        prod = da * at_ref[...]
        scale = HEAD_DIM ** -0.5
        for h in range(N_HEADS):
            sl = slice(h * HEAD_DIM, (h + 1) * HEAD_DIM)
            s = _dot(qq[:, sl], kk[:, sl], "nt") * scale
            p = jnp.where(mask, jnp.exp(s - lse_ref[:, h * HEAD_DIM:h * HEAD_DIM + 1]), 0.0)
            dp = _dot(da[:, sl], vv[:, sl], "nt")
            ds = p * (dp - jnp.sum(prod[:, sl], axis=-1, keepdims=True))
            dq_ref[:, sl] = _dot(ds, kk[:, sl], "nn") * scale
            dk = _dot(ds, qq[:, sl], "tn") * scale
            dv = _dot(p, da[:, sl], "tn")
            dkb_ref[:, sl] = dk[:BAND]
            dka_ref[:, sl] = dk[BAND:]
            dvb_ref[:, sl] = dv[:BAND]
            dva_ref[:, sl] = dv[BAND:]

    outs = pl.pallas_call(
        body,
        name=f"attn_bwd_d{d}",
        grid=(d, nb),
        in_specs=[pl.BlockSpec(blk, cur), pl.BlockSpec(blk, prev), pl.BlockSpec(blk, cur),
                  pl.BlockSpec(blk, prev), pl.BlockSpec(blk, cur),
                  pl.BlockSpec(blk, cur), pl.BlockSpec(blk, cur), pl.BlockSpec(blk, cur)],
        out_specs=[pl.BlockSpec(blk, cur)] * 5,
        out_shape=[jax.ShapeDtypeStruct((n_str, d * D_ATTN), F32)] * 5,
        compiler_params=_params(2),
    )(view(q), view(k), view(k), view(v), view(v), view(dattn), view(attn), view(lse))
    return [t.reshape(s_len, D_ATTN) for t in outs]


def _rope_tables(positions):
    half = ROPE_DIM // 2
    inv_freq = ROPE_THETA ** (-jnp.arange(half, dtype=F32) * (2.0 / ROPE_DIM))
    ang = positions.astype(F32)[:, None] * inv_freq
    cos, sin = jnp.cos(ang), jnp.sin(ang)
    s_len = positions.shape[0]
    one = jnp.ones((s_len, HEAD_DIM - ROPE_DIM), F32)
    zero8 = jnp.zeros((s_len, half), F32)
    zero = jnp.zeros((s_len, HEAD_DIM - ROPE_DIM), F32)
    c = jnp.concatenate([cos, cos, one], axis=1)
    s1 = jnp.concatenate([zero8, sin, zero], axis=1)
    s2 = jnp.concatenate([-sin, zero8, zero], axis=1)
    tile = lambda t: jnp.tile(t, (1, N_HEADS))
    return tile(c), tile(s1), tile(s2)


def _rope(t, c, s1, s2):
    half = ROPE_DIM // 2
    return t * c + pltpu.roll(t, half, 1) * s1 + pltpu.roll(t, D_ATTN - half, 1) * s2


def _rope_transposed(dt, c, s1, s2):
    half = ROPE_DIM // 2
    return dt * c + pltpu.roll(dt * s1, D_ATTN - half, 1) + pltpu.roll(dt * s2, half, 1)


SCAN_ROWS = 256
SCAN_LANES = 512


def _scan(name, bu, a_cat, reverse):
    s_len, width = bu.shape
    nb = s_len // SCAN_ROWS
    half = width // 2

    def body(bu_ref, a_ref, x_ref, carry_ref):
        @pl.when(pl.program_id(0) == 0)
        def _():
            carry_ref[...] = jnp.zeros_like(carry_ref)

        for c in range(half // SCAN_LANES):
            re = pl.ds(c * SCAN_LANES, SCAN_LANES)
            im = pl.ds(half + c * SCAN_LANES, SCAN_LANES)
            ar = a_ref[:, re]
            ai = a_ref[:, im]

            def step(s, state):
                xr, xi = state
                t = (SCAN_ROWS - 1 - s) if reverse else s
                row = pl.ds(t, 1)
                nr = ar * xr - ai * xi + bu_ref[row, re]
                ni = ar * xi + ai * xr + bu_ref[row, im]
                x_ref[row, re] = nr
                x_ref[row, im] = ni
                return nr, ni

            xr, xi = lax.fori_loop(0, SCAN_ROWS, step, (carry_ref[0:1, re], carry_ref[0:1, im]), unroll=8)
            carry_ref[0:1, re] = xr
            carry_ref[0:1, im] = xi

    imap = (lambda i: (nb - 1 - i, 0)) if reverse else (lambda i: (i, 0))
    return pl.pallas_call(
        body,
        name=name,
        grid=(nb,),
        in_specs=[pl.BlockSpec((SCAN_ROWS, width), imap), pl.BlockSpec((1, width), lambda i: (0, 0))],
        out_specs=pl.BlockSpec((SCAN_ROWS, width), imap),
        out_shape=jax.ShapeDtypeStruct((s_len, width), F32),
        scratch_shapes=[pltpu.VMEM((8, width), F32)],
        compiler_params=_params(1),
    )(bu, a_cat)


def _ssm_params(lam_re, lam_im, log_dt, b_re, b_im, c_re, c_im):
    dt = jnp.exp(log_dt)[:, None]
    er = jnp.exp(lam_re * dt)
    a_re = er * jnp.cos(lam_im * dt)
    a_im = er * jnp.sin(lam_im * dt)
    nr, ni = a_re - 1.0, a_im
    den = lam_re * lam_re + lam_im * lam_im
    fr = (nr * lam_re + ni * lam_im) / den
    fi = (ni * lam_re - nr * lam_im) / den
    bb_re = fr[..., None] * b_re - fi[..., None] * b_im
    bb_im = fr[..., None] * b_im + fi[..., None] * b_re
    eye = jnp.eye(N_GROUPS, dtype=F32)

    def in_mat(bb):
        t = bb.transpose(0, 2, 1)[:, :, None, :] * eye[:, None, :, None]
        return t.reshape(D_SSM, N_STATE)

    def out_mat(cc):
        t = cc.transpose(0, 2, 1)[:, :, None, :] * eye[:, None, :, None]
        return t.reshape(N_STATE, D_SSM)

    w_b = jnp.concatenate([in_mat(bb_re), in_mat(bb_im)], axis=1)
    w_c = jnp.concatenate([out_mat(c_re), -out_mat(c_im)], axis=0)
    a_cat = jnp.concatenate([a_re.reshape(1, N_STATE), a_im.reshape(1, N_STATE)], axis=1)
    return a_cat, w_b, w_c


def _conj(a_cat):
    return jnp.concatenate([a_cat[:, :N_STATE], -a_cat[:, N_STATE:]], axis=1)


def _layer_fwd(s_len, h, n1, p_l, rope, gw, sp, gains, g_next):
    nt = s_len // TM
    sv = {}
    h1, a_in, sv["ffn1"] = _ffn_fwd("1", s_len, h, n1, gw["A"], 0, gw["B1"], gains["ffn1_post_g"], gains["mix_pre_g"])

    (proj,) = _mm("w_in", (nt, N_DEV, 1), (a_in, (TM, D_MODEL), lambda i, j, k: (i, 0)),
                  (gw["C"], (None, D_MODEL, 2 * D_MODEL // N_DEV), lambda i, j, k: (j, 0, 0)), "nn",
                  [((s_len, 2 * D_MODEL), F32, (TM, 2 * D_MODEL // N_DEV), lambda i, j, k: (i, j), "row")])

    c, s1, s2 = rope
    q, k, v = _rowwise(
        "rope", s_len, TM,
        [_rows(proj, TM, 0, D_ATTN), _rows(proj, TM, 1, D_ATTN), _rows(proj, TM, 2, D_ATTN),
         _rows(c, TM), _rows(s1, TM), _rows(s2, TM)],
        [_row_out(s_len, D_ATTN, BF16, TM)] * 3,
        lambda tq, tk, tv, cc, a1, a2: (_rope(tq, cc, a1, a2), _rope(tk, cc, a1, a2), tv))

    parts = []
    for d in DILATIONS:
        parts += list(_attn_fwd(s_len, q, k, v, d))

    def mix_fn(o1, l1, o2, l2, o3, l3, g):
        m = jnp.maximum(jnp.maximum(l1, l2), l3)
        e1, e2, e3 = jnp.exp(l1 - m), jnp.exp(l2 - m), jnp.exp(l3 - m)
        tot = e1 + e2 + e3
        attn = (e1 * o1 + e2 * o2 + e3 * o3) / tot
        return attn, m + jnp.log(tot), _rms(attn, g)

    attn, lse, attn_n = _rowwise(
        "attn_mix", s_len, TM, [_rows(t, TM) for t in parts] + [_whole(gains["attn_norm_g"])],
        [_row_out(s_len, D_ATTN, F32, TM), _row_out(s_len, D_ATTN, F32, TM), _row_out(s_len, D_ATTN, BF16, TM)], mix_fn)

    a_cat, w_b, w_c, d_vec = sp
    ts = SCAN_ROWS
    (bu,) = _mm("ssm_bu", (s_len // ts, 1, 1), (proj, (ts, D_SSM), lambda i, j, k: (i, 3)), _whole(w_b), "nn",
                [_row_out(s_len, 2 * N_STATE, F32, ts)])
    xs = _scan("ssm_scan", bu, a_cat, False)

    def y_epi(acc, u, dv):
        z = acc + dv * u
        return z, _gelu(z)

    z, yg = _mm("ssm_y", (s_len // ts, 1, 1), _rows(xs, ts), _whole(w_c), "nn",
                [_row_out(s_len, D_SSM, F32, ts)] * 2,
                extras=[(proj, (ts, D_SSM), lambda i, j, k: (i, 3)), _whole(d_vec)], epilogue=y_epi)

    def glu_epi(acc, y, b, g):
        t = acc + b
        ssm = y * _sigmoid(t)
        return t, ssm, _rms(ssm, g)

    t_glu, ssm, ssm_n = _mm(
        "ssm_glu", (nt, 1, 1), _rows(yg, TM), _whole(gw["w_glu"]), "nn",
        [_row_out(s_len, D_SSM, F32, TM), _row_out(s_len, D_SSM, F32, TM), _row_out(s_len, D_SSM, BF16, TM)],
        extras=[_rows(yg, TM), _whole(gains["ssm_b_glu"]), _whole(gains["ssm_norm_g"])], epilogue=glu_epi)

    mixed = jnp.concatenate([attn_n, ssm_n], axis=1)

    def out_epi(acc, h_blk, gp, gn):
        h_new = h_blk + _rms(acc, gp)
        return acc, h_new, _rms(h_new, gn)

    o, h2, n2 = _mm(
        "w_out", (nt, 1, 1), _rows(mixed, TM), _whole(gw["w_out"]), "nn",
        [_row_out(s_len, D_MODEL, F32, TM), _row_out(s_len, D_MODEL, F32, TM), _row_out(s_len, D_MODEL, BF16, TM)],
        extras=[_rows(h1, TM), _whole(gains["mix_post_g"]), _whole(gains["ffn2_pre_g"])], epilogue=out_epi)

    h3, _, sv["ffn2"] = _ffn_fwd("2", s_len, h2, n2, gw["A"], 1, gw["B1"], gains["ffn2_post_g"], gains["ffn2_post_g"])

    (pu,) = _mm("ple_up", (nt, N_DEV, 1), (p_l, (TM, PLE_DIM), lambda i, j, k: (i, 0)),
                (gw["D"], (None, PLE_DIM, D_MODEL // N_DEV), lambda i, j, k: (j, 0, 0)), "nn",
                [((s_len, D_MODEL), F32, (TM, D_MODEL // N_DEV), lambda i, j, k: (i, j), "row")])

    def ple_epi(acc, pu_blk, h_blk, gp, gn):
        h_new = h_blk + _rms(pu_blk * _sigmoid(acc), gp)
        return acc, h_new, _rms(h_new, gn)

    gt, h4, n_next = _mm(
        "ple_gate", (nt, 1, 1), _rows(h3, TM), _whole(gw["ple_w_gate"]), "nn",
        [_row_out(s_len, D_MODEL, F32, TM), _row_out(s_len, D_MODEL, F32, TM), _row_out(s_len, D_MODEL, BF16, TM)],
        extras=[_rows(pu, TM), _rows(h3, TM), _whole(gains["ple_post_g"]), _whole(g_next)], epilogue=ple_epi)

    sv.update(h1=h1, a_in=a_in, proj=proj, q=q, k=k, v=v, attn=attn, lse=lse, xs=xs, z=z, yg=yg, t_glu=t_glu, ssm=ssm,
              mixed=mixed, o=o, h2=h2, h3=h3, pu=pu, gt=gt, p_l=p_l)
    return h4, n_next, sv


def _vjp(fn, args, cot):
    _, pull = jax.vjp(fn, *args)
    return pull(cot)


def _layer_bwd(s_len, dh4, sv, rope, gw, sp, gains):
    nt = s_len // TM
    gr = {}
    wg = {}

    def ple_fn(dh, pu, gt, g):
        dpu, dgt, dg = _vjp(lambda a, b, c: _rms(a * _sigmoid(b), c), (pu, gt, g), dh)
        return dpu, dgt, dg

    dpu, dgt, gr["ple_post_g"] = _rowwise(
        "ple_bwd", s_len, TM, [_rows(dh4, TM), _rows(sv["pu"], TM), _rows(sv["gt"], TM), _whole(gains["ple_post_g"])],
        [_row_out(s_len, D_MODEL, BF16, TM), _row_out(s_len, D_MODEL, BF16, TM), _col_out(D_MODEL)], ple_fn)

    dcol = D_MODEL // N_DEV
    (wg["ple_w_up"],) = _mm(
        "ple_bwd_wup", (1, N_DEV, nt), (sv["p_l"], (TM, PLE_DIM), lambda i, j, k: (k, 0)),
        (dpu, (TM, dcol), lambda i, j, k: (k, j)), "tn",
        [((N_DEV, PLE_DIM, dcol), BF16, (None, PLE_DIM, dcol), lambda i, j, k: (j, 0, 0), "row")],
        acc_shape=(PLE_DIM, dcol))
    (wg["ple_w_gate"],) = _mm(
        "ple_bwd_wgate", (N_DEV, 1, nt), (sv["h3"], (TM, dcol), lambda i, j, k: (k, i)),
        (dgt, (TM, D_MODEL), lambda i, j, k: (k, 0)), "tn",
        [((D_MODEL, D_MODEL), BF16, (dcol, D_MODEL), lambda i, j, k: (i, 0), "row")], acc_shape=(dcol, D_MODEL))

    def ple_dx_epi(acc, dh, f, g):
        dh3 = dh + acc
        df, dg = _vjp(_rms, (f, g), 0.5 * dh3)
        return dh3, df, dg

    dh3, df2, gr["ffn2_post_g"] = _mm(
        "ple_bwd_dx", (nt, 1, 1), _rows(dgt, TM), _whole(gw["ple_w_gate"]), "nt",
        [_row_out(s_len, D_MODEL, F32, TM), _row_out(s_len, D_MODEL, BF16, TM), _col_out(D_MODEL)],
        extras=[_rows(dh4, TM), _rows(sv["ffn2"]["f"], TM), _whole(gains["ffn2_post_g"])], epilogue=ple_dx_epi)

    def ffn2_final(dn, dh, h, g_pre, o, g_post):
        dx, dg_pre = _vjp(_rms, (h, g_pre), dn)
        dh2 = dh + dx
        do, dg_post = _vjp(_rms, (o, g_post), dh2)
        return dh2, do, dg_pre, dg_post

    wg["ffn2_w_gate"], wg["ffn2_w_up"], wg["ffn2_w_down"], (dh2, do, gr["ffn2_pre_g"], gr["mix_post_g"]) = _ffn_bwd(
        s_len, sv["ffn2"], df2, gw["A"], 1, gw["B1"], ffn2_final,
        [_rows(dh3, TM), _rows(sv["h2"], TM), _whole(gains["ffn2_pre_g"]), _rows(sv["o"], TM),
         _whole(gains["mix_post_g"])],
        [_row_out(s_len, D_MODEL, F32, TM), _row_out(s_len, D_MODEL, BF16, TM), _col_out(D_MODEL), _col_out(D_MODEL)])

    (wg["w_out"],) = _mm(
        "w_out_bwd_w", (N_DEV, 1, nt), (sv["mixed"], (TM, dcol), lambda i, j, k: (k, i)),
        (do, (TM, D_MODEL), lambda i, j, k: (k, 0)), "tn",
        [((D_MODEL, D_MODEL), BF16, (dcol, D_MODEL), lambda i, j, k: (i, 0), "row")], acc_shape=(dcol, D_MODEL))

    def mixed_epi(acc, attn, ssm, yg, t, g_a, g_s):
        dattn, dg_a = _vjp(_rms, (attn, g_a), acc[:, :D_ATTN])
        dssm, dg_s = _vjp(_rms, (ssm, g_s), acc[:, D_ATTN:])
        sg = _sigmoid(t)
        dt = dssm * yg * sg * (1.0 - sg)
        return dattn, dt, dssm * sg, dg_a, dg_s, jnp.sum(dt, axis=0, keepdims=True)

    dattn, dt_glu, dyg_dir, gr["attn_norm_g"], gr["ssm_norm_g"], gr["ssm_b_glu"] = _mm(
        "w_out_bwd_x", (nt, 1, 1), _rows(do, TM), _whole(gw["w_out"]), "nt",
        [_row_out(s_len, D_ATTN, F32, TM), _row_out(s_len, D_SSM, BF16, TM), _row_out(s_len, D_SSM, F32, TM),
         _col_out(D_ATTN), _col_out(D_SSM), _col_out(D_SSM)],
        extras=[_rows(sv["attn"], TM), _rows(sv["ssm"], TM), _rows(sv["yg"], TM), _rows(sv["t_glu"], TM),
                _whole(gains["attn_norm_g"]), _whole(gains["ssm_norm_g"])], epilogue=mixed_epi)

    a_cat, w_b, w_c, d_vec = sp
    ts = SCAN_ROWS
    u_spec = (sv["proj"], (TM, D_SSM), lambda i, *_: (i, 3))
    (wg["ssm_w_glu"],) = _mm(
        "ssm_bwd_wglu", (1, 1, nt), (sv["yg"], (TM, D_SSM), lambda i, j, k: (k, 0)),
        (dt_glu, (TM, D_SSM), lambda i, j, k: (k, 0)), "tn",
        [((D_SSM, D_SSM), BF16, (D_SSM, D_SSM), lambda i, j, k: (0, 0), "row")], acc_shape=(D_SSM, D_SSM))

    def gelu_epi(acc, dy_dir, z, u, dv):
        (dz,) = _vjp(_gelu, (z,), acc + dy_dir)
        return dz, dz * dv, jnp.sum(dz * u, axis=0, keepdims=True)

    dz, du_dir, gr["ssm_d"] = _mm(
        "ssm_bwd_glu", (nt, 1, 1), _rows(dt_glu, TM), _whole(gw["w_glu"]), "nt",
        [_row_out(s_len, D_SSM, BF16, TM), _row_out(s_len, D_SSM, F32, TM), _col_out(D_SSM)],
        extras=[_rows(dyg_dir, TM), _rows(sv["z"], TM), u_spec, _whole(d_vec)], epilogue=gelu_epi)

    (d_w_c,) = _mm(
        "ssm_bwd_wc", (2 * N_STATE // 1024, 1, nt), (sv["xs"], (TM, 1024), lambda i, j, k: (k, i)),
        (dz, (TM, D_SSM), lambda i, j, k: (k, 0)), "tn",
        [((2 * N_STATE, D_SSM), F32, (1024, D_SSM), lambda i, j, k: (i, 0), "row")], acc_shape=(1024, D_SSM))
    (dxs,) = _mm("ssm_bwd_dx", (s_len // ts, 1, 1), _rows(dz, ts), _whole(w_c), "nt",
                 [_row_out(s_len, 2 * N_STATE, F32, ts)])
    gs = _scan("ssm_scan_rev", dxs, _conj(a_cat), True)

    def da_fn(x, x_before, g):
        i = pl.program_id(0)
        rolled = pltpu.roll(x, 1, 0)
        first_row = jnp.where(i > 0, x_before[7:8, :], 0.0)
        rows = lax.broadcasted_iota(jnp.int32, x.shape, 0)
        xp = jnp.where(rows == 0, first_row, rolled)
        xr, xi = xp[:, :N_STATE], xp[:, N_STATE:]
        g_r, g_i = g[:, :N_STATE], g[:, N_STATE:]
        d_re = jnp.sum(xr * g_r + xi * g_i, axis=0, keepdims=True)
        d_im = jnp.sum(xr * g_i - xi * g_r, axis=0, keepdims=True)
        return (jnp.concatenate([d_re, d_im], axis=1),)

    (d_a,) = _rowwise(
        "ssm_bwd_da", s_len, ts,
        [_rows(sv["xs"], ts), (sv["xs"], (8, 2 * N_STATE), lambda i: (jnp.maximum(i * (ts // 8) - 1, 0), 0)),
         _rows(gs, ts)], [_col_out(2 * N_STATE)], da_fn)

    (d_w_b,) = _mm(
        "ssm_bwd_wb", (1, 2, nt), (sv["proj"], (TM, D_SSM), lambda i, j, k: (k, 3)),
        (gs, (TM, N_STATE), lambda i, j, k: (k, j)), "tn",
        [((D_SSM, 2 * N_STATE), F32, (D_SSM, N_STATE), lambda i, j, k: (0, j), "row")], acc_shape=(D_SSM, N_STATE))
    (du,) = _mm("ssm_bwd_du", (s_len // ts, 1, 1), _rows(gs, ts), _whole(w_b), "nt",
                [_row_out(s_len, D_SSM, BF16, ts)], extras=[_rows(du_dir, ts)], epilogue=lambda acc, d: (acc + d,))

    dq = dk = dv = None
    pieces = []
    for d in DILATIONS:
        pieces += _attn_bwd(s_len, sv["q"], sv["k"], sv["v"], dattn, sv["attn"], sv["lse"], d)
    n_blk = s_len // BAND
    ins = []
    for pi, d in enumerate(DILATIONS):
        dq_p, dka, dkb, dva, dvb = pieces[5 * pi:5 * pi + 5]
        ahead = lambda i, d=d: (jnp.minimum(i + d, n_blk - 1), 0)
        ins += [_rows(dq_p, BAND), _rows(dka, BAND), (dkb, (BAND, D_ATTN), ahead), _rows(dva, BAND),
                (dvb, (BAND, D_ATTN), ahead)]
    c, s1, s2 = rope
    ins += [_rows(c, BAND), _rows(s1, BAND), _rows(s2, BAND)]

    def qkv_fn(*blocks):
        i = pl.program_id(0)
        cc, a1, a2 = blocks[15:]
        dq_t = dk_t = dv_t = 0.0
        for pi, d in enumerate(DILATIONS):
            dq_p, dka, dkb, dva, dvb = blocks[5 * pi:5 * pi + 5]
            live = i + d < n_blk
            dq_t = dq_t + dq_p
            dk_t = dk_t + dka + jnp.where(live, dkb, 0.0)
            dv_t = dv_t + dva + jnp.where(live, dvb, 0.0)
        return _rope_transposed(dq_t, cc, a1, a2), _rope_transposed(dk_t, cc, a1, a2), dv_t

    dq, dk, dv = _rowwise("attn_bwd_sum", s_len, BAND, ins, [_row_out(s_len, D_ATTN, BF16, BAND)] * 3, qkv_fn)
    dproj = jnp.concatenate([dq, dk, dv, du], axis=1)

    pcol = 2 * D_MODEL // N_DEV
    (wg["w_in"],) = _mm(
        "w_in_bwd_w", (1, N_DEV, nt), (sv["a_in"], (TM, D_MODEL), lambda i, j, k: (k, 0)),
        (dproj, (TM, pcol), lambda i, j, k: (k, j)), "tn",
        [((N_DEV, D_MODEL, pcol), BF16, (None, D_MODEL, pcol), lambda i, j, k: (j, 0, 0), "row")],
        acc_shape=(D_MODEL, pcol))

    def in_epi(acc, dh, h, g_pre, f, g_post):
        dx, dg_pre = _vjp(_rms, (h, g_pre), acc)
        dh1 = dh + dx
        df, dg_post = _vjp(_rms, (f, g_post), 0.5 * dh1)
        return dh1, df, dg_pre, dg_post

    dh1, df1, gr["mix_pre_g"], gr["ffn1_post_g"] = _mm(
        "w_in_bwd_x", (nt, 1, N_DEV), (dproj, (TM, pcol), lambda i, j, k: (i, k)),
        (gw["C"], (None, D_MODEL, pcol), lambda i, j, k: (k, 0, 0)), "nt",
        [_row_out(s_len, D_MODEL, F32, TM), _row_out(s_len, D_MODEL, BF16, TM), _col_out(D_MODEL), _col_out(D_MODEL)],
        extras=[_rows(dh2, TM), _rows(sv["h1"], TM), _whole(gains["mix_pre_g"]), _rows(sv["ffn1"]["f"], TM),
                _whole(gains["ffn1_post_g"])], epilogue=in_epi, acc_shape=(TM, D_MODEL))

    def ffn1_final(dn, dh, h, g_pre):
        dx, dg_pre = _vjp(_rms, (h, g_pre), dn)
        return dh + dx, dg_pre

    wg["ffn1_w_gate"], wg["ffn1_w_up"], wg["ffn1_w_down"], (dh0, gr["ffn1_pre_g"]) = _ffn_bwd(
        s_len, sv["ffn1"], df1, gw["A"], 0, gw["B1"], ffn1_final,
        [_rows(dh1, TM), _rows(sv["ffn1"]["h"], TM), _whole(gains["ffn1_pre_g"])],
        [_row_out(s_len, D_MODEL, F32, TM), _col_out(D_MODEL)])

    return dh0, wg, gr, (d_a, d_w_b, d_w_c)


def _peers():
    x, y, c = lax.axis_index("x"), lax.axis_index("y"), lax.axis_index("c")
    me = 4 * x + 2 * y + c
    peers = []
    for k in range(1, N_DEV):
        kx, ky, kc = (k >> 2) & 1, (k >> 1) & 1, k & 1
        px, py, pc = x ^ kx, y ^ ky, c ^ kc
        peers.append(((px, py, pc), 4 * px + 2 * py + pc))
    return me, peers


def _exchange(name, arrays, scatter):
    n = len(arrays)

    def body(*refs):
        in_refs, out_refs = refs[:n], refs[n:2 * n]
        send_sems, recv_sems, local_sems = refs[2 * n:]
        me, peers = _peers()
        copies = []
        for t in range(n):
            src_of = (lambda p, t=t: in_refs[t].at[p]) if scatter else (lambda p, t=t: in_refs[t])
            local = pltpu.make_async_copy(src_of(me), out_refs[t].at[me], local_sems.at[t])
            local.start()
            copies.append(local)
        remote = []
        for t in range(n):
            src_of = (lambda p, t=t: in_refs[t].at[p]) if scatter else (lambda p, t=t: in_refs[t])
            for k, (peer, peer_id) in enumerate(peers):
                cp = pltpu.make_async_remote_copy(
                    src_ref=src_of(peer_id), dst_ref=out_refs[t].at[me], send_sem=send_sems.at[t, k],
                    recv_sem=recv_sems.at[t, k], device_id=peer, device_id_type=pl.DeviceIdType.MESH)
                cp.start()
                remote.append((t, k, peer_id, cp))
        for t, k, peer_id, cp in remote:
            src_of = (lambda p, t=t: in_refs[t].at[p]) if scatter else (lambda p, t=t: in_refs[t])
            arrival = pltpu.make_async_remote_copy(
                src_ref=src_of(peer_id), dst_ref=out_refs[t].at[peer_id], send_sem=send_sems.at[t, k],
                recv_sem=recv_sems.at[t, k], device_id=peers[k][0], device_id_type=pl.DeviceIdType.MESH)
            arrival.wait_recv()
        for t, k, peer_id, cp in remote:
            cp.wait_send()
        for local in copies:
            local.wait()

    def out_shape(a):
        return jax.ShapeDtypeStruct(((N_DEV,) + a.shape[1:]) if scatter else ((N_DEV,) + a.shape), a.dtype)

    any_spec = pl.BlockSpec(memory_space=pl.ANY)
    return pl.pallas_call(
        body,
        name=name,
        in_specs=[any_spec] * n,
        out_specs=[any_spec] * n,
        out_shape=[out_shape(a) for a in arrays],
        scratch_shapes=[pltpu.SemaphoreType.DMA((n, N_DEV - 1)), pltpu.SemaphoreType.DMA((n, N_DEV - 1)),
                        pltpu.SemaphoreType.DMA((n,))],
    )(*arrays)


def _adam_math(g, w, m, v):
    m = ADAM_B1 * m + (1.0 - ADAM_B1) * g
    v = ADAM_B2 * v + (1.0 - ADAM_B2) * (g * g)
    m_hat = m / (1.0 - ADAM_B1 ** ADAM_STEP)
    v_hat = v / (1.0 - ADAM_B2 ** ADAM_STEP)
    delta = -ADAM_LR * (m_hat / (jnp.sqrt(v_hat) + ADAM_EPS) + ADAM_WD * w)
    return delta, m, v


def _adamw(name, recv, recv_block, recv_map, w, m, v, tr):
    rows, cols = w.shape

    def fn(r, wb, mb, vb):
        g = r[0].astype(F32)
        for s in range(1, N_DEV):
            g = g + r[s].astype(F32)
        return (g,) + _adam_math(g, wb, mb, vb)

    return _rowwise(name, rows, tr, [(recv, recv_block, recv_map), _rows(w, tr), _rows(m, tr), _rows(v, tr)],
                    [_row_out(rows, cols, F32, tr)] * 4, fn)


def _small_sizes(shapes):
    return [(n, math.prod(shapes[n][1:])) for n in SMALL]


def _pack_small(tensors, depth):
    flat = jnp.concatenate([tensors[n].reshape(depth, -1).astype(F32) for n in SMALL], axis=1)
    used = flat.shape[1]
    padded = -(-used // SMALL_PAD) * SMALL_PAD
    return jnp.pad(flat, ((0, 0), (0, padded - used)))


def _local_step(x, p, positions, loss_target, w, gathered):
    s_len = x.shape[0]
    depth = p.shape[0]
    rope = _rope_tables(positions)
    gains = [{n: w[n][l].reshape(1, -1) for n in SMALL if w[n].ndim == 2 and n != "ssm_log_dt"} for l in range(depth)]
    ssm_args = lambda l: tuple(w[n][l] for n in ("ssm_lam_re", "ssm_lam_im", "ssm_log_dt", "ssm_b_re", "ssm_b_im",
                                                  "ssm_c_re", "ssm_c_im"))
    sps, pulls = [], []
    for l in range(depth):
        (a_cat, w_b, w_c), pull = jax.vjp(_ssm_params, *ssm_args(l))
        sps.append((a_cat, w_b.astype(BF16), w_c.astype(BF16), w["ssm_d"][l].reshape(1, D_SSM)))
        pulls.append(pull)

    (n,) = _rowwise("pre_norm", s_len, TM, [_rows(x, TM), _whole(gains[0]["ffn1_pre_g"])],
                    [_row_out(s_len, D_MODEL, BF16, TM)], lambda a, g: (_rms(a, g),))
    h = x
    saved = []
    for l in range(depth):
        g_next = gains[l + 1]["ffn1_pre_g"] if l + 1 < depth else gains[l]["ffn1_pre_g"]
        h, n, sv = _layer_fwd(s_len, h, n, p[l], rope, gathered[l], sps[l], gains[l], g_next)
        saved.append(sv)

    def loss_fn(y, t):
        e = y - t
        return e * (1.0 / D_MODEL), jnp.sum(e * e, axis=0, keepdims=True)

    dh, sq = _rowwise("loss", s_len, TM, [_rows(h, TM), _rows(loss_target, TM)],
                      [_row_out(s_len, D_MODEL, F32, TM), _col_out(D_MODEL)], loss_fn)
    loss = 0.5 * jnp.sum(sq) / D_MODEL

    w_grads, small_grads = [None] * depth, [None] * depth
    for l in reversed(range(depth)):
        dh, wg, gr, (d_a, d_w_b, d_w_c) = _layer_bwd(s_len, dh, saved[l], rope, gathered[l], sps[l], gains[l])
        d_lre, d_lim, d_dt, d_bre, d_bim, d_cre, d_cim = pulls[l]((d_a, d_w_b, d_w_c))
        gr.update(ssm_lam_re=d_lre, ssm_lam_im=d_lim, ssm_log_dt=d_dt, ssm_b_re=d_bre, ssm_b_im=d_bim,
                  ssm_c_re=d_cre, ssm_c_im=d_cim)
        w_grads[l], small_grads[l] = wg, gr
    return loss, dh, w_grads, small_grads


def _shard_groups(w, l):
    cast = lambda n: w[n][l].astype(BF16)
    return [jnp.stack([cast("ffn1_w_gate"), cast("ffn1_w_up"), cast("ffn2_w_gate"), cast("ffn2_w_up")]),
            jnp.stack([cast("ffn1_w_down"), cast("ffn2_w_down")]),
            jnp.stack([cast("w_out"), cast("ple_w_gate")]),
            cast("w_in"), cast("ple_w_up"), cast("ssm_w_glu")]


def _gathered_views(groups):
    g_a, g_b1, g_b2, g_c, g_d, g_e = groups
    return dict(A=g_a, B1=g_b1, C=g_c, D=g_d, w_out=g_b2[:, 0].reshape(D_MODEL, D_MODEL),
                ple_w_gate=g_b2[:, 1].reshape(D_MODEL, D_MODEL), w_glu=g_e.reshape(D_SSM, D_SSM))


def _grad_groups(wg):
    rows = lambda t: t.reshape(N_DEV, t.shape[0] // N_DEV, t.shape[1])
    return [jnp.stack([wg["ffn1_w_gate"], wg["ffn1_w_up"], wg["ffn2_w_gate"], wg["ffn2_w_up"]], axis=1),
            jnp.stack([wg["ffn1_w_down"], wg["ffn2_w_down"]], axis=1),
            jnp.stack([rows(wg["w_out"]), rows(wg["ple_w_gate"])], axis=1),
            wg["w_in"], wg["ple_w_up"], rows(wg["ssm_w_glu"])]


_PLACE = {"ffn1_w_gate": (0, 0), "ffn1_w_up": (0, 1), "ffn2_w_gate": (0, 2), "ffn2_w_up": (0, 3),
          "ffn1_w_down": (1, 0), "ffn2_w_down": (1, 1), "w_out": (2, 0), "ple_w_gate": (2, 1),
          "w_in": (3, None), "ple_w_up": (4, None), "ssm_w_glu": (5, None)}


def _update_sharded(name, recv_groups, w_l, m_l, v_l):
    group, slot = _PLACE[name]
    recv = recv_groups[group]
    rows, cols = w_l.shape
    tr = rows if rows <= 256 else (rows // 2 if rows % 256 else 256)
    if slot is None:
        block, imap = (N_DEV, tr, cols), (lambda i: (0, i, 0))
    else:
        block, imap = (N_DEV, None, tr, cols), (lambda i: (0, slot, i, 0))
    return _adamw(f"adamw_{name}", recv, block, imap, w_l, m_l, v_l, tr)


def kernel(x, p, positions, ffn1_pre_g, ffn1_w_gate, ffn1_w_up, ffn1_w_down, ffn1_post_g, mix_pre_g, w_in, attn_norm_g, ssm_lam_re, ssm_lam_im, ssm_log_dt, ssm_b_re, ssm_b_im, ssm_c_re, ssm_c_im, ssm_d, ssm_w_glu, ssm_b_glu, ssm_norm_g, w_out, mix_post_g, ffn2_pre_g, ffn2_w_gate, ffn2_w_up, ffn2_w_down, ffn2_post_g, ple_w_up, ple_w_gate, ple_post_g, loss_target, m_ffn1_pre_g, m_ffn1_w_gate, m_ffn1_w_up, m_ffn1_w_down, m_ffn1_post_g, m_mix_pre_g, m_w_in, m_attn_norm_g, m_ssm_lam_re, m_ssm_lam_im, m_ssm_log_dt, m_ssm_b_re, m_ssm_b_im, m_ssm_c_re, m_ssm_c_im, m_ssm_d, m_ssm_w_glu, m_ssm_b_glu, m_ssm_norm_g, m_w_out, m_mix_post_g, m_ffn2_pre_g, m_ffn2_w_gate, m_ffn2_w_up, m_ffn2_w_down, m_ffn2_post_g, m_ple_w_up, m_ple_w_gate, m_ple_post_g, v_ffn1_pre_g, v_ffn1_w_gate, v_ffn1_w_up, v_ffn1_w_down, v_ffn1_post_g, v_mix_pre_g, v_w_in, v_attn_norm_g, v_ssm_lam_re, v_ssm_lam_im, v_ssm_log_dt, v_ssm_b_re, v_ssm_b_im, v_ssm_c_re, v_ssm_c_im, v_ssm_d, v_ssm_w_glu, v_ssm_b_glu, v_ssm_norm_g, v_w_out, v_mix_post_g, v_ffn2_pre_g, v_ffn2_w_gate, v_ffn2_w_up, v_ffn2_w_down, v_ffn2_post_g, v_ple_w_up, v_ple_w_gate, v_ple_post_g):
    args = dict(locals())
    w = {n: args[n] for n in WEIGHTS}
    mom = {n: args["m_" + n] for n in WEIGHTS}
    var = {n: args["v_" + n] for n in WEIGHTS}
    depth = p.shape[0]

    gathered = [_gathered_views(_exchange("all_gather", _shard_groups(w, l), scatter=False)) for l in range(depth)]
    loss, dx, w_grads, small_grads = _local_step(x[0], p[:, 0], positions[0], loss_target[0], w, gathered)
    loss = lax.psum(loss, MESH_AXES)

    out = {}
    for l in range(depth):
        recv = _exchange("reduce_scatter", _grad_groups(w_grads[l]), scatter=True)
        for n in SHARDED:
            out.setdefault(n, []).append(_update_sharded(n, recv, w[n][l], mom[n][l], var[n][l]))
    result = {n: [jnp.stack([out[n][l][q] for l in range(depth)]) for q in range(4)] for n in SHARDED}

    grads = {n: jnp.stack([small_grads[l][n].reshape(w[n].shape[1:]) for l in range(depth)]) for n in SMALL}
    packed = [_pack_small(t, depth) for t in (grads, w, mom, var)]
    width = packed[0].shape[1]
    as_rows = lambda t: t.reshape(depth * width // SMALL_PAD, SMALL_PAD)
    (recv_small,) = _exchange("small_exchange", [as_rows(packed[0])], scatter=False)
    n_rows = depth * width // SMALL_PAD
    small_out = _adamw("adamw_small", recv_small, (N_DEV, 8, SMALL_PAD), lambda i: (0, i, 0),
                       as_rows(packed[1]), as_rows(packed[2]), as_rows(packed[3]), 8)
    for q in range(4):
        flat = small_out[q].reshape(depth, width)
        off = 0
        for n in SMALL:
            size = math.prod(w[n].shape[1:])
            result.setdefault(n, [None] * 4)[q] = flat[:, off:off + size].reshape(w[n].shape)
            off += size

    outputs = [loss, dx[None]]
    for q in range(4):
        outputs += [result[n][q] for n in WEIGHTS]
    return tuple(outputs)
```

```python
import functools
import math

import jax
import jax.numpy as jnp
from jax import lax
from jax.experimental import pallas as pl
from jax.experimental.pallas import tpu as pltpu

F32 = jnp.float32
BF16 = jnp.bfloat16

N_DEV = 8
D_MODEL = 1024
D_FF = 2816
FF_SHARD = D_FF // N_DEV
D_ATTN = 512
D_SSM = 512
HEAD_DIM = 64
N_HEADS = 8
ROPE_DIM = 16
ROPE_THETA = 500000.0
DILATIONS = (1, 4, 16)
BAND = 128
N_GROUPS = 32
SSM_GROUP = 16
SSM_STATE = 64
N_STATE = N_GROUPS * SSM_STATE
PLE_DIM = 256
NORM_EPS = 1e-6
ADAM_LR, ADAM_B1, ADAM_B2, ADAM_EPS, ADAM_WD, ADAM_STEP = 0.001, 0.9, 0.999, 1e-08, 0.01, 10

VMEM_LIMIT_BYTES = 52 * 1024 * 1024
TM = 512
MESH_AXES = ("x", "y", "c")

WEIGHTS = ['ffn1_pre_g', 'ffn1_w_gate', 'ffn1_w_up', 'ffn1_w_down', 'ffn1_post_g', 'mix_pre_g', 'w_in', 'attn_norm_g',
           'ssm_lam_re', 'ssm_lam_im', 'ssm_log_dt', 'ssm_b_re', 'ssm_b_im', 'ssm_c_re', 'ssm_c_im', 'ssm_d',
           'ssm_w_glu', 'ssm_b_glu', 'ssm_norm_g', 'w_out', 'mix_post_g', 'ffn2_pre_g', 'ffn2_w_gate', 'ffn2_w_up',
           'ffn2_w_down', 'ffn2_post_g', 'ple_w_up', 'ple_w_gate', 'ple_post_g']
SHARDED = ['ffn1_w_gate', 'ffn1_w_up', 'ffn1_w_down', 'w_in', 'ssm_w_glu', 'w_out', 'ffn2_w_gate', 'ffn2_w_up',
           'ffn2_w_down', 'ple_w_up', 'ple_w_gate']
SMALL = [n for n in WEIGHTS if n not in SHARDED]
SMALL_PAD = 1024


def _params(n_axes):
    return pltpu.CompilerParams(dimension_semantics=("arbitrary",) * n_axes, vmem_limit_bytes=VMEM_LIMIT_BYTES)


_DIMS = {"nn": (((1,), (0,)), ((), ())), "nt": (((1,), (1,)), ((), ())), "tn": (((0,), (0,)), ((), ()))}


def _dot(a, b, mode):
    return lax.dot_general(a.astype(BF16), b.astype(BF16), _DIMS[mode], preferred_element_type=F32)


def _store(out_refs, vals, kinds, first):
    for ref, val, kind in zip(out_refs, vals, kinds):
        if kind == "row":
            ref[...] = val.astype(ref.dtype)
        else:
            @pl.when(first)
            def _(ref=ref, val=val):
                ref[...] = val.astype(ref.dtype)

            @pl.when(jnp.logical_not(first))
            def _(ref=ref, val=val):
                ref[...] += val.astype(ref.dtype)


def _mm(name, grid, a, b, mode, outs, extras=(), epilogue=None, acc_shape=None):
    nk = grid[2]
    ne, no = len(extras), len(outs)
    kinds = [o[4] for o in outs]
    assert all(k == "row" for k in kinds) or grid[1] == 1

    def body(*refs):
        a_ref, b_ref = refs[0], refs[1]
        ex_refs = refs[2:2 + ne]
        out_refs = refs[2 + ne:2 + ne + no]
        part = _dot(a_ref[...], b_ref[...], mode)
        first = pl.program_id(0) == 0

        def finish(acc):
            vals = epilogue(acc, *[r[...] for r in ex_refs]) if epilogue is not None else (acc,)
            _store(out_refs, vals, kinds, first)

        if nk == 1:
            finish(part)
        else:
            acc_ref = refs[-1]
            k = pl.program_id(2)

            @pl.when(k == 0)
            def _():
                acc_ref[...] = part

            @pl.when(k > 0)
            def _():
                acc_ref[...] += part

            @pl.when(k == nk - 1)
            def _():
                finish(acc_ref[...])

    ins = (a, b) + tuple(extras)
    res = pl.pallas_call(
        body,
        name=name,
        grid=grid,
        in_specs=[_in_spec(t) for t in ins],
        out_specs=[pl.BlockSpec(o[2], o[3]) for o in outs],
        out_shape=[jax.ShapeDtypeStruct(o[0], o[1]) for o in outs],
        scratch_shapes=[pltpu.VMEM(acc_shape, F32)] if nk > 1 else [],
        compiler_params=_params(3),
    )(*[x[0] for x in ins])
    return res


def _rowwise(name, n_rows, tm, ins, outs, fn):
    kinds = [o[4] for o in outs]
    ni = len(ins)

    def body(*refs):
        vals = fn(*[r[...] for r in refs[:ni]])
        _store(refs[ni:], vals, kinds, pl.program_id(0) == 0)

    return pl.pallas_call(
        body,
        name=name,
        grid=(n_rows // tm,),
        in_specs=[_in_spec(t) for t in ins],
        out_specs=[pl.BlockSpec(o[2], o[3]) for o in outs],
        out_shape=[jax.ShapeDtypeStruct(o[0], o[1]) for o in outs],
        compiler_params=_params(1),
    )(*[x[0] for x in ins])


def _rows(arr, tm, col=0, width=None):
    width = arr.shape[1] if width is None else width
    return (arr, (tm, width), lambda i, *_: (i, col))


def _whole(arr):
    nd = arr.ndim
    return (arr, arr.shape, lambda *_: (0,) * nd)


def _resident(arr):
    nd = arr.ndim
    return (arr, arr.shape, lambda *_: (0,) * nd, dict(pipeline_mode=pl.Buffered(1)))


def _in_spec(t):
    return pl.BlockSpec(t[1], t[2], **(t[3] if len(t) > 3 else {}))


def _row_out(n_rows, width, dtype, tm, col=0, total=None):
    return ((n_rows, width if total is None else total), dtype, (tm, width), lambda i, *_: (i, col), "row")


def _col_out(width):
    return ((1, width), F32, (1, width), lambda *_: (0, 0), "colsum")


def _rms(x, g):
    return x * lax.rsqrt(jnp.mean(x * x, axis=-1, keepdims=True) + NORM_EPS) * g


def _sigmoid(x):
    return 1.0 / (1.0 + jnp.exp(-x))


def _gelu(x):
    return 0.5 * x * (1.0 + jnp.tanh(0.7978845608028654 * (x + 0.044715 * x * x * x)))


FFN_TM = 256
FF_HALF = D_FF // 2


def _ffn_fwd(s_len, h, n, w_gu, w_d, g_post, g_next):
    def gu_epi(acc):
        gate, up = acc[:, :D_FF], acc[:, D_FF:]
        return acc, gate * _sigmoid(gate) * up

    gu, act = _mm("ffn_gate_up", (s_len // FFN_TM, 1, 1), _rows(n, FFN_TM), _resident(w_gu), "nn",
                  [_row_out(s_len, 2 * D_FF, BF16, FFN_TM), _row_out(s_len, D_FF, BF16, FFN_TM)], epilogue=gu_epi)

    def down_epi(acc, h_blk, gp, gn):
        h_new = h_blk + 0.5 * _rms(acc, gp)
        return acc, h_new, _rms(h_new, gn)

    f, h_new, n_next = _mm(
        "ffn_down", (s_len // TM, 1, 1), _rows(act, TM), _resident(w_d), "nn",
        [_row_out(s_len, D_MODEL, F32, TM), _row_out(s_len, D_MODEL, F32, TM), _row_out(s_len, D_MODEL, BF16, TM)],
        extras=[_rows(h, TM), _whole(g_post), _whole(g_next)], epilogue=down_epi)
    return h_new, n_next, dict(n=n, gu=gu, act=act, f=f, h=h)


def _ffn_bwd(s_len, saved, df, w_gu, w_d, final_epi, final_extras, final_outs):
    nt = s_len // TM

    def act_epi(acc, gu_blk):
        gate = gu_blk[:, :D_FF].astype(F32)
        up = gu_blk[:, D_FF:].astype(F32)
        sg = _sigmoid(gate)
        dgate = acc * up * sg * (1.0 + gate * (1.0 - sg))
        dup = acc * gate * sg
        return (jnp.concatenate([dgate, dup], axis=1),)

    (dgu,) = _mm("ffn_bwd_act", (s_len // FFN_TM, 1, 1), _rows(df, FFN_TM), _resident(w_d), "nt",
                 [_row_out(s_len, 2 * D_FF, BF16, FFN_TM)], extras=[_rows(saved["gu"], FFN_TM)], epilogue=act_epi)

    (d_w_d,) = _mm(
        "ffn_bwd_wdown", (2, 1, nt), (saved["act"], (TM, FF_HALF), lambda i, j, k: (k, i)),
        (df, (TM, D_MODEL), lambda i, j, k: (k, 0)), "tn",
        [((D_FF, D_MODEL), BF16, (FF_HALF, D_MODEL), lambda i, j, k: (i, 0), "row")], acc_shape=(FF_HALF, D_MODEL))

    (d_w_gu,) = _mm(
        "ffn_bwd_wgu", (1, 4, nt), (saved["n"], (TM, D_MODEL), lambda i, j, k: (k, 0)),
        (dgu, (TM, FF_HALF), lambda i, j, k: (k, j)), "tn",
        [((D_MODEL, 2 * D_FF), BF16, (D_MODEL, FF_HALF), lambda i, j, k: (0, j), "row")], acc_shape=(D_MODEL, FF_HALF))

    outs = _mm("ffn_bwd_dn", (s_len // FFN_TM, 1, 1), _rows(dgu, FFN_TM), _resident(w_gu), "nt",
               final_outs, extras=final_extras, epilogue=final_epi)
    return d_w_gu, d_w_d, outs


def _band_mask(first_block):
    qi = lax.broadcasted_iota(jnp.int32, (BAND, 2 * BAND), 0)
    kj = lax.broadcasted_iota(jnp.int32, (BAND, 2 * BAND), 1)
    ok = (kj >= qi) & (kj <= qi + BAND)
    return ok & (jnp.logical_not(first_block) | (kj >= BAND))


def _attn_views(s_len, d):
    n_str = s_len // d
    nb = n_str // BAND
    blk = (BAND, D_ATTN)
    cur = lambda r, b: (b, r)
    prev = lambda r, b: (jnp.maximum(b - 1, 0), r)
    return n_str, nb, blk, cur, prev


def _attn_fwd(s_len, q, k, v, d):
    n_str, nb, blk, cur, prev = _attn_views(s_len, d)
    view = lambda t: t.reshape(n_str, d * D_ATTN)

    def body(q_ref, kp_ref, kc_ref, vp_ref, vc_ref, o_ref, lse_ref):
        mask = _band_mask(pl.program_id(1) == 0)
        qq = q_ref[...]
        kk = jnp.concatenate([kp_ref[...], kc_ref[...]], axis=0)
        vv = jnp.concatenate([vp_ref[...], vc_ref[...]], axis=0)
        for h in range(N_HEADS):
            sl = slice(h * HEAD_DIM, (h + 1) * HEAD_DIM)
            s = _dot(qq[:, sl], kk[:, sl], "nt") * (HEAD_DIM ** -0.5)
            s = jnp.where(mask, s, -1e30)
            m = jnp.max(s, axis=-1, keepdims=True)
            e = jnp.exp(s - m)
            den = jnp.sum(e, axis=-1, keepdims=True)
            o_ref[:, sl] = _dot(e / den, vv[:, sl], "nn")
            lse_ref[:, sl] = jnp.broadcast_to(m + jnp.log(den), (BAND, HEAD_DIM))

    o, lse = pl.pallas_call(
        body,
        name=f"attn_fwd_d{d}",
        grid=(d, nb),
        in_specs=[pl.BlockSpec(blk, cur), pl.BlockSpec(blk, prev), pl.BlockSpec(blk, cur),
                  pl.BlockSpec(blk, prev), pl.BlockSpec(blk, cur)],
        out_specs=[pl.BlockSpec(blk, cur), pl.BlockSpec(blk, cur)],
        out_shape=[jax.ShapeDtypeStruct((n_str, d * D_ATTN), F32)] * 2,
        compiler_params=_params(2),
    )(view(q), view(k), view(k), view(v), view(v))
    return o.reshape(s_len, D_ATTN), lse.reshape(s_len, D_ATTN)


def _attn_bwd(s_len, q, k, v, dattn, attn, lse, d):
    n_str, nb, blk, cur, prev = _attn_views(s_len, d)
    view = lambda t: t.reshape(n_str, d * D_ATTN)

    def body(q_ref, kp_ref, kc_ref, vp_ref, vc_ref, da_ref, at_ref, lse_ref, dq_ref, dka_ref, dkb_ref, dva_ref, dvb_ref):
        mask = _band_mask(pl.program_id(1) == 0)
        qq = q_ref[...]
        kk = jnp.concatenate([kp_ref[...], kc_ref[...]], axis=0)
        vv = jnp.concatenate([vp_ref[...], vc_ref[...]], axis=0)
        da = da_ref[...]
        prod = da * at_ref[...]
        scale = HEAD_DIM ** -0.5
        for h in range(N_HEADS):
            sl = slice(h * HEAD_DIM, (h + 1) * HEAD_DIM)
            s = _dot(qq[:, sl], kk[:, sl], "nt") * scale
            p = jnp.where(mask, jnp.exp(s - lse_ref[:, h * HEAD_DIM:h * HEAD_DIM + 1]), 0.0)
            dp = _dot(da[:, sl], vv[:, sl], "nt")
            ds = p * (dp - jnp.sum(prod[:, sl], axis=-1, keepdims=True))
            dq_ref[:, sl] = _dot(ds, kk[:, sl], "nn") * scale
            dk = _dot(ds, qq[:, sl], "tn") * scale
            dv = _dot(p, da[:, sl], "tn")
            dkb_ref[:, sl] = dk[:BAND]
            dka_ref[:, sl] = dk[BAND:]
            dvb_ref[:, sl] = dv[:BAND]
            dva_ref[:, sl] = dv[BAND:]

    outs = pl.pallas_call(
        body,
        name=f"attn_bwd_d{d}",
        grid=(d, nb),
        in_specs=[pl.BlockSpec(blk, cur), pl.BlockSpec(blk, prev), pl.BlockSpec(blk, cur),
                  pl.BlockSpec(blk, prev), pl.BlockSpec(blk, cur),
                  pl.BlockSpec(blk, cur), pl.BlockSpec(blk, cur), pl.BlockSpec(blk, cur)],
        out_specs=[pl.BlockSpec(blk, cur)] * 5,
        out_shape=[jax.ShapeDtypeStruct((n_str, d * D_ATTN), F32)] * 5,
        compiler_params=_params(2),
    )(view(q), view(k), view(k), view(v), view(v), view(dattn), view(attn), view(lse))
    return [t.reshape(s_len, D_ATTN) for t in outs]


def _rope_tables(positions):
    half = ROPE_DIM // 2
    inv_freq = ROPE_THETA ** (-jnp.arange(half, dtype=F32) * (2.0 / ROPE_DIM))
    ang = positions.astype(F32)[:, None] * inv_freq
    cos, sin = jnp.cos(ang), jnp.sin(ang)
    s_len = positions.shape[0]
    one = jnp.ones((s_len, HEAD_DIM - ROPE_DIM), F32)
    zero8 = jnp.zeros((s_len, half), F32)
    zero = jnp.zeros((s_len, HEAD_DIM - ROPE_DIM), F32)
    c = jnp.concatenate([cos, cos, one], axis=1)
    s1 = jnp.concatenate([zero8, sin, zero], axis=1)
    s2 = jnp.concatenate([-sin, zero8, zero], axis=1)
    tile = lambda t: jnp.tile(t, (1, N_HEADS))
    return tile(c), tile(s1), tile(s2)


def _rope(t, c, s1, s2):
    half = ROPE_DIM // 2
    return t * c + pltpu.roll(t, half, 1) * s1 + pltpu.roll(t, D_ATTN - half, 1) * s2


def _rope_transposed(dt, c, s1, s2):
    half = ROPE_DIM // 2
    return dt * c + pltpu.roll(dt * s1, D_ATTN - half, 1) + pltpu.roll(dt * s2, half, 1)


SCAN_ROWS = 256
SCAN_LANES = 512


def _scan(name, bu, a_cat, reverse):
    s_len, width = bu.shape
    nb = s_len // SCAN_ROWS
    half = width // 2

    def body(bu_ref, a_ref, x_ref, carry_ref):
        @pl.when(pl.program_id(0) == 0)
        def _():
            carry_ref[...] = jnp.zeros_like(carry_ref)

        for c in range(half // SCAN_LANES):
            re = pl.ds(c * SCAN_LANES, SCAN_LANES)
            im = pl.ds(half + c * SCAN_LANES, SCAN_LANES)
            ar = a_ref[:, re]
            ai = a_ref[:, im]

            def step(s, state):
                xr, xi = state
                t = (SCAN_ROWS - 1 - s) if reverse else s
                row = pl.ds(t, 1)
                nr = ar * xr - ai * xi + bu_ref[row, re]
                ni = ar * xi + ai * xr + bu_ref[row, im]
                x_ref[row, re] = nr
                x_ref[row, im] = ni
                return nr, ni

            xr, xi = lax.fori_loop(0, SCAN_ROWS, step, (carry_ref[0:1, re], carry_ref[0:1, im]), unroll=8)
            carry_ref[0:1, re] = xr
            carry_ref[0:1, im] = xi

    imap = (lambda i: (nb - 1 - i, 0)) if reverse else (lambda i: (i, 0))
    return pl.pallas_call(
        body,
        name=name,
        grid=(nb,),
        in_specs=[pl.BlockSpec((SCAN_ROWS, width), imap), pl.BlockSpec((1, width), lambda i: (0, 0))],
        out_specs=pl.BlockSpec((SCAN_ROWS, width), imap),
        out_shape=jax.ShapeDtypeStruct((s_len, width), F32),
        scratch_shapes=[pltpu.VMEM((8, width), F32)],
        compiler_params=_params(1),
    )(bu, a_cat)


def _ssm_params(lam_re, lam_im, log_dt, b_re, b_im, c_re, c_im):
    dt = jnp.exp(log_dt)[:, None]
    er = jnp.exp(lam_re * dt)
    a_re = er * jnp.cos(lam_im * dt)
    a_im = er * jnp.sin(lam_im * dt)
    nr, ni = a_re - 1.0, a_im
    den = lam_re * lam_re + lam_im * lam_im
    fr = (nr * lam_re + ni * lam_im) / den
    fi = (ni * lam_re - nr * lam_im) / den
    bb_re = fr[..., None] * b_re - fi[..., None] * b_im
    bb_im = fr[..., None] * b_im + fi[..., None] * b_re
    eye = jnp.eye(N_GROUPS, dtype=F32)

    def in_mat(bb):
        t = bb.transpose(0, 2, 1)[:, :, None, :] * eye[:, None, :, None]
        return t.reshape(D_SSM, N_STATE)

    def out_mat(cc):
        t = cc.transpose(0, 2, 1)[:, :, None, :] * eye[:, None, :, None]
        return t.reshape(N_STATE, D_SSM)

    w_b = jnp.concatenate([in_mat(bb_re), in_mat(bb_im)], axis=1)
    w_c = jnp.concatenate([out_mat(c_re), -out_mat(c_im)], axis=0)
    a_cat = jnp.concatenate([a_re.reshape(1, N_STATE), a_im.reshape(1, N_STATE)], axis=1)
    return a_cat, w_b, w_c


def _conj(a_cat):
    return jnp.concatenate([a_cat[:, :N_STATE], -a_cat[:, N_STATE:]], axis=1)


def _layer_fwd(s_len, h, n1, p_l, rope, gw, sp, gains, g_next):
    nt = s_len // TM
    sv = {}
    h1, a_in, sv["ffn1"] = _ffn_fwd(s_len, h, n1, gw["w_gu1"], gw["w_d1"], gains["ffn1_post_g"], gains["mix_pre_g"])

    (proj,) = _mm("w_in", (nt, 1, 1), _rows(a_in, TM), _resident(gw["w_in"]), "nn",
                  [_row_out(s_len, 2 * D_MODEL, F32, TM)])

    c, s1, s2 = rope
    q, k, v = _rowwise(
        "rope", s_len, TM,
        [_rows(proj, TM, 0, D_ATTN), _rows(proj, TM, 1, D_ATTN), _rows(proj, TM, 2, D_ATTN),
         _rows(c, TM), _rows(s1, TM), _rows(s2, TM)],
        [_row_out(s_len, D_ATTN, BF16, TM)] * 3,
        lambda tq, tk, tv, cc, a1, a2: (_rope(tq, cc, a1, a2), _rope(tk, cc, a1, a2), tv))

    parts = []
    for d in DILATIONS:
        parts += list(_attn_fwd(s_len, q, k, v, d))

    def mix_fn(o1, l1, o2, l2, o3, l3, g):
        m = jnp.maximum(jnp.maximum(l1, l2), l3)
        e1, e2, e3 = jnp.exp(l1 - m), jnp.exp(l2 - m), jnp.exp(l3 - m)
        tot = e1 + e2 + e3
        attn = (e1 * o1 + e2 * o2 + e3 * o3) / tot
        return attn, m + jnp.log(tot), _rms(attn, g)

    attn, lse, attn_n = _rowwise(
        "attn_mix", s_len, TM, [_rows(t, TM) for t in parts] + [_whole(gains["attn_norm_g"])],
        [_row_out(s_len, D_ATTN, F32, TM), _row_out(s_len, D_ATTN, F32, TM), _row_out(s_len, D_ATTN, BF16, TM)], mix_fn)

    a_cat, w_b, w_c, d_vec = sp
    ts = SCAN_ROWS
    (bu,) = _mm("ssm_bu", (s_len // ts, 1, 1), (proj, (ts, D_SSM), lambda i, j, k: (i, 3)), _whole(w_b), "nn",
                [_row_out(s_len, 2 * N_STATE, F32, ts)])
    xs = _scan("ssm_scan", bu, a_cat, False)

    def y_epi(acc, u, dv):
        z = acc + dv * u
        return z, _gelu(z)

    z, yg = _mm("ssm_y", (s_len // ts, 1, 1), _rows(xs, ts), _whole(w_c), "nn",
                [_row_out(s_len, D_SSM, F32, ts)] * 2,
                extras=[(proj, (ts, D_SSM), lambda i, j, k: (i, 3)), _whole(d_vec)], epilogue=y_epi)

    def glu_epi(acc, y, b, g):
        t = acc + b
        ssm = y * _sigmoid(t)
        return t, ssm, _rms(ssm, g)

    t_glu, ssm, ssm_n = _mm(
        "ssm_glu", (nt, 1, 1), _rows(yg, TM), _whole(gw["w_glu"]), "nn",
        [_row_out(s_len, D_SSM, F32, TM), _row_out(s_len, D_SSM, F32, TM), _row_out(s_len, D_SSM, BF16, TM)],
        extras=[_rows(yg, TM), _whole(gains["ssm_b_glu"]), _whole(gains["ssm_norm_g"])], epilogue=glu_epi)

    mixed = jnp.concatenate([attn_n, ssm_n], axis=1)

    def out_epi(acc, h_blk, gp, gn):
        h_new = h_blk + _rms(acc, gp)
        return acc, h_new, _rms(h_new, gn)

    o, h2, n2 = _mm(
        "w_out", (nt, 1, 1), _rows(mixed, TM), _whole(gw["w_out"]), "nn",
        [_row_out(s_len, D_MODEL, F32, TM), _row_out(s_len, D_MODEL, F32, TM), _row_out(s_len, D_MODEL, BF16, TM)],
        extras=[_rows(h1, TM), _whole(gains["mix_post_g"]), _whole(gains["ffn2_pre_g"])], epilogue=out_epi)

    h3, _, sv["ffn2"] = _ffn_fwd(s_len, h2, n2, gw["w_gu2"], gw["w_d2"], gains["ffn2_post_g"], gains["ffn2_post_g"])

    (pu,) = _mm("ple_up", (nt, 1, 1), _rows(p_l, TM), _resident(gw["w_pu"]), "nn",
                [_row_out(s_len, D_MODEL, F32, TM)])

    def ple_epi(acc, pu_blk, h_blk, gp, gn):
        h_new = h_blk + _rms(pu_blk * _sigmoid(acc), gp)
        return acc, h_new, _rms(h_new, gn)

    gt, h4, n_next = _mm(
        "ple_gate", (nt, 1, 1), _rows(h3, TM), _whole(gw["ple_w_gate"]), "nn",
        [_row_out(s_len, D_MODEL, F32, TM), _row_out(s_len, D_MODEL, F32, TM), _row_out(s_len, D_MODEL, BF16, TM)],
        extras=[_rows(pu, TM), _rows(h3, TM), _whole(gains["ple_post_g"]), _whole(g_next)], epilogue=ple_epi)

    sv.update(h1=h1, a_in=a_in, proj=proj, q=q, k=k, v=v, attn=attn, lse=lse, xs=xs, z=z, yg=yg, t_glu=t_glu, ssm=ssm,
              mixed=mixed, o=o, h2=h2, h3=h3, pu=pu, gt=gt, p_l=p_l)
    return h4, n_next, sv


def _vjp(fn, args, cot):
    _, pull = jax.vjp(fn, *args)
    return pull(cot)


def _layer_bwd(s_len, dh4, sv, rope, gw, sp, gains):
    nt = s_len // TM
    gr = {}
    wg = {}

    def ple_fn(dh, pu, gt, g):
        dpu, dgt, dg = _vjp(lambda a, b, c: _rms(a * _sigmoid(b), c), (pu, gt, g), dh)
        return dpu, dgt, dg

    dpu, dgt, gr["ple_post_g"] = _rowwise(
        "ple_bwd", s_len, TM, [_rows(dh4, TM), _rows(sv["pu"], TM), _rows(sv["gt"], TM), _whole(gains["ple_post_g"])],
        [_row_out(s_len, D_MODEL, BF16, TM), _row_out(s_len, D_MODEL, BF16, TM), _col_out(D_MODEL)], ple_fn)

    def square_w_grad(name, lhs, rhs):
        half = D_MODEL // 2
        (dw,) = _mm(name, (2, 1, nt), (lhs, (TM, half), lambda i, j, k: (k, i)),
                    (rhs, (TM, D_MODEL), lambda i, j, k: (k, 0)), "tn",
                    [((D_MODEL, D_MODEL), BF16, (half, D_MODEL), lambda i, j, k: (i, 0), "row")],
                    acc_shape=(half, D_MODEL))
        return dw

    (wg["ple_w_up"],) = _mm(
        "ple_bwd_wup", (1, 1, nt), (sv["p_l"], (TM, PLE_DIM), lambda i, j, k: (k, 0)),
        (dpu, (TM, D_MODEL), lambda i, j, k: (k, 0)), "tn",
        [((PLE_DIM, D_MODEL), BF16, (PLE_DIM, D_MODEL), lambda i, j, k: (0, 0), "row")],
        acc_shape=(PLE_DIM, D_MODEL))
    wg["ple_w_gate"] = square_w_grad("ple_bwd_wgate", sv["h3"], dgt)

    def ple_dx_epi(acc, dh, f, g):
        dh3 = dh + acc
        df, dg = _vjp(_rms, (f, g), 0.5 * dh3)
        return dh3, df, dg

    dh3, df2, gr["ffn2_post_g"] = _mm(
        "ple_bwd_dx", (nt, 1, 1), _rows(dgt, TM), _whole(gw["ple_w_gate"]), "nt",
        [_row_out(s_len, D_MODEL, F32, TM), _row_out(s_len, D_MODEL, BF16, TM), _col_out(D_MODEL)],
        extras=[_rows(dh4, TM), _rows(sv["ffn2"]["f"], TM), _whole(gains["ffn2_post_g"])], epilogue=ple_dx_epi)

    def ffn2_final(dn, dh, h, g_pre, o, g_post):
        dx, dg_pre = _vjp(_rms, (h, g_pre), dn)
        dh2 = dh + dx
        do, dg_post = _vjp(_rms, (o, g_post), dh2)
        return dh2, do, dg_pre, dg_post

    wg["ffn2_w_gu"], wg["ffn2_w_down"], (dh2, do, gr["ffn2_pre_g"], gr["mix_post_g"]) = _ffn_bwd(
        s_len, sv["ffn2"], df2, gw["w_gu2"], gw["w_d2"], ffn2_final,
        [_rows(dh3, FFN_TM), _rows(sv["h2"], FFN_TM), _whole(gains["ffn2_pre_g"]), _rows(sv["o"], FFN_TM),
         _whole(gains["mix_post_g"])],
        [_row_out(s_len, D_MODEL, F32, FFN_TM), _row_out(s_len, D_MODEL, BF16, FFN_TM), _col_out(D_MODEL),
         _col_out(D_MODEL)])

    wg["w_out"] = square_w_grad("w_out_bwd_w", sv["mixed"], do)

    def mixed_epi(acc, attn, ssm, yg, t, g_a, g_s):
        dattn, dg_a = _vjp(_rms, (attn, g_a), acc[:, :D_ATTN])
        dssm, dg_s = _vjp(_rms, (ssm, g_s), acc[:, D_ATTN:])
        sg = _sigmoid(t)
        dt = dssm * yg * sg * (1.0 - sg)
        return dattn, dt, dssm * sg, dg_a, dg_s, jnp.sum(dt, axis=0, keepdims=True)

    dattn, dt_glu, dyg_dir, gr["attn_norm_g"], gr["ssm_norm_g"], gr["ssm_b_glu"] = _mm(
        "w_out_bwd_x", (nt, 1, 1), _rows(do, TM), _whole(gw["w_out"]), "nt",
        [_row_out(s_len, D_ATTN, F32, TM), _row_out(s_len, D_SSM, BF16, TM), _row_out(s_len, D_SSM, F32, TM),
         _col_out(D_ATTN), _col_out(D_SSM), _col_out(D_SSM)],
        extras=[_rows(sv["attn"], TM), _rows(sv["ssm"], TM), _rows(sv["yg"], TM), _rows(sv["t_glu"], TM),
                _whole(gains["attn_norm_g"]), _whole(gains["ssm_norm_g"])], epilogue=mixed_epi)

    a_cat, w_b, w_c, d_vec = sp
    ts = SCAN_ROWS
    u_spec = (sv["proj"], (TM, D_SSM), lambda i, *_: (i, 3))
    (wg["ssm_w_glu"],) = _mm(
        "ssm_bwd_wglu", (1, 1, nt), (sv["yg"], (TM, D_SSM), lambda i, j, k: (k, 0)),
        (dt_glu, (TM, D_SSM), lambda i, j, k: (k, 0)), "tn",
        [((D_SSM, D_SSM), BF16, (D_SSM, D_SSM), lambda i, j, k: (0, 0), "row")], acc_shape=(D_SSM, D_SSM))

    def gelu_epi(acc, dy_dir, z, u, dv):
        (dz,) = _vjp(_gelu, (z,), acc + dy_dir)
        return dz, dz * dv, jnp.sum(dz * u, axis=0, keepdims=True)

    dz, du_dir, gr["ssm_d"] = _mm(
        "ssm_bwd_glu", (nt, 1, 1), _rows(dt_glu, TM), _whole(gw["w_glu"]), "nt",
        [_row_out(s_len, D_SSM, BF16, TM), _row_out(s_len, D_SSM, F32, TM), _col_out(D_SSM)],
        extras=[_rows(dyg_dir, TM), _rows(sv["z"], TM), u_spec, _whole(d_vec)], epilogue=gelu_epi)

    (d_w_c,) = _mm(
        "ssm_bwd_wc", (2 * N_STATE // 1024, 1, nt), (sv["xs"], (TM, 1024), lambda i, j, k: (k, i)),
        (dz, (TM, D_SSM), lambda i, j, k: (k, 0)), "tn",
        [((2 * N_STATE, D_SSM), F32, (1024, D_SSM), lambda i, j, k: (i, 0), "row")], acc_shape=(1024, D_SSM))
    (dxs,) = _mm("ssm_bwd_dx", (s_len // ts, 1, 1), _rows(dz, ts), _whole(w_c), "nt",
                 [_row_out(s_len, 2 * N_STATE, F32, ts)])
    gs = _scan("ssm_scan_rev", dxs, _conj(a_cat), True)

    def da_fn(x, x_before, g):
        i = pl.program_id(0)
        rolled = pltpu.roll(x, 1, 0)
        first_row = jnp.where(i > 0, x_before[7:8, :], 0.0)
        rows = lax.broadcasted_iota(jnp.int32, x.shape, 0)
        xp = jnp.where(rows == 0, first_row, rolled)
        xr, xi = xp[:, :N_STATE], xp[:, N_STATE:]
        g_r, g_i = g[:, :N_STATE], g[:, N_STATE:]
        d_re = jnp.sum(xr * g_r + xi * g_i, axis=0, keepdims=True)
        d_im = jnp.sum(xr * g_i - xi * g_r, axis=0, keepdims=True)
        return (jnp.concatenate([d_re, d_im], axis=1),)

    (d_a,) = _rowwise(
        "ssm_bwd_da", s_len, ts,
        [_rows(sv["xs"], ts), (sv["xs"], (8, 2 * N_STATE), lambda i: (jnp.maximum(i * (ts // 8) - 1, 0), 0)),
         _rows(gs, ts)], [_col_out(2 * N_STATE)], da_fn)

    (d_w_b,) = _mm(
        "ssm_bwd_wb", (1, 2, nt), (sv["proj"], (TM, D_SSM), lambda i, j, k: (k, 3)),
        (gs, (TM, N_STATE), lambda i, j, k: (k, j)), "tn",
        [((D_SSM, 2 * N_STATE), F32, (D_SSM, N_STATE), lambda i, j, k: (0, j), "row")], acc_shape=(D_SSM, N_STATE))
    (du,) = _mm("ssm_bwd_du", (s_len // ts, 1, 1), _rows(gs, ts), _whole(w_b), "nt",
                [_row_out(s_len, D_SSM, BF16, ts)], extras=[_rows(du_dir, ts)], epilogue=lambda acc, d: (acc + d,))

    dq = dk = dv = None
    pieces = []
    for d in DILATIONS:
        pieces += _attn_bwd(s_len, sv["q"], sv["k"], sv["v"], dattn, sv["attn"], sv["lse"], d)
    n_blk = s_len // BAND
    ins = []
    for pi, d in enumerate(DILATIONS):
        dq_p, dka, dkb, dva, dvb = pieces[5 * pi:5 * pi + 5]
        ahead = lambda i, d=d: (jnp.minimum(i + d, n_blk - 1), 0)
        ins += [_rows(dq_p, BAND), _rows(dka, BAND), (dkb, (BAND, D_ATTN), ahead), _rows(dva, BAND),
                (dvb, (BAND, D_ATTN), ahead)]
    c, s1, s2 = rope
    ins += [_rows(c, BAND), _rows(s1, BAND), _rows(s2, BAND)]

    def qkv_fn(*blocks):
        i = pl.program_id(0)
        cc, a1, a2 = blocks[15:]
        dq_t = dk_t = dv_t = 0.0
        for pi, d in enumerate(DILATIONS):
            dq_p, dka, dkb, dva, dvb = blocks[5 * pi:5 * pi + 5]
            live = i + d < n_blk
            dq_t = dq_t + dq_p
            dk_t = dk_t + dka + jnp.where(live, dkb, 0.0)
            dv_t = dv_t + dva + jnp.where(live, dvb, 0.0)
        return _rope_transposed(dq_t, cc, a1, a2), _rope_transposed(dk_t, cc, a1, a2), dv_t

    dq, dk, dv = _rowwise("attn_bwd_sum", s_len, BAND, ins, [_row_out(s_len, D_ATTN, BF16, BAND)] * 3, qkv_fn)
    dproj = jnp.concatenate([dq, dk, dv, du], axis=1)

    (wg["w_in"],) = _mm(
        "w_in_bwd_w", (1, 2, nt), (sv["a_in"], (TM, D_MODEL), lambda i, j, k: (k, 0)),
        (dproj, (TM, D_MODEL), lambda i, j, k: (k, j)), "tn",
        [((D_MODEL, 2 * D_MODEL), BF16, (D_MODEL, D_MODEL), lambda i, j, k: (0, j), "row")],
        acc_shape=(D_MODEL, D_MODEL))

    def in_epi(acc, dh, h, g_pre, f, g_post):
        dx, dg_pre = _vjp(_rms, (h, g_pre), acc)
        dh1 = dh + dx
        df, dg_post = _vjp(_rms, (f, g_post), 0.5 * dh1)
        return dh1, df, dg_pre, dg_post

    dh1, df1, gr["mix_pre_g"], gr["ffn1_post_g"] = _mm(
        "w_in_bwd_x", (nt, 1, 1), _rows(dproj, TM), _resident(gw["w_in"]), "nt",
        [_row_out(s_len, D_MODEL, F32, TM), _row_out(s_len, D_MODEL, BF16, TM), _col_out(D_MODEL), _col_out(D_MODEL)],
        extras=[_rows(dh2, TM), _rows(sv["h1"], TM), _whole(gains["mix_pre_g"]), _rows(sv["ffn1"]["f"], TM),
                _whole(gains["ffn1_post_g"])], epilogue=in_epi)

    def ffn1_final(dn, dh, h, g_pre):
        dx, dg_pre = _vjp(_rms, (h, g_pre), dn)
        return dh + dx, dg_pre

    wg["ffn1_w_gu"], wg["ffn1_w_down"], (dh0, gr["ffn1_pre_g"]) = _ffn_bwd(
        s_len, sv["ffn1"], df1, gw["w_gu1"], gw["w_d1"], ffn1_final,
        [_rows(dh1, FFN_TM), _rows(sv["ffn1"]["h"], FFN_TM), _whole(gains["ffn1_pre_g"])],
        [_row_out(s_len, D_MODEL, F32, FFN_TM), _col_out(D_MODEL)])

    return dh0, wg, gr, (d_a, d_w_b, d_w_c)


def _peers():
    x, y, c = lax.axis_index("x"), lax.axis_index("y"), lax.axis_index("c")
    me = 4 * x + 2 * y + c
    peers = []
    for k in range(1, N_DEV):
        kx, ky, kc = (k >> 2) & 1, (k >> 1) & 1, k & 1
        px, py, pc = x ^ kx, y ^ ky, c ^ kc
        peers.append(((px, py, pc), 4 * px + 2 * py + pc))
    return me, peers


def _exchange(name, arrays, scatter):
    n = len(arrays)

    def body(*refs):
        in_refs, out_refs = refs[:n], refs[n:2 * n]
        send_sems, recv_sems, local_sems = refs[2 * n:]
        me, peers = _peers()
        copies = []
        for t in range(n):
            src_of = (lambda p, t=t: in_refs[t].at[p]) if scatter else (lambda p, t=t: in_refs[t])
            local = pltpu.make_async_copy(src_of(me), out_refs[t].at[me], local_sems.at[t])
            local.start()
            copies.append(local)
        remote = []
        for t in range(n):
            src_of = (lambda p, t=t: in_refs[t].at[p]) if scatter else (lambda p, t=t: in_refs[t])
            for k, (peer, peer_id) in enumerate(peers):
                cp = pltpu.make_async_remote_copy(
                    src_ref=src_of(peer_id), dst_ref=out_refs[t].at[me], send_sem=send_sems.at[t, k],
                    recv_sem=recv_sems.at[t, k], device_id=peer, device_id_type=pl.DeviceIdType.MESH)
                cp.start()
                remote.append((t, k, peer_id, cp))
        for t, k, peer_id, cp in remote:
            src_of = (lambda p, t=t: in_refs[t].at[p]) if scatter else (lambda p, t=t: in_refs[t])
            arrival = pltpu.make_async_remote_copy(
                src_ref=src_of(peer_id), dst_ref=out_refs[t].at[peer_id], send_sem=send_sems.at[t, k],
                recv_sem=recv_sems.at[t, k], device_id=peers[k][0], device_id_type=pl.DeviceIdType.MESH)
            arrival.wait_recv()
        for t, k, peer_id, cp in remote:
            cp.wait_send()
        for local in copies:
            local.wait()

    def out_shape(a):
        return jax.ShapeDtypeStruct(((N_DEV,) + a.shape[1:]) if scatter else ((N_DEV,) + a.shape), a.dtype)

    any_spec = pl.BlockSpec(memory_space=pl.ANY)
    return pl.pallas_call(
        body,
        name=name,
        in_specs=[any_spec] * n,
        out_specs=[any_spec] * n,
        out_shape=[out_shape(a) for a in arrays],
        scratch_shapes=[pltpu.SemaphoreType.DMA((n, N_DEV - 1)), pltpu.SemaphoreType.DMA((n, N_DEV - 1)),
                        pltpu.SemaphoreType.DMA((n,))],
    )(*arrays)


def _adam_math(g, w, m, v):
    m = ADAM_B1 * m + (1.0 - ADAM_B1) * g
    v = ADAM_B2 * v + (1.0 - ADAM_B2) * (g * g)
    m_hat = m / (1.0 - ADAM_B1 ** ADAM_STEP)
    v_hat = v / (1.0 - ADAM_B2 ** ADAM_STEP)
    delta = -ADAM_LR * (m_hat / (jnp.sqrt(v_hat) + ADAM_EPS) + ADAM_WD * w)
    return delta, m, v


def _adamw(name, recv, recv_block, recv_map, w, m, v, tr):
    rows, cols = w.shape

    def fn(r, wb, mb, vb):
        g = r[0].astype(F32)
        for s in range(1, N_DEV):
            g = g + r[s].astype(F32)
        return (g,) + _adam_math(g, wb, mb, vb)

    return _rowwise(name, rows, tr, [(recv, recv_block, recv_map), _rows(w, tr), _rows(m, tr), _rows(v, tr)],
                    [_row_out(rows, cols, F32, tr)] * 4, fn)


def _small_sizes(shapes):
    return [(n, math.prod(shapes[n][1:])) for n in SMALL]


def _pack_small(tensors, depth):
    flat = jnp.concatenate([tensors[n].reshape(depth, -1).astype(F32) for n in SMALL], axis=1)
    used = flat.shape[1]
    padded = -(-used // SMALL_PAD) * SMALL_PAD
    return jnp.pad(flat, ((0, 0), (0, padded - used)))


def _local_step(x, p, positions, loss_target, w, gathered):
    s_len = x.shape[0]
    depth = p.shape[0]
    rope = _rope_tables(positions)
    gains = [{n: w[n][l].reshape(1, -1) for n in SMALL if w[n].ndim == 2 and n != "ssm_log_dt"} for l in range(depth)]
    ssm_args = lambda l: tuple(w[n][l] for n in ("ssm_lam_re", "ssm_lam_im", "ssm_log_dt", "ssm_b_re", "ssm_b_im",
                                                  "ssm_c_re", "ssm_c_im"))
    sps, pulls = [], []
    for l in range(depth):
        (a_cat, w_b, w_c), pull = jax.vjp(_ssm_params, *ssm_args(l))
        sps.append((a_cat, w_b.astype(BF16), w_c.astype(BF16), w["ssm_d"][l].reshape(1, D_SSM)))
        pulls.append(pull)

    (n,) = _rowwise("pre_norm", s_len, TM, [_rows(x, TM), _whole(gains[0]["ffn1_pre_g"])],
                    [_row_out(s_len, D_MODEL, BF16, TM)], lambda a, g: (_rms(a, g),))
    h = x
    saved = []
    for l in range(depth):
        g_next = gains[l + 1]["ffn1_pre_g"] if l + 1 < depth else gains[l]["ffn1_pre_g"]
        h, n, sv = _layer_fwd(s_len, h, n, p[l], rope, gathered[l], sps[l], gains[l], g_next)
        saved.append(sv)

    def loss_fn(y, t):
        e = y - t
        return e * (1.0 / D_MODEL), jnp.sum(e * e, axis=0, keepdims=True)

    dh, sq = _rowwise("loss", s_len, TM, [_rows(h, TM), _rows(loss_target, TM)],
                      [_row_out(s_len, D_MODEL, F32, TM), _col_out(D_MODEL)], loss_fn)
    loss = 0.5 * jnp.sum(sq) / D_MODEL

    w_grads, small_grads = [None] * depth, [None] * depth
    for l in reversed(range(depth)):
        dh, wg, gr, (d_a, d_w_b, d_w_c) = _layer_bwd(s_len, dh, saved[l], rope, gathered[l], sps[l], gains[l])
        d_lre, d_lim, d_dt, d_bre, d_bim, d_cre, d_cim = pulls[l]((d_a, d_w_b, d_w_c))
        gr.update(ssm_lam_re=d_lre, ssm_lam_im=d_lim, ssm_log_dt=d_dt, ssm_b_re=d_bre, ssm_b_im=d_bim,
                  ssm_c_re=d_cre, ssm_c_im=d_cim)
        w_grads[l], small_grads[l] = wg, gr
    return loss, dh, w_grads, small_grads


def _shard_groups(w, l):
    cast = lambda n: w[n][l].astype(BF16)
    return [jnp.stack([cast("ffn1_w_gate"), cast("ffn1_w_up"), cast("ffn2_w_gate"), cast("ffn2_w_up")]),
            cast("ffn1_w_down"), cast("ffn2_w_down"), cast("w_out"), cast("ple_w_gate"),
            cast("w_in"), cast("ple_w_up"), cast("ssm_w_glu")]


def _gathered_views(groups):
    g_a, g_d1, g_d2, g_o, g_p, g_in, g_pu, g_glu = groups
    cols = lambda t: t.transpose(1, 0, 2).reshape(t.shape[1], N_DEV * t.shape[2])
    rows = lambda t: t.reshape(N_DEV * t.shape[1], t.shape[2])
    return dict(w_gu1=jnp.concatenate([cols(g_a[:, 0]), cols(g_a[:, 1])], axis=1),
                w_gu2=jnp.concatenate([cols(g_a[:, 2]), cols(g_a[:, 3])], axis=1),
                w_d1=rows(g_d1), w_d2=rows(g_d2), w_out=rows(g_o), ple_w_gate=rows(g_p),
                w_in=cols(g_in), w_pu=cols(g_pu), w_glu=rows(g_glu))


def _grad_groups(wg):
    rows = lambda t: t.reshape(N_DEV, t.shape[0] // N_DEV, t.shape[1])
    cols = lambda t: t.reshape(t.shape[0], N_DEV, t.shape[1] // N_DEV).transpose(1, 0, 2)
    gu1, gu2 = wg["ffn1_w_gu"], wg["ffn2_w_gu"]
    return [jnp.stack([cols(gu1[:, :D_FF]), cols(gu1[:, D_FF:]), cols(gu2[:, :D_FF]), cols(gu2[:, D_FF:])], axis=1),
            rows(wg["ffn1_w_down"]), rows(wg["ffn2_w_down"]), rows(wg["w_out"]), rows(wg["ple_w_gate"]),
            cols(wg["w_in"]), cols(wg["ple_w_up"]), rows(wg["ssm_w_glu"])]


_PLACE = {"ffn1_w_gate": (0, 0), "ffn1_w_up": (0, 1), "ffn2_w_gate": (0, 2), "ffn2_w_up": (0, 3),
          "ffn1_w_down": (1, None), "ffn2_w_down": (2, None), "w_out": (3, None), "ple_w_gate": (4, None),
          "w_in": (5, None), "ple_w_up": (6, None), "ssm_w_glu": (7, None)}


def _update_sharded(name, recv_groups, w_l, m_l, v_l):
    group, slot = _PLACE[name]
    recv = recv_groups[group]
    rows, cols = w_l.shape
    tr = rows if rows <= 256 else (rows // 2 if rows % 256 else 256)
    if slot is None:
        block, imap = (N_DEV, tr, cols), (lambda i: (0, i, 0))
    else:
        block, imap = (N_DEV, None, tr, cols), (lambda i: (0, slot, i, 0))
    return _adamw(f"adamw_{name}", recv, block, imap, w_l, m_l, v_l, tr)


def kernel(x, p, positions, ffn1_pre_g, ffn1_w_gate, ffn1_w_up, ffn1_w_down, ffn1_post_g, mix_pre_g, w_in, attn_norm_g, ssm_lam_re, ssm_lam_im, ssm_log_dt, ssm_b_re, ssm_b_im, ssm_c_re, ssm_c_im, ssm_d, ssm_w_glu, ssm_b_glu, ssm_norm_g, w_out, mix_post_g, ffn2_pre_g, ffn2_w_gate, ffn2_w_up, ffn2_w_down, ffn2_post_g, ple_w_up, ple_w_gate, ple_post_g, loss_target, m_ffn1_pre_g, m_ffn1_w_gate, m_ffn1_w_up, m_ffn1_w_down, m_ffn1_post_g, m_mix_pre_g, m_w_in, m_attn_norm_g, m_ssm_lam_re, m_ssm_lam_im, m_ssm_log_dt, m_ssm_b_re, m_ssm_b_im, m_ssm_c_re, m_ssm_c_im, m_ssm_d, m_ssm_w_glu, m_ssm_b_glu, m_ssm_norm_g, m_w_out, m_mix_post_g, m_ffn2_pre_g, m_ffn2_w_gate, m_ffn2_w_up, m_ffn2_w_down, m_ffn2_post_g, m_ple_w_up, m_ple_w_gate, m_ple_post_g, v_ffn1_pre_g, v_ffn1_w_gate, v_ffn1_w_up, v_ffn1_w_down, v_ffn1_post_g, v_mix_pre_g, v_w_in, v_attn_norm_g, v_ssm_lam_re, v_ssm_lam_im, v_ssm_log_dt, v_ssm_b_re, v_ssm_b_im, v_ssm_c_re, v_ssm_c_im, v_ssm_d, v_ssm_w_glu, v_ssm_b_glu, v_ssm_norm_g, v_w_out, v_mix_post_g, v_ffn2_pre_g, v_ffn2_w_gate, v_ffn2_w_up, v_ffn2_w_down, v_ffn2_post_g, v_ple_w_up, v_ple_w_gate, v_ple_post_g):
    args = dict(locals())
    w = {n: args[n] for n in WEIGHTS}
    mom = {n: args["m_" + n] for n in WEIGHTS}
    var = {n: args["v_" + n] for n in WEIGHTS}
    depth = p.shape[0]

    gathered = [_gathered_views(_exchange("all_gather", _shard_groups(w, l), scatter=False)) for l in range(depth)]
    loss, dx, w_grads, small_grads = _local_step(x[0], p[:, 0], positions[0], loss_target[0], w, gathered)
    loss = lax.psum(loss, MESH_AXES)

    out = {}
    for l in range(depth):
        recv = _exchange("reduce_scatter", _grad_groups(w_grads[l]), scatter=True)
        for n in SHARDED:
            out.setdefault(n, []).append(_update_sharded(n, recv, w[n][l], mom[n][l], var[n][l]))
    result = {n: [jnp.stack([out[n][l][q] for l in range(depth)]) for q in range(4)] for n in SHARDED}

    grads = {n: jnp.stack([small_grads[l][n].reshape(w[n].shape[1:]) for l in range(depth)]) for n in SMALL}
    packed = [_pack_small(t, depth) for t in (grads, w, mom, var)]
    width = packed[0].shape[1]
    as_rows = lambda t: t.reshape(depth * width // SMALL_PAD, SMALL_PAD)
    (recv_small,) = _exchange("small_exchange", [as_rows(packed[0])], scatter=False)
    n_rows = depth * width // SMALL_PAD
    small_out = _adamw("adamw_small", recv_small, (N_DEV, 8, SMALL_PAD), lambda i: (0, i, 0),
                       as_rows(packed[1]), as_rows(packed[2]), as_rows(packed[3]), 8)
    for q in range(4):
        flat = small_out[q].reshape(depth, width)
        off = 0
        for n in SMALL:
            size = math.prod(w[n].shape[1:])
            result.setdefault(n, [None] * 4)[q] = flat[:, off:off + size].reshape(w[n].shape)
            off += size

    outputs = [loss, dx[None]]
    for q in range(4):
        outputs += [result[n][q] for n in WEIGHTS]
    return tuple(outputs)
```

```python
import functools
import math

import jax
import jax.numpy as jnp
from jax import lax
from jax.experimental import pallas as pl
from jax.experimental.pallas import tpu as pltpu

F32 = jnp.float32
BF16 = jnp.bfloat16

N_DEV = 8
D_MODEL = 1024
D_FF = 2816
FF_SHARD = D_FF // N_DEV
D_ATTN = 512
D_SSM = 512
HEAD_DIM = 64
N_HEADS = 8
ROPE_DIM = 16
ROPE_THETA = 500000.0
DILATIONS = (1, 4, 16)
BAND = 128
N_GROUPS = 32
SSM_GROUP = 16
SSM_STATE = 64
N_STATE = N_GROUPS * SSM_STATE
PLE_DIM = 256
NORM_EPS = 1e-6
ADAM_LR, ADAM_B1, ADAM_B2, ADAM_EPS, ADAM_WD, ADAM_STEP = 0.001, 0.9, 0.999, 1e-08, 0.01, 10

VMEM_LIMIT_BYTES = 52 * 1024 * 1024
TM = 512
MESH_AXES = ("x", "y", "c")

WEIGHTS = ['ffn1_pre_g', 'ffn1_w_gate', 'ffn1_w_up', 'ffn1_w_down', 'ffn1_post_g', 'mix_pre_g', 'w_in', 'attn_norm_g',
           'ssm_lam_re', 'ssm_lam_im', 'ssm_log_dt', 'ssm_b_re', 'ssm_b_im', 'ssm_c_re', 'ssm_c_im', 'ssm_d',
           'ssm_w_glu', 'ssm_b_glu', 'ssm_norm_g', 'w_out', 'mix_post_g', 'ffn2_pre_g', 'ffn2_w_gate', 'ffn2_w_up',
           'ffn2_w_down', 'ffn2_post_g', 'ple_w_up', 'ple_w_gate', 'ple_post_g']
SHARDED = ['ffn1_w_gate', 'ffn1_w_up', 'ffn1_w_down', 'w_in', 'ssm_w_glu', 'w_out', 'ffn2_w_gate', 'ffn2_w_up',
           'ffn2_w_down', 'ple_w_up', 'ple_w_gate']
SMALL = [n for n in WEIGHTS if n not in SHARDED]
SMALL_PAD = 1024


def _params(n_axes):
    return pltpu.CompilerParams(dimension_semantics=("arbitrary",) * n_axes, vmem_limit_bytes=VMEM_LIMIT_BYTES)


_DIMS = {"nn": (((1,), (0,)), ((), ())), "nt": (((1,), (1,)), ((), ())), "tn": (((0,), (0,)), ((), ()))}


def _dot(a, b, mode):
    return lax.dot_general(a.astype(BF16), b.astype(BF16), _DIMS[mode], preferred_element_type=F32)


def _store(out_refs, vals, kinds, first):
    for ref, val, kind in zip(out_refs, vals, kinds):
        if kind == "row":
            ref[...] = val.astype(ref.dtype)
        else:
            @pl.when(first)
            def _(ref=ref, val=val):
                ref[...] = val.astype(ref.dtype)

            @pl.when(jnp.logical_not(first))
            def _(ref=ref, val=val):
                ref[...] += val.astype(ref.dtype)


def _mm(name, grid, a, b, mode, outs, extras=(), epilogue=None, acc_shape=None):
    nk = grid[2]
    ne, no = len(extras), len(outs)
    kinds = [o[4] for o in outs]
    assert all(k == "row" for k in kinds) or grid[1] == 1

    def body(*refs):
        a_ref, b_ref = refs[0], refs[1]
        ex_refs = refs[2:2 + ne]
        out_refs = refs[2 + ne:2 + ne + no]
        part = _dot(a_ref[...], b_ref[...], mode)
        first = pl.program_id(0) == 0

        def finish(acc):
            vals = epilogue(acc, *[r[...] for r in ex_refs]) if epilogue is not None else (acc,)
            _store(out_refs, vals, kinds, first)

        if nk == 1:
            finish(part)
        else:
            acc_ref = refs[-1]
            k = pl.program_id(2)

            @pl.when(k == 0)
            def _():
                acc_ref[...] = part

            @pl.when(k > 0)
            def _():
                acc_ref[...] += part

            @pl.when(k == nk - 1)
            def _():
                finish(acc_ref[...])

    ins = (a, b) + tuple(extras)
    res = pl.pallas_call(
        body,
        name=name,
        grid=grid,
        in_specs=[_in_spec(t) for t in ins],
        out_specs=[pl.BlockSpec(o[2], o[3]) for o in outs],
        out_shape=[jax.ShapeDtypeStruct(o[0], o[1]) for o in outs],
        scratch_shapes=[pltpu.VMEM(acc_shape, F32)] if nk > 1 else [],
        compiler_params=_params(3),
    )(*[x[0] for x in ins])
    return res


def _rowwise(name, n_rows, tm, ins, outs, fn):
    kinds = [o[4] for o in outs]
    ni = len(ins)

    def body(*refs):
        vals = fn(*[r[...] for r in refs[:ni]])
        _store(refs[ni:], vals, kinds, pl.program_id(0) == 0)

    return pl.pallas_call(
        body,
        name=name,
        grid=(n_rows // tm,),
        in_specs=[_in_spec(t) for t in ins],
        out_specs=[pl.BlockSpec(o[2], o[3]) for o in outs],
        out_shape=[jax.ShapeDtypeStruct(o[0], o[1]) for o in outs],
        compiler_params=_params(1),
    )(*[x[0] for x in ins])


def _rows(arr, tm, col=0, width=None):
    width = arr.shape[1] if width is None else width
    return (arr, (tm, width), lambda i, *_: (i, col))


def _whole(arr):
    nd = arr.ndim
    return (arr, arr.shape, lambda *_: (0,) * nd)


def _resident(arr):
    nd = arr.ndim
    return (arr, arr.shape, lambda *_: (0,) * nd, dict(pipeline_mode=pl.Buffered(1)))


def _in_spec(t):
    return pl.BlockSpec(t[1], t[2], **(t[3] if len(t) > 3 else {}))


def _row_out(n_rows, width, dtype, tm, col=0, total=None):
    return ((n_rows, width if total is None else total), dtype, (tm, width), lambda i, *_: (i, col), "row")


def _col_out(width):
    return ((1, width), F32, (1, width), lambda *_: (0, 0), "colsum")


def _rms(x, g):
    return x * lax.rsqrt(jnp.mean(x * x, axis=-1, keepdims=True) + NORM_EPS) * g


def _sigmoid(x):
    return 1.0 / (1.0 + jnp.exp(-x))


def _gelu(x):
    return 0.5 * x * (1.0 + jnp.tanh(0.7978845608028654 * (x + 0.044715 * x * x * x)))


FFN_TM = 256
FF_HALF = D_FF // 2


def _ffn_fwd(s_len, h, n, w_gu, w_d, g_post, g_next):
    def gu_epi(acc):
        gate, up = acc[:, :D_FF], acc[:, D_FF:]
        return acc, gate * _sigmoid(gate) * up

    gu, act = _mm("ffn_gate_up", (s_len // FFN_TM, 1, 1), _rows(n, FFN_TM), _resident(w_gu), "nn",
                  [_row_out(s_len, 2 * D_FF, BF16, FFN_TM), _row_out(s_len, D_FF, BF16, FFN_TM)], epilogue=gu_epi)

    def down_epi(acc, h_blk, gp, gn):
        h_new = h_blk + 0.5 * _rms(acc, gp)
        return acc, h_new, _rms(h_new, gn)

    f, h_new, n_next = _mm(
        "ffn_down", (s_len // TM, 1, 1), _rows(act, TM), _resident(w_d), "nn",
        [_row_out(s_len, D_MODEL, F32, TM), _row_out(s_len, D_MODEL, F32, TM), _row_out(s_len, D_MODEL, BF16, TM)],
        extras=[_rows(h, TM), _whole(g_post), _whole(g_next)], epilogue=down_epi)
    return h_new, n_next, dict(n=n, gu=gu, act=act, f=f, h=h)


def _ffn_bwd(s_len, saved, df, w_gu, w_d, final_epi, final_extras, final_outs):
    nt = s_len // TM

    def act_epi(acc, gu_blk):
        gate = gu_blk[:, :D_FF].astype(F32)
        up = gu_blk[:, D_FF:].astype(F32)
        sg = _sigmoid(gate)
        dgate = acc * up * sg * (1.0 + gate * (1.0 - sg))
        dup = acc * gate * sg
        return (jnp.concatenate([dgate, dup], axis=1),)

    (dgu,) = _mm("ffn_bwd_act", (s_len // FFN_TM, 1, 1), _rows(df, FFN_TM), _resident(w_d), "nt",
                 [_row_out(s_len, 2 * D_FF, BF16, FFN_TM)], extras=[_rows(saved["gu"], FFN_TM)], epilogue=act_epi)

    (d_w_d,) = _mm(
        "ffn_bwd_wdown", (2, 1, nt), (saved["act"], (TM, FF_HALF), lambda i, j, k: (k, i)),
        (df, (TM, D_MODEL), lambda i, j, k: (k, 0)), "tn",
        [((D_FF, D_MODEL), BF16, (FF_HALF, D_MODEL), lambda i, j, k: (i, 0), "row")], acc_shape=(FF_HALF, D_MODEL))

    (d_w_gu,) = _mm(
        "ffn_bwd_wgu", (1, 4, nt), (saved["n"], (TM, D_MODEL), lambda i, j, k: (k, 0)),
        (dgu, (TM, FF_HALF), lambda i, j, k: (k, j)), "tn",
        [((D_MODEL, 2 * D_FF), BF16, (D_MODEL, FF_HALF), lambda i, j, k: (0, j), "row")], acc_shape=(D_MODEL, FF_HALF))

    outs = _mm("ffn_bwd_dn", (s_len // FFN_TM, 1, 1), _rows(dgu, FFN_TM), _resident(w_gu), "nt",
               final_outs, extras=final_extras, epilogue=final_epi)
    return d_w_gu, d_w_d, outs


def _band_mask(first_block):
    qi = lax.broadcasted_iota(jnp.int32, (BAND, 2 * BAND), 0)
    kj = lax.broadcasted_iota(jnp.int32, (BAND, 2 * BAND), 1)
    ok = (kj >= qi) & (kj <= qi + BAND)
    return ok & (jnp.logical_not(first_block) | (kj >= BAND))


def _attn_views(s_len, d):
    n_str = s_len // d
    nb = n_str // BAND
    blk = (BAND, D_ATTN)
    cur = lambda r, b: (b, r)
    prev = lambda r, b: (jnp.maximum(b - 1, 0), r)
    return n_str, nb, blk, cur, prev


def _attn_fwd(s_len, q, k, v, d):
    n_str, nb, blk, cur, prev = _attn_views(s_len, d)
    view = lambda t: t.reshape(n_str, d * D_ATTN)

    def body(q_ref, kp_ref, kc_ref, vp_ref, vc_ref, o_ref, lse_ref):
        mask = _band_mask(pl.program_id(1) == 0)
        qq = q_ref[...]
        kk = jnp.concatenate([kp_ref[...], kc_ref[...]], axis=0)
        vv = jnp.concatenate([vp_ref[...], vc_ref[...]], axis=0)
        for h in range(N_HEADS):
            sl = slice(h * HEAD_DIM, (h + 1) * HEAD_DIM)
            s = _dot(qq[:, sl], kk[:, sl], "nt") * (HEAD_DIM ** -0.5)
            s = jnp.where(mask, s, -1e30)
            m = jnp.max(s, axis=-1, keepdims=True)
            e = jnp.exp(s - m)
            den = jnp.sum(e, axis=-1, keepdims=True)
            o_ref[:, sl] = _dot(e / den, vv[:, sl], "nn")
            lse_ref[:, sl] = jnp.broadcast_to(m + jnp.log(den), (BAND, HEAD_DIM))

    o, lse = pl.pallas_call(
        body,
        name=f"attn_fwd_d{d}",
        grid=(d, nb),
        in_specs=[pl.BlockSpec(blk, cur), pl.BlockSpec(blk, prev), pl.BlockSpec(blk, cur),
                  pl.BlockSpec(blk, prev), pl.BlockSpec(blk, cur)],
        out_specs=[pl.BlockSpec(blk, cur), pl.BlockSpec(blk, cur)],
        out_shape=[jax.ShapeDtypeStruct((n_str, d * D_ATTN), F32)] * 2,
        compiler_params=_params(2),
    )(view(q), view(k), view(k), view(v), view(v))
    return o.reshape(s_len, D_ATTN), lse.reshape(s_len, D_ATTN)


def _attn_bwd(s_len, q, k, v, dattn, attn, lse, d):
    n_str, nb, blk, cur, prev = _attn_views(s_len, d)
    view = lambda t: t.reshape(n_str, d * D_ATTN)

    def body(q_ref, kp_ref, kc_ref, vp_ref, vc_ref, da_ref, at_ref, lse_ref, dq_ref, dka_ref, dkb_ref, dva_ref, dvb_ref):
        mask = _band_mask(pl.program_id(1) == 0)
        qq = q_ref[...]
        kk = jnp.concatenate([kp_ref[...], kc_ref[...]], axis=0)
        vv = jnp.concatenate([vp_ref[...], vc_ref[...]], axis=0)
        da = da_ref[...]
        prod = da * at_ref[...]
        scale = HEAD_DIM ** -0.5
        for h in range(N_HEADS):
            sl = slice(h * HEAD_DIM, (h + 1) * HEAD_DIM)
            s = _dot(qq[:, sl], kk[:, sl], "nt") * scale
            p = jnp.where(mask, jnp.exp(s - lse_ref[:, h * HEAD_DIM:h * HEAD_DIM + 1]), 0.0)
            dp = _dot(da[:, sl], vv[:, sl], "nt")
            ds = p * (dp - jnp.sum(prod[:, sl], axis=-1, keepdims=True))
            dq_ref[:, sl] = _dot(ds, kk[:, sl], "nn") * scale
            dk = _dot(ds, qq[:, sl], "tn") * scale
            dv = _dot(p, da[:, sl], "tn")
            dkb_ref[:, sl] = dk[:BAND]
            dka_ref[:, sl] = dk[BAND:]
            dvb_ref[:, sl] = dv[:BAND]
            dva_ref[:, sl] = dv[BAND:]

    outs = pl.pallas_call(
        body,
        name=f"attn_bwd_d{d}",
        grid=(d, nb),
        in_specs=[pl.BlockSpec(blk, cur), pl.BlockSpec(blk, prev), pl.BlockSpec(blk, cur),
                  pl.BlockSpec(blk, prev), pl.BlockSpec(blk, cur),
                  pl.BlockSpec(blk, cur), pl.BlockSpec(blk, cur), pl.BlockSpec(blk, cur)],
        out_specs=[pl.BlockSpec(blk, cur)] * 5,
        out_shape=[jax.ShapeDtypeStruct((n_str, d * D_ATTN), F32)] * 5,
        compiler_params=_params(2),
    )(view(q), view(k), view(k), view(v), view(v), view(dattn), view(attn), view(lse))
    return [t.reshape(s_len, D_ATTN) for t in outs]


def _rope_tables(positions):
    half = ROPE_DIM // 2
    inv_freq = ROPE_THETA ** (-jnp.arange(half, dtype=F32) * (2.0 / ROPE_DIM))
    ang = positions.astype(F32)[:, None] * inv_freq
    cos, sin = jnp.cos(ang), jnp.sin(ang)
    s_len = positions.shape[0]
    one = jnp.ones((s_len, HEAD_DIM - ROPE_DIM), F32)
    zero8 = jnp.zeros((s_len, half), F32)
    zero = jnp.zeros((s_len, HEAD_DIM - ROPE_DIM), F32)
    c = jnp.concatenate([cos, cos, one], axis=1)
    s1 = jnp.concatenate([zero8, sin, zero], axis=1)
    s2 = jnp.concatenate([-sin, zero8, zero], axis=1)
    tile = lambda t: jnp.tile(t, (1, N_HEADS))
    return tile(c), tile(s1), tile(s2)


def _rope(t, c, s1, s2):
    half = ROPE_DIM // 2
    return t * c + pltpu.roll(t, half, 1) * s1 + pltpu.roll(t, D_ATTN - half, 1) * s2


def _rope_transposed(dt, c, s1, s2):
    half = ROPE_DIM // 2
    return dt * c + pltpu.roll(dt * s1, D_ATTN - half, 1) + pltpu.roll(dt * s2, half, 1)


SCAN_ROWS = 256
SCAN_LANES = 512


def _scan(name, bu, a_cat, reverse):
    s_len, width = bu.shape
    nb = s_len // SCAN_ROWS
    half = width // 2

    def body(bu_ref, a_ref, x_ref, carry_ref):
        @pl.when(pl.program_id(0) == 0)
        def _():
            carry_ref[...] = jnp.zeros_like(carry_ref)

        for c in range(half // SCAN_LANES):
            re = pl.ds(c * SCAN_LANES, SCAN_LANES)
            im = pl.ds(half + c * SCAN_LANES, SCAN_LANES)
            ar = a_ref[:, re]
            ai = a_ref[:, im]

            def step(s, state):
                xr, xi = state
                t = (SCAN_ROWS - 1 - s) if reverse else s
                row = pl.ds(t, 1)
                nr = ar * xr - ai * xi + bu_ref[row, re]
                ni = ar * xi + ai * xr + bu_ref[row, im]
                x_ref[row, re] = nr
                x_ref[row, im] = ni
                return nr, ni

            xr, xi = lax.fori_loop(0, SCAN_ROWS, step, (carry_ref[0:1, re], carry_ref[0:1, im]), unroll=8)
            carry_ref[0:1, re] = xr
            carry_ref[0:1, im] = xi

    imap = (lambda i: (nb - 1 - i, 0)) if reverse else (lambda i: (i, 0))
    return pl.pallas_call(
        body,
        name=name,
        grid=(nb,),
        in_specs=[pl.BlockSpec((SCAN_ROWS, width), imap), pl.BlockSpec((1, width), lambda i: (0, 0))],
        out_specs=pl.BlockSpec((SCAN_ROWS, width), imap),
        out_shape=jax.ShapeDtypeStruct((s_len, width), F32),
        scratch_shapes=[pltpu.VMEM((8, width), F32)],
        compiler_params=_params(1),
    )(bu, a_cat)


def _ssm_params(lam_re, lam_im, log_dt, b_re, b_im, c_re, c_im):
    dt = jnp.exp(log_dt)[:, None]
    er = jnp.exp(lam_re * dt)
    a_re = er * jnp.cos(lam_im * dt)
    a_im = er * jnp.sin(lam_im * dt)
    nr, ni = a_re - 1.0, a_im
    den = lam_re * lam_re + lam_im * lam_im
    fr = (nr * lam_re + ni * lam_im) / den
    fi = (ni * lam_re - nr * lam_im) / den
    bb_re = fr[..., None] * b_re - fi[..., None] * b_im
    bb_im = fr[..., None] * b_im + fi[..., None] * b_re
    eye = jnp.eye(N_GROUPS, dtype=F32)

    def in_mat(bb):
        t = bb.transpose(0, 2, 1)[:, :, None, :] * eye[:, None, :, None]
        return t.reshape(D_SSM, N_STATE)

    def out_mat(cc):
        t = cc.transpose(0, 2, 1)[:, :, None, :] * eye[:, None, :, None]
        return t.reshape(N_STATE, D_SSM)

    w_b = jnp.concatenate([in_mat(bb_re), in_mat(bb_im)], axis=1)
    w_c = jnp.concatenate([out_mat(c_re), -out_mat(c_im)], axis=0)
    a_cat = jnp.concatenate([a_re.reshape(1, N_STATE), a_im.reshape(1, N_STATE)], axis=1)
    return a_cat, w_b, w_c


def _conj(a_cat):
    return jnp.concatenate([a_cat[:, :N_STATE], -a_cat[:, N_STATE:]], axis=1)


def _layer_fwd(s_len, h, n1, p_l, rope, gw, sp, gains, g_next):
    nt = s_len // TM
    sv = {}
    h1, a_in, sv["ffn1"] = _ffn_fwd(s_len, h, n1, gw["w_gu1"], gw["w_d1"], gains["ffn1_post_g"], gains["mix_pre_g"])

    (proj,) = _mm("w_in", (nt, 1, 1), _rows(a_in, TM), _resident(gw["w_in"]), "nn",
                  [_row_out(s_len, 2 * D_MODEL, F32, TM)])

    c, s1, s2 = rope
    q, k, v = _rowwise(
        "rope", s_len, TM,
        [_rows(proj, TM, 0, D_ATTN), _rows(proj, TM, 1, D_ATTN), _rows(proj, TM, 2, D_ATTN),
         _rows(c, TM), _rows(s1, TM), _rows(s2, TM)],
        [_row_out(s_len, D_ATTN, BF16, TM)] * 3,
        lambda tq, tk, tv, cc, a1, a2: (_rope(tq, cc, a1, a2), _rope(tk, cc, a1, a2), tv))

    parts = []
    for d in DILATIONS:
        parts += list(_attn_fwd(s_len, q, k, v, d))

    def mix_fn(o1, l1, o2, l2, o3, l3, g):
        m = jnp.maximum(jnp.maximum(l1, l2), l3)
        e1, e2, e3 = jnp.exp(l1 - m), jnp.exp(l2 - m), jnp.exp(l3 - m)
        tot = e1 + e2 + e3
        attn = (e1 * o1 + e2 * o2 + e3 * o3) / tot
        return attn, m + jnp.log(tot), _rms(attn, g)

    attn, lse, attn_n = _rowwise(
        "attn_mix", s_len, TM, [_rows(t, TM) for t in parts] + [_whole(gains["attn_norm_g"])],
        [_row_out(s_len, D_ATTN, F32, TM), _row_out(s_len, D_ATTN, F32, TM), _row_out(s_len, D_ATTN, BF16, TM)], mix_fn)

    a_cat, w_b, w_c, d_vec = sp
    ts = SCAN_ROWS
    (bu,) = _mm("ssm_bu", (s_len // ts, 1, 1), (proj, (ts, D_SSM), lambda i, j, k: (i, 3)), _whole(w_b), "nn",
                [_row_out(s_len, 2 * N_STATE, F32, ts)])
    xs = _scan("ssm_scan", bu, a_cat, False)

    def y_epi(acc, u, dv):
        z = acc + dv * u
        return z, _gelu(z)

    z, yg = _mm("ssm_y", (s_len // ts, 1, 1), _rows(xs, ts), _whole(w_c), "nn",
                [_row_out(s_len, D_SSM, F32, ts)] * 2,
                extras=[(proj, (ts, D_SSM), lambda i, j, k: (i, 3)), _whole(d_vec)], epilogue=y_epi)

    def glu_epi(acc, y, b, g):
        t = acc + b
        ssm = y * _sigmoid(t)
        return t, ssm, _rms(ssm, g)

    t_glu, ssm, ssm_n = _mm(
        "ssm_glu", (nt, 1, 1), _rows(yg, TM), _whole(gw["w_glu"]), "nn",
        [_row_out(s_len, D_SSM, F32, TM), _row_out(s_len, D_SSM, F32, TM), _row_out(s_len, D_SSM, BF16, TM)],
        extras=[_rows(yg, TM), _whole(gains["ssm_b_glu"]), _whole(gains["ssm_norm_g"])], epilogue=glu_epi)

    mixed = jnp.concatenate([attn_n, ssm_n], axis=1)

    def out_epi(acc, h_blk, gp, gn):
        h_new = h_blk + _rms(acc, gp)
        return acc, h_new, _rms(h_new, gn)

    o, h2, n2 = _mm(
        "w_out", (nt, 1, 1), _rows(mixed, TM), _whole(gw["w_out"]), "nn",
        [_row_out(s_len, D_MODEL, F32, TM), _row_out(s_len, D_MODEL, F32, TM), _row_out(s_len, D_MODEL, BF16, TM)],
        extras=[_rows(h1, TM), _whole(gains["mix_post_g"]), _whole(gains["ffn2_pre_g"])], epilogue=out_epi)

    h3, _, sv["ffn2"] = _ffn_fwd(s_len, h2, n2, gw["w_gu2"], gw["w_d2"], gains["ffn2_post_g"], gains["ffn2_post_g"])

    (pu,) = _mm("ple_up", (nt, 1, 1), _rows(p_l, TM), _resident(gw["w_pu"]), "nn",
                [_row_out(s_len, D_MODEL, F32, TM)])

    def ple_epi(acc, pu_blk, h_blk, gp, gn):
        h_new = h_blk + _rms(pu_blk * _sigmoid(acc), gp)
        return acc, h_new, _rms(h_new, gn)

    gt, h4, n_next = _mm(
        "ple_gate", (nt, 1, 1), _rows(h3, TM), _whole(gw["ple_w_gate"]), "nn",
        [_row_out(s_len, D_MODEL, F32, TM), _row_out(s_len, D_MODEL, F32, TM), _row_out(s_len, D_MODEL, BF16, TM)],
        extras=[_rows(pu, TM), _rows(h3, TM), _whole(gains["ple_post_g"]), _whole(g_next)], epilogue=ple_epi)

    sv.update(h1=h1, a_in=a_in, proj=proj, q=q, k=k, v=v, attn=attn, lse=lse, xs=xs, z=z, yg=yg, t_glu=t_glu, ssm=ssm,
              mixed=mixed, o=o, h2=h2, h3=h3, pu=pu, gt=gt, p_l=p_l)
    return h4, n_next, sv


def _vjp(fn, args, cot):
    _, pull = jax.vjp(fn, *args)
    return pull(cot)


def _layer_bwd(s_len, dh4, sv, rope, gw, sp, gains):
    nt = s_len // TM
    gr = {}
    wg = {}

    def ple_fn(dh, pu, gt, g):
        dpu, dgt, dg = _vjp(lambda a, b, c: _rms(a * _sigmoid(b), c), (pu, gt, g), dh)
        return dpu, dgt, dg

    dpu, dgt, gr["ple_post_g"] = _rowwise(
        "ple_bwd", s_len, TM, [_rows(dh4, TM), _rows(sv["pu"], TM), _rows(sv["gt"], TM), _whole(gains["ple_post_g"])],
        [_row_out(s_len, D_MODEL, BF16, TM), _row_out(s_len, D_MODEL, BF16, TM), _col_out(D_MODEL)], ple_fn)

    def square_w_grad(name, lhs, rhs):
        half = D_MODEL // 2
        (dw,) = _mm(name, (2, 1, nt), (lhs, (TM, half), lambda i, j, k: (k, i)),
                    (rhs, (TM, D_MODEL), lambda i, j, k: (k, 0)), "tn",
                    [((D_MODEL, D_MODEL), BF16, (half, D_MODEL), lambda i, j, k: (i, 0), "row")],
                    acc_shape=(half, D_MODEL))
        return dw

    (wg["ple_w_up"],) = _mm(
        "ple_bwd_wup", (1, 1, nt), (sv["p_l"], (TM, PLE_DIM), lambda i, j, k: (k, 0)),
        (dpu, (TM, D_MODEL), lambda i, j, k: (k, 0)), "tn",
        [((PLE_DIM, D_MODEL), BF16, (PLE_DIM, D_MODEL), lambda i, j, k: (0, 0), "row")],
        acc_shape=(PLE_DIM, D_MODEL))
    wg["ple_w_gate"] = square_w_grad("ple_bwd_wgate", sv["h3"], dgt)

    def ple_dx_epi(acc, dh, f, g):
        dh3 = dh + acc
        df, dg = _vjp(_rms, (f, g), 0.5 * dh3)
        return dh3, df, dg

    dh3, df2, gr["ffn2_post_g"] = _mm(
        "ple_bwd_dx", (nt, 1, 1), _rows(dgt, TM), _whole(gw["ple_w_gate"]), "nt",
        [_row_out(s_len, D_MODEL, F32, TM), _row_out(s_len, D_MODEL, BF16, TM), _col_out(D_MODEL)],
        extras=[_rows(dh4, TM), _rows(sv["ffn2"]["f"], TM), _whole(gains["ffn2_post_g"])], epilogue=ple_dx_epi)

    def ffn2_final(dn, dh, h, g_pre, o, g_post):
        dx, dg_pre = _vjp(_rms, (h, g_pre), dn)
        dh2 = dh + dx
        do, dg_post = _vjp(_rms, (o, g_post), dh2)
        return dh2, do, dg_pre, dg_post

    wg["ffn2_w_gu"], wg["ffn2_w_down"], (dh2, do, gr["ffn2_pre_g"], gr["mix_post_g"]) = _ffn_bwd(
        s_len, sv["ffn2"], df2, gw["w_gu2"], gw["w_d2"], ffn2_final,
        [_rows(dh3, FFN_TM), _rows(sv["h2"], FFN_TM), _whole(gains["ffn2_pre_g"]), _rows(sv["o"], FFN_TM),
         _whole(gains["mix_post_g"])],
        [_row_out(s_len, D_MODEL, F32, FFN_TM), _row_out(s_len, D_MODEL, BF16, FFN_TM), _col_out(D_MODEL),
         _col_out(D_MODEL)])

    wg["w_out"] = square_w_grad("w_out_bwd_w", sv["mixed"], do)

    def mixed_epi(acc, attn, ssm, yg, t, g_a, g_s):
        dattn, dg_a = _vjp(_rms, (attn, g_a), acc[:, :D_ATTN])
        dssm, dg_s = _vjp(_rms, (ssm, g_s), acc[:, D_ATTN:])
        sg = _sigmoid(t)
        dt = dssm * yg * sg * (1.0 - sg)
        return dattn, dt, dssm * sg, dg_a, dg_s, jnp.sum(dt, axis=0, keepdims=True)

    dattn, dt_glu, dyg_dir, gr["attn_norm_g"], gr["ssm_norm_g"], gr["ssm_b_glu"] = _mm(
        "w_out_bwd_x", (nt, 1, 1), _rows(do, TM), _whole(gw["w_out"]), "nt",
        [_row_out(s_len, D_ATTN, F32, TM), _row_out(s_len, D_SSM, BF16, TM), _row_out(s_len, D_SSM, F32, TM),
         _col_out(D_ATTN), _col_out(D_SSM), _col_out(D_SSM)],
        extras=[_rows(sv["attn"], TM), _rows(sv["ssm"], TM), _rows(sv["yg"], TM), _rows(sv["t_glu"], TM),
                _whole(gains["attn_norm_g"]), _whole(gains["ssm_norm_g"])], epilogue=mixed_epi)

    a_cat, w_b, w_c, d_vec = sp
    ts = SCAN_ROWS
    u_spec = (sv["proj"], (TM, D_SSM), lambda i, *_: (i, 3))
    (wg["ssm_w_glu"],) = _mm(
        "ssm_bwd_wglu", (1, 1, nt), (sv["yg"], (TM, D_SSM), lambda i, j, k: (k, 0)),
        (dt_glu, (TM, D_SSM), lambda i, j, k: (k, 0)), "tn",
        [((D_SSM, D_SSM), BF16, (D_SSM, D_SSM), lambda i, j, k: (0, 0), "row")], acc_shape=(D_SSM, D_SSM))

    def gelu_epi(acc, dy_dir, z, u, dv):
        (dz,) = _vjp(_gelu, (z,), acc + dy_dir)
        return dz, dz * dv, jnp.sum(dz * u, axis=0, keepdims=True)

    dz, du_dir, gr["ssm_d"] = _mm(
        "ssm_bwd_glu", (nt, 1, 1), _rows(dt_glu, TM), _whole(gw["w_glu"]), "nt",
        [_row_out(s_len, D_SSM, BF16, TM), _row_out(s_len, D_SSM, F32, TM), _col_out(D_SSM)],
        extras=[_rows(dyg_dir, TM), _rows(sv["z"], TM), u_spec, _whole(d_vec)], epilogue=gelu_epi)

    (d_w_c,) = _mm(
        "ssm_bwd_wc", (2 * N_STATE // 1024, 1, nt), (sv["xs"], (TM, 1024), lambda i, j, k: (k, i)),
        (dz, (TM, D_SSM), lambda i, j, k: (k, 0)), "tn",
        [((2 * N_STATE, D_SSM), F32, (1024, D_SSM), lambda i, j, k: (i, 0), "row")], acc_shape=(1024, D_SSM))
    (dxs,) = _mm("ssm_bwd_dx", (s_len // ts, 1, 1), _rows(dz, ts), _whole(w_c), "nt",
                 [_row_out(s_len, 2 * N_STATE, F32, ts)])
    gs = _scan("ssm_scan_rev", dxs, _conj(a_cat), True)

    def da_fn(x, x_before, g):
        i = pl.program_id(0)
        rolled = pltpu.roll(x, 1, 0)
        first_row = jnp.where(i > 0, x_before[7:8, :], 0.0)
        rows = lax.broadcasted_iota(jnp.int32, x.shape, 0)
        xp = jnp.where(rows == 0, first_row, rolled)
        xr, xi = xp[:, :N_STATE], xp[:, N_STATE:]
        g_r, g_i = g[:, :N_STATE], g[:, N_STATE:]
        d_re = jnp.sum(xr * g_r + xi * g_i, axis=0, keepdims=True)
        d_im = jnp.sum(xr * g_i - xi * g_r, axis=0, keepdims=True)
        return (jnp.concatenate([d_re, d_im], axis=1),)

    (d_a,) = _rowwise(
        "ssm_bwd_da", s_len, ts,
        [_rows(sv["xs"], ts), (sv["xs"], (8, 2 * N_STATE), lambda i: (jnp.maximum(i * (ts // 8) - 1, 0), 0)),
         _rows(gs, ts)], [_col_out(2 * N_STATE)], da_fn)

    (d_w_b,) = _mm(
        "ssm_bwd_wb", (1, 2, nt), (sv["proj"], (TM, D_SSM), lambda i, j, k: (k, 3)),
        (gs, (TM, N_STATE), lambda i, j, k: (k, j)), "tn",
        [((D_SSM, 2 * N_STATE), F32, (D_SSM, N_STATE), lambda i, j, k: (0, j), "row")], acc_shape=(D_SSM, N_STATE))
    (du,) = _mm("ssm_bwd_du", (s_len // ts, 1, 1), _rows(gs, ts), _whole(w_b), "nt",
                [_row_out(s_len, D_SSM, BF16, ts)], extras=[_rows(du_dir, ts)], epilogue=lambda acc, d: (acc + d,))

    dq = dk = dv = None
    pieces = []
    for d in DILATIONS:
        pieces += _attn_bwd(s_len, sv["q"], sv["k"], sv["v"], dattn, sv["attn"], sv["lse"], d)
    n_blk = s_len // BAND
    ins = []
    for pi, d in enumerate(DILATIONS):
        dq_p, dka, dkb, dva, dvb = pieces[5 * pi:5 * pi + 5]
        ahead = lambda i, d=d: (jnp.minimum(i + d, n_blk - 1), 0)
        ins += [_rows(dq_p, BAND), _rows(dka, BAND), (dkb, (BAND, D_ATTN), ahead), _rows(dva, BAND),
                (dvb, (BAND, D_ATTN), ahead)]
    c, s1, s2 = rope
    ins += [_rows(c, BAND), _rows(s1, BAND), _rows(s2, BAND)]

    def qkv_fn(*blocks):
        i = pl.program_id(0)
        cc, a1, a2 = blocks[15:]
        dq_t = dk_t = dv_t = 0.0
        for pi, d in enumerate(DILATIONS):
            dq_p, dka, dkb, dva, dvb = blocks[5 * pi:5 * pi + 5]
            live = i + d < n_blk
            dq_t = dq_t + dq_p
            dk_t = dk_t + dka + jnp.where(live, dkb, 0.0)
            dv_t = dv_t + dva + jnp.where(live, dvb, 0.0)
        return _rope_transposed(dq_t, cc, a1, a2), _rope_transposed(dk_t, cc, a1, a2), dv_t

    dq, dk, dv = _rowwise("attn_bwd_sum", s_len, BAND, ins, [_row_out(s_len, D_ATTN, BF16, BAND)] * 3, qkv_fn)
    dproj = jnp.concatenate([dq, dk, dv, du], axis=1)

    (wg["w_in"],) = _mm(
        "w_in_bwd_w", (1, 2, nt), (sv["a_in"], (TM, D_MODEL), lambda i, j, k: (k, 0)),
        (dproj, (TM, D_MODEL), lambda i, j, k: (k, j)), "tn",
        [((D_MODEL, 2 * D_MODEL), BF16, (D_MODEL, D_MODEL), lambda i, j, k: (0, j), "row")],
        acc_shape=(D_MODEL, D_MODEL))

    def in_epi(acc, dh, h, g_pre, f, g_post):
        dx, dg_pre = _vjp(_rms, (h, g_pre), acc)
        dh1 = dh + dx
        df, dg_post = _vjp(_rms, (f, g_post), 0.5 * dh1)
        return dh1, df, dg_pre, dg_post

    dh1, df1, gr["mix_pre_g"], gr["ffn1_post_g"] = _mm(
        "w_in_bwd_x", (nt, 1, 1), _rows(dproj, TM), _resident(gw["w_in"]), "nt",
        [_row_out(s_len, D_MODEL, F32, TM), _row_out(s_len, D_MODEL, BF16, TM), _col_out(D_MODEL), _col_out(D_MODEL)],
        extras=[_rows(dh2, TM), _rows(sv["h1"], TM), _whole(gains["mix_pre_g"]), _rows(sv["ffn1"]["f"], TM),
                _whole(gains["ffn1_post_g"])], epilogue=in_epi)

    def ffn1_final(dn, dh, h, g_pre):
        dx, dg_pre = _vjp(_rms, (h, g_pre), dn)
        return dh + dx, dg_pre

    wg["ffn1_w_gu"], wg["ffn1_w_down"], (dh0, gr["ffn1_pre_g"]) = _ffn_bwd(
        s_len, sv["ffn1"], df1, gw["w_gu1"], gw["w_d1"], ffn1_final,
        [_rows(dh1, FFN_TM), _rows(sv["ffn1"]["h"], FFN_TM), _whole(gains["ffn1_pre_g"])],
        [_row_out(s_len, D_MODEL, F32, FFN_TM), _col_out(D_MODEL)])

    return dh0, wg, gr, (d_a, d_w_b, d_w_c)


def _peers():
    x, y, c = lax.axis_index("x"), lax.axis_index("y"), lax.axis_index("c")
    me = 4 * x + 2 * y + c
    peers = []
    for k in range(1, N_DEV):
        kx, ky, kc = (k >> 2) & 1, (k >> 1) & 1, k & 1
        px, py, pc = x ^ kx, y ^ ky, c ^ kc
        peers.append(((px, py, pc), 4 * px + 2 * py + pc))
    return me, peers


def _exchange(name, arrays, scatter):
    n = len(arrays)

    def body(*refs):
        in_refs, out_refs = refs[:n], refs[n:2 * n]
        send_sems, recv_sems, local_sems = refs[2 * n:]
        me, peers = _peers()
        copies = []
        for t in range(n):
            src_of = (lambda p, t=t: in_refs[t].at[p]) if scatter else (lambda p, t=t: in_refs[t])
            local = pltpu.make_async_copy(src_of(me), out_refs[t].at[me], local_sems.at[t])
            local.start()
            copies.append(local)
        remote = []
        for t in range(n):
            src_of = (lambda p, t=t: in_refs[t].at[p]) if scatter else (lambda p, t=t: in_refs[t])
            for k, (peer, peer_id) in enumerate(peers):
                cp = pltpu.make_async_remote_copy(
                    src_ref=src_of(peer_id), dst_ref=out_refs[t].at[me], send_sem=send_sems.at[t, k],
                    recv_sem=recv_sems.at[t, k], device_id=peer, device_id_type=pl.DeviceIdType.MESH)
                cp.start()
                remote.append((t, k, peer_id, cp))
        for t, k, peer_id, cp in remote:
            src_of = (lambda p, t=t: in_refs[t].at[p]) if scatter else (lambda p, t=t: in_refs[t])
            arrival = pltpu.make_async_remote_copy(
                src_ref=src_of(peer_id), dst_ref=out_refs[t].at[peer_id], send_sem=send_sems.at[t, k],
                recv_sem=recv_sems.at[t, k], device_id=peers[k][0], device_id_type=pl.DeviceIdType.MESH)
            arrival.wait_recv()
        for t, k, peer_id, cp in remote:
            cp.wait_send()
        for local in copies:
            local.wait()

    def out_shape(a):
        return jax.ShapeDtypeStruct(((N_DEV,) + a.shape[1:]) if scatter else ((N_DEV,) + a.shape), a.dtype)

    any_spec = pl.BlockSpec(memory_space=pl.ANY)
    return pl.pallas_call(
        body,
        name=name,
        in_specs=[any_spec] * n,
        out_specs=[any_spec] * n,
        out_shape=[out_shape(a) for a in arrays],
        scratch_shapes=[pltpu.SemaphoreType.DMA((n, N_DEV - 1)), pltpu.SemaphoreType.DMA((n, N_DEV - 1)),
                        pltpu.SemaphoreType.DMA((n,))],
    )(*arrays)


_HBM_SPEC = pl.BlockSpec(memory_space=pltpu.HBM)
_SEM_SPEC = pl.BlockSpec(memory_space=pltpu.SEMAPHORE)
_DATAFLOW = pltpu.SideEffectType.DATAFLOW_SIDE_EFFECTING


def _device_index():
    return 4 * lax.axis_index("x") + 2 * lax.axis_index("y") + lax.axis_index("c")


def _landing(arrays, scatter):
    me = _device_index()
    out = []
    for a in arrays:
        own = lax.dynamic_index_in_dim(a, me, 0, keepdims=True) if scatter else a[None]
        buf = lax.empty((N_DEV,) + own.shape[1:], a.dtype)
        out.append(lax.dynamic_update_slice_in_dim(buf, own, me, 0))
    return out


def _split_copies(src_refs, land_refs, send_sems, recv_sems, scatter):
    me, peers = _peers()
    pairs = []
    for t in range(len(src_refs)):
        for k, (peer, peer_id) in enumerate(peers):
            src = src_refs[t].at[peer_id] if scatter else src_refs[t]
            sem = t * (N_DEV - 1) + k
            mk = lambda slot, src=src, t=t, sem=sem, peer=peer: pltpu.make_async_remote_copy(
                src_ref=src, dst_ref=land_refs[t].at[slot], send_sem=send_sems.at[sem], recv_sem=recv_sems.at[sem],
                device_id=peer, device_id_type=pl.DeviceIdType.MESH)
            pairs.append((functools.partial(mk, me), functools.partial(mk, peer_id)))
    return pairs


def _exchange_start(name, arrays, scatter, order_after):
    n = len(arrays)
    landing = _landing(arrays, scatter)

    def body(*refs):
        src_refs, land_refs = refs[:n], refs[n:2 * n]
        send_sems, recv_sems = refs[2 * n + 1], refs[2 * n + 2]
        token_ref = refs[-1]
        for outgoing, _ in _split_copies(src_refs, land_refs, send_sems, recv_sems, scatter):
            outgoing().start()
        token_ref[...] = jnp.zeros_like(token_ref)

    sem_shape = pltpu.SemaphoreType.DMA((n * (N_DEV - 1),))
    thru = [pltpu.HBM(a.shape, a.dtype) for a in list(arrays) + landing]
    hbm = lambda t: pltpu.with_memory_space_constraint(t, pltpu.HBM)
    res = pl.pallas_call(
        body,
        name=name,
        in_specs=[_HBM_SPEC] * (2 * n) + [pl.BlockSpec(memory_space=pl.ANY)],
        out_specs=[_SEM_SPEC, _SEM_SPEC] + [_HBM_SPEC] * (2 * n) + [pl.BlockSpec(memory_space=pltpu.VMEM)],
        out_shape=[sem_shape, sem_shape] + thru + [jax.ShapeDtypeStruct((8, 128), F32)],
        input_output_aliases={i: 2 + i for i in range(2 * n)},
        compiler_params=pltpu.CompilerParams(has_side_effects=_DATAFLOW),
    )(*[hbm(t) for t in list(arrays) + landing], order_after)
    return (res[0], res[1], res[2:2 + n], res[2 + n:2 + 2 * n]), res[-1]


def _exchange_wait(name, handle, scatter, order_after):
    send_sems, recv_sems, sources, landing = handle
    n = len(sources)

    def body(*refs):
        src_refs, land_refs = refs[:n], refs[n:2 * n]
        for outgoing, arrival in _split_copies(src_refs, land_refs, refs[2 * n], refs[2 * n + 1], scatter):
            outgoing().wait_send()
            arrival().wait_recv()

    thru = [pltpu.HBM(a.shape, a.dtype) for a in list(sources) + list(landing)]
    res = pl.pallas_call(
        body,
        name=name,
        in_specs=[_HBM_SPEC] * (2 * n) + [_SEM_SPEC, _SEM_SPEC, pl.BlockSpec(memory_space=pl.ANY)],
        out_specs=[_HBM_SPEC] * (2 * n),
        out_shape=thru,
        input_output_aliases={i: i for i in range(2 * n)},
        compiler_params=pltpu.CompilerParams(has_side_effects=_DATAFLOW),
    )(*sources, *landing, send_sems, recv_sems, order_after)
    return list(res[n:])


def _adam_math(g, w, m, v):
    m = ADAM_B1 * m + (1.0 - ADAM_B1) * g
    v = ADAM_B2 * v + (1.0 - ADAM_B2) * (g * g)
    m_hat = m / (1.0 - ADAM_B1 ** ADAM_STEP)
    v_hat = v / (1.0 - ADAM_B2 ** ADAM_STEP)
    delta = -ADAM_LR * (m_hat / (jnp.sqrt(v_hat) + ADAM_EPS) + ADAM_WD * w)
    return delta, m, v


def _adamw(name, recv, recv_block, recv_map, w, m, v, tr):
    rows, cols = w.shape

    def fn(r, wb, mb, vb):
        g = r[0].astype(F32)
        for s in range(1, N_DEV):
            g = g + r[s].astype(F32)
        return (g,) + _adam_math(g, wb, mb, vb)

    return _rowwise(name, rows, tr, [(recv, recv_block, recv_map), _rows(w, tr), _rows(m, tr), _rows(v, tr)],
                    [_row_out(rows, cols, F32, tr)] * 4, fn)


def _small_sizes(shapes):
    return [(n, math.prod(shapes[n][1:])) for n in SMALL]


def _pack_small(tensors, depth):
    flat = jnp.concatenate([tensors[n].reshape(depth, -1).astype(F32) for n in SMALL], axis=1)
    used = flat.shape[1]
    padded = -(-used // SMALL_PAD) * SMALL_PAD
    return jnp.pad(flat, ((0, 0), (0, padded - used)))


def _local_step(x, p, positions, loss_target, w, layer_weights, layer_done=None):
    s_len = x.shape[0]
    depth = p.shape[0]
    rope = _rope_tables(positions)
    gains = [{n: w[n][l].reshape(1, -1) for n in SMALL if w[n].ndim == 2 and n != "ssm_log_dt"} for l in range(depth)]
    ssm_args = lambda l: tuple(w[n][l] for n in ("ssm_lam_re", "ssm_lam_im", "ssm_log_dt", "ssm_b_re", "ssm_b_im",
                                                  "ssm_c_re", "ssm_c_im"))
    sps, pulls = [], []
    for l in range(depth):
        (a_cat, w_b, w_c), pull = jax.vjp(_ssm_params, *ssm_args(l))
        sps.append((a_cat, w_b.astype(BF16), w_c.astype(BF16), w["ssm_d"][l].reshape(1, D_SSM)))
        pulls.append(pull)

    (n,) = _rowwise("pre_norm", s_len, TM, [_rows(x, TM), _whole(gains[0]["ffn1_pre_g"])],
                    [_row_out(s_len, D_MODEL, BF16, TM)], lambda a, g: (_rms(a, g),))
    h = x
    saved, gathered = [], []
    for l in range(depth):
        g_next = gains[l + 1]["ffn1_pre_g"] if l + 1 < depth else gains[l]["ffn1_pre_g"]
        gathered.append(layer_weights(l, h))
        h, n, sv = _layer_fwd(s_len, h, n, p[l], rope, gathered[l], sps[l], gains[l], g_next)
        saved.append(sv)

    def loss_fn(y, t):
        e = y - t
        return e * (1.0 / D_MODEL), jnp.sum(e * e, axis=0, keepdims=True)

    dh, sq = _rowwise("loss", s_len, TM, [_rows(h, TM), _rows(loss_target, TM)],
                      [_row_out(s_len, D_MODEL, F32, TM), _col_out(D_MODEL)], loss_fn)
    loss = 0.5 * jnp.sum(sq) / D_MODEL

    w_grads, small_grads = [None] * depth, [None] * depth
    tie = None
    for l in reversed(range(depth)):
        g_l = gains[l] if tie is None else dict(gains[l], ple_post_g=gains[l]["ple_post_g"] + tie)
        dh, wg, gr, (d_a, d_w_b, d_w_c) = _layer_bwd(s_len, dh, saved[l], rope, gathered[l], sps[l], g_l)
        d_lre, d_lim, d_dt, d_bre, d_bim, d_cre, d_cim = pulls[l]((d_a, d_w_b, d_w_c))
        gr.update(ssm_lam_re=d_lre, ssm_lam_im=d_lim, ssm_log_dt=d_dt, ssm_b_re=d_bre, ssm_b_im=d_bim,
                  ssm_c_re=d_cre, ssm_c_im=d_cim)
        w_grads[l], small_grads[l] = wg, gr
        if layer_done is not None:
            tie = layer_done(l, wg, dh)
    return loss, dh, w_grads, small_grads


def _shard_groups(w, l):
    cast = lambda n: w[n][l].astype(BF16)
    return [jnp.stack([cast("ffn1_w_gate"), cast("ffn1_w_up"), cast("ffn2_w_gate"), cast("ffn2_w_up")]),
            cast("ffn1_w_down"), cast("ffn2_w_down"), cast("w_out"), cast("ple_w_gate"),
            cast("w_in"), cast("ple_w_up"), cast("ssm_w_glu")]


def _gathered_views(groups, tie=None):
    g_a, g_d1, g_d2, g_o, g_p, g_in, g_pu, g_glu = groups
    cols = lambda t: t.transpose(1, 0, 2).reshape(t.shape[1], N_DEV * t.shape[2])
    rows = lambda t: t.reshape(N_DEV * t.shape[1], t.shape[2])
    if tie is not None:
        g_a = g_a + tie.astype(g_a.dtype)
    return dict(w_gu1=jnp.concatenate([cols(g_a[:, 0]), cols(g_a[:, 1])], axis=1),
                w_gu2=jnp.concatenate([cols(g_a[:, 2]), cols(g_a[:, 3])], axis=1),
                w_d1=rows(g_d1), w_d2=rows(g_d2), w_out=rows(g_o), ple_w_gate=rows(g_p),
                w_in=cols(g_in), w_pu=cols(g_pu), w_glu=rows(g_glu))


def _grad_groups(wg):
    rows = lambda t: t.reshape(N_DEV, t.shape[0] // N_DEV, t.shape[1])
    cols = lambda t: t.reshape(t.shape[0], N_DEV, t.shape[1] // N_DEV).transpose(1, 0, 2)
    gu1, gu2 = wg["ffn1_w_gu"], wg["ffn2_w_gu"]
    return [jnp.stack([cols(gu1[:, :D_FF]), cols(gu1[:, D_FF:]), cols(gu2[:, :D_FF]), cols(gu2[:, D_FF:])], axis=1),
            rows(wg["ffn1_w_down"]), rows(wg["ffn2_w_down"]), rows(wg["w_out"]), rows(wg["ple_w_gate"]),
            cols(wg["w_in"]), cols(wg["ple_w_up"]), rows(wg["ssm_w_glu"])]


_PLACE = {"ffn1_w_gate": (0, 0), "ffn1_w_up": (0, 1), "ffn2_w_gate": (0, 2), "ffn2_w_up": (0, 3),
          "ffn1_w_down": (1, None), "ffn2_w_down": (2, None), "w_out": (3, None), "ple_w_gate": (4, None),
          "w_in": (5, None), "ple_w_up": (6, None), "ssm_w_glu": (7, None)}


def _update_sharded(name, recv_groups, w_l, m_l, v_l):
    group, slot = _PLACE[name]
    recv = recv_groups[group]
    rows, cols = w_l.shape
    tr = rows if rows <= 256 else (rows // 2 if rows % 256 else 256)
    if slot is None:
        block, imap = (N_DEV, tr, cols), (lambda i: (0, i, 0))
    else:
        block, imap = (N_DEV, None, tr, cols), (lambda i: (0, slot, i, 0))
    return _adamw(f"adamw_{name}", recv, block, imap, w_l, m_l, v_l, tr)


def kernel(x, p, positions, ffn1_pre_g, ffn1_w_gate, ffn1_w_up, ffn1_w_down, ffn1_post_g, mix_pre_g, w_in, attn_norm_g, ssm_lam_re, ssm_lam_im, ssm_log_dt, ssm_b_re, ssm_b_im, ssm_c_re, ssm_c_im, ssm_d, ssm_w_glu, ssm_b_glu, ssm_norm_g, w_out, mix_post_g, ffn2_pre_g, ffn2_w_gate, ffn2_w_up, ffn2_w_down, ffn2_post_g, ple_w_up, ple_w_gate, ple_post_g, loss_target, m_ffn1_pre_g, m_ffn1_w_gate, m_ffn1_w_up, m_ffn1_w_down, m_ffn1_post_g, m_mix_pre_g, m_w_in, m_attn_norm_g, m_ssm_lam_re, m_ssm_lam_im, m_ssm_log_dt, m_ssm_b_re, m_ssm_b_im, m_ssm_c_re, m_ssm_c_im, m_ssm_d, m_ssm_w_glu, m_ssm_b_glu, m_ssm_norm_g, m_w_out, m_mix_post_g, m_ffn2_pre_g, m_ffn2_w_gate, m_ffn2_w_up, m_ffn2_w_down, m_ffn2_post_g, m_ple_w_up, m_ple_w_gate, m_ple_post_g, v_ffn1_pre_g, v_ffn1_w_gate, v_ffn1_w_up, v_ffn1_w_down, v_ffn1_post_g, v_mix_pre_g, v_w_in, v_attn_norm_g, v_ssm_lam_re, v_ssm_lam_im, v_ssm_log_dt, v_ssm_b_re, v_ssm_b_im, v_ssm_c_re, v_ssm_c_im, v_ssm_d, v_ssm_w_glu, v_ssm_b_glu, v_ssm_norm_g, v_w_out, v_mix_post_g, v_ffn2_pre_g, v_ffn2_w_gate, v_ffn2_w_up, v_ffn2_w_down, v_ffn2_post_g, v_ple_w_up, v_ple_w_gate, v_ple_post_g):
    args = dict(locals())
    w = {n: args[n] for n in WEIGHTS}
    mom = {n: args["m_" + n] for n in WEIGHTS}
    var = {n: args["v_" + n] for n in WEIGHTS}
    depth = p.shape[0]

    gathers, scatters, recv = {}, {}, {}

    def layer_weights(l, h):
        if l == 0:
            handle, token = _exchange_start("all_gather_start", _shard_groups(w, 0), False, h)
            groups = _exchange_wait("all_gather_wait", handle, False, token)
        else:
            groups = _exchange_wait("all_gather_wait", gathers.pop(l), False, h)
        tie = None
        if l + 1 < depth:
            gathers[l + 1], token = _exchange_start("all_gather_start", _shard_groups(w, l + 1), False, groups[0])
            tie = token[0, 0]
        return _gathered_views(groups, tie)

    def layer_done(l, wg, dh):
        if l + 1 < depth:
            recv[l + 1] = _exchange_wait("reduce_scatter_wait", scatters.pop(l + 1), True, dh)
        scatters[l], token = _exchange_start("reduce_scatter_start", _grad_groups(wg), True, dh)
        return token[0, 0]

    loss, dx, w_grads, small_grads = _local_step(x[0], p[:, 0], positions[0], loss_target[0], w, layer_weights,
                                                 layer_done)
    loss = lax.psum(loss, MESH_AXES)

    grads = {n: jnp.stack([small_grads[l][n].reshape(w[n].shape[1:]) for l in range(depth)]) for n in SMALL}
    packed = [_pack_small(t, depth) for t in (grads, w, mom, var)]
    width = packed[0].shape[1]
    as_rows = lambda t: t.reshape(depth * width // SMALL_PAD, SMALL_PAD)
    (recv_small,) = _exchange("small_exchange", [as_rows(packed[0])], scatter=False)
    small_out = _adamw("adamw_small", recv_small, (N_DEV, 8, SMALL_PAD), lambda i: (0, i, 0),
                       as_rows(packed[1]), as_rows(packed[2]), as_rows(packed[3]), 8)

    out = {}
    last = small_out[0]
    for l in reversed(range(depth)):
        if l == 0:
            recv[0] = _exchange_wait("reduce_scatter_wait", scatters.pop(0), True, last)
        for n in SHARDED:
            out.setdefault(n, {})[l] = _update_sharded(n, recv[l], w[n][l], mom[n][l], var[n][l])
            last = out[n][l][0]
    result = {n: [jnp.stack([out[n][l][q] for l in range(depth)]) for q in range(4)] for n in SHARDED}

    for q in range(4):
        flat = small_out[q].reshape(depth, width)
        off = 0
        for n in SMALL:
            size = math.prod(w[n].shape[1:])
            result.setdefault(n, [None] * 4)[q] = flat[:, off:off + size].reshape(w[n].shape)
            off += size

    outputs = [loss, dx[None]]
    for q in range(4):
        outputs += [result[n][q] for n in WEIGHTS]
    return tuple(outputs)
```

```python
import functools
import math

import jax
import jax.numpy as jnp
from jax import lax
from jax.experimental import pallas as pl
from jax.experimental.pallas import tpu as pltpu

F32 = jnp.float32
BF16 = jnp.bfloat16

N_DEV = 8
D_MODEL = 1024
D_FF = 2816
FF_SHARD = D_FF // N_DEV
D_ATTN = 512
D_SSM = 512
HEAD_DIM = 64
N_HEADS = 8
ROPE_DIM = 16
ROPE_THETA = 500000.0
DILATIONS = (1, 4, 16)
BAND = 128
N_GROUPS = 32
SSM_GROUP = 16
SSM_STATE = 64
N_STATE = N_GROUPS * SSM_STATE
PLE_DIM = 256
NORM_EPS = 1e-6
ADAM_LR, ADAM_B1, ADAM_B2, ADAM_EPS, ADAM_WD, ADAM_STEP = 0.001, 0.9, 0.999, 1e-08, 0.01, 10

VMEM_LIMIT_BYTES = 52 * 1024 * 1024
TM = 512
MESH_AXES = ("x", "y", "c")

WEIGHTS = ['ffn1_pre_g', 'ffn1_w_gate', 'ffn1_w_up', 'ffn1_w_down', 'ffn1_post_g', 'mix_pre_g', 'w_in', 'attn_norm_g',
           'ssm_lam_re', 'ssm_lam_im', 'ssm_log_dt', 'ssm_b_re', 'ssm_b_im', 'ssm_c_re', 'ssm_c_im', 'ssm_d',
           'ssm_w_glu', 'ssm_b_glu', 'ssm_norm_g', 'w_out', 'mix_post_g', 'ffn2_pre_g', 'ffn2_w_gate', 'ffn2_w_up',
           'ffn2_w_down', 'ffn2_post_g', 'ple_w_up', 'ple_w_gate', 'ple_post_g']
SHARDED = ['ffn1_w_gate', 'ffn1_w_up', 'ffn1_w_down', 'w_in', 'ssm_w_glu', 'w_out', 'ffn2_w_gate', 'ffn2_w_up',
           'ffn2_w_down', 'ple_w_up', 'ple_w_gate']
SMALL = [n for n in WEIGHTS if n not in SHARDED]
SMALL_PAD = 1024


def _params(n_axes):
    return pltpu.CompilerParams(dimension_semantics=("arbitrary",) * n_axes, vmem_limit_bytes=VMEM_LIMIT_BYTES)


_DIMS = {"nn": (((1,), (0,)), ((), ())), "nt": (((1,), (1,)), ((), ())), "tn": (((0,), (0,)), ((), ()))}


def _dot(a, b, mode):
    return lax.dot_general(a.astype(BF16), b.astype(BF16), _DIMS[mode], preferred_element_type=F32)


def _store(out_refs, vals, kinds, first):
    for ref, val, kind in zip(out_refs, vals, kinds):
        if kind == "row":
            ref[...] = val.astype(ref.dtype)
        else:
            @pl.when(first)
            def _(ref=ref, val=val):
                ref[...] = val.astype(ref.dtype)

            @pl.when(jnp.logical_not(first))
            def _(ref=ref, val=val):
                ref[...] += val.astype(ref.dtype)


def _call(name, body, grid, ins, outs, scratch, into):
    arrays = [t[0] for t in ins]
    in_specs = [_in_spec(t) for t in ins]
    aliases = {}
    if into is not None:
        arrays.append(into[0])
        in_specs.append(pl.BlockSpec(memory_space=pl.ANY))
        aliases = {len(arrays) - 1: into[1]}
    return pl.pallas_call(
        body,
        name=name,
        grid=grid,
        in_specs=in_specs,
        out_specs=[pl.BlockSpec(o[2], o[3]) for o in outs],
        out_shape=[jax.ShapeDtypeStruct(o[0], o[1]) for o in outs],
        scratch_shapes=list(scratch),
        input_output_aliases=aliases,
        compiler_params=_params(len(grid)),
    )(*arrays)


def _mm(name, grid, a, b, mode, outs, extras=(), epilogue=None, acc_shape=None, scratch=(), into=None):
    nk = grid[2]
    ne, no = len(extras), len(outs)
    kinds = [o[4] for o in outs]
    assert all(k == "row" for k in kinds) or grid[1] == 1
    n_in = 2 + ne + (into is not None)
    n_acc = int(nk > 1)

    def body(*refs):
        a_ref, b_ref = refs[0], refs[1]
        ex_refs = refs[2:2 + ne]
        out_refs = refs[n_in:n_in + no]
        scr_refs = refs[n_in + no + n_acc:]
        part = _dot(a_ref[...], b_ref[...], mode)
        first = pl.program_id(0) == 0

        def finish(acc):
            vals = epilogue(acc, *[r[...] for r in ex_refs], *scr_refs) if epilogue is not None else (acc,)
            _store(out_refs, vals, kinds, first)

        if nk == 1:
            finish(part)
        else:
            acc_ref = refs[n_in + no]
            k = pl.program_id(2)

            @pl.when(k == 0)
            def _():
                acc_ref[...] = part

            @pl.when(k > 0)
            def _():
                acc_ref[...] += part

            @pl.when(k == nk - 1)
            def _():
                finish(acc_ref[...])

    acc = [pltpu.VMEM(acc_shape, F32)] if nk > 1 else []
    return _call(name, body, grid, (a, b) + tuple(extras), outs, acc + list(scratch), into)


def _rowwise(name, n_rows, tm, ins, outs, fn, scratch=(), into=None):
    kinds = [o[4] for o in outs]
    ni, no = len(ins), len(outs)
    n_in = ni + (into is not None)

    def body(*refs):
        vals = fn(*[r[...] for r in refs[:ni]], *refs[n_in + no:])
        _store(refs[n_in:n_in + no], vals, kinds, pl.program_id(0) == 0)

    return _call(name, body, (n_rows // tm,), ins, outs, scratch, into)


def _rows(arr, tm, col=0, width=None):
    width = arr.shape[1] if width is None else width
    return (arr, (tm, width), lambda i, *_: (i, col))


def _whole(arr):
    nd = arr.ndim
    return (arr, arr.shape, lambda *_: (0,) * nd)


def _resident(arr):
    nd = arr.ndim
    return (arr, arr.shape, lambda *_: (0,) * nd, dict(pipeline_mode=pl.Buffered(1)))


def _in_spec(t):
    return pl.BlockSpec(t[1], t[2], **(t[3] if len(t) > 3 else {}))


def _row_out(n_rows, width, dtype, tm, col=0, total=None):
    return ((n_rows, width if total is None else total), dtype, (tm, width), lambda i, *_: (i, col), "row")


def _col_out(width):
    return ((1, width), F32, (1, width), lambda *_: (0, 0), "colsum")


def _rms(x, g):
    return x * lax.rsqrt(jnp.mean(x * x, axis=-1, keepdims=True) + NORM_EPS) * g


def _sigmoid(x):
    return 1.0 / (1.0 + jnp.exp(-x))


def _gelu(x):
    return 0.5 * x * (1.0 + jnp.tanh(0.7978845608028654 * (x + 0.044715 * x * x * x)))


FFN_TM = 256
FF_HALF = D_FF // 2


def _ffn_fwd(s_len, h, n, w_gu, w_d, g_post, g_next):
    def gu_epi(acc):
        gate, up = acc[:, :D_FF], acc[:, D_FF:]
        return acc, gate * _sigmoid(gate) * up

    gu, act = _mm("ffn_gate_up", (s_len // FFN_TM, 1, 1), _rows(n, FFN_TM), _resident(w_gu), "nn",
                  [_row_out(s_len, 2 * D_FF, BF16, FFN_TM), _row_out(s_len, D_FF, BF16, FFN_TM)], epilogue=gu_epi)

    def down_epi(acc, h_blk, gp, gn):
        h_new = h_blk + 0.5 * _rms(acc, gp)
        return acc, h_new, _rms(h_new, gn)

    f, h_new, n_next = _mm(
        "ffn_down", (s_len // TM, 1, 1), _rows(act, TM), _resident(w_d), "nn",
        [_row_out(s_len, D_MODEL, F32, TM), _row_out(s_len, D_MODEL, F32, TM), _row_out(s_len, D_MODEL, BF16, TM)],
        extras=[_rows(h, TM), _whole(g_post), _whole(g_next)], epilogue=down_epi)
    return h_new, n_next, dict(n=n, gu=gu, act=act, f=f, h=h)


def _ffn_bwd(s_len, saved, df, w_gu, w_d, final_epi, final_extras, final_outs):
    nt = s_len // TM

    def act_epi(acc, gu_blk):
        gate = gu_blk[:, :D_FF].astype(F32)
        up = gu_blk[:, D_FF:].astype(F32)
        sg = _sigmoid(gate)
        dgate = acc * up * sg * (1.0 + gate * (1.0 - sg))
        dup = acc * gate * sg
        return (jnp.concatenate([dgate, dup], axis=1),)

    (dgu,) = _mm("ffn_bwd_act", (s_len // FFN_TM, 1, 1), _rows(df, FFN_TM), _resident(w_d), "nt",
                 [_row_out(s_len, 2 * D_FF, BF16, FFN_TM)], extras=[_rows(saved["gu"], FFN_TM)], epilogue=act_epi)

    (d_w_d,) = _mm(
        "ffn_bwd_wdown", (2, 1, nt), (saved["act"], (TM, FF_HALF), lambda i, j, k: (k, i)),
        (df, (TM, D_MODEL), lambda i, j, k: (k, 0)), "tn",
        [((D_FF, D_MODEL), BF16, (FF_HALF, D_MODEL), lambda i, j, k: (i, 0), "row")], acc_shape=(FF_HALF, D_MODEL))

    (d_w_gu,) = _mm(
        "ffn_bwd_wgu", (1, 4, nt), (saved["n"], (TM, D_MODEL), lambda i, j, k: (k, 0)),
        (dgu, (TM, FF_HALF), lambda i, j, k: (k, j)), "tn",
        [((D_MODEL, 2 * D_FF), BF16, (D_MODEL, FF_HALF), lambda i, j, k: (0, j), "row")], acc_shape=(D_MODEL, FF_HALF))

    outs = _mm("ffn_bwd_dn", (s_len // FFN_TM, 1, 1), _rows(dgu, FFN_TM), _resident(w_gu), "nt",
               final_outs, extras=final_extras, epilogue=final_epi)
    return d_w_gu, d_w_d, outs


def _band_mask(first_block):
    qi = lax.broadcasted_iota(jnp.int32, (BAND, 2 * BAND), 0)
    kj = lax.broadcasted_iota(jnp.int32, (BAND, 2 * BAND), 1)
    ok = (kj >= qi) & (kj <= qi + BAND)
    return ok & (jnp.logical_not(first_block) | (kj >= BAND))


def _attn_views(s_len, d):
    n_str = s_len // d
    nb = n_str // BAND
    blk = (BAND, D_ATTN)
    cur = lambda r, b: (b, r)
    prev = lambda r, b: (jnp.maximum(b - 1, 0), r)
    return n_str, nb, blk, cur, prev


STREAM_LANES = 128


def _stream_scratch(tm):
    return pltpu.VMEM((D_ATTN // STREAM_LANES, tm, STREAM_LANES), F32)


def _to_streams(x, d, scr):
    if d == 1:
        return x
    tm = x.shape[0]
    chunks = D_ATTN // STREAM_LANES
    for c in range(chunks):
        scr[c, 0:tm, :] = x[:, c * STREAM_LANES:(c + 1) * STREAM_LANES]
    return jnp.concatenate([scr.at[c][pl.ds(r, tm // d, stride=d), :] for r in range(d) for c in range(chunks)], axis=1)


def _from_streams(v, d, scr):
    if d == 1:
        return v
    rows = v.shape[0]
    chunks = D_ATTN // STREAM_LANES
    for r in range(d):
        for c in range(chunks):
            lo = r * D_ATTN + c * STREAM_LANES
            scr.at[c][pl.ds(r, rows, stride=d), :] = v[:, lo:lo + STREAM_LANES]
    return jnp.concatenate([scr[c, 0:rows * d, :] for c in range(chunks)], axis=1)


def _stream_in(arr, tm, d, ahead=0, n_blocks=None):
    rows = tm // d
    step = ahead // rows
    if ahead:
        return (arr, (rows, d * D_ATTN), lambda i, *_: (jnp.minimum(i + step, n_blocks - 1), 0))
    return (arr, (rows, d * D_ATTN), lambda i, *_: (i, 0))


def _stream_out(s_len, dtype, tm, d):
    return ((s_len // d, d * D_ATTN), dtype, (tm // d, d * D_ATTN), lambda i, *_: (i, 0), "row")


def _attn_fwd(s_len, q, k, v, d):
    n_str, nb, blk, cur, prev = _attn_views(s_len, d)
    view = lambda t: t

    def body(q_ref, kp_ref, kc_ref, vp_ref, vc_ref, o_ref, lse_ref):
        mask = _band_mask(pl.program_id(1) == 0)
        qq = q_ref[...]
        kk = jnp.concatenate([kp_ref[...], kc_ref[...]], axis=0)
        vv = jnp.concatenate([vp_ref[...], vc_ref[...]], axis=0)
        for h in range(N_HEADS):
            sl = slice(h * HEAD_DIM, (h + 1) * HEAD_DIM)
            s = _dot(qq[:, sl], kk[:, sl], "nt") * (HEAD_DIM ** -0.5)
            s = jnp.where(mask, s, -1e30)
            m = jnp.max(s, axis=-1, keepdims=True)
            e = jnp.exp(s - m)
            den = jnp.sum(e, axis=-1, keepdims=True)
            o_ref[:, sl] = _dot(e / den, vv[:, sl], "nn")
            lse_ref[:, sl] = jnp.broadcast_to(m + jnp.log(den), (BAND, HEAD_DIM))

    o, lse = pl.pallas_call(
        body,
        name=f"attn_fwd_d{d}",
        grid=(d, nb),
        in_specs=[pl.BlockSpec(blk, cur), pl.BlockSpec(blk, prev), pl.BlockSpec(blk, cur),
                  pl.BlockSpec(blk, prev), pl.BlockSpec(blk, cur)],
        out_specs=[pl.BlockSpec(blk, cur), pl.BlockSpec(blk, cur)],
        out_shape=[jax.ShapeDtypeStruct((n_str, d * D_ATTN), F32)] * 2,
        compiler_params=_params(2),
    )(view(q), view(k), view(k), view(v), view(v))
    return o, lse


def _attn_bwd(s_len, q, k, v, dattn, attn, lse, d):
    n_str, nb, blk, cur, prev = _attn_views(s_len, d)
    view = lambda t: t

    def body(q_ref, kp_ref, kc_ref, vp_ref, vc_ref, da_ref, at_ref, lse_ref, dq_ref, dka_ref, dkb_ref, dva_ref, dvb_ref):
        mask = _band_mask(pl.program_id(1) == 0)
        qq = q_ref[...]
        kk = jnp.concatenate([kp_ref[...], kc_ref[...]], axis=0)
        vv = jnp.concatenate([vp_ref[...], vc_ref[...]], axis=0)
        da = da_ref[...]
        prod = da * at_ref[...]
        scale = HEAD_DIM ** -0.5
        for h in range(N_HEADS):
            sl = slice(h * HEAD_DIM, (h + 1) * HEAD_DIM)
            s = _dot(qq[:, sl], kk[:, sl], "nt") * scale
            p = jnp.where(mask, jnp.exp(s - lse_ref[:, h * HEAD_DIM:h * HEAD_DIM + 1]), 0.0)
            dp = _dot(da[:, sl], vv[:, sl], "nt")
            ds = p * (dp - jnp.sum(prod[:, sl], axis=-1, keepdims=True))
            dq_ref[:, sl] = _dot(ds, kk[:, sl], "nn") * scale
            dk = _dot(ds, qq[:, sl], "tn") * scale
            dv = _dot(p, da[:, sl], "tn")
            dkb_ref[:, sl] = dk[:BAND]
            dka_ref[:, sl] = dk[BAND:]
            dvb_ref[:, sl] = dv[:BAND]
            dva_ref[:, sl] = dv[BAND:]

    outs = pl.pallas_call(
        body,
        name=f"attn_bwd_d{d}",
        grid=(d, nb),
        in_specs=[pl.BlockSpec(blk, cur), pl.BlockSpec(blk, prev), pl.BlockSpec(blk, cur),
                  pl.BlockSpec(blk, prev), pl.BlockSpec(blk, cur),
                  pl.BlockSpec(blk, cur), pl.BlockSpec(blk, cur), pl.BlockSpec(blk, cur)],
        out_specs=[pl.BlockSpec(blk, cur)] * 5,
        out_shape=[jax.ShapeDtypeStruct((n_str, d * D_ATTN), F32)] * 5,
        compiler_params=_params(2),
    )(view(q), view(k), view(k), view(v), view(v), view(dattn), view(attn), view(lse))
    return list(outs)


def _rope_tables(positions):
    half = ROPE_DIM // 2
    inv_freq = ROPE_THETA ** (-jnp.arange(half, dtype=F32) * (2.0 / ROPE_DIM))
    ang = positions.astype(F32)[:, None] * inv_freq
    cos, sin = jnp.cos(ang), jnp.sin(ang)
    s_len = positions.shape[0]
    one = jnp.ones((s_len, HEAD_DIM - ROPE_DIM), F32)
    zero8 = jnp.zeros((s_len, half), F32)
    zero = jnp.zeros((s_len, HEAD_DIM - ROPE_DIM), F32)
    c = jnp.concatenate([cos, cos, one], axis=1)
    s1 = jnp.concatenate([zero8, sin, zero], axis=1)
    s2 = jnp.concatenate([-sin, zero8, zero], axis=1)
    tile = lambda t: jnp.tile(t, (1, N_HEADS))
    return tile(c), tile(s1), tile(s2)


def _rope(t, c, s1, s2):
    half = ROPE_DIM // 2
    return t * c + pltpu.roll(t, half, 1) * s1 + pltpu.roll(t, D_ATTN - half, 1) * s2


def _rope_transposed(dt, c, s1, s2):
    half = ROPE_DIM // 2
    return dt * c + pltpu.roll(dt * s1, D_ATTN - half, 1) + pltpu.roll(dt * s2, half, 1)


SCAN_ROWS = 256
SCAN_LANES = 512


def _scan(name, bu, a_cat, reverse):
    s_len, width = bu.shape
    nb = s_len // SCAN_ROWS
    half = width // 2

    def body(bu_ref, a_ref, x_ref, carry_ref):
        @pl.when(pl.program_id(0) == 0)
        def _():
            carry_ref[...] = jnp.zeros_like(carry_ref)

        for c in range(half // SCAN_LANES):
            re = pl.ds(c * SCAN_LANES, SCAN_LANES)
            im = pl.ds(half + c * SCAN_LANES, SCAN_LANES)
            ar = a_ref[:, re]
            ai = a_ref[:, im]

            def step(s, state):
                xr, xi = state
                t = (SCAN_ROWS - 1 - s) if reverse else s
                row = pl.ds(t, 1)
                nr = ar * xr - ai * xi + bu_ref[row, re]
                ni = ar * xi + ai * xr + bu_ref[row, im]
                x_ref[row, re] = nr
                x_ref[row, im] = ni
                return nr, ni

            xr, xi = lax.fori_loop(0, SCAN_ROWS, step, (carry_ref[0:1, re], carry_ref[0:1, im]), unroll=8)
            carry_ref[0:1, re] = xr
            carry_ref[0:1, im] = xi

    imap = (lambda i: (nb - 1 - i, 0)) if reverse else (lambda i: (i, 0))
    return pl.pallas_call(
        body,
        name=name,
        grid=(nb,),
        in_specs=[pl.BlockSpec((SCAN_ROWS, width), imap), pl.BlockSpec((1, width), lambda i: (0, 0))],
        out_specs=pl.BlockSpec((SCAN_ROWS, width), imap),
        out_shape=jax.ShapeDtypeStruct((s_len, width), F32),
        scratch_shapes=[pltpu.VMEM((8, width), F32)],
        compiler_params=_params(1),
    )(bu, a_cat)


def _ssm_params(lam_re, lam_im, log_dt, b_re, b_im, c_re, c_im):
    dt = jnp.exp(log_dt)[:, None]
    er = jnp.exp(lam_re * dt)
    a_re = er * jnp.cos(lam_im * dt)
    a_im = er * jnp.sin(lam_im * dt)
    nr, ni = a_re - 1.0, a_im
    den = lam_re * lam_re + lam_im * lam_im
    fr = (nr * lam_re + ni * lam_im) / den
    fi = (ni * lam_re - nr * lam_im) / den
    bb_re = fr[..., None] * b_re - fi[..., None] * b_im
    bb_im = fr[..., None] * b_im + fi[..., None] * b_re
    eye = jnp.eye(N_GROUPS, dtype=F32)

    def in_mat(bb):
        t = bb.transpose(0, 2, 1)[:, :, None, :] * eye[:, None, :, None]
        return t.reshape(D_SSM, N_STATE)

    def out_mat(cc):
        t = cc.transpose(0, 2, 1)[:, :, None, :] * eye[:, None, :, None]
        return t.reshape(N_STATE, D_SSM)

    w_b = jnp.concatenate([in_mat(bb_re), in_mat(bb_im)], axis=1)
    w_c = jnp.concatenate([out_mat(c_re), -out_mat(c_im)], axis=0)
    a_cat = jnp.concatenate([a_re.reshape(1, N_STATE), a_im.reshape(1, N_STATE)], axis=1)
    return a_cat, w_b, w_c


def _conj(a_cat):
    return jnp.concatenate([a_cat[:, :N_STATE], -a_cat[:, N_STATE:]], axis=1)


def _layer_fwd(s_len, h, n1, p_l, rope, weight, sp, gains, g_next):
    nt = s_len // TM
    sv = {}
    gw = {}

    def need(after, *names):
        for name in names:
            gw[name] = weight(name, after)

    need(h, "w_gu1", "w_d1")
    h1, a_in, sv["ffn1"] = _ffn_fwd(s_len, h, n1, gw["w_gu1"], gw["w_d1"], gains["ffn1_post_g"], gains["mix_pre_g"])

    need(h1, "w_in", "w_glu", "w_out")
    (proj,) = _mm("w_in", (nt, 1, 1), _rows(a_in, TM), _resident(gw["w_in"]), "nn",
                  [_row_out(s_len, 2 * D_MODEL, F32, TM)])

    c, s1, s2 = rope

    def rope_fn(tq, tk, tv, cc, a1, a2, scr):
        q, k = _rope(tq, cc, a1, a2), _rope(tk, cc, a1, a2)
        return [_to_streams(t, d, scr) for d in DILATIONS for t in (q, k, tv)]

    qkv = _rowwise(
        "rope", s_len, TM,
        [_rows(proj, TM, 0, D_ATTN), _rows(proj, TM, 1, D_ATTN), _rows(proj, TM, 2, D_ATTN),
         _rows(c, TM), _rows(s1, TM), _rows(s2, TM)],
        [_stream_out(s_len, BF16, TM, d) for d in DILATIONS for _ in range(3)], rope_fn,
        scratch=[_stream_scratch(TM)])
    qkv = {d: qkv[3 * i:3 * i + 3] for i, d in enumerate(DILATIONS)}
    parts = {d: _attn_fwd(s_len, *qkv[d], d) for d in DILATIONS}

    def mix_fn(*args):
        g, scr = args[6], args[7]
        o1, l1, o2, l2, o3, l3 = [_from_streams(args[2 * i + j], d, scr) for i, d in enumerate(DILATIONS)
                                  for j in range(2)]
        m = jnp.maximum(jnp.maximum(l1, l2), l3)
        e1, e2, e3 = jnp.exp(l1 - m), jnp.exp(l2 - m), jnp.exp(l3 - m)
        tot = e1 + e2 + e3
        attn = (e1 * o1 + e2 * o2 + e3 * o3) / tot
        lse = m + jnp.log(tot)
        return [attn, lse, _rms(attn, g)] + [_to_streams(t, d, scr) for d in DILATIONS[1:] for t in (attn, lse)]

    mix_out = _rowwise(
        "attn_mix", s_len, TM,
        [_stream_in(t, TM, d) for d in DILATIONS for t in parts[d]] + [_whole(gains["attn_norm_g"])],
        [_row_out(s_len, D_ATTN, F32, TM), _row_out(s_len, D_ATTN, F32, TM),
         _row_out(s_len, D_ATTN, BF16, TM, col=0, total=D_MODEL)]
        + [_stream_out(s_len, F32, TM, d) for d in DILATIONS[1:] for _ in range(2)], mix_fn,
        scratch=[_stream_scratch(TM)])
    attn, lse, mixed_half = mix_out[:3]
    attn_s = {1: (attn, lse), DILATIONS[1]: tuple(mix_out[3:5]), DILATIONS[2]: tuple(mix_out[5:7])}

    a_cat, w_b, w_c, d_vec = sp
    ts = SCAN_ROWS
    (bu,) = _mm("ssm_bu", (s_len // ts, 1, 1), (proj, (ts, D_SSM), lambda i, j, k: (i, 3)), _whole(w_b), "nn",
                [_row_out(s_len, 2 * N_STATE, F32, ts)])
    xs = _scan("ssm_scan", bu, a_cat, False)

    def y_epi(acc, u, dv):
        z = acc + dv * u
        return z, _gelu(z)

    z, yg = _mm("ssm_y", (s_len // ts, 1, 1), _rows(xs, ts), _whole(w_c), "nn",
                [_row_out(s_len, D_SSM, F32, ts)] * 2,
                extras=[(proj, (ts, D_SSM), lambda i, j, k: (i, 3)), _whole(d_vec)], epilogue=y_epi)

    def glu_epi(acc, y, b, g):
        t = acc + b
        ssm = y * _sigmoid(t)
        return t, ssm, _rms(ssm, g)

    t_glu, ssm, mixed = _mm(
        "ssm_glu", (nt, 1, 1), _rows(yg, TM), _whole(gw["w_glu"]), "nn",
        [_row_out(s_len, D_SSM, F32, TM), _row_out(s_len, D_SSM, F32, TM),
         _row_out(s_len, D_SSM, BF16, TM, col=1, total=D_MODEL)],
        extras=[_rows(yg, TM), _whole(gains["ssm_b_glu"]), _whole(gains["ssm_norm_g"])], epilogue=glu_epi,
        into=(mixed_half, 2))

    def out_epi(acc, h_blk, gp, gn):
        h_new = h_blk + _rms(acc, gp)
        return acc, h_new, _rms(h_new, gn)

    o, h2, n2 = _mm(
        "w_out", (nt, 1, 1), _rows(mixed, TM), _whole(gw["w_out"]), "nn",
        [_row_out(s_len, D_MODEL, F32, TM), _row_out(s_len, D_MODEL, F32, TM), _row_out(s_len, D_MODEL, BF16, TM)],
        extras=[_rows(h1, TM), _whole(gains["mix_post_g"]), _whole(gains["ffn2_pre_g"])], epilogue=out_epi)

    need(h2, "w_gu2", "w_d2", "w_pu", "ple_w_gate")
    h3, _, sv["ffn2"] = _ffn_fwd(s_len, h2, n2, gw["w_gu2"], gw["w_d2"], gains["ffn2_post_g"], gains["ffn2_post_g"])

    (pu,) = _mm("ple_up", (nt, 1, 1), _rows(p_l, TM), _resident(gw["w_pu"]), "nn",
                [_row_out(s_len, D_MODEL, F32, TM)])

    def ple_epi(acc, pu_blk, h_blk, gp, gn):
        h_new = h_blk + _rms(pu_blk * _sigmoid(acc), gp)
        return acc, h_new, _rms(h_new, gn)

    gt, h4, n_next = _mm(
        "ple_gate", (nt, 1, 1), _rows(h3, TM), _whole(gw["ple_w_gate"]), "nn",
        [_row_out(s_len, D_MODEL, F32, TM), _row_out(s_len, D_MODEL, F32, TM), _row_out(s_len, D_MODEL, BF16, TM)],
        extras=[_rows(pu, TM), _rows(h3, TM), _whole(gains["ple_post_g"]), _whole(g_next)], epilogue=ple_epi)

    sv.update(h1=h1, a_in=a_in, proj=proj, qkv=qkv, attn=attn, attn_s=attn_s, xs=xs, z=z, yg=yg, t_glu=t_glu, ssm=ssm,
              mixed=mixed, o=o, h2=h2, h3=h3, pu=pu, gt=gt, p_l=p_l, w=gw)
    return h4, n_next, sv


def _vjp(fn, args, cot):
    _, pull = jax.vjp(fn, *args)
    return pull(cot)


def _layer_bwd(s_len, dh4, sv, rope, gw, sp, gains):
    nt = s_len // TM
    gr = {}
    wg = {}

    def ple_fn(dh, pu, gt, g):
        dpu, dgt, dg = _vjp(lambda a, b, c: _rms(a * _sigmoid(b), c), (pu, gt, g), dh)
        return dpu, dgt, dg

    dpu, dgt, gr["ple_post_g"] = _rowwise(
        "ple_bwd", s_len, TM, [_rows(dh4, TM), _rows(sv["pu"], TM), _rows(sv["gt"], TM), _whole(gains["ple_post_g"])],
        [_row_out(s_len, D_MODEL, BF16, TM), _row_out(s_len, D_MODEL, BF16, TM), _col_out(D_MODEL)], ple_fn)

    def square_w_grad(name, lhs, rhs):
        half = D_MODEL // 2
        (dw,) = _mm(name, (2, 1, nt), (lhs, (TM, half), lambda i, j, k: (k, i)),
                    (rhs, (TM, D_MODEL), lambda i, j, k: (k, 0)), "tn",
                    [((D_MODEL, D_MODEL), BF16, (half, D_MODEL), lambda i, j, k: (i, 0), "row")],
                    acc_shape=(half, D_MODEL))
        return dw

    (wg["ple_w_up"],) = _mm(
        "ple_bwd_wup", (1, 1, nt), (sv["p_l"], (TM, PLE_DIM), lambda i, j, k: (k, 0)),
        (dpu, (TM, D_MODEL), lambda i, j, k: (k, 0)), "tn",
        [((PLE_DIM, D_MODEL), BF16, (PLE_DIM, D_MODEL), lambda i, j, k: (0, 0), "row")],
        acc_shape=(PLE_DIM, D_MODEL))
    wg["ple_w_gate"] = square_w_grad("ple_bwd_wgate", sv["h3"], dgt)

    def ple_dx_epi(acc, dh, f, g):
        dh3 = dh + acc
        df, dg = _vjp(_rms, (f, g), 0.5 * dh3)
        return dh3, df, dg

    dh3, df2, gr["ffn2_post_g"] = _mm(
        "ple_bwd_dx", (nt, 1, 1), _rows(dgt, TM), _whole(gw["ple_w_gate"]), "nt",
        [_row_out(s_len, D_MODEL, F32, TM), _row_out(s_len, D_MODEL, BF16, TM), _col_out(D_MODEL)],
        extras=[_rows(dh4, TM), _rows(sv["ffn2"]["f"], TM), _whole(gains["ffn2_post_g"])], epilogue=ple_dx_epi)

    def ffn2_final(dn, dh, h, g_pre, o, g_post):
        dx, dg_pre = _vjp(_rms, (h, g_pre), dn)
        dh2 = dh + dx
        do, dg_post = _vjp(_rms, (o, g_post), dh2)
        return dh2, do, dg_pre, dg_post

    wg["ffn2_w_gu"], wg["ffn2_w_down"], (dh2, do, gr["ffn2_pre_g"], gr["mix_post_g"]) = _ffn_bwd(
        s_len, sv["ffn2"], df2, gw["w_gu2"], gw["w_d2"], ffn2_final,
        [_rows(dh3, FFN_TM), _rows(sv["h2"], FFN_TM), _whole(gains["ffn2_pre_g"]), _rows(sv["o"], FFN_TM),
         _whole(gains["mix_post_g"])],
        [_row_out(s_len, D_MODEL, F32, FFN_TM), _row_out(s_len, D_MODEL, BF16, FFN_TM), _col_out(D_MODEL),
         _col_out(D_MODEL)])

    wg["w_out"] = square_w_grad("w_out_bwd_w", sv["mixed"], do)

    def mixed_epi(acc, attn, ssm, yg, t, g_a, g_s, scr):
        dattn, dg_a = _vjp(_rms, (attn, g_a), acc[:, :D_ATTN])
        dssm, dg_s = _vjp(_rms, (ssm, g_s), acc[:, D_ATTN:])
        sg = _sigmoid(t)
        dt = dssm * yg * sg * (1.0 - sg)
        return ([_to_streams(dattn, d, scr) for d in DILATIONS]
                + [dt, dssm * sg, dg_a, dg_s, jnp.sum(dt, axis=0, keepdims=True)])

    res = _mm(
        "w_out_bwd_x", (nt, 1, 1), _rows(do, TM), _whole(gw["w_out"]), "nt",
        [_stream_out(s_len, F32, TM, d) for d in DILATIONS]
        + [_row_out(s_len, D_SSM, BF16, TM), _row_out(s_len, D_SSM, F32, TM),
           _col_out(D_ATTN), _col_out(D_SSM), _col_out(D_SSM)],
        extras=[_rows(sv["attn"], TM), _rows(sv["ssm"], TM), _rows(sv["yg"], TM), _rows(sv["t_glu"], TM),
                _whole(gains["attn_norm_g"]), _whole(gains["ssm_norm_g"])], epilogue=mixed_epi,
        scratch=[_stream_scratch(TM)])
    dattn_s = dict(zip(DILATIONS, res[:3]))
    dt_glu, dyg_dir, gr["attn_norm_g"], gr["ssm_norm_g"], gr["ssm_b_glu"] = res[3:]

    a_cat, w_b, w_c, d_vec = sp
    ts = SCAN_ROWS
    u_spec = (sv["proj"], (TM, D_SSM), lambda i, *_: (i, 3))
    (wg["ssm_w_glu"],) = _mm(
        "ssm_bwd_wglu", (1, 1, nt), (sv["yg"], (TM, D_SSM), lambda i, j, k: (k, 0)),
        (dt_glu, (TM, D_SSM), lambda i, j, k: (k, 0)), "tn",
        [((D_SSM, D_SSM), BF16, (D_SSM, D_SSM), lambda i, j, k: (0, 0), "row")], acc_shape=(D_SSM, D_SSM))

    def gelu_epi(acc, dy_dir, z, u, dv):
        (dz,) = _vjp(_gelu, (z,), acc + dy_dir)
        return dz, dz * dv, jnp.sum(dz * u, axis=0, keepdims=True)

    dz, du_dir, gr["ssm_d"] = _mm(
        "ssm_bwd_glu", (nt, 1, 1), _rows(dt_glu, TM), _whole(gw["w_glu"]), "nt",
        [_row_out(s_len, D_SSM, BF16, TM), _row_out(s_len, D_SSM, F32, TM), _col_out(D_SSM)],
        extras=[_rows(dyg_dir, TM), _rows(sv["z"], TM), u_spec, _whole(d_vec)], epilogue=gelu_epi)

    (d_w_c,) = _mm(
        "ssm_bwd_wc", (2 * N_STATE // 1024, 1, nt), (sv["xs"], (TM, 1024), lambda i, j, k: (k, i)),
        (dz, (TM, D_SSM), lambda i, j, k: (k, 0)), "tn",
        [((2 * N_STATE, D_SSM), F32, (1024, D_SSM), lambda i, j, k: (i, 0), "row")], acc_shape=(1024, D_SSM))
    (dxs,) = _mm("ssm_bwd_dx", (s_len // ts, 1, 1), _rows(dz, ts), _whole(w_c), "nt",
                 [_row_out(s_len, 2 * N_STATE, F32, ts)])
    gs = _scan("ssm_scan_rev", dxs, _conj(a_cat), True)

    def da_fn(x, x_before, g):
        i = pl.program_id(0)
        rolled = pltpu.roll(x, 1, 0)
        first_row = jnp.where(i > 0, x_before[7:8, :], 0.0)
        rows = lax.broadcasted_iota(jnp.int32, x.shape, 0)
        xp = jnp.where(rows == 0, first_row, rolled)
        xr, xi = xp[:, :N_STATE], xp[:, N_STATE:]
        g_r, g_i = g[:, :N_STATE], g[:, N_STATE:]
        d_re = jnp.sum(xr * g_r + xi * g_i, axis=0, keepdims=True)
        d_im = jnp.sum(xr * g_i - xi * g_r, axis=0, keepdims=True)
        return (jnp.concatenate([d_re, d_im], axis=1),)

    (d_a,) = _rowwise(
        "ssm_bwd_da", s_len, ts,
        [_rows(sv["xs"], ts), (sv["xs"], (8, 2 * N_STATE), lambda i: (jnp.maximum(i * (ts // 8) - 1, 0), 0)),
         _rows(gs, ts)], [_col_out(2 * N_STATE)], da_fn)

    (d_w_b,) = _mm(
        "ssm_bwd_wb", (1, 2, nt), (sv["proj"], (TM, D_SSM), lambda i, j, k: (k, 3)),
        (gs, (TM, N_STATE), lambda i, j, k: (k, j)), "tn",
        [((D_SSM, 2 * N_STATE), F32, (D_SSM, N_STATE), lambda i, j, k: (0, j), "row")], acc_shape=(D_SSM, N_STATE))
    sum_tm = 2 * BAND
    n_sum = s_len // sum_tm
    ins = []
    for d in DILATIONS:
        dq_p, dka, dkb, dva, dvb = _attn_bwd(s_len, *sv["qkv"][d], dattn_s[d], *sv["attn_s"][d], d)
        if d == 1:
            nxt = lambda t: (t, (sum_tm, D_ATTN), lambda i: (jnp.minimum(i + 1, n_sum - 1), 0))
            ins += [_rows(dq_p, sum_tm), _rows(dka, sum_tm), _rows(dkb, sum_tm), nxt(dkb), _rows(dva, sum_tm),
                    _rows(dvb, sum_tm), nxt(dvb)]
        else:
            ins += [_stream_in(dq_p, sum_tm, d), _stream_in(dka, sum_tm, d), _stream_in(dkb, sum_tm, d, BAND, n_sum),
                    _stream_in(dva, sum_tm, d), _stream_in(dvb, sum_tm, d, BAND, n_sum)]
    c, s1, s2 = rope
    ins += [_rows(c, sum_tm), _rows(s1, sum_tm), _rows(s2, sum_tm)]

    def qkv_fn(*args):
        i = pl.program_id(0)
        blocks, scr = args[:-1], args[-1]
        dq_t, dka, dkb, dkb_next, dva, dvb, dvb_next = blocks[:7]
        more = i + 1 < n_sum
        ahead = lambda cur, nxt: jnp.concatenate([cur[BAND:], jnp.where(more, nxt[:BAND], 0.0)], axis=0)
        dk_t = dka + ahead(dkb, dkb_next)
        dv_t = dva + ahead(dvb, dvb_next)
        at = 7
        for d in DILATIONS[1:]:
            dq_p, dka, dkb, dva, dvb = blocks[at:at + 5]
            at += 5
            live = i + BAND // (sum_tm // d) < n_sum
            dq_t = dq_t + _from_streams(dq_p, d, scr)
            dk_t = dk_t + _from_streams(dka + jnp.where(live, dkb, 0.0), d, scr)
            dv_t = dv_t + _from_streams(dva + jnp.where(live, dvb, 0.0), d, scr)
        cc, a1, a2 = blocks[at:at + 3]
        return (jnp.concatenate([_rope_transposed(dq_t, cc, a1, a2), _rope_transposed(dk_t, cc, a1, a2), dv_t], axis=1),)

    (dqkv,) = _rowwise(
        "attn_bwd_sum", s_len, sum_tm, ins,
        [((s_len, 2 * D_MODEL), BF16, (sum_tm, 3 * D_ATTN), lambda i: (i, 0), "row")], qkv_fn,
        scratch=[_stream_scratch(sum_tm)])
    (dproj,) = _mm("ssm_bwd_du", (s_len // ts, 1, 1), _rows(gs, ts), _whole(w_b), "nt",
                   [_row_out(s_len, D_SSM, BF16, ts, col=3, total=2 * D_MODEL)], extras=[_rows(du_dir, ts)],
                   epilogue=lambda acc, d: (acc + d,), into=(dqkv, 0))

    (wg["w_in"],) = _mm(
        "w_in_bwd_w", (1, 2, nt), (sv["a_in"], (TM, D_MODEL), lambda i, j, k: (k, 0)),
        (dproj, (TM, D_MODEL), lambda i, j, k: (k, j)), "tn",
        [((D_MODEL, 2 * D_MODEL), BF16, (D_MODEL, D_MODEL), lambda i, j, k: (0, j), "row")],
        acc_shape=(D_MODEL, D_MODEL))

    def in_epi(acc, dh, h, g_pre, f, g_post):
        dx, dg_pre = _vjp(_rms, (h, g_pre), acc)
        dh1 = dh + dx
        df, dg_post = _vjp(_rms, (f, g_post), 0.5 * dh1)
        return dh1, df, dg_pre, dg_post

    dh1, df1, gr["mix_pre_g"], gr["ffn1_post_g"] = _mm(
        "w_in_bwd_x", (nt, 1, 1), _rows(dproj, TM), _resident(gw["w_in"]), "nt",
        [_row_out(s_len, D_MODEL, F32, TM), _row_out(s_len, D_MODEL, BF16, TM), _col_out(D_MODEL), _col_out(D_MODEL)],
        extras=[_rows(dh2, TM), _rows(sv["h1"], TM), _whole(gains["mix_pre_g"]), _rows(sv["ffn1"]["f"], TM),
                _whole(gains["ffn1_post_g"])], epilogue=in_epi)

    def ffn1_final(dn, dh, h, g_pre):
        dx, dg_pre = _vjp(_rms, (h, g_pre), dn)
        return dh + dx, dg_pre

    wg["ffn1_w_gu"], wg["ffn1_w_down"], (dh0, gr["ffn1_pre_g"]) = _ffn_bwd(
        s_len, sv["ffn1"], df1, gw["w_gu1"], gw["w_d1"], ffn1_final,
        [_rows(dh1, FFN_TM), _rows(sv["ffn1"]["h"], FFN_TM), _whole(gains["ffn1_pre_g"])],
        [_row_out(s_len, D_MODEL, F32, FFN_TM), _col_out(D_MODEL)])

    return dh0, wg, gr, (d_a, d_w_b, d_w_c)


def _peers():
    x, y, c = lax.axis_index("x"), lax.axis_index("y"), lax.axis_index("c")
    me = 4 * x + 2 * y + c
    peers = []
    for k in range(1, N_DEV):
        kx, ky, kc = (k >> 2) & 1, (k >> 1) & 1, k & 1
        px, py, pc = x ^ kx, y ^ ky, c ^ kc
        peers.append(((px, py, pc), 4 * px + 2 * py + pc))
    return me, peers


_HBM_SPEC = pl.BlockSpec(memory_space=pltpu.HBM)
_SEM_SPEC = pl.BlockSpec(memory_space=pltpu.SEMAPHORE)
_DATAFLOW = pltpu.SideEffectType.DATAFLOW_SIDE_EFFECTING


def _device_index():
    return 4 * lax.axis_index("x") + 2 * lax.axis_index("y") + lax.axis_index("c")


def _landing(arrays, scatter):
    me = _device_index()
    out = []
    for a in arrays:
        own = lax.dynamic_index_in_dim(a, me, 0, keepdims=True) if scatter else a[None]
        buf = lax.empty((N_DEV,) + own.shape[1:], a.dtype)
        out.append(lax.dynamic_update_slice_in_dim(buf, own, me, 0))
    return out


def _split_copies(src_refs, land_refs, send_sems, recv_sems, scatter):
    me, peers = _peers()
    pairs = []
    for t in range(len(src_refs)):
        for k, (peer, peer_id) in enumerate(peers):
            src = src_refs[t].at[peer_id] if scatter else src_refs[t]
            sem = t * (N_DEV - 1) + k
            mk = lambda slot, src=src, t=t, sem=sem, peer=peer: pltpu.make_async_remote_copy(
                src_ref=src, dst_ref=land_refs[t].at[slot], send_sem=send_sems.at[sem], recv_sem=recv_sems.at[sem],
                device_id=peer, device_id_type=pl.DeviceIdType.MESH)
            pairs.append((functools.partial(mk, me), functools.partial(mk, peer_id)))
    return pairs


def _exchange_start(name, arrays, scatter, order_after):
    n = len(arrays)
    landing = _landing(arrays, scatter)

    def body(*refs):
        src_refs, land_refs = refs[:n], refs[n:2 * n]
        send_sems, recv_sems = refs[2 * n + 1], refs[2 * n + 2]
        token_ref = refs[-1]
        for outgoing, _ in _split_copies(src_refs, land_refs, send_sems, recv_sems, scatter):
            outgoing().start()
        token_ref[...] = jnp.zeros_like(token_ref)

    sem_shape = pltpu.SemaphoreType.DMA((n * (N_DEV - 1),))
    thru = [pltpu.HBM(a.shape, a.dtype) for a in list(arrays) + landing]
    hbm = lambda t: pltpu.with_memory_space_constraint(t, pltpu.HBM)
    res = pl.pallas_call(
        body,
        name=name,
        in_specs=[_HBM_SPEC] * (2 * n) + [pl.BlockSpec(memory_space=pl.ANY)],
        out_specs=[_SEM_SPEC, _SEM_SPEC] + [_HBM_SPEC] * (2 * n) + [pl.BlockSpec(memory_space=pltpu.VMEM)],
        out_shape=[sem_shape, sem_shape] + thru + [jax.ShapeDtypeStruct((8, 128), F32)],
        input_output_aliases={i: 2 + i for i in range(2 * n)},
        compiler_params=pltpu.CompilerParams(has_side_effects=_DATAFLOW),
    )(*[hbm(t) for t in list(arrays) + landing], order_after)
    return (res[0], res[1], res[2:2 + n], res[2 + n:2 + 2 * n]), res[-1]


def _exchange_wait(name, handle, scatter, order_after):
    send_sems, recv_sems, sources, landing = handle
    n = len(sources)

    def body(*refs):
        src_refs, land_refs = refs[:n], refs[n:2 * n]
        for outgoing, arrival in _split_copies(src_refs, land_refs, refs[2 * n], refs[2 * n + 1], scatter):
            outgoing().wait_send()
            arrival().wait_recv()

    thru = [pltpu.HBM(a.shape, a.dtype) for a in list(sources) + list(landing)]
    res = pl.pallas_call(
        body,
        name=name,
        in_specs=[_HBM_SPEC] * (2 * n) + [_SEM_SPEC, _SEM_SPEC, pl.BlockSpec(memory_space=pl.ANY)],
        out_specs=[_HBM_SPEC] * (2 * n),
        out_shape=thru,
        input_output_aliases={i: i for i in range(2 * n)},
        compiler_params=pltpu.CompilerParams(has_side_effects=_DATAFLOW),
    )(*sources, *landing, send_sems, recv_sems, order_after)
    return list(res[n:])


def _adam_math(g, w, m, v):
    m = ADAM_B1 * m + (1.0 - ADAM_B1) * g
    v = ADAM_B2 * v + (1.0 - ADAM_B2) * (g * g)
    m_hat = m / (1.0 - ADAM_B1 ** ADAM_STEP)
    v_hat = v / (1.0 - ADAM_B2 ** ADAM_STEP)
    delta = -ADAM_LR * (m_hat / (jnp.sqrt(v_hat) + ADAM_EPS) + ADAM_WD * w)
    return delta, m, v


def _adamw(name, recv, recv_block, recv_map, w, m, v, tr, after=None):
    rows, cols = w.shape

    def fn(r, wb, mb, vb, *_):
        g = r[0].astype(F32)
        for s in range(1, N_DEV):
            g = g + r[s].astype(F32)
        return (g,) + _adam_math(g, wb, mb, vb)

    ins = [(recv, recv_block, recv_map), _rows(w, tr), _rows(m, tr), _rows(v, tr)]
    if after is not None:
        ins.append(_whole(after))
    return _rowwise(name, rows, tr, ins, [_row_out(rows, cols, F32, tr)] * 4, fn)


def _small_sizes(shapes):
    return [(n, math.prod(shapes[n][1:])) for n in SMALL]


def _pack_small(tensors, depth):
    flat = jnp.concatenate([tensors[n].reshape(depth, -1).astype(F32) for n in SMALL], axis=1)
    used = flat.shape[1]
    padded = -(-used // SMALL_PAD) * SMALL_PAD
    return jnp.pad(flat, ((0, 0), (0, padded - used)))


def _local_step(x, p, positions, loss_target, w, layer_weights, layer_done=None):
    s_len = x.shape[0]
    depth = p.shape[0]
    rope = _rope_tables(positions)
    gains = [{n: w[n][l].reshape(1, -1) for n in SMALL if w[n].ndim == 2 and n != "ssm_log_dt"} for l in range(depth)]
    ssm_args = lambda l: tuple(w[n][l] for n in ("ssm_lam_re", "ssm_lam_im", "ssm_log_dt", "ssm_b_re", "ssm_b_im",
                                                  "ssm_c_re", "ssm_c_im"))
    sps, pulls = [], []
    for l in range(depth):
        (a_cat, w_b, w_c), pull = jax.vjp(_ssm_params, *ssm_args(l))
        sps.append((a_cat, w_b.astype(BF16), w_c.astype(BF16), w["ssm_d"][l].reshape(1, D_SSM)))
        pulls.append(pull)

    (n,) = _rowwise("pre_norm", s_len, TM, [_rows(x, TM), _whole(gains[0]["ffn1_pre_g"])],
                    [_row_out(s_len, D_MODEL, BF16, TM)], lambda a, g: (_rms(a, g),))
    h = x
    saved = []
    for l in range(depth):
        g_next = gains[l + 1]["ffn1_pre_g"] if l + 1 < depth else gains[l]["ffn1_pre_g"]
        h, n, sv = _layer_fwd(s_len, h, n, p[l], rope, layer_weights(l, h), sps[l], gains[l], g_next)
        saved.append(sv)

    def loss_fn(y, t):
        e = y - t
        return e * (1.0 / D_MODEL), jnp.sum(e * e, axis=0, keepdims=True)

    dh, sq = _rowwise("loss", s_len, TM, [_rows(h, TM), _rows(loss_target, TM)],
                      [_row_out(s_len, D_MODEL, F32, TM), _col_out(D_MODEL)], loss_fn)
    loss = 0.5 * jnp.sum(sq) / D_MODEL

    w_grads, small_grads = [None] * depth, [None] * depth
    tie = None
    for l in reversed(range(depth)):
        g_l = gains[l] if tie is None else dict(gains[l], ple_post_g=gains[l]["ple_post_g"] + tie)
        dh, wg, gr, (d_a, d_w_b, d_w_c) = _layer_bwd(s_len, dh, saved[l], rope, saved[l]["w"], sps[l], g_l)
        d_lre, d_lim, d_dt, d_bre, d_bim, d_cre, d_cim = pulls[l]((d_a, d_w_b, d_w_c))
        gr.update(ssm_lam_re=d_lre, ssm_lam_im=d_lim, ssm_log_dt=d_dt, ssm_b_re=d_bre, ssm_b_im=d_bim,
                  ssm_c_re=d_cre, ssm_c_im=d_cim)
        w_grads[l], small_grads[l] = wg, gr
        if layer_done is not None:
            tie = layer_done(l, wg, dh)
    return loss, dh, w_grads, small_grads


GROUPS = ("gu1", "gu2", "d1", "d2", "out", "pg", "in", "pu", "glu")
FIRST_PARTS = (("gu1", "d1"), ("in", "glu", "out"), ("gu2", "d2", "pu", "pg"))
_VIEW_OF = {"gu1": "w_gu1", "gu2": "w_gu2", "d1": "w_d1", "d2": "w_d2", "out": "w_out", "pg": "ple_w_gate",
            "in": "w_in", "pu": "w_pu", "glu": "w_glu"}
_PLACE = {"ffn1_w_gate": ("gu1", 0), "ffn1_w_up": ("gu1", 1), "ffn2_w_gate": ("gu2", 0), "ffn2_w_up": ("gu2", 1),
          "ffn1_w_down": ("d1", None), "ffn2_w_down": ("d2", None), "w_out": ("out", None),
          "ple_w_gate": ("pg", None), "w_in": ("in", None), "ple_w_up": ("pu", None), "ssm_w_glu": ("glu", None)}


def _shard_groups(w, l):
    cast = lambda n: w[n][l].astype(BF16)
    return {"gu1": jnp.stack([cast("ffn1_w_gate"), cast("ffn1_w_up")]),
            "gu2": jnp.stack([cast("ffn2_w_gate"), cast("ffn2_w_up")]),
            "d1": cast("ffn1_w_down"), "d2": cast("ffn2_w_down"), "out": cast("w_out"), "pg": cast("ple_w_gate"),
            "in": cast("w_in"), "pu": cast("ple_w_up"), "glu": cast("ssm_w_glu")}


def _gathered_views(gathered, tie=None):
    cols = lambda t: t.transpose(1, 0, 2).reshape(t.shape[1], N_DEV * t.shape[2])
    rows = lambda t: t.reshape(N_DEV * t.shape[1], t.shape[2])
    views = {}
    for group, g in gathered.items():
        if tie is not None and group in ("gu1", "in", "gu2"):
            g, tie = g + tie.astype(g.dtype), None
        if group in ("gu1", "gu2"):
            views[_VIEW_OF[group]] = jnp.concatenate([cols(g[:, 0]), cols(g[:, 1])], axis=1)
        elif group in ("in", "pu"):
            views[_VIEW_OF[group]] = cols(g)
        else:
            views[_VIEW_OF[group]] = rows(g)
    assert tie is None
    return views


def _grad_groups(wg):
    rows = lambda t: t.reshape(N_DEV, t.shape[0] // N_DEV, t.shape[1])
    cols = lambda t: t.reshape(t.shape[0], N_DEV, t.shape[1] // N_DEV).transpose(1, 0, 2)
    gu = lambda t: jnp.stack([cols(t[:, :D_FF]), cols(t[:, D_FF:])], axis=1)
    return {"gu1": gu(wg["ffn1_w_gu"]), "gu2": gu(wg["ffn2_w_gu"]), "d1": rows(wg["ffn1_w_down"]),
            "d2": rows(wg["ffn2_w_down"]), "out": rows(wg["w_out"]), "pg": rows(wg["ple_w_gate"]),
            "in": cols(wg["w_in"]), "pu": cols(wg["ple_w_up"]), "glu": rows(wg["ssm_w_glu"])}


def _update_sharded(name, recv_groups, w_l, m_l, v_l, after=None):
    group, slot = _PLACE[name]
    recv = recv_groups[group]
    rows, cols = w_l.shape
    tr = rows if rows <= 256 else (rows // 2 if rows % 256 else 256)
    if slot is None:
        block, imap = (N_DEV, tr, cols), (lambda i: (0, i, 0))
    else:
        block, imap = (N_DEV, None, tr, cols), (lambda i: (0, slot, i, 0))
    return _adamw(f"adamw_{name}", recv, block, imap, w_l, m_l, v_l, tr, after)


def kernel(x, p, positions, ffn1_pre_g, ffn1_w_gate, ffn1_w_up, ffn1_w_down, ffn1_post_g, mix_pre_g, w_in, attn_norm_g, ssm_lam_re, ssm_lam_im, ssm_log_dt, ssm_b_re, ssm_b_im, ssm_c_re, ssm_c_im, ssm_d, ssm_w_glu, ssm_b_glu, ssm_norm_g, w_out, mix_post_g, ffn2_pre_g, ffn2_w_gate, ffn2_w_up, ffn2_w_down, ffn2_post_g, ple_w_up, ple_w_gate, ple_post_g, loss_target, m_ffn1_pre_g, m_ffn1_w_gate, m_ffn1_w_up, m_ffn1_w_down, m_ffn1_post_g, m_mix_pre_g, m_w_in, m_attn_norm_g, m_ssm_lam_re, m_ssm_lam_im, m_ssm_log_dt, m_ssm_b_re, m_ssm_b_im, m_ssm_c_re, m_ssm_c_im, m_ssm_d, m_ssm_w_glu, m_ssm_b_glu, m_ssm_norm_g, m_w_out, m_mix_post_g, m_ffn2_pre_g, m_ffn2_w_gate, m_ffn2_w_up, m_ffn2_w_down, m_ffn2_post_g, m_ple_w_up, m_ple_w_gate, m_ple_post_g, v_ffn1_pre_g, v_ffn1_w_gate, v_ffn1_w_up, v_ffn1_w_down, v_ffn1_post_g, v_mix_pre_g, v_w_in, v_attn_norm_g, v_ssm_lam_re, v_ssm_lam_im, v_ssm_log_dt, v_ssm_b_re, v_ssm_b_im, v_ssm_c_re, v_ssm_c_im, v_ssm_d, v_ssm_w_glu, v_ssm_b_glu, v_ssm_norm_g, v_w_out, v_mix_post_g, v_ffn2_pre_g, v_ffn2_w_gate, v_ffn2_w_up, v_ffn2_w_down, v_ffn2_post_g, v_ple_w_up, v_ple_w_gate, v_ple_post_g):
    args = dict(locals())
    w = {n: args[n] for n in WEIGHTS}
    mom = {n: args["m_" + n] for n in WEIGHTS}
    var = {n: args["v_" + n] for n in WEIGHTS}
    depth = p.shape[0]

    gathers, scatters, recv, tokens = {}, {}, {}, {}

    def start(name, groups, names, scatter, order_after):
        handle, token = _exchange_start(name, [groups[n] for n in names], scatter, order_after)
        return (handle, names), token

    def finish(name, pending, scatter, order_after):
        handle, names = pending
        return dict(zip(names, _exchange_wait(name, handle, scatter, order_after)))

    def gather_next(l, order_after):
        if l + 1 >= depth:
            return None
        gathers[l + 1], token = start("all_gather_start", _shard_groups(w, l + 1), GROUPS, False, order_after)
        return token[0, 0]

    def layer_weights(l, h):
        views = {}
        if l > 0:
            got = finish("all_gather_wait", gathers.pop(l), False, h)
            views.update(_gathered_views(got, gather_next(l, got[GROUPS[0]])))
            return lambda name, after: views[name]

        shards = _shard_groups(w, 0)
        part_a, part_b, part_c = FIRST_PARTS
        pending, token = start("all_gather_start_a", shards, part_a, False, h)
        got = finish("all_gather_wait_a", pending, False, token)
        waiting = {}
        waiting["b"], token = start("all_gather_start_b", shards, part_b, False, got[part_a[0]])
        views.update(_gathered_views(got, token[0, 0]))

        def weight(name, after):
            if name not in views and "b" in waiting:
                got = finish("all_gather_wait_b", waiting.pop("b"), False, after)
                waiting["c"], token = start("all_gather_start_c", shards, part_c, False, got[part_b[0]])
                views.update(_gathered_views(got, token[0, 0]))
            if name not in views:
                got = finish("all_gather_wait_c", waiting.pop("c"), False, after)
                views.update(_gathered_views(got, gather_next(0, got[part_c[0]])))
            return views[name]

        return weight

    def layer_done(l, wg, dh):
        if l + 1 < depth:
            recv[l + 1] = finish("reduce_scatter_wait", scatters.pop(l + 1), True, dh)
        scatters[l], tokens[l] = start("reduce_scatter_start", _grad_groups(wg), GROUPS, True, dh)
        return tokens[l][0, 0]

    loss, dx, w_grads, small_grads = _local_step(x[0], p[:, 0], positions[0], loss_target[0], w, layer_weights,
                                                 layer_done)
    loss = lax.psum(loss, MESH_AXES)

    grads = {n: jnp.stack([small_grads[l][n].reshape(w[n].shape[1:]) for l in range(depth)]) for n in SMALL}
    packed = [_pack_small(t, depth) for t in (grads, w, mom, var)]
    width = packed[0].shape[1]
    as_rows = lambda t: t.reshape(depth * width // SMALL_PAD, SMALL_PAD)
    small_pending, after_starts = start("small_exchange_start", {"small": as_rows(packed[0])}, ("small",), False,
                                        tokens[0])

    out = {n: {} for n in SHARDED}
    last = after_starts
    for l in reversed(range(1, depth)):
        for n in SHARDED:
            out[n][l] = _update_sharded(n, recv[l], w[n][l], mom[n][l], var[n][l], after_starts)
            last = out[n][l][0]
    recv_small = finish("small_exchange_wait", small_pending, False, last)["small"]
    small_out = _adamw("adamw_small", recv_small, (N_DEV, 8, SMALL_PAD), lambda i: (0, i, 0),
                       as_rows(packed[1]), as_rows(packed[2]), as_rows(packed[3]), 8)
    recv[0] = finish("reduce_scatter_wait", scatters.pop(0), True, small_out[0])
    for n in SHARDED:
        out[n][0] = _update_sharded(n, recv[0], w[n][0], mom[n][0], var[n][0])
    result = {n: [jnp.stack([out[n][l][q] for l in range(depth)]) for q in range(4)] for n in SHARDED}

    for q in range(4):
        flat = small_out[q].reshape(depth, width)
        off = 0
        for n in SMALL:
            size = math.prod(w[n].shape[1:])
            result.setdefault(n, [None] * 4)[q] = flat[:, off:off + size].reshape(w[n].shape)
            off += size

    outputs = [loss, dx[None]]
    for q in range(4):
        outputs += [result[n][q] for n in WEIGHTS]
    return tuple(outputs)
```

```python
import functools
import math

import jax
import jax.numpy as jnp
from jax import lax
from jax.experimental import pallas as pl
from jax.experimental.pallas import tpu as pltpu

F32 = jnp.float32
BF16 = jnp.bfloat16

N_DEV = 8
D_MODEL = 1024
D_FF = 2816
FF_SHARD = D_FF // N_DEV
D_ATTN = 512
D_SSM = 512
HEAD_DIM = 64
N_HEADS = 8
ROPE_DIM = 16
ROPE_THETA = 500000.0
DILATIONS = (1, 4, 16)
BAND = 128
N_GROUPS = 32
SSM_GROUP = 16
SSM_STATE = 64
N_STATE = N_GROUPS * SSM_STATE
PLE_DIM = 256
NORM_EPS = 1e-6
ADAM_LR, ADAM_B1, ADAM_B2, ADAM_EPS, ADAM_WD, ADAM_STEP = 0.001, 0.9, 0.999, 1e-08, 0.01, 10

VMEM_LIMIT_BYTES = 52 * 1024 * 1024
TM = 512
MESH_AXES = ("x", "y", "c")

WEIGHTS = ['ffn1_pre_g', 'ffn1_w_gate', 'ffn1_w_up', 'ffn1_w_down', 'ffn1_post_g', 'mix_pre_g', 'w_in', 'attn_norm_g',
           'ssm_lam_re', 'ssm_lam_im', 'ssm_log_dt', 'ssm_b_re', 'ssm_b_im', 'ssm_c_re', 'ssm_c_im', 'ssm_d',
           'ssm_w_glu', 'ssm_b_glu', 'ssm_norm_g', 'w_out', 'mix_post_g', 'ffn2_pre_g', 'ffn2_w_gate', 'ffn2_w_up',
           'ffn2_w_down', 'ffn2_post_g', 'ple_w_up', 'ple_w_gate', 'ple_post_g']
SHARDED = ['ffn1_w_gate', 'ffn1_w_up', 'ffn1_w_down', 'w_in', 'ssm_w_glu', 'w_out', 'ffn2_w_gate', 'ffn2_w_up',
           'ffn2_w_down', 'ple_w_up', 'ple_w_gate']
SMALL = [n for n in WEIGHTS if n not in SHARDED]
SMALL_LANES = 1024
SMALL_PAD = 8 * SMALL_LANES


def _params(n_axes):
    return pltpu.CompilerParams(dimension_semantics=("arbitrary",) * n_axes, vmem_limit_bytes=VMEM_LIMIT_BYTES)


_DIMS = {"nn": (((1,), (0,)), ((), ())), "nt": (((1,), (1,)), ((), ())), "tn": (((0,), (0,)), ((), ()))}


def _dot(a, b, mode):
    return lax.dot_general(a.astype(BF16), b.astype(BF16), _DIMS[mode], preferred_element_type=F32)


def _store(out_refs, vals, kinds, first):
    for ref, val, kind in zip(out_refs, vals, kinds):
        if kind == "row":
            ref[...] = val.astype(ref.dtype)
        else:
            @pl.when(first)
            def _(ref=ref, val=val):
                ref[...] = val.astype(ref.dtype)

            @pl.when(jnp.logical_not(first))
            def _(ref=ref, val=val):
                ref[...] += val.astype(ref.dtype)


def _call(name, body, grid, ins, outs, scratch, into):
    arrays = [t[0] for t in ins]
    in_specs = [_in_spec(t) for t in ins]
    aliases = {}
    for arr, k in into:
        arrays.append(arr)
        in_specs.append(pl.BlockSpec(memory_space=pl.ANY))
        aliases[len(arrays) - 1] = k
    return pl.pallas_call(
        body,
        name=name,
        grid=grid,
        in_specs=in_specs,
        out_specs=[pl.BlockSpec(o[2], o[3]) for o in outs],
        out_shape=[jax.ShapeDtypeStruct(o[0], o[1]) for o in outs],
        scratch_shapes=list(scratch),
        input_output_aliases=aliases,
        compiler_params=_params(len(grid)),
    )(*arrays)


def _mm(name, grid, a, b, mode, outs, extras=(), epilogue=None, acc_shape=None, scratch=(), into=()):
    nk = grid[2]
    ne, no = len(extras), len(outs)
    kinds = [o[4] for o in outs]
    assert all(k == "row" for k in kinds) or grid[1] == 1
    n_in = 2 + ne + len(into)
    n_acc = int(nk > 1)

    def body(*refs):
        a_ref, b_ref = refs[0], refs[1]
        ex_refs = refs[2:2 + ne]
        out_refs = refs[n_in:n_in + no]
        scr_refs = refs[n_in + no + n_acc:]
        part = _dot(a_ref[...], b_ref[...], mode)
        first = pl.program_id(0) == 0

        def finish(acc):
            vals = epilogue(acc, *[r[...] for r in ex_refs], *scr_refs) if epilogue is not None else (acc,)
            _store(out_refs, vals, kinds, first)

        if nk == 1:
            finish(part)
        else:
            acc_ref = refs[n_in + no]
            k = pl.program_id(2)

            @pl.when(k == 0)
            def _():
                acc_ref[...] = part

            @pl.when(k > 0)
            def _():
                acc_ref[...] += part

            @pl.when(k == nk - 1)
            def _():
                finish(acc_ref[...])

    acc = [pltpu.VMEM(acc_shape, F32)] if nk > 1 else []
    return _call(name, body, grid, (a, b) + tuple(extras), outs, acc + list(scratch), into)


def _rowwise(name, n_rows, tm, ins, outs, fn, scratch=(), into=()):
    kinds = [o[4] for o in outs]
    ni, no = len(ins), len(outs)
    n_in = ni + len(into)

    def body(*refs):
        vals = fn(*[r[...] for r in refs[:ni]], *refs[n_in + no:])
        _store(refs[n_in:n_in + no], vals, kinds, pl.program_id(0) == 0)

    return _call(name, body, (n_rows // tm,), ins, outs, scratch, into)


def _rows(arr, tm, col=0, width=None):
    width = arr.shape[1] if width is None else width
    return (arr, (tm, width), lambda i, *_: (i, col))


def _whole(arr):
    nd = arr.ndim
    return (arr, arr.shape, lambda *_: (0,) * nd)


def _resident(arr):
    nd = arr.ndim
    return (arr, arr.shape, lambda *_: (0,) * nd, dict(pipeline_mode=pl.Buffered(1)))


def _in_spec(t):
    return pl.BlockSpec(t[1], t[2], **(t[3] if len(t) > 3 else {}))


def _row_out(n_rows, width, dtype, tm, col=0, total=None):
    return ((n_rows, width if total is None else total), dtype, (tm, width), lambda i, *_: (i, col), "row")


def _col_out(width):
    return ((1, width), F32, (1, width), lambda *_: (0, 0), "colsum")


def _rms(x, g):
    return x * lax.rsqrt(jnp.mean(x * x, axis=-1, keepdims=True) + NORM_EPS) * g


def _sigmoid(x):
    return 1.0 / (1.0 + jnp.exp(-x))


def _gelu(x):
    return 0.5 * x * (1.0 + jnp.tanh(0.7978845608028654 * (x + 0.044715 * x * x * x)))


FFN_TM = 256
FF_HALF = D_FF // 2


def _ffn_fwd(s_len, h, n, w_gu, w_d, g_post, g_next):
    def gu_epi(acc):
        gate, up = acc[:, :D_FF], acc[:, D_FF:]
        return acc, gate * _sigmoid(gate) * up

    gu, act = _mm("ffn_gate_up", (s_len // FFN_TM, 1, 1), _rows(n, FFN_TM), _resident(w_gu), "nn",
                  [_row_out(s_len, 2 * D_FF, BF16, FFN_TM), _row_out(s_len, D_FF, BF16, FFN_TM)], epilogue=gu_epi)

    def down_epi(acc, h_blk, gp, gn):
        h_new = h_blk + 0.5 * _rms(acc, gp)
        return acc, h_new, _rms(h_new, gn)

    f, h_new, n_next = _mm(
        "ffn_down", (s_len // TM, 1, 1), _rows(act, TM), _resident(w_d), "nn",
        [_row_out(s_len, D_MODEL, F32, TM), _row_out(s_len, D_MODEL, F32, TM), _row_out(s_len, D_MODEL, BF16, TM)],
        extras=[_rows(h, TM), _whole(g_post), _whole(g_next)], epilogue=down_epi)
    return h_new, n_next, dict(n=n, gu=gu, act=act, f=f, h=h)


def _ffn_bwd(s_len, saved, df, w_gu, w_d, final_epi, final_extras, final_outs):
    nt = s_len // TM

    def act_epi(acc, gu_blk):
        gate = gu_blk[:, :D_FF].astype(F32)
        up = gu_blk[:, D_FF:].astype(F32)
        sg = _sigmoid(gate)
        dgate = acc * up * sg * (1.0 + gate * (1.0 - sg))
        dup = acc * gate * sg
        return (jnp.concatenate([dgate, dup], axis=1),)

    (dgu,) = _mm("ffn_bwd_act", (s_len // FFN_TM, 1, 1), _rows(df, FFN_TM), _resident(w_d), "nt",
                 [_row_out(s_len, 2 * D_FF, BF16, FFN_TM)], extras=[_rows(saved["gu"], FFN_TM)], epilogue=act_epi)

    (d_w_d,) = _mm(
        "ffn_bwd_wdown", (2, 1, nt), (saved["act"], (TM, FF_HALF), lambda i, j, k: (k, i)),
        (df, (TM, D_MODEL), lambda i, j, k: (k, 0)), "tn",
        [((D_FF, D_MODEL), BF16, (FF_HALF, D_MODEL), lambda i, j, k: (i, 0), "row")], acc_shape=(FF_HALF, D_MODEL))

    (d_w_gu,) = _mm(
        "ffn_bwd_wgu", (1, 4, nt), (saved["n"], (TM, D_MODEL), lambda i, j, k: (k, 0)),
        (dgu, (TM, FF_HALF), lambda i, j, k: (k, j)), "tn",
        [((D_MODEL, 2 * D_FF), BF16, (D_MODEL, FF_HALF), lambda i, j, k: (0, j), "row")], acc_shape=(D_MODEL, FF_HALF))

    outs = _mm("ffn_bwd_dn", (s_len // FFN_TM, 1, 1), _rows(dgu, FFN_TM), _resident(w_gu), "nt",
               final_outs, extras=final_extras, epilogue=final_epi)
    return d_w_gu, d_w_d, outs


def _band_mask(first_block):
    qi = lax.broadcasted_iota(jnp.int32, (BAND, 2 * BAND), 0)
    kj = lax.broadcasted_iota(jnp.int32, (BAND, 2 * BAND), 1)
    ok = (kj >= qi) & (kj <= qi + BAND)
    return ok & (jnp.logical_not(first_block) | (kj >= BAND))


def _attn_views(s_len, d):
    n_str = s_len // d
    nb = n_str // BAND
    blk = (BAND, D_ATTN)
    cur = lambda r, b: (b, r)
    prev = lambda r, b: (jnp.maximum(b - 1, 0), r)
    return n_str, nb, blk, cur, prev


STREAM_LANES = 128


def _stream_scratch(tm):
    return pltpu.VMEM((D_ATTN // STREAM_LANES, tm, STREAM_LANES), F32)


def _to_streams(x, d, scr):
    if d == 1:
        return x
    tm = x.shape[0]
    chunks = D_ATTN // STREAM_LANES
    for c in range(chunks):
        scr[c, 0:tm, :] = x[:, c * STREAM_LANES:(c + 1) * STREAM_LANES]
    return jnp.concatenate([scr.at[c][pl.ds(r, tm // d, stride=d), :] for r in range(d) for c in range(chunks)], axis=1)


def _from_streams(v, d, scr):
    if d == 1:
        return v
    rows = v.shape[0]
    chunks = D_ATTN // STREAM_LANES
    for r in range(d):
        for c in range(chunks):
            lo = r * D_ATTN + c * STREAM_LANES
            scr.at[c][pl.ds(r, rows, stride=d), :] = v[:, lo:lo + STREAM_LANES]
    return jnp.concatenate([scr[c, 0:rows * d, :] for c in range(chunks)], axis=1)


def _stream_in(arr, tm, d, ahead=0, n_blocks=None):
    rows = tm // d
    step = ahead // rows
    if ahead:
        return (arr, (rows, d * D_ATTN), lambda i, *_: (jnp.minimum(i + step, n_blocks - 1), 0))
    return (arr, (rows, d * D_ATTN), lambda i, *_: (i, 0))


def _stream_out(s_len, dtype, tm, d):
    return ((s_len // d, d * D_ATTN), dtype, (tm // d, d * D_ATTN), lambda i, *_: (i, 0), "row")


def _attn_fwd(s_len, q, k, v, d):
    n_str, nb, blk, cur, prev = _attn_views(s_len, d)
    view = lambda t: t

    def body(q_ref, kp_ref, kc_ref, vp_ref, vc_ref, o_ref, lse_ref):
        mask = _band_mask(pl.program_id(1) == 0)
        qq = q_ref[...]
        kk = jnp.concatenate([kp_ref[...], kc_ref[...]], axis=0)
        vv = jnp.concatenate([vp_ref[...], vc_ref[...]], axis=0)
        for h in range(N_HEADS):
            sl = slice(h * HEAD_DIM, (h + 1) * HEAD_DIM)
            s = _dot(qq[:, sl], kk[:, sl], "nt") * (HEAD_DIM ** -0.5)
            s = jnp.where(mask, s, -1e30)
            m = jnp.max(s, axis=-1, keepdims=True)
            e = jnp.exp(s - m)
            den = jnp.sum(e, axis=-1, keepdims=True)
            o_ref[:, sl] = _dot(e / den, vv[:, sl], "nn")
            lse_ref[:, sl] = jnp.broadcast_to(m + jnp.log(den), (BAND, HEAD_DIM))

    o, lse = pl.pallas_call(
        body,
        name=f"attn_fwd_d{d}",
        grid=(d, nb),
        in_specs=[pl.BlockSpec(blk, cur), pl.BlockSpec(blk, prev), pl.BlockSpec(blk, cur),
                  pl.BlockSpec(blk, prev), pl.BlockSpec(blk, cur)],
        out_specs=[pl.BlockSpec(blk, cur), pl.BlockSpec(blk, cur)],
        out_shape=[jax.ShapeDtypeStruct((n_str, d * D_ATTN), F32)] * 2,
        compiler_params=_params(2),
    )(view(q), view(k), view(k), view(v), view(v))
    return o, lse


def _attn_bwd(s_len, q, k, v, dattn, attn, lse, d):
    n_str, nb, blk, cur, prev = _attn_views(s_len, d)
    view = lambda t: t

    def body(q_ref, kp_ref, kc_ref, vp_ref, vc_ref, da_ref, at_ref, lse_ref, dq_ref, dka_ref, dkb_ref, dva_ref, dvb_ref):
        mask = _band_mask(pl.program_id(1) == 0)
        qq = q_ref[...]
        kk = jnp.concatenate([kp_ref[...], kc_ref[...]], axis=0)
        vv = jnp.concatenate([vp_ref[...], vc_ref[...]], axis=0)
        da = da_ref[...]
        prod = da * at_ref[...]
        scale = HEAD_DIM ** -0.5
        for h in range(N_HEADS):
            sl = slice(h * HEAD_DIM, (h + 1) * HEAD_DIM)
            s = _dot(qq[:, sl], kk[:, sl], "nt") * scale
            p = jnp.where(mask, jnp.exp(s - lse_ref[:, h * HEAD_DIM:h * HEAD_DIM + 1]), 0.0)
            dp = _dot(da[:, sl], vv[:, sl], "nt")
            ds = p * (dp - jnp.sum(prod[:, sl], axis=-1, keepdims=True))
            dq_ref[:, sl] = _dot(ds, kk[:, sl], "nn") * scale
            dk = _dot(ds, qq[:, sl], "tn") * scale
            dv = _dot(p, da[:, sl], "tn")
            dkb_ref[:, sl] = dk[:BAND]
            dka_ref[:, sl] = dk[BAND:]
            dvb_ref[:, sl] = dv[:BAND]
            dva_ref[:, sl] = dv[BAND:]

    outs = pl.pallas_call(
        body,
        name=f"attn_bwd_d{d}",
        grid=(d, nb),
        in_specs=[pl.BlockSpec(blk, cur), pl.BlockSpec(blk, prev), pl.BlockSpec(blk, cur),
                  pl.BlockSpec(blk, prev), pl.BlockSpec(blk, cur),
                  pl.BlockSpec(blk, cur), pl.BlockSpec(blk, cur), pl.BlockSpec(blk, cur)],
        out_specs=[pl.BlockSpec(blk, cur)] * 5,
        out_shape=[jax.ShapeDtypeStruct((n_str, d * D_ATTN), F32)] * 5,
        compiler_params=_params(2),
    )(view(q), view(k), view(k), view(v), view(v), view(dattn), view(attn), view(lse))
    return list(outs)


def _rope_tables(positions):
    half = ROPE_DIM // 2
    inv_freq = ROPE_THETA ** (-jnp.arange(half, dtype=F32) * (2.0 / ROPE_DIM))
    ang = positions.astype(F32)[:, None] * inv_freq
    cos, sin = jnp.cos(ang), jnp.sin(ang)
    s_len = positions.shape[0]
    one = jnp.ones((s_len, HEAD_DIM - ROPE_DIM), F32)
    zero8 = jnp.zeros((s_len, half), F32)
    zero = jnp.zeros((s_len, HEAD_DIM - ROPE_DIM), F32)
    c = jnp.concatenate([cos, cos, one], axis=1)
    s1 = jnp.concatenate([zero8, sin, zero], axis=1)
    s2 = jnp.concatenate([-sin, zero8, zero], axis=1)
    tile = lambda t: jnp.tile(t, (1, N_HEADS))
    return tile(c), tile(s1), tile(s2)


def _rope(t, c, s1, s2):
    half = ROPE_DIM // 2
    return t * c + pltpu.roll(t, half, 1) * s1 + pltpu.roll(t, D_ATTN - half, 1) * s2


def _rope_transposed(dt, c, s1, s2):
    half = ROPE_DIM // 2
    return dt * c + pltpu.roll(dt * s1, D_ATTN - half, 1) + pltpu.roll(dt * s2, half, 1)


SCAN_ROWS = 256
SCAN_LANES = 512


def _scan(name, bu, a_cat, reverse):
    s_len, width = bu.shape
    nb = s_len // SCAN_ROWS
    half = width // 2

    def body(bu_ref, a_ref, x_ref, carry_ref):
        @pl.when(pl.program_id(0) == 0)
        def _():
            carry_ref[...] = jnp.zeros_like(carry_ref)

        for c in range(half // SCAN_LANES):
            re = pl.ds(c * SCAN_LANES, SCAN_LANES)
            im = pl.ds(half + c * SCAN_LANES, SCAN_LANES)
            ar = a_ref[:, re]
            ai = a_ref[:, im]

            def step(s, state):
                xr, xi = state
                t = (SCAN_ROWS - 1 - s) if reverse else s
                row = pl.ds(t, 1)
                nr = ar * xr - ai * xi + bu_ref[row, re]
                ni = ar * xi + ai * xr + bu_ref[row, im]
                x_ref[row, re] = nr
                x_ref[row, im] = ni
                return nr, ni

            xr, xi = lax.fori_loop(0, SCAN_ROWS, step, (carry_ref[0:1, re], carry_ref[0:1, im]), unroll=8)
            carry_ref[0:1, re] = xr
            carry_ref[0:1, im] = xi

    imap = (lambda i: (nb - 1 - i, 0)) if reverse else (lambda i: (i, 0))
    return pl.pallas_call(
        body,
        name=name,
        grid=(nb,),
        in_specs=[pl.BlockSpec((SCAN_ROWS, width), imap), pl.BlockSpec((1, width), lambda i: (0, 0))],
        out_specs=pl.BlockSpec((SCAN_ROWS, width), imap),
        out_shape=jax.ShapeDtypeStruct((s_len, width), F32),
        scratch_shapes=[pltpu.VMEM((8, width), F32)],
        compiler_params=_params(1),
    )(bu, a_cat)


def _ssm_params(lam_re, lam_im, log_dt, b_re, b_im, c_re, c_im):
    dt = jnp.exp(log_dt)[:, None]
    er = jnp.exp(lam_re * dt)
    a_re = er * jnp.cos(lam_im * dt)
    a_im = er * jnp.sin(lam_im * dt)
    nr, ni = a_re - 1.0, a_im
    den = lam_re * lam_re + lam_im * lam_im
    fr = (nr * lam_re + ni * lam_im) / den
    fi = (ni * lam_re - nr * lam_im) / den
    bb_re = fr[..., None] * b_re - fi[..., None] * b_im
    bb_im = fr[..., None] * b_im + fi[..., None] * b_re
    eye = jnp.eye(N_GROUPS, dtype=F32)

    def in_mat(bb):
        t = bb.transpose(0, 2, 1)[:, :, None, :] * eye[:, None, :, None]
        return t.reshape(D_SSM, N_STATE)

    def out_mat(cc):
        t = cc.transpose(0, 2, 1)[:, :, None, :] * eye[:, None, :, None]
        return t.reshape(N_STATE, D_SSM)

    w_b = jnp.concatenate([in_mat(bb_re), in_mat(bb_im)], axis=1)
    w_c = jnp.concatenate([out_mat(c_re), -out_mat(c_im)], axis=0)
    a_cat = jnp.concatenate([a_re.reshape(1, N_STATE), a_im.reshape(1, N_STATE)], axis=1)
    return a_cat, w_b, w_c


def _conj(a_cat):
    return jnp.concatenate([a_cat[:, :N_STATE], -a_cat[:, N_STATE:]], axis=1)


def _layer_fwd(s_len, h, n1, p_l, rope, weight, sp, gains, g_next):
    nt = s_len // TM
    sv = {}
    gw = {}

    def need(after, *names):
        for name in names:
            gw[name] = weight(name, after)

    need(h, "w_gu1", "w_d1")
    h1, a_in, sv["ffn1"] = _ffn_fwd(s_len, h, n1, gw["w_gu1"], gw["w_d1"], gains["ffn1_post_g"], gains["mix_pre_g"])

    need(h1, "w_in", "w_glu", "w_out")
    (proj,) = _mm("w_in", (nt, 1, 1), _rows(a_in, TM), _resident(gw["w_in"]), "nn",
                  [_row_out(s_len, 2 * D_MODEL, F32, TM)])

    c, s1, s2 = rope

    def rope_fn(tq, tk, tv, cc, a1, a2, scr):
        q, k = _rope(tq, cc, a1, a2), _rope(tk, cc, a1, a2)
        return [_to_streams(t, d, scr) for d in DILATIONS for t in (q, k, tv)]

    qkv = _rowwise(
        "rope", s_len, TM,
        [_rows(proj, TM, 0, D_ATTN), _rows(proj, TM, 1, D_ATTN), _rows(proj, TM, 2, D_ATTN),
         _rows(c, TM), _rows(s1, TM), _rows(s2, TM)],
        [_stream_out(s_len, BF16, TM, d) for d in DILATIONS for _ in range(3)], rope_fn,
        scratch=[_stream_scratch(TM)])
    qkv = {d: qkv[3 * i:3 * i + 3] for i, d in enumerate(DILATIONS)}
    parts = {d: _attn_fwd(s_len, *qkv[d], d) for d in DILATIONS}

    def mix_fn(*args):
        g, scr = args[6], args[7]
        o1, l1, o2, l2, o3, l3 = [_from_streams(args[2 * i + j], d, scr) for i, d in enumerate(DILATIONS)
                                  for j in range(2)]
        m = jnp.maximum(jnp.maximum(l1, l2), l3)
        e1, e2, e3 = jnp.exp(l1 - m), jnp.exp(l2 - m), jnp.exp(l3 - m)
        tot = e1 + e2 + e3
        attn = (e1 * o1 + e2 * o2 + e3 * o3) / tot
        lse = m + jnp.log(tot)
        return [attn, lse, _rms(attn, g)] + [_to_streams(t, d, scr) for d in DILATIONS[1:] for t in (attn, lse)]

    mix_out = _rowwise(
        "attn_mix", s_len, TM,
        [_stream_in(t, TM, d) for d in DILATIONS for t in parts[d]] + [_whole(gains["attn_norm_g"])],
        [_row_out(s_len, D_ATTN, F32, TM), _row_out(s_len, D_ATTN, F32, TM),
         _row_out(s_len, D_ATTN, BF16, TM, col=0, total=D_MODEL)]
        + [_stream_out(s_len, F32, TM, d) for d in DILATIONS[1:] for _ in range(2)], mix_fn,
        scratch=[_stream_scratch(TM)])
    attn, lse, mixed_half = mix_out[:3]
    attn_s = {1: (attn, lse), DILATIONS[1]: tuple(mix_out[3:5]), DILATIONS[2]: tuple(mix_out[5:7])}

    a_cat, w_b, w_c, d_vec = sp
    ts = SCAN_ROWS
    (bu,) = _mm("ssm_bu", (s_len // ts, 1, 1), (proj, (ts, D_SSM), lambda i, j, k: (i, 3)), _whole(w_b), "nn",
                [_row_out(s_len, 2 * N_STATE, F32, ts)])
    xs = _scan("ssm_scan", bu, a_cat, False)

    def y_epi(acc, u, dv):
        z = acc + dv * u
        return z, _gelu(z)

    z, yg = _mm("ssm_y", (s_len // ts, 1, 1), _rows(xs, ts), _whole(w_c), "nn",
                [_row_out(s_len, D_SSM, F32, ts)] * 2,
                extras=[(proj, (ts, D_SSM), lambda i, j, k: (i, 3)), _whole(d_vec)], epilogue=y_epi)

    def glu_epi(acc, y, b, g):
        t = acc + b
        ssm = y * _sigmoid(t)
        return t, ssm, _rms(ssm, g)

    t_glu, ssm, mixed = _mm(
        "ssm_glu", (nt, 1, 1), _rows(yg, TM), _whole(gw["w_glu"]), "nn",
        [_row_out(s_len, D_SSM, F32, TM), _row_out(s_len, D_SSM, F32, TM),
         _row_out(s_len, D_SSM, BF16, TM, col=1, total=D_MODEL)],
        extras=[_rows(yg, TM), _whole(gains["ssm_b_glu"]), _whole(gains["ssm_norm_g"])], epilogue=glu_epi,
        into=[(mixed_half, 2)])

    def out_epi(acc, h_blk, gp, gn):
        h_new = h_blk + _rms(acc, gp)
        return acc, h_new, _rms(h_new, gn)

    o, h2, n2 = _mm(
        "w_out", (nt, 1, 1), _rows(mixed, TM), _whole(gw["w_out"]), "nn",
        [_row_out(s_len, D_MODEL, F32, TM), _row_out(s_len, D_MODEL, F32, TM), _row_out(s_len, D_MODEL, BF16, TM)],
        extras=[_rows(h1, TM), _whole(gains["mix_post_g"]), _whole(gains["ffn2_pre_g"])], epilogue=out_epi)

    need(h2, "w_gu2", "w_d2", "w_pu", "ple_w_gate")
    h3, _, sv["ffn2"] = _ffn_fwd(s_len, h2, n2, gw["w_gu2"], gw["w_d2"], gains["ffn2_post_g"], gains["ffn2_post_g"])

    (pu,) = _mm("ple_up", (nt, 1, 1), _rows(p_l, TM), _resident(gw["w_pu"]), "nn",
                [_row_out(s_len, D_MODEL, F32, TM)])

    def ple_epi(acc, pu_blk, h_blk, gp, gn):
        h_new = h_blk + _rms(pu_blk * _sigmoid(acc), gp)
        return acc, h_new, _rms(h_new, gn)

    gt, h4, n_next = _mm(
        "ple_gate", (nt, 1, 1), _rows(h3, TM), _whole(gw["ple_w_gate"]), "nn",
        [_row_out(s_len, D_MODEL, F32, TM), _row_out(s_len, D_MODEL, F32, TM), _row_out(s_len, D_MODEL, BF16, TM)],
        extras=[_rows(pu, TM), _rows(h3, TM), _whole(gains["ple_post_g"]), _whole(g_next)], epilogue=ple_epi)

    sv.update(h1=h1, a_in=a_in, proj=proj, qkv=qkv, attn=attn, attn_s=attn_s, xs=xs, z=z, yg=yg, t_glu=t_glu, ssm=ssm,
              mixed=mixed, o=o, h2=h2, h3=h3, pu=pu, gt=gt, p_l=p_l, w=gw)
    return h4, n_next, sv


def _vjp(fn, args, cot):
    _, pull = jax.vjp(fn, *args)
    return pull(cot)


def _layer_bwd(s_len, dh4, sv, rope, gw, sp, gains, on_partial=None):
    nt = s_len // TM
    gr = {}
    wg = {}

    def ple_fn(dh, pu, gt, g):
        dpu, dgt, dg = _vjp(lambda a, b, c: _rms(a * _sigmoid(b), c), (pu, gt, g), dh)
        return dpu, dgt, dg

    dpu, dgt, gr["ple_post_g"] = _rowwise(
        "ple_bwd", s_len, TM, [_rows(dh4, TM), _rows(sv["pu"], TM), _rows(sv["gt"], TM), _whole(gains["ple_post_g"])],
        [_row_out(s_len, D_MODEL, BF16, TM), _row_out(s_len, D_MODEL, BF16, TM), _col_out(D_MODEL)], ple_fn)

    def square_w_grad(name, lhs, rhs):
        half = D_MODEL // 2
        (dw,) = _mm(name, (2, 1, nt), (lhs, (TM, half), lambda i, j, k: (k, i)),
                    (rhs, (TM, D_MODEL), lambda i, j, k: (k, 0)), "tn",
                    [((D_MODEL, D_MODEL), BF16, (half, D_MODEL), lambda i, j, k: (i, 0), "row")],
                    acc_shape=(half, D_MODEL))
        return dw

    (wg["ple_w_up"],) = _mm(
        "ple_bwd_wup", (1, 1, nt), (sv["p_l"], (TM, PLE_DIM), lambda i, j, k: (k, 0)),
        (dpu, (TM, D_MODEL), lambda i, j, k: (k, 0)), "tn",
        [((PLE_DIM, D_MODEL), BF16, (PLE_DIM, D_MODEL), lambda i, j, k: (0, 0), "row")],
        acc_shape=(PLE_DIM, D_MODEL))
    wg["ple_w_gate"] = square_w_grad("ple_bwd_wgate", sv["h3"], dgt)

    def ple_dx_epi(acc, dh, f, g):
        dh3 = dh + acc
        df, dg = _vjp(_rms, (f, g), 0.5 * dh3)
        return dh3, df, dg

    dh3, df2, gr["ffn2_post_g"] = _mm(
        "ple_bwd_dx", (nt, 1, 1), _rows(dgt, TM), _whole(gw["ple_w_gate"]), "nt",
        [_row_out(s_len, D_MODEL, F32, TM), _row_out(s_len, D_MODEL, BF16, TM), _col_out(D_MODEL)],
        extras=[_rows(dh4, TM), _rows(sv["ffn2"]["f"], TM), _whole(gains["ffn2_post_g"])], epilogue=ple_dx_epi)

    def ffn2_final(dn, dh, h, g_pre, o, g_post):
        dx, dg_pre = _vjp(_rms, (h, g_pre), dn)
        dh2 = dh + dx
        do, dg_post = _vjp(_rms, (o, g_post), dh2)
        return dh2, do, dg_pre, dg_post

    wg["ffn2_w_gu"], wg["ffn2_w_down"], (dh2, do, gr["ffn2_pre_g"], gr["mix_post_g"]) = _ffn_bwd(
        s_len, sv["ffn2"], df2, gw["w_gu2"], gw["w_d2"], ffn2_final,
        [_rows(dh3, FFN_TM), _rows(sv["h2"], FFN_TM), _whole(gains["ffn2_pre_g"]), _rows(sv["o"], FFN_TM),
         _whole(gains["mix_post_g"])],
        [_row_out(s_len, D_MODEL, F32, FFN_TM), _row_out(s_len, D_MODEL, BF16, FFN_TM), _col_out(D_MODEL),
         _col_out(D_MODEL)])

    wg["w_out"] = square_w_grad("w_out_bwd_w", sv["mixed"], do)

    def mixed_epi(acc, attn, ssm, yg, t, g_a, g_s, scr):
        dattn, dg_a = _vjp(_rms, (attn, g_a), acc[:, :D_ATTN])
        dssm, dg_s = _vjp(_rms, (ssm, g_s), acc[:, D_ATTN:])
        sg = _sigmoid(t)
        dt = dssm * yg * sg * (1.0 - sg)
        return ([_to_streams(dattn, d, scr) for d in DILATIONS]
                + [dt, dssm * sg, dg_a, dg_s, jnp.sum(dt, axis=0, keepdims=True)])

    res = _mm(
        "w_out_bwd_x", (nt, 1, 1), _rows(do, TM), _whole(gw["w_out"]), "nt",
        [_stream_out(s_len, F32, TM, d) for d in DILATIONS]
        + [_row_out(s_len, D_SSM, BF16, TM), _row_out(s_len, D_SSM, F32, TM),
           _col_out(D_ATTN), _col_out(D_SSM), _col_out(D_SSM)],
        extras=[_rows(sv["attn"], TM), _rows(sv["ssm"], TM), _rows(sv["yg"], TM), _rows(sv["t_glu"], TM),
                _whole(gains["attn_norm_g"]), _whole(gains["ssm_norm_g"])], epilogue=mixed_epi,
        scratch=[_stream_scratch(TM)])
    dattn_s = dict(zip(DILATIONS, res[:3]))
    dt_glu, dyg_dir, gr["attn_norm_g"], gr["ssm_norm_g"], gr["ssm_b_glu"] = res[3:]

    a_cat, w_b, w_c, d_vec = sp
    ts = SCAN_ROWS
    u_spec = (sv["proj"], (TM, D_SSM), lambda i, *_: (i, 3))
    (wg["ssm_w_glu"],) = _mm(
        "ssm_bwd_wglu", (1, 1, nt), (sv["yg"], (TM, D_SSM), lambda i, j, k: (k, 0)),
        (dt_glu, (TM, D_SSM), lambda i, j, k: (k, 0)), "tn",
        [((D_SSM, D_SSM), BF16, (D_SSM, D_SSM), lambda i, j, k: (0, 0), "row")], acc_shape=(D_SSM, D_SSM))

    def gelu_epi(acc, dy_dir, z, u, dv):
        (dz,) = _vjp(_gelu, (z,), acc + dy_dir)
        return dz, dz * dv, jnp.sum(dz * u, axis=0, keepdims=True)

    dz, du_dir, gr["ssm_d"] = _mm(
        "ssm_bwd_glu", (nt, 1, 1), _rows(dt_glu, TM), _whole(gw["w_glu"]), "nt",
        [_row_out(s_len, D_SSM, BF16, TM), _row_out(s_len, D_SSM, F32, TM), _col_out(D_SSM)],
        extras=[_rows(dyg_dir, TM), _rows(sv["z"], TM), u_spec, _whole(d_vec)], epilogue=gelu_epi)

    (d_w_c,) = _mm(
        "ssm_bwd_wc", (2 * N_STATE // 1024, 1, nt), (sv["xs"], (TM, 1024), lambda i, j, k: (k, i)),
        (dz, (TM, D_SSM), lambda i, j, k: (k, 0)), "tn",
        [((2 * N_STATE, D_SSM), F32, (1024, D_SSM), lambda i, j, k: (i, 0), "row")], acc_shape=(1024, D_SSM))
    (dxs,) = _mm("ssm_bwd_dx", (s_len // ts, 1, 1), _rows(dz, ts), _whole(w_c), "nt",
                 [_row_out(s_len, 2 * N_STATE, F32, ts)])
    gs = _scan("ssm_scan_rev", dxs, _conj(a_cat), True)

    def da_fn(x, x_before, g):
        i = pl.program_id(0)
        rolled = pltpu.roll(x, 1, 0)
        first_row = jnp.where(i > 0, x_before[7:8, :], 0.0)
        rows = lax.broadcasted_iota(jnp.int32, x.shape, 0)
        xp = jnp.where(rows == 0, first_row, rolled)
        xr, xi = xp[:, :N_STATE], xp[:, N_STATE:]
        g_r, g_i = g[:, :N_STATE], g[:, N_STATE:]
        d_re = jnp.sum(xr * g_r + xi * g_i, axis=0, keepdims=True)
        d_im = jnp.sum(xr * g_i - xi * g_r, axis=0, keepdims=True)
        return (jnp.concatenate([d_re, d_im], axis=1),)

    (d_a,) = _rowwise(
        "ssm_bwd_da", s_len, ts,
        [_rows(sv["xs"], ts), (sv["xs"], (8, 2 * N_STATE), lambda i: (jnp.maximum(i * (ts // 8) - 1, 0), 0)),
         _rows(gs, ts)], [_col_out(2 * N_STATE)], da_fn)

    (d_w_b,) = _mm(
        "ssm_bwd_wb", (1, 2, nt), (sv["proj"], (TM, D_SSM), lambda i, j, k: (k, 3)),
        (gs, (TM, N_STATE), lambda i, j, k: (k, j)), "tn",
        [((D_SSM, 2 * N_STATE), F32, (D_SSM, N_STATE), lambda i, j, k: (0, j), "row")], acc_shape=(D_SSM, N_STATE))
    sum_tm = 2 * BAND
    n_sum = s_len // sum_tm
    ins = []
    for d in DILATIONS:
        dq_p, dka, dkb, dva, dvb = _attn_bwd(s_len, *sv["qkv"][d], dattn_s[d], *sv["attn_s"][d], d)
        if d == 1:
            nxt = lambda t: (t, (sum_tm, D_ATTN), lambda i: (jnp.minimum(i + 1, n_sum - 1), 0))
            ins += [_rows(dq_p, sum_tm), _rows(dka, sum_tm), _rows(dkb, sum_tm), nxt(dkb), _rows(dva, sum_tm),
                    _rows(dvb, sum_tm), nxt(dvb)]
        else:
            ins += [_stream_in(dq_p, sum_tm, d), _stream_in(dka, sum_tm, d), _stream_in(dkb, sum_tm, d, BAND, n_sum),
                    _stream_in(dva, sum_tm, d), _stream_in(dvb, sum_tm, d, BAND, n_sum)]
    c, s1, s2 = rope
    ins += [_rows(c, sum_tm), _rows(s1, sum_tm), _rows(s2, sum_tm)]

    def qkv_fn(*args):
        i = pl.program_id(0)
        blocks, scr = args[:-1], args[-1]
        dq_t, dka, dkb, dkb_next, dva, dvb, dvb_next = blocks[:7]
        more = i + 1 < n_sum
        ahead = lambda cur, nxt: jnp.concatenate([cur[BAND:], jnp.where(more, nxt[:BAND], 0.0)], axis=0)
        dk_t = dka + ahead(dkb, dkb_next)
        dv_t = dva + ahead(dvb, dvb_next)
        at = 7
        for d in DILATIONS[1:]:
            dq_p, dka, dkb, dva, dvb = blocks[at:at + 5]
            at += 5
            live = i + BAND // (sum_tm // d) < n_sum
            dq_t = dq_t + _from_streams(dq_p, d, scr)
            dk_t = dk_t + _from_streams(dka + jnp.where(live, dkb, 0.0), d, scr)
            dv_t = dv_t + _from_streams(dva + jnp.where(live, dvb, 0.0), d, scr)
        cc, a1, a2 = blocks[at:at + 3]
        return (jnp.concatenate([_rope_transposed(dq_t, cc, a1, a2), _rope_transposed(dk_t, cc, a1, a2), dv_t], axis=1),)

    (dqkv,) = _rowwise(
        "attn_bwd_sum", s_len, sum_tm, ins,
        [((s_len, 2 * D_MODEL), BF16, (sum_tm, 3 * D_ATTN), lambda i: (i, 0), "row")], qkv_fn,
        scratch=[_stream_scratch(sum_tm)])
    (dproj,) = _mm("ssm_bwd_du", (s_len // ts, 1, 1), _rows(gs, ts), _whole(w_b), "nt",
                   [_row_out(s_len, D_SSM, BF16, ts, col=3, total=2 * D_MODEL)], extras=[_rows(du_dir, ts)],
                   epilogue=lambda acc, d: (acc + d,), into=[(dqkv, 0)])

    (wg["w_in"],) = _mm(
        "w_in_bwd_w", (1, 2, nt), (sv["a_in"], (TM, D_MODEL), lambda i, j, k: (k, 0)),
        (dproj, (TM, D_MODEL), lambda i, j, k: (k, j)), "tn",
        [((D_MODEL, 2 * D_MODEL), BF16, (D_MODEL, D_MODEL), lambda i, j, k: (0, j), "row")],
        acc_shape=(D_MODEL, D_MODEL))

    tie = on_partial(wg) if on_partial is not None else None
    mix_pre_g = gains["mix_pre_g"] if tie is None else gains["mix_pre_g"] + tie

    def in_epi(acc, dh, h, g_pre, f, g_post):
        dx, dg_pre = _vjp(_rms, (h, g_pre), acc)
        dh1 = dh + dx
        df, dg_post = _vjp(_rms, (f, g_post), 0.5 * dh1)
        return dh1, df, dg_pre, dg_post

    dh1, df1, gr["mix_pre_g"], gr["ffn1_post_g"] = _mm(
        "w_in_bwd_x", (nt, 1, 1), _rows(dproj, TM), _resident(gw["w_in"]), "nt",
        [_row_out(s_len, D_MODEL, F32, TM), _row_out(s_len, D_MODEL, BF16, TM), _col_out(D_MODEL), _col_out(D_MODEL)],
        extras=[_rows(dh2, TM), _rows(sv["h1"], TM), _whole(mix_pre_g), _rows(sv["ffn1"]["f"], TM),
                _whole(gains["ffn1_post_g"])], epilogue=in_epi)

    def ffn1_final(dn, dh, h, g_pre):
        dx, dg_pre = _vjp(_rms, (h, g_pre), dn)
        return dh + dx, dg_pre

    wg["ffn1_w_gu"], wg["ffn1_w_down"], (dh0, gr["ffn1_pre_g"]) = _ffn_bwd(
        s_len, sv["ffn1"], df1, gw["w_gu1"], gw["w_d1"], ffn1_final,
        [_rows(dh1, FFN_TM), _rows(sv["ffn1"]["h"], FFN_TM), _whole(gains["ffn1_pre_g"])],
        [_row_out(s_len, D_MODEL, F32, FFN_TM), _col_out(D_MODEL)])

    return dh0, wg, gr, (d_a, d_w_b, d_w_c)


def _peers():
    x, y, c = lax.axis_index("x"), lax.axis_index("y"), lax.axis_index("c")
    me = 4 * x + 2 * y + c
    peers = []
    for k in range(1, N_DEV):
        kx, ky, kc = (k >> 2) & 1, (k >> 1) & 1, k & 1
        px, py, pc = x ^ kx, y ^ ky, c ^ kc
        peers.append(((px, py, pc), 4 * px + 2 * py + pc))
    return me, peers


_HBM_SPEC = pl.BlockSpec(memory_space=pltpu.HBM)
_SEM_SPEC = pl.BlockSpec(memory_space=pltpu.SEMAPHORE)
_DATAFLOW = pltpu.SideEffectType.DATAFLOW_SIDE_EFFECTING


def _device_index():
    return 4 * lax.axis_index("x") + 2 * lax.axis_index("y") + lax.axis_index("c")


def _landing(arrays, scatter):
    me = _device_index()
    out = []
    for a, scattered in zip(arrays, scatter):
        own = lax.dynamic_index_in_dim(a, me, 0, keepdims=True) if scattered else a[None]
        buf = lax.empty((N_DEV,) + own.shape[1:], a.dtype)
        out.append(lax.dynamic_update_slice_in_dim(buf, own, me, 0))
    return out


def _split_copies(src_refs, land_refs, send_sems, recv_sems, scatter):
    me, peers = _peers()
    pairs = []
    for t in range(len(src_refs)):
        for k, (peer, peer_id) in enumerate(peers):
            src = src_refs[t].at[peer_id] if scatter[t] else src_refs[t]
            sem = t * (N_DEV - 1) + k
            mk = lambda slot, src=src, t=t, sem=sem, peer=peer: pltpu.make_async_remote_copy(
                src_ref=src, dst_ref=land_refs[t].at[slot], send_sem=send_sems.at[sem], recv_sem=recv_sems.at[sem],
                device_id=peer, device_id_type=pl.DeviceIdType.MESH)
            pairs.append((functools.partial(mk, me), functools.partial(mk, peer_id)))
    return pairs


def _exchange_start(name, arrays, scatter, order_after):
    n = len(arrays)
    landing = _landing(arrays, scatter)

    def body(*refs):
        src_refs, land_refs = refs[:n], refs[n:2 * n]
        send_sems, recv_sems = refs[2 * n + 1], refs[2 * n + 2]
        token_ref = refs[-1]
        for outgoing, _ in _split_copies(src_refs, land_refs, send_sems, recv_sems, scatter):
            outgoing().start()
        token_ref[...] = jnp.zeros_like(token_ref)

    sem_shape = pltpu.SemaphoreType.DMA((n * (N_DEV - 1),))
    thru = [pltpu.HBM(a.shape, a.dtype) for a in list(arrays) + landing]
    hbm = lambda t: pltpu.with_memory_space_constraint(t, pltpu.HBM)
    res = pl.pallas_call(
        body,
        name=name,
        in_specs=[_HBM_SPEC] * (2 * n) + [pl.BlockSpec(memory_space=pl.ANY)],
        out_specs=[_SEM_SPEC, _SEM_SPEC] + [_HBM_SPEC] * (2 * n) + [pl.BlockSpec(memory_space=pltpu.VMEM)],
        out_shape=[sem_shape, sem_shape] + thru + [jax.ShapeDtypeStruct((8, 128), F32)],
        input_output_aliases={i: 2 + i for i in range(2 * n)},
        compiler_params=pltpu.CompilerParams(has_side_effects=_DATAFLOW),
    )(*[hbm(t) for t in list(arrays) + landing], order_after)
    return (res[0], res[1], res[2:2 + n], res[2 + n:2 + 2 * n]), res[-1]


def _exchange_wait(name, handle, scatter, order_after):
    send_sems, recv_sems, sources, landing = handle
    n = len(sources)

    def body(*refs):
        src_refs, land_refs = refs[:n], refs[n:2 * n]
        for outgoing, arrival in _split_copies(src_refs, land_refs, refs[2 * n], refs[2 * n + 1], scatter):
            outgoing().wait_send()
            arrival().wait_recv()

    thru = [pltpu.HBM(a.shape, a.dtype) for a in list(sources) + list(landing)]
    res = pl.pallas_call(
        body,
        name=name,
        in_specs=[_HBM_SPEC] * (2 * n) + [_SEM_SPEC, _SEM_SPEC, pl.BlockSpec(memory_space=pl.ANY)],
        out_specs=[_HBM_SPEC] * (2 * n),
        out_shape=thru,
        input_output_aliases={i: i for i in range(2 * n)},
        compiler_params=pltpu.CompilerParams(has_side_effects=_DATAFLOW),
    )(*sources, *landing, send_sems, recv_sems, order_after)
    return list(res[n:])


def _adam_math(g, w, m, v):
    m = ADAM_B1 * m + (1.0 - ADAM_B1) * g
    v = ADAM_B2 * v + (1.0 - ADAM_B2) * (g * g)
    m_hat = m / (1.0 - ADAM_B1 ** ADAM_STEP)
    v_hat = v / (1.0 - ADAM_B2 ** ADAM_STEP)
    delta = -ADAM_LR * (m_hat / (jnp.sqrt(v_hat) + ADAM_EPS) + ADAM_WD * w)
    return delta, m, v


def _adamw(name, recv, recv_block, recv_map, w, m, v, tr, layer=None, prev=None, after=None):
    def fn(r, wb, mb, vb, *_):
        g = r[0].astype(F32)
        for s in range(1, N_DEV):
            g = g + r[s].astype(F32)
        return (g,) + _adam_math(g, wb, mb, vb)

    rows, cols = w.shape[-2:]
    if layer is None:
        spec = lambda t: _rows(t, tr)
        out = _row_out(rows, cols, F32, tr)
    else:
        spec = lambda t: (t, (None, tr, cols), lambda i: (layer, i, 0))
        out = (w.shape, F32, (None, tr, cols), lambda i: (layer, i, 0), "row")
    ins = [(recv, recv_block, recv_map), spec(w), spec(m), spec(v)]
    if after is not None:
        ins.append(_whole(after))
    into = [] if prev is None else [(t, k) for k, t in enumerate(prev)]
    return _rowwise(name, rows, tr, ins, [out] * 4, fn, into=into)


def _pack_small(tensors):
    flat = jnp.concatenate([tensors[n].reshape(-1).astype(F32) for n in SMALL])
    padded = -(-flat.shape[0] // SMALL_PAD) * SMALL_PAD
    return jnp.pad(flat, (0, padded - flat.shape[0])).reshape(padded // SMALL_LANES, SMALL_LANES)


def _local_step(x, p, positions, loss_target, w, layer_weights, layer_done=None, layer_partial=None):
    s_len = x.shape[0]
    depth = p.shape[0]
    rope = _rope_tables(positions)
    gains = [{n: w[n][l].reshape(1, -1) for n in SMALL if w[n].ndim == 2 and n != "ssm_log_dt"} for l in range(depth)]
    ssm_args = lambda l: tuple(w[n][l] for n in ("ssm_lam_re", "ssm_lam_im", "ssm_log_dt", "ssm_b_re", "ssm_b_im",
                                                  "ssm_c_re", "ssm_c_im"))
    sps, pulls = [], []
    for l in range(depth):
        (a_cat, w_b, w_c), pull = jax.vjp(_ssm_params, *ssm_args(l))
        sps.append((a_cat, w_b.astype(BF16), w_c.astype(BF16), w["ssm_d"][l].reshape(1, D_SSM)))
        pulls.append(pull)

    (n,) = _rowwise("pre_norm", s_len, TM, [_rows(x, TM), _whole(gains[0]["ffn1_pre_g"])],
                    [_row_out(s_len, D_MODEL, BF16, TM)], lambda a, g: (_rms(a, g),))
    h = x
    saved = []
    for l in range(depth):
        g_next = gains[l + 1]["ffn1_pre_g"] if l + 1 < depth else gains[l]["ffn1_pre_g"]
        h, n, sv = _layer_fwd(s_len, h, n, p[l], rope, layer_weights(l, h), sps[l], gains[l], g_next)
        saved.append(sv)

    def loss_fn(y, t):
        e = y - t
        return e * (1.0 / D_MODEL), jnp.sum(e * e, axis=0, keepdims=True)

    dh, sq = _rowwise("loss", s_len, TM, [_rows(h, TM), _rows(loss_target, TM)],
                      [_row_out(s_len, D_MODEL, F32, TM), _col_out(D_MODEL)], loss_fn)
    loss = 0.5 * jnp.sum(sq) / D_MODEL

    w_grads, small_grads = [None] * depth, [None] * depth
    tie = None
    for l in reversed(range(depth)):
        g_l = gains[l] if tie is None else dict(gains[l], ple_post_g=gains[l]["ple_post_g"] + tie)
        partial = None if layer_partial is None else functools.partial(layer_partial, l)
        dh, wg, gr, (d_a, d_w_b, d_w_c) = _layer_bwd(s_len, dh, saved[l], rope, saved[l]["w"], sps[l], g_l, partial)
        d_lre, d_lim, d_dt, d_bre, d_bim, d_cre, d_cim = pulls[l]((d_a, d_w_b, d_w_c))
        gr.update(ssm_lam_re=d_lre, ssm_lam_im=d_lim, ssm_log_dt=d_dt, ssm_b_re=d_bre, ssm_b_im=d_bim,
                  ssm_c_re=d_cre, ssm_c_im=d_cim)
        gr = {n: g.reshape(w[n].shape[1:]) for n, g in gr.items()}
        w_grads[l], small_grads[l] = wg, gr
        if layer_done is not None:
            tie = layer_done(l, wg, gr, dh)
    return loss, dh, w_grads, small_grads


GROUPS = ("gu1", "gu2", "d1", "d2", "out", "pg", "in", "pu", "glu")
FIRST_PARTS = (("gu1", "d1"), ("in", "glu", "out"), ("gu2", "d2", "pu", "pg"))
_VIEW_OF = {"gu1": "w_gu1", "gu2": "w_gu2", "d1": "w_d1", "d2": "w_d2", "out": "w_out", "pg": "ple_w_gate",
            "in": "w_in", "pu": "w_pu", "glu": "w_glu"}
_PLACE = {"ffn1_w_gate": ("gu1", 0), "ffn1_w_up": ("gu1", 1), "ffn2_w_gate": ("gu2", 0), "ffn2_w_up": ("gu2", 1),
          "ffn1_w_down": ("d1", None), "ffn2_w_down": ("d2", None), "w_out": ("out", None),
          "ple_w_gate": ("pg", None), "w_in": ("in", None), "ple_w_up": ("pu", None), "ssm_w_glu": ("glu", None)}


def _shard_groups(w, l):
    cast = lambda n: w[n][l].astype(BF16)
    return {"gu1": jnp.stack([cast("ffn1_w_gate"), cast("ffn1_w_up")]),
            "gu2": jnp.stack([cast("ffn2_w_gate"), cast("ffn2_w_up")]),
            "d1": cast("ffn1_w_down"), "d2": cast("ffn2_w_down"), "out": cast("w_out"), "pg": cast("ple_w_gate"),
            "in": cast("w_in"), "pu": cast("ple_w_up"), "glu": cast("ssm_w_glu")}


def _gathered_views(gathered, tie=None):
    cols = lambda t: t.transpose(1, 0, 2).reshape(t.shape[1], N_DEV * t.shape[2])
    rows = lambda t: t.reshape(N_DEV * t.shape[1], t.shape[2])
    views = {}
    for group, g in gathered.items():
        if tie is not None and group in ("gu1", "in", "gu2"):
            g, tie = g + tie.astype(g.dtype), None
        if group in ("gu1", "gu2"):
            views[_VIEW_OF[group]] = jnp.concatenate([cols(g[:, 0]), cols(g[:, 1])], axis=1)
        elif group in ("in", "pu"):
            views[_VIEW_OF[group]] = cols(g)
        else:
            views[_VIEW_OF[group]] = rows(g)
    assert tie is None
    return views


def _grad_groups(wg):
    rows = lambda t: t.reshape(N_DEV, t.shape[0] // N_DEV, t.shape[1])
    cols = lambda t: t.reshape(t.shape[0], N_DEV, t.shape[1] // N_DEV).transpose(1, 0, 2)
    gu = lambda t: jnp.stack([cols(t[:, :D_FF]), cols(t[:, D_FF:])], axis=1)
    make = {"gu1": ("ffn1_w_gu", gu), "gu2": ("ffn2_w_gu", gu), "d1": ("ffn1_w_down", rows), "d2": ("ffn2_w_down", rows),
            "out": ("w_out", rows), "pg": ("ple_w_gate", rows), "in": ("w_in", cols), "pu": ("ple_w_up", cols),
            "glu": ("ssm_w_glu", rows)}
    return {group: fn(wg[src]) for group, (src, fn) in make.items() if src in wg}


def _update_sharded(name, recv_groups, w, m, v, layer, prev, after=None):
    group, slot = _PLACE[name]
    recv = recv_groups[group]
    rows, cols = w.shape[1:]
    tr = rows if rows <= 256 else (rows // 2 if rows % 256 else 256)
    if slot is None:
        block, imap = (N_DEV, tr, cols), (lambda i: (0, i, 0))
    else:
        block, imap = (N_DEV, None, tr, cols), (lambda i: (0, slot, i, 0))
    return _adamw(f"adamw_{name}", recv, block, imap, w, m, v, tr, layer, prev, after)


def kernel(x, p, positions, ffn1_pre_g, ffn1_w_gate, ffn1_w_up, ffn1_w_down, ffn1_post_g, mix_pre_g, w_in, attn_norm_g, ssm_lam_re, ssm_lam_im, ssm_log_dt, ssm_b_re, ssm_b_im, ssm_c_re, ssm_c_im, ssm_d, ssm_w_glu, ssm_b_glu, ssm_norm_g, w_out, mix_post_g, ffn2_pre_g, ffn2_w_gate, ffn2_w_up, ffn2_w_down, ffn2_post_g, ple_w_up, ple_w_gate, ple_post_g, loss_target, m_ffn1_pre_g, m_ffn1_w_gate, m_ffn1_w_up, m_ffn1_w_down, m_ffn1_post_g, m_mix_pre_g, m_w_in, m_attn_norm_g, m_ssm_lam_re, m_ssm_lam_im, m_ssm_log_dt, m_ssm_b_re, m_ssm_b_im, m_ssm_c_re, m_ssm_c_im, m_ssm_d, m_ssm_w_glu, m_ssm_b_glu, m_ssm_norm_g, m_w_out, m_mix_post_g, m_ffn2_pre_g, m_ffn2_w_gate, m_ffn2_w_up, m_ffn2_w_down, m_ffn2_post_g, m_ple_w_up, m_ple_w_gate, m_ple_post_g, v_ffn1_pre_g, v_ffn1_w_gate, v_ffn1_w_up, v_ffn1_w_down, v_ffn1_post_g, v_mix_pre_g, v_w_in, v_attn_norm_g, v_ssm_lam_re, v_ssm_lam_im, v_ssm_log_dt, v_ssm_b_re, v_ssm_b_im, v_ssm_c_re, v_ssm_c_im, v_ssm_d, v_ssm_w_glu, v_ssm_b_glu, v_ssm_norm_g, v_w_out, v_mix_post_g, v_ffn2_pre_g, v_ffn2_w_gate, v_ffn2_w_up, v_ffn2_w_down, v_ffn2_post_g, v_ple_w_up, v_ple_w_gate, v_ple_post_g):
    args = dict(locals())
    w = {n: args[n] for n in WEIGHTS}
    mom = {n: args["m_" + n] for n in WEIGHTS}
    var = {n: args["v_" + n] for n in WEIGHTS}
    depth = p.shape[0]

    gathers, scatters, recv, tokens = {}, {}, {}, {}

    def start(name, groups, names, scatter, order_after):
        flags = scatter if isinstance(scatter, tuple) else (scatter,) * len(names)
        handle, token = _exchange_start(name, [groups[n] for n in names], flags, order_after)
        return (handle, names, flags), token

    def finish(name, pending, order_after):
        handle, names, flags = pending
        return dict(zip(names, _exchange_wait(name, handle, flags, order_after)))

    def gather_next(l, order_after):
        if l + 1 >= depth:
            return None
        gathers[l + 1], token = start("all_gather_start", _shard_groups(w, l + 1), GROUPS, False, order_after)
        return token[0, 0]

    def layer_weights(l, h):
        views = {}
        if l > 0:
            got = finish("all_gather_wait", gathers.pop(l), h)
            views.update(_gathered_views(got, gather_next(l, got[GROUPS[0]])))
            return lambda name, after: views[name]

        shards = _shard_groups(w, 0)
        part_a, part_b, part_c = FIRST_PARTS
        pending, token = start("all_gather_start_a", shards, part_a, False, h)
        got = finish("all_gather_wait_a", pending, token)
        waiting = {}
        waiting["b"], token = start("all_gather_start_b", shards, part_b, False, got[part_a[0]])
        waiting["c"], token = start("all_gather_start_c", shards, part_c, False, token)
        views.update(_gathered_views(got, gather_next(0, token)))

        def weight(name, after):
            for part in ("b", "c"):
                if name not in views and part in waiting:
                    views.update(_gathered_views(finish("all_gather_wait_" + part, waiting.pop(part), after)))
            return views[name]

        return weight

    small_l = lambda t, l: _pack_small({n: t[n][l] for n in SMALL})
    scatter_flags = lambda names: tuple(n != "small" for n in names)
    late = ("gu1", "d1", "small")
    early = tuple(g for g in GROUPS if g not in late)

    def layer_partial(l, wg):
        if l > 0:
            return None
        scatters["early"], token = start("reduce_scatter_start_a", _grad_groups(wg), early, scatter_flags(early),
                                         wg["w_in"])
        return token[0, 0]

    def layer_done(l, wg, gr, dh):
        if l + 1 < depth:
            recv[l + 1] = finish("reduce_scatter_wait", scatters.pop(l + 1), dh)
        groups = dict(_grad_groups(wg), small=_pack_small(gr))
        names = late if l == 0 else GROUPS + ("small",)
        scatters[l], tokens[l] = start("reduce_scatter_start_b" if l == 0 else "reduce_scatter_start", groups, names,
                                       scatter_flags(names), dh)
        return tokens[l][0, 0]

    loss, dx, w_grads, small_grads = _local_step(x[0], p[:, 0], positions[0], loss_target[0], w, layer_weights,
                                                 layer_done, layer_partial)
    loss = lax.psum(loss, MESH_AXES)

    results = {n: None for n in SHARDED}
    last = tokens[0]
    for l in reversed(range(1, depth)):
        for n in SHARDED:
            results[n] = _update_sharded(n, recv[l], w[n], mom[n], var[n], l, results[n], tokens[0])
            last = results[n][0]
    recv[0] = dict(finish("reduce_scatter_wait_a", scatters.pop("early"), last))
    recv[0].update(finish("reduce_scatter_wait_b", scatters.pop(0), last))
    for n in SHARDED:
        results[n] = _update_sharded(n, recv[0], w[n], mom[n], var[n], 0, results[n])
    result = dict(results)

    recv_small = jnp.concatenate([recv[l]["small"] for l in range(depth)], axis=1)
    packed = [jnp.concatenate([small_l(t, l) for l in range(depth)], axis=0) for t in (w, mom, var)]
    small_out = _adamw("adamw_small", recv_small, (N_DEV, 8, SMALL_LANES), lambda i: (0, i, 0), *packed, 8)
    for q in range(4):
        flat = small_out[q].reshape(depth, -1)
        off = 0
        for n in SMALL:
            size = math.prod(w[n].shape[1:])
            result.setdefault(n, [None] * 4)[q] = flat[:, off:off + size].reshape(w[n].shape)
            off += size

    outputs = [loss, dx[None]]
    for q in range(4):
        outputs += [result[n][q] for n in WEIGHTS]
    return tuple(outputs)
```

```python
import functools
import math

import jax
import jax.numpy as jnp
from jax import lax
from jax.experimental import pallas as pl
from jax.experimental.pallas import tpu as pltpu

F32 = jnp.float32
BF16 = jnp.bfloat16

N_DEV = 8
D_MODEL = 1024
D_FF = 2816
FF_SHARD = D_FF // N_DEV
D_ATTN = 512
D_SSM = 512
HEAD_DIM = 64
N_HEADS = 8
ROPE_DIM = 16
ROPE_THETA = 500000.0
DILATIONS = (1, 4, 16)
BAND = 128
N_GROUPS = 32
SSM_GROUP = 16
SSM_STATE = 64
N_STATE = N_GROUPS * SSM_STATE
PLE_DIM = 256
NORM_EPS = 1e-6
ADAM_LR, ADAM_B1, ADAM_B2, ADAM_EPS, ADAM_WD, ADAM_STEP = 0.001, 0.9, 0.999, 1e-08, 0.01, 10

VMEM_LIMIT_BYTES = 52 * 1024 * 1024
TM = 512
MESH_AXES = ("x", "y", "c")

WEIGHTS = ['ffn1_pre_g', 'ffn1_w_gate', 'ffn1_w_up', 'ffn1_w_down', 'ffn1_post_g', 'mix_pre_g', 'w_in', 'attn_norm_g',
           'ssm_lam_re', 'ssm_lam_im', 'ssm_log_dt', 'ssm_b_re', 'ssm_b_im', 'ssm_c_re', 'ssm_c_im', 'ssm_d',
           'ssm_w_glu', 'ssm_b_glu', 'ssm_norm_g', 'w_out', 'mix_post_g', 'ffn2_pre_g', 'ffn2_w_gate', 'ffn2_w_up',
           'ffn2_w_down', 'ffn2_post_g', 'ple_w_up', 'ple_w_gate', 'ple_post_g']
SHARDED = ['ffn1_w_gate', 'ffn1_w_up', 'ffn1_w_down', 'w_in', 'ssm_w_glu', 'w_out', 'ffn2_w_gate', 'ffn2_w_up',
           'ffn2_w_down', 'ple_w_up', 'ple_w_gate']
SMALL = [n for n in WEIGHTS if n not in SHARDED]
SMALL_LANES = 1024
SMALL_PAD = 8 * SMALL_LANES


def _params(n_axes):
    return pltpu.CompilerParams(dimension_semantics=("arbitrary",) * n_axes, vmem_limit_bytes=VMEM_LIMIT_BYTES)


_DIMS = {"nn": (((1,), (0,)), ((), ())), "nt": (((1,), (1,)), ((), ())), "tn": (((0,), (0,)), ((), ()))}


def _dot(a, b, mode):
    return lax.dot_general(a.astype(BF16), b.astype(BF16), _DIMS[mode], preferred_element_type=F32)


def _store(out_refs, vals, kinds, first):
    for ref, val, kind in zip(out_refs, vals, kinds):
        if isinstance(val, (list, tuple)):
            for q, piece in enumerate(val):
                ref[q] = piece.astype(ref.dtype)
        elif kind == "row":
            ref[...] = val.astype(ref.dtype)
        else:
            @pl.when(first)
            def _(ref=ref, val=val):
                ref[...] = val.astype(ref.dtype)

            @pl.when(jnp.logical_not(first))
            def _(ref=ref, val=val):
                ref[...] += val.astype(ref.dtype)


def _call(name, body, grid, ins, outs, scratch, into):
    arrays = [t[0] for t in ins]
    in_specs = [_in_spec(t) for t in ins]
    aliases = {}
    for arr, k in into:
        arrays.append(arr)
        in_specs.append(pl.BlockSpec(memory_space=pl.ANY))
        aliases[len(arrays) - 1] = k
    return pl.pallas_call(
        body,
        name=name,
        grid=grid,
        in_specs=in_specs,
        out_specs=[pl.BlockSpec(o[2], o[3]) for o in outs],
        out_shape=[jax.ShapeDtypeStruct(o[0], o[1]) for o in outs],
        scratch_shapes=list(scratch),
        input_output_aliases=aliases,
        compiler_params=_params(len(grid)),
    )(*arrays)


def _mm(name, grid, a, b, mode, outs, extras=(), epilogue=None, acc_shape=None, scratch=(), into=()):
    nk = grid[2]
    ne, no = len(extras), len(outs)
    kinds = [o[4] for o in outs]
    assert all(k == "row" for k in kinds) or grid[1] == 1
    n_in = 2 + ne + len(into)
    n_acc = int(nk > 1)

    def body(*refs):
        a_ref, b_ref = refs[0], refs[1]
        ex_refs = refs[2:2 + ne]
        out_refs = refs[n_in:n_in + no]
        scr_refs = refs[n_in + no + n_acc:]
        part = _dot(a_ref[...], b_ref[...], mode)
        first = pl.program_id(0) == 0

        def finish(acc):
            vals = epilogue(acc, *[r[...] for r in ex_refs], *scr_refs) if epilogue is not None else (acc,)
            _store(out_refs, vals, kinds, first)

        if nk == 1:
            finish(part)
        else:
            acc_ref = refs[n_in + no]
            k = pl.program_id(2)

            @pl.when(k == 0)
            def _():
                acc_ref[...] = part

            @pl.when(k > 0)
            def _():
                acc_ref[...] += part

            @pl.when(k == nk - 1)
            def _():
                finish(acc_ref[...])

    acc = [pltpu.VMEM(acc_shape, F32)] if nk > 1 else []
    return _call(name, body, grid, (a, b) + tuple(extras), outs, acc + list(scratch), into)


def _rowwise(name, n_rows, tm, ins, outs, fn, scratch=(), into=()):
    kinds = [o[4] for o in outs]
    ni, no = len(ins), len(outs)
    n_in = ni + len(into)

    def body(*refs):
        vals = fn(*[r[...] for r in refs[:ni]], *refs[n_in + no:])
        _store(refs[n_in:n_in + no], vals, kinds, pl.program_id(0) == 0)

    return _call(name, body, (n_rows // tm,), ins, outs, scratch, into)


def _rows(arr, tm, col=0, width=None):
    width = arr.shape[1] if width is None else width
    return (arr, (tm, width), lambda i, *_: (i, col))


def _whole(arr):
    nd = arr.ndim
    return (arr, arr.shape, lambda *_: (0,) * nd)


def _resident(arr):
    nd = arr.ndim
    return (arr, arr.shape, lambda *_: (0,) * nd, dict(pipeline_mode=pl.Buffered(1)))


def _in_spec(t):
    return pl.BlockSpec(t[1], t[2], **(t[3] if len(t) > 3 else {}))


def _row_out(n_rows, width, dtype, tm, col=0, total=None):
    return ((n_rows, width if total is None else total), dtype, (tm, width), lambda i, *_: (i, col), "row")


def _col_out(width):
    return ((1, width), F32, (1, width), lambda *_: (0, 0), "colsum")


def _column_shards(acc, width):
    return [acc[:, q * width:(q + 1) * width] for q in range(acc.shape[1] // width)]


def _natural_cols(name, parts, after=None):
    rows, width = parts[0].shape[1:]
    tr = min(rows, 256)
    n = len(parts)

    def body(*refs):
        out_ref = refs[-1]
        for t in range(n):
            for j in range(N_DEV):
                lo = (t * N_DEV + j) * width
                out_ref[:, lo:lo + width] = refs[t][j]

    ins = [(t, (N_DEV, tr, width), lambda i: (0, i, 0)) for t in parts]
    if after is not None:
        ins.append(_whole(after))
    out = _row_out(rows, n * N_DEV * width, parts[0].dtype, tr)
    return _call(name, body, (rows // tr,), ins, [out], (), ())[0]


def _rms(x, g):
    return x * lax.rsqrt(jnp.mean(x * x, axis=-1, keepdims=True) + NORM_EPS) * g


def _sigmoid(x):
    return 1.0 / (1.0 + jnp.exp(-x))


def _gelu(x):
    return 0.5 * x * (1.0 + jnp.tanh(0.7978845608028654 * (x + 0.044715 * x * x * x)))


FFN_TM = 256
FF_HALF = D_FF // 2


def _ffn_fwd(s_len, h, n, weight, g_post, g_next):
    def gu_epi(acc):
        gate, up = acc[:, :D_FF], acc[:, D_FF:]
        return acc, gate * _sigmoid(gate) * up

    w_gu = weight("gu", n)
    gu, act = _mm("ffn_gate_up", (s_len // FFN_TM, 1, 1), _rows(n, FFN_TM), _resident(w_gu), "nn",
                  [_row_out(s_len, 2 * D_FF, BF16, FFN_TM), _row_out(s_len, D_FF, BF16, FFN_TM)], epilogue=gu_epi)
    w_d = weight("d", act)

    def down_epi(acc, h_blk, gp, gn):
        h_new = h_blk + 0.5 * _rms(acc, gp)
        return acc, h_new, _rms(h_new, gn)

    f, h_new, n_next = _mm(
        "ffn_down", (s_len // TM, 1, 1), _rows(act, TM), _resident(w_d), "nn",
        [_row_out(s_len, D_MODEL, F32, TM), _row_out(s_len, D_MODEL, F32, TM), _row_out(s_len, D_MODEL, BF16, TM)],
        extras=[_rows(h, TM), _whole(g_post), _whole(g_next)], epilogue=down_epi)
    return h_new, n_next, dict(n=n, gu=gu, act=act, f=f, h=h)


def _ffn_bwd(s_len, saved, df, w_gu, w_d, final_epi, final_extras, final_outs):
    nt = s_len // TM

    def act_epi(acc, gu_blk):
        gate = gu_blk[:, :D_FF].astype(F32)
        up = gu_blk[:, D_FF:].astype(F32)
        sg = _sigmoid(gate)
        dgate = acc * up * sg * (1.0 + gate * (1.0 - sg))
        dup = acc * gate * sg
        return (jnp.concatenate([dgate, dup], axis=1),)

    (dgu,) = _mm("ffn_bwd_act", (s_len // FFN_TM, 1, 1), _rows(df, FFN_TM), _resident(w_d), "nt",
                 [_row_out(s_len, 2 * D_FF, BF16, FFN_TM)], extras=[_rows(saved["gu"], FFN_TM)], epilogue=act_epi)

    (d_w_d,) = _mm(
        "ffn_bwd_wdown", (2, 1, nt), (saved["act"], (TM, FF_HALF), lambda i, j, k: (k, i)),
        (df, (TM, D_MODEL), lambda i, j, k: (k, 0)), "tn",
        [((D_FF, D_MODEL), BF16, (FF_HALF, D_MODEL), lambda i, j, k: (i, 0), "row")], acc_shape=(FF_HALF, D_MODEL))

    def w_grad(name, half):
        (dw,) = _mm(
            name, (1, 2, nt), (saved["n"], (TM, D_MODEL), lambda i, j, k: (k, 0)),
            (dgu, (TM, FF_HALF), lambda i, j, k: (k, 2 * half + j)), "tn",
            [((N_DEV, D_MODEL, FF_SHARD), BF16, (N_DEV // 2, D_MODEL, FF_SHARD), lambda i, j, k: (j, 0, 0), "row")],
            epilogue=lambda acc: (_column_shards(acc, FF_SHARD),), acc_shape=(D_MODEL, FF_HALF))
        return dw

    d_w_gate, d_w_up = w_grad("ffn_bwd_wgate", 0), w_grad("ffn_bwd_wup", 1)

    outs = _mm("ffn_bwd_dn", (s_len // FFN_TM, 1, 1), _rows(dgu, FFN_TM), _resident(w_gu), "nt",
               final_outs, extras=final_extras, epilogue=final_epi)
    return d_w_gate, d_w_up, d_w_d, outs


def _band_mask(first_block):
    qi = lax.broadcasted_iota(jnp.int32, (BAND, 2 * BAND), 0)
    kj = lax.broadcasted_iota(jnp.int32, (BAND, 2 * BAND), 1)
    ok = (kj >= qi) & (kj <= qi + BAND)
    return ok & (jnp.logical_not(first_block) | (kj >= BAND))


def _attn_views(s_len, d):
    n_str = s_len // d
    nb = n_str // BAND
    blk = (BAND, D_ATTN)
    cur = lambda r, b: (b, r)
    prev = lambda r, b: (jnp.maximum(b - 1, 0), r)
    return n_str, nb, blk, cur, prev


STREAM_LANES = 128


def _stream_scratch(tm):
    return pltpu.VMEM((D_ATTN // STREAM_LANES, tm, STREAM_LANES), F32)


def _to_streams(x, d, scr):
    if d == 1:
        return x
    tm = x.shape[0]
    chunks = D_ATTN // STREAM_LANES
    for c in range(chunks):
        scr[c, 0:tm, :] = x[:, c * STREAM_LANES:(c + 1) * STREAM_LANES]
    return jnp.concatenate([scr.at[c][pl.ds(r, tm // d, stride=d), :] for r in range(d) for c in range(chunks)], axis=1)


def _from_streams(v, d, scr):
    if d == 1:
        return v
    rows = v.shape[0]
    chunks = D_ATTN // STREAM_LANES
    for r in range(d):
        for c in range(chunks):
            lo = r * D_ATTN + c * STREAM_LANES
            scr.at[c][pl.ds(r, rows, stride=d), :] = v[:, lo:lo + STREAM_LANES]
    return jnp.concatenate([scr[c, 0:rows * d, :] for c in range(chunks)], axis=1)


def _stream_in(arr, tm, d, ahead=0, n_blocks=None):
    rows = tm // d
    step = ahead // rows
    if ahead:
        return (arr, (rows, d * D_ATTN), lambda i, *_: (jnp.minimum(i + step, n_blocks - 1), 0))
    return (arr, (rows, d * D_ATTN), lambda i, *_: (i, 0))


def _stream_out(s_len, dtype, tm, d):
    return ((s_len // d, d * D_ATTN), dtype, (tm // d, d * D_ATTN), lambda i, *_: (i, 0), "row")


def _attn_fwd(s_len, q, k, v, d):
    n_str, nb, blk, cur, prev = _attn_views(s_len, d)
    view = lambda t: t

    def body(q_ref, kp_ref, kc_ref, vp_ref, vc_ref, o_ref, lse_ref):
        mask = _band_mask(pl.program_id(1) == 0)
        qq = q_ref[...]
        kk = jnp.concatenate([kp_ref[...], kc_ref[...]], axis=0)
        vv = jnp.concatenate([vp_ref[...], vc_ref[...]], axis=0)
        for h in range(N_HEADS):
            sl = slice(h * HEAD_DIM, (h + 1) * HEAD_DIM)
            s = _dot(qq[:, sl], kk[:, sl], "nt") * (HEAD_DIM ** -0.5)
            s = jnp.where(mask, s, -1e30)
            m = jnp.max(s, axis=-1, keepdims=True)
            e = jnp.exp(s - m)
            den = jnp.sum(e, axis=-1, keepdims=True)
            o_ref[:, sl] = _dot(e / den, vv[:, sl], "nn")
            lse_ref[:, sl] = jnp.broadcast_to(m + jnp.log(den), (BAND, HEAD_DIM))

    o, lse = pl.pallas_call(
        body,
        name=f"attn_fwd_d{d}",
        grid=(d, nb),
        in_specs=[pl.BlockSpec(blk, cur), pl.BlockSpec(blk, prev), pl.BlockSpec(blk, cur),
                  pl.BlockSpec(blk, prev), pl.BlockSpec(blk, cur)],
        out_specs=[pl.BlockSpec(blk, cur), pl.BlockSpec(blk, cur)],
        out_shape=[jax.ShapeDtypeStruct((n_str, d * D_ATTN), F32)] * 2,
        compiler_params=_params(2),
    )(view(q), view(k), view(k), view(v), view(v))
    return o, lse


def _attn_bwd(s_len, q, k, v, dattn, attn, lse, d):
    n_str, nb, blk, cur, prev = _attn_views(s_len, d)
    view = lambda t: t

    def body(q_ref, kp_ref, kc_ref, vp_ref, vc_ref, da_ref, at_ref, lse_ref, dq_ref, dka_ref, dkb_ref, dva_ref, dvb_ref):
        mask = _band_mask(pl.program_id(1) == 0)
        qq = q_ref[...]
        kk = jnp.concatenate([kp_ref[...], kc_ref[...]], axis=0)
        vv = jnp.concatenate([vp_ref[...], vc_ref[...]], axis=0)
        da = da_ref[...]
        prod = da * at_ref[...]
        scale = HEAD_DIM ** -0.5
        for h in range(N_HEADS):
            sl = slice(h * HEAD_DIM, (h + 1) * HEAD_DIM)
            s = _dot(qq[:, sl], kk[:, sl], "nt") * scale
            p = jnp.where(mask, jnp.exp(s - lse_ref[:, h * HEAD_DIM:h * HEAD_DIM + 1]), 0.0)
            dp = _dot(da[:, sl], vv[:, sl], "nt")
            ds = p * (dp - jnp.sum(prod[:, sl], axis=-1, keepdims=True))
            dq_ref[:, sl] = _dot(ds, kk[:, sl], "nn") * scale
            dk = _dot(ds, qq[:, sl], "tn") * scale
            dv = _dot(p, da[:, sl], "tn")
            dkb_ref[:, sl] = dk[:BAND]
            dka_ref[:, sl] = dk[BAND:]
            dvb_ref[:, sl] = dv[:BAND]
            dva_ref[:, sl] = dv[BAND:]

    outs = pl.pallas_call(
        body,
        name=f"attn_bwd_d{d}",
        grid=(d, nb),
        in_specs=[pl.BlockSpec(blk, cur), pl.BlockSpec(blk, prev), pl.BlockSpec(blk, cur),
                  pl.BlockSpec(blk, prev), pl.BlockSpec(blk, cur),
                  pl.BlockSpec(blk, cur), pl.BlockSpec(blk, cur), pl.BlockSpec(blk, cur)],
        out_specs=[pl.BlockSpec(blk, cur)] * 5,
        out_shape=[jax.ShapeDtypeStruct((n_str, d * D_ATTN), F32)] * 5,
        compiler_params=_params(2),
    )(view(q), view(k), view(k), view(v), view(v), view(dattn), view(attn), view(lse))
    return list(outs)


def _rope_tables(positions):
    half = ROPE_DIM // 2
    inv_freq = ROPE_THETA ** (-jnp.arange(half, dtype=F32) * (2.0 / ROPE_DIM))
    ang = positions.astype(F32)[:, None] * inv_freq
    cos, sin = jnp.cos(ang), jnp.sin(ang)
    s_len = positions.shape[0]
    one = jnp.ones((s_len, HEAD_DIM - ROPE_DIM), F32)
    zero8 = jnp.zeros((s_len, half), F32)
    zero = jnp.zeros((s_len, HEAD_DIM - ROPE_DIM), F32)
    c = jnp.concatenate([cos, cos, one], axis=1)
    s1 = jnp.concatenate([zero8, sin, zero], axis=1)
    s2 = jnp.concatenate([-sin, zero8, zero], axis=1)
    tile = lambda t: jnp.tile(t, (1, N_HEADS))
    return tile(c), tile(s1), tile(s2)


def _rope(t, c, s1, s2):
    half = ROPE_DIM // 2
    return t * c + pltpu.roll(t, half, 1) * s1 + pltpu.roll(t, D_ATTN - half, 1) * s2


def _rope_transposed(dt, c, s1, s2):
    half = ROPE_DIM // 2
    return dt * c + pltpu.roll(dt * s1, D_ATTN - half, 1) + pltpu.roll(dt * s2, half, 1)


SCAN_ROWS = 256
SCAN_LANES = 512


def _scan(name, bu, a_cat, reverse):
    s_len, width = bu.shape
    nb = s_len // SCAN_ROWS
    half = width // 2

    def body(bu_ref, a_ref, x_ref, carry_ref):
        @pl.when(pl.program_id(0) == 0)
        def _():
            carry_ref[...] = jnp.zeros_like(carry_ref)

        for c in range(half // SCAN_LANES):
            re = pl.ds(c * SCAN_LANES, SCAN_LANES)
            im = pl.ds(half + c * SCAN_LANES, SCAN_LANES)
            ar = a_ref[:, re]
            ai = a_ref[:, im]

            def step(s, state):
                xr, xi = state
                t = (SCAN_ROWS - 1 - s) if reverse else s
                row = pl.ds(t, 1)
                nr = ar * xr - ai * xi + bu_ref[row, re]
                ni = ar * xi + ai * xr + bu_ref[row, im]
                x_ref[row, re] = nr
                x_ref[row, im] = ni
                return nr, ni

            xr, xi = lax.fori_loop(0, SCAN_ROWS, step, (carry_ref[0:1, re], carry_ref[0:1, im]), unroll=8)
            carry_ref[0:1, re] = xr
            carry_ref[0:1, im] = xi

    imap = (lambda i: (nb - 1 - i, 0)) if reverse else (lambda i: (i, 0))
    return pl.pallas_call(
        body,
        name=name,
        grid=(nb,),
        in_specs=[pl.BlockSpec((SCAN_ROWS, width), imap), pl.BlockSpec((1, width), lambda i: (0, 0))],
        out_specs=pl.BlockSpec((SCAN_ROWS, width), imap),
        out_shape=jax.ShapeDtypeStruct((s_len, width), F32),
        scratch_shapes=[pltpu.VMEM((8, width), F32)],
        compiler_params=_params(1),
    )(bu, a_cat)


def _ssm_params(lam_re, lam_im, log_dt, b_re, b_im, c_re, c_im):
    dt = jnp.exp(log_dt)[:, None]
    er = jnp.exp(lam_re * dt)
    a_re = er * jnp.cos(lam_im * dt)
    a_im = er * jnp.sin(lam_im * dt)
    nr, ni = a_re - 1.0, a_im
    den = lam_re * lam_re + lam_im * lam_im
    fr = (nr * lam_re + ni * lam_im) / den
    fi = (ni * lam_re - nr * lam_im) / den
    bb_re = fr[..., None] * b_re - fi[..., None] * b_im
    bb_im = fr[..., None] * b_im + fi[..., None] * b_re
    eye = jnp.eye(N_GROUPS, dtype=F32)

    def in_mat(bb):
        t = bb.transpose(0, 2, 1)[:, :, None, :] * eye[:, None, :, None]
        return t.reshape(D_SSM, N_STATE)

    def out_mat(cc):
        t = cc.transpose(0, 2, 1)[:, :, None, :] * eye[:, None, :, None]
        return t.reshape(N_STATE, D_SSM)

    w_b = jnp.concatenate([in_mat(bb_re), in_mat(bb_im)], axis=1)
    w_c = jnp.concatenate([out_mat(c_re), -out_mat(c_im)], axis=0)
    a_cat = jnp.concatenate([a_re.reshape(1, N_STATE), a_im.reshape(1, N_STATE)], axis=1)
    return a_cat, w_b, w_c


def _conj(a_cat):
    return jnp.concatenate([a_cat[:, :N_STATE], -a_cat[:, N_STATE:]], axis=1)


def _layer_fwd(s_len, h, n1, p_l, rope, weight, sp, gains, g_next):
    nt = s_len // TM
    sv = {}
    gw = {}

    def need(after, *names):
        for name in names:
            gw[name] = weight(name, after)

    def ffn_weight(which):
        def get(kind, after):
            need(after, f"w_{kind}{which}")
            return gw[f"w_{kind}{which}"]
        return get

    h1, a_in, sv["ffn1"] = _ffn_fwd(s_len, h, n1, ffn_weight(1), gains["ffn1_post_g"], gains["mix_pre_g"])

    need(h1, "w_in", "w_glu", "w_out")
    (proj,) = _mm("w_in", (nt, 1, 1), _rows(a_in, TM), _resident(gw["w_in"]), "nn",
                  [_row_out(s_len, 2 * D_MODEL, F32, TM)])

    c, s1, s2 = rope

    def rope_fn(tq, tk, tv, cc, a1, a2, scr):
        q, k = _rope(tq, cc, a1, a2), _rope(tk, cc, a1, a2)
        return [_to_streams(t, d, scr) for d in DILATIONS for t in (q, k, tv)]

    qkv = _rowwise(
        "rope", s_len, TM,
        [_rows(proj, TM, 0, D_ATTN), _rows(proj, TM, 1, D_ATTN), _rows(proj, TM, 2, D_ATTN),
         _rows(c, TM), _rows(s1, TM), _rows(s2, TM)],
        [_stream_out(s_len, BF16, TM, d) for d in DILATIONS for _ in range(3)], rope_fn,
        scratch=[_stream_scratch(TM)])
    qkv = {d: qkv[3 * i:3 * i + 3] for i, d in enumerate(DILATIONS)}
    parts = {d: _attn_fwd(s_len, *qkv[d], d) for d in DILATIONS}

    def mix_fn(*args):
        g, scr = args[6], args[7]
        o1, l1, o2, l2, o3, l3 = [_from_streams(args[2 * i + j], d, scr) for i, d in enumerate(DILATIONS)
                                  for j in range(2)]
        m = jnp.maximum(jnp.maximum(l1, l2), l3)
        e1, e2, e3 = jnp.exp(l1 - m), jnp.exp(l2 - m), jnp.exp(l3 - m)
        tot = e1 + e2 + e3
        attn = (e1 * o1 + e2 * o2 + e3 * o3) / tot
        lse = m + jnp.log(tot)
        return [attn, lse, _rms(attn, g)] + [_to_streams(t, d, scr) for d in DILATIONS[1:] for t in (attn, lse)]

    mix_out = _rowwise(
        "attn_mix", s_len, TM,
        [_stream_in(t, TM, d) for d in DILATIONS for t in parts[d]] + [_whole(gains["attn_norm_g"])],
        [_row_out(s_len, D_ATTN, F32, TM), _row_out(s_len, D_ATTN, F32, TM),
         _row_out(s_len, D_ATTN, BF16, TM, col=0, total=D_MODEL)]
        + [_stream_out(s_len, F32, TM, d) for d in DILATIONS[1:] for _ in range(2)], mix_fn,
        scratch=[_stream_scratch(TM)])
    attn, lse, mixed_half = mix_out[:3]
    attn_s = {1: (attn, lse), DILATIONS[1]: tuple(mix_out[3:5]), DILATIONS[2]: tuple(mix_out[5:7])}

    a_cat, w_b, w_c, d_vec = sp
    ts = SCAN_ROWS
    (bu,) = _mm("ssm_bu", (s_len // ts, 1, 1), (proj, (ts, D_SSM), lambda i, j, k: (i, 3)), _whole(w_b), "nn",
                [_row_out(s_len, 2 * N_STATE, F32, ts)])
    xs = _scan("ssm_scan", bu, a_cat, False)

    def y_epi(acc, u, dv):
        z = acc + dv * u
        return z, _gelu(z)

    z, yg = _mm("ssm_y", (s_len // ts, 1, 1), _rows(xs, ts), _whole(w_c), "nn",
                [_row_out(s_len, D_SSM, F32, ts)] * 2,
                extras=[(proj, (ts, D_SSM), lambda i, j, k: (i, 3)), _whole(d_vec)], epilogue=y_epi)

    def glu_epi(acc, y, b, g):
        t = acc + b
        ssm = y * _sigmoid(t)
        return t, ssm, _rms(ssm, g)

    t_glu, ssm, mixed = _mm(
        "ssm_glu", (nt, 1, 1), _rows(yg, TM), _whole(gw["w_glu"]), "nn",
        [_row_out(s_len, D_SSM, F32, TM), _row_out(s_len, D_SSM, F32, TM),
         _row_out(s_len, D_SSM, BF16, TM, col=1, total=D_MODEL)],
        extras=[_rows(yg, TM), _whole(gains["ssm_b_glu"]), _whole(gains["ssm_norm_g"])], epilogue=glu_epi,
        into=[(mixed_half, 2)])

    def out_epi(acc, h_blk, gp, gn):
        h_new = h_blk + _rms(acc, gp)
        return acc, h_new, _rms(h_new, gn)

    o, h2, n2 = _mm(
        "w_out", (nt, 1, 1), _rows(mixed, TM), _whole(gw["w_out"]), "nn",
        [_row_out(s_len, D_MODEL, F32, TM), _row_out(s_len, D_MODEL, F32, TM), _row_out(s_len, D_MODEL, BF16, TM)],
        extras=[_rows(h1, TM), _whole(gains["mix_post_g"]), _whole(gains["ffn2_pre_g"])], epilogue=out_epi)

    h3, _, sv["ffn2"] = _ffn_fwd(s_len, h2, n2, ffn_weight(2), gains["ffn2_post_g"], gains["ffn2_post_g"])
    need(h3, "w_pu", "ple_w_gate")

    (pu,) = _mm("ple_up", (nt, 1, 1), _rows(p_l, TM), _resident(gw["w_pu"]), "nn",
                [_row_out(s_len, D_MODEL, F32, TM)])

    def ple_epi(acc, pu_blk, h_blk, gp, gn):
        h_new = h_blk + _rms(pu_blk * _sigmoid(acc), gp)
        return acc, h_new, _rms(h_new, gn)

    gt, h4, n_next = _mm(
        "ple_gate", (nt, 1, 1), _rows(h3, TM), _whole(gw["ple_w_gate"]), "nn",
        [_row_out(s_len, D_MODEL, F32, TM), _row_out(s_len, D_MODEL, F32, TM), _row_out(s_len, D_MODEL, BF16, TM)],
        extras=[_rows(pu, TM), _rows(h3, TM), _whole(gains["ple_post_g"]), _whole(g_next)], epilogue=ple_epi)

    sv.update(h1=h1, a_in=a_in, proj=proj, qkv=qkv, attn=attn, attn_s=attn_s, xs=xs, z=z, yg=yg, t_glu=t_glu, ssm=ssm,
              mixed=mixed, o=o, h2=h2, h3=h3, pu=pu, gt=gt, p_l=p_l, w=gw)
    return h4, n_next, sv


def _vjp(fn, args, cot):
    _, pull = jax.vjp(fn, *args)
    return pull(cot)


def _layer_bwd(s_len, dh4, sv, rope, gw, sp, gains, on_partial=None):
    nt = s_len // TM
    gr = {}
    wg = {}

    def ple_fn(dh, pu, gt, g):
        dpu, dgt, dg = _vjp(lambda a, b, c: _rms(a * _sigmoid(b), c), (pu, gt, g), dh)
        return dpu, dgt, dg

    dpu, dgt, gr["ple_post_g"] = _rowwise(
        "ple_bwd", s_len, TM, [_rows(dh4, TM), _rows(sv["pu"], TM), _rows(sv["gt"], TM), _whole(gains["ple_post_g"])],
        [_row_out(s_len, D_MODEL, BF16, TM), _row_out(s_len, D_MODEL, BF16, TM), _col_out(D_MODEL)], ple_fn)

    def square_w_grad(name, lhs, rhs):
        half = D_MODEL // 2
        (dw,) = _mm(name, (2, 1, nt), (lhs, (TM, half), lambda i, j, k: (k, i)),
                    (rhs, (TM, D_MODEL), lambda i, j, k: (k, 0)), "tn",
                    [((D_MODEL, D_MODEL), BF16, (half, D_MODEL), lambda i, j, k: (i, 0), "row")],
                    acc_shape=(half, D_MODEL))
        return dw

    (wg["ple_w_up"],) = _mm(
        "ple_bwd_wup", (1, 1, nt), (sv["p_l"], (TM, PLE_DIM), lambda i, j, k: (k, 0)),
        (dpu, (TM, D_MODEL), lambda i, j, k: (k, 0)), "tn",
        [((N_DEV, PLE_DIM, D_MODEL // N_DEV), BF16, (N_DEV, PLE_DIM, D_MODEL // N_DEV), lambda i, j, k: (0, 0, 0),
          "row")],
        epilogue=lambda acc: (_column_shards(acc, D_MODEL // N_DEV),), acc_shape=(PLE_DIM, D_MODEL))
    wg["ple_w_gate"] = square_w_grad("ple_bwd_wgate", sv["h3"], dgt)

    def ple_dx_epi(acc, dh, f, g):
        dh3 = dh + acc
        df, dg = _vjp(_rms, (f, g), 0.5 * dh3)
        return dh3, df, dg

    dh3, df2, gr["ffn2_post_g"] = _mm(
        "ple_bwd_dx", (nt, 1, 1), _rows(dgt, TM), _whole(gw["ple_w_gate"]), "nt",
        [_row_out(s_len, D_MODEL, F32, TM), _row_out(s_len, D_MODEL, BF16, TM), _col_out(D_MODEL)],
        extras=[_rows(dh4, TM), _rows(sv["ffn2"]["f"], TM), _whole(gains["ffn2_post_g"])], epilogue=ple_dx_epi)

    def ffn2_final(dn, dh, h, g_pre, o, g_post):
        dx, dg_pre = _vjp(_rms, (h, g_pre), dn)
        dh2 = dh + dx
        do, dg_post = _vjp(_rms, (o, g_post), dh2)
        return dh2, do, dg_pre, dg_post

    wg["ffn2_w_gate"], wg["ffn2_w_up"], wg["ffn2_w_down"], (dh2, do, gr["ffn2_pre_g"], gr["mix_post_g"]) = _ffn_bwd(
        s_len, sv["ffn2"], df2, gw["w_gu2"], gw["w_d2"], ffn2_final,
        [_rows(dh3, FFN_TM), _rows(sv["h2"], FFN_TM), _whole(gains["ffn2_pre_g"]), _rows(sv["o"], FFN_TM),
         _whole(gains["mix_post_g"])],
        [_row_out(s_len, D_MODEL, F32, FFN_TM), _row_out(s_len, D_MODEL, BF16, FFN_TM), _col_out(D_MODEL),
         _col_out(D_MODEL)])

    wg["w_out"] = square_w_grad("w_out_bwd_w", sv["mixed"], do)

    def mixed_epi(acc, attn, ssm, yg, t, g_a, g_s, scr):
        dattn, dg_a = _vjp(_rms, (attn, g_a), acc[:, :D_ATTN])
        dssm, dg_s = _vjp(_rms, (ssm, g_s), acc[:, D_ATTN:])
        sg = _sigmoid(t)
        dt = dssm * yg * sg * (1.0 - sg)
        return ([_to_streams(dattn, d, scr) for d in DILATIONS]
                + [dt, dssm * sg, dg_a, dg_s, jnp.sum(dt, axis=0, keepdims=True)])

    res = _mm(
        "w_out_bwd_x", (nt, 1, 1), _rows(do, TM), _whole(gw["w_out"]), "nt",
        [_stream_out(s_len, F32, TM, d) for d in DILATIONS]
        + [_row_out(s_len, D_SSM, BF16, TM), _row_out(s_len, D_SSM, F32, TM),
           _col_out(D_ATTN), _col_out(D_SSM), _col_out(D_SSM)],
        extras=[_rows(sv["attn"], TM), _rows(sv["ssm"], TM), _rows(sv["yg"], TM), _rows(sv["t_glu"], TM),
                _whole(gains["attn_norm_g"]), _whole(gains["ssm_norm_g"])], epilogue=mixed_epi,
        scratch=[_stream_scratch(TM)])
    dattn_s = dict(zip(DILATIONS, res[:3]))
    dt_glu, dyg_dir, gr["attn_norm_g"], gr["ssm_norm_g"], gr["ssm_b_glu"] = res[3:]

    a_cat, w_b, w_c, d_vec = sp
    ts = SCAN_ROWS
    u_spec = (sv["proj"], (TM, D_SSM), lambda i, *_: (i, 3))
    (wg["ssm_w_glu"],) = _mm(
        "ssm_bwd_wglu", (1, 1, nt), (sv["yg"], (TM, D_SSM), lambda i, j, k: (k, 0)),
        (dt_glu, (TM, D_SSM), lambda i, j, k: (k, 0)), "tn",
        [((D_SSM, D_SSM), BF16, (D_SSM, D_SSM), lambda i, j, k: (0, 0), "row")], acc_shape=(D_SSM, D_SSM))

    def gelu_epi(acc, dy_dir, z, u, dv):
        (dz,) = _vjp(_gelu, (z,), acc + dy_dir)
        return dz, dz * dv, jnp.sum(dz * u, axis=0, keepdims=True)

    dz, du_dir, gr["ssm_d"] = _mm(
        "ssm_bwd_glu", (nt, 1, 1), _rows(dt_glu, TM), _whole(gw["w_glu"]), "nt",
        [_row_out(s_len, D_SSM, BF16, TM), _row_out(s_len, D_SSM, F32, TM), _col_out(D_SSM)],
        extras=[_rows(dyg_dir, TM), _rows(sv["z"], TM), u_spec, _whole(d_vec)], epilogue=gelu_epi)

    (d_w_c,) = _mm(
        "ssm_bwd_wc", (2 * N_STATE // 1024, 1, nt), (sv["xs"], (TM, 1024), lambda i, j, k: (k, i)),
        (dz, (TM, D_SSM), lambda i, j, k: (k, 0)), "tn",
        [((2 * N_STATE, D_SSM), F32, (1024, D_SSM), lambda i, j, k: (i, 0), "row")], acc_shape=(1024, D_SSM))
    (dxs,) = _mm("ssm_bwd_dx", (s_len // ts, 1, 1), _rows(dz, ts), _whole(w_c), "nt",
                 [_row_out(s_len, 2 * N_STATE, F32, ts)])
    gs = _scan("ssm_scan_rev", dxs, _conj(a_cat), True)

    def da_fn(x, x_before, g):
        i = pl.program_id(0)
        rolled = pltpu.roll(x, 1, 0)
        first_row = jnp.where(i > 0, x_before[7:8, :], 0.0)
        rows = lax.broadcasted_iota(jnp.int32, x.shape, 0)
        xp = jnp.where(rows == 0, first_row, rolled)
        xr, xi = xp[:, :N_STATE], xp[:, N_STATE:]
        g_r, g_i = g[:, :N_STATE], g[:, N_STATE:]
        d_re = jnp.sum(xr * g_r + xi * g_i, axis=0, keepdims=True)
        d_im = jnp.sum(xr * g_i - xi * g_r, axis=0, keepdims=True)
        return (jnp.concatenate([d_re, d_im], axis=1),)

    (d_a,) = _rowwise(
        "ssm_bwd_da", s_len, ts,
        [_rows(sv["xs"], ts), (sv["xs"], (8, 2 * N_STATE), lambda i: (jnp.maximum(i * (ts // 8) - 1, 0), 0)),
         _rows(gs, ts)], [_col_out(2 * N_STATE)], da_fn)

    (d_w_b,) = _mm(
        "ssm_bwd_wb", (1, 2, nt), (sv["proj"], (TM, D_SSM), lambda i, j, k: (k, 3)),
        (gs, (TM, N_STATE), lambda i, j, k: (k, j)), "tn",
        [((D_SSM, 2 * N_STATE), F32, (D_SSM, N_STATE), lambda i, j, k: (0, j), "row")], acc_shape=(D_SSM, N_STATE))
    sum_tm = 2 * BAND
    n_sum = s_len // sum_tm
    ins = []
    for d in DILATIONS:
        dq_p, dka, dkb, dva, dvb = _attn_bwd(s_len, *sv["qkv"][d], dattn_s[d], *sv["attn_s"][d], d)
        if d == 1:
            nxt = lambda t: (t, (sum_tm, D_ATTN), lambda i: (jnp.minimum(i + 1, n_sum - 1), 0))
            ins += [_rows(dq_p, sum_tm), _rows(dka, sum_tm), _rows(dkb, sum_tm), nxt(dkb), _rows(dva, sum_tm),
                    _rows(dvb, sum_tm), nxt(dvb)]
        else:
            ins += [_stream_in(dq_p, sum_tm, d), _stream_in(dka, sum_tm, d), _stream_in(dkb, sum_tm, d, BAND, n_sum),
                    _stream_in(dva, sum_tm, d), _stream_in(dvb, sum_tm, d, BAND, n_sum)]
    c, s1, s2 = rope
    ins += [_rows(c, sum_tm), _rows(s1, sum_tm), _rows(s2, sum_tm)]

    def qkv_fn(*args):
        i = pl.program_id(0)
        blocks, scr = args[:-1], args[-1]
        dq_t, dka, dkb, dkb_next, dva, dvb, dvb_next = blocks[:7]
        more = i + 1 < n_sum
        ahead = lambda cur, nxt: jnp.concatenate([cur[BAND:], jnp.where(more, nxt[:BAND], 0.0)], axis=0)
        dk_t = dka + ahead(dkb, dkb_next)
        dv_t = dva + ahead(dvb, dvb_next)
        at = 7
        for d in DILATIONS[1:]:
            dq_p, dka, dkb, dva, dvb = blocks[at:at + 5]
            at += 5
            live = i + BAND // (sum_tm // d) < n_sum
            dq_t = dq_t + _from_streams(dq_p, d, scr)
            dk_t = dk_t + _from_streams(dka + jnp.where(live, dkb, 0.0), d, scr)
            dv_t = dv_t + _from_streams(dva + jnp.where(live, dvb, 0.0), d, scr)
        cc, a1, a2 = blocks[at:at + 3]
        return (jnp.concatenate([_rope_transposed(dq_t, cc, a1, a2), _rope_transposed(dk_t, cc, a1, a2), dv_t], axis=1),)

    (dqkv,) = _rowwise(
        "attn_bwd_sum", s_len, sum_tm, ins,
        [((s_len, 2 * D_MODEL), BF16, (sum_tm, 3 * D_ATTN), lambda i: (i, 0), "row")], qkv_fn,
        scratch=[_stream_scratch(sum_tm)])
    (dproj,) = _mm("ssm_bwd_du", (s_len // ts, 1, 1), _rows(gs, ts), _whole(w_b), "nt",
                   [_row_out(s_len, D_SSM, BF16, ts, col=3, total=2 * D_MODEL)], extras=[_rows(du_dir, ts)],
                   epilogue=lambda acc, d: (acc + d,), into=[(dqkv, 0)])

    (wg["w_in"],) = _mm(
        "w_in_bwd_w", (1, 2, nt), (sv["a_in"], (TM, D_MODEL), lambda i, j, k: (k, 0)),
        (dproj, (TM, D_MODEL), lambda i, j, k: (k, j)), "tn",
        [((N_DEV, D_MODEL, 2 * D_MODEL // N_DEV), BF16, (N_DEV // 2, D_MODEL, 2 * D_MODEL // N_DEV),
          lambda i, j, k: (j, 0, 0), "row")],
        epilogue=lambda acc: (_column_shards(acc, 2 * D_MODEL // N_DEV),), acc_shape=(D_MODEL, D_MODEL))

    tie = on_partial(wg) if on_partial is not None else None
    mix_pre_g = gains["mix_pre_g"] if tie is None else gains["mix_pre_g"] + tie

    def in_epi(acc, dh, h, g_pre, f, g_post):
        dx, dg_pre = _vjp(_rms, (h, g_pre), acc)
        dh1 = dh + dx
        df, dg_post = _vjp(_rms, (f, g_post), 0.5 * dh1)
        return dh1, df, dg_pre, dg_post

    dh1, df1, gr["mix_pre_g"], gr["ffn1_post_g"] = _mm(
        "w_in_bwd_x", (nt, 1, 1), _rows(dproj, TM), _resident(gw["w_in"]), "nt",
        [_row_out(s_len, D_MODEL, F32, TM), _row_out(s_len, D_MODEL, BF16, TM), _col_out(D_MODEL), _col_out(D_MODEL)],
        extras=[_rows(dh2, TM), _rows(sv["h1"], TM), _whole(mix_pre_g), _rows(sv["ffn1"]["f"], TM),
                _whole(gains["ffn1_post_g"])], epilogue=in_epi)

    def ffn1_final(dn, dh, h, g_pre):
        dx, dg_pre = _vjp(_rms, (h, g_pre), dn)
        return dh + dx, dg_pre

    wg["ffn1_w_gate"], wg["ffn1_w_up"], wg["ffn1_w_down"], (dh0, gr["ffn1_pre_g"]) = _ffn_bwd(
        s_len, sv["ffn1"], df1, gw["w_gu1"], gw["w_d1"], ffn1_final,
        [_rows(dh1, FFN_TM), _rows(sv["ffn1"]["h"], FFN_TM), _whole(gains["ffn1_pre_g"])],
        [_row_out(s_len, D_MODEL, F32, FFN_TM), _col_out(D_MODEL)])

    return dh0, wg, gr, (d_a, d_w_b, d_w_c)


def _peers():
    x, y, c = lax.axis_index("x"), lax.axis_index("y"), lax.axis_index("c")
    me = 4 * x + 2 * y + c
    peers = []
    for k in range(1, N_DEV):
        kx, ky, kc = (k >> 2) & 1, (k >> 1) & 1, k & 1
        px, py, pc = x ^ kx, y ^ ky, c ^ kc
        peers.append(((px, py, pc), 4 * px + 2 * py + pc))
    return me, peers


_HBM_SPEC = pl.BlockSpec(memory_space=pltpu.HBM)
_SEM_SPEC = pl.BlockSpec(memory_space=pltpu.SEMAPHORE)
_DATAFLOW = pltpu.SideEffectType.DATAFLOW_SIDE_EFFECTING


def _device_index():
    return 4 * lax.axis_index("x") + 2 * lax.axis_index("y") + lax.axis_index("c")


def _landing(arrays, scatter):
    me = _device_index()
    out = []
    for a, scattered in zip(arrays, scatter):
        own = lax.dynamic_index_in_dim(a, me, 0, keepdims=True) if scattered else a[None]
        buf = lax.empty((N_DEV,) + own.shape[1:], a.dtype)
        out.append(lax.dynamic_update_slice_in_dim(buf, own, me, 0))
    return out


def _split_copies(src_refs, land_refs, send_sems, recv_sems, scatter):
    me, peers = _peers()
    pairs = []
    for t in range(len(src_refs)):
        for k, (peer, peer_id) in enumerate(peers):
            src = src_refs[t].at[peer_id] if scatter[t] else src_refs[t]
            sem = t * (N_DEV - 1) + k
            mk = lambda slot, src=src, t=t, sem=sem, peer=peer: pltpu.make_async_remote_copy(
                src_ref=src, dst_ref=land_refs[t].at[slot], send_sem=send_sems.at[sem], recv_sem=recv_sems.at[sem],
                device_id=peer, device_id_type=pl.DeviceIdType.MESH)
            pairs.append((functools.partial(mk, me), functools.partial(mk, peer_id)))
    return pairs


def _exchange_start(name, arrays, scatter, order_after):
    n = len(arrays)
    landing = _landing(arrays, scatter)

    def body(*refs):
        src_refs, land_refs = refs[:n], refs[n:2 * n]
        send_sems, recv_sems = refs[2 * n + 1], refs[2 * n + 2]
        token_ref = refs[-1]
        for outgoing, _ in _split_copies(src_refs, land_refs, send_sems, recv_sems, scatter):
            outgoing().start()
        token_ref[...] = jnp.zeros_like(token_ref)

    sem_shape = pltpu.SemaphoreType.DMA((n * (N_DEV - 1),))
    thru = [pltpu.HBM(a.shape, a.dtype) for a in list(arrays) + landing]
    hbm = lambda t: pltpu.with_memory_space_constraint(t, pltpu.HBM)
    res = pl.pallas_call(
        body,
        name=name,
        in_specs=[_HBM_SPEC] * (2 * n) + [pl.BlockSpec(memory_space=pl.ANY)],
        out_specs=[_SEM_SPEC, _SEM_SPEC] + [_HBM_SPEC] * (2 * n) + [pl.BlockSpec(memory_space=pltpu.VMEM)],
        out_shape=[sem_shape, sem_shape] + thru + [jax.ShapeDtypeStruct((8, 128), F32)],
        input_output_aliases={i: 2 + i for i in range(2 * n)},
        compiler_params=pltpu.CompilerParams(has_side_effects=_DATAFLOW),
    )(*[hbm(t) for t in list(arrays) + landing], order_after)
    return (res[0], res[1], res[2:2 + n], res[2 + n:2 + 2 * n]), res[-1]


def _exchange_wait(name, handle, scatter, order_after):
    send_sems, recv_sems, sources, landing = handle
    n = len(sources)

    def body(*refs):
        src_refs, land_refs = refs[:n], refs[n:2 * n]
        for outgoing, arrival in _split_copies(src_refs, land_refs, refs[2 * n], refs[2 * n + 1], scatter):
            outgoing().wait_send()
            arrival().wait_recv()

    thru = [pltpu.HBM(a.shape, a.dtype) for a in list(sources) + list(landing)]
    res = pl.pallas_call(
        body,
        name=name,
        in_specs=[_HBM_SPEC] * (2 * n) + [_SEM_SPEC, _SEM_SPEC, pl.BlockSpec(memory_space=pl.ANY)],
        out_specs=[_HBM_SPEC] * (2 * n),
        out_shape=thru,
        input_output_aliases={i: i for i in range(2 * n)},
        compiler_params=pltpu.CompilerParams(has_side_effects=_DATAFLOW),
    )(*sources, *landing, send_sems, recv_sems, order_after)
    return list(res[n:])


def _adam_math(g, w, m, v):
    m = ADAM_B1 * m + (1.0 - ADAM_B1) * g
    v = ADAM_B2 * v + (1.0 - ADAM_B2) * (g * g)
    m_hat = m / (1.0 - ADAM_B1 ** ADAM_STEP)
    v_hat = v / (1.0 - ADAM_B2 ** ADAM_STEP)
    delta = -ADAM_LR * (m_hat / (jnp.sqrt(v_hat) + ADAM_EPS) + ADAM_WD * w)
    return delta, m, v


def _adamw(name, recv, recv_block, recv_map, w, m, v, tr, layer=None, prev=None, after=None):
    def fn(r, wb, mb, vb, *token):
        g = r[0].astype(F32)
        for s in range(1, N_DEV):
            g = g + r[s].astype(F32)
        return (g,) + _adam_math(g, wb, mb, vb) + tuple(jnp.zeros_like(t) for t in token)

    rows, cols = w.shape[-2:]
    if layer is None:
        spec = lambda t: _rows(t, tr)
        out = _row_out(rows, cols, F32, tr)
    else:
        spec = lambda t: (t, (None, tr, cols), lambda i: (layer, i, 0))
        out = (w.shape, F32, (None, tr, cols), lambda i: (layer, i, 0), "row")
    ins = [(recv, recv_block, recv_map), spec(w), spec(m), spec(v)]
    outs = [out] * 4
    if after is not None:
        ins.append(_whole(after))
        outs.append((after.shape, F32, after.shape, lambda i: (0, 0), "row"))
    into = [] if prev is None else [(t, k) for k, t in enumerate(prev)]
    return _rowwise(name, rows, tr, ins, outs, fn, into=into)


def _pack_small(tensors):
    flat = jnp.concatenate([tensors[n].reshape(-1).astype(F32) for n in SMALL])
    padded = -(-flat.shape[0] // SMALL_PAD) * SMALL_PAD
    return jnp.pad(flat, (0, padded - flat.shape[0])).reshape(padded // SMALL_LANES, SMALL_LANES)


def _local_step(x, p, positions, loss_target, w, layer_weights, layer_done=None, layer_partial=None):
    s_len = x.shape[0]
    depth = p.shape[0]
    rope = _rope_tables(positions)
    gains = [{n: w[n][l].reshape(1, -1) for n in SMALL if w[n].ndim == 2 and n != "ssm_log_dt"} for l in range(depth)]
    ssm_args = lambda l: tuple(w[n][l] for n in ("ssm_lam_re", "ssm_lam_im", "ssm_log_dt", "ssm_b_re", "ssm_b_im",
                                                  "ssm_c_re", "ssm_c_im"))
    sps, pulls = [], []
    for l in range(depth):
        (a_cat, w_b, w_c), pull = jax.vjp(_ssm_params, *ssm_args(l))
        sps.append((a_cat, w_b.astype(BF16), w_c.astype(BF16), w["ssm_d"][l].reshape(1, D_SSM)))
        pulls.append(pull)

    (n,) = _rowwise("pre_norm", s_len, TM, [_rows(x, TM), _whole(gains[0]["ffn1_pre_g"])],
                    [_row_out(s_len, D_MODEL, BF16, TM)], lambda a, g: (_rms(a, g),))
    h = x
    saved = []
    for l in range(depth):
        g_next = gains[l + 1]["ffn1_pre_g"] if l + 1 < depth else gains[l]["ffn1_pre_g"]
        h, n, sv = _layer_fwd(s_len, h, n, p[l], rope, layer_weights(l, h), sps[l], gains[l], g_next)
        saved.append(sv)

    def loss_fn(y, t):
        e = y - t
        return e * (1.0 / D_MODEL), jnp.sum(e * e, axis=0, keepdims=True)

    dh, sq = _rowwise("loss", s_len, TM, [_rows(h, TM), _rows(loss_target, TM)],
                      [_row_out(s_len, D_MODEL, F32, TM), _col_out(D_MODEL)], loss_fn)
    loss = 0.5 * jnp.sum(sq) / D_MODEL

    w_grads, small_grads = [None] * depth, [None] * depth
    tie = None
    for l in reversed(range(depth)):
        g_l = gains[l] if tie is None else dict(gains[l], ple_post_g=gains[l]["ple_post_g"] + tie)
        partial = None if layer_partial is None else functools.partial(layer_partial, l)
        dh, wg, gr, (d_a, d_w_b, d_w_c) = _layer_bwd(s_len, dh, saved[l], rope, saved[l]["w"], sps[l], g_l, partial)
        d_lre, d_lim, d_dt, d_bre, d_bim, d_cre, d_cim = pulls[l]((d_a, d_w_b, d_w_c))
        gr.update(ssm_lam_re=d_lre, ssm_lam_im=d_lim, ssm_log_dt=d_dt, ssm_b_re=d_bre, ssm_b_im=d_bim,
                  ssm_c_re=d_cre, ssm_c_im=d_cim)
        gr = {n: g.reshape(w[n].shape[1:]) for n, g in gr.items()}
        w_grads[l], small_grads[l] = wg, gr
        if layer_done is not None:
            tie = layer_done(l, wg, gr, dh)
    return loss, dh, w_grads, small_grads


GROUP_OF = {"ffn1_w_gate": "g1", "ffn1_w_up": "u1", "ffn2_w_gate": "g2", "ffn2_w_up": "u2", "ffn1_w_down": "d1",
            "ffn2_w_down": "d2", "w_out": "out", "ple_w_gate": "pg", "w_in": "in", "ple_w_up": "pu", "ssm_w_glu": "glu"}
GROUPS = tuple(GROUP_OF[n] for n in SHARDED)
FIRST_PARTS = (("g1", "u1"), ("d1",), ("in", "glu", "out"), ("g2", "u2", "d2", "pu", "pg"))
_COLUMN_VIEWS = {"w_gu1": ("g1", "u1"), "w_gu2": ("g2", "u2"), "w_in": ("in",), "w_pu": ("pu",)}
_ROW_VIEWS = {"d1": "w_d1", "d2": "w_d2", "out": "w_out", "pg": "ple_w_gate", "glu": "w_glu"}


def _shard_groups(w, l):
    return {group: w[n][l].astype(BF16) for n, group in GROUP_OF.items()}


def _gathered_views(gathered, after=None):
    views = {}
    for name, groups in _COLUMN_VIEWS.items():
        if all(g in gathered for g in groups):
            views[name] = _natural_cols("relayout_" + name, [gathered[g] for g in groups], after)
            after = None
    for group, name in _ROW_VIEWS.items():
        if group in gathered:
            g = gathered[group]
            views[name] = g.reshape(N_DEV * g.shape[1], g.shape[2])
    return views


def _grad_groups(wg):
    rows = lambda t: t if t.ndim == 3 else t.reshape(N_DEV, t.shape[0] // N_DEV, t.shape[1])
    return {GROUP_OF[n]: rows(t) for n, t in wg.items()}


def _update_sharded(name, recv_groups, w, m, v, layer, prev, after=None):
    recv = recv_groups[GROUP_OF[name]]
    rows, cols = w.shape[1:]
    tr = rows if rows <= 256 else (rows // 2 if rows % 256 else 256)
    return _adamw(f"adamw_{name}", recv, (N_DEV, tr, cols), lambda i: (0, i, 0), w, m, v, tr, layer, prev, after)


def kernel(x, p, positions, ffn1_pre_g, ffn1_w_gate, ffn1_w_up, ffn1_w_down, ffn1_post_g, mix_pre_g, w_in, attn_norm_g, ssm_lam_re, ssm_lam_im, ssm_log_dt, ssm_b_re, ssm_b_im, ssm_c_re, ssm_c_im, ssm_d, ssm_w_glu, ssm_b_glu, ssm_norm_g, w_out, mix_post_g, ffn2_pre_g, ffn2_w_gate, ffn2_w_up, ffn2_w_down, ffn2_post_g, ple_w_up, ple_w_gate, ple_post_g, loss_target, m_ffn1_pre_g, m_ffn1_w_gate, m_ffn1_w_up, m_ffn1_w_down, m_ffn1_post_g, m_mix_pre_g, m_w_in, m_attn_norm_g, m_ssm_lam_re, m_ssm_lam_im, m_ssm_log_dt, m_ssm_b_re, m_ssm_b_im, m_ssm_c_re, m_ssm_c_im, m_ssm_d, m_ssm_w_glu, m_ssm_b_glu, m_ssm_norm_g, m_w_out, m_mix_post_g, m_ffn2_pre_g, m_ffn2_w_gate, m_ffn2_w_up, m_ffn2_w_down, m_ffn2_post_g, m_ple_w_up, m_ple_w_gate, m_ple_post_g, v_ffn1_pre_g, v_ffn1_w_gate, v_ffn1_w_up, v_ffn1_w_down, v_ffn1_post_g, v_mix_pre_g, v_w_in, v_attn_norm_g, v_ssm_lam_re, v_ssm_lam_im, v_ssm_log_dt, v_ssm_b_re, v_ssm_b_im, v_ssm_c_re, v_ssm_c_im, v_ssm_d, v_ssm_w_glu, v_ssm_b_glu, v_ssm_norm_g, v_w_out, v_mix_post_g, v_ffn2_pre_g, v_ffn2_w_gate, v_ffn2_w_up, v_ffn2_w_down, v_ffn2_post_g, v_ple_w_up, v_ple_w_gate, v_ple_post_g):
    args = dict(locals())
    w = {n: args[n] for n in WEIGHTS}
    mom = {n: args["m_" + n] for n in WEIGHTS}
    var = {n: args["v_" + n] for n in WEIGHTS}
    depth = p.shape[0]

    gathers, scatters, recv, tokens = {}, {}, {}, {}

    def start(name, groups, names, scatter, order_after):
        flags = scatter if isinstance(scatter, tuple) else (scatter,) * len(names)
        handle, token = _exchange_start(name, [groups[n] for n in names], flags, order_after)
        return (handle, names, flags), token

    def finish(name, pending, order_after):
        handle, names, flags = pending
        return dict(zip(names, _exchange_wait(name, handle, flags, order_after)))

    def gather_next(l, order_after):
        if l + 1 >= depth:
            return None
        gathers[l + 1], token = start("all_gather_start", _shard_groups(w, l + 1), GROUPS, False, order_after)
        return token

    def layer_weights(l, h):
        views = {}
        if l > 0:
            got = finish("all_gather_wait", gathers.pop(l), h)
            views.update(_gathered_views(got, gather_next(l, got[GROUPS[0]])))
            return lambda name, after: views[name]

        shards = _shard_groups(w, 0)
        pending, token = start("all_gather_start_0", shards, FIRST_PARTS[0], False, h)
        got = finish("all_gather_wait_0", pending, token)
        waiting = {}
        token = got[FIRST_PARTS[0][0]]
        for i, part in enumerate(FIRST_PARTS[1:], 1):
            waiting[i], token = start(f"all_gather_start_{i}", shards, part, False, token)
        views.update(_gathered_views(got, gather_next(0, token)))

        def weight(name, after):
            for i in sorted(waiting):
                if name not in views:
                    views.update(_gathered_views(finish(f"all_gather_wait_{i}", waiting.pop(i), after)))
            return views[name]

        return weight

    small_l = lambda t, l: _pack_small({n: t[n][l] for n in SMALL})
    scatter_flags = lambda names: tuple(n != "small" for n in names)
    late = ("g1", "u1", "d1", "small")
    early = tuple(g for g in GROUPS if g not in late)

    def layer_partial(l, wg):
        if l > 0:
            return None
        scatters["early"], token = start("reduce_scatter_start_a", _grad_groups(wg), early, scatter_flags(early),
                                         wg["w_in"])
        return token[0, 0]

    def layer_done(l, wg, gr, dh):
        if l + 1 < depth:
            recv[l + 1] = finish("reduce_scatter_wait", scatters.pop(l + 1), dh)
        groups = dict(_grad_groups(wg), small=_pack_small(gr))
        names = late if l == 0 else GROUPS + ("small",)
        scatters[l], tokens[l] = start("reduce_scatter_start_b" if l == 0 else "reduce_scatter_start", groups, names,
                                       scatter_flags(names), dh)
        return tokens[l][0, 0]

    loss, dx, w_grads, small_grads = _local_step(x[0], p[:, 0], positions[0], loss_target[0], w, layer_weights,
                                                 layer_done, layer_partial)
    loss = lax.psum(loss, MESH_AXES)

    results = {n: None for n in SHARDED}
    last = tokens[0]
    for l in reversed(range(1, depth)):
        for n in SHARDED:
            *results[n], last = _update_sharded(n, recv[l], w[n], mom[n], var[n], l, results[n], last)
    recv[0] = dict(finish("reduce_scatter_wait_a", scatters.pop("early"), last))
    recv[0].update(finish("reduce_scatter_wait_b", scatters.pop(0), last))
    for n in SHARDED:
        results[n] = _update_sharded(n, recv[0], w[n], mom[n], var[n], 0, results[n])
    result = dict(results)

    recv_small = jnp.concatenate([recv[l]["small"] for l in range(depth)], axis=1)
    packed = [jnp.concatenate([small_l(t, l) for l in range(depth)], axis=0) for t in (w, mom, var)]
    small_out = _adamw("adamw_small", recv_small, (N_DEV, 8, SMALL_LANES), lambda i: (0, i, 0), *packed, 8)
    for q in range(4):
        flat = small_out[q].reshape(depth, -1)
        off = 0
        for n in SMALL:
            size = math.prod(w[n].shape[1:])
            result.setdefault(n, [None] * 4)[q] = flat[:, off:off + size].reshape(w[n].shape)
            off += size

    outputs = [loss, dx[None]]
    for q in range(4):
        outputs += [result[n][q] for n in WEIGHTS]
    return tuple(outputs)
```

```python
import functools
import math

import jax
import jax.numpy as jnp
from jax import lax
from jax.experimental import pallas as pl
from jax.experimental.pallas import tpu as pltpu

F32 = jnp.float32
BF16 = jnp.bfloat16

N_DEV = 8
D_MODEL = 1024
D_FF = 2816
FF_SHARD = D_FF // N_DEV
D_ATTN = 512
D_SSM = 512
HEAD_DIM = 64
N_HEADS = 8
ROPE_DIM = 16
ROPE_THETA = 500000.0
DILATIONS = (1, 4, 16)
BAND = 128
N_GROUPS = 32
SSM_GROUP = 16
SSM_STATE = 64
N_STATE = N_GROUPS * SSM_STATE
PLE_DIM = 256
NORM_EPS = 1e-6
ADAM_LR, ADAM_B1, ADAM_B2, ADAM_EPS, ADAM_WD, ADAM_STEP = 0.001, 0.9, 0.999, 1e-08, 0.01, 10

VMEM_LIMIT_BYTES = 52 * 1024 * 1024
TM = 512
MESH_AXES = ("x", "y", "c")

WEIGHTS = ['ffn1_pre_g', 'ffn1_w_gate', 'ffn1_w_up', 'ffn1_w_down', 'ffn1_post_g', 'mix_pre_g', 'w_in', 'attn_norm_g',
           'ssm_lam_re', 'ssm_lam_im', 'ssm_log_dt', 'ssm_b_re', 'ssm_b_im', 'ssm_c_re', 'ssm_c_im', 'ssm_d',
           'ssm_w_glu', 'ssm_b_glu', 'ssm_norm_g', 'w_out', 'mix_post_g', 'ffn2_pre_g', 'ffn2_w_gate', 'ffn2_w_up',
           'ffn2_w_down', 'ffn2_post_g', 'ple_w_up', 'ple_w_gate', 'ple_post_g']
SHARDED = ['ffn1_w_gate', 'ffn1_w_up', 'ffn1_w_down', 'w_in', 'ssm_w_glu', 'w_out', 'ffn2_w_gate', 'ffn2_w_up',
           'ffn2_w_down', 'ple_w_up', 'ple_w_gate']
SMALL = [n for n in WEIGHTS if n not in SHARDED]
SMALL_LANES = 1024
SMALL_PAD = 8 * SMALL_LANES


def _params(n_axes):
    return pltpu.CompilerParams(dimension_semantics=("arbitrary",) * n_axes, vmem_limit_bytes=VMEM_LIMIT_BYTES)


_DIMS = {"nn": (((1,), (0,)), ((), ())), "nt": (((1,), (1,)), ((), ())), "tn": (((0,), (0,)), ((), ()))}


def _dot(a, b, mode):
    return lax.dot_general(a.astype(BF16), b.astype(BF16), _DIMS[mode], preferred_element_type=F32)


def _store(out_refs, vals, kinds, first):
    for ref, val, kind in zip(out_refs, vals, kinds):
        if isinstance(val, (list, tuple)):
            for q, piece in enumerate(val):
                ref[q] = piece.astype(ref.dtype)
        elif kind == "row":
            ref[...] = val.astype(ref.dtype)
        else:
            @pl.when(first)
            def _(ref=ref, val=val):
                ref[...] = val.astype(ref.dtype)

            @pl.when(jnp.logical_not(first))
            def _(ref=ref, val=val):
                ref[...] += val.astype(ref.dtype)


def _call(name, body, grid, ins, outs, scratch, into):
    arrays = [t[0] for t in ins]
    in_specs = [_in_spec(t) for t in ins]
    aliases = {}
    for arr, k in into:
        arrays.append(arr)
        in_specs.append(pl.BlockSpec(memory_space=pl.ANY))
        aliases[len(arrays) - 1] = k
    return pl.pallas_call(
        body,
        name=name,
        grid=grid,
        in_specs=in_specs,
        out_specs=[pl.BlockSpec(o[2], o[3]) for o in outs],
        out_shape=[jax.ShapeDtypeStruct(o[0], o[1]) for o in outs],
        scratch_shapes=list(scratch),
        input_output_aliases=aliases,
        compiler_params=_params(len(grid)),
    )(*arrays)


def _mm(name, grid, a, b, mode, outs, extras=(), epilogue=None, acc_shape=None, scratch=(), into=(), more=()):
    nk = grid[2]
    ne, no = len(extras), len(outs)
    kinds = [o[4] for o in outs]
    assert all(k == "row" for k in kinds) or grid[1] == 1
    n_ab = 2 + 2 * len(more)
    n_in = n_ab + ne + len(into)
    n_acc = int(nk > 1)

    def body(*refs):
        ex_refs = refs[n_ab:n_ab + ne]
        out_refs = refs[n_in:n_in + no]
        scr_refs = refs[n_in + no + n_acc:]
        part = _dot(refs[0][...], refs[1][...], mode)
        for p in range(2, n_ab, 2):
            part = part + _dot(refs[p][...], refs[p + 1][...], mode)
        first = pl.program_id(0) == 0

        def finish(acc):
            vals = epilogue(acc, *[r[...] for r in ex_refs], *scr_refs) if epilogue is not None else (acc,)
            _store(out_refs, vals, kinds, first)

        if nk == 1:
            finish(part)
        else:
            acc_ref = refs[n_in + no]
            k = pl.program_id(2)

            @pl.when(k == 0)
            def _():
                acc_ref[...] = part

            @pl.when(k > 0)
            def _():
                acc_ref[...] += part

            @pl.when(k == nk - 1)
            def _():
                finish(acc_ref[...])

    acc = [pltpu.VMEM(acc_shape, F32)] if nk > 1 else []
    pairs = tuple(t for pair in more for t in pair)
    return _call(name, body, grid, (a, b) + pairs + tuple(extras), outs, acc + list(scratch), into)


def _rowwise(name, n_rows, tm, ins, outs, fn, scratch=(), into=()):
    kinds = [o[4] for o in outs]
    ni, no = len(ins), len(outs)
    n_in = ni + len(into)

    def body(*refs):
        vals = fn(*[r[...] for r in refs[:ni]], *refs[n_in + no:])
        _store(refs[n_in:n_in + no], vals, kinds, pl.program_id(0) == 0)

    return _call(name, body, (n_rows // tm,), ins, outs, scratch, into)


def _rows(arr, tm, col=0, width=None):
    width = arr.shape[1] if width is None else width
    return (arr, (tm, width), lambda i, *_: (i, col))


def _whole(arr):
    nd = arr.ndim
    return (arr, arr.shape, lambda *_: (0,) * nd)


def _resident(arr):
    nd = arr.ndim
    return (arr, arr.shape, lambda *_: (0,) * nd, dict(pipeline_mode=pl.Buffered(1)))


def _in_spec(t):
    return pl.BlockSpec(t[1], t[2], **(t[3] if len(t) > 3 else {}))


def _row_out(n_rows, width, dtype, tm, col=0, total=None):
    return ((n_rows, width if total is None else total), dtype, (tm, width), lambda i, *_: (i, col), "row")


def _col_out(width):
    return ((1, width), F32, (1, width), lambda *_: (0, 0), "colsum")


def _column_shards(acc, width):
    return [acc[:, q * width:(q + 1) * width] for q in range(acc.shape[1] // width)]


def _natural_cols(name, parts, after=None):
    rows, width = parts[0].shape[1:]
    tr = min(rows, 256)
    n = len(parts)

    def body(*refs):
        out_ref = refs[-1]
        for t in range(n):
            for j in range(N_DEV):
                lo = (t * N_DEV + j) * width
                out_ref[:, lo:lo + width] = refs[t][j]

    ins = [(t, (N_DEV, tr, width), lambda i: (0, i, 0)) for t in parts]
    if after is not None:
        ins.append(_whole(after))
    out = _row_out(rows, n * N_DEV * width, parts[0].dtype, tr)
    return _call(name, body, (rows // tr,), ins, [out], (), ())[0]


def _rms(x, g):
    return x * lax.rsqrt(jnp.mean(x * x, axis=-1, keepdims=True) + NORM_EPS) * g


def _sigmoid(x):
    return 1.0 / (1.0 + jnp.exp(-x))


def _gelu(x):
    return 0.5 * x * (1.0 + jnp.tanh(0.7978845608028654 * (x + 0.044715 * x * x * x)))


FFN_TM = 256
FF_HALF = D_FF // 2


def _ffn_fwd(s_len, h, n, weight, g_post, g_next):
    def gu_epi(acc):
        gate, up = acc[:, :D_FF], acc[:, D_FF:]
        return acc, gate * _sigmoid(gate) * up

    w_gu = weight("gu", n)
    gu, act = _mm("ffn_gate_up", (s_len // FFN_TM, 1, 1), _rows(n, FFN_TM), _resident(w_gu), "nn",
                  [_row_out(s_len, 2 * D_FF, BF16, FFN_TM), _row_out(s_len, D_FF, BF16, FFN_TM)], epilogue=gu_epi)
    w_d = weight("d", act)

    def down_epi(acc, h_blk, gp, gn):
        h_new = h_blk + 0.5 * _rms(acc, gp)
        return acc, h_new, _rms(h_new, gn)

    f, h_new, n_next = _mm(
        "ffn_down", (s_len // TM, 1, 1), _rows(act, TM), _resident(w_d), "nn",
        [_row_out(s_len, D_MODEL, F32, TM), _row_out(s_len, D_MODEL, F32, TM), _row_out(s_len, D_MODEL, BF16, TM)],
        extras=[_rows(h, TM), _whole(g_post), _whole(g_next)], epilogue=down_epi)
    return h_new, n_next, dict(n=n, gu=gu, act=act, f=f, h=h)


def _ffn_bwd(s_len, saved, df, w_gu, w_d, final_epi, final_extras, final_outs):
    nt = s_len // TM

    def act_epi(acc, gu_blk):
        gate = gu_blk[:, :D_FF].astype(F32)
        up = gu_blk[:, D_FF:].astype(F32)
        sg = _sigmoid(gate)
        dgate = acc * up * sg * (1.0 + gate * (1.0 - sg))
        dup = acc * gate * sg
        return (jnp.concatenate([dgate, dup], axis=1),)

    (dgu,) = _mm("ffn_bwd_act", (s_len // FFN_TM, 1, 1), _rows(df, FFN_TM), _resident(w_d), "nt",
                 [_row_out(s_len, 2 * D_FF, BF16, FFN_TM)], extras=[_rows(saved["gu"], FFN_TM)], epilogue=act_epi)

    (d_w_d,) = _mm(
        "ffn_bwd_wdown", (2, 1, nt), (saved["act"], (TM, FF_HALF), lambda i, j, k: (k, i)),
        (df, (TM, D_MODEL), lambda i, j, k: (k, 0)), "tn",
        [((D_FF, D_MODEL), BF16, (FF_HALF, D_MODEL), lambda i, j, k: (i, 0), "row")], acc_shape=(FF_HALF, D_MODEL))

    def w_grad(name, half):
        (dw,) = _mm(
            name, (1, 2, nt), (saved["n"], (TM, D_MODEL), lambda i, j, k: (k, 0)),
            (dgu, (TM, FF_HALF), lambda i, j, k: (k, 2 * half + j)), "tn",
            [((N_DEV, D_MODEL, FF_SHARD), BF16, (N_DEV // 2, D_MODEL, FF_SHARD), lambda i, j, k: (j, 0, 0), "row")],
            epilogue=lambda acc: (_column_shards(acc, FF_SHARD),), acc_shape=(D_MODEL, FF_HALF))
        return dw

    d_w_gate, d_w_up = w_grad("ffn_bwd_wgate", 0), w_grad("ffn_bwd_wup", 1)

    outs = _mm("ffn_bwd_dn", (s_len // FFN_TM, 1, 1), _rows(dgu, FFN_TM), _resident(w_gu), "nt",
               final_outs, extras=final_extras, epilogue=final_epi)
    return d_w_gate, d_w_up, d_w_d, outs


def _band_mask(first_block):
    qi = lax.broadcasted_iota(jnp.int32, (BAND, 2 * BAND), 0)
    kj = lax.broadcasted_iota(jnp.int32, (BAND, 2 * BAND), 1)
    ok = (kj >= qi) & (kj <= qi + BAND)
    return ok & (jnp.logical_not(first_block) | (kj >= BAND))


def _attn_views(s_len, d):
    n_str = s_len // d
    nb = n_str // BAND
    blk = (BAND, D_ATTN)
    cur = lambda r, b: (b, r)
    prev = lambda r, b: (jnp.maximum(b - 1, 0), r)
    return n_str, nb, blk, cur, prev


STREAM_LANES = 128


def _stream_scratch(tm):
    return pltpu.VMEM((D_ATTN // STREAM_LANES, tm, STREAM_LANES), F32)


def _to_streams(x, d, scr):
    if d == 1:
        return x
    tm = x.shape[0]
    chunks = D_ATTN // STREAM_LANES
    for c in range(chunks):
        scr[c, 0:tm, :] = x[:, c * STREAM_LANES:(c + 1) * STREAM_LANES]
    return jnp.concatenate([scr.at[c][pl.ds(r, tm // d, stride=d), :] for r in range(d) for c in range(chunks)], axis=1)


def _from_streams(v, d, scr):
    if d == 1:
        return v
    rows = v.shape[0]
    chunks = D_ATTN // STREAM_LANES
    for r in range(d):
        for c in range(chunks):
            lo = r * D_ATTN + c * STREAM_LANES
            scr.at[c][pl.ds(r, rows, stride=d), :] = v[:, lo:lo + STREAM_LANES]
    return jnp.concatenate([scr[c, 0:rows * d, :] for c in range(chunks)], axis=1)


def _stream_in(arr, tm, d, ahead=0, n_blocks=None):
    rows = tm // d
    step = ahead // rows
    if ahead:
        return (arr, (rows, d * D_ATTN), lambda i, *_: (jnp.minimum(i + step, n_blocks - 1), 0))
    return (arr, (rows, d * D_ATTN), lambda i, *_: (i, 0))


def _stream_out(s_len, dtype, tm, d):
    return ((s_len // d, d * D_ATTN), dtype, (tm // d, d * D_ATTN), lambda i, *_: (i, 0), "row")


def _attn_fwd(s_len, q, k, v, d):
    n_str, nb, blk, cur, prev = _attn_views(s_len, d)
    view = lambda t: t

    def body(q_ref, kp_ref, kc_ref, vp_ref, vc_ref, o_ref, lse_ref):
        mask = _band_mask(pl.program_id(1) == 0)
        qq = q_ref[...]
        kk = jnp.concatenate([kp_ref[...], kc_ref[...]], axis=0)
        vv = jnp.concatenate([vp_ref[...], vc_ref[...]], axis=0)
        for h in range(N_HEADS):
            sl = slice(h * HEAD_DIM, (h + 1) * HEAD_DIM)
            s = _dot(qq[:, sl], kk[:, sl], "nt") * (HEAD_DIM ** -0.5)
            s = jnp.where(mask, s, -1e30)
            m = jnp.max(s, axis=-1, keepdims=True)
            e = jnp.exp(s - m)
            den = jnp.sum(e, axis=-1, keepdims=True)
            o_ref[:, sl] = _dot(e / den, vv[:, sl], "nn")
            lse_ref[:, sl] = jnp.broadcast_to(m + jnp.log(den), (BAND, HEAD_DIM))

    o, lse = pl.pallas_call(
        body,
        name=f"attn_fwd_d{d}",
        grid=(d, nb),
        in_specs=[pl.BlockSpec(blk, cur), pl.BlockSpec(blk, prev), pl.BlockSpec(blk, cur),
                  pl.BlockSpec(blk, prev), pl.BlockSpec(blk, cur)],
        out_specs=[pl.BlockSpec(blk, cur), pl.BlockSpec(blk, cur)],
        out_shape=[jax.ShapeDtypeStruct((n_str, d * D_ATTN), F32)] * 2,
        compiler_params=_params(2),
    )(view(q), view(k), view(k), view(v), view(v))
    return o, lse


def _attn_bwd(s_len, q, k, v, dattn, attn, lse, d):
    n_str, nb, blk, cur, prev = _attn_views(s_len, d)
    view = lambda t: t

    def body(q_ref, kp_ref, kc_ref, vp_ref, vc_ref, da_ref, at_ref, lse_ref, dq_ref, dka_ref, dkb_ref, dva_ref, dvb_ref):
        mask = _band_mask(pl.program_id(1) == 0)
        qq = q_ref[...]
        kk = jnp.concatenate([kp_ref[...], kc_ref[...]], axis=0)
        vv = jnp.concatenate([vp_ref[...], vc_ref[...]], axis=0)
        da = da_ref[...]
        prod = da * at_ref[...]
        scale = HEAD_DIM ** -0.5
        for h in range(N_HEADS):
            sl = slice(h * HEAD_DIM, (h + 1) * HEAD_DIM)
            s = _dot(qq[:, sl], kk[:, sl], "nt") * scale
            p = jnp.where(mask, jnp.exp(s - lse_ref[:, h * HEAD_DIM:h * HEAD_DIM + 1]), 0.0)
            dp = _dot(da[:, sl], vv[:, sl], "nt")
            ds = p * (dp - jnp.sum(prod[:, sl], axis=-1, keepdims=True))
            dq_ref[:, sl] = _dot(ds, kk[:, sl], "nn") * scale
            dk = _dot(ds, qq[:, sl], "tn") * scale
            dv = _dot(p, da[:, sl], "tn")
            dkb_ref[:, sl] = dk[:BAND]
            dka_ref[:, sl] = dk[BAND:]
            dvb_ref[:, sl] = dv[:BAND]
            dva_ref[:, sl] = dv[BAND:]

    outs = pl.pallas_call(
        body,
        name=f"attn_bwd_d{d}",
        grid=(d, nb),
        in_specs=[pl.BlockSpec(blk, cur), pl.BlockSpec(blk, prev), pl.BlockSpec(blk, cur),
                  pl.BlockSpec(blk, prev), pl.BlockSpec(blk, cur),
                  pl.BlockSpec(blk, cur), pl.BlockSpec(blk, cur), pl.BlockSpec(blk, cur)],
        out_specs=[pl.BlockSpec(blk, cur)] * 5,
        out_shape=[jax.ShapeDtypeStruct((n_str, d * D_ATTN), F32)] * 5,
        compiler_params=_params(2),
    )(view(q), view(k), view(k), view(v), view(v), view(dattn), view(attn), view(lse))
    return list(outs)


def _rope_tables(positions):
    half = ROPE_DIM // 2
    inv_freq = ROPE_THETA ** (-jnp.arange(half, dtype=F32) * (2.0 / ROPE_DIM))
    ang = positions.astype(F32)[:, None] * inv_freq
    cos, sin = jnp.cos(ang), jnp.sin(ang)
    s_len = positions.shape[0]
    one = jnp.ones((s_len, HEAD_DIM - ROPE_DIM), F32)
    zero8 = jnp.zeros((s_len, half), F32)
    zero = jnp.zeros((s_len, HEAD_DIM - ROPE_DIM), F32)
    c = jnp.concatenate([cos, cos, one], axis=1)
    s1 = jnp.concatenate([zero8, sin, zero], axis=1)
    s2 = jnp.concatenate([-sin, zero8, zero], axis=1)
    tile = lambda t: jnp.tile(t, (1, N_HEADS))
    return tile(c), tile(s1), tile(s2)


def _rope(t, c, s1, s2):
    half = ROPE_DIM // 2
    return t * c + pltpu.roll(t, half, 1) * s1 + pltpu.roll(t, D_ATTN - half, 1) * s2


def _rope_transposed(dt, c, s1, s2):
    half = ROPE_DIM // 2
    return dt * c + pltpu.roll(dt * s1, D_ATTN - half, 1) + pltpu.roll(dt * s2, half, 1)


SCAN_ROWS = 256
SCAN_CHUNK = 128
LANES = 128
SUBLANES = 8


def _cmul(xr, xi, yr, yi):
    return xr * yr - xi * yi, xr * yi + xi * yr


def _scan(name, bu, a_cat, reverse):
    s_len, width = bu.shape
    half = width // 2
    n_chunks = s_len // SCAN_CHUNK
    groups = n_chunks // SUBLANES
    log_chunk = SCAN_CHUNK.bit_length() - 1

    def body(br_ref, bi_ref, ar_ref, ai_ref, xr_ref, xi_ref, pr_ref, pi_ref, cr_ref, ci_ref):
        ar, ai = ar_ref[...], ai_ref[...]

        def step(i, state):
            off = (SCAN_CHUNK - 1 - i) if reverse else i
            new = []
            for g in range(groups):
                rows = pl.ds(g * SUBLANES * SCAN_CHUNK + off, SUBLANES, stride=SCAN_CHUNK)
                pr, pi = _cmul(ar, ai, *state[g])
                nr, ni = pr + br_ref[rows, :], pi + bi_ref[rows, :]
                xr_ref[rows, :] = nr
                xi_ref[rows, :] = ni
                new.append((nr, ni))
            return tuple(new)

        zero = jnp.zeros((SUBLANES, LANES), F32)
        lax.fori_loop(0, SCAN_CHUNK, step, tuple((zero, zero) for _ in range(groups)), unroll=4)

        pw = [(ar, ai)]
        for _ in range(log_chunk + n_chunks.bit_length()):
            pw.append(_cmul(*pw[-1], *pw[-1]))

        last = 0 if reverse else SCAN_CHUNK - 1
        er = xr_ref[pl.ds(last, n_chunks, stride=SCAN_CHUNK), :]
        ei = xi_ref[pl.ds(last, n_chunks, stride=SCAN_CHUNK), :]
        chunk = lax.broadcasted_iota(jnp.int32, (n_chunks, LANES), 0)

        def shifted(v, s):
            if reverse:
                return jnp.where(chunk < n_chunks - s, pltpu.roll(v, n_chunks - s, 0), 0.0)
            return jnp.where(chunk >= s, pltpu.roll(v, s, 0), 0.0)

        s, b = 1, log_chunk
        while s < n_chunks:
            mr, mi = _cmul(*pw[b], shifted(er, s), shifted(ei, s))
            er, ei = er + mr, ei + mi
            s, b = 2 * s, b + 1
        cr_ref[...] = shifted(er, 1)
        ci_ref[...] = shifted(ei, 1)

        step_no = lax.broadcasted_iota(jnp.int32, (SUBLANES, LANES), 0)
        expo = (SUBLANES - step_no) if reverse else (step_no + 1)
        qr, qi = jnp.ones((SUBLANES, LANES), F32), jnp.zeros((SUBLANES, LANES), F32)
        for bit in range(4):
            mr, mi = _cmul(qr, qi, *pw[bit])
            hit = (expo & (1 << bit)) != 0
            qr, qi = jnp.where(hit, mr, qr), jnp.where(hit, mi, qi)
        lo = SCAN_CHUNK - SUBLANES if reverse else 0
        pr_ref[lo:lo + SUBLANES, :] = qr
        pi_ref[lo:lo + SUBLANES, :] = qi
        m, b = SUBLANES, 3
        while m < SCAN_CHUNK:
            src = pl.ds(SCAN_CHUNK - m, m) if reverse else pl.ds(0, m)
            dst = pl.ds(SCAN_CHUNK - 2 * m, m) if reverse else pl.ds(m, m)
            mr, mi = _cmul(pr_ref[src, :], pi_ref[src, :], *pw[b])
            pr_ref[dst, :] = mr
            pi_ref[dst, :] = mi
            m, b = 2 * m, b + 1

        pr, pi = pr_ref[...], pi_ref[...]
        for j in range(n_chunks):
            rows = pl.ds(j * SCAN_CHUNK, SCAN_CHUNK)
            mr, mi = _cmul(pr, pi, cr_ref[j:j + 1, :], ci_ref[j:j + 1, :])
            xr_ref[rows, :] += mr
            xi_ref[rows, :] += mi

    n_blocks = half // LANES
    col = lambda shape, off: pl.BlockSpec(shape, lambda c: (0, off + c))
    return pl.pallas_call(
        body,
        name=name,
        grid=(n_blocks,),
        in_specs=[col((s_len, LANES), 0), col((s_len, LANES), n_blocks), col((1, LANES), 0), col((1, LANES), n_blocks)],
        out_specs=[col((s_len, LANES), 0), col((s_len, LANES), 0)],
        out_shape=[jax.ShapeDtypeStruct((s_len, half), F32)] * 2,
        scratch_shapes=[pltpu.VMEM((SCAN_CHUNK, LANES), F32)] * 2 + [pltpu.VMEM((n_chunks, LANES), F32)] * 2,
        compiler_params=_params(1),
    )(bu, bu, a_cat, a_cat)


def _ssm_params(lam_re, lam_im, log_dt, b_re, b_im, c_re, c_im):
    dt = jnp.exp(log_dt)[:, None]
    er = jnp.exp(lam_re * dt)
    a_re = er * jnp.cos(lam_im * dt)
    a_im = er * jnp.sin(lam_im * dt)
    nr, ni = a_re - 1.0, a_im
    den = lam_re * lam_re + lam_im * lam_im
    fr = (nr * lam_re + ni * lam_im) / den
    fi = (ni * lam_re - nr * lam_im) / den
    bb_re = fr[..., None] * b_re - fi[..., None] * b_im
    bb_im = fr[..., None] * b_im + fi[..., None] * b_re
    eye = jnp.eye(N_GROUPS, dtype=F32)

    def in_mat(bb):
        t = bb.transpose(0, 2, 1)[:, :, None, :] * eye[:, None, :, None]
        return t.reshape(D_SSM, N_STATE)

    def out_mat(cc):
        t = cc.transpose(0, 2, 1)[:, :, None, :] * eye[:, None, :, None]
        return t.reshape(N_STATE, D_SSM)

    w_b = jnp.concatenate([in_mat(bb_re), in_mat(bb_im)], axis=1)
    w_c = jnp.concatenate([out_mat(c_re), -out_mat(c_im)], axis=0)
    a_cat = jnp.concatenate([a_re.reshape(1, N_STATE), a_im.reshape(1, N_STATE)], axis=1)
    return a_cat, w_b, w_c


def _conj(a_cat):
    return jnp.concatenate([a_cat[:, :N_STATE], -a_cat[:, N_STATE:]], axis=1)


def _layer_fwd(s_len, h, n1, p_l, rope, weight, sp, gains, g_next):
    nt = s_len // TM
    sv = {}
    gw = {}

    def need(after, *names):
        for name in names:
            gw[name] = weight(name, after)

    def ffn_weight(which):
        def get(kind, after):
            need(after, f"w_{kind}{which}")
            return gw[f"w_{kind}{which}"]
        return get

    h1, a_in, sv["ffn1"] = _ffn_fwd(s_len, h, n1, ffn_weight(1), gains["ffn1_post_g"], gains["mix_pre_g"])

    need(h1, "w_in", "w_glu", "w_out")
    (proj,) = _mm("w_in", (nt, 1, 1), _rows(a_in, TM), _resident(gw["w_in"]), "nn",
                  [_row_out(s_len, 2 * D_MODEL, F32, TM)])

    c, s1, s2 = rope

    def rope_fn(tq, tk, tv, cc, a1, a2, scr):
        q, k = _rope(tq, cc, a1, a2), _rope(tk, cc, a1, a2)
        return [_to_streams(t, d, scr) for d in DILATIONS for t in (q, k, tv)]

    qkv = _rowwise(
        "rope", s_len, TM,
        [_rows(proj, TM, 0, D_ATTN), _rows(proj, TM, 1, D_ATTN), _rows(proj, TM, 2, D_ATTN),
         _rows(c, TM), _rows(s1, TM), _rows(s2, TM)],
        [_stream_out(s_len, BF16, TM, d) for d in DILATIONS for _ in range(3)], rope_fn,
        scratch=[_stream_scratch(TM)])
    qkv = {d: qkv[3 * i:3 * i + 3] for i, d in enumerate(DILATIONS)}
    parts = {d: _attn_fwd(s_len, *qkv[d], d) for d in DILATIONS}

    def mix_fn(*args):
        g, scr = args[6], args[7]
        o1, l1, o2, l2, o3, l3 = [_from_streams(args[2 * i + j], d, scr) for i, d in enumerate(DILATIONS)
                                  for j in range(2)]
        m = jnp.maximum(jnp.maximum(l1, l2), l3)
        e1, e2, e3 = jnp.exp(l1 - m), jnp.exp(l2 - m), jnp.exp(l3 - m)
        tot = e1 + e2 + e3
        attn = (e1 * o1 + e2 * o2 + e3 * o3) / tot
        lse = m + jnp.log(tot)
        return [attn, lse, _rms(attn, g)] + [_to_streams(t, d, scr) for d in DILATIONS[1:] for t in (attn, lse)]

    mix_out = _rowwise(
        "attn_mix", s_len, TM,
        [_stream_in(t, TM, d) for d in DILATIONS for t in parts[d]] + [_whole(gains["attn_norm_g"])],
        [_row_out(s_len, D_ATTN, F32, TM), _row_out(s_len, D_ATTN, F32, TM),
         _row_out(s_len, D_ATTN, BF16, TM, col=0, total=D_MODEL)]
        + [_stream_out(s_len, F32, TM, d) for d in DILATIONS[1:] for _ in range(2)], mix_fn,
        scratch=[_stream_scratch(TM)])
    attn, lse, mixed_half = mix_out[:3]
    attn_s = {1: (attn, lse), DILATIONS[1]: tuple(mix_out[3:5]), DILATIONS[2]: tuple(mix_out[5:7])}

    a_cat, w_b, w_c, d_vec = sp
    ts = SCAN_ROWS
    (bu,) = _mm("ssm_bu", (s_len // ts, 1, 1), (proj, (ts, D_SSM), lambda i, j, k: (i, 3)), _whole(w_b), "nn",
                [_row_out(s_len, 2 * N_STATE, F32, ts)])
    xs = _scan("ssm_scan", bu, a_cat, False)
    w_c_half = lambda part: (w_c, (N_STATE, D_SSM), lambda *_: (part, 0))
    w_b_half = lambda part: (w_b, (D_SSM, N_STATE), lambda *_: (0, part))

    def y_epi(acc, u, dv):
        z = acc + dv * u
        return z, _gelu(z)

    z, yg = _mm("ssm_y", (s_len // ts, 1, 1), _rows(xs[0], ts), w_c_half(0), "nn",
                [_row_out(s_len, D_SSM, F32, ts)] * 2, more=[(_rows(xs[1], ts), w_c_half(1))],
                extras=[(proj, (ts, D_SSM), lambda i, j, k: (i, 3)), _whole(d_vec)], epilogue=y_epi)

    def glu_epi(acc, y, b, g):
        t = acc + b
        ssm = y * _sigmoid(t)
        return t, ssm, _rms(ssm, g)

    t_glu, ssm, mixed = _mm(
        "ssm_glu", (nt, 1, 1), _rows(yg, TM), _whole(gw["w_glu"]), "nn",
        [_row_out(s_len, D_SSM, F32, TM), _row_out(s_len, D_SSM, F32, TM),
         _row_out(s_len, D_SSM, BF16, TM, col=1, total=D_MODEL)],
        extras=[_rows(yg, TM), _whole(gains["ssm_b_glu"]), _whole(gains["ssm_norm_g"])], epilogue=glu_epi,
        into=[(mixed_half, 2)])

    def out_epi(acc, h_blk, gp, gn):
        h_new = h_blk + _rms(acc, gp)
        return acc, h_new, _rms(h_new, gn)

    o, h2, n2 = _mm(
        "w_out", (nt, 1, 1), _rows(mixed, TM), _whole(gw["w_out"]), "nn",
        [_row_out(s_len, D_MODEL, F32, TM), _row_out(s_len, D_MODEL, F32, TM), _row_out(s_len, D_MODEL, BF16, TM)],
        extras=[_rows(h1, TM), _whole(gains["mix_post_g"]), _whole(gains["ffn2_pre_g"])], epilogue=out_epi)

    h3, _, sv["ffn2"] = _ffn_fwd(s_len, h2, n2, ffn_weight(2), gains["ffn2_post_g"], gains["ffn2_post_g"])
    need(h3, "w_pu", "ple_w_gate")

    (pu,) = _mm("ple_up", (nt, 1, 1), _rows(p_l, TM), _resident(gw["w_pu"]), "nn",
                [_row_out(s_len, D_MODEL, F32, TM)])

    def ple_epi(acc, pu_blk, h_blk, gp, gn):
        h_new = h_blk + _rms(pu_blk * _sigmoid(acc), gp)
        return acc, h_new, _rms(h_new, gn)

    gt, h4, n_next = _mm(
        "ple_gate", (nt, 1, 1), _rows(h3, TM), _whole(gw["ple_w_gate"]), "nn",
        [_row_out(s_len, D_MODEL, F32, TM), _row_out(s_len, D_MODEL, F32, TM), _row_out(s_len, D_MODEL, BF16, TM)],
        extras=[_rows(pu, TM), _rows(h3, TM), _whole(gains["ple_post_g"]), _whole(g_next)], epilogue=ple_epi)

    sv.update(h1=h1, a_in=a_in, proj=proj, qkv=qkv, attn=attn, attn_s=attn_s, xs=xs, z=z, yg=yg, t_glu=t_glu, ssm=ssm,
              mixed=mixed, o=o, h2=h2, h3=h3, pu=pu, gt=gt, p_l=p_l, w=gw)
    return h4, n_next, sv


def _vjp(fn, args, cot):
    _, pull = jax.vjp(fn, *args)
    return pull(cot)


def _layer_bwd(s_len, dh4, sv, rope, gw, sp, gains, on_partial=None):
    nt = s_len // TM
    gr = {}
    wg = {}

    def ple_fn(dh, pu, gt, g):
        dpu, dgt, dg = _vjp(lambda a, b, c: _rms(a * _sigmoid(b), c), (pu, gt, g), dh)
        return dpu, dgt, dg

    dpu, dgt, gr["ple_post_g"] = _rowwise(
        "ple_bwd", s_len, TM, [_rows(dh4, TM), _rows(sv["pu"], TM), _rows(sv["gt"], TM), _whole(gains["ple_post_g"])],
        [_row_out(s_len, D_MODEL, BF16, TM), _row_out(s_len, D_MODEL, BF16, TM), _col_out(D_MODEL)], ple_fn)

    def square_w_grad(name, lhs, rhs):
        half = D_MODEL // 2
        (dw,) = _mm(name, (2, 1, nt), (lhs, (TM, half), lambda i, j, k: (k, i)),
                    (rhs, (TM, D_MODEL), lambda i, j, k: (k, 0)), "tn",
                    [((D_MODEL, D_MODEL), BF16, (half, D_MODEL), lambda i, j, k: (i, 0), "row")],
                    acc_shape=(half, D_MODEL))
        return dw

    (wg["ple_w_up"],) = _mm(
        "ple_bwd_wup", (1, 1, nt), (sv["p_l"], (TM, PLE_DIM), lambda i, j, k: (k, 0)),
        (dpu, (TM, D_MODEL), lambda i, j, k: (k, 0)), "tn",
        [((N_DEV, PLE_DIM, D_MODEL // N_DEV), BF16, (N_DEV, PLE_DIM, D_MODEL // N_DEV), lambda i, j, k: (0, 0, 0),
          "row")],
        epilogue=lambda acc: (_column_shards(acc, D_MODEL // N_DEV),), acc_shape=(PLE_DIM, D_MODEL))
    wg["ple_w_gate"] = square_w_grad("ple_bwd_wgate", sv["h3"], dgt)

    def ple_dx_epi(acc, dh, f, g):
        dh3 = dh + acc
        df, dg = _vjp(_rms, (f, g), 0.5 * dh3)
        return dh3, df, dg

    dh3, df2, gr["ffn2_post_g"] = _mm(
        "ple_bwd_dx", (nt, 1, 1), _rows(dgt, TM), _whole(gw["ple_w_gate"]), "nt",
        [_row_out(s_len, D_MODEL, F32, TM), _row_out(s_len, D_MODEL, BF16, TM), _col_out(D_MODEL)],
        extras=[_rows(dh4, TM), _rows(sv["ffn2"]["f"], TM), _whole(gains["ffn2_post_g"])], epilogue=ple_dx_epi)

    def ffn2_final(dn, dh, h, g_pre, o, g_post):
        dx, dg_pre = _vjp(_rms, (h, g_pre), dn)
        dh2 = dh + dx
        do, dg_post = _vjp(_rms, (o, g_post), dh2)
        return dh2, do, dg_pre, dg_post

    wg["ffn2_w_gate"], wg["ffn2_w_up"], wg["ffn2_w_down"], (dh2, do, gr["ffn2_pre_g"], gr["mix_post_g"]) = _ffn_bwd(
        s_len, sv["ffn2"], df2, gw["w_gu2"], gw["w_d2"], ffn2_final,
        [_rows(dh3, FFN_TM), _rows(sv["h2"], FFN_TM), _whole(gains["ffn2_pre_g"]), _rows(sv["o"], FFN_TM),
         _whole(gains["mix_post_g"])],
        [_row_out(s_len, D_MODEL, F32, FFN_TM), _row_out(s_len, D_MODEL, BF16, FFN_TM), _col_out(D_MODEL),
         _col_out(D_MODEL)])

    wg["w_out"] = square_w_grad("w_out_bwd_w", sv["mixed"], do)

    def mixed_epi(acc, attn, ssm, yg, t, g_a, g_s, scr):
        dattn, dg_a = _vjp(_rms, (attn, g_a), acc[:, :D_ATTN])
        dssm, dg_s = _vjp(_rms, (ssm, g_s), acc[:, D_ATTN:])
        sg = _sigmoid(t)
        dt = dssm * yg * sg * (1.0 - sg)
        return ([_to_streams(dattn, d, scr) for d in DILATIONS]
                + [dt, dssm * sg, dg_a, dg_s, jnp.sum(dt, axis=0, keepdims=True)])

    res = _mm(
        "w_out_bwd_x", (nt, 1, 1), _rows(do, TM), _whole(gw["w_out"]), "nt",
        [_stream_out(s_len, F32, TM, d) for d in DILATIONS]
        + [_row_out(s_len, D_SSM, BF16, TM), _row_out(s_len, D_SSM, F32, TM),
           _col_out(D_ATTN), _col_out(D_SSM), _col_out(D_SSM)],
        extras=[_rows(sv["attn"], TM), _rows(sv["ssm"], TM), _rows(sv["yg"], TM), _rows(sv["t_glu"], TM),
                _whole(gains["attn_norm_g"]), _whole(gains["ssm_norm_g"])], epilogue=mixed_epi,
        scratch=[_stream_scratch(TM)])
    dattn_s = dict(zip(DILATIONS, res[:3]))
    dt_glu, dyg_dir, gr["attn_norm_g"], gr["ssm_norm_g"], gr["ssm_b_glu"] = res[3:]

    a_cat, w_b, w_c, d_vec = sp
    ts = SCAN_ROWS
    u_spec = (sv["proj"], (TM, D_SSM), lambda i, *_: (i, 3))
    (wg["ssm_w_glu"],) = _mm(
        "ssm_bwd_wglu", (1, 1, nt), (sv["yg"], (TM, D_SSM), lambda i, j, k: (k, 0)),
        (dt_glu, (TM, D_SSM), lambda i, j, k: (k, 0)), "tn",
        [((D_SSM, D_SSM), BF16, (D_SSM, D_SSM), lambda i, j, k: (0, 0), "row")], acc_shape=(D_SSM, D_SSM))

    def gelu_epi(acc, dy_dir, z, u, dv):
        (dz,) = _vjp(_gelu, (z,), acc + dy_dir)
        return dz, dz * dv, jnp.sum(dz * u, axis=0, keepdims=True)

    dz, du_dir, gr["ssm_d"] = _mm(
        "ssm_bwd_glu", (nt, 1, 1), _rows(dt_glu, TM), _whole(gw["w_glu"]), "nt",
        [_row_out(s_len, D_SSM, BF16, TM), _row_out(s_len, D_SSM, F32, TM), _col_out(D_SSM)],
        extras=[_rows(dyg_dir, TM), _rows(sv["z"], TM), u_spec, _whole(d_vec)], epilogue=gelu_epi)

    x_re, x_im = sv["xs"]
    w_c_half = lambda part: (w_c, (N_STATE, D_SSM), lambda *_: (part, 0))
    w_b_half = lambda part: (w_b, (D_SSM, N_STATE), lambda *_: (0, part))

    def w_c_grad(name, x):
        (dw,) = _mm(name, (N_STATE // 1024, 1, nt), (x, (TM, 1024), lambda i, j, k: (k, i)),
                    (dz, (TM, D_SSM), lambda i, j, k: (k, 0)), "tn",
                    [((N_STATE, D_SSM), F32, (1024, D_SSM), lambda i, j, k: (i, 0), "row")], acc_shape=(1024, D_SSM))
        return dw

    d_w_c = jnp.concatenate([w_c_grad("ssm_bwd_wc_re", x_re), w_c_grad("ssm_bwd_wc_im", x_im)], axis=0)
    (dxs,) = _mm("ssm_bwd_dx", (s_len // ts, 1, 1), _rows(dz, ts), _whole(w_c), "nt",
                 [_row_out(s_len, 2 * N_STATE, F32, ts)])
    g_re, g_im = _scan("ssm_scan_rev", dxs, _conj(a_cat), True)

    def da_fn(xr, xi, xr_before, xi_before, g_r, g_i):
        i = pl.program_id(0)
        rows = lax.broadcasted_iota(jnp.int32, xr.shape, 0)

        def previous(x, x_before):
            first_row = jnp.where(i > 0, x_before[7:8, :], 0.0)
            return jnp.where(rows == 0, first_row, pltpu.roll(x, 1, 0))

        pr, pi = previous(xr, xr_before), previous(xi, xi_before)
        return (jnp.sum(pr * g_r + pi * g_i, axis=0, keepdims=True),
                jnp.sum(pr * g_i - pi * g_r, axis=0, keepdims=True))

    before = lambda t: (t, (8, N_STATE), lambda i: (jnp.maximum(i * (ts // 8) - 1, 0), 0))
    d_a = jnp.concatenate(_rowwise(
        "ssm_bwd_da", s_len, ts,
        [_rows(x_re, ts), _rows(x_im, ts), before(x_re), before(x_im), _rows(g_re, ts), _rows(g_im, ts)],
        [_col_out(N_STATE)] * 2, da_fn), axis=1)

    def w_b_grad(name, g):
        (dw,) = _mm(name, (1, 1, nt), (sv["proj"], (TM, D_SSM), lambda i, j, k: (k, 3)),
                    (g, (TM, N_STATE), lambda i, j, k: (k, 0)), "tn",
                    [((D_SSM, N_STATE), F32, (D_SSM, N_STATE), lambda i, j, k: (0, 0), "row")],
                    acc_shape=(D_SSM, N_STATE))
        return dw

    d_w_b = jnp.concatenate([w_b_grad("ssm_bwd_wb_re", g_re), w_b_grad("ssm_bwd_wb_im", g_im)], axis=1)
    sum_tm = 2 * BAND
    n_sum = s_len // sum_tm
    ins = []
    for d in DILATIONS:
        dq_p, dka, dkb, dva, dvb = _attn_bwd(s_len, *sv["qkv"][d], dattn_s[d], *sv["attn_s"][d], d)
        if d == 1:
            nxt = lambda t: (t, (sum_tm, D_ATTN), lambda i: (jnp.minimum(i + 1, n_sum - 1), 0))
            ins += [_rows(dq_p, sum_tm), _rows(dka, sum_tm), _rows(dkb, sum_tm), nxt(dkb), _rows(dva, sum_tm),
                    _rows(dvb, sum_tm), nxt(dvb)]
        else:
            ins += [_stream_in(dq_p, sum_tm, d), _stream_in(dka, sum_tm, d), _stream_in(dkb, sum_tm, d, BAND, n_sum),
                    _stream_in(dva, sum_tm, d), _stream_in(dvb, sum_tm, d, BAND, n_sum)]
    c, s1, s2 = rope
    ins += [_rows(c, sum_tm), _rows(s1, sum_tm), _rows(s2, sum_tm)]

    def qkv_fn(*args):
        i = pl.program_id(0)
        blocks, scr = args[:-1], args[-1]
        dq_t, dka, dkb, dkb_next, dva, dvb, dvb_next = blocks[:7]
        more = i + 1 < n_sum
        ahead = lambda cur, nxt: jnp.concatenate([cur[BAND:], jnp.where(more, nxt[:BAND], 0.0)], axis=0)
        dk_t = dka + ahead(dkb, dkb_next)
        dv_t = dva + ahead(dvb, dvb_next)
        at = 7
        for d in DILATIONS[1:]:
            dq_p, dka, dkb, dva, dvb = blocks[at:at + 5]
            at += 5
            live = i + BAND // (sum_tm // d) < n_sum
            dq_t = dq_t + _from_streams(dq_p, d, scr)
            dk_t = dk_t + _from_streams(dka + jnp.where(live, dkb, 0.0), d, scr)
            dv_t = dv_t + _from_streams(dva + jnp.where(live, dvb, 0.0), d, scr)
        cc, a1, a2 = blocks[at:at + 3]
        return (jnp.concatenate([_rope_transposed(dq_t, cc, a1, a2), _rope_transposed(dk_t, cc, a1, a2), dv_t], axis=1),)

    (dqkv,) = _rowwise(
        "attn_bwd_sum", s_len, sum_tm, ins,
        [((s_len, 2 * D_MODEL), BF16, (sum_tm, 3 * D_ATTN), lambda i: (i, 0), "row")], qkv_fn,
        scratch=[_stream_scratch(sum_tm)])
    (dproj,) = _mm("ssm_bwd_du", (s_len // ts, 1, 1), _rows(g_re, ts), w_b_half(0), "nt",
                   [_row_out(s_len, D_SSM, BF16, ts, col=3, total=2 * D_MODEL)], more=[(_rows(g_im, ts), w_b_half(1))],
                   extras=[_rows(du_dir, ts)], epilogue=lambda acc, d: (acc + d,), into=[(dqkv, 0)])

    (wg["w_in"],) = _mm(
        "w_in_bwd_w", (1, 2, nt), (sv["a_in"], (TM, D_MODEL), lambda i, j, k: (k, 0)),
        (dproj, (TM, D_MODEL), lambda i, j, k: (k, j)), "tn",
        [((N_DEV, D_MODEL, 2 * D_MODEL // N_DEV), BF16, (N_DEV // 2, D_MODEL, 2 * D_MODEL // N_DEV),
          lambda i, j, k: (j, 0, 0), "row")],
        epilogue=lambda acc: (_column_shards(acc, 2 * D_MODEL // N_DEV),), acc_shape=(D_MODEL, D_MODEL))

    tie = on_partial(wg) if on_partial is not None else None
    mix_pre_g = gains["mix_pre_g"] if tie is None else gains["mix_pre_g"] + tie

    def in_epi(acc, dh, h, g_pre, f, g_post):
        dx, dg_pre = _vjp(_rms, (h, g_pre), acc)
        dh1 = dh + dx
        df, dg_post = _vjp(_rms, (f, g_post), 0.5 * dh1)
        return dh1, df, dg_pre, dg_post

    dh1, df1, gr["mix_pre_g"], gr["ffn1_post_g"] = _mm(
        "w_in_bwd_x", (nt, 1, 1), _rows(dproj, TM), _resident(gw["w_in"]), "nt",
        [_row_out(s_len, D_MODEL, F32, TM), _row_out(s_len, D_MODEL, BF16, TM), _col_out(D_MODEL), _col_out(D_MODEL)],
        extras=[_rows(dh2, TM), _rows(sv["h1"], TM), _whole(mix_pre_g), _rows(sv["ffn1"]["f"], TM),
                _whole(gains["ffn1_post_g"])], epilogue=in_epi)

    def ffn1_final(dn, dh, h, g_pre):
        dx, dg_pre = _vjp(_rms, (h, g_pre), dn)
        return dh + dx, dg_pre

    wg["ffn1_w_gate"], wg["ffn1_w_up"], wg["ffn1_w_down"], (dh0, gr["ffn1_pre_g"]) = _ffn_bwd(
        s_len, sv["ffn1"], df1, gw["w_gu1"], gw["w_d1"], ffn1_final,
        [_rows(dh1, FFN_TM), _rows(sv["ffn1"]["h"], FFN_TM), _whole(gains["ffn1_pre_g"])],
        [_row_out(s_len, D_MODEL, F32, FFN_TM), _col_out(D_MODEL)])

    return dh0, wg, gr, (d_a, d_w_b, d_w_c)


def _peers():
    x, y, c = lax.axis_index("x"), lax.axis_index("y"), lax.axis_index("c")
    me = 4 * x + 2 * y + c
    peers = []
    for k in range(1, N_DEV):
        kx, ky, kc = (k >> 2) & 1, (k >> 1) & 1, k & 1
        px, py, pc = x ^ kx, y ^ ky, c ^ kc
        peers.append(((px, py, pc), 4 * px + 2 * py + pc))
    return me, peers


_HBM_SPEC = pl.BlockSpec(memory_space=pltpu.HBM)
_SEM_SPEC = pl.BlockSpec(memory_space=pltpu.SEMAPHORE)
_DATAFLOW = pltpu.SideEffectType.DATAFLOW_SIDE_EFFECTING


def _device_index():
    return 4 * lax.axis_index("x") + 2 * lax.axis_index("y") + lax.axis_index("c")


def _landing(arrays, scatter):
    me = _device_index()
    out = []
    for a, scattered in zip(arrays, scatter):
        own = lax.dynamic_index_in_dim(a, me, 0, keepdims=True) if scattered else a[None]
        buf = lax.empty((N_DEV,) + own.shape[1:], a.dtype)
        out.append(lax.dynamic_update_slice_in_dim(buf, own, me, 0))
    return out


def _split_copies(src_refs, land_refs, send_sems, recv_sems, scatter):
    me, peers = _peers()
    pairs = []
    for t in range(len(src_refs)):
        for k, (peer, peer_id) in enumerate(peers):
            src = src_refs[t].at[peer_id] if scatter[t] else src_refs[t]
            sem = t * (N_DEV - 1) + k
            mk = lambda slot, src=src, t=t, sem=sem, peer=peer: pltpu.make_async_remote_copy(
                src_ref=src, dst_ref=land_refs[t].at[slot], send_sem=send_sems.at[sem], recv_sem=recv_sems.at[sem],
                device_id=peer, device_id_type=pl.DeviceIdType.MESH)
            pairs.append((functools.partial(mk, me), functools.partial(mk, peer_id)))
    return pairs


def _exchange_start(name, arrays, scatter, order_after):
    n = len(arrays)
    landing = _landing(arrays, scatter)

    def body(*refs):
        src_refs, land_refs = refs[:n], refs[n:2 * n]
        send_sems, recv_sems = refs[2 * n + 1], refs[2 * n + 2]
        token_ref = refs[-1]
        for outgoing, _ in _split_copies(src_refs, land_refs, send_sems, recv_sems, scatter):
            outgoing().start()
        token_ref[...] = jnp.zeros_like(token_ref)

    sem_shape = pltpu.SemaphoreType.DMA((n * (N_DEV - 1),))
    thru = [pltpu.HBM(a.shape, a.dtype) for a in list(arrays) + landing]
    hbm = lambda t: pltpu.with_memory_space_constraint(t, pltpu.HBM)
    res = pl.pallas_call(
        body,
        name=name,
        in_specs=[_HBM_SPEC] * (2 * n) + [pl.BlockSpec(memory_space=pl.ANY)],
        out_specs=[_SEM_SPEC, _SEM_SPEC] + [_HBM_SPEC] * (2 * n) + [pl.BlockSpec(memory_space=pltpu.VMEM)],
        out_shape=[sem_shape, sem_shape] + thru + [jax.ShapeDtypeStruct((8, 128), F32)],
        input_output_aliases={i: 2 + i for i in range(2 * n)},
        compiler_params=pltpu.CompilerParams(has_side_effects=_DATAFLOW),
    )(*[hbm(t) for t in list(arrays) + landing], order_after)
    return (res[0], res[1], res[2:2 + n], res[2 + n:2 + 2 * n]), res[-1]


def _exchange_wait(name, handle, scatter, order_after):
    send_sems, recv_sems, sources, landing = handle
    n = len(sources)

    def body(*refs):
        src_refs, land_refs = refs[:n], refs[n:2 * n]
        for outgoing, arrival in _split_copies(src_refs, land_refs, refs[2 * n], refs[2 * n + 1], scatter):
            outgoing().wait_send()
            arrival().wait_recv()

    thru = [pltpu.HBM(a.shape, a.dtype) for a in list(sources) + list(landing)]
    res = pl.pallas_call(
        body,
        name=name,
        in_specs=[_HBM_SPEC] * (2 * n) + [_SEM_SPEC, _SEM_SPEC, pl.BlockSpec(memory_space=pl.ANY)],
        out_specs=[_HBM_SPEC] * (2 * n),
        out_shape=thru,
        input_output_aliases={i: i for i in range(2 * n)},
        compiler_params=pltpu.CompilerParams(has_side_effects=_DATAFLOW),
    )(*sources, *landing, send_sems, recv_sems, order_after)
    return list(res[n:])


def _adam_math(g, w, m, v):
    m = ADAM_B1 * m + (1.0 - ADAM_B1) * g
    v = ADAM_B2 * v + (1.0 - ADAM_B2) * (g * g)
    m_hat = m / (1.0 - ADAM_B1 ** ADAM_STEP)
    v_hat = v / (1.0 - ADAM_B2 ** ADAM_STEP)
    delta = -ADAM_LR * (m_hat / (jnp.sqrt(v_hat) + ADAM_EPS) + ADAM_WD * w)
    return delta, m, v


def _adamw(name, recv, recv_block, recv_map, w, m, v, tr, layer=None, prev=None, after=None):
    def fn(r, wb, mb, vb, *token):
        g = r[0].astype(F32)
        for s in range(1, N_DEV):
            g = g + r[s].astype(F32)
        return (g,) + _adam_math(g, wb, mb, vb) + tuple(jnp.zeros_like(t) for t in token)

    rows, cols = w.shape[-2:]
    if layer is None:
        spec = lambda t: _rows(t, tr)
        out = _row_out(rows, cols, F32, tr)
    else:
        spec = lambda t: (t, (None, tr, cols), lambda i: (layer, i, 0))
        out = (w.shape, F32, (None, tr, cols), lambda i: (layer, i, 0), "row")
    ins = [(recv, recv_block, recv_map), spec(w), spec(m), spec(v)]
    outs = [out] * 4
    if after is not None:
        ins.append(_whole(after))
        outs.append((after.shape, F32, after.shape, lambda i: (0, 0), "row"))
    into = [] if prev is None else [(t, k) for k, t in enumerate(prev)]
    return _rowwise(name, rows, tr, ins, outs, fn, into=into)


def _pack_small(tensors):
    flat = jnp.concatenate([tensors[n].reshape(-1).astype(F32) for n in SMALL])
    padded = -(-flat.shape[0] // SMALL_PAD) * SMALL_PAD
    return jnp.pad(flat, (0, padded - flat.shape[0])).reshape(padded // SMALL_LANES, SMALL_LANES)


def _local_step(x, p, positions, loss_target, w, layer_weights, layer_done=None, layer_partial=None):
    s_len = x.shape[0]
    depth = p.shape[0]
    rope = _rope_tables(positions)
    gains = [{n: w[n][l].reshape(1, -1) for n in SMALL if w[n].ndim == 2 and n != "ssm_log_dt"} for l in range(depth)]
    ssm_args = lambda l: tuple(w[n][l] for n in ("ssm_lam_re", "ssm_lam_im", "ssm_log_dt", "ssm_b_re", "ssm_b_im",
                                                  "ssm_c_re", "ssm_c_im"))
    sps, pulls = [], []
    for l in range(depth):
        (a_cat, w_b, w_c), pull = jax.vjp(_ssm_params, *ssm_args(l))
        sps.append((a_cat, w_b.astype(BF16), w_c.astype(BF16), w["ssm_d"][l].reshape(1, D_SSM)))
        pulls.append(pull)

    (n,) = _rowwise("pre_norm", s_len, TM, [_rows(x, TM), _whole(gains[0]["ffn1_pre_g"])],
                    [_row_out(s_len, D_MODEL, BF16, TM)], lambda a, g: (_rms(a, g),))
    h = x
    saved = []
    for l in range(depth):
        g_next = gains[l + 1]["ffn1_pre_g"] if l + 1 < depth else gains[l]["ffn1_pre_g"]
        h, n, sv = _layer_fwd(s_len, h, n, p[l], rope, layer_weights(l, h), sps[l], gains[l], g_next)
        saved.append(sv)

    def loss_fn(y, t):
        e = y - t
        return e * (1.0 / D_MODEL), jnp.sum(e * e, axis=0, keepdims=True)

    dh, sq = _rowwise("loss", s_len, TM, [_rows(h, TM), _rows(loss_target, TM)],
                      [_row_out(s_len, D_MODEL, F32, TM), _col_out(D_MODEL)], loss_fn)
    loss = 0.5 * jnp.sum(sq) / D_MODEL

    w_grads, small_grads = [None] * depth, [None] * depth
    tie = None
    for l in reversed(range(depth)):
        g_l = gains[l] if tie is None else dict(gains[l], ple_post_g=gains[l]["ple_post_g"] + tie)
        partial = None if layer_partial is None else functools.partial(layer_partial, l)
        dh, wg, gr, (d_a, d_w_b, d_w_c) = _layer_bwd(s_len, dh, saved[l], rope, saved[l]["w"], sps[l], g_l, partial)
        d_lre, d_lim, d_dt, d_bre, d_bim, d_cre, d_cim = pulls[l]((d_a, d_w_b, d_w_c))
        gr.update(ssm_lam_re=d_lre, ssm_lam_im=d_lim, ssm_log_dt=d_dt, ssm_b_re=d_bre, ssm_b_im=d_bim,
                  ssm_c_re=d_cre, ssm_c_im=d_cim)
        gr = {n: g.reshape(w[n].shape[1:]) for n, g in gr.items()}
        w_grads[l], small_grads[l] = wg, gr
        if layer_done is not None:
            tie = layer_done(l, wg, gr, dh)
    return loss, dh, w_grads, small_grads


GROUP_OF = {"ffn1_w_gate": "g1", "ffn1_w_up": "u1", "ffn2_w_gate": "g2", "ffn2_w_up": "u2", "ffn1_w_down": "d1",
            "ffn2_w_down": "d2", "w_out": "out", "ple_w_gate": "pg", "w_in": "in", "ple_w_up": "pu", "ssm_w_glu": "glu"}
GROUPS = tuple(GROUP_OF[n] for n in SHARDED)
FIRST_PARTS = (("g1", "u1"), ("d1",), ("in", "glu", "out"), ("g2", "u2", "d2", "pu", "pg"))
_COLUMN_VIEWS = {"w_gu1": ("g1", "u1"), "w_gu2": ("g2", "u2"), "w_in": ("in",), "w_pu": ("pu",)}
_ROW_VIEWS = {"d1": "w_d1", "d2": "w_d2", "out": "w_out", "pg": "ple_w_gate", "glu": "w_glu"}


def _shard_groups(w, l):
    return {group: w[n][l].astype(BF16) for n, group in GROUP_OF.items()}


def _gathered_views(gathered, after=None):
    views = {}
    for name, groups in _COLUMN_VIEWS.items():
        if all(g in gathered for g in groups):
            views[name] = _natural_cols("relayout_" + name, [gathered[g] for g in groups], after)
            after = None
    for group, name in _ROW_VIEWS.items():
        if group in gathered:
            g = gathered[group]
            views[name] = g.reshape(N_DEV * g.shape[1], g.shape[2])
    return views


def _grad_groups(wg):
    rows = lambda t: t if t.ndim == 3 else t.reshape(N_DEV, t.shape[0] // N_DEV, t.shape[1])
    return {GROUP_OF[n]: rows(t) for n, t in wg.items()}


def _update_sharded(name, recv_groups, w, m, v, layer, prev, after=None):
    recv = recv_groups[GROUP_OF[name]]
    rows, cols = w.shape[1:]
    tr = rows if rows <= 256 else (rows // 2 if rows % 256 else 256)
    return _adamw(f"adamw_{name}", recv, (N_DEV, tr, cols), lambda i: (0, i, 0), w, m, v, tr, layer, prev, after)


def kernel(x, p, positions, ffn1_pre_g, ffn1_w_gate, ffn1_w_up, ffn1_w_down, ffn1_post_g, mix_pre_g, w_in, attn_norm_g, ssm_lam_re, ssm_lam_im, ssm_log_dt, ssm_b_re, ssm_b_im, ssm_c_re, ssm_c_im, ssm_d, ssm_w_glu, ssm_b_glu, ssm_norm_g, w_out, mix_post_g, ffn2_pre_g, ffn2_w_gate, ffn2_w_up, ffn2_w_down, ffn2_post_g, ple_w_up, ple_w_gate, ple_post_g, loss_target, m_ffn1_pre_g, m_ffn1_w_gate, m_ffn1_w_up, m_ffn1_w_down, m_ffn1_post_g, m_mix_pre_g, m_w_in, m_attn_norm_g, m_ssm_lam_re, m_ssm_lam_im, m_ssm_log_dt, m_ssm_b_re, m_ssm_b_im, m_ssm_c_re, m_ssm_c_im, m_ssm_d, m_ssm_w_glu, m_ssm_b_glu, m_ssm_norm_g, m_w_out, m_mix_post_g, m_ffn2_pre_g, m_ffn2_w_gate, m_ffn2_w_up, m_ffn2_w_down, m_ffn2_post_g, m_ple_w_up, m_ple_w_gate, m_ple_post_g, v_ffn1_pre_g, v_ffn1_w_gate, v_ffn1_w_up, v_ffn1_w_down, v_ffn1_post_g, v_mix_pre_g, v_w_in, v_attn_norm_g, v_ssm_lam_re, v_ssm_lam_im, v_ssm_log_dt, v_ssm_b_re, v_ssm_b_im, v_ssm_c_re, v_ssm_c_im, v_ssm_d, v_ssm_w_glu, v_ssm_b_glu, v_ssm_norm_g, v_w_out, v_mix_post_g, v_ffn2_pre_g, v_ffn2_w_gate, v_ffn2_w_up, v_ffn2_w_down, v_ffn2_post_g, v_ple_w_up, v_ple_w_gate, v_ple_post_g):
    args = dict(locals())
    w = {n: args[n] for n in WEIGHTS}
    mom = {n: args["m_" + n] for n in WEIGHTS}
    var = {n: args["v_" + n] for n in WEIGHTS}
    depth = p.shape[0]

    gathers, scatters, recv, tokens = {}, {}, {}, {}

    def start(name, groups, names, scatter, order_after):
        flags = scatter if isinstance(scatter, tuple) else (scatter,) * len(names)
        handle, token = _exchange_start(name, [groups[n] for n in names], flags, order_after)
        return (handle, names, flags), token

    def finish(name, pending, order_after):
        handle, names, flags = pending
        return dict(zip(names, _exchange_wait(name, handle, flags, order_after)))

    def gather_next(l, order_after):
        if l + 1 >= depth:
            return None
        gathers[l + 1], token = start("all_gather_start", _shard_groups(w, l + 1), GROUPS, False, order_after)
        return token

    def layer_weights(l, h):
        views = {}
        if l > 0:
            got = finish("all_gather_wait", gathers.pop(l), h)
            views.update(_gathered_views(got, gather_next(l, got[GROUPS[0]])))
            return lambda name, after: views[name]

        shards = _shard_groups(w, 0)
        pending, token = start("all_gather_start_0", shards, FIRST_PARTS[0], False, h)
        got = finish("all_gather_wait_0", pending, token)
        waiting = {}
        token = got[FIRST_PARTS[0][0]]
        for i, part in enumerate(FIRST_PARTS[1:], 1):
            waiting[i], token = start(f"all_gather_start_{i}", shards, part, False, token)
        views.update(_gathered_views(got, gather_next(0, token)))

        def weight(name, after):
            for i in sorted(waiting):
                if name not in views:
                    views.update(_gathered_views(finish(f"all_gather_wait_{i}", waiting.pop(i), after)))
            return views[name]

        return weight

    small_l = lambda t, l: _pack_small({n: t[n][l] for n in SMALL})
    scatter_flags = lambda names: tuple(n != "small" for n in names)
    late = ("g1", "u1", "d1", "small")
    early = tuple(g for g in GROUPS if g not in late)

    def layer_partial(l, wg):
        if l > 0:
            return None
        scatters["early"], token = start("reduce_scatter_start_a", _grad_groups(wg), early, scatter_flags(early),
                                         wg["w_in"])
        return token[0, 0]

    def layer_done(l, wg, gr, dh):
        if l + 1 < depth:
            recv[l + 1] = finish("reduce_scatter_wait", scatters.pop(l + 1), dh)
        groups = dict(_grad_groups(wg), small=_pack_small(gr))
        names = late if l == 0 else GROUPS + ("small",)
        scatters[l], tokens[l] = start("reduce_scatter_start_b" if l == 0 else "reduce_scatter_start", groups, names,
                                       scatter_flags(names), dh)
        return tokens[l][0, 0]

    loss, dx, w_grads, small_grads = _local_step(x[0], p[:, 0], positions[0], loss_target[0], w, layer_weights,
                                                 layer_done, layer_partial)
    loss = lax.psum(loss, MESH_AXES)

    results = {n: None for n in SHARDED}
    last = tokens[0]
    for l in reversed(range(1, depth)):
        for n in SHARDED:
            *results[n], last = _update_sharded(n, recv[l], w[n], mom[n], var[n], l, results[n], last)
    recv[0] = dict(finish("reduce_scatter_wait_a", scatters.pop("early"), last))
    recv[0].update(finish("reduce_scatter_wait_b", scatters.pop(0), last))
    for n in SHARDED:
        results[n] = _update_sharded(n, recv[0], w[n], mom[n], var[n], 0, results[n])
    result = dict(results)

    recv_small = jnp.concatenate([recv[l]["small"] for l in range(depth)], axis=1)
    packed = [jnp.concatenate([small_l(t, l) for l in range(depth)], axis=0) for t in (w, mom, var)]
    small_out = _adamw("adamw_small", recv_small, (N_DEV, 8, SMALL_LANES), lambda i: (0, i, 0), *packed, 8)
    for q in range(4):
        flat = small_out[q].reshape(depth, -1)
        off = 0
        for n in SMALL:
            size = math.prod(w[n].shape[1:])
            result.setdefault(n, [None] * 4)[q] = flat[:, off:off + size].reshape(w[n].shape)
            off += size

    outputs = [loss, dx[None]]
    for q in range(4):
        outputs += [result[n][q] for n in WEIGHTS]
    return tuple(outputs)
```

```python
import functools
import math

import jax
import jax.numpy as jnp
from jax import lax
from jax.experimental import pallas as pl
from jax.experimental.pallas import tpu as pltpu

F32 = jnp.float32
BF16 = jnp.bfloat16

N_DEV = 8
D_MODEL = 1024
D_FF = 2816
FF_SHARD = D_FF // N_DEV
D_ATTN = 512
D_SSM = 512
HEAD_DIM = 64
N_HEADS = 8
ROPE_DIM = 16
ROPE_THETA = 500000.0
DILATIONS = (1, 4, 16)
BAND = 128
N_GROUPS = 32
SSM_GROUP = 16
SSM_STATE = 64
N_STATE = N_GROUPS * SSM_STATE
PLE_DIM = 256
NORM_EPS = 1e-6
ADAM_LR, ADAM_B1, ADAM_B2, ADAM_EPS, ADAM_WD, ADAM_STEP = 0.001, 0.9, 0.999, 1e-08, 0.01, 10

VMEM_LIMIT_BYTES = 52 * 1024 * 1024
TM = 512
MESH_AXES = ("x", "y", "c")

WEIGHTS = ['ffn1_pre_g', 'ffn1_w_gate', 'ffn1_w_up', 'ffn1_w_down', 'ffn1_post_g', 'mix_pre_g', 'w_in', 'attn_norm_g',
           'ssm_lam_re', 'ssm_lam_im', 'ssm_log_dt', 'ssm_b_re', 'ssm_b_im', 'ssm_c_re', 'ssm_c_im', 'ssm_d',
           'ssm_w_glu', 'ssm_b_glu', 'ssm_norm_g', 'w_out', 'mix_post_g', 'ffn2_pre_g', 'ffn2_w_gate', 'ffn2_w_up',
           'ffn2_w_down', 'ffn2_post_g', 'ple_w_up', 'ple_w_gate', 'ple_post_g']
SHARDED = ['ffn1_w_gate', 'ffn1_w_up', 'ffn1_w_down', 'w_in', 'ssm_w_glu', 'w_out', 'ffn2_w_gate', 'ffn2_w_up',
           'ffn2_w_down', 'ple_w_up', 'ple_w_gate']
SMALL = [n for n in WEIGHTS if n not in SHARDED]
SMALL_LANES = 1024
SMALL_PAD = 8 * SMALL_LANES


def _params(n_axes):
    return pltpu.CompilerParams(dimension_semantics=("arbitrary",) * n_axes, vmem_limit_bytes=VMEM_LIMIT_BYTES)


_DIMS = {"nn": (((1,), (0,)), ((), ())), "nt": (((1,), (1,)), ((), ())), "tn": (((0,), (0,)), ((), ()))}


def _dot(a, b, mode):
    return lax.dot_general(a.astype(BF16), b.astype(BF16), _DIMS[mode], preferred_element_type=F32)


def _store(out_refs, vals, kinds, first):
    for ref, val, kind in zip(out_refs, vals, kinds):
        if isinstance(val, (list, tuple)):
            for q, piece in enumerate(val):
                ref[q] = piece.astype(ref.dtype)
        elif kind == "row":
            ref[...] = val.astype(ref.dtype)
        else:
            @pl.when(first)
            def _(ref=ref, val=val):
                ref[...] = val.astype(ref.dtype)

            @pl.when(jnp.logical_not(first))
            def _(ref=ref, val=val):
                ref[...] += val.astype(ref.dtype)


def _call(name, body, grid, ins, outs, scratch, into):
    arrays = [t[0] for t in ins]
    in_specs = [_in_spec(t) for t in ins]
    aliases = {}
    for arr, k in into:
        arrays.append(arr)
        in_specs.append(pl.BlockSpec(memory_space=pl.ANY))
        aliases[len(arrays) - 1] = k
    return pl.pallas_call(
        body,
        name=name,
        grid=grid,
        in_specs=in_specs,
        out_specs=[pl.BlockSpec(o[2], o[3]) for o in outs],
        out_shape=[jax.ShapeDtypeStruct(o[0], o[1]) for o in outs],
        scratch_shapes=list(scratch),
        input_output_aliases=aliases,
        compiler_params=_params(len(grid)),
    )(*arrays)


def _mm(name, grid, a, b, mode, outs, extras=(), epilogue=None, acc_shape=None, scratch=(), into=(), more=(),
        a_prep=None, b_prep=None):
    prep_a = a_prep if a_prep is not None else (lambda t: t)
    prep_b = b_prep if b_prep is not None else (lambda t: t)
    nk = grid[2]
    ne, no = len(extras), len(outs)
    kinds = [o[4] for o in outs]
    assert all(k == "row" for k in kinds) or grid[1] == 1
    n_ab = 2 + 2 * len(more)
    n_in = n_ab + ne + len(into)
    n_acc = int(nk > 1)

    def body(*refs):
        ex_refs = refs[n_ab:n_ab + ne]
        out_refs = refs[n_in:n_in + no]
        scr_refs = refs[n_in + no + n_acc:]
        part = _dot(prep_a(refs[0][...]), prep_b(refs[1][...]), mode)
        for p in range(2, n_ab, 2):
            part = part + _dot(prep_a(refs[p][...]), prep_b(refs[p + 1][...]), mode)
        first = pl.program_id(0) == 0

        def finish(acc):
            vals = epilogue(acc, *[r[...] for r in ex_refs], *scr_refs) if epilogue is not None else (acc,)
            _store(out_refs, vals, kinds, first)

        if nk == 1:
            finish(part)
        else:
            acc_ref = refs[n_in + no]
            k = pl.program_id(2)

            @pl.when(k == 0)
            def _():
                acc_ref[...] = part

            @pl.when(k > 0)
            def _():
                acc_ref[...] += part

            @pl.when(k == nk - 1)
            def _():
                finish(acc_ref[...])

    acc = [pltpu.VMEM(acc_shape, F32)] if nk > 1 else []
    pairs = tuple(t for pair in more for t in pair)
    return _call(name, body, grid, (a, b) + pairs + tuple(extras), outs, acc + list(scratch), into)


def _rowwise(name, n_rows, tm, ins, outs, fn, scratch=(), into=()):
    kinds = [o[4] for o in outs]
    ni, no = len(ins), len(outs)
    n_in = ni + len(into)

    def body(*refs):
        vals = fn(*[r[...] for r in refs[:ni]], *refs[n_in + no:])
        _store(refs[n_in:n_in + no], vals, kinds, pl.program_id(0) == 0)

    return _call(name, body, (n_rows // tm,), ins, outs, scratch, into)


def _rows(arr, tm, col=0, width=None):
    width = arr.shape[1] if width is None else width
    return (arr, (tm, width), lambda i, *_: (i, col))


def _whole(arr):
    nd = arr.ndim
    return (arr, arr.shape, lambda *_: (0,) * nd)


def _resident(arr):
    nd = arr.ndim
    return (arr, arr.shape, lambda *_: (0,) * nd, dict(pipeline_mode=pl.Buffered(1)))


def _in_spec(t):
    return pl.BlockSpec(t[1], t[2], **(t[3] if len(t) > 3 else {}))


def _row_out(n_rows, width, dtype, tm, col=0, total=None):
    return ((n_rows, width if total is None else total), dtype, (tm, width), lambda i, *_: (i, col), "row")


def _col_out(width):
    return ((1, width), F32, (1, width), lambda *_: (0, 0), "colsum")


def _lane_blocks(acc):
    return [acc[:, q * LANES:(q + 1) * LANES] for q in range(acc.shape[1] // LANES)]


def _join_lanes(blk):
    return jnp.concatenate([blk[q] for q in range(blk.shape[0])], axis=1)


def _column_shards(acc, width):
    return [acc[:, q * width:(q + 1) * width] for q in range(acc.shape[1] // width)]


def _natural_cols(name, parts, after=None):
    rows, width = parts[0].shape[1:]
    tr = min(rows, 256)
    n = len(parts)

    def body(*refs):
        out_ref = refs[-1]
        for t in range(n):
            for j in range(N_DEV):
                lo = (t * N_DEV + j) * width
                out_ref[:, lo:lo + width] = refs[t][j]

    ins = [(t, (N_DEV, tr, width), lambda i: (0, i, 0)) for t in parts]
    if after is not None:
        ins.append(_whole(after))
    out = _row_out(rows, n * N_DEV * width, parts[0].dtype, tr)
    return _call(name, body, (rows // tr,), ins, [out], (), ())[0]


def _rms(x, g):
    return x * lax.rsqrt(jnp.mean(x * x, axis=-1, keepdims=True) + NORM_EPS) * g


def _sigmoid(x):
    return 1.0 / (1.0 + jnp.exp(-x))


def _gelu(x):
    return 0.5 * x * (1.0 + jnp.tanh(0.7978845608028654 * (x + 0.044715 * x * x * x)))


FFN_TM = 256
FF_HALF = D_FF // 2


def _ffn_fwd(s_len, h, n, weight, g_post, g_next):
    def gu_epi(acc):
        gate, up = acc[:, :D_FF], acc[:, D_FF:]
        return acc, gate * _sigmoid(gate) * up

    w_gu = weight("gu", n)
    gu, act = _mm("ffn_gate_up", (s_len // FFN_TM, 1, 1), _rows(n, FFN_TM), _resident(w_gu), "nn",
                  [_row_out(s_len, 2 * D_FF, BF16, FFN_TM), _row_out(s_len, D_FF, BF16, FFN_TM)], epilogue=gu_epi)
    w_d = weight("d", act)

    def down_epi(acc, h_blk, gp, gn):
        h_new = h_blk + 0.5 * _rms(acc, gp)
        return acc, h_new, _rms(h_new, gn)

    f, h_new, n_next = _mm(
        "ffn_down", (s_len // TM, 1, 1), _rows(act, TM), _resident(w_d), "nn",
        [_row_out(s_len, D_MODEL, F32, TM), _row_out(s_len, D_MODEL, F32, TM), _row_out(s_len, D_MODEL, BF16, TM)],
        extras=[_rows(h, TM), _whole(g_post), _whole(g_next)], epilogue=down_epi)
    return h_new, n_next, dict(n=n, gu=gu, act=act, f=f, h=h)


def _ffn_bwd(s_len, saved, df, w_gu, w_d, final_epi, final_extras, final_outs):
    nt = s_len // TM

    def act_epi(acc, gu_blk):
        gate = gu_blk[:, :D_FF].astype(F32)
        up = gu_blk[:, D_FF:].astype(F32)
        sg = _sigmoid(gate)
        dgate = acc * up * sg * (1.0 + gate * (1.0 - sg))
        dup = acc * gate * sg
        return (jnp.concatenate([dgate, dup], axis=1),)

    (dgu,) = _mm("ffn_bwd_act", (s_len // FFN_TM, 1, 1), _rows(df, FFN_TM), _resident(w_d), "nt",
                 [_row_out(s_len, 2 * D_FF, BF16, FFN_TM)], extras=[_rows(saved["gu"], FFN_TM)], epilogue=act_epi)

    (d_w_d,) = _mm(
        "ffn_bwd_wdown", (2, 1, nt), (saved["act"], (TM, FF_HALF), lambda i, j, k: (k, i)),
        (df, (TM, D_MODEL), lambda i, j, k: (k, 0)), "tn",
        [((D_FF, D_MODEL), BF16, (FF_HALF, D_MODEL), lambda i, j, k: (i, 0), "row")], acc_shape=(FF_HALF, D_MODEL))

    def w_grad(name, half):
        (dw,) = _mm(
            name, (1, 2, nt), (saved["n"], (TM, D_MODEL), lambda i, j, k: (k, 0)),
            (dgu, (TM, FF_HALF), lambda i, j, k: (k, 2 * half + j)), "tn",
            [((N_DEV, D_MODEL, FF_SHARD), BF16, (N_DEV // 2, D_MODEL, FF_SHARD), lambda i, j, k: (j, 0, 0), "row")],
            epilogue=lambda acc: (_column_shards(acc, FF_SHARD),), acc_shape=(D_MODEL, FF_HALF))
        return dw

    d_w_gate, d_w_up = w_grad("ffn_bwd_wgate", 0), w_grad("ffn_bwd_wup", 1)

    outs = _mm("ffn_bwd_dn", (s_len // FFN_TM, 1, 1), _rows(dgu, FFN_TM), _resident(w_gu), "nt",
               final_outs, extras=final_extras, epilogue=final_epi)
    return d_w_gate, d_w_up, d_w_d, outs


def _band_mask(first_block):
    qi = lax.broadcasted_iota(jnp.int32, (BAND, 2 * BAND), 0)
    kj = lax.broadcasted_iota(jnp.int32, (BAND, 2 * BAND), 1)
    ok = (kj >= qi) & (kj <= qi + BAND)
    return ok & (jnp.logical_not(first_block) | (kj >= BAND))


def _attn_views(s_len, d):
    n_str = s_len // d
    nb = n_str // BAND
    blk = (BAND, D_ATTN)
    cur = lambda r, b: (b, r)
    prev = lambda r, b: (jnp.maximum(b - 1, 0), r)
    return n_str, nb, blk, cur, prev


STREAM_LANES = 128


def _stream_scratch(tm):
    return pltpu.VMEM((D_ATTN // STREAM_LANES, tm, STREAM_LANES), F32)


def _to_streams(x, d, scr):
    if d == 1:
        return x
    tm = x.shape[0]
    chunks = D_ATTN // STREAM_LANES
    for c in range(chunks):
        scr[c, 0:tm, :] = x[:, c * STREAM_LANES:(c + 1) * STREAM_LANES]
    return jnp.concatenate([scr.at[c][pl.ds(r, tm // d, stride=d), :] for r in range(d) for c in range(chunks)], axis=1)


def _from_streams(v, d, scr):
    if d == 1:
        return v
    rows = v.shape[0]
    chunks = D_ATTN // STREAM_LANES
    for r in range(d):
        for c in range(chunks):
            lo = r * D_ATTN + c * STREAM_LANES
            scr.at[c][pl.ds(r, rows, stride=d), :] = v[:, lo:lo + STREAM_LANES]
    return jnp.concatenate([scr[c, 0:rows * d, :] for c in range(chunks)], axis=1)


def _stream_in(arr, tm, d, ahead=0, n_blocks=None):
    rows = tm // d
    step = ahead // rows
    if ahead:
        return (arr, (rows, d * D_ATTN), lambda i, *_: (jnp.minimum(i + step, n_blocks - 1), 0))
    return (arr, (rows, d * D_ATTN), lambda i, *_: (i, 0))


def _stream_out(s_len, dtype, tm, d):
    return ((s_len // d, d * D_ATTN), dtype, (tm // d, d * D_ATTN), lambda i, *_: (i, 0), "row")


def _attn_fwd(s_len, q, k, v, d):
    n_str, nb, blk, cur, prev = _attn_views(s_len, d)
    view = lambda t: t

    def body(q_ref, kp_ref, kc_ref, vp_ref, vc_ref, o_ref, lse_ref):
        mask = _band_mask(pl.program_id(1) == 0)
        qq = q_ref[...]
        kk = jnp.concatenate([kp_ref[...], kc_ref[...]], axis=0)
        vv = jnp.concatenate([vp_ref[...], vc_ref[...]], axis=0)
        for h in range(N_HEADS):
            sl = slice(h * HEAD_DIM, (h + 1) * HEAD_DIM)
            s = _dot(qq[:, sl], kk[:, sl], "nt") * (HEAD_DIM ** -0.5)
            s = jnp.where(mask, s, -1e30)
            m = jnp.max(s, axis=-1, keepdims=True)
            e = jnp.exp(s - m)
            den = jnp.sum(e, axis=-1, keepdims=True)
            o_ref[:, sl] = _dot(e / den, vv[:, sl], "nn")
            lse_ref[:, sl] = jnp.broadcast_to(m + jnp.log(den), (BAND, HEAD_DIM))

    o, lse = pl.pallas_call(
        body,
        name=f"attn_fwd_d{d}",
        grid=(d, nb),
        in_specs=[pl.BlockSpec(blk, cur), pl.BlockSpec(blk, prev), pl.BlockSpec(blk, cur),
                  pl.BlockSpec(blk, prev), pl.BlockSpec(blk, cur)],
        out_specs=[pl.BlockSpec(blk, cur), pl.BlockSpec(blk, cur)],
        out_shape=[jax.ShapeDtypeStruct((n_str, d * D_ATTN), F32)] * 2,
        compiler_params=_params(2),
    )(view(q), view(k), view(k), view(v), view(v))
    return o, lse


def _attn_bwd(s_len, q, k, v, dattn, attn, lse, d):
    n_str, nb, blk, cur, prev = _attn_views(s_len, d)
    view = lambda t: t

    def body(q_ref, kp_ref, kc_ref, vp_ref, vc_ref, da_ref, at_ref, lse_ref, dq_ref, dka_ref, dkb_ref, dva_ref, dvb_ref):
        mask = _band_mask(pl.program_id(1) == 0)
        qq = q_ref[...]
        kk = jnp.concatenate([kp_ref[...], kc_ref[...]], axis=0)
        vv = jnp.concatenate([vp_ref[...], vc_ref[...]], axis=0)
        da = da_ref[...]
        prod = da * at_ref[...]
        scale = HEAD_DIM ** -0.5
        for h in range(N_HEADS):
            sl = slice(h * HEAD_DIM, (h + 1) * HEAD_DIM)
            s = _dot(qq[:, sl], kk[:, sl], "nt") * scale
            p = jnp.where(mask, jnp.exp(s - lse_ref[:, h * HEAD_DIM:h * HEAD_DIM + 1]), 0.0)
            dp = _dot(da[:, sl], vv[:, sl], "nt")
            ds = p * (dp - jnp.sum(prod[:, sl], axis=-1, keepdims=True))
            dq_ref[:, sl] = _dot(ds, kk[:, sl], "nn") * scale
            dk = _dot(ds, qq[:, sl], "tn") * scale
            dv = _dot(p, da[:, sl], "tn")
            dkb_ref[:, sl] = dk[:BAND]
            dka_ref[:, sl] = dk[BAND:]
            dvb_ref[:, sl] = dv[:BAND]
            dva_ref[:, sl] = dv[BAND:]

    outs = pl.pallas_call(
        body,
        name=f"attn_bwd_d{d}",
        grid=(d, nb),
        in_specs=[pl.BlockSpec(blk, cur), pl.BlockSpec(blk, prev), pl.BlockSpec(blk, cur),
                  pl.BlockSpec(blk, prev), pl.BlockSpec(blk, cur),
                  pl.BlockSpec(blk, cur), pl.BlockSpec(blk, cur), pl.BlockSpec(blk, cur)],
        out_specs=[pl.BlockSpec(blk, cur)] * 5,
        out_shape=[jax.ShapeDtypeStruct((n_str, d * D_ATTN), F32)] * 5,
        compiler_params=_params(2),
    )(view(q), view(k), view(k), view(v), view(v), view(dattn), view(attn), view(lse))
    return list(outs)


def _rope_tables(positions):
    half = ROPE_DIM // 2
    inv_freq = ROPE_THETA ** (-jnp.arange(half, dtype=F32) * (2.0 / ROPE_DIM))
    ang = positions.astype(F32)[:, None] * inv_freq
    cos, sin = jnp.cos(ang), jnp.sin(ang)
    s_len = positions.shape[0]
    one = jnp.ones((s_len, HEAD_DIM - ROPE_DIM), F32)
    zero8 = jnp.zeros((s_len, half), F32)
    zero = jnp.zeros((s_len, HEAD_DIM - ROPE_DIM), F32)
    c = jnp.concatenate([cos, cos, one], axis=1)
    s1 = jnp.concatenate([zero8, sin, zero], axis=1)
    s2 = jnp.concatenate([-sin, zero8, zero], axis=1)
    tile = lambda t: jnp.tile(t, (1, N_HEADS))
    return tile(c), tile(s1), tile(s2)


def _rope(t, c, s1, s2):
    half = ROPE_DIM // 2
    return t * c + pltpu.roll(t, half, 1) * s1 + pltpu.roll(t, D_ATTN - half, 1) * s2


def _rope_transposed(dt, c, s1, s2):
    half = ROPE_DIM // 2
    return dt * c + pltpu.roll(dt * s1, D_ATTN - half, 1) + pltpu.roll(dt * s2, half, 1)


SCAN_ROWS = 256
SCAN_CHUNK = 128
LANES = 128
SUBLANES = 8


def _cmul(xr, xi, yr, yi):
    return xr * yr - xi * yi, xr * yi + xi * yr


def _scan(name, bu, a_cat, reverse):
    s_len = bu.shape[1]
    n_blocks = bu.shape[0] // 2
    n_chunks = s_len // SCAN_CHUNK
    groups = n_chunks // SUBLANES
    log_chunk = SCAN_CHUNK.bit_length() - 1

    def body(br_ref, bi_ref, ar_ref, ai_ref, xr_ref, xi_ref, pr_ref, pi_ref, cr_ref, ci_ref):
        ar, ai = ar_ref[...], ai_ref[...]

        def step(i, state):
            off = (SCAN_CHUNK - 1 - i) if reverse else i
            new = []
            for g in range(groups):
                rows = pl.ds(g * SUBLANES * SCAN_CHUNK + off, SUBLANES, stride=SCAN_CHUNK)
                pr, pi = _cmul(ar, ai, *state[g])
                nr, ni = pr + br_ref[rows, :], pi + bi_ref[rows, :]
                xr_ref[rows, :] = nr
                xi_ref[rows, :] = ni
                new.append((nr, ni))
            return tuple(new)

        zero = jnp.zeros((SUBLANES, LANES), F32)
        lax.fori_loop(0, SCAN_CHUNK, step, tuple((zero, zero) for _ in range(groups)), unroll=4)

        pw = [(ar, ai)]
        for _ in range(log_chunk + n_chunks.bit_length()):
            pw.append(_cmul(*pw[-1], *pw[-1]))

        last = 0 if reverse else SCAN_CHUNK - 1
        er = xr_ref[pl.ds(last, n_chunks, stride=SCAN_CHUNK), :]
        ei = xi_ref[pl.ds(last, n_chunks, stride=SCAN_CHUNK), :]
        chunk = lax.broadcasted_iota(jnp.int32, (n_chunks, LANES), 0)

        def shifted(v, s):
            if reverse:
                return jnp.where(chunk < n_chunks - s, pltpu.roll(v, n_chunks - s, 0), 0.0)
            return jnp.where(chunk >= s, pltpu.roll(v, s, 0), 0.0)

        s, b = 1, log_chunk
        while s < n_chunks:
            mr, mi = _cmul(*pw[b], shifted(er, s), shifted(ei, s))
            er, ei = er + mr, ei + mi
            s, b = 2 * s, b + 1
        cr_ref[...] = shifted(er, 1)
        ci_ref[...] = shifted(ei, 1)

        step_no = lax.broadcasted_iota(jnp.int32, (SUBLANES, LANES), 0)
        expo = (SUBLANES - step_no) if reverse else (step_no + 1)
        qr, qi = jnp.ones((SUBLANES, LANES), F32), jnp.zeros((SUBLANES, LANES), F32)
        for bit in range(4):
            mr, mi = _cmul(qr, qi, *pw[bit])
            hit = (expo & (1 << bit)) != 0
            qr, qi = jnp.where(hit, mr, qr), jnp.where(hit, mi, qi)
        lo = SCAN_CHUNK - SUBLANES if reverse else 0
        pr_ref[lo:lo + SUBLANES, :] = qr
        pi_ref[lo:lo + SUBLANES, :] = qi
        m, b = SUBLANES, 3
        while m < SCAN_CHUNK:
            src = pl.ds(SCAN_CHUNK - m, m) if reverse else pl.ds(0, m)
            dst = pl.ds(SCAN_CHUNK - 2 * m, m) if reverse else pl.ds(m, m)
            mr, mi = _cmul(pr_ref[src, :], pi_ref[src, :], *pw[b])
            pr_ref[dst, :] = mr
            pi_ref[dst, :] = mi
            m, b = 2 * m, b + 1

        pr, pi = pr_ref[...], pi_ref[...]
        for j in range(n_chunks):
            rows = pl.ds(j * SCAN_CHUNK, SCAN_CHUNK)
            mr, mi = _cmul(pr, pi, cr_ref[j:j + 1, :], ci_ref[j:j + 1, :])
            xr_ref[rows, :] += mr
            xi_ref[rows, :] += mi

    blk = lambda off: pl.BlockSpec((None, s_len, LANES), lambda c: (off + c, 0, 0))
    col = lambda off: pl.BlockSpec((1, LANES), lambda c: (0, off + c))
    return pl.pallas_call(
        body,
        name=name,
        grid=(n_blocks,),
        in_specs=[blk(0), blk(n_blocks), col(0), col(n_blocks)],
        out_specs=[blk(0), blk(0)],
        out_shape=[jax.ShapeDtypeStruct((n_blocks, s_len, LANES), F32)] * 2,
        scratch_shapes=[pltpu.VMEM((SCAN_CHUNK, LANES), F32)] * 2 + [pltpu.VMEM((n_chunks, LANES), F32)] * 2,
        compiler_params=_params(1),
    )(bu, bu, a_cat, a_cat)


def _state_in(arr, tm):
    return (arr, (arr.shape[0], tm, LANES), lambda i, *_: (0, i, 0))


def _state_out(s_len, width, tm):
    n = width // LANES
    return ((n, s_len, LANES), F32, (n, tm, LANES), lambda i, *_: (0, i, 0), "row")


def _ssm_params(lam_re, lam_im, log_dt, b_re, b_im, c_re, c_im):
    dt = jnp.exp(log_dt)[:, None]
    er = jnp.exp(lam_re * dt)
    a_re = er * jnp.cos(lam_im * dt)
    a_im = er * jnp.sin(lam_im * dt)
    nr, ni = a_re - 1.0, a_im
    den = lam_re * lam_re + lam_im * lam_im
    fr = (nr * lam_re + ni * lam_im) / den
    fi = (ni * lam_re - nr * lam_im) / den
    bb_re = fr[..., None] * b_re - fi[..., None] * b_im
    bb_im = fr[..., None] * b_im + fi[..., None] * b_re
    eye = jnp.eye(N_GROUPS, dtype=F32)

    def in_mat(bb):
        t = bb.transpose(0, 2, 1)[:, :, None, :] * eye[:, None, :, None]
        return t.reshape(D_SSM, N_STATE)

    def out_mat(cc):
        t = cc.transpose(0, 2, 1)[:, :, None, :] * eye[:, None, :, None]
        return t.reshape(N_STATE, D_SSM)

    w_b = jnp.concatenate([in_mat(bb_re), in_mat(bb_im)], axis=1)
    w_c = jnp.concatenate([out_mat(c_re), -out_mat(c_im)], axis=0)
    a_cat = jnp.concatenate([a_re.reshape(1, N_STATE), a_im.reshape(1, N_STATE)], axis=1)
    return a_cat, w_b, w_c


def _conj(a_cat):
    return jnp.concatenate([a_cat[:, :N_STATE], -a_cat[:, N_STATE:]], axis=1)


def _layer_fwd(s_len, h, n1, p_l, rope, weight, sp, gains, g_next):
    nt = s_len // TM
    sv = {}
    gw = {}

    def need(after, *names):
        for name in names:
            gw[name] = weight(name, after)

    def ffn_weight(which):
        def get(kind, after):
            need(after, f"w_{kind}{which}")
            return gw[f"w_{kind}{which}"]
        return get

    h1, a_in, sv["ffn1"] = _ffn_fwd(s_len, h, n1, ffn_weight(1), gains["ffn1_post_g"], gains["mix_pre_g"])

    need(h1, "w_in", "w_glu", "w_out")
    (proj,) = _mm("w_in", (nt, 1, 1), _rows(a_in, TM), _resident(gw["w_in"]), "nn",
                  [_row_out(s_len, 2 * D_MODEL, F32, TM)])

    c, s1, s2 = rope

    def rope_fn(tq, tk, tv, cc, a1, a2, scr):
        q, k = _rope(tq, cc, a1, a2), _rope(tk, cc, a1, a2)
        return [_to_streams(t, d, scr) for d in DILATIONS for t in (q, k, tv)]

    qkv = _rowwise(
        "rope", s_len, TM,
        [_rows(proj, TM, 0, D_ATTN), _rows(proj, TM, 1, D_ATTN), _rows(proj, TM, 2, D_ATTN),
         _rows(c, TM), _rows(s1, TM), _rows(s2, TM)],
        [_stream_out(s_len, BF16, TM, d) for d in DILATIONS for _ in range(3)], rope_fn,
        scratch=[_stream_scratch(TM)])
    qkv = {d: qkv[3 * i:3 * i + 3] for i, d in enumerate(DILATIONS)}
    parts = {d: _attn_fwd(s_len, *qkv[d], d) for d in DILATIONS}

    def mix_fn(*args):
        g, scr = args[6], args[7]
        o1, l1, o2, l2, o3, l3 = [_from_streams(args[2 * i + j], d, scr) for i, d in enumerate(DILATIONS)
                                  for j in range(2)]
        m = jnp.maximum(jnp.maximum(l1, l2), l3)
        e1, e2, e3 = jnp.exp(l1 - m), jnp.exp(l2 - m), jnp.exp(l3 - m)
        tot = e1 + e2 + e3
        attn = (e1 * o1 + e2 * o2 + e3 * o3) / tot
        lse = m + jnp.log(tot)
        return [attn, lse, _rms(attn, g)] + [_to_streams(t, d, scr) for d in DILATIONS[1:] for t in (attn, lse)]

    mix_out = _rowwise(
        "attn_mix", s_len, TM,
        [_stream_in(t, TM, d) for d in DILATIONS for t in parts[d]] + [_whole(gains["attn_norm_g"])],
        [_row_out(s_len, D_ATTN, F32, TM), _row_out(s_len, D_ATTN, F32, TM),
         _row_out(s_len, D_ATTN, BF16, TM, col=0, total=D_MODEL)]
        + [_stream_out(s_len, F32, TM, d) for d in DILATIONS[1:] for _ in range(2)], mix_fn,
        scratch=[_stream_scratch(TM)])
    attn, lse, mixed_half = mix_out[:3]
    attn_s = {1: (attn, lse), DILATIONS[1]: tuple(mix_out[3:5]), DILATIONS[2]: tuple(mix_out[5:7])}

    a_cat, w_b, w_c, d_vec = sp
    ts = SCAN_ROWS
    (bu,) = _mm("ssm_bu", (s_len // ts, 1, 1), (proj, (ts, D_SSM), lambda i, j, k: (i, 3)), _whole(w_b), "nn",
                [_state_out(s_len, 2 * N_STATE, ts)], epilogue=lambda acc: (_lane_blocks(acc),))
    xs = _scan("ssm_scan", bu, a_cat, False)
    w_c_half = lambda part: (w_c, (N_STATE, D_SSM), lambda *_: (part, 0))

    def y_epi(acc, u, dv):
        z = acc + dv * u
        return z, _gelu(z)

    z, yg = _mm("ssm_y", (s_len // ts, 1, 1), _state_in(xs[0], ts), w_c_half(0), "nn",
                [_row_out(s_len, D_SSM, F32, ts)] * 2, more=[(_state_in(xs[1], ts), w_c_half(1))],
                a_prep=_join_lanes,
                extras=[(proj, (ts, D_SSM), lambda i, j, k: (i, 3)), _whole(d_vec)], epilogue=y_epi)

    def glu_epi(acc, y, b, g):
        t = acc + b
        ssm = y * _sigmoid(t)
        return t, ssm, _rms(ssm, g)

    t_glu, ssm, mixed = _mm(
        "ssm_glu", (nt, 1, 1), _rows(yg, TM), _whole(gw["w_glu"]), "nn",
        [_row_out(s_len, D_SSM, F32, TM), _row_out(s_len, D_SSM, F32, TM),
         _row_out(s_len, D_SSM, BF16, TM, col=1, total=D_MODEL)],
        extras=[_rows(yg, TM), _whole(gains["ssm_b_glu"]), _whole(gains["ssm_norm_g"])], epilogue=glu_epi,
        into=[(mixed_half, 2)])

    def out_epi(acc, h_blk, gp, gn):
        h_new = h_blk + _rms(acc, gp)
        return acc, h_new, _rms(h_new, gn)

    o, h2, n2 = _mm(
        "w_out", (nt, 1, 1), _rows(mixed, TM), _whole(gw["w_out"]), "nn",
        [_row_out(s_len, D_MODEL, F32, TM), _row_out(s_len, D_MODEL, F32, TM), _row_out(s_len, D_MODEL, BF16, TM)],
        extras=[_rows(h1, TM), _whole(gains["mix_post_g"]), _whole(gains["ffn2_pre_g"])], epilogue=out_epi)

    h3, _, sv["ffn2"] = _ffn_fwd(s_len, h2, n2, ffn_weight(2), gains["ffn2_post_g"], gains["ffn2_post_g"])
    need(h3, "w_pu", "ple_w_gate")

    (pu,) = _mm("ple_up", (nt, 1, 1), _rows(p_l, TM), _resident(gw["w_pu"]), "nn",
                [_row_out(s_len, D_MODEL, F32, TM)])

    def ple_epi(acc, pu_blk, h_blk, gp, gn):
        h_new = h_blk + _rms(pu_blk * _sigmoid(acc), gp)
        return acc, h_new, _rms(h_new, gn)

    gt, h4, n_next = _mm(
        "ple_gate", (nt, 1, 1), _rows(h3, TM), _whole(gw["ple_w_gate"]), "nn",
        [_row_out(s_len, D_MODEL, F32, TM), _row_out(s_len, D_MODEL, F32, TM), _row_out(s_len, D_MODEL, BF16, TM)],
        extras=[_rows(pu, TM), _rows(h3, TM), _whole(gains["ple_post_g"]), _whole(g_next)], epilogue=ple_epi)

    sv.update(h1=h1, a_in=a_in, proj=proj, qkv=qkv, attn=attn, attn_s=attn_s, xs=xs, z=z, yg=yg, t_glu=t_glu, ssm=ssm,
              mixed=mixed, o=o, h2=h2, h3=h3, pu=pu, gt=gt, p_l=p_l, w=gw)
    return h4, n_next, sv


def _vjp(fn, args, cot):
    _, pull = jax.vjp(fn, *args)
    return pull(cot)


def _layer_bwd(s_len, dh4, sv, rope, gw, sp, gains, on_partial=None):
    nt = s_len // TM
    gr = {}
    wg = {}

    def ple_fn(dh, pu, gt, g):
        dpu, dgt, dg = _vjp(lambda a, b, c: _rms(a * _sigmoid(b), c), (pu, gt, g), dh)
        return dpu, dgt, dg

    dpu, dgt, gr["ple_post_g"] = _rowwise(
        "ple_bwd", s_len, TM, [_rows(dh4, TM), _rows(sv["pu"], TM), _rows(sv["gt"], TM), _whole(gains["ple_post_g"])],
        [_row_out(s_len, D_MODEL, BF16, TM), _row_out(s_len, D_MODEL, BF16, TM), _col_out(D_MODEL)], ple_fn)

    def square_w_grad(name, lhs, rhs):
        half = D_MODEL // 2
        (dw,) = _mm(name, (2, 1, nt), (lhs, (TM, half), lambda i, j, k: (k, i)),
                    (rhs, (TM, D_MODEL), lambda i, j, k: (k, 0)), "tn",
                    [((D_MODEL, D_MODEL), BF16, (half, D_MODEL), lambda i, j, k: (i, 0), "row")],
                    acc_shape=(half, D_MODEL))
        return dw

    (wg["ple_w_up"],) = _mm(
        "ple_bwd_wup", (1, 1, nt), (sv["p_l"], (TM, PLE_DIM), lambda i, j, k: (k, 0)),
        (dpu, (TM, D_MODEL), lambda i, j, k: (k, 0)), "tn",
        [((N_DEV, PLE_DIM, D_MODEL // N_DEV), BF16, (N_DEV, PLE_DIM, D_MODEL // N_DEV), lambda i, j, k: (0, 0, 0),
          "row")],
        epilogue=lambda acc: (_column_shards(acc, D_MODEL // N_DEV),), acc_shape=(PLE_DIM, D_MODEL))
    wg["ple_w_gate"] = square_w_grad("ple_bwd_wgate", sv["h3"], dgt)

    def ple_dx_epi(acc, dh, f, g):
        dh3 = dh + acc
        df, dg = _vjp(_rms, (f, g), 0.5 * dh3)
        return dh3, df, dg

    dh3, df2, gr["ffn2_post_g"] = _mm(
        "ple_bwd_dx", (nt, 1, 1), _rows(dgt, TM), _whole(gw["ple_w_gate"]), "nt",
        [_row_out(s_len, D_MODEL, F32, TM), _row_out(s_len, D_MODEL, BF16, TM), _col_out(D_MODEL)],
        extras=[_rows(dh4, TM), _rows(sv["ffn2"]["f"], TM), _whole(gains["ffn2_post_g"])], epilogue=ple_dx_epi)

    def ffn2_final(dn, dh, h, g_pre, o, g_post):
        dx, dg_pre = _vjp(_rms, (h, g_pre), dn)
        dh2 = dh + dx
        do, dg_post = _vjp(_rms, (o, g_post), dh2)
        return dh2, do, dg_pre, dg_post

    wg["ffn2_w_gate"], wg["ffn2_w_up"], wg["ffn2_w_down"], (dh2, do, gr["ffn2_pre_g"], gr["mix_post_g"]) = _ffn_bwd(
        s_len, sv["ffn2"], df2, gw["w_gu2"], gw["w_d2"], ffn2_final,
        [_rows(dh3, FFN_TM), _rows(sv["h2"], FFN_TM), _whole(gains["ffn2_pre_g"]), _rows(sv["o"], FFN_TM),
         _whole(gains["mix_post_g"])],
        [_row_out(s_len, D_MODEL, F32, FFN_TM), _row_out(s_len, D_MODEL, BF16, FFN_TM), _col_out(D_MODEL),
         _col_out(D_MODEL)])

    wg["w_out"] = square_w_grad("w_out_bwd_w", sv["mixed"], do)

    def mixed_epi(acc, attn, ssm, yg, t, g_a, g_s, scr):
        dattn, dg_a = _vjp(_rms, (attn, g_a), acc[:, :D_ATTN])
        dssm, dg_s = _vjp(_rms, (ssm, g_s), acc[:, D_ATTN:])
        sg = _sigmoid(t)
        dt = dssm * yg * sg * (1.0 - sg)
        return ([_to_streams(dattn, d, scr) for d in DILATIONS]
                + [dt, dssm * sg, dg_a, dg_s, jnp.sum(dt, axis=0, keepdims=True)])

    res = _mm(
        "w_out_bwd_x", (nt, 1, 1), _rows(do, TM), _whole(gw["w_out"]), "nt",
        [_stream_out(s_len, F32, TM, d) for d in DILATIONS]
        + [_row_out(s_len, D_SSM, BF16, TM), _row_out(s_len, D_SSM, F32, TM),
           _col_out(D_ATTN), _col_out(D_SSM), _col_out(D_SSM)],
        extras=[_rows(sv["attn"], TM), _rows(sv["ssm"], TM), _rows(sv["yg"], TM), _rows(sv["t_glu"], TM),
                _whole(gains["attn_norm_g"]), _whole(gains["ssm_norm_g"])], epilogue=mixed_epi,
        scratch=[_stream_scratch(TM)])
    dattn_s = dict(zip(DILATIONS, res[:3]))
    dt_glu, dyg_dir, gr["attn_norm_g"], gr["ssm_norm_g"], gr["ssm_b_glu"] = res[3:]

    a_cat, w_b, w_c, d_vec = sp
    ts = SCAN_ROWS
    u_spec = (sv["proj"], (TM, D_SSM), lambda i, *_: (i, 3))
    (wg["ssm_w_glu"],) = _mm(
        "ssm_bwd_wglu", (1, 1, nt), (sv["yg"], (TM, D_SSM), lambda i, j, k: (k, 0)),
        (dt_glu, (TM, D_SSM), lambda i, j, k: (k, 0)), "tn",
        [((D_SSM, D_SSM), BF16, (D_SSM, D_SSM), lambda i, j, k: (0, 0), "row")], acc_shape=(D_SSM, D_SSM))

    def gelu_epi(acc, dy_dir, z, u, dv):
        (dz,) = _vjp(_gelu, (z,), acc + dy_dir)
        return dz, dz * dv, jnp.sum(dz * u, axis=0, keepdims=True)

    dz, du_dir, gr["ssm_d"] = _mm(
        "ssm_bwd_glu", (nt, 1, 1), _rows(dt_glu, TM), _whole(gw["w_glu"]), "nt",
        [_row_out(s_len, D_SSM, BF16, TM), _row_out(s_len, D_SSM, F32, TM), _col_out(D_SSM)],
        extras=[_rows(dyg_dir, TM), _rows(sv["z"], TM), u_spec, _whole(d_vec)], epilogue=gelu_epi)

    x_re, x_im = sv["xs"]
    w_c_half = lambda part: (w_c, (N_STATE, D_SSM), lambda *_: (part, 0))
    w_b_half = lambda part: (w_b, (D_SSM, N_STATE), lambda *_: (0, part))

    n_lb = N_STATE // LANES

    def w_c_grad(name, x):
        per = 1024 // LANES
        (dw,) = _mm(name, (N_STATE // 1024, 1, nt), (x, (per, TM, LANES), lambda i, j, k: (i, k, 0)),
                    (dz, (TM, D_SSM), lambda i, j, k: (k, 0)), "tn",
                    [((N_STATE, D_SSM), F32, (1024, D_SSM), lambda i, j, k: (i, 0), "row")], acc_shape=(1024, D_SSM),
                    a_prep=_join_lanes)
        return dw

    d_w_c = jnp.concatenate([w_c_grad("ssm_bwd_wc_re", x_re), w_c_grad("ssm_bwd_wc_im", x_im)], axis=0)
    (dxs,) = _mm("ssm_bwd_dx", (s_len // ts, 1, 1), _rows(dz, ts), _whole(w_c), "nt",
                 [_state_out(s_len, 2 * N_STATE, ts)], epilogue=lambda acc: (_lane_blocks(acc),))
    g_re, g_im = _scan("ssm_scan_rev", dxs, _conj(a_cat), True)

    def da_fn(xr, xi, xr_before, xi_before, g_r, g_i):
        i = pl.program_id(0)
        rows = lax.broadcasted_iota(jnp.int32, xr.shape, 1)

        def previous(x, x_before):
            first_row = jnp.where(i > 0, x_before[:, 7:8, :], 0.0)
            return jnp.where(rows == 0, first_row, pltpu.roll(x, 1, 1))

        pr, pi = previous(xr, xr_before), previous(xi, xi_before)
        return (jnp.sum(pr * g_r + pi * g_i, axis=1, keepdims=True),
                jnp.sum(pr * g_i - pi * g_r, axis=1, keepdims=True))

    before = lambda t: (t, (n_lb, 8, LANES), lambda i: (0, jnp.maximum(i * (ts // 8) - 1, 0), 0))
    lane_sum = ((n_lb, 1, LANES), F32, (n_lb, 1, LANES), lambda *_: (0, 0, 0), "colsum")
    d_a = jnp.concatenate([t.reshape(1, N_STATE) for t in _rowwise(
        "ssm_bwd_da", s_len, ts,
        [_state_in(x_re, ts), _state_in(x_im, ts), before(x_re), before(x_im), _state_in(g_re, ts),
         _state_in(g_im, ts)], [lane_sum] * 2, da_fn)], axis=1)

    def w_b_grad(name, g):
        (dw,) = _mm(name, (1, 1, nt), (sv["proj"], (TM, D_SSM), lambda i, j, k: (k, 3)),
                    (g, (n_lb, TM, LANES), lambda i, j, k: (0, k, 0)), "tn",
                    [((D_SSM, N_STATE), F32, (D_SSM, N_STATE), lambda i, j, k: (0, 0), "row")],
                    acc_shape=(D_SSM, N_STATE), b_prep=_join_lanes)
        return dw

    d_w_b = jnp.concatenate([w_b_grad("ssm_bwd_wb_re", g_re), w_b_grad("ssm_bwd_wb_im", g_im)], axis=1)
    sum_tm = 2 * BAND
    n_sum = s_len // sum_tm
    ins = []
    for d in DILATIONS:
        dq_p, dka, dkb, dva, dvb = _attn_bwd(s_len, *sv["qkv"][d], dattn_s[d], *sv["attn_s"][d], d)
        if d == 1:
            nxt = lambda t: (t, (sum_tm, D_ATTN), lambda i: (jnp.minimum(i + 1, n_sum - 1), 0))
            ins += [_rows(dq_p, sum_tm), _rows(dka, sum_tm), _rows(dkb, sum_tm), nxt(dkb), _rows(dva, sum_tm),
                    _rows(dvb, sum_tm), nxt(dvb)]
        else:
            ins += [_stream_in(dq_p, sum_tm, d), _stream_in(dka, sum_tm, d), _stream_in(dkb, sum_tm, d, BAND, n_sum),
                    _stream_in(dva, sum_tm, d), _stream_in(dvb, sum_tm, d, BAND, n_sum)]
    c, s1, s2 = rope
    ins += [_rows(c, sum_tm), _rows(s1, sum_tm), _rows(s2, sum_tm)]

    def qkv_fn(*args):
        i = pl.program_id(0)
        blocks, scr = args[:-1], args[-1]
        dq_t, dka, dkb, dkb_next, dva, dvb, dvb_next = blocks[:7]
        more = i + 1 < n_sum
        ahead = lambda cur, nxt: jnp.concatenate([cur[BAND:], jnp.where(more, nxt[:BAND], 0.0)], axis=0)
        dk_t = dka + ahead(dkb, dkb_next)
        dv_t = dva + ahead(dvb, dvb_next)
        at = 7
        for d in DILATIONS[1:]:
            dq_p, dka, dkb, dva, dvb = blocks[at:at + 5]
            at += 5
            live = i + BAND // (sum_tm // d) < n_sum
            dq_t = dq_t + _from_streams(dq_p, d, scr)
            dk_t = dk_t + _from_streams(dka + jnp.where(live, dkb, 0.0), d, scr)
            dv_t = dv_t + _from_streams(dva + jnp.where(live, dvb, 0.0), d, scr)
        cc, a1, a2 = blocks[at:at + 3]
        return (jnp.concatenate([_rope_transposed(dq_t, cc, a1, a2), _rope_transposed(dk_t, cc, a1, a2), dv_t], axis=1),)

    (dqkv,) = _rowwise(
        "attn_bwd_sum", s_len, sum_tm, ins,
        [((s_len, 2 * D_MODEL), BF16, (sum_tm, 3 * D_ATTN), lambda i: (i, 0), "row")], qkv_fn,
        scratch=[_stream_scratch(sum_tm)])
    (dproj,) = _mm("ssm_bwd_du", (s_len // ts, 1, 1), _state_in(g_re, ts), w_b_half(0), "nt",
                   [_row_out(s_len, D_SSM, BF16, ts, col=3, total=2 * D_MODEL)],
                   more=[(_state_in(g_im, ts), w_b_half(1))], a_prep=_join_lanes,
                   extras=[_rows(du_dir, ts)], epilogue=lambda acc, d: (acc + d,), into=[(dqkv, 0)])

    (wg["w_in"],) = _mm(
        "w_in_bwd_w", (1, 2, nt), (sv["a_in"], (TM, D_MODEL), lambda i, j, k: (k, 0)),
        (dproj, (TM, D_MODEL), lambda i, j, k: (k, j)), "tn",
        [((N_DEV, D_MODEL, 2 * D_MODEL // N_DEV), BF16, (N_DEV // 2, D_MODEL, 2 * D_MODEL // N_DEV),
          lambda i, j, k: (j, 0, 0), "row")],
        epilogue=lambda acc: (_column_shards(acc, 2 * D_MODEL // N_DEV),), acc_shape=(D_MODEL, D_MODEL))

    tie = on_partial(wg) if on_partial is not None else None
    mix_pre_g = gains["mix_pre_g"] if tie is None else gains["mix_pre_g"] + tie

    def in_epi(acc, dh, h, g_pre, f, g_post):
        dx, dg_pre = _vjp(_rms, (h, g_pre), acc)
        dh1 = dh + dx
        df, dg_post = _vjp(_rms, (f, g_post), 0.5 * dh1)
        return dh1, df, dg_pre, dg_post

    dh1, df1, gr["mix_pre_g"], gr["ffn1_post_g"] = _mm(
        "w_in_bwd_x", (nt, 1, 1), _rows(dproj, TM), _resident(gw["w_in"]), "nt",
        [_row_out(s_len, D_MODEL, F32, TM), _row_out(s_len, D_MODEL, BF16, TM), _col_out(D_MODEL), _col_out(D_MODEL)],
        extras=[_rows(dh2, TM), _rows(sv["h1"], TM), _whole(mix_pre_g), _rows(sv["ffn1"]["f"], TM),
                _whole(gains["ffn1_post_g"])], epilogue=in_epi)

    def ffn1_final(dn, dh, h, g_pre):
        dx, dg_pre = _vjp(_rms, (h, g_pre), dn)
        return dh + dx, dg_pre

    wg["ffn1_w_gate"], wg["ffn1_w_up"], wg["ffn1_w_down"], (dh0, gr["ffn1_pre_g"]) = _ffn_bwd(
        s_len, sv["ffn1"], df1, gw["w_gu1"], gw["w_d1"], ffn1_final,
        [_rows(dh1, FFN_TM), _rows(sv["ffn1"]["h"], FFN_TM), _whole(gains["ffn1_pre_g"])],
        [_row_out(s_len, D_MODEL, F32, FFN_TM), _col_out(D_MODEL)])

    return dh0, wg, gr, (d_a, d_w_b, d_w_c)


def _peers():
    x, y, c = lax.axis_index("x"), lax.axis_index("y"), lax.axis_index("c")
    me = 4 * x + 2 * y + c
    peers = []
    for k in range(1, N_DEV):
        kx, ky, kc = (k >> 2) & 1, (k >> 1) & 1, k & 1
        px, py, pc = x ^ kx, y ^ ky, c ^ kc
        peers.append(((px, py, pc), 4 * px + 2 * py + pc))
    return me, peers


_HBM_SPEC = pl.BlockSpec(memory_space=pltpu.HBM)
_SEM_SPEC = pl.BlockSpec(memory_space=pltpu.SEMAPHORE)
_DATAFLOW = pltpu.SideEffectType.DATAFLOW_SIDE_EFFECTING


def _device_index():
    return 4 * lax.axis_index("x") + 2 * lax.axis_index("y") + lax.axis_index("c")


def _landing(arrays, scatter):
    me = _device_index()
    out = []
    for a, scattered in zip(arrays, scatter):
        own = lax.dynamic_index_in_dim(a, me, 0, keepdims=True) if scattered else a[None]
        buf = lax.empty((N_DEV,) + own.shape[1:], a.dtype)
        out.append(lax.dynamic_update_slice_in_dim(buf, own, me, 0))
    return out


def _split_copies(src_refs, land_refs, send_sems, recv_sems, scatter):
    me, peers = _peers()
    pairs = []
    for t in range(len(src_refs)):
        for k, (peer, peer_id) in enumerate(peers):
            src = src_refs[t].at[peer_id] if scatter[t] else src_refs[t]
            sem = t * (N_DEV - 1) + k
            mk = lambda slot, src=src, t=t, sem=sem, peer=peer: pltpu.make_async_remote_copy(
                src_ref=src, dst_ref=land_refs[t].at[slot], send_sem=send_sems.at[sem], recv_sem=recv_sems.at[sem],
                device_id=peer, device_id_type=pl.DeviceIdType.MESH)
            pairs.append((functools.partial(mk, me), functools.partial(mk, peer_id)))
    return pairs


def _exchange_start(name, arrays, scatter, order_after):
    n = len(arrays)
    landing = _landing(arrays, scatter)

    def body(*refs):
        src_refs, land_refs = refs[:n], refs[n:2 * n]
        send_sems, recv_sems = refs[2 * n + 1], refs[2 * n + 2]
        token_ref = refs[-1]
        for outgoing, _ in _split_copies(src_refs, land_refs, send_sems, recv_sems, scatter):
            outgoing().start()
        token_ref[...] = jnp.zeros_like(token_ref)

    sem_shape = pltpu.SemaphoreType.DMA((n * (N_DEV - 1),))
    thru = [pltpu.HBM(a.shape, a.dtype) for a in list(arrays) + landing]
    hbm = lambda t: pltpu.with_memory_space_constraint(t, pltpu.HBM)
    res = pl.pallas_call(
        body,
        name=name,
        in_specs=[_HBM_SPEC] * (2 * n) + [pl.BlockSpec(memory_space=pl.ANY)],
        out_specs=[_SEM_SPEC, _SEM_SPEC] + [_HBM_SPEC] * (2 * n) + [pl.BlockSpec(memory_space=pltpu.VMEM)],
        out_shape=[sem_shape, sem_shape] + thru + [jax.ShapeDtypeStruct((8, 128), F32)],
        input_output_aliases={i: 2 + i for i in range(2 * n)},
        compiler_params=pltpu.CompilerParams(has_side_effects=_DATAFLOW),
    )(*[hbm(t) for t in list(arrays) + landing], order_after)
    return (res[0], res[1], res[2:2 + n], res[2 + n:2 + 2 * n]), res[-1]


def _exchange_wait(name, handle, scatter, order_after):
    send_sems, recv_sems, sources, landing = handle
    n = len(sources)

    def body(*refs):
        src_refs, land_refs = refs[:n], refs[n:2 * n]
        for outgoing, arrival in _split_copies(src_refs, land_refs, refs[2 * n], refs[2 * n + 1], scatter):
            outgoing().wait_send()
            arrival().wait_recv()

    thru = [pltpu.HBM(a.shape, a.dtype) for a in list(sources) + list(landing)]
    res = pl.pallas_call(
        body,
        name=name,
        in_specs=[_HBM_SPEC] * (2 * n) + [_SEM_SPEC, _SEM_SPEC, pl.BlockSpec(memory_space=pl.ANY)],
        out_specs=[_HBM_SPEC] * (2 * n),
        out_shape=thru,
        input_output_aliases={i: i for i in range(2 * n)},
        compiler_params=pltpu.CompilerParams(has_side_effects=_DATAFLOW),
    )(*sources, *landing, send_sems, recv_sems, order_after)
    return list(res[n:])


def _adam_math(g, w, m, v):
    m = ADAM_B1 * m + (1.0 - ADAM_B1) * g
    v = ADAM_B2 * v + (1.0 - ADAM_B2) * (g * g)
    m_hat = m / (1.0 - ADAM_B1 ** ADAM_STEP)
    v_hat = v / (1.0 - ADAM_B2 ** ADAM_STEP)
    delta = -ADAM_LR * (m_hat / (jnp.sqrt(v_hat) + ADAM_EPS) + ADAM_WD * w)
    return delta, m, v


def _adamw(name, recv, recv_block, recv_map, w, m, v, tr, layer=None, prev=None, after=None):
    def fn(r, wb, mb, vb, *token):
        g = r[0].astype(F32)
        for s in range(1, N_DEV):
            g = g + r[s].astype(F32)
        return (g,) + _adam_math(g, wb, mb, vb) + tuple(jnp.zeros_like(t) for t in token)

    rows, cols = w.shape[-2:]
    if layer is None:
        spec = lambda t: _rows(t, tr)
        out = _row_out(rows, cols, F32, tr)
    else:
        spec = lambda t: (t, (None, tr, cols), lambda i: (layer, i, 0))
        out = (w.shape, F32, (None, tr, cols), lambda i: (layer, i, 0), "row")
    ins = [(recv, recv_block, recv_map), spec(w), spec(m), spec(v)]
    outs = [out] * 4
    if after is not None:
        ins.append(_whole(after))
        outs.append((after.shape, F32, after.shape, lambda i: (0, 0), "row"))
    into = [] if prev is None else [(t, k) for k, t in enumerate(prev)]
    return _rowwise(name, rows, tr, ins, outs, fn, into=into)


def _pack_small(tensors):
    flat = jnp.concatenate([tensors[n].reshape(-1).astype(F32) for n in SMALL])
    padded = -(-flat.shape[0] // SMALL_PAD) * SMALL_PAD
    return jnp.pad(flat, (0, padded - flat.shape[0])).reshape(padded // SMALL_LANES, SMALL_LANES)


def _local_step(x, p, positions, loss_target, w, layer_weights, layer_done=None, layer_partial=None):
    s_len = x.shape[0]
    depth = p.shape[0]
    rope = _rope_tables(positions)
    gains = [{n: w[n][l].reshape(1, -1) for n in SMALL if w[n].ndim == 2 and n != "ssm_log_dt"} for l in range(depth)]
    ssm_args = lambda l: tuple(w[n][l] for n in ("ssm_lam_re", "ssm_lam_im", "ssm_log_dt", "ssm_b_re", "ssm_b_im",
                                                  "ssm_c_re", "ssm_c_im"))
    sps, pulls = [], []
    for l in range(depth):
        (a_cat, w_b, w_c), pull = jax.vjp(_ssm_params, *ssm_args(l))
        sps.append((a_cat, w_b.astype(BF16), w_c.astype(BF16), w["ssm_d"][l].reshape(1, D_SSM)))
        pulls.append(pull)

    (n,) = _rowwise("pre_norm", s_len, TM, [_rows(x, TM), _whole(gains[0]["ffn1_pre_g"])],
                    [_row_out(s_len, D_MODEL, BF16, TM)], lambda a, g: (_rms(a, g),))
    h = x
    saved = []
    for l in range(depth):
        g_next = gains[l + 1]["ffn1_pre_g"] if l + 1 < depth else gains[l]["ffn1_pre_g"]
        h, n, sv = _layer_fwd(s_len, h, n, p[l], rope, layer_weights(l, h), sps[l], gains[l], g_next)
        saved.append(sv)

    def loss_fn(y, t):
        e = y - t
        return e * (1.0 / D_MODEL), jnp.sum(e * e, axis=0, keepdims=True)

    dh, sq = _rowwise("loss", s_len, TM, [_rows(h, TM), _rows(loss_target, TM)],
                      [_row_out(s_len, D_MODEL, F32, TM), _col_out(D_MODEL)], loss_fn)
    loss = 0.5 * jnp.sum(sq) / D_MODEL

    w_grads, small_grads = [None] * depth, [None] * depth
    tie = None
    for l in reversed(range(depth)):
        g_l = gains[l] if tie is None else dict(gains[l], ple_post_g=gains[l]["ple_post_g"] + tie)
        partial = None if layer_partial is None else functools.partial(layer_partial, l)
        dh, wg, gr, (d_a, d_w_b, d_w_c) = _layer_bwd(s_len, dh, saved[l], rope, saved[l]["w"], sps[l], g_l, partial)
        d_lre, d_lim, d_dt, d_bre, d_bim, d_cre, d_cim = pulls[l]((d_a, d_w_b, d_w_c))
        gr.update(ssm_lam_re=d_lre, ssm_lam_im=d_lim, ssm_log_dt=d_dt, ssm_b_re=d_bre, ssm_b_im=d_bim,
                  ssm_c_re=d_cre, ssm_c_im=d_cim)
        gr = {n: g.reshape(w[n].shape[1:]) for n, g in gr.items()}
        w_grads[l], small_grads[l] = wg, gr
        if layer_done is not None:
            tie = layer_done(l, wg, gr, dh)
    return loss, dh, w_grads, small_grads


GROUP_OF = {"ffn1_w_gate": "g1", "ffn1_w_up": "u1", "ffn2_w_gate": "g2", "ffn2_w_up": "u2", "ffn1_w_down": "d1",
            "ffn2_w_down": "d2", "w_out": "out", "ple_w_gate": "pg", "w_in": "in", "ple_w_up": "pu", "ssm_w_glu": "glu"}
GROUPS = tuple(GROUP_OF[n] for n in SHARDED)
FIRST_PARTS = (("g1", "u1"), ("d1",), ("in", "glu", "out"), ("g2", "u2", "d2", "pu", "pg"))
_COLUMN_VIEWS = {"w_gu1": ("g1", "u1"), "w_gu2": ("g2", "u2"), "w_in": ("in",), "w_pu": ("pu",)}
_ROW_VIEWS = {"d1": "w_d1", "d2": "w_d2", "out": "w_out", "pg": "ple_w_gate", "glu": "w_glu"}


def _shard_groups(w, l):
    return {group: w[n][l].astype(BF16) for n, group in GROUP_OF.items()}


def _gathered_views(gathered, after=None):
    views = {}
    for name, groups in _COLUMN_VIEWS.items():
        if all(g in gathered for g in groups):
            views[name] = _natural_cols("relayout_" + name, [gathered[g] for g in groups], after)
            after = None
    for group, name in _ROW_VIEWS.items():
        if group in gathered:
            g = gathered[group]
            views[name] = g.reshape(N_DEV * g.shape[1], g.shape[2])
    return views


def _grad_groups(wg):
    rows = lambda t: t if t.ndim == 3 else t.reshape(N_DEV, t.shape[0] // N_DEV, t.shape[1])
    return {GROUP_OF[n]: rows(t) for n, t in wg.items()}


def _update_sharded(name, recv_groups, w, m, v, layer, prev, after=None):
    recv = recv_groups[GROUP_OF[name]]
    rows, cols = w.shape[1:]
    tr = rows if rows <= 256 else (rows // 2 if rows % 256 else 256)
    return _adamw(f"adamw_{name}", recv, (N_DEV, tr, cols), lambda i: (0, i, 0), w, m, v, tr, layer, prev, after)


def kernel(x, p, positions, ffn1_pre_g, ffn1_w_gate, ffn1_w_up, ffn1_w_down, ffn1_post_g, mix_pre_g, w_in, attn_norm_g, ssm_lam_re, ssm_lam_im, ssm_log_dt, ssm_b_re, ssm_b_im, ssm_c_re, ssm_c_im, ssm_d, ssm_w_glu, ssm_b_glu, ssm_norm_g, w_out, mix_post_g, ffn2_pre_g, ffn2_w_gate, ffn2_w_up, ffn2_w_down, ffn2_post_g, ple_w_up, ple_w_gate, ple_post_g, loss_target, m_ffn1_pre_g, m_ffn1_w_gate, m_ffn1_w_up, m_ffn1_w_down, m_ffn1_post_g, m_mix_pre_g, m_w_in, m_attn_norm_g, m_ssm_lam_re, m_ssm_lam_im, m_ssm_log_dt, m_ssm_b_re, m_ssm_b_im, m_ssm_c_re, m_ssm_c_im, m_ssm_d, m_ssm_w_glu, m_ssm_b_glu, m_ssm_norm_g, m_w_out, m_mix_post_g, m_ffn2_pre_g, m_ffn2_w_gate, m_ffn2_w_up, m_ffn2_w_down, m_ffn2_post_g, m_ple_w_up, m_ple_w_gate, m_ple_post_g, v_ffn1_pre_g, v_ffn1_w_gate, v_ffn1_w_up, v_ffn1_w_down, v_ffn1_post_g, v_mix_pre_g, v_w_in, v_attn_norm_g, v_ssm_lam_re, v_ssm_lam_im, v_ssm_log_dt, v_ssm_b_re, v_ssm_b_im, v_ssm_c_re, v_ssm_c_im, v_ssm_d, v_ssm_w_glu, v_ssm_b_glu, v_ssm_norm_g, v_w_out, v_mix_post_g, v_ffn2_pre_g, v_ffn2_w_gate, v_ffn2_w_up, v_ffn2_w_down, v_ffn2_post_g, v_ple_w_up, v_ple_w_gate, v_ple_post_g):
    args = dict(locals())
    w = {n: args[n] for n in WEIGHTS}
    mom = {n: args["m_" + n] for n in WEIGHTS}
    var = {n: args["v_" + n] for n in WEIGHTS}
    depth = p.shape[0]

    gathers, scatters, recv, tokens = {}, {}, {}, {}

    def start(name, groups, names, scatter, order_after):
        flags = scatter if isinstance(scatter, tuple) else (scatter,) * len(names)
        handle, token = _exchange_start(name, [groups[n] for n in names], flags, order_after)
        return (handle, names, flags), token

    def finish(name, pending, order_after):
        handle, names, flags = pending
        return dict(zip(names, _exchange_wait(name, handle, flags, order_after)))

    def gather_next(l, order_after):
        if l + 1 >= depth:
            return None
        gathers[l + 1], token = start("all_gather_start", _shard_groups(w, l + 1), GROUPS, False, order_after)
        return token

    def layer_weights(l, h):
        views = {}
        if l > 0:
            got = finish("all_gather_wait", gathers.pop(l), h)
            views.update(_gathered_views(got, gather_next(l, got[GROUPS[0]])))
            return lambda name, after: views[name]

        shards = _shard_groups(w, 0)
        pending, token = start("all_gather_start_0", shards, FIRST_PARTS[0], False, h)
        got = finish("all_gather_wait_0", pending, token)
        waiting = {}
        token = got[FIRST_PARTS[0][0]]
        for i, part in enumerate(FIRST_PARTS[1:], 1):
            waiting[i], token = start(f"all_gather_start_{i}", shards, part, False, token)
        views.update(_gathered_views(got, gather_next(0, token)))

        def weight(name, after):
            for i in sorted(waiting):
                if name not in views:
                    views.update(_gathered_views(finish(f"all_gather_wait_{i}", waiting.pop(i), after)))
            return views[name]

        return weight

    small_l = lambda t, l: _pack_small({n: t[n][l] for n in SMALL})
    scatter_flags = lambda names: tuple(n != "small" for n in names)
    late = ("g1", "u1", "d1", "small")
    early = tuple(g for g in GROUPS if g not in late)

    def layer_partial(l, wg):
        if l > 0:
            return None
        scatters["early"], token = start("reduce_scatter_start_a", _grad_groups(wg), early, scatter_flags(early),
                                         wg["w_in"])
        return token[0, 0]

    def layer_done(l, wg, gr, dh):
        if l + 1 < depth:
            recv[l + 1] = finish("reduce_scatter_wait", scatters.pop(l + 1), dh)
        groups = dict(_grad_groups(wg), small=_pack_small(gr))
        names = late if l == 0 else GROUPS + ("small",)
        scatters[l], tokens[l] = start("reduce_scatter_start_b" if l == 0 else "reduce_scatter_start", groups, names,
                                       scatter_flags(names), dh)
        return tokens[l][0, 0]

    loss, dx, w_grads, small_grads = _local_step(x[0], p[:, 0], positions[0], loss_target[0], w, layer_weights,
                                                 layer_done, layer_partial)
    loss = lax.psum(loss, MESH_AXES)

    results = {n: None for n in SHARDED}
    last = tokens[0]
    for l in reversed(range(1, depth)):
        for n in SHARDED:
            *results[n], last = _update_sharded(n, recv[l], w[n], mom[n], var[n], l, results[n], last)
    recv[0] = dict(finish("reduce_scatter_wait_a", scatters.pop("early"), last))
    recv[0].update(finish("reduce_scatter_wait_b", scatters.pop(0), last))
    for n in SHARDED:
        results[n] = _update_sharded(n, recv[0], w[n], mom[n], var[n], 0, results[n])
    result = dict(results)

    recv_small = jnp.concatenate([recv[l]["small"] for l in range(depth)], axis=1)
    packed = [jnp.concatenate([small_l(t, l) for l in range(depth)], axis=0) for t in (w, mom, var)]
    small_out = _adamw("adamw_small", recv_small, (N_DEV, 8, SMALL_LANES), lambda i: (0, i, 0), *packed, 8)
    for q in range(4):
        flat = small_out[q].reshape(depth, -1)
        off = 0
        for n in SMALL:
            size = math.prod(w[n].shape[1:])
            result.setdefault(n, [None] * 4)[q] = flat[:, off:off + size].reshape(w[n].shape)
            off += size

    outputs = [loss, dx[None]]
    for q in range(4):
        outputs += [result[n][q] for n in WEIGHTS]
    return tuple(outputs)
```

```python
import functools
import math

import jax
import jax.numpy as jnp
from jax import lax
from jax.experimental import pallas as pl
from jax.experimental.pallas import tpu as pltpu

F32 = jnp.float32
BF16 = jnp.bfloat16

N_DEV = 8
D_MODEL = 1024
D_FF = 2816
FF_SHARD = D_FF // N_DEV
D_ATTN = 512
D_SSM = 512
HEAD_DIM = 64
N_HEADS = 8
ROPE_DIM = 16
ROPE_THETA = 500000.0
DILATIONS = (1, 4, 16)
BAND = 128
N_GROUPS = 32
SSM_GROUP = 16
SSM_STATE = 64
N_STATE = N_GROUPS * SSM_STATE
PLE_DIM = 256
NORM_EPS = 1e-6
ADAM_LR, ADAM_B1, ADAM_B2, ADAM_EPS, ADAM_WD, ADAM_STEP = 0.001, 0.9, 0.999, 1e-08, 0.01, 10

VMEM_LIMIT_BYTES = 52 * 1024 * 1024
TM = 512
MESH_AXES = ("x", "y", "c")

WEIGHTS = ['ffn1_pre_g', 'ffn1_w_gate', 'ffn1_w_up', 'ffn1_w_down', 'ffn1_post_g', 'mix_pre_g', 'w_in', 'attn_norm_g',
           'ssm_lam_re', 'ssm_lam_im', 'ssm_log_dt', 'ssm_b_re', 'ssm_b_im', 'ssm_c_re', 'ssm_c_im', 'ssm_d',
           'ssm_w_glu', 'ssm_b_glu', 'ssm_norm_g', 'w_out', 'mix_post_g', 'ffn2_pre_g', 'ffn2_w_gate', 'ffn2_w_up',
           'ffn2_w_down', 'ffn2_post_g', 'ple_w_up', 'ple_w_gate', 'ple_post_g']
SHARDED = ['ffn1_w_gate', 'ffn1_w_up', 'ffn1_w_down', 'w_in', 'ssm_w_glu', 'w_out', 'ffn2_w_gate', 'ffn2_w_up',
           'ffn2_w_down', 'ple_w_up', 'ple_w_gate']
SMALL = [n for n in WEIGHTS if n not in SHARDED]
SMALL_LANES = 1024
SMALL_PAD = 8 * SMALL_LANES


def _params(n_axes):
    return pltpu.CompilerParams(dimension_semantics=("arbitrary",) * n_axes, vmem_limit_bytes=VMEM_LIMIT_BYTES)


_DIMS = {"nn": (((1,), (0,)), ((), ())), "nt": (((1,), (1,)), ((), ())), "tn": (((0,), (0,)), ((), ()))}


def _dot(a, b, mode):
    return lax.dot_general(a.astype(BF16), b.astype(BF16), _DIMS[mode], preferred_element_type=F32)


def _store(out_refs, vals, kinds, first):
    for ref, val, kind in zip(out_refs, vals, kinds):
        if isinstance(val, (list, tuple)):
            for q, piece in enumerate(val):
                ref[q] = piece.astype(ref.dtype)
        elif kind == "row":
            ref[...] = val.astype(ref.dtype)
        else:
            @pl.when(first)
            def _(ref=ref, val=val):
                ref[...] = val.astype(ref.dtype)

            @pl.when(jnp.logical_not(first))
            def _(ref=ref, val=val):
                ref[...] += val.astype(ref.dtype)


def _call(name, body, grid, ins, outs, scratch, into):
    arrays = [t[0] for t in ins]
    in_specs = [_in_spec(t) for t in ins]
    aliases = {}
    for arr, k in into:
        arrays.append(arr)
        in_specs.append(pl.BlockSpec(memory_space=pl.ANY))
        aliases[len(arrays) - 1] = k
    return pl.pallas_call(
        body,
        name=name,
        grid=grid,
        in_specs=in_specs,
        out_specs=[pl.BlockSpec(o[2], o[3]) for o in outs],
        out_shape=[jax.ShapeDtypeStruct(o[0], o[1]) for o in outs],
        scratch_shapes=list(scratch),
        input_output_aliases=aliases,
        compiler_params=_params(len(grid)),
    )(*arrays)


def _mm(name, grid, a, b, mode, outs, extras=(), epilogue=None, acc_shape=None, scratch=(), into=(), more=(),
        a_prep=None, b_prep=None):
    prep_a = a_prep if a_prep is not None else (lambda t: t)
    prep_b = b_prep if b_prep is not None else (lambda t: t)
    nk = grid[2]
    ne, no = len(extras), len(outs)
    kinds = [o[4] for o in outs]
    assert all(k == "row" for k in kinds) or grid[1] == 1
    n_ab = 2 + 2 * len(more)
    n_in = n_ab + ne + len(into)
    n_acc = int(nk > 1)

    def body(*refs):
        ex_refs = refs[n_ab:n_ab + ne]
        out_refs = refs[n_in:n_in + no]
        scr_refs = refs[n_in + no + n_acc:]
        part = _dot(prep_a(refs[0][...]), prep_b(refs[1][...]), mode)
        for p in range(2, n_ab, 2):
            part = part + _dot(prep_a(refs[p][...]), prep_b(refs[p + 1][...]), mode)
        first = pl.program_id(0) == 0

        def finish(acc):
            vals = epilogue(acc, *[r[...] for r in ex_refs], *scr_refs) if epilogue is not None else (acc,)
            _store(out_refs, vals, kinds, first)

        if nk == 1:
            finish(part)
        else:
            acc_ref = refs[n_in + no]
            k = pl.program_id(2)

            @pl.when(k == 0)
            def _():
                acc_ref[...] = part

            @pl.when(k > 0)
            def _():
                acc_ref[...] += part

            @pl.when(k == nk - 1)
            def _():
                finish(acc_ref[...])

    acc = [pltpu.VMEM(acc_shape, F32)] if nk > 1 else []
    pairs = tuple(t for pair in more for t in pair)
    return _call(name, body, grid, (a, b) + pairs + tuple(extras), outs, acc + list(scratch), into)


def _rowwise(name, n_rows, tm, ins, outs, fn, scratch=(), into=()):
    kinds = [o[4] for o in outs]
    ni, no = len(ins), len(outs)
    n_in = ni + len(into)

    def body(*refs):
        vals = fn(*[r[...] for r in refs[:ni]], *refs[n_in + no:])
        _store(refs[n_in:n_in + no], vals, kinds, pl.program_id(0) == 0)

    return _call(name, body, (n_rows // tm,), ins, outs, scratch, into)


def _rows(arr, tm, col=0, width=None):
    width = arr.shape[1] if width is None else width
    return (arr, (tm, width), lambda i, *_: (i, col))


def _whole(arr):
    nd = arr.ndim
    return (arr, arr.shape, lambda *_: (0,) * nd)


def _resident(arr):
    nd = arr.ndim
    return (arr, arr.shape, lambda *_: (0,) * nd, dict(pipeline_mode=pl.Buffered(1)))


def _in_spec(t):
    return pl.BlockSpec(t[1], t[2], **(t[3] if len(t) > 3 else {}))


def _row_out(n_rows, width, dtype, tm, col=0, total=None):
    return ((n_rows, width if total is None else total), dtype, (tm, width), lambda i, *_: (i, col), "row")


def _col_out(width):
    return ((1, width), F32, (1, width), lambda *_: (0, 0), "colsum")


def _lane_blocks(acc):
    return [acc[:, q * LANES:(q + 1) * LANES] for q in range(acc.shape[1] // LANES)]


def _join_lanes(blk):
    return jnp.concatenate([blk[q] for q in range(blk.shape[0])], axis=1)


def _column_shards(acc, width):
    return [acc[:, q * width:(q + 1) * width] for q in range(acc.shape[1] // width)]


def _natural_cols(name, parts, after=None):
    rows, width = parts[0].shape[1:]
    tr = min(rows, 256)
    n = len(parts)

    def body(*refs):
        out_ref = refs[-1]
        for t in range(n):
            for j in range(N_DEV):
                lo = (t * N_DEV + j) * width
                out_ref[:, lo:lo + width] = refs[t][j]

    ins = [(t, (N_DEV, tr, width), lambda i: (0, i, 0)) for t in parts]
    if after is not None:
        ins.append(_whole(after))
    out = _row_out(rows, n * N_DEV * width, parts[0].dtype, tr)
    return _call(name, body, (rows // tr,), ins, [out], (), ())[0]


def _rms(x, g):
    return x * lax.rsqrt(jnp.mean(x * x, axis=-1, keepdims=True) + NORM_EPS) * g


def _sigmoid(x):
    return 1.0 / (1.0 + jnp.exp(-x))


def _gelu(x):
    return 0.5 * x * (1.0 + jnp.tanh(0.7978845608028654 * (x + 0.044715 * x * x * x)))


FFN_TM = 256
FF_HALF = D_FF // 2


def _ffn_fwd(s_len, h, n, weight, g_post, g_next):
    def gu_epi(acc):
        gate, up = acc[:, :D_FF], acc[:, D_FF:]
        return acc, gate * _sigmoid(gate) * up

    w_gu = weight("gu", n)
    gu, act = _mm("ffn_gate_up", (s_len // FFN_TM, 1, 1), _rows(n, FFN_TM), _resident(w_gu), "nn",
                  [_row_out(s_len, 2 * D_FF, BF16, FFN_TM), _row_out(s_len, D_FF, BF16, FFN_TM)], epilogue=gu_epi)
    w_d = weight("d", act)

    def down_epi(acc, h_blk, gp, gn):
        h_new = h_blk + 0.5 * _rms(acc, gp)
        return acc, h_new, _rms(h_new, gn)

    f, h_new, n_next = _mm(
        "ffn_down", (s_len // TM, 1, 1), _rows(act, TM), _resident(w_d), "nn",
        [_row_out(s_len, D_MODEL, F32, TM), _row_out(s_len, D_MODEL, F32, TM), _row_out(s_len, D_MODEL, BF16, TM)],
        extras=[_rows(h, TM), _whole(g_post), _whole(g_next)], epilogue=down_epi)
    return h_new, n_next, dict(n=n, gu=gu, act=act, f=f, h=h)


def _ffn_bwd(s_len, saved, df, w_gu, w_d, final_epi, final_extras, final_outs):
    nt = s_len // TM

    def act_epi(acc, gu_blk):
        gate = gu_blk[:, :D_FF].astype(F32)
        up = gu_blk[:, D_FF:].astype(F32)
        sg = _sigmoid(gate)
        dgate = acc * up * sg * (1.0 + gate * (1.0 - sg))
        dup = acc * gate * sg
        return (jnp.concatenate([dgate, dup], axis=1),)

    (dgu,) = _mm("ffn_bwd_act", (s_len // FFN_TM, 1, 1), _rows(df, FFN_TM), _resident(w_d), "nt",
                 [_row_out(s_len, 2 * D_FF, BF16, FFN_TM)], extras=[_rows(saved["gu"], FFN_TM)], epilogue=act_epi)

    (d_w_d,) = _mm(
        "ffn_bwd_wdown", (2, 1, nt), (saved["act"], (TM, FF_HALF), lambda i, j, k: (k, i)),
        (df, (TM, D_MODEL), lambda i, j, k: (k, 0)), "tn",
        [((D_FF, D_MODEL), BF16, (FF_HALF, D_MODEL), lambda i, j, k: (i, 0), "row")], acc_shape=(FF_HALF, D_MODEL))

    def w_grad(name, half):
        (dw,) = _mm(
            name, (1, 2, nt), (saved["n"], (TM, D_MODEL), lambda i, j, k: (k, 0)),
            (dgu, (TM, FF_HALF), lambda i, j, k: (k, 2 * half + j)), "tn",
            [((N_DEV, D_MODEL, FF_SHARD), BF16, (N_DEV // 2, D_MODEL, FF_SHARD), lambda i, j, k: (j, 0, 0), "row")],
            epilogue=lambda acc: (_column_shards(acc, FF_SHARD),), acc_shape=(D_MODEL, FF_HALF))
        return dw

    d_w_gate, d_w_up = w_grad("ffn_bwd_wgate", 0), w_grad("ffn_bwd_wup", 1)

    outs = _mm("ffn_bwd_dn", (s_len // FFN_TM, 1, 1), _rows(dgu, FFN_TM), _resident(w_gu), "nt",
               final_outs, extras=final_extras, epilogue=final_epi)
    return d_w_gate, d_w_up, d_w_d, outs


def _band_mask(first_block):
    qi = lax.broadcasted_iota(jnp.int32, (BAND, 2 * BAND), 0)
    kj = lax.broadcasted_iota(jnp.int32, (BAND, 2 * BAND), 1)
    ok = (kj >= qi) & (kj <= qi + BAND)
    return ok & (jnp.logical_not(first_block) | (kj >= BAND))


def _attn_views(s_len, d):
    n_str = s_len // d
    nb = n_str // BAND
    blk = (BAND, D_ATTN)
    cur = lambda r, b: (b, r)
    prev = lambda r, b: (jnp.maximum(b - 1, 0), r)
    return n_str, nb, blk, cur, prev


STREAM_LANES = 128


def _stream_scratch(tm):
    return pltpu.VMEM((D_ATTN // STREAM_LANES, tm, STREAM_LANES), F32)


def _to_streams(x, d, scr):
    if d == 1:
        return x
    tm = x.shape[0]
    chunks = D_ATTN // STREAM_LANES
    for c in range(chunks):
        scr[c, 0:tm, :] = x[:, c * STREAM_LANES:(c + 1) * STREAM_LANES]
    return jnp.concatenate([scr.at[c][pl.ds(r, tm // d, stride=d), :] for r in range(d) for c in range(chunks)], axis=1)


def _from_streams(v, d, scr):
    if d == 1:
        return v
    rows = v.shape[0]
    chunks = D_ATTN // STREAM_LANES
    for r in range(d):
        for c in range(chunks):
            lo = r * D_ATTN + c * STREAM_LANES
            scr.at[c][pl.ds(r, rows, stride=d), :] = v[:, lo:lo + STREAM_LANES]
    return jnp.concatenate([scr[c, 0:rows * d, :] for c in range(chunks)], axis=1)


def _stream_in(arr, tm, d, ahead=0, n_blocks=None):
    rows = tm // d
    step = ahead // rows
    if ahead:
        return (arr, (rows, d * D_ATTN), lambda i, *_: (jnp.minimum(i + step, n_blocks - 1), 0))
    return (arr, (rows, d * D_ATTN), lambda i, *_: (i, 0))


def _stream_out(s_len, dtype, tm, d):
    return ((s_len // d, d * D_ATTN), dtype, (tm // d, d * D_ATTN), lambda i, *_: (i, 0), "row")


def _attn_fwd(s_len, q, k, v, d):
    n_str, nb, blk, cur, prev = _attn_views(s_len, d)
    view = lambda t: t

    def body(q_ref, kp_ref, kc_ref, vp_ref, vc_ref, o_ref, lse_ref):
        mask = _band_mask(pl.program_id(1) == 0)
        qq = q_ref[...]
        kk = jnp.concatenate([kp_ref[...], kc_ref[...]], axis=0)
        vv = jnp.concatenate([vp_ref[...], vc_ref[...]], axis=0)
        for h in range(N_HEADS):
            sl = slice(h * HEAD_DIM, (h + 1) * HEAD_DIM)
            s = _dot(qq[:, sl], kk[:, sl], "nt") * (HEAD_DIM ** -0.5)
            s = jnp.where(mask, s, -1e30)
            m = jnp.max(s, axis=-1, keepdims=True)
            e = jnp.exp(s - m)
            den = jnp.sum(e, axis=-1, keepdims=True)
            o_ref[:, sl] = _dot(e / den, vv[:, sl], "nn")
            lse_ref[:, sl] = jnp.broadcast_to(m + jnp.log(den), (BAND, HEAD_DIM))

    o, lse = pl.pallas_call(
        body,
        name=f"attn_fwd_d{d}",
        grid=(d, nb),
        in_specs=[pl.BlockSpec(blk, cur), pl.BlockSpec(blk, prev), pl.BlockSpec(blk, cur),
                  pl.BlockSpec(blk, prev), pl.BlockSpec(blk, cur)],
        out_specs=[pl.BlockSpec(blk, cur), pl.BlockSpec(blk, cur)],
        out_shape=[jax.ShapeDtypeStruct((n_str, d * D_ATTN), F32)] * 2,
        compiler_params=_params(2),
    )(view(q), view(k), view(k), view(v), view(v))
    return o, lse


def _attn_bwd(s_len, q, k, v, dattn, attn, lse, d):
    n_str, nb, blk, cur, prev = _attn_views(s_len, d)
    view = lambda t: t

    def body(q_ref, kp_ref, kc_ref, vp_ref, vc_ref, da_ref, at_ref, lse_ref, dq_ref, dka_ref, dkb_ref, dva_ref, dvb_ref):
        mask = _band_mask(pl.program_id(1) == 0)
        qq = q_ref[...]
        kk = jnp.concatenate([kp_ref[...], kc_ref[...]], axis=0)
        vv = jnp.concatenate([vp_ref[...], vc_ref[...]], axis=0)
        da = da_ref[...]
        prod = da * at_ref[...]
        scale = HEAD_DIM ** -0.5
        for h in range(N_HEADS):
            sl = slice(h * HEAD_DIM, (h + 1) * HEAD_DIM)
            s = _dot(qq[:, sl], kk[:, sl], "nt") * scale
            p = jnp.where(mask, jnp.exp(s - lse_ref[:, h * HEAD_DIM:h * HEAD_DIM + 1]), 0.0)
            dp = _dot(da[:, sl], vv[:, sl], "nt")
            ds = p * (dp - jnp.sum(prod[:, sl], axis=-1, keepdims=True))
            dq_ref[:, sl] = _dot(ds, kk[:, sl], "nn") * scale
            dk = _dot(ds, qq[:, sl], "tn") * scale
            dv = _dot(p, da[:, sl], "tn")
            dkb_ref[:, sl] = dk[:BAND]
            dka_ref[:, sl] = dk[BAND:]
            dvb_ref[:, sl] = dv[:BAND]
            dva_ref[:, sl] = dv[BAND:]

    outs = pl.pallas_call(
        body,
        name=f"attn_bwd_d{d}",
        grid=(d, nb),
        in_specs=[pl.BlockSpec(blk, cur), pl.BlockSpec(blk, prev), pl.BlockSpec(blk, cur),
                  pl.BlockSpec(blk, prev), pl.BlockSpec(blk, cur),
                  pl.BlockSpec(blk, cur), pl.BlockSpec(blk, cur), pl.BlockSpec(blk, cur)],
        out_specs=[pl.BlockSpec(blk, cur)] * 5,
        out_shape=[jax.ShapeDtypeStruct((n_str, d * D_ATTN), F32)] * 5,
        compiler_params=_params(2),
    )(view(q), view(k), view(k), view(v), view(v), view(dattn), view(attn), view(lse))
    return list(outs)


def _rope_tables(positions):
    half = ROPE_DIM // 2
    inv_freq = ROPE_THETA ** (-jnp.arange(half, dtype=F32) * (2.0 / ROPE_DIM))
    ang = positions.astype(F32)[:, None] * inv_freq
    cos, sin = jnp.cos(ang), jnp.sin(ang)
    s_len = positions.shape[0]
    one = jnp.ones((s_len, HEAD_DIM - ROPE_DIM), F32)
    zero8 = jnp.zeros((s_len, half), F32)
    zero = jnp.zeros((s_len, HEAD_DIM - ROPE_DIM), F32)
    c = jnp.concatenate([cos, cos, one], axis=1)
    s1 = jnp.concatenate([zero8, sin, zero], axis=1)
    s2 = jnp.concatenate([-sin, zero8, zero], axis=1)
    tile = lambda t: jnp.tile(t, (1, N_HEADS))
    return tile(c), tile(s1), tile(s2)


def _rope(t, c, s1, s2):
    half = ROPE_DIM // 2
    return t * c + pltpu.roll(t, half, 1) * s1 + pltpu.roll(t, D_ATTN - half, 1) * s2


def _rope_transposed(dt, c, s1, s2):
    half = ROPE_DIM // 2
    return dt * c + pltpu.roll(dt * s1, D_ATTN - half, 1) + pltpu.roll(dt * s2, half, 1)


SCAN_ROWS = 256
SCAN_CHUNK = 128
LANES = 128
SUBLANES = 8


def _cmul(xr, xi, yr, yi):
    return xr * yr - xi * yi, xr * yi + xi * yr


def _scan(name, bu, a_cat, reverse):
    s_len = bu.shape[1]
    n_blocks = bu.shape[0] // 2
    n_chunks = s_len // SCAN_CHUNK
    log_chunk = SCAN_CHUNK.bit_length() - 1
    rows_of = lambda off: pl.ds(pl.multiple_of(off * n_chunks, n_chunks), n_chunks)

    def body(br_ref, bi_ref, ar_ref, ai_ref, xr_ref, xi_ref, pr_ref, pi_ref, cr_ref, ci_ref):
        ar, ai = ar_ref[...], ai_ref[...]

        def step(i, state):
            rows = rows_of((SCAN_CHUNK - 1 - i) if reverse else i)
            pr, pi = _cmul(ar, ai, *state)
            nr, ni = pr + br_ref[rows, :], pi + bi_ref[rows, :]
            xr_ref[rows, :] = nr
            xi_ref[rows, :] = ni
            return nr, ni

        zero = jnp.zeros((n_chunks, LANES), F32)
        lax.fori_loop(0, SCAN_CHUNK, step, (zero, zero), unroll=4)

        pw = [(ar, ai)]
        for _ in range(log_chunk + n_chunks.bit_length()):
            pw.append(_cmul(*pw[-1], *pw[-1]))

        last = pl.ds((0 if reverse else SCAN_CHUNK - 1) * n_chunks, n_chunks)
        er, ei = xr_ref[last, :], xi_ref[last, :]
        chunk = lax.broadcasted_iota(jnp.int32, (n_chunks, LANES), 0)

        def shifted(v, s):
            if reverse:
                return jnp.where(chunk < n_chunks - s, pltpu.roll(v, n_chunks - s, 0), 0.0)
            return jnp.where(chunk >= s, pltpu.roll(v, s, 0), 0.0)

        s, b = 1, log_chunk
        while s < n_chunks:
            mr, mi = _cmul(*pw[b], shifted(er, s), shifted(ei, s))
            er, ei = er + mr, ei + mi
            s, b = 2 * s, b + 1
        cr_ref[...] = shifted(er, 1)
        ci_ref[...] = shifted(ei, 1)

        step_no = lax.broadcasted_iota(jnp.int32, (SUBLANES, LANES), 0)
        expo = (SUBLANES - step_no) if reverse else (step_no + 1)
        qr, qi = jnp.ones((SUBLANES, LANES), F32), jnp.zeros((SUBLANES, LANES), F32)
        for bit in range(4):
            mr, mi = _cmul(qr, qi, *pw[bit])
            hit = (expo & (1 << bit)) != 0
            qr, qi = jnp.where(hit, mr, qr), jnp.where(hit, mi, qi)
        lo = SCAN_CHUNK - SUBLANES if reverse else 0
        pr_ref[lo:lo + SUBLANES, :] = qr
        pi_ref[lo:lo + SUBLANES, :] = qi
        m, b = SUBLANES, 3
        while m < SCAN_CHUNK:
            src = pl.ds(SCAN_CHUNK - m, m) if reverse else pl.ds(0, m)
            dst = pl.ds(SCAN_CHUNK - 2 * m, m) if reverse else pl.ds(m, m)
            mr, mi = _cmul(pr_ref[src, :], pi_ref[src, :], *pw[b])
            pr_ref[dst, :] = mr
            pi_ref[dst, :] = mi
            m, b = 2 * m, b + 1

        cr, ci = cr_ref[...], ci_ref[...]

        def fix(i, _):
            rows = rows_of(i)
            mr, mi = _cmul(pr_ref[pl.ds(i, 1), :], pi_ref[pl.ds(i, 1), :], cr, ci)
            xr_ref[rows, :] += mr
            xi_ref[rows, :] += mi
            return 0

        lax.fori_loop(0, SCAN_CHUNK, fix, 0, unroll=4)

    blk = lambda off: pl.BlockSpec((None, s_len, LANES), lambda c: (off + c, 0, 0))
    col = lambda off: pl.BlockSpec((1, LANES), lambda c: (0, off + c))
    return pl.pallas_call(
        body,
        name=name,
        grid=(n_blocks,),
        in_specs=[blk(0), blk(n_blocks), col(0), col(n_blocks)],
        out_specs=[blk(0), blk(0)],
        out_shape=[jax.ShapeDtypeStruct((n_blocks, s_len, LANES), F32)] * 2,
        scratch_shapes=[pltpu.VMEM((SCAN_CHUNK, LANES), F32)] * 2 + [pltpu.VMEM((n_chunks, LANES), F32)] * 2,
        compiler_params=_params(1),
    )(bu, bu, a_cat, a_cat)


def _to_chunk_order(t):
    s_len, width = t.shape
    return t.reshape(s_len // SCAN_CHUNK, SCAN_CHUNK, width).transpose(1, 0, 2).reshape(s_len, width)


def _from_chunk_order(t):
    s_len, width = t.shape
    return t.reshape(SCAN_CHUNK, s_len // SCAN_CHUNK, width).transpose(1, 0, 2).reshape(s_len, width)


def _state_in(arr, tm):
    return (arr, (arr.shape[0], tm, LANES), lambda i, *_: (0, i, 0))


def _state_out(s_len, width, tm):
    n = width // LANES
    return ((n, s_len, LANES), F32, (n, tm, LANES), lambda i, *_: (0, i, 0), "row")


def _ssm_params(lam_re, lam_im, log_dt, b_re, b_im, c_re, c_im):
    dt = jnp.exp(log_dt)[:, None]
    er = jnp.exp(lam_re * dt)
    a_re = er * jnp.cos(lam_im * dt)
    a_im = er * jnp.sin(lam_im * dt)
    nr, ni = a_re - 1.0, a_im
    den = lam_re * lam_re + lam_im * lam_im
    fr = (nr * lam_re + ni * lam_im) / den
    fi = (ni * lam_re - nr * lam_im) / den
    bb_re = fr[..., None] * b_re - fi[..., None] * b_im
    bb_im = fr[..., None] * b_im + fi[..., None] * b_re
    eye = jnp.eye(N_GROUPS, dtype=F32)

    def in_mat(bb):
        t = bb.transpose(0, 2, 1)[:, :, None, :] * eye[:, None, :, None]
        return t.reshape(D_SSM, N_STATE)

    def out_mat(cc):
        t = cc.transpose(0, 2, 1)[:, :, None, :] * eye[:, None, :, None]
        return t.reshape(N_STATE, D_SSM)

    w_b = jnp.concatenate([in_mat(bb_re), in_mat(bb_im)], axis=1)
    w_c = jnp.concatenate([out_mat(c_re), -out_mat(c_im)], axis=0)
    a_cat = jnp.concatenate([a_re.reshape(1, N_STATE), a_im.reshape(1, N_STATE)], axis=1)
    return a_cat, w_b, w_c


def _conj(a_cat):
    return jnp.concatenate([a_cat[:, :N_STATE], -a_cat[:, N_STATE:]], axis=1)


def _layer_fwd(s_len, h, n1, p_l, rope, weight, sp, gains, g_next):
    nt = s_len // TM
    sv = {}
    gw = {}

    def need(after, *names):
        for name in names:
            gw[name] = weight(name, after)

    def ffn_weight(which):
        def get(kind, after):
            need(after, f"w_{kind}{which}")
            return gw[f"w_{kind}{which}"]
        return get

    h1, a_in, sv["ffn1"] = _ffn_fwd(s_len, h, n1, ffn_weight(1), gains["ffn1_post_g"], gains["mix_pre_g"])

    need(h1, "w_in", "w_glu", "w_out")
    (proj,) = _mm("w_in", (nt, 1, 1), _rows(a_in, TM), _resident(gw["w_in"]), "nn",
                  [_row_out(s_len, 2 * D_MODEL, F32, TM)])

    c, s1, s2 = rope

    def rope_fn(tq, tk, tv, cc, a1, a2, scr):
        q, k = _rope(tq, cc, a1, a2), _rope(tk, cc, a1, a2)
        return [_to_streams(t, d, scr) for d in DILATIONS for t in (q, k, tv)]

    qkv = _rowwise(
        "rope", s_len, TM,
        [_rows(proj, TM, 0, D_ATTN), _rows(proj, TM, 1, D_ATTN), _rows(proj, TM, 2, D_ATTN),
         _rows(c, TM), _rows(s1, TM), _rows(s2, TM)],
        [_stream_out(s_len, BF16, TM, d) for d in DILATIONS for _ in range(3)], rope_fn,
        scratch=[_stream_scratch(TM)])
    qkv = {d: qkv[3 * i:3 * i + 3] for i, d in enumerate(DILATIONS)}
    parts = {d: _attn_fwd(s_len, *qkv[d], d) for d in DILATIONS}

    def mix_fn(*args):
        g, scr = args[6], args[7]
        o1, l1, o2, l2, o3, l3 = [_from_streams(args[2 * i + j], d, scr) for i, d in enumerate(DILATIONS)
                                  for j in range(2)]
        m = jnp.maximum(jnp.maximum(l1, l2), l3)
        e1, e2, e3 = jnp.exp(l1 - m), jnp.exp(l2 - m), jnp.exp(l3 - m)
        tot = e1 + e2 + e3
        attn = (e1 * o1 + e2 * o2 + e3 * o3) / tot
        lse = m + jnp.log(tot)
        return [attn, lse, _rms(attn, g)] + [_to_streams(t, d, scr) for d in DILATIONS[1:] for t in (attn, lse)]

    mix_out = _rowwise(
        "attn_mix", s_len, TM,
        [_stream_in(t, TM, d) for d in DILATIONS for t in parts[d]] + [_whole(gains["attn_norm_g"])],
        [_row_out(s_len, D_ATTN, F32, TM), _row_out(s_len, D_ATTN, F32, TM),
         _row_out(s_len, D_ATTN, BF16, TM, col=0, total=D_MODEL)]
        + [_stream_out(s_len, F32, TM, d) for d in DILATIONS[1:] for _ in range(2)], mix_fn,
        scratch=[_stream_scratch(TM)])
    attn, lse, mixed_half = mix_out[:3]
    attn_s = {1: (attn, lse), DILATIONS[1]: tuple(mix_out[3:5]), DILATIONS[2]: tuple(mix_out[5:7])}

    a_cat, w_b, w_c, d_vec = sp
    ts = SCAN_ROWS
    u_c = _to_chunk_order(proj[:, 3 * D_ATTN:])
    (bu,) = _mm("ssm_bu", (s_len // ts, 1, 1), _rows(u_c, ts), _whole(w_b), "nn",
                [_state_out(s_len, 2 * N_STATE, ts)], epilogue=lambda acc: (_lane_blocks(acc),))
    xs = _scan("ssm_scan", bu, a_cat, False)
    w_c_half = lambda part: (w_c, (N_STATE, D_SSM), lambda *_: (part, 0))
    (z_c,) = _mm("ssm_y", (s_len // ts, 1, 1), _state_in(xs[0], ts), w_c_half(0), "nn",
                 [_row_out(s_len, D_SSM, F32, ts)], more=[(_state_in(xs[1], ts), w_c_half(1))], a_prep=_join_lanes,
                 extras=[_rows(u_c, ts), _whole(d_vec)], epilogue=lambda acc, u, dv: (acc + dv * u,))
    z = _from_chunk_order(z_c)

    def glu_epi(acc, z_blk, b, g):
        y = _gelu(z_blk)
        t = acc + b
        ssm = y * _sigmoid(t)
        return y, t, ssm, _rms(ssm, g)

    yg, t_glu, ssm, mixed = _mm(
        "ssm_glu", (nt, 1, 1), _rows(z, TM), _whole(gw["w_glu"]), "nn",
        [_row_out(s_len, D_SSM, F32, TM), _row_out(s_len, D_SSM, F32, TM), _row_out(s_len, D_SSM, F32, TM),
         _row_out(s_len, D_SSM, BF16, TM, col=1, total=D_MODEL)], a_prep=_gelu,
        extras=[_rows(z, TM), _whole(gains["ssm_b_glu"]), _whole(gains["ssm_norm_g"])], epilogue=glu_epi,
        into=[(mixed_half, 3)])

    def out_epi(acc, h_blk, gp, gn):
        h_new = h_blk + _rms(acc, gp)
        return acc, h_new, _rms(h_new, gn)

    o, h2, n2 = _mm(
        "w_out", (nt, 1, 1), _rows(mixed, TM), _whole(gw["w_out"]), "nn",
        [_row_out(s_len, D_MODEL, F32, TM), _row_out(s_len, D_MODEL, F32, TM), _row_out(s_len, D_MODEL, BF16, TM)],
        extras=[_rows(h1, TM), _whole(gains["mix_post_g"]), _whole(gains["ffn2_pre_g"])], epilogue=out_epi)

    h3, _, sv["ffn2"] = _ffn_fwd(s_len, h2, n2, ffn_weight(2), gains["ffn2_post_g"], gains["ffn2_post_g"])
    need(h3, "w_pu", "ple_w_gate")

    (pu,) = _mm("ple_up", (nt, 1, 1), _rows(p_l, TM), _resident(gw["w_pu"]), "nn",
                [_row_out(s_len, D_MODEL, F32, TM)])

    def ple_epi(acc, pu_blk, h_blk, gp, gn):
        h_new = h_blk + _rms(pu_blk * _sigmoid(acc), gp)
        return acc, h_new, _rms(h_new, gn)

    gt, h4, n_next = _mm(
        "ple_gate", (nt, 1, 1), _rows(h3, TM), _whole(gw["ple_w_gate"]), "nn",
        [_row_out(s_len, D_MODEL, F32, TM), _row_out(s_len, D_MODEL, F32, TM), _row_out(s_len, D_MODEL, BF16, TM)],
        extras=[_rows(pu, TM), _rows(h3, TM), _whole(gains["ple_post_g"]), _whole(g_next)], epilogue=ple_epi)

    sv.update(h1=h1, a_in=a_in, proj=proj, qkv=qkv, attn=attn, attn_s=attn_s, xs=xs, u_c=u_c, z=z, yg=yg, t_glu=t_glu,
              ssm=ssm,
              mixed=mixed, o=o, h2=h2, h3=h3, pu=pu, gt=gt, p_l=p_l, w=gw)
    return h4, n_next, sv


def _vjp(fn, args, cot):
    _, pull = jax.vjp(fn, *args)
    return pull(cot)


def _layer_bwd(s_len, dh4, sv, rope, gw, sp, gains, on_partial=None):
    nt = s_len // TM
    gr = {}
    wg = {}

    def ple_fn(dh, pu, gt, g):
        dpu, dgt, dg = _vjp(lambda a, b, c: _rms(a * _sigmoid(b), c), (pu, gt, g), dh)
        return dpu, dgt, dg

    dpu, dgt, gr["ple_post_g"] = _rowwise(
        "ple_bwd", s_len, TM, [_rows(dh4, TM), _rows(sv["pu"], TM), _rows(sv["gt"], TM), _whole(gains["ple_post_g"])],
        [_row_out(s_len, D_MODEL, BF16, TM), _row_out(s_len, D_MODEL, BF16, TM), _col_out(D_MODEL)], ple_fn)

    def square_w_grad(name, lhs, rhs):
        half = D_MODEL // 2
        (dw,) = _mm(name, (2, 1, nt), (lhs, (TM, half), lambda i, j, k: (k, i)),
                    (rhs, (TM, D_MODEL), lambda i, j, k: (k, 0)), "tn",
                    [((D_MODEL, D_MODEL), BF16, (half, D_MODEL), lambda i, j, k: (i, 0), "row")],
                    acc_shape=(half, D_MODEL))
        return dw

    (wg["ple_w_up"],) = _mm(
        "ple_bwd_wup", (1, 1, nt), (sv["p_l"], (TM, PLE_DIM), lambda i, j, k: (k, 0)),
        (dpu, (TM, D_MODEL), lambda i, j, k: (k, 0)), "tn",
        [((N_DEV, PLE_DIM, D_MODEL // N_DEV), BF16, (N_DEV, PLE_DIM, D_MODEL // N_DEV), lambda i, j, k: (0, 0, 0),
          "row")],
        epilogue=lambda acc: (_column_shards(acc, D_MODEL // N_DEV),), acc_shape=(PLE_DIM, D_MODEL))
    wg["ple_w_gate"] = square_w_grad("ple_bwd_wgate", sv["h3"], dgt)

    def ple_dx_epi(acc, dh, f, g):
        dh3 = dh + acc
        df, dg = _vjp(_rms, (f, g), 0.5 * dh3)
        return dh3, df, dg

    dh3, df2, gr["ffn2_post_g"] = _mm(
        "ple_bwd_dx", (nt, 1, 1), _rows(dgt, TM), _whole(gw["ple_w_gate"]), "nt",
        [_row_out(s_len, D_MODEL, F32, TM), _row_out(s_len, D_MODEL, BF16, TM), _col_out(D_MODEL)],
        extras=[_rows(dh4, TM), _rows(sv["ffn2"]["f"], TM), _whole(gains["ffn2_post_g"])], epilogue=ple_dx_epi)

    def ffn2_final(dn, dh, h, g_pre, o, g_post):
        dx, dg_pre = _vjp(_rms, (h, g_pre), dn)
        dh2 = dh + dx
        do, dg_post = _vjp(_rms, (o, g_post), dh2)
        return dh2, do, dg_pre, dg_post

    wg["ffn2_w_gate"], wg["ffn2_w_up"], wg["ffn2_w_down"], (dh2, do, gr["ffn2_pre_g"], gr["mix_post_g"]) = _ffn_bwd(
        s_len, sv["ffn2"], df2, gw["w_gu2"], gw["w_d2"], ffn2_final,
        [_rows(dh3, FFN_TM), _rows(sv["h2"], FFN_TM), _whole(gains["ffn2_pre_g"]), _rows(sv["o"], FFN_TM),
         _whole(gains["mix_post_g"])],
        [_row_out(s_len, D_MODEL, F32, FFN_TM), _row_out(s_len, D_MODEL, BF16, FFN_TM), _col_out(D_MODEL),
         _col_out(D_MODEL)])

    wg["w_out"] = square_w_grad("w_out_bwd_w", sv["mixed"], do)

    def mixed_epi(acc, attn, ssm, yg, t, g_a, g_s, scr):
        dattn, dg_a = _vjp(_rms, (attn, g_a), acc[:, :D_ATTN])
        dssm, dg_s = _vjp(_rms, (ssm, g_s), acc[:, D_ATTN:])
        sg = _sigmoid(t)
        dt = dssm * yg * sg * (1.0 - sg)
        return ([_to_streams(dattn, d, scr) for d in DILATIONS]
                + [dt, dssm * sg, dg_a, dg_s, jnp.sum(dt, axis=0, keepdims=True)])

    res = _mm(
        "w_out_bwd_x", (nt, 1, 1), _rows(do, TM), _whole(gw["w_out"]), "nt",
        [_stream_out(s_len, F32, TM, d) for d in DILATIONS]
        + [_row_out(s_len, D_SSM, BF16, TM), _row_out(s_len, D_SSM, F32, TM),
           _col_out(D_ATTN), _col_out(D_SSM), _col_out(D_SSM)],
        extras=[_rows(sv["attn"], TM), _rows(sv["ssm"], TM), _rows(sv["yg"], TM), _rows(sv["t_glu"], TM),
                _whole(gains["attn_norm_g"]), _whole(gains["ssm_norm_g"])], epilogue=mixed_epi,
        scratch=[_stream_scratch(TM)])
    dattn_s = dict(zip(DILATIONS, res[:3]))
    dt_glu, dyg_dir, gr["attn_norm_g"], gr["ssm_norm_g"], gr["ssm_b_glu"] = res[3:]

    a_cat, w_b, w_c, d_vec = sp
    ts = SCAN_ROWS
    u_spec = (sv["proj"], (TM, D_SSM), lambda i, *_: (i, 3))
    (wg["ssm_w_glu"],) = _mm(
        "ssm_bwd_wglu", (1, 1, nt), (sv["yg"], (TM, D_SSM), lambda i, j, k: (k, 0)),
        (dt_glu, (TM, D_SSM), lambda i, j, k: (k, 0)), "tn",
        [((D_SSM, D_SSM), BF16, (D_SSM, D_SSM), lambda i, j, k: (0, 0), "row")], acc_shape=(D_SSM, D_SSM))

    def gelu_epi(acc, dy_dir, z, u, dv):
        (dz,) = _vjp(_gelu, (z,), acc + dy_dir)
        return dz, dz * dv, jnp.sum(dz * u, axis=0, keepdims=True)

    dz, du_dir, gr["ssm_d"] = _mm(
        "ssm_bwd_glu", (nt, 1, 1), _rows(dt_glu, TM), _whole(gw["w_glu"]), "nt",
        [_row_out(s_len, D_SSM, BF16, TM), _row_out(s_len, D_SSM, F32, TM), _col_out(D_SSM)],
        extras=[_rows(dyg_dir, TM), _rows(sv["z"], TM), u_spec, _whole(d_vec)], epilogue=gelu_epi)

    x_re, x_im = sv["xs"]
    w_c_half = lambda part: (w_c, (N_STATE, D_SSM), lambda *_: (part, 0))
    w_b_half = lambda part: (w_b, (D_SSM, N_STATE), lambda *_: (0, part))

    n_lb = N_STATE // LANES
    n_chunks = s_len // SCAN_CHUNK
    dz_c = _to_chunk_order(dz)

    def w_c_grad(name, x):
        per = 1024 // LANES
        (dw,) = _mm(name, (N_STATE // 1024, 1, nt), (x, (per, TM, LANES), lambda i, j, k: (i, k, 0)),
                    (dz_c, (TM, D_SSM), lambda i, j, k: (k, 0)), "tn",
                    [((N_STATE, D_SSM), F32, (1024, D_SSM), lambda i, j, k: (i, 0), "row")], acc_shape=(1024, D_SSM),
                    a_prep=_join_lanes)
        return dw

    d_w_c = jnp.concatenate([w_c_grad("ssm_bwd_wc_re", x_re), w_c_grad("ssm_bwd_wc_im", x_im)], axis=0)
    (dxs,) = _mm("ssm_bwd_dx", (s_len // ts, 1, 1), _rows(dz_c, ts), _whole(w_c), "nt",
                 [_state_out(s_len, 2 * N_STATE, ts)], epilogue=lambda acc: (_lane_blocks(acc),))
    g_re, g_im = _scan("ssm_scan_rev", dxs, _conj(a_cat), True)

    def da_fn(xr, xi, xr_before, xi_before, g_r, g_i):
        i = pl.program_id(0)
        chunk = lax.broadcasted_iota(jnp.int32, xr_before.shape, 1)

        def previous(x, x_before):
            wrapped = jnp.where(chunk == 0, 0.0, pltpu.roll(x_before, 1, 1))
            head = jnp.where(i == 0, wrapped, x_before)
            return jnp.concatenate([head, x[:, :ts - n_chunks, :]], axis=1)

        pr, pi = previous(xr, xr_before), previous(xi, xi_before)
        return (jnp.sum(pr * g_r + pi * g_i, axis=1, keepdims=True),
                jnp.sum(pr * g_i - pi * g_r, axis=1, keepdims=True))

    steps = s_len // n_chunks
    before = lambda t: (t, (n_lb, n_chunks, LANES), lambda i: (0, (i * (ts // n_chunks) + steps - 1) % steps, 0))
    lane_sum = ((n_lb, 1, LANES), F32, (n_lb, 1, LANES), lambda *_: (0, 0, 0), "colsum")
    d_a = jnp.concatenate([t.reshape(1, N_STATE) for t in _rowwise(
        "ssm_bwd_da", s_len, ts,
        [_state_in(x_re, ts), _state_in(x_im, ts), before(x_re), before(x_im), _state_in(g_re, ts),
         _state_in(g_im, ts)], [lane_sum] * 2, da_fn)], axis=1)

    def w_b_grad(name, g):
        (dw,) = _mm(name, (1, 1, nt), (sv["u_c"], (TM, D_SSM), lambda i, j, k: (k, 0)),
                    (g, (n_lb, TM, LANES), lambda i, j, k: (0, k, 0)), "tn",
                    [((D_SSM, N_STATE), F32, (D_SSM, N_STATE), lambda i, j, k: (0, 0), "row")],
                    acc_shape=(D_SSM, N_STATE), b_prep=_join_lanes)
        return dw

    d_w_b = jnp.concatenate([w_b_grad("ssm_bwd_wb_re", g_re), w_b_grad("ssm_bwd_wb_im", g_im)], axis=1)
    (du_c,) = _mm("ssm_bwd_du", (s_len // ts, 1, 1), _state_in(g_re, ts), w_b_half(0), "nt",
                  [_row_out(s_len, D_SSM, F32, ts)], more=[(_state_in(g_im, ts), w_b_half(1))], a_prep=_join_lanes)
    du = (_from_chunk_order(du_c) + du_dir).astype(BF16)
    sum_tm = 2 * BAND
    n_sum = s_len // sum_tm
    ins = []
    for d in DILATIONS:
        dq_p, dka, dkb, dva, dvb = _attn_bwd(s_len, *sv["qkv"][d], dattn_s[d], *sv["attn_s"][d], d)
        if d == 1:
            nxt = lambda t: (t, (sum_tm, D_ATTN), lambda i: (jnp.minimum(i + 1, n_sum - 1), 0))
            ins += [_rows(dq_p, sum_tm), _rows(dka, sum_tm), _rows(dkb, sum_tm), nxt(dkb), _rows(dva, sum_tm),
                    _rows(dvb, sum_tm), nxt(dvb)]
        else:
            ins += [_stream_in(dq_p, sum_tm, d), _stream_in(dka, sum_tm, d), _stream_in(dkb, sum_tm, d, BAND, n_sum),
                    _stream_in(dva, sum_tm, d), _stream_in(dvb, sum_tm, d, BAND, n_sum)]
    c, s1, s2 = rope
    ins += [_rows(c, sum_tm), _rows(s1, sum_tm), _rows(s2, sum_tm), _rows(du, sum_tm)]

    def qkv_fn(*args):
        i = pl.program_id(0)
        blocks, scr = args[:-1], args[-1]
        dq_t, dka, dkb, dkb_next, dva, dvb, dvb_next = blocks[:7]
        more = i + 1 < n_sum
        ahead = lambda cur, nxt: jnp.concatenate([cur[BAND:], jnp.where(more, nxt[:BAND], 0.0)], axis=0)
        dk_t = dka + ahead(dkb, dkb_next)
        dv_t = dva + ahead(dvb, dvb_next)
        at = 7
        for d in DILATIONS[1:]:
            dq_p, dka, dkb, dva, dvb = blocks[at:at + 5]
            at += 5
            live = i + BAND // (sum_tm // d) < n_sum
            dq_t = dq_t + _from_streams(dq_p, d, scr)
            dk_t = dk_t + _from_streams(dka + jnp.where(live, dkb, 0.0), d, scr)
            dv_t = dv_t + _from_streams(dva + jnp.where(live, dvb, 0.0), d, scr)
        cc, a1, a2, du_blk = blocks[at:at + 4]
        return (jnp.concatenate([_rope_transposed(dq_t, cc, a1, a2), _rope_transposed(dk_t, cc, a1, a2), dv_t,
                                 du_blk.astype(F32)], axis=1),)

    (dproj,) = _rowwise("attn_bwd_sum", s_len, sum_tm, ins, [_row_out(s_len, 2 * D_MODEL, BF16, sum_tm)], qkv_fn,
                        scratch=[_stream_scratch(sum_tm)])

    (wg["w_in"],) = _mm(
        "w_in_bwd_w", (1, 2, nt), (sv["a_in"], (TM, D_MODEL), lambda i, j, k: (k, 0)),
        (dproj, (TM, D_MODEL), lambda i, j, k: (k, j)), "tn",
        [((N_DEV, D_MODEL, 2 * D_MODEL // N_DEV), BF16, (N_DEV // 2, D_MODEL, 2 * D_MODEL // N_DEV),
          lambda i, j, k: (j, 0, 0), "row")],
        epilogue=lambda acc: (_column_shards(acc, 2 * D_MODEL // N_DEV),), acc_shape=(D_MODEL, D_MODEL))

    tie = on_partial(wg) if on_partial is not None else None
    mix_pre_g = gains["mix_pre_g"] if tie is None else gains["mix_pre_g"] + tie

    def in_epi(acc, dh, h, g_pre, f, g_post):
        dx, dg_pre = _vjp(_rms, (h, g_pre), acc)
        dh1 = dh + dx
        df, dg_post = _vjp(_rms, (f, g_post), 0.5 * dh1)
        return dh1, df, dg_pre, dg_post

    dh1, df1, gr["mix_pre_g"], gr["ffn1_post_g"] = _mm(
        "w_in_bwd_x", (nt, 1, 1), _rows(dproj, TM), _resident(gw["w_in"]), "nt",
        [_row_out(s_len, D_MODEL, F32, TM), _row_out(s_len, D_MODEL, BF16, TM), _col_out(D_MODEL), _col_out(D_MODEL)],
        extras=[_rows(dh2, TM), _rows(sv["h1"], TM), _whole(mix_pre_g), _rows(sv["ffn1"]["f"], TM),
                _whole(gains["ffn1_post_g"])], epilogue=in_epi)

    def ffn1_final(dn, dh, h, g_pre):
        dx, dg_pre = _vjp(_rms, (h, g_pre), dn)
        return dh + dx, dg_pre

    wg["ffn1_w_gate"], wg["ffn1_w_up"], wg["ffn1_w_down"], (dh0, gr["ffn1_pre_g"]) = _ffn_bwd(
        s_len, sv["ffn1"], df1, gw["w_gu1"], gw["w_d1"], ffn1_final,
        [_rows(dh1, FFN_TM), _rows(sv["ffn1"]["h"], FFN_TM), _whole(gains["ffn1_pre_g"])],
        [_row_out(s_len, D_MODEL, F32, FFN_TM), _col_out(D_MODEL)])

    return dh0, wg, gr, (d_a, d_w_b, d_w_c)


def _peers():
    x, y, c = lax.axis_index("x"), lax.axis_index("y"), lax.axis_index("c")
    me = 4 * x + 2 * y + c
    peers = []
    for k in range(1, N_DEV):
        kx, ky, kc = (k >> 2) & 1, (k >> 1) & 1, k & 1
        px, py, pc = x ^ kx, y ^ ky, c ^ kc
        peers.append(((px, py, pc), 4 * px + 2 * py + pc))
    return me, peers


_HBM_SPEC = pl.BlockSpec(memory_space=pltpu.HBM)
_SEM_SPEC = pl.BlockSpec(memory_space=pltpu.SEMAPHORE)
_DATAFLOW = pltpu.SideEffectType.DATAFLOW_SIDE_EFFECTING


def _device_index():
    return 4 * lax.axis_index("x") + 2 * lax.axis_index("y") + lax.axis_index("c")


def _landing(arrays, scatter):
    me = _device_index()
    out = []
    for a, scattered in zip(arrays, scatter):
        own = lax.dynamic_index_in_dim(a, me, 0, keepdims=True) if scattered else a[None]
        buf = lax.empty((N_DEV,) + own.shape[1:], a.dtype)
        out.append(lax.dynamic_update_slice_in_dim(buf, own, me, 0))
    return out


def _split_copies(src_refs, land_refs, send_sems, recv_sems, scatter):
    me, peers = _peers()
    pairs = []
    for t in range(len(src_refs)):
        for k, (peer, peer_id) in enumerate(peers):
            src = src_refs[t].at[peer_id] if scatter[t] else src_refs[t]
            sem = t * (N_DEV - 1) + k
            mk = lambda slot, src=src, t=t, sem=sem, peer=peer: pltpu.make_async_remote_copy(
                src_ref=src, dst_ref=land_refs[t].at[slot], send_sem=send_sems.at[sem], recv_sem=recv_sems.at[sem],
                device_id=peer, device_id_type=pl.DeviceIdType.MESH)
            pairs.append((functools.partial(mk, me), functools.partial(mk, peer_id)))
    return pairs


def _exchange_start(name, arrays, scatter, order_after):
    n = len(arrays)
    landing = _landing(arrays, scatter)

    def body(*refs):
        src_refs, land_refs = refs[:n], refs[n:2 * n]
        send_sems, recv_sems = refs[2 * n + 1], refs[2 * n + 2]
        token_ref = refs[-1]
        for outgoing, _ in _split_copies(src_refs, land_refs, send_sems, recv_sems, scatter):
            outgoing().start()
        token_ref[...] = jnp.zeros_like(token_ref)

    sem_shape = pltpu.SemaphoreType.DMA((n * (N_DEV - 1),))
    thru = [pltpu.HBM(a.shape, a.dtype) for a in list(arrays) + landing]
    hbm = lambda t: pltpu.with_memory_space_constraint(t, pltpu.HBM)
    res = pl.pallas_call(
        body,
        name=name,
        in_specs=[_HBM_SPEC] * (2 * n) + [pl.BlockSpec(memory_space=pl.ANY)],
        out_specs=[_SEM_SPEC, _SEM_SPEC] + [_HBM_SPEC] * (2 * n) + [pl.BlockSpec(memory_space=pltpu.VMEM)],
        out_shape=[sem_shape, sem_shape] + thru + [jax.ShapeDtypeStruct((8, 128), F32)],
        input_output_aliases={i: 2 + i for i in range(2 * n)},
        compiler_params=pltpu.CompilerParams(has_side_effects=_DATAFLOW),
    )(*[hbm(t) for t in list(arrays) + landing], order_after)
    return (res[0], res[1], res[2:2 + n], res[2 + n:2 + 2 * n]), res[-1]


def _exchange_wait(name, handle, scatter, order_after):
    send_sems, recv_sems, sources, landing = handle
    n = len(sources)

    def body(*refs):
        src_refs, land_refs = refs[:n], refs[n:2 * n]
        for outgoing, arrival in _split_copies(src_refs, land_refs, refs[2 * n], refs[2 * n + 1], scatter):
            outgoing().wait_send()
            arrival().wait_recv()

    thru = [pltpu.HBM(a.shape, a.dtype) for a in list(sources) + list(landing)]
    res = pl.pallas_call(
        body,
        name=name,
        in_specs=[_HBM_SPEC] * (2 * n) + [_SEM_SPEC, _SEM_SPEC, pl.BlockSpec(memory_space=pl.ANY)],
        out_specs=[_HBM_SPEC] * (2 * n),
        out_shape=thru,
        input_output_aliases={i: i for i in range(2 * n)},
        compiler_params=pltpu.CompilerParams(has_side_effects=_DATAFLOW),
    )(*sources, *landing, send_sems, recv_sems, order_after)
    return list(res[n:])


def _adam_math(g, w, m, v):
    m = ADAM_B1 * m + (1.0 - ADAM_B1) * g
    v = ADAM_B2 * v + (1.0 - ADAM_B2) * (g * g)
    m_hat = m / (1.0 - ADAM_B1 ** ADAM_STEP)
    v_hat = v / (1.0 - ADAM_B2 ** ADAM_STEP)
    delta = -ADAM_LR * (m_hat / (jnp.sqrt(v_hat) + ADAM_EPS) + ADAM_WD * w)
    return delta, m, v


def _adamw(name, recv, recv_block, recv_map, w, m, v, tr, layer=None, prev=None, after=None):
    def fn(r, wb, mb, vb, *token):
        g = r[0].astype(F32)
        for s in range(1, N_DEV):
            g = g + r[s].astype(F32)
        return (g,) + _adam_math(g, wb, mb, vb) + tuple(jnp.zeros_like(t) for t in token)

    rows, cols = w.shape[-2:]
    if layer is None:
        spec = lambda t: _rows(t, tr)
        out = _row_out(rows, cols, F32, tr)
    else:
        spec = lambda t: (t, (None, tr, cols), lambda i: (layer, i, 0))
        out = (w.shape, F32, (None, tr, cols), lambda i: (layer, i, 0), "row")
    ins = [(recv, recv_block, recv_map), spec(w), spec(m), spec(v)]
    outs = [out] * 4
    if after is not None:
        ins.append(_whole(after))
        outs.append((after.shape, F32, after.shape, lambda i: (0, 0), "row"))
    into = [] if prev is None else [(t, k) for k, t in enumerate(prev)]
    return _rowwise(name, rows, tr, ins, outs, fn, into=into)


def _pack_small(tensors):
    flat = jnp.concatenate([tensors[n].reshape(-1).astype(F32) for n in SMALL])
    padded = -(-flat.shape[0] // SMALL_PAD) * SMALL_PAD
    return jnp.pad(flat, (0, padded - flat.shape[0])).reshape(padded // SMALL_LANES, SMALL_LANES)


def _local_step(x, p, positions, loss_target, w, layer_weights, layer_done=None, layer_partial=None):
    s_len = x.shape[0]
    depth = p.shape[0]
    rope = _rope_tables(positions)
    gains = [{n: w[n][l].reshape(1, -1) for n in SMALL if w[n].ndim == 2 and n != "ssm_log_dt"} for l in range(depth)]
    ssm_args = lambda l: tuple(w[n][l] for n in ("ssm_lam_re", "ssm_lam_im", "ssm_log_dt", "ssm_b_re", "ssm_b_im",
                                                  "ssm_c_re", "ssm_c_im"))
    sps, pulls = [], []
    for l in range(depth):
        (a_cat, w_b, w_c), pull = jax.vjp(_ssm_params, *ssm_args(l))
        sps.append((a_cat, w_b.astype(BF16), w_c.astype(BF16), w["ssm_d"][l].reshape(1, D_SSM)))
        pulls.append(pull)

    (n,) = _rowwise("pre_norm", s_len, TM, [_rows(x, TM), _whole(gains[0]["ffn1_pre_g"])],
                    [_row_out(s_len, D_MODEL, BF16, TM)], lambda a, g: (_rms(a, g),))
    h = x
    saved = []
    for l in range(depth):
        g_next = gains[l + 1]["ffn1_pre_g"] if l + 1 < depth else gains[l]["ffn1_pre_g"]
        h, n, sv = _layer_fwd(s_len, h, n, p[l], rope, layer_weights(l, h), sps[l], gains[l], g_next)
        saved.append(sv)

    def loss_fn(y, t):
        e = y - t
        return e * (1.0 / D_MODEL), jnp.sum(e * e, axis=0, keepdims=True)

    dh, sq = _rowwise("loss", s_len, TM, [_rows(h, TM), _rows(loss_target, TM)],
                      [_row_out(s_len, D_MODEL, F32, TM), _col_out(D_MODEL)], loss_fn)
    loss = 0.5 * jnp.sum(sq) / D_MODEL

    w_grads, small_grads = [None] * depth, [None] * depth
    tie = None
    for l in reversed(range(depth)):
        g_l = gains[l] if tie is None else dict(gains[l], ple_post_g=gains[l]["ple_post_g"] + tie)
        partial = None if layer_partial is None else functools.partial(layer_partial, l)
        dh, wg, gr, (d_a, d_w_b, d_w_c) = _layer_bwd(s_len, dh, saved[l], rope, saved[l]["w"], sps[l], g_l, partial)
        d_lre, d_lim, d_dt, d_bre, d_bim, d_cre, d_cim = pulls[l]((d_a, d_w_b, d_w_c))
        gr.update(ssm_lam_re=d_lre, ssm_lam_im=d_lim, ssm_log_dt=d_dt, ssm_b_re=d_bre, ssm_b_im=d_bim,
                  ssm_c_re=d_cre, ssm_c_im=d_cim)
        gr = {n: g.reshape(w[n].shape[1:]) for n, g in gr.items()}
        w_grads[l], small_grads[l] = wg, gr
        if layer_done is not None:
            tie = layer_done(l, wg, gr, dh)
    return loss, dh, w_grads, small_grads


GROUP_OF = {"ffn1_w_gate": "g1", "ffn1_w_up": "u1", "ffn2_w_gate": "g2", "ffn2_w_up": "u2", "ffn1_w_down": "d1",
            "ffn2_w_down": "d2", "w_out": "out", "ple_w_gate": "pg", "w_in": "in", "ple_w_up": "pu", "ssm_w_glu": "glu"}
GROUPS = tuple(GROUP_OF[n] for n in SHARDED)
FIRST_PARTS = (("g1", "u1"), ("d1",), ("in", "glu", "out"), ("g2", "u2", "d2", "pu", "pg"))
_COLUMN_VIEWS = {"w_gu1": ("g1", "u1"), "w_gu2": ("g2", "u2"), "w_in": ("in",), "w_pu": ("pu",)}
_ROW_VIEWS = {"d1": "w_d1", "d2": "w_d2", "out": "w_out", "pg": "ple_w_gate", "glu": "w_glu"}


def _shard_groups(w, l):
    return {group: w[n][l].astype(BF16) for n, group in GROUP_OF.items()}


def _gathered_views(gathered, after=None):
    views = {}
    for name, groups in _COLUMN_VIEWS.items():
        if all(g in gathered for g in groups):
            views[name] = _natural_cols("relayout_" + name, [gathered[g] for g in groups], after)
            after = None
    for group, name in _ROW_VIEWS.items():
        if group in gathered:
            g = gathered[group]
            views[name] = g.reshape(N_DEV * g.shape[1], g.shape[2])
    return views


def _grad_groups(wg):
    rows = lambda t: t if t.ndim == 3 else t.reshape(N_DEV, t.shape[0] // N_DEV, t.shape[1])
    return {GROUP_OF[n]: rows(t) for n, t in wg.items()}


def _update_sharded(name, recv_groups, w, m, v, layer, prev, after=None):
    recv = recv_groups[GROUP_OF[name]]
    rows, cols = w.shape[1:]
    tr = rows if rows <= 256 else (rows // 2 if rows % 256 else 256)
    return _adamw(f"adamw_{name}", recv, (N_DEV, tr, cols), lambda i: (0, i, 0), w, m, v, tr, layer, prev, after)


def kernel(x, p, positions, ffn1_pre_g, ffn1_w_gate, ffn1_w_up, ffn1_w_down, ffn1_post_g, mix_pre_g, w_in, attn_norm_g, ssm_lam_re, ssm_lam_im, ssm_log_dt, ssm_b_re, ssm_b_im, ssm_c_re, ssm_c_im, ssm_d, ssm_w_glu, ssm_b_glu, ssm_norm_g, w_out, mix_post_g, ffn2_pre_g, ffn2_w_gate, ffn2_w_up, ffn2_w_down, ffn2_post_g, ple_w_up, ple_w_gate, ple_post_g, loss_target, m_ffn1_pre_g, m_ffn1_w_gate, m_ffn1_w_up, m_ffn1_w_down, m_ffn1_post_g, m_mix_pre_g, m_w_in, m_attn_norm_g, m_ssm_lam_re, m_ssm_lam_im, m_ssm_log_dt, m_ssm_b_re, m_ssm_b_im, m_ssm_c_re, m_ssm_c_im, m_ssm_d, m_ssm_w_glu, m_ssm_b_glu, m_ssm_norm_g, m_w_out, m_mix_post_g, m_ffn2_pre_g, m_ffn2_w_gate, m_ffn2_w_up, m_ffn2_w_down, m_ffn2_post_g, m_ple_w_up, m_ple_w_gate, m_ple_post_g, v_ffn1_pre_g, v_ffn1_w_gate, v_ffn1_w_up, v_ffn1_w_down, v_ffn1_post_g, v_mix_pre_g, v_w_in, v_attn_norm_g, v_ssm_lam_re, v_ssm_lam_im, v_ssm_log_dt, v_ssm_b_re, v_ssm_b_im, v_ssm_c_re, v_ssm_c_im, v_ssm_d, v_ssm_w_glu, v_ssm_b_glu, v_ssm_norm_g, v_w_out, v_mix_post_g, v_ffn2_pre_g, v_ffn2_w_gate, v_ffn2_w_up, v_ffn2_w_down, v_ffn2_post_g, v_ple_w_up, v_ple_w_gate, v_ple_post_g):
    args = dict(locals())
    w = {n: args[n] for n in WEIGHTS}
    mom = {n: args["m_" + n] for n in WEIGHTS}
    var = {n: args["v_" + n] for n in WEIGHTS}
    depth = p.shape[0]

    gathers, scatters, recv, tokens = {}, {}, {}, {}

    def start(name, groups, names, scatter, order_after):
        flags = scatter if isinstance(scatter, tuple) else (scatter,) * len(names)
        handle, token = _exchange_start(name, [groups[n] for n in names], flags, order_after)
        return (handle, names, flags), token

    def finish(name, pending, order_after):
        handle, names, flags = pending
        return dict(zip(names, _exchange_wait(name, handle, flags, order_after)))

    def gather_next(l, order_after):
        if l + 1 >= depth:
            return None
        gathers[l + 1], token = start("all_gather_start", _shard_groups(w, l + 1), GROUPS, False, order_after)
        return token

    def layer_weights(l, h):
        views = {}
        if l > 0:
            got = finish("all_gather_wait", gathers.pop(l), h)
            views.update(_gathered_views(got, gather_next(l, got[GROUPS[0]])))
            return lambda name, after: views[name]

        shards = _shard_groups(w, 0)
        pending, token = start("all_gather_start_0", shards, FIRST_PARTS[0], False, h)
        got = finish("all_gather_wait_0", pending, token)
        waiting = {}
        token = got[FIRST_PARTS[0][0]]
        for i, part in enumerate(FIRST_PARTS[1:], 1):
            waiting[i], token = start(f"all_gather_start_{i}", shards, part, False, token)
        views.update(_gathered_views(got, gather_next(0, token)))

        def weight(name, after):
            for i in sorted(waiting):
                if name not in views:
                    views.update(_gathered_views(finish(f"all_gather_wait_{i}", waiting.pop(i), after)))
            return views[name]

        return weight

    small_l = lambda t, l: _pack_small({n: t[n][l] for n in SMALL})
    scatter_flags = lambda names: tuple(n != "small" for n in names)
    late = ("g1", "u1", "d1", "small")
    early = tuple(g for g in GROUPS if g not in late)

    def layer_partial(l, wg):
        if l > 0:
            return None
        scatters["early"], token = start("reduce_scatter_start_a", _grad_groups(wg), early, scatter_flags(early),
                                         wg["w_in"])
        return token[0, 0]

    def layer_done(l, wg, gr, dh):
        if l + 1 < depth:
            recv[l + 1] = finish("reduce_scatter_wait", scatters.pop(l + 1), dh)
        groups = dict(_grad_groups(wg), small=_pack_small(gr))
        names = late if l == 0 else GROUPS + ("small",)
        scatters[l], tokens[l] = start("reduce_scatter_start_b" if l == 0 else "reduce_scatter_start", groups, names,
                                       scatter_flags(names), dh)
        return tokens[l][0, 0]

    loss, dx, w_grads, small_grads = _local_step(x[0], p[:, 0], positions[0], loss_target[0], w, layer_weights,
                                                 layer_done, layer_partial)
    loss = lax.psum(loss, MESH_AXES)

    results = {n: None for n in SHARDED}
    last = tokens[0]
    for l in reversed(range(1, depth)):
        for n in SHARDED:
            *results[n], last = _update_sharded(n, recv[l], w[n], mom[n], var[n], l, results[n], last)
    recv[0] = dict(finish("reduce_scatter_wait_a", scatters.pop("early"), last))
    recv[0].update(finish("reduce_scatter_wait_b", scatters.pop(0), last))
    for n in SHARDED:
        results[n] = _update_sharded(n, recv[0], w[n], mom[n], var[n], 0, results[n])
    result = dict(results)

    recv_small = jnp.concatenate([recv[l]["small"] for l in range(depth)], axis=1)
    packed = [jnp.concatenate([small_l(t, l) for l in range(depth)], axis=0) for t in (w, mom, var)]
    small_out = _adamw("adamw_small", recv_small, (N_DEV, 8, SMALL_LANES), lambda i: (0, i, 0), *packed, 8)
    for q in range(4):
        flat = small_out[q].reshape(depth, -1)
        off = 0
        for n in SMALL:
            size = math.prod(w[n].shape[1:])
            result.setdefault(n, [None] * 4)[q] = flat[:, off:off + size].reshape(w[n].shape)
            off += size

    outputs = [loss, dx[None]]
    for q in range(4):
        outputs += [result[n][q] for n in WEIGHTS]
    return tuple(outputs)
```

```python
import functools
import math

import jax
import jax.numpy as jnp
from jax import lax
from jax.experimental import pallas as pl
from jax.experimental.pallas import tpu as pltpu

F32 = jnp.float32
BF16 = jnp.bfloat16

N_DEV = 8
D_MODEL = 1024
D_FF = 2816
FF_SHARD = D_FF // N_DEV
D_ATTN = 512
D_SSM = 512
HEAD_DIM = 64
N_HEADS = 8
ROPE_DIM = 16
ROPE_THETA = 500000.0
DILATIONS = (1, 4, 16)
BAND = 128
N_GROUPS = 32
SSM_GROUP = 16
SSM_STATE = 64
N_STATE = N_GROUPS * SSM_STATE
PLE_DIM = 256
NORM_EPS = 1e-6
ADAM_LR, ADAM_B1, ADAM_B2, ADAM_EPS, ADAM_WD, ADAM_STEP = 0.001, 0.9, 0.999, 1e-08, 0.01, 10

VMEM_LIMIT_BYTES = 52 * 1024 * 1024
TM = 512
MESH_AXES = ("x", "y", "c")

WEIGHTS = ['ffn1_pre_g', 'ffn1_w_gate', 'ffn1_w_up', 'ffn1_w_down', 'ffn1_post_g', 'mix_pre_g', 'w_in', 'attn_norm_g',
           'ssm_lam_re', 'ssm_lam_im', 'ssm_log_dt', 'ssm_b_re', 'ssm_b_im', 'ssm_c_re', 'ssm_c_im', 'ssm_d',
           'ssm_w_glu', 'ssm_b_glu', 'ssm_norm_g', 'w_out', 'mix_post_g', 'ffn2_pre_g', 'ffn2_w_gate', 'ffn2_w_up',
           'ffn2_w_down', 'ffn2_post_g', 'ple_w_up', 'ple_w_gate', 'ple_post_g']
SHARDED = ['ffn1_w_gate', 'ffn1_w_up', 'ffn1_w_down', 'w_in', 'ssm_w_glu', 'w_out', 'ffn2_w_gate', 'ffn2_w_up',
           'ffn2_w_down', 'ple_w_up', 'ple_w_gate']
SMALL = [n for n in WEIGHTS if n not in SHARDED]
SMALL_LANES = 1024
SMALL_PAD = 8 * SMALL_LANES


def _params(n_axes):
    return pltpu.CompilerParams(dimension_semantics=("arbitrary",) * n_axes, vmem_limit_bytes=VMEM_LIMIT_BYTES)


_DIMS = {"nn": (((1,), (0,)), ((), ())), "nt": (((1,), (1,)), ((), ())), "tn": (((0,), (0,)), ((), ()))}


def _dot(a, b, mode):
    return lax.dot_general(a.astype(BF16), b.astype(BF16), _DIMS[mode], preferred_element_type=F32)


def _store(out_refs, vals, kinds, first):
    for ref, val, kind in zip(out_refs, vals, kinds):
        if isinstance(val, (list, tuple)):
            for q, piece in enumerate(val):
                ref[q] = piece.astype(ref.dtype)
        elif kind == "row":
            ref[...] = val.astype(ref.dtype)
        else:
            @pl.when(first)
            def _(ref=ref, val=val):
                ref[...] = val.astype(ref.dtype)

            @pl.when(jnp.logical_not(first))
            def _(ref=ref, val=val):
                ref[...] += val.astype(ref.dtype)


def _call(name, body, grid, ins, outs, scratch, into):
    arrays = [t[0] for t in ins]
    in_specs = [_in_spec(t) for t in ins]
    aliases = {}
    for arr, k in into:
        arrays.append(arr)
        in_specs.append(pl.BlockSpec(memory_space=pl.ANY))
        aliases[len(arrays) - 1] = k
    return pl.pallas_call(
        body,
        name=name,
        grid=grid,
        in_specs=in_specs,
        out_specs=[pl.BlockSpec(o[2], o[3]) for o in outs],
        out_shape=[jax.ShapeDtypeStruct(o[0], o[1]) for o in outs],
        scratch_shapes=list(scratch),
        input_output_aliases=aliases,
        compiler_params=_params(len(grid)),
    )(*arrays)


def _mm(name, grid, a, b, mode, outs, extras=(), epilogue=None, acc_shape=None, scratch=(), into=(), more=(),
        a_prep=None, b_prep=None):
    prep_a = a_prep if a_prep is not None else (lambda t: t)
    prep_b = b_prep if b_prep is not None else (lambda t: t)
    nk = grid[2]
    ne, no = len(extras), len(outs)
    kinds = [o[4] for o in outs]
    assert all(k == "row" for k in kinds) or grid[1] == 1
    n_ab = 2 + 2 * len(more)
    n_in = n_ab + ne + len(into)
    n_acc = int(nk > 1)

    def body(*refs):
        ex_refs = refs[n_ab:n_ab + ne]
        out_refs = refs[n_in:n_in + no]
        scr_refs = refs[n_in + no + n_acc:]
        part = _dot(prep_a(refs[0][...]), prep_b(refs[1][...]), mode)
        for p in range(2, n_ab, 2):
            part = part + _dot(prep_a(refs[p][...]), prep_b(refs[p + 1][...]), mode)
        first = pl.program_id(0) == 0

        def finish(acc):
            vals = epilogue(acc, *[r[...] for r in ex_refs], *scr_refs) if epilogue is not None else (acc,)
            _store(out_refs, vals, kinds, first)

        if nk == 1:
            finish(part)
        else:
            acc_ref = refs[n_in + no]
            k = pl.program_id(2)

            @pl.when(k == 0)
            def _():
                acc_ref[...] = part

            @pl.when(k > 0)
            def _():
                acc_ref[...] += part

            @pl.when(k == nk - 1)
            def _():
                finish(acc_ref[...])

    acc = [pltpu.VMEM(acc_shape, F32)] if nk > 1 else []
    pairs = tuple(t for pair in more for t in pair)
    return _call(name, body, grid, (a, b) + pairs + tuple(extras), outs, acc + list(scratch), into)


def _rowwise(name, n_rows, tm, ins, outs, fn, scratch=(), into=()):
    kinds = [o[4] for o in outs]
    ni, no = len(ins), len(outs)
    n_in = ni + len(into)

    def body(*refs):
        vals = fn(*[r[...] for r in refs[:ni]], *refs[n_in + no:])
        _store(refs[n_in:n_in + no], vals, kinds, pl.program_id(0) == 0)

    return _call(name, body, (n_rows // tm,), ins, outs, scratch, into)


def _rows(arr, tm, col=0, width=None):
    width = arr.shape[1] if width is None else width
    return (arr, (tm, width), lambda i, *_: (i, col))


def _whole(arr):
    nd = arr.ndim
    return (arr, arr.shape, lambda *_: (0,) * nd)


def _resident(arr):
    nd = arr.ndim
    return (arr, arr.shape, lambda *_: (0,) * nd, dict(pipeline_mode=pl.Buffered(1)))


def _in_spec(t):
    return pl.BlockSpec(t[1], t[2], **(t[3] if len(t) > 3 else {}))


def _row_out(n_rows, width, dtype, tm, col=0, total=None):
    return ((n_rows, width if total is None else total), dtype, (tm, width), lambda i, *_: (i, col), "row")


def _col_out(width):
    return ((1, width), F32, (1, width), lambda *_: (0, 0), "colsum")


def _lane_blocks(acc):
    return [acc[:, q * LANES:(q + 1) * LANES] for q in range(acc.shape[1] // LANES)]


def _join_lanes(blk):
    return jnp.concatenate([blk[q] for q in range(blk.shape[0])], axis=1)


def _column_shards(acc, width):
    return [acc[:, q * width:(q + 1) * width] for q in range(acc.shape[1] // width)]


def _natural_cols(name, parts, after=None):
    rows, width = parts[0].shape[1:]
    tr = min(rows, 256)
    n = len(parts)

    def body(*refs):
        out_ref = refs[-1]
        for t in range(n):
            for j in range(N_DEV):
                lo = (t * N_DEV + j) * width
                out_ref[:, lo:lo + width] = refs[t][j]

    ins = [(t, (N_DEV, tr, width), lambda i: (0, i, 0)) for t in parts]
    if after is not None:
        ins.append(_whole(after))
    out = _row_out(rows, n * N_DEV * width, parts[0].dtype, tr)
    return _call(name, body, (rows // tr,), ins, [out], (), ())[0]


def _rms(x, g):
    return x * lax.rsqrt(jnp.mean(x * x, axis=-1, keepdims=True) + NORM_EPS) * g


def _sigmoid(x):
    return 1.0 / (1.0 + jnp.exp(-x))


def _gelu(x):
    return 0.5 * x * (1.0 + jnp.tanh(0.7978845608028654 * (x + 0.044715 * x * x * x)))


FFN_TM = 256
FF_HALF = D_FF // 2


def _ffn_fwd(s_len, h, n, weight, g_post, g_next):
    def gu_epi(acc):
        gate, up = acc[:, :D_FF], acc[:, D_FF:]
        return acc, gate * _sigmoid(gate) * up

    w_gu = weight("gu", n)
    gu, act = _mm("ffn_gate_up", (s_len // FFN_TM, 1, 1), _rows(n, FFN_TM), _resident(w_gu), "nn",
                  [_row_out(s_len, 2 * D_FF, BF16, FFN_TM), _row_out(s_len, D_FF, BF16, FFN_TM)], epilogue=gu_epi)
    w_d = weight("d", act)

    def down_epi(acc, h_blk, gp, gn):
        h_new = h_blk + 0.5 * _rms(acc, gp)
        return acc, h_new, _rms(h_new, gn)

    f, h_new, n_next = _mm(
        "ffn_down", (s_len // TM, 1, 1), _rows(act, TM), _resident(w_d), "nn",
        [_row_out(s_len, D_MODEL, F32, TM), _row_out(s_len, D_MODEL, F32, TM), _row_out(s_len, D_MODEL, BF16, TM)],
        extras=[_rows(h, TM), _whole(g_post), _whole(g_next)], epilogue=down_epi)
    return h_new, n_next, dict(n=n, gu=gu, act=act, f=f, h=h)


def _ffn_bwd(s_len, saved, df, w_gu, w_d, final_epi, final_extras, final_outs):
    nt = s_len // TM

    def act_epi(acc, gu_blk):
        gate = gu_blk[:, :D_FF].astype(F32)
        up = gu_blk[:, D_FF:].astype(F32)
        sg = _sigmoid(gate)
        dgate = acc * up * sg * (1.0 + gate * (1.0 - sg))
        dup = acc * gate * sg
        return (jnp.concatenate([dgate, dup], axis=1),)

    (dgu,) = _mm("ffn_bwd_act", (s_len // FFN_TM, 1, 1), _rows(df, FFN_TM), _resident(w_d), "nt",
                 [_row_out(s_len, 2 * D_FF, BF16, FFN_TM)], extras=[_rows(saved["gu"], FFN_TM)], epilogue=act_epi)

    (d_w_d,) = _mm(
        "ffn_bwd_wdown", (2, 1, nt), (saved["act"], (TM, FF_HALF), lambda i, j, k: (k, i)),
        (df, (TM, D_MODEL), lambda i, j, k: (k, 0)), "tn",
        [((D_FF, D_MODEL), BF16, (FF_HALF, D_MODEL), lambda i, j, k: (i, 0), "row")], acc_shape=(FF_HALF, D_MODEL))

    def w_grad(name, half):
        (dw,) = _mm(
            name, (1, 2, nt), (saved["n"], (TM, D_MODEL), lambda i, j, k: (k, 0)),
            (dgu, (TM, FF_HALF), lambda i, j, k: (k, 2 * half + j)), "tn",
            [((N_DEV, D_MODEL, FF_SHARD), BF16, (N_DEV // 2, D_MODEL, FF_SHARD), lambda i, j, k: (j, 0, 0), "row")],
            epilogue=lambda acc: (_column_shards(acc, FF_SHARD),), acc_shape=(D_MODEL, FF_HALF))
        return dw

    d_w_gate, d_w_up = w_grad("ffn_bwd_wgate", 0), w_grad("ffn_bwd_wup", 1)

    outs = _mm("ffn_bwd_dn", (s_len // FFN_TM, 1, 1), _rows(dgu, FFN_TM), _resident(w_gu), "nt",
               final_outs, extras=final_extras, epilogue=final_epi)
    return d_w_gate, d_w_up, d_w_d, outs


def _band_mask(first_block):
    qi = lax.broadcasted_iota(jnp.int32, (BAND, 2 * BAND), 0)
    kj = lax.broadcasted_iota(jnp.int32, (BAND, 2 * BAND), 1)
    ok = (kj >= qi) & (kj <= qi + BAND)
    return ok & (jnp.logical_not(first_block) | (kj >= BAND))


def _attn_views(s_len, d):
    n_str = s_len // d
    nb = n_str // BAND
    blk = (BAND, D_ATTN)
    cur = lambda r, b: (b, r)
    prev = lambda r, b: (jnp.maximum(b - 1, 0), r)
    return n_str, nb, blk, cur, prev


STREAM_LANES = 128


def _stream_scratch(tm):
    return pltpu.VMEM((D_ATTN // STREAM_LANES, tm, STREAM_LANES), F32)


def _to_streams(x, d, scr):
    if d == 1:
        return x
    tm = x.shape[0]
    chunks = D_ATTN // STREAM_LANES
    for c in range(chunks):
        scr[c, 0:tm, :] = x[:, c * STREAM_LANES:(c + 1) * STREAM_LANES]
    return jnp.concatenate([scr.at[c][pl.ds(r, tm // d, stride=d), :] for r in range(d) for c in range(chunks)], axis=1)


def _from_streams(v, d, scr):
    if d == 1:
        return v
    rows = v.shape[0]
    chunks = D_ATTN // STREAM_LANES
    for r in range(d):
        for c in range(chunks):
            lo = r * D_ATTN + c * STREAM_LANES
            scr.at[c][pl.ds(r, rows, stride=d), :] = v[:, lo:lo + STREAM_LANES]
    return jnp.concatenate([scr[c, 0:rows * d, :] for c in range(chunks)], axis=1)


def _stream_in(arr, tm, d, ahead=0, n_blocks=None):
    rows = tm // d
    step = ahead // rows
    if ahead:
        return (arr, (rows, d * D_ATTN), lambda i, *_: (jnp.minimum(i + step, n_blocks - 1), 0))
    return (arr, (rows, d * D_ATTN), lambda i, *_: (i, 0))


def _stream_out(s_len, dtype, tm, d):
    return ((s_len // d, d * D_ATTN), dtype, (tm // d, d * D_ATTN), lambda i, *_: (i, 0), "row")


def _attn_fwd(s_len, q, k, v, d):
    n_str, nb, blk, cur, prev = _attn_views(s_len, d)
    view = lambda t: t

    def body(q_ref, kp_ref, kc_ref, vp_ref, vc_ref, o_ref, lse_ref):
        mask = _band_mask(pl.program_id(1) == 0)
        even = lax.broadcasted_iota(jnp.int32, (BAND, 2 * HEAD_DIM), 1) < HEAD_DIM
        qq = q_ref[...]
        kk = jnp.concatenate([kp_ref[...], kc_ref[...]], axis=0)
        vv = jnp.concatenate([vp_ref[...], vc_ref[...]], axis=0)
        for pair in range(N_HEADS // 2):
            sl = slice(2 * pair * HEAD_DIM, 2 * (pair + 1) * HEAD_DIM)
            q_p, k_p, v_p = qq[:, sl], kk[:, sl], vv[:, sl]
            o, lse = [], []
            for half in range(2):
                sel = even if half == 0 else jnp.logical_not(even)
                s = _dot(jnp.where(sel, q_p, jnp.zeros_like(q_p)), k_p, "nt") * (HEAD_DIM ** -0.5)
                s = jnp.where(mask, s, -1e30)
                m = jnp.max(s, axis=-1, keepdims=True)
                e = jnp.exp(s - m)
                den = jnp.sum(e, axis=-1, keepdims=True)
                o.append(_dot(e / den, v_p, "nn"))
                lse.append(m + jnp.log(den))
            o_ref[:, sl] = jnp.where(even, o[0], o[1])
            lse_ref[:, sl] = jnp.where(even, lse[0], lse[1])

    o, lse = pl.pallas_call(
        body,
        name=f"attn_fwd_d{d}",
        grid=(d, nb),
        in_specs=[pl.BlockSpec(blk, cur), pl.BlockSpec(blk, prev), pl.BlockSpec(blk, cur),
                  pl.BlockSpec(blk, prev), pl.BlockSpec(blk, cur)],
        out_specs=[pl.BlockSpec(blk, cur), pl.BlockSpec(blk, cur)],
        out_shape=[jax.ShapeDtypeStruct((n_str, d * D_ATTN), F32)] * 2,
        compiler_params=_params(2),
    )(view(q), view(k), view(k), view(v), view(v))
    return o, lse


def _attn_bwd(s_len, q, k, v, dattn, attn, lse, d):
    n_str, nb, blk, cur, prev = _attn_views(s_len, d)
    view = lambda t: t

    def body(q_ref, kp_ref, kc_ref, vp_ref, vc_ref, da_ref, at_ref, lse_ref, dq_ref, dka_ref, dkb_ref, dva_ref, dvb_ref):
        mask = _band_mask(pl.program_id(1) == 0)
        even = lax.broadcasted_iota(jnp.int32, (BAND, 2 * HEAD_DIM), 1) < HEAD_DIM
        qq = q_ref[...]
        kk = jnp.concatenate([kp_ref[...], kc_ref[...]], axis=0)
        vv = jnp.concatenate([vp_ref[...], vc_ref[...]], axis=0)
        da = da_ref[...]
        prod = da * at_ref[...]
        da = da.astype(BF16)
        scale = HEAD_DIM ** -0.5
        for pair in range(N_HEADS // 2):
            sl = slice(2 * pair * HEAD_DIM, 2 * (pair + 1) * HEAD_DIM)
            q_p, k_p, v_p, da_p, prod_p = qq[:, sl], kk[:, sl], vv[:, sl], da[:, sl], prod[:, sl]
            dq, dk, dv = [], 0.0, 0.0
            for half in range(2):
                sel = even if half == 0 else jnp.logical_not(even)
                q_h = jnp.where(sel, q_p, jnp.zeros_like(q_p))
                da_h = jnp.where(sel, da_p, jnp.zeros_like(da_p))
                lse_col = lse_ref[:, (2 * pair + half) * HEAD_DIM:(2 * pair + half) * HEAD_DIM + 1]
                d_col = jnp.sum(jnp.where(sel, prod_p, 0.0), axis=-1, keepdims=True)
                s = _dot(q_h, k_p, "nt") * scale
                p = jnp.where(mask, jnp.exp(s - lse_col), 0.0)
                ds = p * (_dot(da_h, v_p, "nt") - d_col)
                dq.append(_dot(ds, k_p, "nn") * scale)
                dk = dk + _dot(ds, q_h, "tn") * scale
                dv = dv + _dot(p, da_h, "tn")
            dq_ref[:, sl] = jnp.where(even, dq[0], dq[1])
            dkb_ref[:, sl] = dk[:BAND]
            dka_ref[:, sl] = dk[BAND:]
            dvb_ref[:, sl] = dv[:BAND]
            dva_ref[:, sl] = dv[BAND:]

    outs = pl.pallas_call(
        body,
        name=f"attn_bwd_d{d}",
        grid=(d, nb),
        in_specs=[pl.BlockSpec(blk, cur), pl.BlockSpec(blk, prev), pl.BlockSpec(blk, cur),
                  pl.BlockSpec(blk, prev), pl.BlockSpec(blk, cur),
                  pl.BlockSpec(blk, cur), pl.BlockSpec(blk, cur), pl.BlockSpec(blk, cur)],
        out_specs=[pl.BlockSpec(blk, cur)] * 5,
        out_shape=[jax.ShapeDtypeStruct((n_str, d * D_ATTN), F32)] * 5,
        compiler_params=_params(2),
    )(view(q), view(k), view(k), view(v), view(v), view(dattn), view(attn), view(lse))
    return list(outs)


def _rope_tables(positions):
    half = ROPE_DIM // 2
    inv_freq = ROPE_THETA ** (-jnp.arange(half, dtype=F32) * (2.0 / ROPE_DIM))
    ang = positions.astype(F32)[:, None] * inv_freq
    cos, sin = jnp.cos(ang), jnp.sin(ang)
    s_len = positions.shape[0]
    one = jnp.ones((s_len, HEAD_DIM - ROPE_DIM), F32)
    zero8 = jnp.zeros((s_len, half), F32)
    zero = jnp.zeros((s_len, HEAD_DIM - ROPE_DIM), F32)
    c = jnp.concatenate([cos, cos, one], axis=1)
    s1 = jnp.concatenate([zero8, sin, zero], axis=1)
    s2 = jnp.concatenate([-sin, zero8, zero], axis=1)
    tile = lambda t: jnp.tile(t, (1, N_HEADS))
    return tile(c), tile(s1), tile(s2)


def _rope(t, c, s1, s2):
    half = ROPE_DIM // 2
    return t * c + pltpu.roll(t, half, 1) * s1 + pltpu.roll(t, D_ATTN - half, 1) * s2


def _rope_transposed(dt, c, s1, s2):
    half = ROPE_DIM // 2
    return dt * c + pltpu.roll(dt * s1, D_ATTN - half, 1) + pltpu.roll(dt * s2, half, 1)


SCAN_ROWS = 256
SCAN_CHUNK = 128
LANES = 128
SUBLANES = 8


def _cmul(xr, xi, yr, yi):
    return xr * yr - xi * yi, xr * yi + xi * yr


def _scan(name, bu, a_cat, reverse):
    s_len = bu.shape[1]
    n_blocks = bu.shape[0] // 2
    n_chunks = s_len // SCAN_CHUNK
    log_chunk = SCAN_CHUNK.bit_length() - 1
    rows_of = lambda off: pl.ds(pl.multiple_of(off * n_chunks, n_chunks), n_chunks)

    def body(br_ref, bi_ref, ar_ref, ai_ref, xr_ref, xi_ref, pr_ref, pi_ref, cr_ref, ci_ref):
        ar, ai = ar_ref[...], ai_ref[...]

        def step(i, state):
            rows = rows_of((SCAN_CHUNK - 1 - i) if reverse else i)
            pr, pi = _cmul(ar, ai, *state)
            nr, ni = pr + br_ref[rows, :], pi + bi_ref[rows, :]
            xr_ref[rows, :] = nr
            xi_ref[rows, :] = ni
            return nr, ni

        zero = jnp.zeros((n_chunks, LANES), F32)
        lax.fori_loop(0, SCAN_CHUNK, step, (zero, zero), unroll=4)

        pw = [(ar, ai)]
        for _ in range(log_chunk + n_chunks.bit_length()):
            pw.append(_cmul(*pw[-1], *pw[-1]))

        last = pl.ds((0 if reverse else SCAN_CHUNK - 1) * n_chunks, n_chunks)
        er, ei = xr_ref[last, :], xi_ref[last, :]
        chunk = lax.broadcasted_iota(jnp.int32, (n_chunks, LANES), 0)

        def shifted(v, s):
            if reverse:
                return jnp.where(chunk < n_chunks - s, pltpu.roll(v, n_chunks - s, 0), 0.0)
            return jnp.where(chunk >= s, pltpu.roll(v, s, 0), 0.0)

        s, b = 1, log_chunk
        while s < n_chunks:
            mr, mi = _cmul(*pw[b], shifted(er, s), shifted(ei, s))
            er, ei = er + mr, ei + mi
            s, b = 2 * s, b + 1
        cr_ref[...] = shifted(er, 1)
        ci_ref[...] = shifted(ei, 1)

        step_no = lax.broadcasted_iota(jnp.int32, (SUBLANES, LANES), 0)
        expo = (SUBLANES - step_no) if reverse else (step_no + 1)
        qr, qi = jnp.ones((SUBLANES, LANES), F32), jnp.zeros((SUBLANES, LANES), F32)
        for bit in range(4):
            mr, mi = _cmul(qr, qi, *pw[bit])
            hit = (expo & (1 << bit)) != 0
            qr, qi = jnp.where(hit, mr, qr), jnp.where(hit, mi, qi)
        lo = SCAN_CHUNK - SUBLANES if reverse else 0
        pr_ref[lo:lo + SUBLANES, :] = qr
        pi_ref[lo:lo + SUBLANES, :] = qi
        m, b = SUBLANES, 3
        while m < SCAN_CHUNK:
            src = pl.ds(SCAN_CHUNK - m, m) if reverse else pl.ds(0, m)
            dst = pl.ds(SCAN_CHUNK - 2 * m, m) if reverse else pl.ds(m, m)
            mr, mi = _cmul(pr_ref[src, :], pi_ref[src, :], *pw[b])
            pr_ref[dst, :] = mr
            pi_ref[dst, :] = mi
            m, b = 2 * m, b + 1

        cr, ci = cr_ref[...], ci_ref[...]

        def fix(i, _):
            rows = rows_of(i)
            mr, mi = _cmul(pr_ref[pl.ds(i, 1), :], pi_ref[pl.ds(i, 1), :], cr, ci)
            xr_ref[rows, :] += mr
            xi_ref[rows, :] += mi
            return 0

        lax.fori_loop(0, SCAN_CHUNK, fix, 0, unroll=4)

    blk = lambda off: pl.BlockSpec((None, s_len, LANES), lambda c: (off + c, 0, 0))
    col = lambda off: pl.BlockSpec((1, LANES), lambda c: (0, off + c))
    return pl.pallas_call(
        body,
        name=name,
        grid=(n_blocks,),
        in_specs=[blk(0), blk(n_blocks), col(0), col(n_blocks)],
        out_specs=[blk(0), blk(0)],
        out_shape=[jax.ShapeDtypeStruct((n_blocks, s_len, LANES), F32)] * 2,
        scratch_shapes=[pltpu.VMEM((SCAN_CHUNK, LANES), F32)] * 2 + [pltpu.VMEM((n_chunks, LANES), F32)] * 2,
        compiler_params=_params(1),
    )(bu, bu, a_cat, a_cat)


def _to_chunk_order(t):
    s_len, width = t.shape
    return t.reshape(s_len // SCAN_CHUNK, SCAN_CHUNK, width).transpose(1, 0, 2).reshape(s_len, width)


def _from_chunk_order(t):
    s_len, width = t.shape
    return t.reshape(SCAN_CHUNK, s_len // SCAN_CHUNK, width).transpose(1, 0, 2).reshape(s_len, width)


def _state_in(arr, tm):
    return (arr, (arr.shape[0], tm, LANES), lambda i, *_: (0, i, 0))


def _state_out(s_len, width, tm):
    n = width // LANES
    return ((n, s_len, LANES), F32, (n, tm, LANES), lambda i, *_: (0, i, 0), "row")


def _ssm_params(lam_re, lam_im, log_dt, b_re, b_im, c_re, c_im):
    dt = jnp.exp(log_dt)[:, None]
    er = jnp.exp(lam_re * dt)
    a_re = er * jnp.cos(lam_im * dt)
    a_im = er * jnp.sin(lam_im * dt)
    nr, ni = a_re - 1.0, a_im
    den = lam_re * lam_re + lam_im * lam_im
    fr = (nr * lam_re + ni * lam_im) / den
    fi = (ni * lam_re - nr * lam_im) / den
    bb_re = fr[..., None] * b_re - fi[..., None] * b_im
    bb_im = fr[..., None] * b_im + fi[..., None] * b_re
    eye = jnp.eye(N_GROUPS, dtype=F32)

    def in_mat(bb):
        t = bb.transpose(0, 2, 1)[:, :, None, :] * eye[:, None, :, None]
        return t.reshape(D_SSM, N_STATE)

    def out_mat(cc):
        t = cc.transpose(0, 2, 1)[:, :, None, :] * eye[:, None, :, None]
        return t.reshape(N_STATE, D_SSM)

    w_b = jnp.concatenate([in_mat(bb_re), in_mat(bb_im)], axis=1)
    w_c = jnp.concatenate([out_mat(c_re), -out_mat(c_im)], axis=0)
    a_cat = jnp.concatenate([a_re.reshape(1, N_STATE), a_im.reshape(1, N_STATE)], axis=1)
    return a_cat, w_b, w_c


def _conj(a_cat):
    return jnp.concatenate([a_cat[:, :N_STATE], -a_cat[:, N_STATE:]], axis=1)


def _layer_fwd(s_len, h, n1, p_l, rope, weight, sp, gains, g_next):
    nt = s_len // TM
    sv = {}
    gw = {}

    def need(after, *names):
        for name in names:
            gw[name] = weight(name, after)

    def ffn_weight(which):
        def get(kind, after):
            need(after, f"w_{kind}{which}")
            return gw[f"w_{kind}{which}"]
        return get

    h1, a_in, sv["ffn1"] = _ffn_fwd(s_len, h, n1, ffn_weight(1), gains["ffn1_post_g"], gains["mix_pre_g"])

    need(h1, "w_in", "w_glu", "w_out")
    (proj,) = _mm("w_in", (nt, 1, 1), _rows(a_in, TM), _resident(gw["w_in"]), "nn",
                  [_row_out(s_len, 2 * D_MODEL, F32, TM)])

    c, s1, s2 = rope

    def rope_fn(tq, tk, tv, cc, a1, a2, scr):
        q, k = _rope(tq, cc, a1, a2), _rope(tk, cc, a1, a2)
        return [_to_streams(t, d, scr) for d in DILATIONS for t in (q, k, tv)]

    qkv = _rowwise(
        "rope", s_len, TM,
        [_rows(proj, TM, 0, D_ATTN), _rows(proj, TM, 1, D_ATTN), _rows(proj, TM, 2, D_ATTN),
         _rows(c, TM), _rows(s1, TM), _rows(s2, TM)],
        [_stream_out(s_len, BF16, TM, d) for d in DILATIONS for _ in range(3)], rope_fn,
        scratch=[_stream_scratch(TM)])
    qkv = {d: qkv[3 * i:3 * i + 3] for i, d in enumerate(DILATIONS)}
    parts = {d: _attn_fwd(s_len, *qkv[d], d) for d in DILATIONS}

    def mix_fn(*args):
        g, scr = args[6], args[7]
        o1, l1, o2, l2, o3, l3 = [_from_streams(args[2 * i + j], d, scr) for i, d in enumerate(DILATIONS)
                                  for j in range(2)]
        m = jnp.maximum(jnp.maximum(l1, l2), l3)
        e1, e2, e3 = jnp.exp(l1 - m), jnp.exp(l2 - m), jnp.exp(l3 - m)
        tot = e1 + e2 + e3
        attn = (e1 * o1 + e2 * o2 + e3 * o3) / tot
        lse = m + jnp.log(tot)
        return [attn, lse, _rms(attn, g)] + [_to_streams(t, d, scr) for d in DILATIONS[1:] for t in (attn, lse)]

    mix_out = _rowwise(
        "attn_mix", s_len, TM,
        [_stream_in(t, TM, d) for d in DILATIONS for t in parts[d]] + [_whole(gains["attn_norm_g"])],
        [_row_out(s_len, D_ATTN, F32, TM), _row_out(s_len, D_ATTN, F32, TM),
         _row_out(s_len, D_ATTN, BF16, TM, col=0, total=D_MODEL)]
        + [_stream_out(s_len, F32, TM, d) for d in DILATIONS[1:] for _ in range(2)], mix_fn,
        scratch=[_stream_scratch(TM)])
    attn, lse, mixed_half = mix_out[:3]
    attn_s = {1: (attn, lse), DILATIONS[1]: tuple(mix_out[3:5]), DILATIONS[2]: tuple(mix_out[5:7])}

    a_cat, w_b, w_c, d_vec = sp
    ts = SCAN_ROWS
    u_c = _to_chunk_order(proj[:, 3 * D_ATTN:])
    (bu,) = _mm("ssm_bu", (s_len // ts, 1, 1), _rows(u_c, ts), _whole(w_b), "nn",
                [_state_out(s_len, 2 * N_STATE, ts)], epilogue=lambda acc: (_lane_blocks(acc),))
    xs = _scan("ssm_scan", bu, a_cat, False)
    w_c_half = lambda part: (w_c, (N_STATE, D_SSM), lambda *_: (part, 0))
    (z_c,) = _mm("ssm_y", (s_len // ts, 1, 1), _state_in(xs[0], ts), w_c_half(0), "nn",
                 [_row_out(s_len, D_SSM, F32, ts)], more=[(_state_in(xs[1], ts), w_c_half(1))], a_prep=_join_lanes,
                 extras=[_rows(u_c, ts), _whole(d_vec)], epilogue=lambda acc, u, dv: (acc + dv * u,))
    z = _from_chunk_order(z_c)

    def glu_epi(acc, z_blk, b, g):
        y = _gelu(z_blk)
        t = acc + b
        ssm = y * _sigmoid(t)
        return y, t, ssm, _rms(ssm, g)

    yg, t_glu, ssm, mixed = _mm(
        "ssm_glu", (nt, 1, 1), _rows(z, TM), _whole(gw["w_glu"]), "nn",
        [_row_out(s_len, D_SSM, F32, TM), _row_out(s_len, D_SSM, F32, TM), _row_out(s_len, D_SSM, F32, TM),
         _row_out(s_len, D_SSM, BF16, TM, col=1, total=D_MODEL)], a_prep=_gelu,
        extras=[_rows(z, TM), _whole(gains["ssm_b_glu"]), _whole(gains["ssm_norm_g"])], epilogue=glu_epi,
        into=[(mixed_half, 3)])

    def out_epi(acc, h_blk, gp, gn):
        h_new = h_blk + _rms(acc, gp)
        return acc, h_new, _rms(h_new, gn)

    o, h2, n2 = _mm(
        "w_out", (nt, 1, 1), _rows(mixed, TM), _whole(gw["w_out"]), "nn",
        [_row_out(s_len, D_MODEL, F32, TM), _row_out(s_len, D_MODEL, F32, TM), _row_out(s_len, D_MODEL, BF16, TM)],
        extras=[_rows(h1, TM), _whole(gains["mix_post_g"]), _whole(gains["ffn2_pre_g"])], epilogue=out_epi)

    h3, _, sv["ffn2"] = _ffn_fwd(s_len, h2, n2, ffn_weight(2), gains["ffn2_post_g"], gains["ffn2_post_g"])
    need(h3, "w_pu", "ple_w_gate")

    (pu,) = _mm("ple_up", (nt, 1, 1), _rows(p_l, TM), _resident(gw["w_pu"]), "nn",
                [_row_out(s_len, D_MODEL, F32, TM)])

    def ple_epi(acc, pu_blk, h_blk, gp, gn):
        h_new = h_blk + _rms(pu_blk * _sigmoid(acc), gp)
        return acc, h_new, _rms(h_new, gn)

    gt, h4, n_next = _mm(
        "ple_gate", (nt, 1, 1), _rows(h3, TM), _whole(gw["ple_w_gate"]), "nn",
        [_row_out(s_len, D_MODEL, F32, TM), _row_out(s_len, D_MODEL, F32, TM), _row_out(s_len, D_MODEL, BF16, TM)],
        extras=[_rows(pu, TM), _rows(h3, TM), _whole(gains["ple_post_g"]), _whole(g_next)], epilogue=ple_epi)

    sv.update(h1=h1, a_in=a_in, proj=proj, qkv=qkv, attn=attn, attn_s=attn_s, xs=xs, u_c=u_c, z=z, yg=yg, t_glu=t_glu,
              ssm=ssm,
              mixed=mixed, o=o, h2=h2, h3=h3, pu=pu, gt=gt, p_l=p_l, w=gw)
    return h4, n_next, sv


def _vjp(fn, args, cot):
    _, pull = jax.vjp(fn, *args)
    return pull(cot)


def _layer_bwd(s_len, dh4, sv, rope, gw, sp, gains, on_partial=None):
    nt = s_len // TM
    gr = {}
    wg = {}

    def ple_fn(dh, pu, gt, g):
        dpu, dgt, dg = _vjp(lambda a, b, c: _rms(a * _sigmoid(b), c), (pu, gt, g), dh)
        return dpu, dgt, dg

    dpu, dgt, gr["ple_post_g"] = _rowwise(
        "ple_bwd", s_len, TM, [_rows(dh4, TM), _rows(sv["pu"], TM), _rows(sv["gt"], TM), _whole(gains["ple_post_g"])],
        [_row_out(s_len, D_MODEL, BF16, TM), _row_out(s_len, D_MODEL, BF16, TM), _col_out(D_MODEL)], ple_fn)

    def square_w_grad(name, lhs, rhs):
        half = D_MODEL // 2
        (dw,) = _mm(name, (2, 1, nt), (lhs, (TM, half), lambda i, j, k: (k, i)),
                    (rhs, (TM, D_MODEL), lambda i, j, k: (k, 0)), "tn",
                    [((D_MODEL, D_MODEL), BF16, (half, D_MODEL), lambda i, j, k: (i, 0), "row")],
                    acc_shape=(half, D_MODEL))
        return dw

    (wg["ple_w_up"],) = _mm(
        "ple_bwd_wup", (1, 1, nt), (sv["p_l"], (TM, PLE_DIM), lambda i, j, k: (k, 0)),
        (dpu, (TM, D_MODEL), lambda i, j, k: (k, 0)), "tn",
        [((N_DEV, PLE_DIM, D_MODEL // N_DEV), BF16, (N_DEV, PLE_DIM, D_MODEL // N_DEV), lambda i, j, k: (0, 0, 0),
          "row")],
        epilogue=lambda acc: (_column_shards(acc, D_MODEL // N_DEV),), acc_shape=(PLE_DIM, D_MODEL))
    wg["ple_w_gate"] = square_w_grad("ple_bwd_wgate", sv["h3"], dgt)

    def ple_dx_epi(acc, dh, f, g):
        dh3 = dh + acc
        df, dg = _vjp(_rms, (f, g), 0.5 * dh3)
        return dh3, df, dg

    dh3, df2, gr["ffn2_post_g"] = _mm(
        "ple_bwd_dx", (nt, 1, 1), _rows(dgt, TM), _whole(gw["ple_w_gate"]), "nt",
        [_row_out(s_len, D_MODEL, F32, TM), _row_out(s_len, D_MODEL, BF16, TM), _col_out(D_MODEL)],
        extras=[_rows(dh4, TM), _rows(sv["ffn2"]["f"], TM), _whole(gains["ffn2_post_g"])], epilogue=ple_dx_epi)

    def ffn2_final(dn, dh, h, g_pre, o, g_post):
        dx, dg_pre = _vjp(_rms, (h, g_pre), dn)
        dh2 = dh + dx
        do, dg_post = _vjp(_rms, (o, g_post), dh2)
        return dh2, do, dg_pre, dg_post

    wg["ffn2_w_gate"], wg["ffn2_w_up"], wg["ffn2_w_down"], (dh2, do, gr["ffn2_pre_g"], gr["mix_post_g"]) = _ffn_bwd(
        s_len, sv["ffn2"], df2, gw["w_gu2"], gw["w_d2"], ffn2_final,
        [_rows(dh3, FFN_TM), _rows(sv["h2"], FFN_TM), _whole(gains["ffn2_pre_g"]), _rows(sv["o"], FFN_TM),
         _whole(gains["mix_post_g"])],
        [_row_out(s_len, D_MODEL, F32, FFN_TM), _row_out(s_len, D_MODEL, BF16, FFN_TM), _col_out(D_MODEL),
         _col_out(D_MODEL)])

    wg["w_out"] = square_w_grad("w_out_bwd_w", sv["mixed"], do)

    def mixed_epi(acc, attn, ssm, yg, t, g_a, g_s, scr):
        dattn, dg_a = _vjp(_rms, (attn, g_a), acc[:, :D_ATTN])
        dssm, dg_s = _vjp(_rms, (ssm, g_s), acc[:, D_ATTN:])
        sg = _sigmoid(t)
        dt = dssm * yg * sg * (1.0 - sg)
        return ([_to_streams(dattn, d, scr) for d in DILATIONS]
                + [dt, dssm * sg, dg_a, dg_s, jnp.sum(dt, axis=0, keepdims=True)])

    res = _mm(
        "w_out_bwd_x", (nt, 1, 1), _rows(do, TM), _whole(gw["w_out"]), "nt",
        [_stream_out(s_len, F32, TM, d) for d in DILATIONS]
        + [_row_out(s_len, D_SSM, BF16, TM), _row_out(s_len, D_SSM, F32, TM),
           _col_out(D_ATTN), _col_out(D_SSM), _col_out(D_SSM)],
        extras=[_rows(sv["attn"], TM), _rows(sv["ssm"], TM), _rows(sv["yg"], TM), _rows(sv["t_glu"], TM),
                _whole(gains["attn_norm_g"]), _whole(gains["ssm_norm_g"])], epilogue=mixed_epi,
        scratch=[_stream_scratch(TM)])
    dattn_s = dict(zip(DILATIONS, res[:3]))
    dt_glu, dyg_dir, gr["attn_norm_g"], gr["ssm_norm_g"], gr["ssm_b_glu"] = res[3:]

    a_cat, w_b, w_c, d_vec = sp
    ts = SCAN_ROWS
    u_spec = (sv["proj"], (TM, D_SSM), lambda i, *_: (i, 3))
    (wg["ssm_w_glu"],) = _mm(
        "ssm_bwd_wglu", (1, 1, nt), (sv["yg"], (TM, D_SSM), lambda i, j, k: (k, 0)),
        (dt_glu, (TM, D_SSM), lambda i, j, k: (k, 0)), "tn",
        [((D_SSM, D_SSM), BF16, (D_SSM, D_SSM), lambda i, j, k: (0, 0), "row")], acc_shape=(D_SSM, D_SSM))

    def gelu_epi(acc, dy_dir, z, u, dv):
        (dz,) = _vjp(_gelu, (z,), acc + dy_dir)
        return dz, dz * dv, jnp.sum(dz * u, axis=0, keepdims=True)

    dz, du_dir, gr["ssm_d"] = _mm(
        "ssm_bwd_glu", (nt, 1, 1), _rows(dt_glu, TM), _whole(gw["w_glu"]), "nt",
        [_row_out(s_len, D_SSM, BF16, TM), _row_out(s_len, D_SSM, F32, TM), _col_out(D_SSM)],
        extras=[_rows(dyg_dir, TM), _rows(sv["z"], TM), u_spec, _whole(d_vec)], epilogue=gelu_epi)

    x_re, x_im = sv["xs"]
    w_c_half = lambda part: (w_c, (N_STATE, D_SSM), lambda *_: (part, 0))
    w_b_half = lambda part: (w_b, (D_SSM, N_STATE), lambda *_: (0, part))

    n_lb = N_STATE // LANES
    n_chunks = s_len // SCAN_CHUNK
    dz_c = _to_chunk_order(dz)

    def w_c_grad(name, x):
        per = 1024 // LANES
        (dw,) = _mm(name, (N_STATE // 1024, 1, nt), (x, (per, TM, LANES), lambda i, j, k: (i, k, 0)),
                    (dz_c, (TM, D_SSM), lambda i, j, k: (k, 0)), "tn",
                    [((N_STATE, D_SSM), F32, (1024, D_SSM), lambda i, j, k: (i, 0), "row")], acc_shape=(1024, D_SSM),
                    a_prep=_join_lanes)
        return dw

    d_w_c = jnp.concatenate([w_c_grad("ssm_bwd_wc_re", x_re), w_c_grad("ssm_bwd_wc_im", x_im)], axis=0)
    (dxs,) = _mm("ssm_bwd_dx", (s_len // ts, 1, 1), _rows(dz_c, ts), _whole(w_c), "nt",
                 [_state_out(s_len, 2 * N_STATE, ts)], epilogue=lambda acc: (_lane_blocks(acc),))
    g_re, g_im = _scan("ssm_scan_rev", dxs, _conj(a_cat), True)

    def da_fn(xr, xi, xr_before, xi_before, g_r, g_i):
        i = pl.program_id(0)
        chunk = lax.broadcasted_iota(jnp.int32, xr_before.shape, 1)

        def previous(x, x_before):
            wrapped = jnp.where(chunk == 0, 0.0, pltpu.roll(x_before, 1, 1))
            head = jnp.where(i == 0, wrapped, x_before)
            return jnp.concatenate([head, x[:, :ts - n_chunks, :]], axis=1)

        pr, pi = previous(xr, xr_before), previous(xi, xi_before)
        return (jnp.sum(pr * g_r + pi * g_i, axis=1, keepdims=True),
                jnp.sum(pr * g_i - pi * g_r, axis=1, keepdims=True))

    steps = s_len // n_chunks
    before = lambda t: (t, (n_lb, n_chunks, LANES), lambda i: (0, (i * (ts // n_chunks) + steps - 1) % steps, 0))
    lane_sum = ((n_lb, 1, LANES), F32, (n_lb, 1, LANES), lambda *_: (0, 0, 0), "colsum")
    d_a = jnp.concatenate([t.reshape(1, N_STATE) for t in _rowwise(
        "ssm_bwd_da", s_len, ts,
        [_state_in(x_re, ts), _state_in(x_im, ts), before(x_re), before(x_im), _state_in(g_re, ts),
         _state_in(g_im, ts)], [lane_sum] * 2, da_fn)], axis=1)

    def w_b_grad(name, g):
        (dw,) = _mm(name, (1, 1, nt), (sv["u_c"], (TM, D_SSM), lambda i, j, k: (k, 0)),
                    (g, (n_lb, TM, LANES), lambda i, j, k: (0, k, 0)), "tn",
                    [((D_SSM, N_STATE), F32, (D_SSM, N_STATE), lambda i, j, k: (0, 0), "row")],
                    acc_shape=(D_SSM, N_STATE), b_prep=_join_lanes)
        return dw

    d_w_b = jnp.concatenate([w_b_grad("ssm_bwd_wb_re", g_re), w_b_grad("ssm_bwd_wb_im", g_im)], axis=1)
    (du_c,) = _mm("ssm_bwd_du", (s_len // ts, 1, 1), _state_in(g_re, ts), w_b_half(0), "nt",
                  [_row_out(s_len, D_SSM, F32, ts)], more=[(_state_in(g_im, ts), w_b_half(1))], a_prep=_join_lanes)
    du = (_from_chunk_order(du_c) + du_dir).astype(BF16)
    sum_tm = 2 * BAND
    n_sum = s_len // sum_tm
    ins = []
    for d in DILATIONS:
        dq_p, dka, dkb, dva, dvb = _attn_bwd(s_len, *sv["qkv"][d], dattn_s[d], *sv["attn_s"][d], d)
        if d == 1:
            nxt = lambda t: (t, (sum_tm, D_ATTN), lambda i: (jnp.minimum(i + 1, n_sum - 1), 0))
            ins += [_rows(dq_p, sum_tm), _rows(dka, sum_tm), _rows(dkb, sum_tm), nxt(dkb), _rows(dva, sum_tm),
                    _rows(dvb, sum_tm), nxt(dvb)]
        else:
            ins += [_stream_in(dq_p, sum_tm, d), _stream_in(dka, sum_tm, d), _stream_in(dkb, sum_tm, d, BAND, n_sum),
                    _stream_in(dva, sum_tm, d), _stream_in(dvb, sum_tm, d, BAND, n_sum)]
    c, s1, s2 = rope
    ins += [_rows(c, sum_tm), _rows(s1, sum_tm), _rows(s2, sum_tm), _rows(du, sum_tm)]

    def qkv_fn(*args):
        i = pl.program_id(0)
        blocks, scr = args[:-1], args[-1]
        dq_t, dka, dkb, dkb_next, dva, dvb, dvb_next = blocks[:7]
        more = i + 1 < n_sum
        ahead = lambda cur, nxt: jnp.concatenate([cur[BAND:], jnp.where(more, nxt[:BAND], 0.0)], axis=0)
        dk_t = dka + ahead(dkb, dkb_next)
        dv_t = dva + ahead(dvb, dvb_next)
        at = 7
        for d in DILATIONS[1:]:
            dq_p, dka, dkb, dva, dvb = blocks[at:at + 5]
            at += 5
            live = i + BAND // (sum_tm // d) < n_sum
            dq_t = dq_t + _from_streams(dq_p, d, scr)
            dk_t = dk_t + _from_streams(dka + jnp.where(live, dkb, 0.0), d, scr)
            dv_t = dv_t + _from_streams(dva + jnp.where(live, dvb, 0.0), d, scr)
        cc, a1, a2, du_blk = blocks[at:at + 4]
        return (jnp.concatenate([_rope_transposed(dq_t, cc, a1, a2), _rope_transposed(dk_t, cc, a1, a2), dv_t,
                                 du_blk.astype(F32)], axis=1),)

    (dproj,) = _rowwise("attn_bwd_sum", s_len, sum_tm, ins, [_row_out(s_len, 2 * D_MODEL, BF16, sum_tm)], qkv_fn,
                        scratch=[_stream_scratch(sum_tm)])

    (wg["w_in"],) = _mm(
        "w_in_bwd_w", (1, 2, nt), (sv["a_in"], (TM, D_MODEL), lambda i, j, k: (k, 0)),
        (dproj, (TM, D_MODEL), lambda i, j, k: (k, j)), "tn",
        [((N_DEV, D_MODEL, 2 * D_MODEL // N_DEV), BF16, (N_DEV // 2, D_MODEL, 2 * D_MODEL // N_DEV),
          lambda i, j, k: (j, 0, 0), "row")],
        epilogue=lambda acc: (_column_shards(acc, 2 * D_MODEL // N_DEV),), acc_shape=(D_MODEL, D_MODEL))

    tie = on_partial(wg) if on_partial is not None else None
    mix_pre_g = gains["mix_pre_g"] if tie is None else gains["mix_pre_g"] + tie

    def in_epi(acc, dh, h, g_pre, f, g_post):
        dx, dg_pre = _vjp(_rms, (h, g_pre), acc)
        dh1 = dh + dx
        df, dg_post = _vjp(_rms, (f, g_post), 0.5 * dh1)
        return dh1, df, dg_pre, dg_post

    dh1, df1, gr["mix_pre_g"], gr["ffn1_post_g"] = _mm(
        "w_in_bwd_x", (nt, 1, 1), _rows(dproj, TM), _resident(gw["w_in"]), "nt",
        [_row_out(s_len, D_MODEL, F32, TM), _row_out(s_len, D_MODEL, BF16, TM), _col_out(D_MODEL), _col_out(D_MODEL)],
        extras=[_rows(dh2, TM), _rows(sv["h1"], TM), _whole(mix_pre_g), _rows(sv["ffn1"]["f"], TM),
                _whole(gains["ffn1_post_g"])], epilogue=in_epi)

    def ffn1_final(dn, dh, h, g_pre):
        dx, dg_pre = _vjp(_rms, (h, g_pre), dn)
        return dh + dx, dg_pre

    wg["ffn1_w_gate"], wg["ffn1_w_up"], wg["ffn1_w_down"], (dh0, gr["ffn1_pre_g"]) = _ffn_bwd(
        s_len, sv["ffn1"], df1, gw["w_gu1"], gw["w_d1"], ffn1_final,
        [_rows(dh1, FFN_TM), _rows(sv["ffn1"]["h"], FFN_TM), _whole(gains["ffn1_pre_g"])],
        [_row_out(s_len, D_MODEL, F32, FFN_TM), _col_out(D_MODEL)])

    return dh0, wg, gr, (d_a, d_w_b, d_w_c)


def _peers():
    x, y, c = lax.axis_index("x"), lax.axis_index("y"), lax.axis_index("c")
    me = 4 * x + 2 * y + c
    peers = []
    for k in range(1, N_DEV):
        kx, ky, kc = (k >> 2) & 1, (k >> 1) & 1, k & 1
        px, py, pc = x ^ kx, y ^ ky, c ^ kc
        peers.append(((px, py, pc), 4 * px + 2 * py + pc))
    return me, peers


_HBM_SPEC = pl.BlockSpec(memory_space=pltpu.HBM)
_SEM_SPEC = pl.BlockSpec(memory_space=pltpu.SEMAPHORE)
_DATAFLOW = pltpu.SideEffectType.DATAFLOW_SIDE_EFFECTING


def _device_index():
    return 4 * lax.axis_index("x") + 2 * lax.axis_index("y") + lax.axis_index("c")


def _landing(arrays, scatter):
    me = _device_index()
    out = []
    for a, scattered in zip(arrays, scatter):
        own = lax.dynamic_index_in_dim(a, me, 0, keepdims=True) if scattered else a[None]
        buf = lax.empty((N_DEV,) + own.shape[1:], a.dtype)
        out.append(lax.dynamic_update_slice_in_dim(buf, own, me, 0))
    return out


def _split_copies(src_refs, land_refs, send_sems, recv_sems, scatter):
    me, peers = _peers()
    pairs = []
    for t in range(len(src_refs)):
        for k, (peer, peer_id) in enumerate(peers):
            src = src_refs[t].at[peer_id] if scatter[t] else src_refs[t]
            sem = t * (N_DEV - 1) + k
            mk = lambda slot, src=src, t=t, sem=sem, peer=peer: pltpu.make_async_remote_copy(
                src_ref=src, dst_ref=land_refs[t].at[slot], send_sem=send_sems.at[sem], recv_sem=recv_sems.at[sem],
                device_id=peer, device_id_type=pl.DeviceIdType.MESH)
            pairs.append((functools.partial(mk, me), functools.partial(mk, peer_id)))
    return pairs


def _exchange_start(name, arrays, scatter, order_after):
    n = len(arrays)
    landing = _landing(arrays, scatter)

    def body(*refs):
        src_refs, land_refs = refs[:n], refs[n:2 * n]
        send_sems, recv_sems = refs[2 * n + 1], refs[2 * n + 2]
        token_ref = refs[-1]
        for outgoing, _ in _split_copies(src_refs, land_refs, send_sems, recv_sems, scatter):
            outgoing().start()
        token_ref[...] = jnp.zeros_like(token_ref)

    sem_shape = pltpu.SemaphoreType.DMA((n * (N_DEV - 1),))
    thru = [pltpu.HBM(a.shape, a.dtype) for a in list(arrays) + landing]
    hbm = lambda t: pltpu.with_memory_space_constraint(t, pltpu.HBM)
    res = pl.pallas_call(
        body,
        name=name,
        in_specs=[_HBM_SPEC] * (2 * n) + [pl.BlockSpec(memory_space=pl.ANY)],
        out_specs=[_SEM_SPEC, _SEM_SPEC] + [_HBM_SPEC] * (2 * n) + [pl.BlockSpec(memory_space=pltpu.VMEM)],
        out_shape=[sem_shape, sem_shape] + thru + [jax.ShapeDtypeStruct((8, 128), F32)],
        input_output_aliases={i: 2 + i for i in range(2 * n)},
        compiler_params=pltpu.CompilerParams(has_side_effects=_DATAFLOW),
    )(*[hbm(t) for t in list(arrays) + landing], order_after)
    return (res[0], res[1], res[2:2 + n], res[2 + n:2 + 2 * n]), res[-1]


def _exchange_wait(name, handle, scatter, order_after):
    send_sems, recv_sems, sources, landing = handle
    n = len(sources)

    def body(*refs):
        src_refs, land_refs = refs[:n], refs[n:2 * n]
        for outgoing, arrival in _split_copies(src_refs, land_refs, refs[2 * n], refs[2 * n + 1], scatter):
            outgoing().wait_send()
            arrival().wait_recv()

    thru = [pltpu.HBM(a.shape, a.dtype) for a in list(sources) + list(landing)]
    res = pl.pallas_call(
        body,
        name=name,
        in_specs=[_HBM_SPEC] * (2 * n) + [_SEM_SPEC, _SEM_SPEC, pl.BlockSpec(memory_space=pl.ANY)],
        out_specs=[_HBM_SPEC] * (2 * n),
        out_shape=thru,
        input_output_aliases={i: i for i in range(2 * n)},
        compiler_params=pltpu.CompilerParams(has_side_effects=_DATAFLOW),
    )(*sources, *landing, send_sems, recv_sems, order_after)
    return list(res[n:])


def _adam_math(g, w, m, v):
    m = ADAM_B1 * m + (1.0 - ADAM_B1) * g
    v = ADAM_B2 * v + (1.0 - ADAM_B2) * (g * g)
    m_hat = m / (1.0 - ADAM_B1 ** ADAM_STEP)
    v_hat = v / (1.0 - ADAM_B2 ** ADAM_STEP)
    delta = -ADAM_LR * (m_hat / (jnp.sqrt(v_hat) + ADAM_EPS) + ADAM_WD * w)
    return delta, m, v


def _adamw(name, recv, recv_block, recv_map, w, m, v, tr, layer=None, prev=None, after=None):
    def fn(r, wb, mb, vb, *token):
        g = r[0].astype(F32)
        for s in range(1, N_DEV):
            g = g + r[s].astype(F32)
        return (g,) + _adam_math(g, wb, mb, vb) + tuple(jnp.zeros_like(t) for t in token)

    rows, cols = w.shape[-2:]
    if layer is None:
        spec = lambda t: _rows(t, tr)
        out = _row_out(rows, cols, F32, tr)
    else:
        spec = lambda t: (t, (None, tr, cols), lambda i: (layer, i, 0))
        out = (w.shape, F32, (None, tr, cols), lambda i: (layer, i, 0), "row")
    ins = [(recv, recv_block, recv_map), spec(w), spec(m), spec(v)]
    outs = [out] * 4
    if after is not None:
        ins.append(_whole(after))
        outs.append((after.shape, F32, after.shape, lambda i: (0, 0), "row"))
    into = [] if prev is None else [(t, k) for k, t in enumerate(prev)]
    return _rowwise(name, rows, tr, ins, outs, fn, into=into)


def _pack_small(tensors):
    flat = jnp.concatenate([tensors[n].reshape(-1).astype(F32) for n in SMALL])
    padded = -(-flat.shape[0] // SMALL_PAD) * SMALL_PAD
    return jnp.pad(flat, (0, padded - flat.shape[0])).reshape(padded // SMALL_LANES, SMALL_LANES)


def _local_step(x, p, positions, loss_target, w, layer_weights, layer_done=None, layer_partial=None):
    s_len = x.shape[0]
    depth = p.shape[0]
    rope = _rope_tables(positions)
    gains = [{n: w[n][l].reshape(1, -1) for n in SMALL if w[n].ndim == 2 and n != "ssm_log_dt"} for l in range(depth)]
    ssm_args = lambda l: tuple(w[n][l] for n in ("ssm_lam_re", "ssm_lam_im", "ssm_log_dt", "ssm_b_re", "ssm_b_im",
                                                  "ssm_c_re", "ssm_c_im"))
    sps, pulls = [], []
    for l in range(depth):
        (a_cat, w_b, w_c), pull = jax.vjp(_ssm_params, *ssm_args(l))
        sps.append((a_cat, w_b.astype(BF16), w_c.astype(BF16), w["ssm_d"][l].reshape(1, D_SSM)))
        pulls.append(pull)

    (n,) = _rowwise("pre_norm", s_len, TM, [_rows(x, TM), _whole(gains[0]["ffn1_pre_g"])],
                    [_row_out(s_len, D_MODEL, BF16, TM)], lambda a, g: (_rms(a, g),))
    h = x
    saved = []
    for l in range(depth):
        g_next = gains[l + 1]["ffn1_pre_g"] if l + 1 < depth else gains[l]["ffn1_pre_g"]
        h, n, sv = _layer_fwd(s_len, h, n, p[l], rope, layer_weights(l, h), sps[l], gains[l], g_next)
        saved.append(sv)

    def loss_fn(y, t):
        e = y - t
        return e * (1.0 / D_MODEL), jnp.sum(e * e, axis=0, keepdims=True)

    dh, sq = _rowwise("loss", s_len, TM, [_rows(h, TM), _rows(loss_target, TM)],
                      [_row_out(s_len, D_MODEL, F32, TM), _col_out(D_MODEL)], loss_fn)
    loss = 0.5 * jnp.sum(sq) / D_MODEL

    w_grads, small_grads = [None] * depth, [None] * depth
    tie = None
    for l in reversed(range(depth)):
        g_l = gains[l] if tie is None else dict(gains[l], ple_post_g=gains[l]["ple_post_g"] + tie)
        partial = None if layer_partial is None else functools.partial(layer_partial, l)
        dh, wg, gr, (d_a, d_w_b, d_w_c) = _layer_bwd(s_len, dh, saved[l], rope, saved[l]["w"], sps[l], g_l, partial)
        d_lre, d_lim, d_dt, d_bre, d_bim, d_cre, d_cim = pulls[l]((d_a, d_w_b, d_w_c))
        gr.update(ssm_lam_re=d_lre, ssm_lam_im=d_lim, ssm_log_dt=d_dt, ssm_b_re=d_bre, ssm_b_im=d_bim,
                  ssm_c_re=d_cre, ssm_c_im=d_cim)
        gr = {n: g.reshape(w[n].shape[1:]) for n, g in gr.items()}
        w_grads[l], small_grads[l] = wg, gr
        if layer_done is not None:
            tie = layer_done(l, wg, gr, dh)
    return loss, dh, w_grads, small_grads


GROUP_OF = {"ffn1_w_gate": "g1", "ffn1_w_up": "u1", "ffn2_w_gate": "g2", "ffn2_w_up": "u2", "ffn1_w_down": "d1",
            "ffn2_w_down": "d2", "w_out": "out", "ple_w_gate": "pg", "w_in": "in", "ple_w_up": "pu", "ssm_w_glu": "glu"}
GROUPS = tuple(GROUP_OF[n] for n in SHARDED)
FIRST_PARTS = (("g1", "u1"), ("d1",), ("in", "glu", "out"), ("g2", "u2", "d2", "pu", "pg"))
_COLUMN_VIEWS = {"w_gu1": ("g1", "u1"), "w_gu2": ("g2", "u2"), "w_in": ("in",), "w_pu": ("pu",)}
_ROW_VIEWS = {"d1": "w_d1", "d2": "w_d2", "out": "w_out", "pg": "ple_w_gate", "glu": "w_glu"}


def _shard_groups(w, l):
    return {group: w[n][l].astype(BF16) for n, group in GROUP_OF.items()}


def _gathered_views(gathered, after=None):
    views = {}
    for name, groups in _COLUMN_VIEWS.items():
        if all(g in gathered for g in groups):
            views[name] = _natural_cols("relayout_" + name, [gathered[g] for g in groups], after)
            after = None
    for group, name in _ROW_VIEWS.items():
        if group in gathered:
            g = gathered[group]
            views[name] = g.reshape(N_DEV * g.shape[1], g.shape[2])
    return views


def _grad_groups(wg):
    rows = lambda t: t if t.ndim == 3 else t.reshape(N_DEV, t.shape[0] // N_DEV, t.shape[1])
    return {GROUP_OF[n]: rows(t) for n, t in wg.items()}


def _update_sharded(name, recv_groups, w, m, v, layer, prev, after=None):
    recv = recv_groups[GROUP_OF[name]]
    rows, cols = w.shape[1:]
    tr = rows if rows <= 256 else (rows // 2 if rows % 256 else 256)
    return _adamw(f"adamw_{name}", recv, (N_DEV, tr, cols), lambda i: (0, i, 0), w, m, v, tr, layer, prev, after)


def kernel(x, p, positions, ffn1_pre_g, ffn1_w_gate, ffn1_w_up, ffn1_w_down, ffn1_post_g, mix_pre_g, w_in, attn_norm_g, ssm_lam_re, ssm_lam_im, ssm_log_dt, ssm_b_re, ssm_b_im, ssm_c_re, ssm_c_im, ssm_d, ssm_w_glu, ssm_b_glu, ssm_norm_g, w_out, mix_post_g, ffn2_pre_g, ffn2_w_gate, ffn2_w_up, ffn2_w_down, ffn2_post_g, ple_w_up, ple_w_gate, ple_post_g, loss_target, m_ffn1_pre_g, m_ffn1_w_gate, m_ffn1_w_up, m_ffn1_w_down, m_ffn1_post_g, m_mix_pre_g, m_w_in, m_attn_norm_g, m_ssm_lam_re, m_ssm_lam_im, m_ssm_log_dt, m_ssm_b_re, m_ssm_b_im, m_ssm_c_re, m_ssm_c_im, m_ssm_d, m_ssm_w_glu, m_ssm_b_glu, m_ssm_norm_g, m_w_out, m_mix_post_g, m_ffn2_pre_g, m_ffn2_w_gate, m_ffn2_w_up, m_ffn2_w_down, m_ffn2_post_g, m_ple_w_up, m_ple_w_gate, m_ple_post_g, v_ffn1_pre_g, v_ffn1_w_gate, v_ffn1_w_up, v_ffn1_w_down, v_ffn1_post_g, v_mix_pre_g, v_w_in, v_attn_norm_g, v_ssm_lam_re, v_ssm_lam_im, v_ssm_log_dt, v_ssm_b_re, v_ssm_b_im, v_ssm_c_re, v_ssm_c_im, v_ssm_d, v_ssm_w_glu, v_ssm_b_glu, v_ssm_norm_g, v_w_out, v_mix_post_g, v_ffn2_pre_g, v_ffn2_w_gate, v_ffn2_w_up, v_ffn2_w_down, v_ffn2_post_g, v_ple_w_up, v_ple_w_gate, v_ple_post_g):
    args = dict(locals())
    w = {n: args[n] for n in WEIGHTS}
    mom = {n: args["m_" + n] for n in WEIGHTS}
    var = {n: args["v_" + n] for n in WEIGHTS}
    depth = p.shape[0]

    gathers, scatters, recv, tokens = {}, {}, {}, {}

    def start(name, groups, names, scatter, order_after):
        flags = scatter if isinstance(scatter, tuple) else (scatter,) * len(names)
        handle, token = _exchange_start(name, [groups[n] for n in names], flags, order_after)
        return (handle, names, flags), token

    def finish(name, pending, order_after):
        handle, names, flags = pending
        return dict(zip(names, _exchange_wait(name, handle, flags, order_after)))

    def gather_next(l, order_after):
        if l + 1 >= depth:
            return None
        gathers[l + 1], token = start("all_gather_start", _shard_groups(w, l + 1), GROUPS, False, order_after)
        return token

    def layer_weights(l, h):
        views = {}
        if l > 0:
            got = finish("all_gather_wait", gathers.pop(l), h)
            views.update(_gathered_views(got, gather_next(l, got[GROUPS[0]])))
            return lambda name, after: views[name]

        shards = _shard_groups(w, 0)
        pending, token = start("all_gather_start_0", shards, FIRST_PARTS[0], False, h)
        got = finish("all_gather_wait_0", pending, token)
        waiting = {}
        token = got[FIRST_PARTS[0][0]]
        for i, part in enumerate(FIRST_PARTS[1:], 1):
            waiting[i], token = start(f"all_gather_start_{i}", shards, part, False, token)
        views.update(_gathered_views(got, gather_next(0, token)))

        def weight(name, after):
            for i in sorted(waiting):
                if name not in views:
                    views.update(_gathered_views(finish(f"all_gather_wait_{i}", waiting.pop(i), after)))
            return views[name]

        return weight

    small_l = lambda t, l: _pack_small({n: t[n][l] for n in SMALL})
    scatter_flags = lambda names: tuple(n != "small" for n in names)
    late = ("g1", "u1", "d1", "small")
    early = tuple(g for g in GROUPS if g not in late)

    def layer_partial(l, wg):
        if l > 0:
            return None
        scatters["early"], token = start("reduce_scatter_start_a", _grad_groups(wg), early, scatter_flags(early),
                                         wg["w_in"])
        return token[0, 0]

    def layer_done(l, wg, gr, dh):
        if l + 1 < depth:
            recv[l + 1] = finish("reduce_scatter_wait", scatters.pop(l + 1), dh)
        groups = dict(_grad_groups(wg), small=_pack_small(gr))
        names = late if l == 0 else GROUPS + ("small",)
        scatters[l], tokens[l] = start("reduce_scatter_start_b" if l == 0 else "reduce_scatter_start", groups, names,
                                       scatter_flags(names), dh)
        return tokens[l][0, 0]

    loss, dx, w_grads, small_grads = _local_step(x[0], p[:, 0], positions[0], loss_target[0], w, layer_weights,
                                                 layer_done, layer_partial)
    loss = lax.psum(loss, MESH_AXES)

    results = {n: None for n in SHARDED}
    last = tokens[0]
    for l in reversed(range(1, depth)):
        for n in SHARDED:
            *results[n], last = _update_sharded(n, recv[l], w[n], mom[n], var[n], l, results[n], last)
    recv[0] = dict(finish("reduce_scatter_wait_a", scatters.pop("early"), last))
    recv[0].update(finish("reduce_scatter_wait_b", scatters.pop(0), last))
    for n in SHARDED:
        results[n] = _update_sharded(n, recv[0], w[n], mom[n], var[n], 0, results[n])
    result = dict(results)

    recv_small = jnp.concatenate([recv[l]["small"] for l in range(depth)], axis=1)
    packed = [jnp.concatenate([small_l(t, l) for l in range(depth)], axis=0) for t in (w, mom, var)]
    small_out = _adamw("adamw_small", recv_small, (N_DEV, 8, SMALL_LANES), lambda i: (0, i, 0), *packed, 8)
    for q in range(4):
        flat = small_out[q].reshape(depth, -1)
        off = 0
        for n in SMALL:
            size = math.prod(w[n].shape[1:])
            result.setdefault(n, [None] * 4)[q] = flat[:, off:off + size].reshape(w[n].shape)
            off += size

    outputs = [loss, dx[None]]
    for q in range(4):
        outputs += [result[n][q] for n in WEIGHTS]
    return tuple(outputs)
```

```python
import functools
import math

import jax
import jax.numpy as jnp
from jax import lax
from jax.experimental import pallas as pl
from jax.experimental.pallas import tpu as pltpu

F32 = jnp.float32
BF16 = jnp.bfloat16

N_DEV = 8
D_MODEL = 1024
D_FF = 2816
FF_SHARD = D_FF // N_DEV
D_ATTN = 512
D_SSM = 512
HEAD_DIM = 64
N_HEADS = 8
ROPE_DIM = 16
ROPE_THETA = 500000.0
DILATIONS = (1, 4, 16)
BAND = 128
N_GROUPS = 32
SSM_GROUP = 16
SSM_STATE = 64
N_STATE = N_GROUPS * SSM_STATE
PLE_DIM = 256
NORM_EPS = 1e-6
ADAM_LR, ADAM_B1, ADAM_B2, ADAM_EPS, ADAM_WD, ADAM_STEP = 0.001, 0.9, 0.999, 1e-08, 0.01, 10

VMEM_LIMIT_BYTES = 52 * 1024 * 1024
TM = 512
MESH_AXES = ("x", "y", "c")

WEIGHTS = ['ffn1_pre_g', 'ffn1_w_gate', 'ffn1_w_up', 'ffn1_w_down', 'ffn1_post_g', 'mix_pre_g', 'w_in', 'attn_norm_g',
           'ssm_lam_re', 'ssm_lam_im', 'ssm_log_dt', 'ssm_b_re', 'ssm_b_im', 'ssm_c_re', 'ssm_c_im', 'ssm_d',
           'ssm_w_glu', 'ssm_b_glu', 'ssm_norm_g', 'w_out', 'mix_post_g', 'ffn2_pre_g', 'ffn2_w_gate', 'ffn2_w_up',
           'ffn2_w_down', 'ffn2_post_g', 'ple_w_up', 'ple_w_gate', 'ple_post_g']
SHARDED = ['ffn1_w_gate', 'ffn1_w_up', 'ffn1_w_down', 'w_in', 'ssm_w_glu', 'w_out', 'ffn2_w_gate', 'ffn2_w_up',
           'ffn2_w_down', 'ple_w_up', 'ple_w_gate']
SMALL = [n for n in WEIGHTS if n not in SHARDED]
SMALL_LANES = 1024
SMALL_PAD = 8 * SMALL_LANES


def _params(n_axes):
    return pltpu.CompilerParams(dimension_semantics=("arbitrary",) * n_axes, vmem_limit_bytes=VMEM_LIMIT_BYTES)


_DIMS = {"nn": (((1,), (0,)), ((), ())), "nt": (((1,), (1,)), ((), ())), "tn": (((0,), (0,)), ((), ()))}


def _dot(a, b, mode):
    return lax.dot_general(a.astype(BF16), b.astype(BF16), _DIMS[mode], preferred_element_type=F32)


def _store(out_refs, vals, kinds, first):
    for ref, val, kind in zip(out_refs, vals, kinds):
        if isinstance(val, (list, tuple)):
            for q, piece in enumerate(val):
                ref[q] = piece.astype(ref.dtype)
        elif kind == "row":
            ref[...] = val.astype(ref.dtype)
        else:
            @pl.when(first)
            def _(ref=ref, val=val):
                ref[...] = val.astype(ref.dtype)

            @pl.when(jnp.logical_not(first))
            def _(ref=ref, val=val):
                ref[...] += val.astype(ref.dtype)


def _call(name, body, grid, ins, outs, scratch, into):
    arrays = [t[0] for t in ins]
    in_specs = [_in_spec(t) for t in ins]
    aliases = {}
    for arr, k in into:
        arrays.append(arr)
        in_specs.append(pl.BlockSpec(memory_space=pl.ANY))
        aliases[len(arrays) - 1] = k
    return pl.pallas_call(
        body,
        name=name,
        grid=grid,
        in_specs=in_specs,
        out_specs=[pl.BlockSpec(o[2], o[3]) for o in outs],
        out_shape=[jax.ShapeDtypeStruct(o[0], o[1]) for o in outs],
        scratch_shapes=list(scratch),
        input_output_aliases=aliases,
        compiler_params=_params(len(grid)),
    )(*arrays)


def _mm(name, grid, a, b, mode, outs, extras=(), epilogue=None, acc_shape=None, scratch=(), into=(), more=(),
        a_prep=None, b_prep=None):
    prep_a = a_prep if a_prep is not None else (lambda t: t)
    prep_b = b_prep if b_prep is not None else (lambda t: t)
    nk = grid[2]
    ne, no = len(extras), len(outs)
    kinds = [o[4] for o in outs]
    assert all(k == "row" for k in kinds) or grid[1] == 1
    n_ab = 2 + 2 * len(more)
    n_in = n_ab + ne + len(into)
    n_acc = int(nk > 1)

    def body(*refs):
        ex_refs = refs[n_ab:n_ab + ne]
        out_refs = refs[n_in:n_in + no]
        scr_refs = refs[n_in + no + n_acc:]
        part = _dot(prep_a(refs[0][...]), prep_b(refs[1][...]), mode)
        for p in range(2, n_ab, 2):
            part = part + _dot(prep_a(refs[p][...]), prep_b(refs[p + 1][...]), mode)
        first = pl.program_id(0) == 0

        def finish(acc):
            vals = epilogue(acc, *[r[...] for r in ex_refs], *scr_refs) if epilogue is not None else (acc,)
            _store(out_refs, vals, kinds, first)

        if nk == 1:
            finish(part)
        else:
            acc_ref = refs[n_in + no]
            k = pl.program_id(2)

            @pl.when(k == 0)
            def _():
                acc_ref[...] = part

            @pl.when(k > 0)
            def _():
                acc_ref[...] += part

            @pl.when(k == nk - 1)
            def _():
                finish(acc_ref[...])

    acc = [pltpu.VMEM(acc_shape, F32)] if nk > 1 else []
    pairs = tuple(t for pair in more for t in pair)
    return _call(name, body, grid, (a, b) + pairs + tuple(extras), outs, acc + list(scratch), into)


def _rowwise(name, n_rows, tm, ins, outs, fn, scratch=(), into=()):
    kinds = [o[4] for o in outs]
    ni, no = len(ins), len(outs)
    n_in = ni + len(into)

    def body(*refs):
        vals = fn(*[r[...] for r in refs[:ni]], *refs[n_in + no:])
        _store(refs[n_in:n_in + no], vals, kinds, pl.program_id(0) == 0)

    return _call(name, body, (n_rows // tm,), ins, outs, scratch, into)


def _rows(arr, tm, col=0, width=None):
    width = arr.shape[1] if width is None else width
    return (arr, (tm, width), lambda i, *_: (i, col))


def _whole(arr):
    nd = arr.ndim
    return (arr, arr.shape, lambda *_: (0,) * nd)


def _resident(arr):
    nd = arr.ndim
    return (arr, arr.shape, lambda *_: (0,) * nd, dict(pipeline_mode=pl.Buffered(1)))


def _in_spec(t):
    return pl.BlockSpec(t[1], t[2], **(t[3] if len(t) > 3 else {}))


def _row_out(n_rows, width, dtype, tm, col=0, total=None):
    return ((n_rows, width if total is None else total), dtype, (tm, width), lambda i, *_: (i, col), "row")


def _col_out(width):
    return ((1, width), F32, (1, width), lambda *_: (0, 0), "colsum")


def _lane_blocks(acc):
    return [acc[:, q * LANES:(q + 1) * LANES] for q in range(acc.shape[1] // LANES)]


def _join_lanes(blk):
    return jnp.concatenate([blk[q] for q in range(blk.shape[0])], axis=1)


def _column_shards(acc, width):
    return [acc[:, q * width:(q + 1) * width] for q in range(acc.shape[1] // width)]


def _natural_cols(name, parts, after=None):
    rows, width = parts[0].shape[1:]
    tr = min(rows, 256)
    n = len(parts)

    def body(*refs):
        out_ref = refs[-1]
        for t in range(n):
            for j in range(N_DEV):
                lo = (t * N_DEV + j) * width
                out_ref[:, lo:lo + width] = refs[t][j]

    ins = [(t, (N_DEV, tr, width), lambda i: (0, i, 0)) for t in parts]
    if after is not None:
        ins.append(_whole(after))
    out = _row_out(rows, n * N_DEV * width, parts[0].dtype, tr)
    return _call(name, body, (rows // tr,), ins, [out], (), ())[0]


def _rms(x, g):
    return x * lax.rsqrt(jnp.mean(x * x, axis=-1, keepdims=True) + NORM_EPS) * g


def _sigmoid(x):
    return 1.0 / (1.0 + jnp.exp(-x))


def _gelu(x):
    return 0.5 * x * (1.0 + jnp.tanh(0.7978845608028654 * (x + 0.044715 * x * x * x)))


FFN_TM = 256
FF_HALF = D_FF // 2


def _ffn_fwd(s_len, h, n, weight, g_post, g_next):
    def gu_epi(acc):
        gate, up = acc[:, :D_FF], acc[:, D_FF:]
        return acc, gate * _sigmoid(gate) * up

    w_gu = weight("gu", n)
    gu, act = _mm("ffn_gate_up", (s_len // FFN_TM, 1, 1), _rows(n, FFN_TM), _resident(w_gu), "nn",
                  [_row_out(s_len, 2 * D_FF, BF16, FFN_TM), _row_out(s_len, D_FF, BF16, FFN_TM)], epilogue=gu_epi)
    w_d = weight("d", act)

    def down_epi(acc, h_blk, gp, gn):
        h_new = h_blk + 0.5 * _rms(acc, gp)
        return acc, h_new, _rms(h_new, gn)

    f, h_new, n_next = _mm(
        "ffn_down", (s_len // TM, 1, 1), _rows(act, TM), _resident(w_d), "nn",
        [_row_out(s_len, D_MODEL, F32, TM), _row_out(s_len, D_MODEL, F32, TM), _row_out(s_len, D_MODEL, BF16, TM)],
        extras=[_rows(h, TM), _whole(g_post), _whole(g_next)], epilogue=down_epi)
    return h_new, n_next, dict(n=n, gu=gu, act=act, f=f, h=h)


def _ffn_bwd(s_len, saved, df, w_gu, w_d, final_epi, final_extras, final_outs):
    nt = s_len // TM

    def act_epi(acc, gu_blk):
        gate = gu_blk[:, :D_FF].astype(F32)
        up = gu_blk[:, D_FF:].astype(F32)
        sg = _sigmoid(gate)
        dgate = acc * up * sg * (1.0 + gate * (1.0 - sg))
        dup = acc * gate * sg
        return (jnp.concatenate([dgate, dup], axis=1),)

    (dgu,) = _mm("ffn_bwd_act", (s_len // FFN_TM, 1, 1), _rows(df, FFN_TM), _resident(w_d), "nt",
                 [_row_out(s_len, 2 * D_FF, BF16, FFN_TM)], extras=[_rows(saved["gu"], FFN_TM)], epilogue=act_epi)

    (d_w_d,) = _mm(
        "ffn_bwd_wdown", (2, 1, nt), (saved["act"], (TM, FF_HALF), lambda i, j, k: (k, i)),
        (df, (TM, D_MODEL), lambda i, j, k: (k, 0)), "tn",
        [((D_FF, D_MODEL), BF16, (FF_HALF, D_MODEL), lambda i, j, k: (i, 0), "row")], acc_shape=(FF_HALF, D_MODEL))

    def w_grad(name, half):
        (dw,) = _mm(
            name, (1, 2, nt), (saved["n"], (TM, D_MODEL), lambda i, j, k: (k, 0)),
            (dgu, (TM, FF_HALF), lambda i, j, k: (k, 2 * half + j)), "tn",
            [((N_DEV, D_MODEL, FF_SHARD), BF16, (N_DEV // 2, D_MODEL, FF_SHARD), lambda i, j, k: (j, 0, 0), "row")],
            epilogue=lambda acc: (_column_shards(acc, FF_SHARD),), acc_shape=(D_MODEL, FF_HALF))
        return dw

    d_w_gate, d_w_up = w_grad("ffn_bwd_wgate", 0), w_grad("ffn_bwd_wup", 1)

    outs = _mm("ffn_bwd_dn", (s_len // FFN_TM, 1, 1), _rows(dgu, FFN_TM), _resident(w_gu), "nt",
               final_outs, extras=final_extras, epilogue=final_epi)
    return d_w_gate, d_w_up, d_w_d, outs


def _band_mask(first_block):
    qi = lax.broadcasted_iota(jnp.int32, (BAND, 2 * BAND), 0)
    kj = lax.broadcasted_iota(jnp.int32, (BAND, 2 * BAND), 1)
    ok = (kj >= qi) & (kj <= qi + BAND)
    return ok & (jnp.logical_not(first_block) | (kj >= BAND))


def _attn_views(s_len, d):
    n_str = s_len // d
    nb = n_str // BAND
    blk = (BAND, D_ATTN)
    cur = lambda r, b: (b, r)
    prev = lambda r, b: (jnp.maximum(b - 1, 0), r)
    return n_str, nb, blk, cur, prev


STREAM_LANES = 128


def _stream_scratch(tm):
    return pltpu.VMEM((D_ATTN // STREAM_LANES, tm, STREAM_LANES), F32)


def _to_streams(x, d, scr):
    if d == 1:
        return x
    tm = x.shape[0]
    chunks = D_ATTN // STREAM_LANES
    for c in range(chunks):
        scr[c, 0:tm, :] = x[:, c * STREAM_LANES:(c + 1) * STREAM_LANES]
    return jnp.concatenate([scr.at[c][pl.ds(r, tm // d, stride=d), :] for r in range(d) for c in range(chunks)], axis=1)


def _from_streams(v, d, scr):
    if d == 1:
        return v
    rows = v.shape[0]
    chunks = D_ATTN // STREAM_LANES
    for r in range(d):
        for c in range(chunks):
            lo = r * D_ATTN + c * STREAM_LANES
            scr.at[c][pl.ds(r, rows, stride=d), :] = v[:, lo:lo + STREAM_LANES]
    return jnp.concatenate([scr[c, 0:rows * d, :] for c in range(chunks)], axis=1)


def _stream_in(arr, tm, d, ahead=0, n_blocks=None):
    rows = tm // d
    step = ahead // rows
    if ahead:
        return (arr, (rows, d * D_ATTN), lambda i, *_: (jnp.minimum(i + step, n_blocks - 1), 0))
    return (arr, (rows, d * D_ATTN), lambda i, *_: (i, 0))


def _stream_out(s_len, dtype, tm, d):
    return ((s_len // d, d * D_ATTN), dtype, (tm // d, d * D_ATTN), lambda i, *_: (i, 0), "row")


def _attn_fwd(s_len, q, k, v, d):
    n_str, nb, blk, cur, prev = _attn_views(s_len, d)
    view = lambda t: t

    def body(q_ref, kp_ref, kc_ref, vp_ref, vc_ref, o_ref, lse_ref):
        mask = _band_mask(pl.program_id(1) == 0)
        even = lax.broadcasted_iota(jnp.int32, (BAND, 2 * HEAD_DIM), 1) < HEAD_DIM
        qq = q_ref[...]
        kk = jnp.concatenate([kp_ref[...], kc_ref[...]], axis=0)
        vv = jnp.concatenate([vp_ref[...], vc_ref[...]], axis=0)
        for pair in range(N_HEADS // 2):
            sl = slice(2 * pair * HEAD_DIM, 2 * (pair + 1) * HEAD_DIM)
            q_p, k_p, v_p = qq[:, sl], kk[:, sl], vv[:, sl]
            o, lse = [], []
            for half in range(2):
                sel = even if half == 0 else jnp.logical_not(even)
                s = _dot(jnp.where(sel, q_p, jnp.zeros_like(q_p)), k_p, "nt") * (HEAD_DIM ** -0.5)
                s = jnp.where(mask, s, -1e30)
                m = jnp.max(s, axis=-1, keepdims=True)
                e = jnp.exp(s - m)
                den = jnp.sum(e, axis=-1, keepdims=True)
                o.append(_dot(e / den, v_p, "nn"))
                lse.append(m + jnp.log(den))
            o_ref[:, sl] = jnp.where(even, o[0], o[1])
            lse_ref[:, sl] = jnp.where(even, lse[0], lse[1])

    o, lse = pl.pallas_call(
        body,
        name=f"attn_fwd_d{d}",
        grid=(d, nb),
        in_specs=[pl.BlockSpec(blk, cur), pl.BlockSpec(blk, prev), pl.BlockSpec(blk, cur),
                  pl.BlockSpec(blk, prev), pl.BlockSpec(blk, cur)],
        out_specs=[pl.BlockSpec(blk, cur), pl.BlockSpec(blk, cur)],
        out_shape=[jax.ShapeDtypeStruct((n_str, d * D_ATTN), F32)] * 2,
        compiler_params=_params(2),
    )(view(q), view(k), view(k), view(v), view(v))
    return o, lse


def _attn_bwd(s_len, q, k, v, dattn, attn, lse, d):
    n_str, nb, blk, cur, prev = _attn_views(s_len, d)
    view = lambda t: t

    def body(q_ref, kp_ref, kc_ref, vp_ref, vc_ref, da_ref, at_ref, lse_ref, dq_ref, dka_ref, dkb_ref, dva_ref, dvb_ref):
        mask = _band_mask(pl.program_id(1) == 0)
        even = lax.broadcasted_iota(jnp.int32, (BAND, 2 * HEAD_DIM), 1) < HEAD_DIM
        qq = q_ref[...]
        kk = jnp.concatenate([kp_ref[...], kc_ref[...]], axis=0)
        vv = jnp.concatenate([vp_ref[...], vc_ref[...]], axis=0)
        da = da_ref[...]
        prod = da * at_ref[...]
        da = da.astype(BF16)
        scale = HEAD_DIM ** -0.5
        for pair in range(N_HEADS // 2):
            sl = slice(2 * pair * HEAD_DIM, 2 * (pair + 1) * HEAD_DIM)
            q_p, k_p, v_p, da_p, prod_p = qq[:, sl], kk[:, sl], vv[:, sl], da[:, sl], prod[:, sl]
            dq, dk, dv = [], 0.0, 0.0
            for half in range(2):
                sel = even if half == 0 else jnp.logical_not(even)
                q_h = jnp.where(sel, q_p, jnp.zeros_like(q_p))
                da_h = jnp.where(sel, da_p, jnp.zeros_like(da_p))
                lse_col = lse_ref[:, (2 * pair + half) * HEAD_DIM:(2 * pair + half) * HEAD_DIM + 1]
                d_col = jnp.sum(jnp.where(sel, prod_p, 0.0), axis=-1, keepdims=True)
                s = _dot(q_h, k_p, "nt") * scale
                p = jnp.where(mask, jnp.exp(s - lse_col), 0.0)
                ds = p * (_dot(da_h, v_p, "nt") - d_col)
                dq.append(_dot(ds, k_p, "nn") * scale)
                dk = dk + _dot(ds, q_h, "tn") * scale
                dv = dv + _dot(p, da_h, "tn")
            dq_ref[:, sl] = jnp.where(even, dq[0], dq[1]).astype(dq_ref.dtype)
            dkb_ref[:, sl] = dk[:BAND].astype(dkb_ref.dtype)
            dka_ref[:, sl] = dk[BAND:].astype(dka_ref.dtype)
            dvb_ref[:, sl] = dv[:BAND].astype(dvb_ref.dtype)
            dva_ref[:, sl] = dv[BAND:].astype(dva_ref.dtype)

    outs = pl.pallas_call(
        body,
        name=f"attn_bwd_d{d}",
        grid=(d, nb),
        in_specs=[pl.BlockSpec(blk, cur), pl.BlockSpec(blk, prev), pl.BlockSpec(blk, cur),
                  pl.BlockSpec(blk, prev), pl.BlockSpec(blk, cur),
                  pl.BlockSpec(blk, cur), pl.BlockSpec(blk, cur), pl.BlockSpec(blk, cur)],
        out_specs=[pl.BlockSpec(blk, cur)] * 5,
        out_shape=[jax.ShapeDtypeStruct((n_str, d * D_ATTN), BF16)] * 5,
        compiler_params=_params(2),
    )(view(q), view(k), view(k), view(v), view(v), view(dattn), view(attn), view(lse))
    return list(outs)


def _rope_tables(positions):
    half = ROPE_DIM // 2
    inv_freq = ROPE_THETA ** (-jnp.arange(half, dtype=F32) * (2.0 / ROPE_DIM))
    ang = positions.astype(F32)[:, None] * inv_freq
    cos, sin = jnp.cos(ang), jnp.sin(ang)
    s_len = positions.shape[0]
    one = jnp.ones((s_len, HEAD_DIM - ROPE_DIM), F32)
    zero8 = jnp.zeros((s_len, half), F32)
    zero = jnp.zeros((s_len, HEAD_DIM - ROPE_DIM), F32)
    c = jnp.concatenate([cos, cos, one], axis=1)
    s1 = jnp.concatenate([zero8, sin, zero], axis=1)
    s2 = jnp.concatenate([-sin, zero8, zero], axis=1)
    tile = lambda t: jnp.tile(t, (1, N_HEADS))
    return tile(c), tile(s1), tile(s2)


def _rope(t, c, s1, s2):
    half = ROPE_DIM // 2
    return t * c + pltpu.roll(t, half, 1) * s1 + pltpu.roll(t, D_ATTN - half, 1) * s2


def _rope_transposed(dt, c, s1, s2):
    half = ROPE_DIM // 2
    return dt * c + pltpu.roll(dt * s1, D_ATTN - half, 1) + pltpu.roll(dt * s2, half, 1)


SCAN_ROWS = 256
SCAN_CHUNK = 128
LANES = 128
SUBLANES = 8


def _cmul(xr, xi, yr, yi):
    return xr * yr - xi * yi, xr * yi + xi * yr


def _scan(name, bu, a_cat, reverse):
    s_len = bu.shape[1]
    n_blocks = bu.shape[0] // 2
    n_chunks = s_len // SCAN_CHUNK
    log_chunk = SCAN_CHUNK.bit_length() - 1
    rows_of = lambda off: pl.ds(pl.multiple_of(off * n_chunks, n_chunks), n_chunks)

    def body(br_ref, bi_ref, ar_ref, ai_ref, xr_ref, xi_ref, pr_ref, pi_ref, cr_ref, ci_ref):
        ar, ai = ar_ref[...], ai_ref[...]

        def step(i, state):
            rows = rows_of((SCAN_CHUNK - 1 - i) if reverse else i)
            pr, pi = _cmul(ar, ai, *state)
            nr, ni = pr + br_ref[rows, :], pi + bi_ref[rows, :]
            xr_ref[rows, :] = nr
            xi_ref[rows, :] = ni
            return nr, ni

        zero = jnp.zeros((n_chunks, LANES), F32)
        lax.fori_loop(0, SCAN_CHUNK, step, (zero, zero), unroll=4)

        pw = [(ar, ai)]
        for _ in range(log_chunk + n_chunks.bit_length()):
            pw.append(_cmul(*pw[-1], *pw[-1]))

        last = pl.ds((0 if reverse else SCAN_CHUNK - 1) * n_chunks, n_chunks)
        er, ei = xr_ref[last, :], xi_ref[last, :]
        chunk = lax.broadcasted_iota(jnp.int32, (n_chunks, LANES), 0)

        def shifted(v, s):
            if reverse:
                return jnp.where(chunk < n_chunks - s, pltpu.roll(v, n_chunks - s, 0), 0.0)
            return jnp.where(chunk >= s, pltpu.roll(v, s, 0), 0.0)

        s, b = 1, log_chunk
        while s < n_chunks:
            mr, mi = _cmul(*pw[b], shifted(er, s), shifted(ei, s))
            er, ei = er + mr, ei + mi
            s, b = 2 * s, b + 1
        cr_ref[...] = shifted(er, 1)
        ci_ref[...] = shifted(ei, 1)

        step_no = lax.broadcasted_iota(jnp.int32, (SUBLANES, LANES), 0)
        expo = (SUBLANES - step_no) if reverse else (step_no + 1)
        qr, qi = jnp.ones((SUBLANES, LANES), F32), jnp.zeros((SUBLANES, LANES), F32)
        for bit in range(4):
            mr, mi = _cmul(qr, qi, *pw[bit])
            hit = (expo & (1 << bit)) != 0
            qr, qi = jnp.where(hit, mr, qr), jnp.where(hit, mi, qi)
        lo = SCAN_CHUNK - SUBLANES if reverse else 0
        pr_ref[lo:lo + SUBLANES, :] = qr
        pi_ref[lo:lo + SUBLANES, :] = qi
        m, b = SUBLANES, 3
        while m < SCAN_CHUNK:
            src = pl.ds(SCAN_CHUNK - m, m) if reverse else pl.ds(0, m)
            dst = pl.ds(SCAN_CHUNK - 2 * m, m) if reverse else pl.ds(m, m)
            mr, mi = _cmul(pr_ref[src, :], pi_ref[src, :], *pw[b])
            pr_ref[dst, :] = mr
            pi_ref[dst, :] = mi
            m, b = 2 * m, b + 1

        cr, ci = cr_ref[...], ci_ref[...]

        def fix(i, _):
            rows = rows_of(i)
            mr, mi = _cmul(pr_ref[pl.ds(i, 1), :], pi_ref[pl.ds(i, 1), :], cr, ci)
            xr_ref[rows, :] += mr
            xi_ref[rows, :] += mi
            return 0

        lax.fori_loop(0, SCAN_CHUNK, fix, 0, unroll=4)

    blk = lambda off: pl.BlockSpec((None, s_len, LANES), lambda c: (off + c, 0, 0))
    col = lambda off: pl.BlockSpec((1, LANES), lambda c: (0, off + c))
    return pl.pallas_call(
        body,
        name=name,
        grid=(n_blocks,),
        in_specs=[blk(0), blk(n_blocks), col(0), col(n_blocks)],
        out_specs=[blk(0), blk(0)],
        out_shape=[jax.ShapeDtypeStruct((n_blocks, s_len, LANES), F32)] * 2,
        scratch_shapes=[pltpu.VMEM((SCAN_CHUNK, LANES), F32)] * 2 + [pltpu.VMEM((n_chunks, LANES), F32)] * 2,
        compiler_params=_params(1),
    )(bu, bu, a_cat, a_cat)


def _to_chunk_order(t):
    s_len, width = t.shape
    return t.reshape(s_len // SCAN_CHUNK, SCAN_CHUNK, width).transpose(1, 0, 2).reshape(s_len, width)


def _from_chunk_order(t):
    s_len, width = t.shape
    return t.reshape(SCAN_CHUNK, s_len // SCAN_CHUNK, width).transpose(1, 0, 2).reshape(s_len, width)


def _state_in(arr, tm):
    return (arr, (arr.shape[0], tm, LANES), lambda i, *_: (0, i, 0))


def _state_out(s_len, width, tm):
    n = width // LANES
    return ((n, s_len, LANES), F32, (n, tm, LANES), lambda i, *_: (0, i, 0), "row")


@jax.custom_vjp
def _block_diag(blocks):
    g, r, c = blocks.shape
    eye = jnp.eye(g, dtype=blocks.dtype)
    return (blocks[:, :, None, :] * eye[:, None, :, None]).reshape(g * r, g * c)


def _block_diag_fwd(blocks):
    return _block_diag(blocks), blocks.shape


def _block_diag_bwd(shape, ct):
    g, r, c = shape
    per = max(LANES // c, 1)

    def body(ct_ref, out_ref):
        for k in range(per):
            out_ref[k] = ct_ref[k * r:(k + 1) * r, k * c:(k + 1) * c]

    return (pl.pallas_call(
        body,
        name=f"diag_blocks_{r}x{c}",
        grid=(g // per,),
        in_specs=[pl.BlockSpec((per * r, per * c), lambda i: (i, i))],
        out_specs=pl.BlockSpec((per, r, c), lambda i: (i, 0, 0)),
        out_shape=jax.ShapeDtypeStruct((g, r, c), ct.dtype),
        compiler_params=_params(1),
    )(ct),)


_block_diag.defvjp(_block_diag_fwd, _block_diag_bwd)


def _ssm_params(lam_re, lam_im, log_dt, b_re, b_im, c_re, c_im):
    dt = jnp.exp(log_dt)[:, None]
    er = jnp.exp(lam_re * dt)
    a_re = er * jnp.cos(lam_im * dt)
    a_im = er * jnp.sin(lam_im * dt)
    nr, ni = a_re - 1.0, a_im
    den = lam_re * lam_re + lam_im * lam_im
    fr = (nr * lam_re + ni * lam_im) / den
    fi = (ni * lam_re - nr * lam_im) / den
    bb_re = fr[..., None] * b_re - fi[..., None] * b_im
    bb_im = fr[..., None] * b_im + fi[..., None] * b_re
    in_mat = lambda bb: _block_diag(bb.transpose(0, 2, 1))
    out_mat = lambda cc: _block_diag(cc.transpose(0, 2, 1))
    w_b = (in_mat(bb_re), in_mat(bb_im))
    w_c = (out_mat(c_re), -out_mat(c_im))
    a_cat = jnp.concatenate([a_re.reshape(1, N_STATE), a_im.reshape(1, N_STATE)], axis=1)
    return a_cat, w_b, w_c


def _conj(a_cat):
    return jnp.concatenate([a_cat[:, :N_STATE], -a_cat[:, N_STATE:]], axis=1)


def _layer_fwd(s_len, h, n1, p_l, rope, weight, sp, gains, g_next):
    nt = s_len // TM
    sv = {}
    gw = {}

    def need(after, *names):
        for name in names:
            gw[name] = weight(name, after)

    def ffn_weight(which):
        def get(kind, after):
            need(after, f"w_{kind}{which}")
            return gw[f"w_{kind}{which}"]
        return get

    h1, a_in, sv["ffn1"] = _ffn_fwd(s_len, h, n1, ffn_weight(1), gains["ffn1_post_g"], gains["mix_pre_g"])

    need(h1, "w_in", "w_glu", "w_out")
    (proj,) = _mm("w_in", (nt, 1, 1), _rows(a_in, TM), _resident(gw["w_in"]), "nn",
                  [_row_out(s_len, 2 * D_MODEL, F32, TM)])

    c, s1, s2 = rope

    def rope_fn(tq, tk, tv, cc, a1, a2, scr):
        q, k = _rope(tq, cc, a1, a2), _rope(tk, cc, a1, a2)
        return [_to_streams(t, d, scr) for d in DILATIONS for t in (q, k, tv)]

    qkv = _rowwise(
        "rope", s_len, TM,
        [_rows(proj, TM, 0, D_ATTN), _rows(proj, TM, 1, D_ATTN), _rows(proj, TM, 2, D_ATTN),
         _rows(c, TM), _rows(s1, TM), _rows(s2, TM)],
        [_stream_out(s_len, BF16, TM, d) for d in DILATIONS for _ in range(3)], rope_fn,
        scratch=[_stream_scratch(TM)])
    qkv = {d: qkv[3 * i:3 * i + 3] for i, d in enumerate(DILATIONS)}
    parts = {d: _attn_fwd(s_len, *qkv[d], d) for d in DILATIONS}

    def mix_fn(*args):
        g, scr = args[6], args[7]
        o1, l1, o2, l2, o3, l3 = [_from_streams(args[2 * i + j], d, scr) for i, d in enumerate(DILATIONS)
                                  for j in range(2)]
        m = jnp.maximum(jnp.maximum(l1, l2), l3)
        e1, e2, e3 = jnp.exp(l1 - m), jnp.exp(l2 - m), jnp.exp(l3 - m)
        tot = e1 + e2 + e3
        attn = (e1 * o1 + e2 * o2 + e3 * o3) / tot
        lse = m + jnp.log(tot)
        return [attn, lse, _rms(attn, g)] + [_to_streams(t, d, scr) for d in DILATIONS[1:] for t in (attn, lse)]

    mix_out = _rowwise(
        "attn_mix", s_len, TM,
        [_stream_in(t, TM, d) for d in DILATIONS for t in parts[d]] + [_whole(gains["attn_norm_g"])],
        [_row_out(s_len, D_ATTN, F32, TM), _row_out(s_len, D_ATTN, F32, TM),
         _row_out(s_len, D_ATTN, BF16, TM, col=0, total=D_MODEL)]
        + [_stream_out(s_len, F32, TM, d) for d in DILATIONS[1:] for _ in range(2)], mix_fn,
        scratch=[_stream_scratch(TM)])
    attn, lse, mixed_half = mix_out[:3]
    attn_s = {1: (attn, lse), DILATIONS[1]: tuple(mix_out[3:5]), DILATIONS[2]: tuple(mix_out[5:7])}

    a_cat, w_b, w_c, d_vec = sp
    ts = SCAN_ROWS
    u_c = _to_chunk_order(proj[:, 3 * D_ATTN:])
    (bu,) = _mm("ssm_bu", (s_len // ts, 1, 1), _rows(u_c, ts), _whole(w_b), "nn",
                [_state_out(s_len, 2 * N_STATE, ts)], epilogue=lambda acc: (_lane_blocks(acc),))
    xs = _scan("ssm_scan", bu, a_cat, False)
    w_c_half = lambda part: (w_c, (N_STATE, D_SSM), lambda *_: (part, 0))
    (z_c,) = _mm("ssm_y", (s_len // ts, 1, 1), _state_in(xs[0], ts), w_c_half(0), "nn",
                 [_row_out(s_len, D_SSM, F32, ts)], more=[(_state_in(xs[1], ts), w_c_half(1))], a_prep=_join_lanes,
                 extras=[_rows(u_c, ts), _whole(d_vec)], epilogue=lambda acc, u, dv: (acc + dv * u,))
    z = _from_chunk_order(z_c)

    def glu_epi(acc, z_blk, b, g):
        y = _gelu(z_blk)
        t = acc + b
        ssm = y * _sigmoid(t)
        return y, t, ssm, _rms(ssm, g)

    yg, t_glu, ssm, mixed = _mm(
        "ssm_glu", (nt, 1, 1), _rows(z, TM), _whole(gw["w_glu"]), "nn",
        [_row_out(s_len, D_SSM, F32, TM), _row_out(s_len, D_SSM, F32, TM), _row_out(s_len, D_SSM, F32, TM),
         _row_out(s_len, D_SSM, BF16, TM, col=1, total=D_MODEL)], a_prep=_gelu,
        extras=[_rows(z, TM), _whole(gains["ssm_b_glu"]), _whole(gains["ssm_norm_g"])], epilogue=glu_epi,
        into=[(mixed_half, 3)])

    def out_epi(acc, h_blk, gp, gn):
        h_new = h_blk + _rms(acc, gp)
        return acc, h_new, _rms(h_new, gn)

    o, h2, n2 = _mm(
        "w_out", (nt, 1, 1), _rows(mixed, TM), _whole(gw["w_out"]), "nn",
        [_row_out(s_len, D_MODEL, F32, TM), _row_out(s_len, D_MODEL, F32, TM), _row_out(s_len, D_MODEL, BF16, TM)],
        extras=[_rows(h1, TM), _whole(gains["mix_post_g"]), _whole(gains["ffn2_pre_g"])], epilogue=out_epi)

    h3, _, sv["ffn2"] = _ffn_fwd(s_len, h2, n2, ffn_weight(2), gains["ffn2_post_g"], gains["ffn2_post_g"])
    need(h3, "w_pu", "ple_w_gate")

    (pu,) = _mm("ple_up", (nt, 1, 1), _rows(p_l, TM), _resident(gw["w_pu"]), "nn",
                [_row_out(s_len, D_MODEL, F32, TM)])

    def ple_epi(acc, pu_blk, h_blk, gp, gn):
        h_new = h_blk + _rms(pu_blk * _sigmoid(acc), gp)
        return acc, h_new, _rms(h_new, gn)

    gt, h4, n_next = _mm(
        "ple_gate", (nt, 1, 1), _rows(h3, TM), _whole(gw["ple_w_gate"]), "nn",
        [_row_out(s_len, D_MODEL, F32, TM), _row_out(s_len, D_MODEL, F32, TM), _row_out(s_len, D_MODEL, BF16, TM)],
        extras=[_rows(pu, TM), _rows(h3, TM), _whole(gains["ple_post_g"]), _whole(g_next)], epilogue=ple_epi)

    sv.update(h1=h1, a_in=a_in, proj=proj, qkv=qkv, attn=attn, attn_s=attn_s, xs=xs, u_c=u_c, z=z, yg=yg, t_glu=t_glu,
              ssm=ssm,
              mixed=mixed, o=o, h2=h2, h3=h3, pu=pu, gt=gt, p_l=p_l, w=gw)
    return h4, n_next, sv


def _vjp(fn, args, cot):
    _, pull = jax.vjp(fn, *args)
    return pull(cot)


def _layer_bwd(s_len, dh4, sv, rope, gw, sp, gains, on_partial=None):
    nt = s_len // TM
    gr = {}
    wg = {}

    def ple_fn(dh, pu, gt, g):
        dpu, dgt, dg = _vjp(lambda a, b, c: _rms(a * _sigmoid(b), c), (pu, gt, g), dh)
        return dpu, dgt, dg

    dpu, dgt, gr["ple_post_g"] = _rowwise(
        "ple_bwd", s_len, TM, [_rows(dh4, TM), _rows(sv["pu"], TM), _rows(sv["gt"], TM), _whole(gains["ple_post_g"])],
        [_row_out(s_len, D_MODEL, BF16, TM), _row_out(s_len, D_MODEL, BF16, TM), _col_out(D_MODEL)], ple_fn)

    def square_w_grad(name, lhs, rhs):
        half = D_MODEL // 2
        (dw,) = _mm(name, (2, 1, nt), (lhs, (TM, half), lambda i, j, k: (k, i)),
                    (rhs, (TM, D_MODEL), lambda i, j, k: (k, 0)), "tn",
                    [((D_MODEL, D_MODEL), BF16, (half, D_MODEL), lambda i, j, k: (i, 0), "row")],
                    acc_shape=(half, D_MODEL))
        return dw

    (wg["ple_w_up"],) = _mm(
        "ple_bwd_wup", (1, 1, nt), (sv["p_l"], (TM, PLE_DIM), lambda i, j, k: (k, 0)),
        (dpu, (TM, D_MODEL), lambda i, j, k: (k, 0)), "tn",
        [((N_DEV, PLE_DIM, D_MODEL // N_DEV), BF16, (N_DEV, PLE_DIM, D_MODEL // N_DEV), lambda i, j, k: (0, 0, 0),
          "row")],
        epilogue=lambda acc: (_column_shards(acc, D_MODEL // N_DEV),), acc_shape=(PLE_DIM, D_MODEL))
    wg["ple_w_gate"] = square_w_grad("ple_bwd_wgate", sv["h3"], dgt)

    def ple_dx_epi(acc, dh, f, g):
        dh3 = dh + acc
        df, dg = _vjp(_rms, (f, g), 0.5 * dh3)
        return dh3, df, dg

    dh3, df2, gr["ffn2_post_g"] = _mm(
        "ple_bwd_dx", (nt, 1, 1), _rows(dgt, TM), _whole(gw["ple_w_gate"]), "nt",
        [_row_out(s_len, D_MODEL, F32, TM), _row_out(s_len, D_MODEL, BF16, TM), _col_out(D_MODEL)],
        extras=[_rows(dh4, TM), _rows(sv["ffn2"]["f"], TM), _whole(gains["ffn2_post_g"])], epilogue=ple_dx_epi)

    def ffn2_final(dn, dh, h, g_pre, o, g_post):
        dx, dg_pre = _vjp(_rms, (h, g_pre), dn)
        dh2 = dh + dx
        do, dg_post = _vjp(_rms, (o, g_post), dh2)
        return dh2, do, dg_pre, dg_post

    wg["ffn2_w_gate"], wg["ffn2_w_up"], wg["ffn2_w_down"], (dh2, do, gr["ffn2_pre_g"], gr["mix_post_g"]) = _ffn_bwd(
        s_len, sv["ffn2"], df2, gw["w_gu2"], gw["w_d2"], ffn2_final,
        [_rows(dh3, FFN_TM), _rows(sv["h2"], FFN_TM), _whole(gains["ffn2_pre_g"]), _rows(sv["o"], FFN_TM),
         _whole(gains["mix_post_g"])],
        [_row_out(s_len, D_MODEL, F32, FFN_TM), _row_out(s_len, D_MODEL, BF16, FFN_TM), _col_out(D_MODEL),
         _col_out(D_MODEL)])

    wg["w_out"] = square_w_grad("w_out_bwd_w", sv["mixed"], do)

    def mixed_epi(acc, attn, ssm, yg, t, g_a, g_s, scr):
        dattn, dg_a = _vjp(_rms, (attn, g_a), acc[:, :D_ATTN])
        dssm, dg_s = _vjp(_rms, (ssm, g_s), acc[:, D_ATTN:])
        sg = _sigmoid(t)
        dt = dssm * yg * sg * (1.0 - sg)
        return ([_to_streams(dattn, d, scr) for d in DILATIONS]
                + [dt, dssm * sg, dg_a, dg_s, jnp.sum(dt, axis=0, keepdims=True)])

    res = _mm(
        "w_out_bwd_x", (nt, 1, 1), _rows(do, TM), _whole(gw["w_out"]), "nt",
        [_stream_out(s_len, F32, TM, d) for d in DILATIONS]
        + [_row_out(s_len, D_SSM, BF16, TM), _row_out(s_len, D_SSM, F32, TM),
           _col_out(D_ATTN), _col_out(D_SSM), _col_out(D_SSM)],
        extras=[_rows(sv["attn"], TM), _rows(sv["ssm"], TM), _rows(sv["yg"], TM), _rows(sv["t_glu"], TM),
                _whole(gains["attn_norm_g"]), _whole(gains["ssm_norm_g"])], epilogue=mixed_epi,
        scratch=[_stream_scratch(TM)])
    dattn_s = dict(zip(DILATIONS, res[:3]))
    dt_glu, dyg_dir, gr["attn_norm_g"], gr["ssm_norm_g"], gr["ssm_b_glu"] = res[3:]

    a_cat, w_b, w_c, d_vec = sp
    ts = SCAN_ROWS
    u_spec = (sv["proj"], (TM, D_SSM), lambda i, *_: (i, 3))
    (wg["ssm_w_glu"],) = _mm(
        "ssm_bwd_wglu", (1, 1, nt), (sv["yg"], (TM, D_SSM), lambda i, j, k: (k, 0)),
        (dt_glu, (TM, D_SSM), lambda i, j, k: (k, 0)), "tn",
        [((D_SSM, D_SSM), BF16, (D_SSM, D_SSM), lambda i, j, k: (0, 0), "row")], acc_shape=(D_SSM, D_SSM))

    def gelu_epi(acc, dy_dir, z, u, dv):
        (dz,) = _vjp(_gelu, (z,), acc + dy_dir)
        return dz, dz * dv, jnp.sum(dz * u, axis=0, keepdims=True)

    dz, du_dir, gr["ssm_d"] = _mm(
        "ssm_bwd_glu", (nt, 1, 1), _rows(dt_glu, TM), _whole(gw["w_glu"]), "nt",
        [_row_out(s_len, D_SSM, BF16, TM), _row_out(s_len, D_SSM, F32, TM), _col_out(D_SSM)],
        extras=[_rows(dyg_dir, TM), _rows(sv["z"], TM), u_spec, _whole(d_vec)], epilogue=gelu_epi)

    x_re, x_im = sv["xs"]
    w_c_half = lambda part: (w_c, (N_STATE, D_SSM), lambda *_: (part, 0))
    w_b_half = lambda part: (w_b, (D_SSM, N_STATE), lambda *_: (0, part))

    n_lb = N_STATE // LANES
    n_chunks = s_len // SCAN_CHUNK
    dz_c = _to_chunk_order(dz)

    def w_c_grad(name, x):
        per = 1024 // LANES
        (dw,) = _mm(name, (N_STATE // 1024, 1, nt), (x, (per, TM, LANES), lambda i, j, k: (i, k, 0)),
                    (dz_c, (TM, D_SSM), lambda i, j, k: (k, 0)), "tn",
                    [((N_STATE, D_SSM), F32, (1024, D_SSM), lambda i, j, k: (i, 0), "row")], acc_shape=(1024, D_SSM),
                    a_prep=_join_lanes)
        return dw

    d_w_c = (w_c_grad("ssm_bwd_wc_re", x_re), w_c_grad("ssm_bwd_wc_im", x_im))
    (dxs,) = _mm("ssm_bwd_dx", (s_len // ts, 1, 1), _rows(dz_c, ts), _whole(w_c), "nt",
                 [_state_out(s_len, 2 * N_STATE, ts)], epilogue=lambda acc: (_lane_blocks(acc),))
    g_re, g_im = _scan("ssm_scan_rev", dxs, _conj(a_cat), True)

    def da_fn(xr, xi, xr_before, xi_before, g_r, g_i):
        i = pl.program_id(0)
        chunk = lax.broadcasted_iota(jnp.int32, xr_before.shape, 1)

        def previous(x, x_before):
            wrapped = jnp.where(chunk == 0, 0.0, pltpu.roll(x_before, 1, 1))
            head = jnp.where(i == 0, wrapped, x_before)
            return jnp.concatenate([head, x[:, :ts - n_chunks, :]], axis=1)

        pr, pi = previous(xr, xr_before), previous(xi, xi_before)
        return (jnp.sum(pr * g_r + pi * g_i, axis=1, keepdims=True),
                jnp.sum(pr * g_i - pi * g_r, axis=1, keepdims=True))

    steps = s_len // n_chunks
    before = lambda t: (t, (n_lb, n_chunks, LANES), lambda i: (0, (i * (ts // n_chunks) + steps - 1) % steps, 0))
    lane_sum = ((n_lb, 1, LANES), F32, (n_lb, 1, LANES), lambda *_: (0, 0, 0), "colsum")
    d_a = jnp.concatenate([t.reshape(1, N_STATE) for t in _rowwise(
        "ssm_bwd_da", s_len, ts,
        [_state_in(x_re, ts), _state_in(x_im, ts), before(x_re), before(x_im), _state_in(g_re, ts),
         _state_in(g_im, ts)], [lane_sum] * 2, da_fn)], axis=1)

    def w_b_grad(name, g):
        (dw,) = _mm(name, (1, 1, nt), (sv["u_c"], (TM, D_SSM), lambda i, j, k: (k, 0)),
                    (g, (n_lb, TM, LANES), lambda i, j, k: (0, k, 0)), "tn",
                    [((D_SSM, N_STATE), F32, (D_SSM, N_STATE), lambda i, j, k: (0, 0), "row")],
                    acc_shape=(D_SSM, N_STATE), b_prep=_join_lanes)
        return dw

    d_w_b = (w_b_grad("ssm_bwd_wb_re", g_re), w_b_grad("ssm_bwd_wb_im", g_im))
    (du_c,) = _mm("ssm_bwd_du", (s_len // ts, 1, 1), _state_in(g_re, ts), w_b_half(0), "nt",
                  [_row_out(s_len, D_SSM, F32, ts)], more=[(_state_in(g_im, ts), w_b_half(1))], a_prep=_join_lanes)
    du = (_from_chunk_order(du_c) + du_dir).astype(BF16)
    sum_tm = 2 * BAND
    n_sum = s_len // sum_tm
    ins = []
    for d in DILATIONS:
        dq_p, dka, dkb, dva, dvb = _attn_bwd(s_len, *sv["qkv"][d], dattn_s[d], *sv["attn_s"][d], d)
        if d == 1:
            nxt = lambda t: (t, (sum_tm, D_ATTN), lambda i: (jnp.minimum(i + 1, n_sum - 1), 0))
            ins += [_rows(dq_p, sum_tm), _rows(dka, sum_tm), _rows(dkb, sum_tm), nxt(dkb), _rows(dva, sum_tm),
                    _rows(dvb, sum_tm), nxt(dvb)]
        else:
            ins += [_stream_in(dq_p, sum_tm, d), _stream_in(dka, sum_tm, d), _stream_in(dkb, sum_tm, d, BAND, n_sum),
                    _stream_in(dva, sum_tm, d), _stream_in(dvb, sum_tm, d, BAND, n_sum)]
    c, s1, s2 = rope
    ins += [_rows(c, sum_tm), _rows(s1, sum_tm), _rows(s2, sum_tm), _rows(du, sum_tm)]

    def qkv_fn(*args):
        i = pl.program_id(0)
        blocks, scr = [b.astype(F32) for b in args[:-1]], args[-1]
        dq_t, dka, dkb, dkb_next, dva, dvb, dvb_next = blocks[:7]
        more = i + 1 < n_sum
        ahead = lambda cur, nxt: jnp.concatenate([cur[BAND:], jnp.where(more, nxt[:BAND], 0.0)], axis=0)
        dk_t = dka + ahead(dkb, dkb_next)
        dv_t = dva + ahead(dvb, dvb_next)
        at = 7
        for d in DILATIONS[1:]:
            dq_p, dka, dkb, dva, dvb = blocks[at:at + 5]
            at += 5
            live = i + BAND // (sum_tm // d) < n_sum
            dq_t = dq_t + _from_streams(dq_p, d, scr)
            dk_t = dk_t + _from_streams(dka + jnp.where(live, dkb, 0.0), d, scr)
            dv_t = dv_t + _from_streams(dva + jnp.where(live, dvb, 0.0), d, scr)
        cc, a1, a2, du_blk = blocks[at:at + 4]
        return (jnp.concatenate([_rope_transposed(dq_t, cc, a1, a2), _rope_transposed(dk_t, cc, a1, a2), dv_t,
                                 du_blk.astype(F32)], axis=1),)

    (dproj,) = _rowwise("attn_bwd_sum", s_len, sum_tm, ins, [_row_out(s_len, 2 * D_MODEL, BF16, sum_tm)], qkv_fn,
                        scratch=[_stream_scratch(sum_tm)])

    (wg["w_in"],) = _mm(
        "w_in_bwd_w", (1, 2, nt), (sv["a_in"], (TM, D_MODEL), lambda i, j, k: (k, 0)),
        (dproj, (TM, D_MODEL), lambda i, j, k: (k, j)), "tn",
        [((N_DEV, D_MODEL, 2 * D_MODEL // N_DEV), BF16, (N_DEV // 2, D_MODEL, 2 * D_MODEL // N_DEV),
          lambda i, j, k: (j, 0, 0), "row")],
        epilogue=lambda acc: (_column_shards(acc, 2 * D_MODEL // N_DEV),), acc_shape=(D_MODEL, D_MODEL))

    tie = on_partial(wg) if on_partial is not None else None
    mix_pre_g = gains["mix_pre_g"] if tie is None else gains["mix_pre_g"] + tie

    def in_epi(acc, dh, h, g_pre, f, g_post):
        dx, dg_pre = _vjp(_rms, (h, g_pre), acc)
        dh1 = dh + dx
        df, dg_post = _vjp(_rms, (f, g_post), 0.5 * dh1)
        return dh1, df, dg_pre, dg_post

    dh1, df1, gr["mix_pre_g"], gr["ffn1_post_g"] = _mm(
        "w_in_bwd_x", (nt, 1, 1), _rows(dproj, TM), _resident(gw["w_in"]), "nt",
        [_row_out(s_len, D_MODEL, F32, TM), _row_out(s_len, D_MODEL, BF16, TM), _col_out(D_MODEL), _col_out(D_MODEL)],
        extras=[_rows(dh2, TM), _rows(sv["h1"], TM), _whole(mix_pre_g), _rows(sv["ffn1"]["f"], TM),
                _whole(gains["ffn1_post_g"])], epilogue=in_epi)

    def ffn1_final(dn, dh, h, g_pre):
        dx, dg_pre = _vjp(_rms, (h, g_pre), dn)
        return dh + dx, dg_pre

    wg["ffn1_w_gate"], wg["ffn1_w_up"], wg["ffn1_w_down"], (dh0, gr["ffn1_pre_g"]) = _ffn_bwd(
        s_len, sv["ffn1"], df1, gw["w_gu1"], gw["w_d1"], ffn1_final,
        [_rows(dh1, FFN_TM), _rows(sv["ffn1"]["h"], FFN_TM), _whole(gains["ffn1_pre_g"])],
        [_row_out(s_len, D_MODEL, F32, FFN_TM), _col_out(D_MODEL)])

    return dh0, wg, gr, (d_a, d_w_b, d_w_c)


def _peers():
    x, y, c = lax.axis_index("x"), lax.axis_index("y"), lax.axis_index("c")
    me = 4 * x + 2 * y + c
    peers = []
    for k in range(1, N_DEV):
        kx, ky, kc = (k >> 2) & 1, (k >> 1) & 1, k & 1
        px, py, pc = x ^ kx, y ^ ky, c ^ kc
        peers.append(((px, py, pc), 4 * px + 2 * py + pc))
    return me, peers


_HBM_SPEC = pl.BlockSpec(memory_space=pltpu.HBM)
_SEM_SPEC = pl.BlockSpec(memory_space=pltpu.SEMAPHORE)
_DATAFLOW = pltpu.SideEffectType.DATAFLOW_SIDE_EFFECTING


def _device_index():
    return 4 * lax.axis_index("x") + 2 * lax.axis_index("y") + lax.axis_index("c")


def _landing(arrays, scatter):
    me = _device_index()
    out = []
    for a, scattered in zip(arrays, scatter):
        own = lax.dynamic_index_in_dim(a, me, 0, keepdims=True) if scattered else a[None]
        buf = lax.empty((N_DEV,) + own.shape[1:], a.dtype)
        out.append(lax.dynamic_update_slice_in_dim(buf, own, me, 0))
    return out


def _split_copies(src_refs, land_refs, send_sems, recv_sems, scatter):
    me, peers = _peers()
    pairs = []
    for t in range(len(src_refs)):
        for k, (peer, peer_id) in enumerate(peers):
            src = src_refs[t].at[peer_id] if scatter[t] else src_refs[t]
            sem = t * (N_DEV - 1) + k
            mk = lambda slot, src=src, t=t, sem=sem, peer=peer: pltpu.make_async_remote_copy(
                src_ref=src, dst_ref=land_refs[t].at[slot], send_sem=send_sems.at[sem], recv_sem=recv_sems.at[sem],
                device_id=peer, device_id_type=pl.DeviceIdType.MESH)
            pairs.append((functools.partial(mk, me), functools.partial(mk, peer_id)))
    return pairs


def _exchange_start(name, arrays, scatter, order_after):
    n = len(arrays)
    landing = _landing(arrays, scatter)

    def body(*refs):
        src_refs, land_refs = refs[:n], refs[n:2 * n]
        send_sems, recv_sems = refs[2 * n + 1], refs[2 * n + 2]
        token_ref = refs[-1]
        for outgoing, _ in _split_copies(src_refs, land_refs, send_sems, recv_sems, scatter):
            outgoing().start()
        token_ref[...] = jnp.zeros_like(token_ref)

    sem_shape = pltpu.SemaphoreType.DMA((n * (N_DEV - 1),))
    thru = [pltpu.HBM(a.shape, a.dtype) for a in list(arrays) + landing]
    hbm = lambda t: pltpu.with_memory_space_constraint(t, pltpu.HBM)
    res = pl.pallas_call(
        body,
        name=name,
        in_specs=[_HBM_SPEC] * (2 * n) + [pl.BlockSpec(memory_space=pl.ANY)],
        out_specs=[_SEM_SPEC, _SEM_SPEC] + [_HBM_SPEC] * (2 * n) + [pl.BlockSpec(memory_space=pltpu.VMEM)],
        out_shape=[sem_shape, sem_shape] + thru + [jax.ShapeDtypeStruct((8, 128), F32)],
        input_output_aliases={i: 2 + i for i in range(2 * n)},
        compiler_params=pltpu.CompilerParams(has_side_effects=_DATAFLOW),
    )(*[hbm(t) for t in list(arrays) + landing], order_after)
    return (res[0], res[1], res[2:2 + n], res[2 + n:2 + 2 * n]), res[-1]


def _exchange_wait(name, handle, scatter, order_after):
    send_sems, recv_sems, sources, landing = handle
    n = len(sources)

    def body(*refs):
        src_refs, land_refs = refs[:n], refs[n:2 * n]
        for outgoing, arrival in _split_copies(src_refs, land_refs, refs[2 * n], refs[2 * n + 1], scatter):
            outgoing().wait_send()
            arrival().wait_recv()

    thru = [pltpu.HBM(a.shape, a.dtype) for a in list(sources) + list(landing)]
    res = pl.pallas_call(
        body,
        name=name,
        in_specs=[_HBM_SPEC] * (2 * n) + [_SEM_SPEC, _SEM_SPEC, pl.BlockSpec(memory_space=pl.ANY)],
        out_specs=[_HBM_SPEC] * (2 * n),
        out_shape=thru,
        input_output_aliases={i: i for i in range(2 * n)},
        compiler_params=pltpu.CompilerParams(has_side_effects=_DATAFLOW),
    )(*sources, *landing, send_sems, recv_sems, order_after)
    return list(res[n:])


def _adam_math(g, w, m, v):
    m = ADAM_B1 * m + (1.0 - ADAM_B1) * g
    v = ADAM_B2 * v + (1.0 - ADAM_B2) * (g * g)
    m_hat = m / (1.0 - ADAM_B1 ** ADAM_STEP)
    v_hat = v / (1.0 - ADAM_B2 ** ADAM_STEP)
    delta = -ADAM_LR * (m_hat / (jnp.sqrt(v_hat) + ADAM_EPS) + ADAM_WD * w)
    return delta, m, v


def _adamw(name, recv, recv_block, recv_map, w, m, v, tr, layer=None, prev=None, after=None):
    def fn(r, wb, mb, vb, *token):
        g = r[0].astype(F32)
        for s in range(1, N_DEV):
            g = g + r[s].astype(F32)
        return (g,) + _adam_math(g, wb, mb, vb) + tuple(jnp.zeros_like(t) for t in token)

    rows, cols = w.shape[-2:]
    if layer is None:
        spec = lambda t: _rows(t, tr)
        out = _row_out(rows, cols, F32, tr)
    else:
        spec = lambda t: (t, (None, tr, cols), lambda i: (layer, i, 0))
        out = (w.shape, F32, (None, tr, cols), lambda i: (layer, i, 0), "row")
    ins = [(recv, recv_block, recv_map), spec(w), spec(m), spec(v)]
    outs = [out] * 4
    if after is not None:
        ins.append(_whole(after))
        outs.append((after.shape, F32, after.shape, lambda i: (0, 0), "row"))
    into = [] if prev is None else [(t, k) for k, t in enumerate(prev)]
    return _rowwise(name, rows, tr, ins, outs, fn, into=into)


def _pack_small(tensors):
    flat = jnp.concatenate([tensors[n].reshape(-1).astype(F32) for n in SMALL])
    padded = -(-flat.shape[0] // SMALL_PAD) * SMALL_PAD
    return jnp.pad(flat, (0, padded - flat.shape[0])).reshape(padded // SMALL_LANES, SMALL_LANES)


def _local_step(x, p, positions, loss_target, w, layer_weights, layer_done=None, layer_partial=None):
    s_len = x.shape[0]
    depth = p.shape[0]
    rope = _rope_tables(positions)
    gains = [{n: w[n][l].reshape(1, -1) for n in SMALL if w[n].ndim == 2 and n != "ssm_log_dt"} for l in range(depth)]
    ssm_args = lambda l: tuple(w[n][l] for n in ("ssm_lam_re", "ssm_lam_im", "ssm_log_dt", "ssm_b_re", "ssm_b_im",
                                                  "ssm_c_re", "ssm_c_im"))
    sps, pulls = [], []
    for l in range(depth):
        (a_cat, w_b, w_c), pull = jax.vjp(_ssm_params, *ssm_args(l))
        w_b = jnp.concatenate([t.astype(BF16) for t in w_b], axis=1)
        w_c = jnp.concatenate([t.astype(BF16) for t in w_c], axis=0)
        sps.append((a_cat, w_b, w_c, w["ssm_d"][l].reshape(1, D_SSM)))
        pulls.append(pull)

    (n,) = _rowwise("pre_norm", s_len, TM, [_rows(x, TM), _whole(gains[0]["ffn1_pre_g"])],
                    [_row_out(s_len, D_MODEL, BF16, TM)], lambda a, g: (_rms(a, g),))
    h = x
    saved = []
    for l in range(depth):
        g_next = gains[l + 1]["ffn1_pre_g"] if l + 1 < depth else gains[l]["ffn1_pre_g"]
        h, n, sv = _layer_fwd(s_len, h, n, p[l], rope, layer_weights(l, h), sps[l], gains[l], g_next)
        saved.append(sv)

    def loss_fn(y, t):
        e = y - t
        return e * (1.0 / D_MODEL), jnp.sum(e * e, axis=0, keepdims=True)

    dh, sq = _rowwise("loss", s_len, TM, [_rows(h, TM), _rows(loss_target, TM)],
                      [_row_out(s_len, D_MODEL, F32, TM), _col_out(D_MODEL)], loss_fn)
    loss = 0.5 * jnp.sum(sq) / D_MODEL

    w_grads, small_grads = [None] * depth, [None] * depth
    tie = None
    for l in reversed(range(depth)):
        g_l = gains[l] if tie is None else dict(gains[l], ple_post_g=gains[l]["ple_post_g"] + tie)
        partial = None if layer_partial is None else functools.partial(layer_partial, l)
        dh, wg, gr, (d_a, d_w_b, d_w_c) = _layer_bwd(s_len, dh, saved[l], rope, saved[l]["w"], sps[l], g_l, partial)
        d_lre, d_lim, d_dt, d_bre, d_bim, d_cre, d_cim = pulls[l]((d_a, d_w_b, d_w_c))
        gr.update(ssm_lam_re=d_lre, ssm_lam_im=d_lim, ssm_log_dt=d_dt, ssm_b_re=d_bre, ssm_b_im=d_bim,
                  ssm_c_re=d_cre, ssm_c_im=d_cim)
        gr = {n: g.reshape(w[n].shape[1:]) for n, g in gr.items()}
        w_grads[l], small_grads[l] = wg, gr
        if layer_done is not None:
            tie = layer_done(l, wg, gr, dh)
    return loss, dh, w_grads, small_grads


GROUP_OF = {"ffn1_w_gate": "g1", "ffn1_w_up": "u1", "ffn2_w_gate": "g2", "ffn2_w_up": "u2", "ffn1_w_down": "d1",
            "ffn2_w_down": "d2", "w_out": "out", "ple_w_gate": "pg", "w_in": "in", "ple_w_up": "pu", "ssm_w_glu": "glu"}
GROUPS = tuple(GROUP_OF[n] for n in SHARDED)
FIRST_PARTS = (("g1", "u1"), ("d1",), ("in", "glu", "out"), ("g2", "u2", "d2", "pu", "pg"))
_COLUMN_VIEWS = {"w_gu1": ("g1", "u1"), "w_gu2": ("g2", "u2"), "w_in": ("in",), "w_pu": ("pu",)}
_ROW_VIEWS = {"d1": "w_d1", "d2": "w_d2", "out": "w_out", "pg": "ple_w_gate", "glu": "w_glu"}


def _shard_groups(w, l):
    return {group: w[n][l].astype(BF16) for n, group in GROUP_OF.items()}


def _gathered_views(gathered, after=None):
    views = {}
    for name, groups in _COLUMN_VIEWS.items():
        if all(g in gathered for g in groups):
            views[name] = _natural_cols("relayout_" + name, [gathered[g] for g in groups], after)
            after = None
    for group, name in _ROW_VIEWS.items():
        if group in gathered:
            g = gathered[group]
            views[name] = g.reshape(N_DEV * g.shape[1], g.shape[2])
    return views


def _grad_groups(wg):
    rows = lambda t: t if t.ndim == 3 else t.reshape(N_DEV, t.shape[0] // N_DEV, t.shape[1])
    return {GROUP_OF[n]: rows(t) for n, t in wg.items()}


def _update_sharded(name, recv_groups, w, m, v, layer, prev, after=None):
    recv = recv_groups[GROUP_OF[name]]
    rows, cols = w.shape[1:]
    tr = rows if rows <= 256 else (rows // 2 if rows % 256 else 256)
    return _adamw(f"adamw_{name}", recv, (N_DEV, tr, cols), lambda i: (0, i, 0), w, m, v, tr, layer, prev, after)


def kernel(x, p, positions, ffn1_pre_g, ffn1_w_gate, ffn1_w_up, ffn1_w_down, ffn1_post_g, mix_pre_g, w_in, attn_norm_g, ssm_lam_re, ssm_lam_im, ssm_log_dt, ssm_b_re, ssm_b_im, ssm_c_re, ssm_c_im, ssm_d, ssm_w_glu, ssm_b_glu, ssm_norm_g, w_out, mix_post_g, ffn2_pre_g, ffn2_w_gate, ffn2_w_up, ffn2_w_down, ffn2_post_g, ple_w_up, ple_w_gate, ple_post_g, loss_target, m_ffn1_pre_g, m_ffn1_w_gate, m_ffn1_w_up, m_ffn1_w_down, m_ffn1_post_g, m_mix_pre_g, m_w_in, m_attn_norm_g, m_ssm_lam_re, m_ssm_lam_im, m_ssm_log_dt, m_ssm_b_re, m_ssm_b_im, m_ssm_c_re, m_ssm_c_im, m_ssm_d, m_ssm_w_glu, m_ssm_b_glu, m_ssm_norm_g, m_w_out, m_mix_post_g, m_ffn2_pre_g, m_ffn2_w_gate, m_ffn2_w_up, m_ffn2_w_down, m_ffn2_post_g, m_ple_w_up, m_ple_w_gate, m_ple_post_g, v_ffn1_pre_g, v_ffn1_w_gate, v_ffn1_w_up, v_ffn1_w_down, v_ffn1_post_g, v_mix_pre_g, v_w_in, v_attn_norm_g, v_ssm_lam_re, v_ssm_lam_im, v_ssm_log_dt, v_ssm_b_re, v_ssm_b_im, v_ssm_c_re, v_ssm_c_im, v_ssm_d, v_ssm_w_glu, v_ssm_b_glu, v_ssm_norm_g, v_w_out, v_mix_post_g, v_ffn2_pre_g, v_ffn2_w_gate, v_ffn2_w_up, v_ffn2_w_down, v_ffn2_post_g, v_ple_w_up, v_ple_w_gate, v_ple_post_g):
    args = dict(locals())
    w = {n: args[n] for n in WEIGHTS}
    mom = {n: args["m_" + n] for n in WEIGHTS}
    var = {n: args["v_" + n] for n in WEIGHTS}
    depth = p.shape[0]

    gathers, scatters, recv, tokens = {}, {}, {}, {}

    def start(name, groups, names, scatter, order_after):
        flags = scatter if isinstance(scatter, tuple) else (scatter,) * len(names)
        handle, token = _exchange_start(name, [groups[n] for n in names], flags, order_after)
        return (handle, names, flags), token

    def finish(name, pending, order_after):
        handle, names, flags = pending
        return dict(zip(names, _exchange_wait(name, handle, flags, order_after)))

    def gather_next(l, order_after):
        if l + 1 >= depth:
            return None
        gathers[l + 1], token = start("all_gather_start", _shard_groups(w, l + 1), GROUPS, False, order_after)
        return token

    def layer_weights(l, h):
        views = {}
        if l > 0:
            got = finish("all_gather_wait", gathers.pop(l), h)
            views.update(_gathered_views(got, gather_next(l, got[GROUPS[0]])))
            return lambda name, after: views[name]

        shards = _shard_groups(w, 0)
        pending, token = start("all_gather_start_0", shards, FIRST_PARTS[0], False, h)
        got = finish("all_gather_wait_0", pending, token)
        waiting = {}
        token = got[FIRST_PARTS[0][0]]
        for i, part in enumerate(FIRST_PARTS[1:], 1):
            waiting[i], token = start(f"all_gather_start_{i}", shards, part, False, token)
        views.update(_gathered_views(got, gather_next(0, token)))

        def weight(name, after):
            for i in sorted(waiting):
                if name not in views:
                    views.update(_gathered_views(finish(f"all_gather_wait_{i}", waiting.pop(i), after)))
            return views[name]

        return weight

    small_l = lambda t, l: _pack_small({n: t[n][l] for n in SMALL})
    scatter_flags = lambda names: tuple(n != "small" for n in names)
    late = ("g1", "u1", "d1", "small")
    early = tuple(g for g in GROUPS if g not in late)

    def layer_partial(l, wg):
        if l > 0:
            return None
        scatters["early"], token = start("reduce_scatter_start_a", _grad_groups(wg), early, scatter_flags(early),
                                         wg["w_in"])
        return token[0, 0]

    def layer_done(l, wg, gr, dh):
        if l + 1 < depth:
            recv[l + 1] = finish("reduce_scatter_wait", scatters.pop(l + 1), dh)
        groups = dict(_grad_groups(wg), small=_pack_small(gr))
        names = late if l == 0 else GROUPS + ("small",)
        scatters[l], tokens[l] = start("reduce_scatter_start_b" if l == 0 else "reduce_scatter_start", groups, names,
                                       scatter_flags(names), dh)
        return tokens[l][0, 0]

    loss, dx, w_grads, small_grads = _local_step(x[0], p[:, 0], positions[0], loss_target[0], w, layer_weights,
                                                 layer_done, layer_partial)
    loss = lax.psum(loss, MESH_AXES)

    results = {n: None for n in SHARDED}
    last = tokens[0]
    for l in reversed(range(1, depth)):
        for n in SHARDED:
            *results[n], last = _update_sharded(n, recv[l], w[n], mom[n], var[n], l, results[n], last)
    recv[0] = dict(finish("reduce_scatter_wait_a", scatters.pop("early"), last))
    recv[0].update(finish("reduce_scatter_wait_b", scatters.pop(0), last))
    for n in SHARDED:
        results[n] = _update_sharded(n, recv[0], w[n], mom[n], var[n], 0, results[n])
    result = dict(results)

    recv_small = jnp.concatenate([recv[l]["small"] for l in range(depth)], axis=1)
    packed = [jnp.concatenate([small_l(t, l) for l in range(depth)], axis=0) for t in (w, mom, var)]
    small_out = _adamw("adamw_small", recv_small, (N_DEV, 8, SMALL_LANES), lambda i: (0, i, 0), *packed, 8)
    for q in range(4):
        flat = small_out[q].reshape(depth, -1)
        off = 0
        for n in SMALL:
            size = math.prod(w[n].shape[1:])
            result.setdefault(n, [None] * 4)[q] = flat[:, off:off + size].reshape(w[n].shape)
            off += size

    outputs = [loss, dx[None]]
    for q in range(4):
        outputs += [result[n][q] for n in WEIGHTS]
    return tuple(outputs)
```

```python
import functools
import math

import jax
import jax.numpy as jnp
from jax import lax
from jax.experimental import pallas as pl
from jax.experimental.pallas import tpu as pltpu

F32 = jnp.float32
BF16 = jnp.bfloat16

N_DEV = 8
D_MODEL = 1024
D_FF = 2816
FF_SHARD = D_FF // N_DEV
D_ATTN = 512
D_SSM = 512
HEAD_DIM = 64
N_HEADS = 8
ROPE_DIM = 16
ROPE_THETA = 500000.0
DILATIONS = (1, 4, 16)
BAND = 128
N_GROUPS = 32
SSM_GROUP = 16
SSM_STATE = 64
N_STATE = N_GROUPS * SSM_STATE
PLE_DIM = 256
NORM_EPS = 1e-6
ADAM_LR, ADAM_B1, ADAM_B2, ADAM_EPS, ADAM_WD, ADAM_STEP = 0.001, 0.9, 0.999, 1e-08, 0.01, 10

VMEM_LIMIT_BYTES = 52 * 1024 * 1024
TM = 512
MESH_AXES = ("x", "y", "c")

WEIGHTS = ['ffn1_pre_g', 'ffn1_w_gate', 'ffn1_w_up', 'ffn1_w_down', 'ffn1_post_g', 'mix_pre_g', 'w_in', 'attn_norm_g',
           'ssm_lam_re', 'ssm_lam_im', 'ssm_log_dt', 'ssm_b_re', 'ssm_b_im', 'ssm_c_re', 'ssm_c_im', 'ssm_d',
           'ssm_w_glu', 'ssm_b_glu', 'ssm_norm_g', 'w_out', 'mix_post_g', 'ffn2_pre_g', 'ffn2_w_gate', 'ffn2_w_up',
           'ffn2_w_down', 'ffn2_post_g', 'ple_w_up', 'ple_w_gate', 'ple_post_g']
SHARDED = ['ffn1_w_gate', 'ffn1_w_up', 'ffn1_w_down', 'w_in', 'ssm_w_glu', 'w_out', 'ffn2_w_gate', 'ffn2_w_up',
           'ffn2_w_down', 'ple_w_up', 'ple_w_gate']
SMALL = [n for n in WEIGHTS if n not in SHARDED]
SMALL_LANES = 1024
SMALL_PAD = 8 * SMALL_LANES


def _params(n_axes):
    return pltpu.CompilerParams(dimension_semantics=("arbitrary",) * n_axes, vmem_limit_bytes=VMEM_LIMIT_BYTES)


_DIMS = {"nn": (((1,), (0,)), ((), ())), "nt": (((1,), (1,)), ((), ())), "tn": (((0,), (0,)), ((), ()))}


def _dot(a, b, mode):
    return lax.dot_general(a.astype(BF16), b.astype(BF16), _DIMS[mode], preferred_element_type=F32)


def _store(out_refs, vals, kinds, first):
    for ref, val, kind in zip(out_refs, vals, kinds):
        if isinstance(val, (list, tuple)):
            for q, piece in enumerate(val):
                ref[q] = piece.astype(ref.dtype)
        elif kind == "row":
            ref[...] = val.astype(ref.dtype)
        else:
            @pl.when(first)
            def _(ref=ref, val=val):
                ref[...] = val.astype(ref.dtype)

            @pl.when(jnp.logical_not(first))
            def _(ref=ref, val=val):
                ref[...] += val.astype(ref.dtype)


def _call(name, body, grid, ins, outs, scratch, into):
    arrays = [t[0] for t in ins]
    in_specs = [_in_spec(t) for t in ins]
    aliases = {}
    for arr, k in into:
        arrays.append(arr)
        in_specs.append(pl.BlockSpec(memory_space=pl.ANY))
        aliases[len(arrays) - 1] = k
    return pl.pallas_call(
        body,
        name=name,
        grid=grid,
        in_specs=in_specs,
        out_specs=[pl.BlockSpec(o[2], o[3]) for o in outs],
        out_shape=[jax.ShapeDtypeStruct(o[0], o[1]) for o in outs],
        scratch_shapes=list(scratch),
        input_output_aliases=aliases,
        compiler_params=_params(len(grid)),
    )(*arrays)


def _mm(name, grid, a, b, mode, outs, extras=(), epilogue=None, acc_shape=None, scratch=(), into=(), more=(),
        a_prep=None, b_prep=None):
    prep_a = a_prep if a_prep is not None else (lambda t: t)
    prep_b = b_prep if b_prep is not None else (lambda t: t)
    nk = grid[2]
    ne, no = len(extras), len(outs)
    kinds = [o[4] for o in outs]
    assert all(k == "row" for k in kinds) or grid[1] == 1
    n_ab = 2 + 2 * len(more)
    n_in = n_ab + ne + len(into)
    n_acc = int(nk > 1)

    def body(*refs):
        ex_refs = refs[n_ab:n_ab + ne]
        out_refs = refs[n_in:n_in + no]
        scr_refs = refs[n_in + no + n_acc:]
        part = _dot(prep_a(refs[0][...]), prep_b(refs[1][...]), mode)
        for p in range(2, n_ab, 2):
            part = part + _dot(prep_a(refs[p][...]), prep_b(refs[p + 1][...]), mode)
        first = pl.program_id(0) == 0

        def finish(acc):
            vals = epilogue(acc, *[r[...] for r in ex_refs], *scr_refs) if epilogue is not None else (acc,)
            _store(out_refs, vals, kinds, first)

        if nk == 1:
            finish(part)
        else:
            acc_ref = refs[n_in + no]
            k = pl.program_id(2)

            @pl.when(k == 0)
            def _():
                acc_ref[...] = part

            @pl.when(k > 0)
            def _():
                acc_ref[...] += part

            @pl.when(k == nk - 1)
            def _():
                finish(acc_ref[...])

    acc = [pltpu.VMEM(acc_shape, F32)] if nk > 1 else []
    pairs = tuple(t for pair in more for t in pair)
    return _call(name, body, grid, (a, b) + pairs + tuple(extras), outs, acc + list(scratch), into)


def _rowwise(name, n_rows, tm, ins, outs, fn, scratch=(), into=()):
    kinds = [o[4] for o in outs]
    ni, no = len(ins), len(outs)
    n_in = ni + len(into)

    def body(*refs):
        vals = fn(*[r[...] for r in refs[:ni]], *refs[n_in + no:])
        _store(refs[n_in:n_in + no], vals, kinds, pl.program_id(0) == 0)

    return _call(name, body, (n_rows // tm,), ins, outs, scratch, into)


def _rows(arr, tm, col=0, width=None):
    width = arr.shape[1] if width is None else width
    return (arr, (tm, width), lambda i, *_: (i, col))


def _whole(arr):
    nd = arr.ndim
    return (arr, arr.shape, lambda *_: (0,) * nd)


def _resident(arr):
    nd = arr.ndim
    return (arr, arr.shape, lambda *_: (0,) * nd, dict(pipeline_mode=pl.Buffered(1)))


def _in_spec(t):
    return pl.BlockSpec(t[1], t[2], **(t[3] if len(t) > 3 else {}))


def _row_out(n_rows, width, dtype, tm, col=0, total=None):
    return ((n_rows, width if total is None else total), dtype, (tm, width), lambda i, *_: (i, col), "row")


def _col_out(width):
    return ((1, width), F32, (1, width), lambda *_: (0, 0), "colsum")


def _lane_blocks(acc):
    return [acc[:, q * LANES:(q + 1) * LANES] for q in range(acc.shape[1] // LANES)]


def _join_lanes(blk):
    return jnp.concatenate([blk[q] for q in range(blk.shape[0])], axis=1)


def _column_shards(acc, width):
    return [acc[:, q * width:(q + 1) * width] for q in range(acc.shape[1] // width)]


def _natural_cols(name, parts, after=None):
    rows, width = parts[0].shape[1:]
    tr = min(rows, 256)
    n = len(parts)

    def body(*refs):
        out_ref = refs[-1]
        for t in range(n):
            for j in range(N_DEV):
                lo = (t * N_DEV + j) * width
                out_ref[:, lo:lo + width] = refs[t][j]

    ins = [(t, (N_DEV, tr, width), lambda i: (0, i, 0)) for t in parts]
    if after is not None:
        ins.append(_whole(after))
    out = _row_out(rows, n * N_DEV * width, parts[0].dtype, tr)
    return _call(name, body, (rows // tr,), ins, [out], (), ())[0]


def _rms(x, g):
    return x * lax.rsqrt(jnp.mean(x * x, axis=-1, keepdims=True) + NORM_EPS) * g


def _sigmoid(x):
    return 1.0 / (1.0 + jnp.exp(-x))


def _gelu(x):
    return 0.5 * x * (1.0 + jnp.tanh(0.7978845608028654 * (x + 0.044715 * x * x * x)))


FFN_TM = 256
FF_HALF = D_FF // 2


def _ffn_fwd(s_len, h, n, weight, g_post, g_next):
    def gu_epi(acc):
        gate, up = acc[:, :D_FF], acc[:, D_FF:]
        return acc, gate * _sigmoid(gate) * up

    w_gu = weight("gu", n)
    gu, act = _mm("ffn_gate_up", (s_len // FFN_TM, 1, 1), _rows(n, FFN_TM), _resident(w_gu), "nn",
                  [_row_out(s_len, 2 * D_FF, BF16, FFN_TM), _row_out(s_len, D_FF, BF16, FFN_TM)], epilogue=gu_epi)
    w_d = weight("d", act)

    def down_epi(acc, h_blk, gp, gn):
        h_new = h_blk + 0.5 * _rms(acc, gp)
        return acc, h_new, _rms(h_new, gn)

    f, h_new, n_next = _mm(
        "ffn_down", (s_len // TM, 1, 1), _rows(act, TM), _resident(w_d), "nn",
        [_row_out(s_len, D_MODEL, F32, TM), _row_out(s_len, D_MODEL, F32, TM), _row_out(s_len, D_MODEL, BF16, TM)],
        extras=[_rows(h, TM), _whole(g_post), _whole(g_next)], epilogue=down_epi)
    return h_new, n_next, dict(n=n, gu=gu, act=act, f=f, h=h)


def _ffn_bwd(s_len, saved, df, w_gu, w_d, final_epi, final_extras, final_outs):
    nt = s_len // TM

    def act_epi(acc, gu_blk):
        gate = gu_blk[:, :D_FF].astype(F32)
        up = gu_blk[:, D_FF:].astype(F32)
        sg = _sigmoid(gate)
        dgate = acc * up * sg * (1.0 + gate * (1.0 - sg))
        dup = acc * gate * sg
        return (jnp.concatenate([dgate, dup], axis=1),)

    (dgu,) = _mm("ffn_bwd_act", (s_len // FFN_TM, 1, 1), _rows(df, FFN_TM), _resident(w_d), "nt",
                 [_row_out(s_len, 2 * D_FF, BF16, FFN_TM)], extras=[_rows(saved["gu"], FFN_TM)], epilogue=act_epi)

    (d_w_d,) = _mm(
        "ffn_bwd_wdown", (2, 1, nt), (saved["act"], (TM, FF_HALF), lambda i, j, k: (k, i)),
        (df, (TM, D_MODEL), lambda i, j, k: (k, 0)), "tn",
        [((D_FF, D_MODEL), BF16, (FF_HALF, D_MODEL), lambda i, j, k: (i, 0), "row")], acc_shape=(FF_HALF, D_MODEL))

    def w_grad(name, half):
        (dw,) = _mm(
            name, (1, 2, nt), (saved["n"], (TM, D_MODEL), lambda i, j, k: (k, 0)),
            (dgu, (TM, FF_HALF), lambda i, j, k: (k, 2 * half + j)), "tn",
            [((N_DEV, D_MODEL, FF_SHARD), BF16, (N_DEV // 2, D_MODEL, FF_SHARD), lambda i, j, k: (j, 0, 0), "row")],
            epilogue=lambda acc: (_column_shards(acc, FF_SHARD),), acc_shape=(D_MODEL, FF_HALF))
        return dw

    d_w_gate, d_w_up = w_grad("ffn_bwd_wgate", 0), w_grad("ffn_bwd_wup", 1)

    outs = _mm("ffn_bwd_dn", (s_len // FFN_TM, 1, 1), _rows(dgu, FFN_TM), _resident(w_gu), "nt",
               final_outs, extras=final_extras, epilogue=final_epi)
    return d_w_gate, d_w_up, d_w_d, outs


def _band_mask(first_block):
    qi = lax.broadcasted_iota(jnp.int32, (BAND, 2 * BAND), 0)
    kj = lax.broadcasted_iota(jnp.int32, (BAND, 2 * BAND), 1)
    ok = (kj >= qi) & (kj <= qi + BAND)
    return ok & (jnp.logical_not(first_block) | (kj >= BAND))


def _attn_views(s_len, d):
    n_str = s_len // d
    nb = n_str // BAND
    blk = (BAND, D_ATTN)
    cur = lambda r, b: (b, r)
    prev = lambda r, b: (jnp.maximum(b - 1, 0), r)
    return n_str, nb, blk, cur, prev


STREAM_LANES = 128


def _stream_scratch(tm):
    return pltpu.VMEM((D_ATTN // STREAM_LANES, tm, STREAM_LANES), F32)


def _to_streams(x, d, scr):
    if d == 1:
        return x
    tm = x.shape[0]
    chunks = D_ATTN // STREAM_LANES
    for c in range(chunks):
        scr[c, 0:tm, :] = x[:, c * STREAM_LANES:(c + 1) * STREAM_LANES]
    return jnp.concatenate([scr.at[c][pl.ds(r, tm // d, stride=d), :] for r in range(d) for c in range(chunks)], axis=1)


def _from_streams(v, d, scr):
    if d == 1:
        return v
    rows = v.shape[0]
    chunks = D_ATTN // STREAM_LANES
    for r in range(d):
        for c in range(chunks):
            lo = r * D_ATTN + c * STREAM_LANES
            scr.at[c][pl.ds(r, rows, stride=d), :] = v[:, lo:lo + STREAM_LANES]
    return jnp.concatenate([scr[c, 0:rows * d, :] for c in range(chunks)], axis=1)


def _stream_in(arr, tm, d, ahead=0, n_blocks=None):
    rows = tm // d
    step = ahead // rows
    if ahead:
        return (arr, (rows, d * D_ATTN), lambda i, *_: (jnp.minimum(i + step, n_blocks - 1), 0))
    return (arr, (rows, d * D_ATTN), lambda i, *_: (i, 0))


def _stream_out(s_len, dtype, tm, d):
    return ((s_len // d, d * D_ATTN), dtype, (tm // d, d * D_ATTN), lambda i, *_: (i, 0), "row")


def _attn_fwd(s_len, q, k, v, d):
    n_str, nb, blk, cur, prev = _attn_views(s_len, d)
    view = lambda t: t

    def body(q_ref, kp_ref, kc_ref, vp_ref, vc_ref, o_ref, lse_ref):
        mask = _band_mask(pl.program_id(1) == 0)
        even = lax.broadcasted_iota(jnp.int32, (BAND, 2 * HEAD_DIM), 1) < HEAD_DIM
        qq = q_ref[...]
        kk = jnp.concatenate([kp_ref[...], kc_ref[...]], axis=0)
        vv = jnp.concatenate([vp_ref[...], vc_ref[...]], axis=0)
        for pair in range(N_HEADS // 2):
            sl = slice(2 * pair * HEAD_DIM, 2 * (pair + 1) * HEAD_DIM)
            q_p, k_p, v_p = qq[:, sl], kk[:, sl], vv[:, sl]
            o, lse = [], []
            for half in range(2):
                sel = even if half == 0 else jnp.logical_not(even)
                s = _dot(jnp.where(sel, q_p, jnp.zeros_like(q_p)), k_p, "nt") * (HEAD_DIM ** -0.5)
                s = jnp.where(mask, s, -1e30)
                m = jnp.max(s, axis=-1, keepdims=True)
                e = jnp.exp(s - m)
                den = jnp.sum(e, axis=-1, keepdims=True)
                o.append(_dot(e / den, v_p, "nn"))
                lse.append(m + jnp.log(den))
            o_ref[:, sl] = jnp.where(even, o[0], o[1])
            lse_ref[:, sl] = jnp.where(even, lse[0], lse[1])

    o, lse = pl.pallas_call(
        body,
        name=f"attn_fwd_d{d}",
        grid=(d, nb),
        in_specs=[pl.BlockSpec(blk, cur), pl.BlockSpec(blk, prev), pl.BlockSpec(blk, cur),
                  pl.BlockSpec(blk, prev), pl.BlockSpec(blk, cur)],
        out_specs=[pl.BlockSpec(blk, cur), pl.BlockSpec(blk, cur)],
        out_shape=[jax.ShapeDtypeStruct((n_str, d * D_ATTN), F32)] * 2,
        compiler_params=_params(2),
    )(view(q), view(k), view(k), view(v), view(v))
    return o, lse


def _attn_bwd(s_len, q, k, v, dattn, attn, lse, d):
    n_str, nb, blk, cur, prev = _attn_views(s_len, d)
    view = lambda t: t

    def body(q_ref, kp_ref, kc_ref, vp_ref, vc_ref, da_ref, at_ref, lse_ref, dq_ref, dka_ref, dkb_ref, dva_ref, dvb_ref):
        mask = _band_mask(pl.program_id(1) == 0)
        even = lax.broadcasted_iota(jnp.int32, (BAND, 2 * HEAD_DIM), 1) < HEAD_DIM
        qq = q_ref[...]
        kk = jnp.concatenate([kp_ref[...], kc_ref[...]], axis=0)
        vv = jnp.concatenate([vp_ref[...], vc_ref[...]], axis=0)
        da = da_ref[...]
        prod = da * at_ref[...]
        da = da.astype(BF16)
        scale = HEAD_DIM ** -0.5
        for pair in range(N_HEADS // 2):
            sl = slice(2 * pair * HEAD_DIM, 2 * (pair + 1) * HEAD_DIM)
            q_p, k_p, v_p, da_p, prod_p = qq[:, sl], kk[:, sl], vv[:, sl], da[:, sl], prod[:, sl]
            dq, dk, dv = [], 0.0, 0.0
            for half in range(2):
                sel = even if half == 0 else jnp.logical_not(even)
                q_h = jnp.where(sel, q_p, jnp.zeros_like(q_p))
                da_h = jnp.where(sel, da_p, jnp.zeros_like(da_p))
                lse_col = lse_ref[:, (2 * pair + half) * HEAD_DIM:(2 * pair + half) * HEAD_DIM + 1]
                d_col = jnp.sum(jnp.where(sel, prod_p, 0.0), axis=-1, keepdims=True)
                s = _dot(q_h, k_p, "nt") * scale
                p = jnp.where(mask, jnp.exp(s - lse_col), 0.0)
                ds = p * (_dot(da_h, v_p, "nt") - d_col)
                dq.append(_dot(ds, k_p, "nn") * scale)
                dk = dk + _dot(ds, q_h, "tn") * scale
                dv = dv + _dot(p, da_h, "tn")
            dq_ref[:, sl] = jnp.where(even, dq[0], dq[1]).astype(dq_ref.dtype)
            dkb_ref[:, sl] = dk[:BAND].astype(dkb_ref.dtype)
            dka_ref[:, sl] = dk[BAND:].astype(dka_ref.dtype)
            dvb_ref[:, sl] = dv[:BAND].astype(dvb_ref.dtype)
            dva_ref[:, sl] = dv[BAND:].astype(dva_ref.dtype)

    outs = pl.pallas_call(
        body,
        name=f"attn_bwd_d{d}",
        grid=(d, nb),
        in_specs=[pl.BlockSpec(blk, cur), pl.BlockSpec(blk, prev), pl.BlockSpec(blk, cur),
                  pl.BlockSpec(blk, prev), pl.BlockSpec(blk, cur),
                  pl.BlockSpec(blk, cur), pl.BlockSpec(blk, cur), pl.BlockSpec(blk, cur)],
        out_specs=[pl.BlockSpec(blk, cur)] * 5,
        out_shape=[jax.ShapeDtypeStruct((n_str, d * D_ATTN), BF16)] * 5,
        compiler_params=_params(2),
    )(view(q), view(k), view(k), view(v), view(v), view(dattn), view(attn), view(lse))
    return list(outs)


def _rope_tables(positions):
    half = ROPE_DIM // 2
    inv_freq = ROPE_THETA ** (-jnp.arange(half, dtype=F32) * (2.0 / ROPE_DIM))
    ang = positions.astype(F32)[:, None] * inv_freq
    cos, sin = jnp.cos(ang), jnp.sin(ang)
    s_len = positions.shape[0]
    one = jnp.ones((s_len, HEAD_DIM - ROPE_DIM), F32)
    zero8 = jnp.zeros((s_len, half), F32)
    zero = jnp.zeros((s_len, HEAD_DIM - ROPE_DIM), F32)
    c = jnp.concatenate([cos, cos, one], axis=1)
    s1 = jnp.concatenate([zero8, sin, zero], axis=1)
    s2 = jnp.concatenate([-sin, zero8, zero], axis=1)
    tile = lambda t: jnp.tile(t, (1, N_HEADS))
    return tile(c), tile(s1), tile(s2)


def _rope(t, c, s1, s2):
    half = ROPE_DIM // 2
    return t * c + pltpu.roll(t, half, 1) * s1 + pltpu.roll(t, D_ATTN - half, 1) * s2


def _rope_transposed(dt, c, s1, s2):
    half = ROPE_DIM // 2
    return dt * c + pltpu.roll(dt * s1, D_ATTN - half, 1) + pltpu.roll(dt * s2, half, 1)


SCAN_ROWS = 256
SCAN_CHUNK = 128
LANES = 128
SUBLANES = 8


def _cmul(xr, xi, yr, yi):
    return xr * yr - xi * yi, xr * yi + xi * yr


def _scan(name, bu, a_cat, reverse, prev_state=None):
    s_len = bu.shape[1]
    n_blocks = bu.shape[0] // 2
    n_chunks = s_len // SCAN_CHUNK
    log_chunk = SCAN_CHUNK.bit_length() - 1
    rows_of = lambda off: pl.ds(pl.multiple_of(off * n_chunks, n_chunks), n_chunks)

    with_da = prev_state is not None

    def body(*refs):
        br_ref, bi_ref, ar_ref, ai_ref = refs[:4]
        at = 4
        if with_da:
            sr_ref, si_ref = refs[at:at + 2]
            at += 2
        xr_ref, xi_ref = refs[at:at + 2]
        at += 2
        if with_da:
            dar_ref, dai_ref = refs[at:at + 2]
            at += 2
        pr_ref, pi_ref, cr_ref, ci_ref = refs[at:at + 4]
        ar, ai = ar_ref[...], ai_ref[...]

        def step(i, state):
            rows = rows_of((SCAN_CHUNK - 1 - i) if reverse else i)
            pr, pi = _cmul(ar, ai, *state)
            nr, ni = pr + br_ref[rows, :], pi + bi_ref[rows, :]
            xr_ref[rows, :] = nr
            xi_ref[rows, :] = ni
            return nr, ni

        zero = jnp.zeros((n_chunks, LANES), F32)
        lax.fori_loop(0, SCAN_CHUNK, step, (zero, zero), unroll=4)

        pw = [(ar, ai)]
        for _ in range(log_chunk + n_chunks.bit_length()):
            pw.append(_cmul(*pw[-1], *pw[-1]))

        last = pl.ds((0 if reverse else SCAN_CHUNK - 1) * n_chunks, n_chunks)
        er, ei = xr_ref[last, :], xi_ref[last, :]
        chunk = lax.broadcasted_iota(jnp.int32, (n_chunks, LANES), 0)

        def shifted(v, s):
            if reverse:
                return jnp.where(chunk < n_chunks - s, pltpu.roll(v, n_chunks - s, 0), 0.0)
            return jnp.where(chunk >= s, pltpu.roll(v, s, 0), 0.0)

        s, b = 1, log_chunk
        while s < n_chunks:
            mr, mi = _cmul(*pw[b], shifted(er, s), shifted(ei, s))
            er, ei = er + mr, ei + mi
            s, b = 2 * s, b + 1
        cr_ref[...] = shifted(er, 1)
        ci_ref[...] = shifted(ei, 1)

        step_no = lax.broadcasted_iota(jnp.int32, (SUBLANES, LANES), 0)
        expo = (SUBLANES - step_no) if reverse else (step_no + 1)
        qr, qi = jnp.ones((SUBLANES, LANES), F32), jnp.zeros((SUBLANES, LANES), F32)
        for bit in range(4):
            mr, mi = _cmul(qr, qi, *pw[bit])
            hit = (expo & (1 << bit)) != 0
            qr, qi = jnp.where(hit, mr, qr), jnp.where(hit, mi, qi)
        lo = SCAN_CHUNK - SUBLANES if reverse else 0
        pr_ref[lo:lo + SUBLANES, :] = qr
        pi_ref[lo:lo + SUBLANES, :] = qi
        m, b = SUBLANES, 3
        while m < SCAN_CHUNK:
            src = pl.ds(SCAN_CHUNK - m, m) if reverse else pl.ds(0, m)
            dst = pl.ds(SCAN_CHUNK - 2 * m, m) if reverse else pl.ds(m, m)
            mr, mi = _cmul(pr_ref[src, :], pi_ref[src, :], *pw[b])
            pr_ref[dst, :] = mr
            pi_ref[dst, :] = mi
            m, b = 2 * m, b + 1

        cr, ci = cr_ref[...], ci_ref[...]

        if with_da:
            ends = rows_of(SCAN_CHUNK - 1)
            wrap_r = jnp.where(chunk == 0, 0.0, pltpu.roll(sr_ref[ends, :], 1, 0))
            wrap_i = jnp.where(chunk == 0, 0.0, pltpu.roll(si_ref[ends, :], 1, 0))

        def fix(i, acc):
            rows = rows_of(i)
            mr, mi = _cmul(pr_ref[pl.ds(i, 1), :], pi_ref[pl.ds(i, 1), :], cr, ci)
            gr, gi = xr_ref[rows, :] + mr, xi_ref[rows, :] + mi
            xr_ref[rows, :] = gr
            xi_ref[rows, :] = gi
            if not with_da:
                return acc
            before = rows_of(jnp.maximum(i - 1, 0))
            qr = jnp.where(i == 0, wrap_r, sr_ref[before, :])
            qi = jnp.where(i == 0, wrap_i, si_ref[before, :])
            return acc[0] + qr * gr + qi * gi, acc[1] + qr * gi - qi * gr

        sums = lax.fori_loop(0, SCAN_CHUNK, fix, (zero, zero), unroll=4)
        if with_da:
            dar_ref[...] = jnp.sum(sums[0], axis=0, keepdims=True)
            dai_ref[...] = jnp.sum(sums[1], axis=0, keepdims=True)

    blk = lambda off: pl.BlockSpec((None, s_len, LANES), lambda c: (off + c, 0, 0))
    col = lambda off: pl.BlockSpec((1, LANES), lambda c: (0, off + c))
    one = pl.BlockSpec((None, 1, LANES), lambda c: (c, 0, 0))
    extra_in = [blk(0), blk(0)] if with_da else []
    extra_out = [one, one] if with_da else []
    return pl.pallas_call(
        body,
        name=name,
        grid=(n_blocks,),
        in_specs=[blk(0), blk(n_blocks), col(0), col(n_blocks)] + extra_in,
        out_specs=[blk(0), blk(0)] + extra_out,
        out_shape=[jax.ShapeDtypeStruct((n_blocks, s_len, LANES), F32)] * 2
        + [jax.ShapeDtypeStruct((n_blocks, 1, LANES), F32)] * len(extra_out),
        scratch_shapes=[pltpu.VMEM((SCAN_CHUNK, LANES), F32)] * 2 + [pltpu.VMEM((n_chunks, LANES), F32)] * 2,
        compiler_params=_params(1),
    )(bu, bu, a_cat, a_cat, *(prev_state if with_da else ()))


def _to_chunk_order(t):
    s_len, width = t.shape
    return t.reshape(s_len // SCAN_CHUNK, SCAN_CHUNK, width).transpose(1, 0, 2).reshape(s_len, width)


def _from_chunk_order(t):
    s_len, width = t.shape
    return t.reshape(SCAN_CHUNK, s_len // SCAN_CHUNK, width).transpose(1, 0, 2).reshape(s_len, width)


def _state_in(arr, tm):
    return (arr, (arr.shape[0], tm, LANES), lambda i, *_: (0, i, 0))


def _state_out(s_len, width, tm):
    n = width // LANES
    return ((n, s_len, LANES), F32, (n, tm, LANES), lambda i, *_: (0, i, 0), "row")


@jax.custom_vjp
def _block_diag(blocks):
    g, r, c = blocks.shape
    eye = jnp.eye(g, dtype=blocks.dtype)
    return (blocks[:, :, None, :] * eye[:, None, :, None]).reshape(g * r, g * c)


def _block_diag_fwd(blocks):
    return _block_diag(blocks), blocks.shape


def _block_diag_bwd(shape, ct):
    g, r, c = shape
    per = max(LANES // c, 1)

    def body(ct_ref, out_ref):
        for k in range(per):
            out_ref[k] = ct_ref[k * r:(k + 1) * r, k * c:(k + 1) * c]

    return (pl.pallas_call(
        body,
        name=f"diag_blocks_{r}x{c}",
        grid=(g // per,),
        in_specs=[pl.BlockSpec((per * r, per * c), lambda i: (i, i))],
        out_specs=pl.BlockSpec((per, r, c), lambda i: (i, 0, 0)),
        out_shape=jax.ShapeDtypeStruct((g, r, c), ct.dtype),
        compiler_params=_params(1),
    )(ct),)


_block_diag.defvjp(_block_diag_fwd, _block_diag_bwd)


def _ssm_params(lam_re, lam_im, log_dt, b_re, b_im, c_re, c_im):
    dt = jnp.exp(log_dt)[:, None]
    er = jnp.exp(lam_re * dt)
    a_re = er * jnp.cos(lam_im * dt)
    a_im = er * jnp.sin(lam_im * dt)
    nr, ni = a_re - 1.0, a_im
    den = lam_re * lam_re + lam_im * lam_im
    fr = (nr * lam_re + ni * lam_im) / den
    fi = (ni * lam_re - nr * lam_im) / den
    bb_re = fr[..., None] * b_re - fi[..., None] * b_im
    bb_im = fr[..., None] * b_im + fi[..., None] * b_re
    in_mat = lambda bb: _block_diag(bb.transpose(0, 2, 1))
    out_mat = lambda cc: _block_diag(cc.transpose(0, 2, 1))
    w_b = (in_mat(bb_re), in_mat(bb_im))
    w_c = (out_mat(c_re), -out_mat(c_im))
    a_cat = jnp.concatenate([a_re.reshape(1, N_STATE), a_im.reshape(1, N_STATE)], axis=1)
    return a_cat, w_b, w_c


def _conj(a_cat):
    return jnp.concatenate([a_cat[:, :N_STATE], -a_cat[:, N_STATE:]], axis=1)


def _layer_fwd(s_len, h, n1, p_l, rope, weight, sp, gains, g_next):
    nt = s_len // TM
    sv = {}
    gw = {}

    def need(after, *names):
        for name in names:
            gw[name] = weight(name, after)

    def ffn_weight(which):
        def get(kind, after):
            need(after, f"w_{kind}{which}")
            return gw[f"w_{kind}{which}"]
        return get

    h1, a_in, sv["ffn1"] = _ffn_fwd(s_len, h, n1, ffn_weight(1), gains["ffn1_post_g"], gains["mix_pre_g"])

    need(h1, "w_in", "w_glu", "w_out")
    (proj,) = _mm("w_in", (nt, 1, 1), _rows(a_in, TM), _resident(gw["w_in"]), "nn",
                  [_row_out(s_len, 2 * D_MODEL, F32, TM)])

    c, s1, s2 = rope

    def rope_fn(tq, tk, tv, cc, a1, a2, scr):
        q, k = _rope(tq, cc, a1, a2), _rope(tk, cc, a1, a2)
        return [_to_streams(t, d, scr) for d in DILATIONS for t in (q, k, tv)]

    qkv = _rowwise(
        "rope", s_len, TM,
        [_rows(proj, TM, 0, D_ATTN), _rows(proj, TM, 1, D_ATTN), _rows(proj, TM, 2, D_ATTN),
         _rows(c, TM), _rows(s1, TM), _rows(s2, TM)],
        [_stream_out(s_len, BF16, TM, d) for d in DILATIONS for _ in range(3)], rope_fn,
        scratch=[_stream_scratch(TM)])
    qkv = {d: qkv[3 * i:3 * i + 3] for i, d in enumerate(DILATIONS)}
    parts = {d: _attn_fwd(s_len, *qkv[d], d) for d in DILATIONS}

    def mix_fn(*args):
        g, scr = args[6], args[7]
        o1, l1, o2, l2, o3, l3 = [_from_streams(args[2 * i + j], d, scr) for i, d in enumerate(DILATIONS)
                                  for j in range(2)]
        m = jnp.maximum(jnp.maximum(l1, l2), l3)
        e1, e2, e3 = jnp.exp(l1 - m), jnp.exp(l2 - m), jnp.exp(l3 - m)
        tot = e1 + e2 + e3
        attn = (e1 * o1 + e2 * o2 + e3 * o3) / tot
        lse = m + jnp.log(tot)
        return [attn, lse, _rms(attn, g)] + [_to_streams(t, d, scr) for d in DILATIONS[1:] for t in (attn, lse)]

    mix_out = _rowwise(
        "attn_mix", s_len, TM,
        [_stream_in(t, TM, d) for d in DILATIONS for t in parts[d]] + [_whole(gains["attn_norm_g"])],
        [_row_out(s_len, D_ATTN, F32, TM), _row_out(s_len, D_ATTN, F32, TM),
         _row_out(s_len, D_ATTN, BF16, TM, col=0, total=D_MODEL)]
        + [_stream_out(s_len, F32, TM, d) for d in DILATIONS[1:] for _ in range(2)], mix_fn,
        scratch=[_stream_scratch(TM)])
    attn, lse, mixed_half = mix_out[:3]
    attn_s = {1: (attn, lse), DILATIONS[1]: tuple(mix_out[3:5]), DILATIONS[2]: tuple(mix_out[5:7])}

    a_cat, w_b, w_c, d_vec = sp
    ts = SCAN_ROWS
    u_c = _to_chunk_order(proj[:, 3 * D_ATTN:])
    (bu,) = _mm("ssm_bu", (s_len // ts, 1, 1), _rows(u_c, ts), _whole(w_b), "nn",
                [_state_out(s_len, 2 * N_STATE, ts)], epilogue=lambda acc: (_lane_blocks(acc),))
    xs = _scan("ssm_scan", bu, a_cat, False)
    w_c_half = lambda part: (w_c, (N_STATE, D_SSM), lambda *_: (part, 0))
    (z_c,) = _mm("ssm_y", (s_len // ts, 1, 1), _state_in(xs[0], ts), w_c_half(0), "nn",
                 [_row_out(s_len, D_SSM, F32, ts)], more=[(_state_in(xs[1], ts), w_c_half(1))], a_prep=_join_lanes,
                 extras=[_rows(u_c, ts), _whole(d_vec)], epilogue=lambda acc, u, dv: (acc + dv * u,))
    z = _from_chunk_order(z_c)

    def glu_epi(acc, z_blk, b, g):
        y = _gelu(z_blk)
        t = acc + b
        ssm = y * _sigmoid(t)
        return y, t, ssm, _rms(ssm, g)

    yg, t_glu, ssm, mixed = _mm(
        "ssm_glu", (nt, 1, 1), _rows(z, TM), _whole(gw["w_glu"]), "nn",
        [_row_out(s_len, D_SSM, F32, TM), _row_out(s_len, D_SSM, F32, TM), _row_out(s_len, D_SSM, F32, TM),
         _row_out(s_len, D_SSM, BF16, TM, col=1, total=D_MODEL)], a_prep=_gelu,
        extras=[_rows(z, TM), _whole(gains["ssm_b_glu"]), _whole(gains["ssm_norm_g"])], epilogue=glu_epi,
        into=[(mixed_half, 3)])

    def out_epi(acc, h_blk, gp, gn):
        h_new = h_blk + _rms(acc, gp)
        return acc, h_new, _rms(h_new, gn)

    o, h2, n2 = _mm(
        "w_out", (nt, 1, 1), _rows(mixed, TM), _whole(gw["w_out"]), "nn",
        [_row_out(s_len, D_MODEL, F32, TM), _row_out(s_len, D_MODEL, F32, TM), _row_out(s_len, D_MODEL, BF16, TM)],
        extras=[_rows(h1, TM), _whole(gains["mix_post_g"]), _whole(gains["ffn2_pre_g"])], epilogue=out_epi)

    h3, _, sv["ffn2"] = _ffn_fwd(s_len, h2, n2, ffn_weight(2), gains["ffn2_post_g"], gains["ffn2_post_g"])
    need(h3, "w_pu", "ple_w_gate")

    (pu,) = _mm("ple_up", (nt, 1, 1), _rows(p_l, TM), _resident(gw["w_pu"]), "nn",
                [_row_out(s_len, D_MODEL, F32, TM)])

    def ple_epi(acc, pu_blk, h_blk, gp, gn):
        h_new = h_blk + _rms(pu_blk * _sigmoid(acc), gp)
        return acc, h_new, _rms(h_new, gn)

    gt, h4, n_next = _mm(
        "ple_gate", (nt, 1, 1), _rows(h3, TM), _whole(gw["ple_w_gate"]), "nn",
        [_row_out(s_len, D_MODEL, F32, TM), _row_out(s_len, D_MODEL, F32, TM), _row_out(s_len, D_MODEL, BF16, TM)],
        extras=[_rows(pu, TM), _rows(h3, TM), _whole(gains["ple_post_g"]), _whole(g_next)], epilogue=ple_epi)

    sv.update(h1=h1, a_in=a_in, proj=proj, qkv=qkv, attn=attn, attn_s=attn_s, xs=xs, u_c=u_c, z=z, yg=yg, t_glu=t_glu,
              ssm=ssm,
              mixed=mixed, o=o, h2=h2, h3=h3, pu=pu, gt=gt, p_l=p_l, w=gw)
    return h4, n_next, sv


def _vjp(fn, args, cot):
    _, pull = jax.vjp(fn, *args)
    return pull(cot)


def _layer_bwd(s_len, dh4, sv, rope, gw, sp, gains, on_partial=None):
    nt = s_len // TM
    gr = {}
    wg = {}

    def ple_fn(dh, pu, gt, g):
        dpu, dgt, dg = _vjp(lambda a, b, c: _rms(a * _sigmoid(b), c), (pu, gt, g), dh)
        return dpu, dgt, dg

    dpu, dgt, gr["ple_post_g"] = _rowwise(
        "ple_bwd", s_len, TM, [_rows(dh4, TM), _rows(sv["pu"], TM), _rows(sv["gt"], TM), _whole(gains["ple_post_g"])],
        [_row_out(s_len, D_MODEL, BF16, TM), _row_out(s_len, D_MODEL, BF16, TM), _col_out(D_MODEL)], ple_fn)

    def square_w_grad(name, lhs, rhs):
        half = D_MODEL // 2
        (dw,) = _mm(name, (2, 1, nt), (lhs, (TM, half), lambda i, j, k: (k, i)),
                    (rhs, (TM, D_MODEL), lambda i, j, k: (k, 0)), "tn",
                    [((D_MODEL, D_MODEL), BF16, (half, D_MODEL), lambda i, j, k: (i, 0), "row")],
                    acc_shape=(half, D_MODEL))
        return dw

    (wg["ple_w_up"],) = _mm(
        "ple_bwd_wup", (1, 1, nt), (sv["p_l"], (TM, PLE_DIM), lambda i, j, k: (k, 0)),
        (dpu, (TM, D_MODEL), lambda i, j, k: (k, 0)), "tn",
        [((N_DEV, PLE_DIM, D_MODEL // N_DEV), BF16, (N_DEV, PLE_DIM, D_MODEL // N_DEV), lambda i, j, k: (0, 0, 0),
          "row")],
        epilogue=lambda acc: (_column_shards(acc, D_MODEL // N_DEV),), acc_shape=(PLE_DIM, D_MODEL))
    wg["ple_w_gate"] = square_w_grad("ple_bwd_wgate", sv["h3"], dgt)

    def ple_dx_epi(acc, dh, f, g):
        dh3 = dh + acc
        df, dg = _vjp(_rms, (f, g), 0.5 * dh3)
        return dh3, df, dg

    dh3, df2, gr["ffn2_post_g"] = _mm(
        "ple_bwd_dx", (nt, 1, 1), _rows(dgt, TM), _whole(gw["ple_w_gate"]), "nt",
        [_row_out(s_len, D_MODEL, F32, TM), _row_out(s_len, D_MODEL, BF16, TM), _col_out(D_MODEL)],
        extras=[_rows(dh4, TM), _rows(sv["ffn2"]["f"], TM), _whole(gains["ffn2_post_g"])], epilogue=ple_dx_epi)

    def ffn2_final(dn, dh, h, g_pre, o, g_post):
        dx, dg_pre = _vjp(_rms, (h, g_pre), dn)
        dh2 = dh + dx
        do, dg_post = _vjp(_rms, (o, g_post), dh2)
        return dh2, do, dg_pre, dg_post

    wg["ffn2_w_gate"], wg["ffn2_w_up"], wg["ffn2_w_down"], (dh2, do, gr["ffn2_pre_g"], gr["mix_post_g"]) = _ffn_bwd(
        s_len, sv["ffn2"], df2, gw["w_gu2"], gw["w_d2"], ffn2_final,
        [_rows(dh3, FFN_TM), _rows(sv["h2"], FFN_TM), _whole(gains["ffn2_pre_g"]), _rows(sv["o"], FFN_TM),
         _whole(gains["mix_post_g"])],
        [_row_out(s_len, D_MODEL, F32, FFN_TM), _row_out(s_len, D_MODEL, BF16, FFN_TM), _col_out(D_MODEL),
         _col_out(D_MODEL)])

    wg["w_out"] = square_w_grad("w_out_bwd_w", sv["mixed"], do)

    def mixed_epi(acc, attn, ssm, yg, t, g_a, g_s, scr):
        dattn, dg_a = _vjp(_rms, (attn, g_a), acc[:, :D_ATTN])
        dssm, dg_s = _vjp(_rms, (ssm, g_s), acc[:, D_ATTN:])
        sg = _sigmoid(t)
        dt = dssm * yg * sg * (1.0 - sg)
        return ([_to_streams(dattn, d, scr) for d in DILATIONS]
                + [dt, dssm * sg, dg_a, dg_s, jnp.sum(dt, axis=0, keepdims=True)])

    res = _mm(
        "w_out_bwd_x", (nt, 1, 1), _rows(do, TM), _whole(gw["w_out"]), "nt",
        [_stream_out(s_len, F32, TM, d) for d in DILATIONS]
        + [_row_out(s_len, D_SSM, BF16, TM), _row_out(s_len, D_SSM, F32, TM),
           _col_out(D_ATTN), _col_out(D_SSM), _col_out(D_SSM)],
        extras=[_rows(sv["attn"], TM), _rows(sv["ssm"], TM), _rows(sv["yg"], TM), _rows(sv["t_glu"], TM),
                _whole(gains["attn_norm_g"]), _whole(gains["ssm_norm_g"])], epilogue=mixed_epi,
        scratch=[_stream_scratch(TM)])
    dattn_s = dict(zip(DILATIONS, res[:3]))
    dt_glu, dyg_dir, gr["attn_norm_g"], gr["ssm_norm_g"], gr["ssm_b_glu"] = res[3:]

    a_cat, w_b, w_c, d_vec = sp
    ts = SCAN_ROWS
    u_spec = (sv["proj"], (TM, D_SSM), lambda i, *_: (i, 3))
    (wg["ssm_w_glu"],) = _mm(
        "ssm_bwd_wglu", (1, 1, nt), (sv["yg"], (TM, D_SSM), lambda i, j, k: (k, 0)),
        (dt_glu, (TM, D_SSM), lambda i, j, k: (k, 0)), "tn",
        [((D_SSM, D_SSM), BF16, (D_SSM, D_SSM), lambda i, j, k: (0, 0), "row")], acc_shape=(D_SSM, D_SSM))

    def gelu_epi(acc, dy_dir, z, u, dv):
        (dz,) = _vjp(_gelu, (z,), acc + dy_dir)
        return dz, dz * dv, jnp.sum(dz * u, axis=0, keepdims=True)

    dz, du_dir, gr["ssm_d"] = _mm(
        "ssm_bwd_glu", (nt, 1, 1), _rows(dt_glu, TM), _whole(gw["w_glu"]), "nt",
        [_row_out(s_len, D_SSM, BF16, TM), _row_out(s_len, D_SSM, F32, TM), _col_out(D_SSM)],
        extras=[_rows(dyg_dir, TM), _rows(sv["z"], TM), u_spec, _whole(d_vec)], epilogue=gelu_epi)

    x_re, x_im = sv["xs"]
    w_c_half = lambda part: (w_c, (N_STATE, D_SSM), lambda *_: (part, 0))
    w_b_half = lambda part: (w_b, (D_SSM, N_STATE), lambda *_: (0, part))

    n_lb = N_STATE // LANES
    n_chunks = s_len // SCAN_CHUNK
    dz_c = _to_chunk_order(dz)

    def w_c_grad(name, x):
        per = 1024 // LANES
        (dw,) = _mm(name, (N_STATE // 1024, 1, nt), (x, (per, TM, LANES), lambda i, j, k: (i, k, 0)),
                    (dz_c, (TM, D_SSM), lambda i, j, k: (k, 0)), "tn",
                    [((N_STATE, D_SSM), F32, (1024, D_SSM), lambda i, j, k: (i, 0), "row")], acc_shape=(1024, D_SSM),
                    a_prep=_join_lanes)
        return dw

    d_w_c = (w_c_grad("ssm_bwd_wc_re", x_re), w_c_grad("ssm_bwd_wc_im", x_im))
    (dxs,) = _mm("ssm_bwd_dx", (s_len // ts, 1, 1), _rows(dz_c, ts), _whole(w_c), "nt",
                 [_state_out(s_len, 2 * N_STATE, ts)], epilogue=lambda acc: (_lane_blocks(acc),))
    g_re, g_im, da_re, da_im = _scan("ssm_scan_rev", dxs, _conj(a_cat), True, prev_state=(x_re, x_im))
    d_a = jnp.concatenate([da_re.reshape(1, N_STATE), da_im.reshape(1, N_STATE)], axis=1)

    def w_b_grad(name, g):
        (dw,) = _mm(name, (1, 1, nt), (sv["u_c"], (TM, D_SSM), lambda i, j, k: (k, 0)),
                    (g, (n_lb, TM, LANES), lambda i, j, k: (0, k, 0)), "tn",
                    [((D_SSM, N_STATE), F32, (D_SSM, N_STATE), lambda i, j, k: (0, 0), "row")],
                    acc_shape=(D_SSM, N_STATE), b_prep=_join_lanes)
        return dw

    d_w_b = (w_b_grad("ssm_bwd_wb_re", g_re), w_b_grad("ssm_bwd_wb_im", g_im))
    (du_c,) = _mm("ssm_bwd_du", (s_len // ts, 1, 1), _state_in(g_re, ts), w_b_half(0), "nt",
                  [_row_out(s_len, D_SSM, F32, ts)], more=[(_state_in(g_im, ts), w_b_half(1))], a_prep=_join_lanes)
    du = (_from_chunk_order(du_c) + du_dir).astype(BF16)
    sum_tm = 2 * BAND
    n_sum = s_len // sum_tm
    ins = []
    for d in DILATIONS:
        dq_p, dka, dkb, dva, dvb = _attn_bwd(s_len, *sv["qkv"][d], dattn_s[d], *sv["attn_s"][d], d)
        if d == 1:
            nxt = lambda t: (t, (sum_tm, D_ATTN), lambda i: (jnp.minimum(i + 1, n_sum - 1), 0))
            ins += [_rows(dq_p, sum_tm), _rows(dka, sum_tm), _rows(dkb, sum_tm), nxt(dkb), _rows(dva, sum_tm),
                    _rows(dvb, sum_tm), nxt(dvb)]
        else:
            ins += [_stream_in(dq_p, sum_tm, d), _stream_in(dka, sum_tm, d), _stream_in(dkb, sum_tm, d, BAND, n_sum),
                    _stream_in(dva, sum_tm, d), _stream_in(dvb, sum_tm, d, BAND, n_sum)]
    c, s1, s2 = rope
    ins += [_rows(c, sum_tm), _rows(s1, sum_tm), _rows(s2, sum_tm), _rows(du, sum_tm)]

    def qkv_fn(*args):
        i = pl.program_id(0)
        blocks, scr = [b.astype(F32) for b in args[:-1]], args[-1]
        dq_t, dka, dkb, dkb_next, dva, dvb, dvb_next = blocks[:7]
        more = i + 1 < n_sum
        ahead = lambda cur, nxt: jnp.concatenate([cur[BAND:], jnp.where(more, nxt[:BAND], 0.0)], axis=0)
        dk_t = dka + ahead(dkb, dkb_next)
        dv_t = dva + ahead(dvb, dvb_next)
        at = 7
        for d in DILATIONS[1:]:
            dq_p, dka, dkb, dva, dvb = blocks[at:at + 5]
            at += 5
            live = i + BAND // (sum_tm // d) < n_sum
            dq_t = dq_t + _from_streams(dq_p, d, scr)
            dk_t = dk_t + _from_streams(dka + jnp.where(live, dkb, 0.0), d, scr)
            dv_t = dv_t + _from_streams(dva + jnp.where(live, dvb, 0.0), d, scr)
        cc, a1, a2, du_blk = blocks[at:at + 4]
        return (jnp.concatenate([_rope_transposed(dq_t, cc, a1, a2), _rope_transposed(dk_t, cc, a1, a2), dv_t,
                                 du_blk.astype(F32)], axis=1),)

    (dproj,) = _rowwise("attn_bwd_sum", s_len, sum_tm, ins, [_row_out(s_len, 2 * D_MODEL, BF16, sum_tm)], qkv_fn,
                        scratch=[_stream_scratch(sum_tm)])

    (wg["w_in"],) = _mm(
        "w_in_bwd_w", (1, 2, nt), (sv["a_in"], (TM, D_MODEL), lambda i, j, k: (k, 0)),
        (dproj, (TM, D_MODEL), lambda i, j, k: (k, j)), "tn",
        [((N_DEV, D_MODEL, 2 * D_MODEL // N_DEV), BF16, (N_DEV // 2, D_MODEL, 2 * D_MODEL // N_DEV),
          lambda i, j, k: (j, 0, 0), "row")],
        epilogue=lambda acc: (_column_shards(acc, 2 * D_MODEL // N_DEV),), acc_shape=(D_MODEL, D_MODEL))

    tie = on_partial(wg) if on_partial is not None else None
    mix_pre_g = gains["mix_pre_g"] if tie is None else gains["mix_pre_g"] + tie

    def in_epi(acc, dh, h, g_pre, f, g_post):
        dx, dg_pre = _vjp(_rms, (h, g_pre), acc)
        dh1 = dh + dx
        df, dg_post = _vjp(_rms, (f, g_post), 0.5 * dh1)
        return dh1, df, dg_pre, dg_post

    dh1, df1, gr["mix_pre_g"], gr["ffn1_post_g"] = _mm(
        "w_in_bwd_x", (nt, 1, 1), _rows(dproj, TM), _resident(gw["w_in"]), "nt",
        [_row_out(s_len, D_MODEL, F32, TM), _row_out(s_len, D_MODEL, BF16, TM), _col_out(D_MODEL), _col_out(D_MODEL)],
        extras=[_rows(dh2, TM), _rows(sv["h1"], TM), _whole(mix_pre_g), _rows(sv["ffn1"]["f"], TM),
                _whole(gains["ffn1_post_g"])], epilogue=in_epi)

    def ffn1_final(dn, dh, h, g_pre):
        dx, dg_pre = _vjp(_rms, (h, g_pre), dn)
        return dh + dx, dg_pre

    wg["ffn1_w_gate"], wg["ffn1_w_up"], wg["ffn1_w_down"], (dh0, gr["ffn1_pre_g"]) = _ffn_bwd(
        s_len, sv["ffn1"], df1, gw["w_gu1"], gw["w_d1"], ffn1_final,
        [_rows(dh1, FFN_TM), _rows(sv["ffn1"]["h"], FFN_TM), _whole(gains["ffn1_pre_g"])],
        [_row_out(s_len, D_MODEL, F32, FFN_TM), _col_out(D_MODEL)])

    return dh0, wg, gr, (d_a, d_w_b, d_w_c)


def _peers():
    x, y, c = lax.axis_index("x"), lax.axis_index("y"), lax.axis_index("c")
    me = 4 * x + 2 * y + c
    peers = []
    for k in range(1, N_DEV):
        kx, ky, kc = (k >> 2) & 1, (k >> 1) & 1, k & 1
        px, py, pc = x ^ kx, y ^ ky, c ^ kc
        peers.append(((px, py, pc), 4 * px + 2 * py + pc))
    return me, peers


_HBM_SPEC = pl.BlockSpec(memory_space=pltpu.HBM)
_SEM_SPEC = pl.BlockSpec(memory_space=pltpu.SEMAPHORE)
_DATAFLOW = pltpu.SideEffectType.DATAFLOW_SIDE_EFFECTING


def _device_index():
    return 4 * lax.axis_index("x") + 2 * lax.axis_index("y") + lax.axis_index("c")


def _landing(arrays, scatter):
    me = _device_index()
    out = []
    for a, scattered in zip(arrays, scatter):
        own = lax.dynamic_index_in_dim(a, me, 0, keepdims=True) if scattered else a[None]
        buf = lax.empty((N_DEV,) + own.shape[1:], a.dtype)
        out.append(lax.dynamic_update_slice_in_dim(buf, own, me, 0))
    return out


def _split_copies(src_refs, land_refs, send_sems, recv_sems, scatter):
    me, peers = _peers()
    pairs = []
    for t in range(len(src_refs)):
        for k, (peer, peer_id) in enumerate(peers):
            src = src_refs[t].at[peer_id] if scatter[t] else src_refs[t]
            sem = t * (N_DEV - 1) + k
            mk = lambda slot, src=src, t=t, sem=sem, peer=peer: pltpu.make_async_remote_copy(
                src_ref=src, dst_ref=land_refs[t].at[slot], send_sem=send_sems.at[sem], recv_sem=recv_sems.at[sem],
                device_id=peer, device_id_type=pl.DeviceIdType.MESH)
            pairs.append((functools.partial(mk, me), functools.partial(mk, peer_id)))
    return pairs


def _exchange_start(name, arrays, scatter, order_after):
    n = len(arrays)
    landing = _landing(arrays, scatter)

    def body(*refs):
        src_refs, land_refs = refs[:n], refs[n:2 * n]
        send_sems, recv_sems = refs[2 * n + 1], refs[2 * n + 2]
        token_ref = refs[-1]
        for outgoing, _ in _split_copies(src_refs, land_refs, send_sems, recv_sems, scatter):
            outgoing().start()
        token_ref[...] = jnp.zeros_like(token_ref)

    sem_shape = pltpu.SemaphoreType.DMA((n * (N_DEV - 1),))
    thru = [pltpu.HBM(a.shape, a.dtype) for a in list(arrays) + landing]
    hbm = lambda t: pltpu.with_memory_space_constraint(t, pltpu.HBM)
    res = pl.pallas_call(
        body,
        name=name,
        in_specs=[_HBM_SPEC] * (2 * n) + [pl.BlockSpec(memory_space=pl.ANY)],
        out_specs=[_SEM_SPEC, _SEM_SPEC] + [_HBM_SPEC] * (2 * n) + [pl.BlockSpec(memory_space=pltpu.VMEM)],
        out_shape=[sem_shape, sem_shape] + thru + [jax.ShapeDtypeStruct((8, 128), F32)],
        input_output_aliases={i: 2 + i for i in range(2 * n)},
        compiler_params=pltpu.CompilerParams(has_side_effects=_DATAFLOW),
    )(*[hbm(t) for t in list(arrays) + landing], order_after)
    return (res[0], res[1], res[2:2 + n], res[2 + n:2 + 2 * n]), res[-1]


def _exchange_wait(name, handle, scatter, order_after):
    send_sems, recv_sems, sources, landing = handle
    n = len(sources)

    def body(*refs):
        src_refs, land_refs = refs[:n], refs[n:2 * n]
        for outgoing, arrival in _split_copies(src_refs, land_refs, refs[2 * n], refs[2 * n + 1], scatter):
            outgoing().wait_send()
            arrival().wait_recv()

    thru = [pltpu.HBM(a.shape, a.dtype) for a in list(sources) + list(landing)]
    res = pl.pallas_call(
        body,
        name=name,
        in_specs=[_HBM_SPEC] * (2 * n) + [_SEM_SPEC, _SEM_SPEC, pl.BlockSpec(memory_space=pl.ANY)],
        out_specs=[_HBM_SPEC] * (2 * n),
        out_shape=thru,
        input_output_aliases={i: i for i in range(2 * n)},
        compiler_params=pltpu.CompilerParams(has_side_effects=_DATAFLOW),
    )(*sources, *landing, send_sems, recv_sems, order_after)
    return list(res[n:])


def _adam_math(g, w, m, v):
    m = ADAM_B1 * m + (1.0 - ADAM_B1) * g
    v = ADAM_B2 * v + (1.0 - ADAM_B2) * (g * g)
    m_hat = m / (1.0 - ADAM_B1 ** ADAM_STEP)
    v_hat = v / (1.0 - ADAM_B2 ** ADAM_STEP)
    delta = -ADAM_LR * (m_hat / (jnp.sqrt(v_hat) + ADAM_EPS) + ADAM_WD * w)
    return delta, m, v


def _adamw(name, recv, recv_block, recv_map, w, m, v, tr, layer=None, prev=None, after=None):
    def fn(r, wb, mb, vb, *token):
        g = r[0].astype(F32)
        for s in range(1, N_DEV):
            g = g + r[s].astype(F32)
        return (g,) + _adam_math(g, wb, mb, vb) + tuple(jnp.zeros_like(t) for t in token)

    rows, cols = w.shape[-2:]
    if layer is None:
        spec = lambda t: _rows(t, tr)
        out = _row_out(rows, cols, F32, tr)
    else:
        spec = lambda t: (t, (None, tr, cols), lambda i: (layer, i, 0))
        out = (w.shape, F32, (None, tr, cols), lambda i: (layer, i, 0), "row")
    ins = [(recv, recv_block, recv_map), spec(w), spec(m), spec(v)]
    outs = [out] * 4
    if after is not None:
        ins.append(_whole(after))
        outs.append((after.shape, F32, after.shape, lambda i: (0, 0), "row"))
    into = [] if prev is None else [(t, k) for k, t in enumerate(prev)]
    return _rowwise(name, rows, tr, ins, outs, fn, into=into)


def _pack_small(tensors):
    flat = jnp.concatenate([tensors[n].reshape(-1).astype(F32) for n in SMALL])
    padded = -(-flat.shape[0] // SMALL_PAD) * SMALL_PAD
    return jnp.pad(flat, (0, padded - flat.shape[0])).reshape(padded // SMALL_LANES, SMALL_LANES)


def _local_step(x, p, positions, loss_target, w, layer_weights, layer_done=None, layer_partial=None):
    s_len = x.shape[0]
    depth = p.shape[0]
    rope = _rope_tables(positions)
    gains = [{n: w[n][l].reshape(1, -1) for n in SMALL if w[n].ndim == 2 and n != "ssm_log_dt"} for l in range(depth)]
    ssm_args = lambda l: tuple(w[n][l] for n in ("ssm_lam_re", "ssm_lam_im", "ssm_log_dt", "ssm_b_re", "ssm_b_im",
                                                  "ssm_c_re", "ssm_c_im"))
    sps, pulls = [], []
    for l in range(depth):
        (a_cat, w_b, w_c), pull = jax.vjp(_ssm_params, *ssm_args(l))
        w_b = jnp.concatenate([t.astype(BF16) for t in w_b], axis=1)
        w_c = jnp.concatenate([t.astype(BF16) for t in w_c], axis=0)
        sps.append((a_cat, w_b, w_c, w["ssm_d"][l].reshape(1, D_SSM)))
        pulls.append(pull)

    (n,) = _rowwise("pre_norm", s_len, TM, [_rows(x, TM), _whole(gains[0]["ffn1_pre_g"])],
                    [_row_out(s_len, D_MODEL, BF16, TM)], lambda a, g: (_rms(a, g),))
    h = x
    saved = []
    for l in range(depth):
        g_next = gains[l + 1]["ffn1_pre_g"] if l + 1 < depth else gains[l]["ffn1_pre_g"]
        h, n, sv = _layer_fwd(s_len, h, n, p[l], rope, layer_weights(l, h), sps[l], gains[l], g_next)
        saved.append(sv)

    def loss_fn(y, t):
        e = y - t
        return e * (1.0 / D_MODEL), jnp.sum(e * e, axis=0, keepdims=True)

    dh, sq = _rowwise("loss", s_len, TM, [_rows(h, TM), _rows(loss_target, TM)],
                      [_row_out(s_len, D_MODEL, F32, TM), _col_out(D_MODEL)], loss_fn)
    loss = 0.5 * jnp.sum(sq) / D_MODEL

    w_grads, small_grads = [None] * depth, [None] * depth
    tie = None
    for l in reversed(range(depth)):
        g_l = gains[l] if tie is None else dict(gains[l], ple_post_g=gains[l]["ple_post_g"] + tie)
        partial = None if layer_partial is None else functools.partial(layer_partial, l)
        dh, wg, gr, (d_a, d_w_b, d_w_c) = _layer_bwd(s_len, dh, saved[l], rope, saved[l]["w"], sps[l], g_l, partial)
        d_lre, d_lim, d_dt, d_bre, d_bim, d_cre, d_cim = pulls[l]((d_a, d_w_b, d_w_c))
        gr.update(ssm_lam_re=d_lre, ssm_lam_im=d_lim, ssm_log_dt=d_dt, ssm_b_re=d_bre, ssm_b_im=d_bim,
                  ssm_c_re=d_cre, ssm_c_im=d_cim)
        gr = {n: g.reshape(w[n].shape[1:]) for n, g in gr.items()}
        w_grads[l], small_grads[l] = wg, gr
        if layer_done is not None:
            tie = layer_done(l, wg, gr, dh)
    return loss, dh, w_grads, small_grads


GROUP_OF = {"ffn1_w_gate": "g1", "ffn1_w_up": "u1", "ffn2_w_gate": "g2", "ffn2_w_up": "u2", "ffn1_w_down": "d1",
            "ffn2_w_down": "d2", "w_out": "out", "ple_w_gate": "pg", "w_in": "in", "ple_w_up": "pu", "ssm_w_glu": "glu"}
GROUPS = tuple(GROUP_OF[n] for n in SHARDED)
FIRST_PARTS = (("g1", "u1"), ("d1",), ("in", "glu", "out"), ("g2", "u2", "d2", "pu", "pg"))
_COLUMN_VIEWS = {"w_gu1": ("g1", "u1"), "w_gu2": ("g2", "u2"), "w_in": ("in",), "w_pu": ("pu",)}
_ROW_VIEWS = {"d1": "w_d1", "d2": "w_d2", "out": "w_out", "pg": "ple_w_gate", "glu": "w_glu"}


def _shard_groups(w, l):
    return {group: w[n][l].astype(BF16) for n, group in GROUP_OF.items()}


def _gathered_views(gathered, after=None):
    views = {}
    for name, groups in _COLUMN_VIEWS.items():
        if all(g in gathered for g in groups):
            views[name] = _natural_cols("relayout_" + name, [gathered[g] for g in groups], after)
            after = None
    for group, name in _ROW_VIEWS.items():
        if group in gathered:
            g = gathered[group]
            views[name] = g.reshape(N_DEV * g.shape[1], g.shape[2])
    return views


def _grad_groups(wg):
    rows = lambda t: t if t.ndim == 3 else t.reshape(N_DEV, t.shape[0] // N_DEV, t.shape[1])
    return {GROUP_OF[n]: rows(t) for n, t in wg.items()}


def _update_sharded(name, recv_groups, w, m, v, layer, prev, after=None):
    recv = recv_groups[GROUP_OF[name]]
    rows, cols = w.shape[1:]
    tr = rows if rows <= 256 else (rows // 2 if rows % 256 else 256)
    return _adamw(f"adamw_{name}", recv, (N_DEV, tr, cols), lambda i: (0, i, 0), w, m, v, tr, layer, prev, after)


def kernel(x, p, positions, ffn1_pre_g, ffn1_w_gate, ffn1_w_up, ffn1_w_down, ffn1_post_g, mix_pre_g, w_in, attn_norm_g, ssm_lam_re, ssm_lam_im, ssm_log_dt, ssm_b_re, ssm_b_im, ssm_c_re, ssm_c_im, ssm_d, ssm_w_glu, ssm_b_glu, ssm_norm_g, w_out, mix_post_g, ffn2_pre_g, ffn2_w_gate, ffn2_w_up, ffn2_w_down, ffn2_post_g, ple_w_up, ple_w_gate, ple_post_g, loss_target, m_ffn1_pre_g, m_ffn1_w_gate, m_ffn1_w_up, m_ffn1_w_down, m_ffn1_post_g, m_mix_pre_g, m_w_in, m_attn_norm_g, m_ssm_lam_re, m_ssm_lam_im, m_ssm_log_dt, m_ssm_b_re, m_ssm_b_im, m_ssm_c_re, m_ssm_c_im, m_ssm_d, m_ssm_w_glu, m_ssm_b_glu, m_ssm_norm_g, m_w_out, m_mix_post_g, m_ffn2_pre_g, m_ffn2_w_gate, m_ffn2_w_up, m_ffn2_w_down, m_ffn2_post_g, m_ple_w_up, m_ple_w_gate, m_ple_post_g, v_ffn1_pre_g, v_ffn1_w_gate, v_ffn1_w_up, v_ffn1_w_down, v_ffn1_post_g, v_mix_pre_g, v_w_in, v_attn_norm_g, v_ssm_lam_re, v_ssm_lam_im, v_ssm_log_dt, v_ssm_b_re, v_ssm_b_im, v_ssm_c_re, v_ssm_c_im, v_ssm_d, v_ssm_w_glu, v_ssm_b_glu, v_ssm_norm_g, v_w_out, v_mix_post_g, v_ffn2_pre_g, v_ffn2_w_gate, v_ffn2_w_up, v_ffn2_w_down, v_ffn2_post_g, v_ple_w_up, v_ple_w_gate, v_ple_post_g):
    args = dict(locals())
    w = {n: args[n] for n in WEIGHTS}
    mom = {n: args["m_" + n] for n in WEIGHTS}
    var = {n: args["v_" + n] for n in WEIGHTS}
    depth = p.shape[0]

    gathers, scatters, recv, tokens = {}, {}, {}, {}

    def start(name, groups, names, scatter, order_after):
        flags = scatter if isinstance(scatter, tuple) else (scatter,) * len(names)
        handle, token = _exchange_start(name, [groups[n] for n in names], flags, order_after)
        return (handle, names, flags), token

    def finish(name, pending, order_after):
        handle, names, flags = pending
        return dict(zip(names, _exchange_wait(name, handle, flags, order_after)))

    def gather_next(l, order_after):
        if l + 1 >= depth:
            return None
        gathers[l + 1], token = start("all_gather_start", _shard_groups(w, l + 1), GROUPS, False, order_after)
        return token

    def layer_weights(l, h):
        views = {}
        if l > 0:
            got = finish("all_gather_wait", gathers.pop(l), h)
            views.update(_gathered_views(got, gather_next(l, got[GROUPS[0]])))
            return lambda name, after: views[name]

        shards = _shard_groups(w, 0)
        pending, token = start("all_gather_start_0", shards, FIRST_PARTS[0], False, h)
        got = finish("all_gather_wait_0", pending, token)
        waiting = {}
        token = got[FIRST_PARTS[0][0]]
        for i, part in enumerate(FIRST_PARTS[1:], 1):
            waiting[i], token = start(f"all_gather_start_{i}", shards, part, False, token)
        views.update(_gathered_views(got, gather_next(0, token)))

        def weight(name, after):
            for i in sorted(waiting):
                if name not in views:
                    views.update(_gathered_views(finish(f"all_gather_wait_{i}", waiting.pop(i), after)))
            return views[name]

        return weight

    small_l = lambda t, l: _pack_small({n: t[n][l] for n in SMALL})
    scatter_flags = lambda names: tuple(n != "small" for n in names)
    late = ("g1", "u1", "d1", "small")
    early = tuple(g for g in GROUPS if g not in late)

    def layer_partial(l, wg):
        if l > 0:
            return None
        scatters["early"], token = start("reduce_scatter_start_a", _grad_groups(wg), early, scatter_flags(early),
                                         wg["w_in"])
        return token[0, 0]

    def layer_done(l, wg, gr, dh):
        if l + 1 < depth:
            recv[l + 1] = finish("reduce_scatter_wait", scatters.pop(l + 1), dh)
        groups = dict(_grad_groups(wg), small=_pack_small(gr))
        names = late if l == 0 else GROUPS + ("small",)
        scatters[l], tokens[l] = start("reduce_scatter_start_b" if l == 0 else "reduce_scatter_start", groups, names,
                                       scatter_flags(names), dh)
        return tokens[l][0, 0]

    loss, dx, w_grads, small_grads = _local_step(x[0], p[:, 0], positions[0], loss_target[0], w, layer_weights,
                                                 layer_done, layer_partial)
    loss = lax.psum(loss, MESH_AXES)

    results = {n: None for n in SHARDED}
    last = tokens[0]
    for l in reversed(range(1, depth)):
        for n in SHARDED:
            *results[n], last = _update_sharded(n, recv[l], w[n], mom[n], var[n], l, results[n], last)
    recv[0] = dict(finish("reduce_scatter_wait_a", scatters.pop("early"), last))
    recv[0].update(finish("reduce_scatter_wait_b", scatters.pop(0), last))
    for n in SHARDED:
        results[n] = _update_sharded(n, recv[0], w[n], mom[n], var[n], 0, results[n])
    result = dict(results)

    recv_small = jnp.concatenate([recv[l]["small"] for l in range(depth)], axis=1)
    packed = [jnp.concatenate([small_l(t, l) for l in range(depth)], axis=0) for t in (w, mom, var)]
    small_out = _adamw("adamw_small", recv_small, (N_DEV, 8, SMALL_LANES), lambda i: (0, i, 0), *packed, 8)
    for q in range(4):
        flat = small_out[q].reshape(depth, -1)
        off = 0
        for n in SMALL:
            size = math.prod(w[n].shape[1:])
            result.setdefault(n, [None] * 4)[q] = flat[:, off:off + size].reshape(w[n].shape)
            off += size

    outputs = [loss, dx[None]]
    for q in range(4):
        outputs += [result[n][q] for n in WEIGHTS]
    return tuple(outputs)
```

```python
import functools
import math

import jax
import jax.numpy as jnp
from jax import lax
from jax.experimental import pallas as pl
from jax.experimental.pallas import tpu as pltpu

F32 = jnp.float32
BF16 = jnp.bfloat16

N_DEV = 8
D_MODEL = 1024
D_FF = 2816
FF_SHARD = D_FF // N_DEV
D_ATTN = 512
D_SSM = 512
HEAD_DIM = 64
N_HEADS = 8
ROPE_DIM = 16
ROPE_THETA = 500000.0
DILATIONS = (1, 4, 16)
BAND = 128
N_GROUPS = 32
SSM_GROUP = 16
SSM_STATE = 64
N_STATE = N_GROUPS * SSM_STATE
PLE_DIM = 256
NORM_EPS = 1e-6
ADAM_LR, ADAM_B1, ADAM_B2, ADAM_EPS, ADAM_WD, ADAM_STEP = 0.001, 0.9, 0.999, 1e-08, 0.01, 10

VMEM_LIMIT_BYTES = 52 * 1024 * 1024
TM = 512
MESH_AXES = ("x", "y", "c")

WEIGHTS = ['ffn1_pre_g', 'ffn1_w_gate', 'ffn1_w_up', 'ffn1_w_down', 'ffn1_post_g', 'mix_pre_g', 'w_in', 'attn_norm_g',
           'ssm_lam_re', 'ssm_lam_im', 'ssm_log_dt', 'ssm_b_re', 'ssm_b_im', 'ssm_c_re', 'ssm_c_im', 'ssm_d',
           'ssm_w_glu', 'ssm_b_glu', 'ssm_norm_g', 'w_out', 'mix_post_g', 'ffn2_pre_g', 'ffn2_w_gate', 'ffn2_w_up',
           'ffn2_w_down', 'ffn2_post_g', 'ple_w_up', 'ple_w_gate', 'ple_post_g']
SHARDED = ['ffn1_w_gate', 'ffn1_w_up', 'ffn1_w_down', 'w_in', 'ssm_w_glu', 'w_out', 'ffn2_w_gate', 'ffn2_w_up',
           'ffn2_w_down', 'ple_w_up', 'ple_w_gate']
SMALL = [n for n in WEIGHTS if n not in SHARDED]
SMALL_LANES = 1024
SMALL_PAD = 8 * SMALL_LANES


def _params(n_axes):
    return pltpu.CompilerParams(dimension_semantics=("arbitrary",) * n_axes, vmem_limit_bytes=VMEM_LIMIT_BYTES)


_DIMS = {"nn": (((1,), (0,)), ((), ())), "nt": (((1,), (1,)), ((), ())), "tn": (((0,), (0,)), ((), ()))}


def _dot(a, b, mode):
    return lax.dot_general(a.astype(BF16), b.astype(BF16), _DIMS[mode], preferred_element_type=F32)


def _store(out_refs, vals, kinds, first):
    for ref, val, kind in zip(out_refs, vals, kinds):
        if isinstance(val, (list, tuple)):
            for q, piece in enumerate(val):
                ref[q] = piece.astype(ref.dtype)
        elif kind == "row":
            ref[...] = val.astype(ref.dtype)
        else:
            @pl.when(first)
            def _(ref=ref, val=val):
                ref[...] = val.astype(ref.dtype)

            @pl.when(jnp.logical_not(first))
            def _(ref=ref, val=val):
                ref[...] += val.astype(ref.dtype)


def _call(name, body, grid, ins, outs, scratch, into):
    arrays = [t[0] for t in ins]
    in_specs = [_in_spec(t) for t in ins]
    aliases = {}
    for arr, k in into:
        arrays.append(arr)
        in_specs.append(pl.BlockSpec(memory_space=pl.ANY))
        aliases[len(arrays) - 1] = k
    return pl.pallas_call(
        body,
        name=name,
        grid=grid,
        in_specs=in_specs,
        out_specs=[pl.BlockSpec(o[2], o[3]) for o in outs],
        out_shape=[jax.ShapeDtypeStruct(o[0], o[1]) for o in outs],
        scratch_shapes=list(scratch),
        input_output_aliases=aliases,
        compiler_params=_params(len(grid)),
    )(*arrays)


def _mm(name, grid, a, b, mode, outs, extras=(), epilogue=None, acc_shape=None, scratch=(), into=(), more=(),
        a_prep=None, b_prep=None):
    prep_a = a_prep if a_prep is not None else (lambda t: t)
    prep_b = b_prep if b_prep is not None else (lambda t: t)
    nk = grid[2]
    ne, no = len(extras), len(outs)
    kinds = [o[4] for o in outs]
    assert all(k == "row" for k in kinds) or grid[1] == 1
    n_ab = 2 + 2 * len(more)
    n_in = n_ab + ne + len(into)
    n_acc = int(nk > 1)

    def body(*refs):
        ex_refs = refs[n_ab:n_ab + ne]
        out_refs = refs[n_in:n_in + no]
        scr_refs = refs[n_in + no + n_acc:]
        part = _dot(prep_a(refs[0][...]), prep_b(refs[1][...]), mode)
        for p in range(2, n_ab, 2):
            part = part + _dot(prep_a(refs[p][...]), prep_b(refs[p + 1][...]), mode)
        first = pl.program_id(0) == 0

        def finish(acc):
            vals = epilogue(acc, *[r[...] for r in ex_refs], *scr_refs) if epilogue is not None else (acc,)
            _store(out_refs, vals, kinds, first)

        if nk == 1:
            finish(part)
        else:
            acc_ref = refs[n_in + no]
            k = pl.program_id(2)

            @pl.when(k == 0)
            def _():
                acc_ref[...] = part

            @pl.when(k > 0)
            def _():
                acc_ref[...] += part

            @pl.when(k == nk - 1)
            def _():
                finish(acc_ref[...])

    acc = [pltpu.VMEM(acc_shape, F32)] if nk > 1 else []
    pairs = tuple(t for pair in more for t in pair)
    return _call(name, body, grid, (a, b) + pairs + tuple(extras), outs, acc + list(scratch), into)


def _rowwise(name, n_rows, tm, ins, outs, fn, scratch=(), into=()):
    kinds = [o[4] for o in outs]
    ni, no = len(ins), len(outs)
    n_in = ni + len(into)

    def body(*refs):
        vals = fn(*[r[...] for r in refs[:ni]], *refs[n_in + no:])
        _store(refs[n_in:n_in + no], vals, kinds, pl.program_id(0) == 0)

    return _call(name, body, (n_rows // tm,), ins, outs, scratch, into)


def _rows(arr, tm, col=0, width=None):
    width = arr.shape[1] if width is None else width
    return (arr, (tm, width), lambda i, *_: (i, col))


def _whole(arr):
    nd = arr.ndim
    return (arr, arr.shape, lambda *_: (0,) * nd)


def _resident(arr):
    nd = arr.ndim
    return (arr, arr.shape, lambda *_: (0,) * nd, dict(pipeline_mode=pl.Buffered(1)))


def _in_spec(t):
    return pl.BlockSpec(t[1], t[2], **(t[3] if len(t) > 3 else {}))


def _row_out(n_rows, width, dtype, tm, col=0, total=None):
    return ((n_rows, width if total is None else total), dtype, (tm, width), lambda i, *_: (i, col), "row")


def _col_out(width):
    return ((1, width), F32, (1, width), lambda *_: (0, 0), "colsum")


def _lane_blocks(acc):
    return [acc[:, q * LANES:(q + 1) * LANES] for q in range(acc.shape[1] // LANES)]


def _join_lanes(blk):
    return jnp.concatenate([blk[q] for q in range(blk.shape[0])], axis=1)


def _column_shards(acc, width):
    return [acc[:, q * width:(q + 1) * width] for q in range(acc.shape[1] // width)]


def _natural_cols(name, parts, after=None):
    rows, width = parts[0].shape[1:]
    tr = min(rows, 256)
    n = len(parts)

    def body(*refs):
        out_ref = refs[-1]
        for t in range(n):
            for j in range(N_DEV):
                lo = (t * N_DEV + j) * width
                out_ref[:, lo:lo + width] = refs[t][j]

    ins = [(t, (N_DEV, tr, width), lambda i: (0, i, 0)) for t in parts]
    if after is not None:
        ins.append(_whole(after))
    out = _row_out(rows, n * N_DEV * width, parts[0].dtype, tr)
    return _call(name, body, (rows // tr,), ins, [out], (), ())[0]


def _rms(x, g):
    return x * lax.rsqrt(jnp.mean(x * x, axis=-1, keepdims=True) + NORM_EPS) * g


def _sigmoid(x):
    return 1.0 / (1.0 + jnp.exp(-x))


def _gelu(x):
    return 0.5 * x * (1.0 + jnp.tanh(0.7978845608028654 * (x + 0.044715 * x * x * x)))


FFN_TM = 256
FF_HALF = D_FF // 2


def _ffn_fwd(s_len, h, n, weight, g_post, g_next):
    def gu_epi(acc):
        gate, up = acc[:, :D_FF], acc[:, D_FF:]
        return acc, gate * _sigmoid(gate) * up

    w_gu = weight("gu", n)
    gu, act = _mm("ffn_gate_up", (s_len // FFN_TM, 1, 1), _rows(n, FFN_TM), _resident(w_gu), "nn",
                  [_row_out(s_len, 2 * D_FF, BF16, FFN_TM), _row_out(s_len, D_FF, BF16, FFN_TM)], epilogue=gu_epi)
    w_d = weight("d", act)

    def down_epi(acc, h_blk, gp, gn):
        h_new = h_blk + 0.5 * _rms(acc, gp)
        return acc, h_new, _rms(h_new, gn)

    f, h_new, n_next = _mm(
        "ffn_down", (s_len // TM, 1, 1), _rows(act, TM), _resident(w_d), "nn",
        [_row_out(s_len, D_MODEL, F32, TM), _row_out(s_len, D_MODEL, F32, TM), _row_out(s_len, D_MODEL, BF16, TM)],
        extras=[_rows(h, TM), _whole(g_post), _whole(g_next)], epilogue=down_epi)
    return h_new, n_next, dict(n=n, gu=gu, act=act, f=f, h=h)


def _ffn_bwd(s_len, saved, df, w_gu, w_d, final_epi, final_extras, final_outs):
    nt = s_len // TM

    def act_epi(acc, gu_blk):
        gate = gu_blk[:, :D_FF].astype(F32)
        up = gu_blk[:, D_FF:].astype(F32)
        sg = _sigmoid(gate)
        dgate = acc * up * sg * (1.0 + gate * (1.0 - sg))
        dup = acc * gate * sg
        return (jnp.concatenate([dgate, dup], axis=1),)

    (dgu,) = _mm("ffn_bwd_act", (s_len // FFN_TM, 1, 1), _rows(df, FFN_TM), _resident(w_d), "nt",
                 [_row_out(s_len, 2 * D_FF, BF16, FFN_TM)], extras=[_rows(saved["gu"], FFN_TM)], epilogue=act_epi)

    (d_w_d,) = _mm(
        "ffn_bwd_wdown", (2, 1, nt), (saved["act"], (TM, FF_HALF), lambda i, j, k: (k, i)),
        (df, (TM, D_MODEL), lambda i, j, k: (k, 0)), "tn",
        [((D_FF, D_MODEL), BF16, (FF_HALF, D_MODEL), lambda i, j, k: (i, 0), "row")], acc_shape=(FF_HALF, D_MODEL))

    def w_grad(name, half):
        (dw,) = _mm(
            name, (1, 2, nt), (saved["n"], (TM, D_MODEL), lambda i, j, k: (k, 0)),
            (dgu, (TM, FF_HALF), lambda i, j, k: (k, 2 * half + j)), "tn",
            [((N_DEV, D_MODEL, FF_SHARD), BF16, (N_DEV // 2, D_MODEL, FF_SHARD), lambda i, j, k: (j, 0, 0), "row")],
            epilogue=lambda acc: (_column_shards(acc, FF_SHARD),), acc_shape=(D_MODEL, FF_HALF))
        return dw

    d_w_gate, d_w_up = w_grad("ffn_bwd_wgate", 0), w_grad("ffn_bwd_wup", 1)

    outs = _mm("ffn_bwd_dn", (s_len // FFN_TM, 1, 1), _rows(dgu, FFN_TM), _resident(w_gu), "nt",
               final_outs, extras=final_extras, epilogue=final_epi)
    return d_w_gate, d_w_up, d_w_d, outs


def _band_mask(first_block):
    qi = lax.broadcasted_iota(jnp.int32, (BAND, 2 * BAND), 0)
    kj = lax.broadcasted_iota(jnp.int32, (BAND, 2 * BAND), 1)
    ok = (kj >= qi) & (kj <= qi + BAND)
    return ok & (jnp.logical_not(first_block) | (kj >= BAND))


def _attn_views(s_len, d):
    n_str = s_len // d
    nb = n_str // BAND
    blk = (BAND, D_ATTN)
    cur = lambda r, b: (b, r)
    prev = lambda r, b: (jnp.maximum(b - 1, 0), r)
    return n_str, nb, blk, cur, prev


STREAM_LANES = 128


def _stream_scratch(tm):
    return pltpu.VMEM((D_ATTN // STREAM_LANES, tm, STREAM_LANES), F32)


def _to_streams(x, d, scr):
    if d == 1:
        return x
    tm = x.shape[0]
    chunks = D_ATTN // STREAM_LANES
    for c in range(chunks):
        scr[c, 0:tm, :] = x[:, c * STREAM_LANES:(c + 1) * STREAM_LANES]
    return jnp.concatenate([scr.at[c][pl.ds(r, tm // d, stride=d), :] for r in range(d) for c in range(chunks)], axis=1)


def _from_streams(v, d, scr):
    if d == 1:
        return v
    rows = v.shape[0]
    chunks = D_ATTN // STREAM_LANES
    for r in range(d):
        for c in range(chunks):
            lo = r * D_ATTN + c * STREAM_LANES
            scr.at[c][pl.ds(r, rows, stride=d), :] = v[:, lo:lo + STREAM_LANES]
    return jnp.concatenate([scr[c, 0:rows * d, :] for c in range(chunks)], axis=1)


def _stream_in(arr, tm, d, ahead=0, n_blocks=None):
    rows = tm // d
    step = ahead // rows
    if ahead:
        return (arr, (rows, d * D_ATTN), lambda i, *_: (jnp.minimum(i + step, n_blocks - 1), 0))
    return (arr, (rows, d * D_ATTN), lambda i, *_: (i, 0))


def _stream_out(s_len, dtype, tm, d):
    return ((s_len // d, d * D_ATTN), dtype, (tm // d, d * D_ATTN), lambda i, *_: (i, 0), "row")


def _attn_fwd(s_len, q, k, v, d):
    n_str, nb, blk, cur, prev = _attn_views(s_len, d)
    view = lambda t: t

    def body(q_ref, kp_ref, kc_ref, vp_ref, vc_ref, o_ref, lse_ref):
        mask = _band_mask(pl.program_id(1) == 0)
        even = lax.broadcasted_iota(jnp.int32, (BAND, 2 * HEAD_DIM), 1) < HEAD_DIM
        qq = q_ref[...]
        kk = jnp.concatenate([kp_ref[...], kc_ref[...]], axis=0)
        vv = jnp.concatenate([vp_ref[...], vc_ref[...]], axis=0)
        for pair in range(N_HEADS // 2):
            sl = slice(2 * pair * HEAD_DIM, 2 * (pair + 1) * HEAD_DIM)
            q_p, k_p, v_p = qq[:, sl] * (HEAD_DIM ** -0.5), kk[:, sl], vv[:, sl]
            o, lse = [], []
            for half in range(2):
                sel = even if half == 0 else jnp.logical_not(even)
                s = _dot(jnp.where(sel, q_p, jnp.zeros_like(q_p)), k_p, "nt")
                s = jnp.where(mask, s, -1e30)
                m = jnp.max(s, axis=-1, keepdims=True)
                e = jnp.exp(s - m)
                den = jnp.sum(e, axis=-1, keepdims=True)
                o.append(_dot(e / den, v_p, "nn"))
                lse.append(m + jnp.log(den))
            o_ref[:, sl] = jnp.where(even, o[0], o[1])
            lse_ref[:, sl] = jnp.where(even, lse[0], lse[1])

    o, lse = pl.pallas_call(
        body,
        name=f"attn_fwd_d{d}",
        grid=(d, nb),
        in_specs=[pl.BlockSpec(blk, cur), pl.BlockSpec(blk, prev), pl.BlockSpec(blk, cur),
                  pl.BlockSpec(blk, prev), pl.BlockSpec(blk, cur)],
        out_specs=[pl.BlockSpec(blk, cur), pl.BlockSpec(blk, cur)],
        out_shape=[jax.ShapeDtypeStruct((n_str, d * D_ATTN), F32)] * 2,
        compiler_params=_params(2),
    )(view(q), view(k), view(k), view(v), view(v))
    return o, lse


def _attn_bwd(s_len, q, k, v, dattn, attn, lse, d):
    n_str, nb, blk, cur, prev = _attn_views(s_len, d)
    view = lambda t: t

    def body(q_ref, kp_ref, kc_ref, vp_ref, vc_ref, da_ref, at_ref, lse_ref, dq_ref, dka_ref, dkb_ref, dva_ref, dvb_ref):
        mask = _band_mask(pl.program_id(1) == 0)
        even = lax.broadcasted_iota(jnp.int32, (BAND, 2 * HEAD_DIM), 1) < HEAD_DIM
        qq = q_ref[...]
        kk = jnp.concatenate([kp_ref[...], kc_ref[...]], axis=0)
        vv = jnp.concatenate([vp_ref[...], vc_ref[...]], axis=0)
        da = da_ref[...]
        prod = da * at_ref[...]
        da = da.astype(BF16)
        scale = HEAD_DIM ** -0.5
        for pair in range(N_HEADS // 2):
            sl = slice(2 * pair * HEAD_DIM, 2 * (pair + 1) * HEAD_DIM)
            q_p, k_p, v_p, da_p, prod_p = qq[:, sl] * scale, kk[:, sl], vv[:, sl], da[:, sl], prod[:, sl]
            dq, dk, dv = [], 0.0, 0.0
            for half in range(2):
                sel = even if half == 0 else jnp.logical_not(even)
                q_h = jnp.where(sel, q_p, jnp.zeros_like(q_p))
                da_h = jnp.where(sel, da_p, jnp.zeros_like(da_p))
                lse_col = lse_ref[:, (2 * pair + half) * HEAD_DIM:(2 * pair + half) * HEAD_DIM + 1]
                d_col = jnp.sum(jnp.where(sel, prod_p, 0.0), axis=-1, keepdims=True)
                s = _dot(q_h, k_p, "nt")
                p = jnp.where(mask, jnp.exp(s - lse_col), 0.0)
                ds = p * (_dot(da_h, v_p, "nt") - d_col)
                dq.append(_dot(ds, k_p, "nn") * scale)
                dk = dk + _dot(ds, q_h, "tn")
                dv = dv + _dot(p, da_h, "tn")
            dq_ref[:, sl] = jnp.where(even, dq[0], dq[1]).astype(dq_ref.dtype)
            dkb_ref[:, sl] = dk[:BAND].astype(dkb_ref.dtype)
            dka_ref[:, sl] = dk[BAND:].astype(dka_ref.dtype)
            dvb_ref[:, sl] = dv[:BAND].astype(dvb_ref.dtype)
            dva_ref[:, sl] = dv[BAND:].astype(dva_ref.dtype)

    outs = pl.pallas_call(
        body,
        name=f"attn_bwd_d{d}",
        grid=(d, nb),
        in_specs=[pl.BlockSpec(blk, cur), pl.BlockSpec(blk, prev), pl.BlockSpec(blk, cur),
                  pl.BlockSpec(blk, prev), pl.BlockSpec(blk, cur),
                  pl.BlockSpec(blk, cur), pl.BlockSpec(blk, cur), pl.BlockSpec(blk, cur)],
        out_specs=[pl.BlockSpec(blk, cur)] * 5,
        out_shape=[jax.ShapeDtypeStruct((n_str, d * D_ATTN), BF16)] * 5,
        compiler_params=_params(2),
    )(view(q), view(k), view(k), view(v), view(v), view(dattn), view(attn), view(lse))
    return list(outs)


def _rope_tables(positions):
    half = ROPE_DIM // 2
    inv_freq = ROPE_THETA ** (-jnp.arange(half, dtype=F32) * (2.0 / ROPE_DIM))
    ang = positions.astype(F32)[:, None] * inv_freq
    cos, sin = jnp.cos(ang), jnp.sin(ang)
    s_len = positions.shape[0]
    one = jnp.ones((s_len, HEAD_DIM - ROPE_DIM), F32)
    zero8 = jnp.zeros((s_len, half), F32)
    zero = jnp.zeros((s_len, HEAD_DIM - ROPE_DIM), F32)
    c = jnp.concatenate([cos, cos, one], axis=1)
    s1 = jnp.concatenate([zero8, sin, zero], axis=1)
    s2 = jnp.concatenate([-sin, zero8, zero], axis=1)
    tile = lambda t: jnp.tile(t, (1, N_HEADS))
    return tile(c), tile(s1), tile(s2)


def _rope(t, c, s1, s2):
    half = ROPE_DIM // 2
    return t * c + pltpu.roll(t, half, 1) * s1 + pltpu.roll(t, D_ATTN - half, 1) * s2


def _rope_transposed(dt, c, s1, s2):
    half = ROPE_DIM // 2
    return dt * c + pltpu.roll(dt * s1, D_ATTN - half, 1) + pltpu.roll(dt * s2, half, 1)


SCAN_ROWS = 256
SCAN_CHUNK = 128
LANES = 128
SUBLANES = 8


def _cmul(xr, xi, yr, yi):
    return xr * yr - xi * yi, xr * yi + xi * yr


def _scan(name, bu, a_cat, reverse, prev_state=None):
    s_len = bu.shape[1]
    n_blocks = bu.shape[0] // 2
    n_chunks = s_len // SCAN_CHUNK
    log_chunk = SCAN_CHUNK.bit_length() - 1
    rows_of = lambda off: pl.ds(pl.multiple_of(off * n_chunks, n_chunks), n_chunks)

    with_da = prev_state is not None

    def body(*refs):
        br_ref, bi_ref, ar_ref, ai_ref = refs[:4]
        at = 4
        if with_da:
            sr_ref, si_ref = refs[at:at + 2]
            at += 2
        xr_ref, xi_ref = refs[at:at + 2]
        at += 2
        if with_da:
            dar_ref, dai_ref = refs[at:at + 2]
            at += 2
        pr_ref, pi_ref, cr_ref, ci_ref = refs[at:at + 4]
        ar, ai = ar_ref[...], ai_ref[...]

        def step(i, state):
            rows = rows_of((SCAN_CHUNK - 1 - i) if reverse else i)
            pr, pi = _cmul(ar, ai, *state)
            nr, ni = pr + br_ref[rows, :], pi + bi_ref[rows, :]
            xr_ref[rows, :] = nr
            xi_ref[rows, :] = ni
            return nr, ni

        zero = jnp.zeros((n_chunks, LANES), F32)
        lax.fori_loop(0, SCAN_CHUNK, step, (zero, zero), unroll=4)

        pw = [(ar, ai)]
        for _ in range(log_chunk + n_chunks.bit_length()):
            pw.append(_cmul(*pw[-1], *pw[-1]))

        last = pl.ds((0 if reverse else SCAN_CHUNK - 1) * n_chunks, n_chunks)
        er, ei = xr_ref[last, :], xi_ref[last, :]
        chunk = lax.broadcasted_iota(jnp.int32, (n_chunks, LANES), 0)

        def shifted(v, s):
            if reverse:
                return jnp.where(chunk < n_chunks - s, pltpu.roll(v, n_chunks - s, 0), 0.0)
            return jnp.where(chunk >= s, pltpu.roll(v, s, 0), 0.0)

        s, b = 1, log_chunk
        while s < n_chunks:
            mr, mi = _cmul(*pw[b], shifted(er, s), shifted(ei, s))
            er, ei = er + mr, ei + mi
            s, b = 2 * s, b + 1
        cr_ref[...] = shifted(er, 1)
        ci_ref[...] = shifted(ei, 1)

        step_no = lax.broadcasted_iota(jnp.int32, (SUBLANES, LANES), 0)
        expo = (SUBLANES - step_no) if reverse else (step_no + 1)
        qr, qi = jnp.ones((SUBLANES, LANES), F32), jnp.zeros((SUBLANES, LANES), F32)
        for bit in range(4):
            mr, mi = _cmul(qr, qi, *pw[bit])
            hit = (expo & (1 << bit)) != 0
            qr, qi = jnp.where(hit, mr, qr), jnp.where(hit, mi, qi)
        lo = SCAN_CHUNK - SUBLANES if reverse else 0
        pr_ref[lo:lo + SUBLANES, :] = qr
        pi_ref[lo:lo + SUBLANES, :] = qi
        m, b = SUBLANES, 3
        while m < SCAN_CHUNK:
            src = pl.ds(SCAN_CHUNK - m, m) if reverse else pl.ds(0, m)
            dst = pl.ds(SCAN_CHUNK - 2 * m, m) if reverse else pl.ds(m, m)
            mr, mi = _cmul(pr_ref[src, :], pi_ref[src, :], *pw[b])
            pr_ref[dst, :] = mr
            pi_ref[dst, :] = mi
            m, b = 2 * m, b + 1

        cr, ci = cr_ref[...], ci_ref[...]

        if with_da:
            ends = rows_of(SCAN_CHUNK - 1)
            wrap_r = jnp.where(chunk == 0, 0.0, pltpu.roll(sr_ref[ends, :], 1, 0))
            wrap_i = jnp.where(chunk == 0, 0.0, pltpu.roll(si_ref[ends, :], 1, 0))

        def fix(i, acc):
            rows = rows_of(i)
            mr, mi = _cmul(pr_ref[pl.ds(i, 1), :], pi_ref[pl.ds(i, 1), :], cr, ci)
            gr, gi = xr_ref[rows, :] + mr, xi_ref[rows, :] + mi
            xr_ref[rows, :] = gr
            xi_ref[rows, :] = gi
            if not with_da:
                return acc
            before = rows_of(jnp.maximum(i - 1, 0))
            qr = jnp.where(i == 0, wrap_r, sr_ref[before, :])
            qi = jnp.where(i == 0, wrap_i, si_ref[before, :])
            return acc[0] + qr * gr + qi * gi, acc[1] + qr * gi - qi * gr

        sums = lax.fori_loop(0, SCAN_CHUNK, fix, (zero, zero), unroll=4)
        if with_da:
            dar_ref[...] = jnp.sum(sums[0], axis=0, keepdims=True)
            dai_ref[...] = jnp.sum(sums[1], axis=0, keepdims=True)

    blk = lambda off: pl.BlockSpec((None, s_len, LANES), lambda c: (off + c, 0, 0))
    col = lambda off: pl.BlockSpec((1, LANES), lambda c: (0, off + c))
    one = pl.BlockSpec((None, 1, LANES), lambda c: (c, 0, 0))
    extra_in = [blk(0), blk(0)] if with_da else []
    extra_out = [one, one] if with_da else []
    return pl.pallas_call(
        body,
        name=name,
        grid=(n_blocks,),
        in_specs=[blk(0), blk(n_blocks), col(0), col(n_blocks)] + extra_in,
        out_specs=[blk(0), blk(0)] + extra_out,
        out_shape=[jax.ShapeDtypeStruct((n_blocks, s_len, LANES), F32)] * 2
        + [jax.ShapeDtypeStruct((n_blocks, 1, LANES), F32)] * len(extra_out),
        scratch_shapes=[pltpu.VMEM((SCAN_CHUNK, LANES), F32)] * 2 + [pltpu.VMEM((n_chunks, LANES), F32)] * 2,
        compiler_params=_params(1),
    )(bu, bu, a_cat, a_cat, *(prev_state if with_da else ()))


def _to_chunk_order(t):
    s_len, width = t.shape
    return t.reshape(s_len // SCAN_CHUNK, SCAN_CHUNK, width).transpose(1, 0, 2).reshape(s_len, width)


def _from_chunk_order(t):
    s_len, width = t.shape
    return t.reshape(SCAN_CHUNK, s_len // SCAN_CHUNK, width).transpose(1, 0, 2).reshape(s_len, width)


def _state_in(arr, tm):
    return (arr, (arr.shape[0], tm, LANES), lambda i, *_: (0, i, 0))


def _state_out(s_len, width, tm):
    n = width // LANES
    return ((n, s_len, LANES), F32, (n, tm, LANES), lambda i, *_: (0, i, 0), "row")


@jax.custom_vjp
def _block_diag(blocks):
    g, r, c = blocks.shape
    eye = jnp.eye(g, dtype=blocks.dtype)
    return (blocks[:, :, None, :] * eye[:, None, :, None]).reshape(g * r, g * c)


def _block_diag_fwd(blocks):
    return _block_diag(blocks), blocks.shape


def _block_diag_bwd(shape, ct):
    g, r, c = shape
    per = max(LANES // c, 1)

    def body(ct_ref, out_ref):
        for k in range(per):
            out_ref[k] = ct_ref[k * r:(k + 1) * r, k * c:(k + 1) * c]

    return (pl.pallas_call(
        body,
        name=f"diag_blocks_{r}x{c}",
        grid=(g // per,),
        in_specs=[pl.BlockSpec((per * r, per * c), lambda i: (i, i))],
        out_specs=pl.BlockSpec((per, r, c), lambda i: (i, 0, 0)),
        out_shape=jax.ShapeDtypeStruct((g, r, c), ct.dtype),
        compiler_params=_params(1),
    )(ct),)


_block_diag.defvjp(_block_diag_fwd, _block_diag_bwd)


def _ssm_params(lam_re, lam_im, log_dt, b_re, b_im, c_re, c_im):
    dt = jnp.exp(log_dt)[:, None]
    er = jnp.exp(lam_re * dt)
    a_re = er * jnp.cos(lam_im * dt)
    a_im = er * jnp.sin(lam_im * dt)
    nr, ni = a_re - 1.0, a_im
    den = lam_re * lam_re + lam_im * lam_im
    fr = (nr * lam_re + ni * lam_im) / den
    fi = (ni * lam_re - nr * lam_im) / den
    bb_re = fr[..., None] * b_re - fi[..., None] * b_im
    bb_im = fr[..., None] * b_im + fi[..., None] * b_re
    in_mat = lambda bb: _block_diag(bb.transpose(0, 2, 1))
    out_mat = lambda cc: _block_diag(cc.transpose(0, 2, 1))
    w_b = (in_mat(bb_re), in_mat(bb_im))
    w_c = (out_mat(c_re), -out_mat(c_im))
    a_cat = jnp.concatenate([a_re.reshape(1, N_STATE), a_im.reshape(1, N_STATE)], axis=1)
    return a_cat, w_b, w_c


def _conj(a_cat):
    return jnp.concatenate([a_cat[:, :N_STATE], -a_cat[:, N_STATE:]], axis=1)


def _layer_fwd(s_len, h, n1, p_l, rope, weight, sp, gains, g_next):
    nt = s_len // TM
    sv = {}
    gw = {}

    def need(after, *names):
        for name in names:
            gw[name] = weight(name, after)

    def ffn_weight(which):
        def get(kind, after):
            need(after, f"w_{kind}{which}")
            return gw[f"w_{kind}{which}"]
        return get

    h1, a_in, sv["ffn1"] = _ffn_fwd(s_len, h, n1, ffn_weight(1), gains["ffn1_post_g"], gains["mix_pre_g"])

    need(h1, "w_in", "w_glu", "w_out")
    (proj,) = _mm("w_in", (nt, 1, 1), _rows(a_in, TM), _resident(gw["w_in"]), "nn",
                  [_row_out(s_len, 2 * D_MODEL, F32, TM)])

    c, s1, s2 = rope

    def rope_fn(tq, tk, tv, cc, a1, a2, scr):
        q, k = _rope(tq, cc, a1, a2), _rope(tk, cc, a1, a2)
        return [_to_streams(t, d, scr) for d in DILATIONS for t in (q, k, tv)]

    qkv = _rowwise(
        "rope", s_len, TM,
        [_rows(proj, TM, 0, D_ATTN), _rows(proj, TM, 1, D_ATTN), _rows(proj, TM, 2, D_ATTN),
         _rows(c, TM), _rows(s1, TM), _rows(s2, TM)],
        [_stream_out(s_len, BF16, TM, d) for d in DILATIONS for _ in range(3)], rope_fn,
        scratch=[_stream_scratch(TM)])
    qkv = {d: qkv[3 * i:3 * i + 3] for i, d in enumerate(DILATIONS)}
    parts = {d: _attn_fwd(s_len, *qkv[d], d) for d in DILATIONS}

    def mix_fn(*args):
        g, scr = args[6], args[7]
        o1, l1, o2, l2, o3, l3 = [_from_streams(args[2 * i + j], d, scr) for i, d in enumerate(DILATIONS)
                                  for j in range(2)]
        m = jnp.maximum(jnp.maximum(l1, l2), l3)
        e1, e2, e3 = jnp.exp(l1 - m), jnp.exp(l2 - m), jnp.exp(l3 - m)
        tot = e1 + e2 + e3
        attn = (e1 * o1 + e2 * o2 + e3 * o3) / tot
        lse = m + jnp.log(tot)
        return [attn, lse, _rms(attn, g)] + [_to_streams(t, d, scr) for d in DILATIONS[1:] for t in (attn, lse)]

    mix_out = _rowwise(
        "attn_mix", s_len, TM,
        [_stream_in(t, TM, d) for d in DILATIONS for t in parts[d]] + [_whole(gains["attn_norm_g"])],
        [_row_out(s_len, D_ATTN, F32, TM), _row_out(s_len, D_ATTN, F32, TM),
         _row_out(s_len, D_ATTN, BF16, TM, col=0, total=D_MODEL)]
        + [_stream_out(s_len, F32, TM, d) for d in DILATIONS[1:] for _ in range(2)], mix_fn,
        scratch=[_stream_scratch(TM)])
    attn, lse, mixed_half = mix_out[:3]
    attn_s = {1: (attn, lse), DILATIONS[1]: tuple(mix_out[3:5]), DILATIONS[2]: tuple(mix_out[5:7])}

    a_cat, w_b, w_c, d_vec = sp
    ts = SCAN_ROWS
    u_c = _to_chunk_order(proj[:, 3 * D_ATTN:])
    (bu,) = _mm("ssm_bu", (s_len // ts, 1, 1), _rows(u_c, ts), _whole(w_b), "nn",
                [_state_out(s_len, 2 * N_STATE, ts)], epilogue=lambda acc: (_lane_blocks(acc),))
    xs = _scan("ssm_scan", bu, a_cat, False)
    w_c_half = lambda part: (w_c, (N_STATE, D_SSM), lambda *_: (part, 0))
    (z_c,) = _mm("ssm_y", (s_len // ts, 1, 1), _state_in(xs[0], ts), w_c_half(0), "nn",
                 [_row_out(s_len, D_SSM, F32, ts)], more=[(_state_in(xs[1], ts), w_c_half(1))], a_prep=_join_lanes,
                 extras=[_rows(u_c, ts), _whole(d_vec)], epilogue=lambda acc, u, dv: (acc + dv * u,))
    z = _from_chunk_order(z_c)

    def glu_epi(acc, z_blk, b, g):
        y = _gelu(z_blk)
        t = acc + b
        ssm = y * _sigmoid(t)
        return y, t, ssm, _rms(ssm, g)

    yg, t_glu, ssm, mixed = _mm(
        "ssm_glu", (nt, 1, 1), _rows(z, TM), _whole(gw["w_glu"]), "nn",
        [_row_out(s_len, D_SSM, F32, TM), _row_out(s_len, D_SSM, F32, TM), _row_out(s_len, D_SSM, F32, TM),
         _row_out(s_len, D_SSM, BF16, TM, col=1, total=D_MODEL)], a_prep=_gelu,
        extras=[_rows(z, TM), _whole(gains["ssm_b_glu"]), _whole(gains["ssm_norm_g"])], epilogue=glu_epi,
        into=[(mixed_half, 3)])

    def out_epi(acc, h_blk, gp, gn):
        h_new = h_blk + _rms(acc, gp)
        return acc, h_new, _rms(h_new, gn)

    o, h2, n2 = _mm(
        "w_out", (nt, 1, 1), _rows(mixed, TM), _whole(gw["w_out"]), "nn",
        [_row_out(s_len, D_MODEL, F32, TM), _row_out(s_len, D_MODEL, F32, TM), _row_out(s_len, D_MODEL, BF16, TM)],
        extras=[_rows(h1, TM), _whole(gains["mix_post_g"]), _whole(gains["ffn2_pre_g"])], epilogue=out_epi)

    h3, _, sv["ffn2"] = _ffn_fwd(s_len, h2, n2, ffn_weight(2), gains["ffn2_post_g"], gains["ffn2_post_g"])
    need(h3, "w_pu", "ple_w_gate")

    (pu,) = _mm("ple_up", (nt, 1, 1), _rows(p_l, TM), _resident(gw["w_pu"]), "nn",
                [_row_out(s_len, D_MODEL, F32, TM)])

    def ple_epi(acc, pu_blk, h_blk, gp, gn):
        h_new = h_blk + _rms(pu_blk * _sigmoid(acc), gp)
        return acc, h_new, _rms(h_new, gn)

    gt, h4, n_next = _mm(
        "ple_gate", (nt, 1, 1), _rows(h3, TM), _whole(gw["ple_w_gate"]), "nn",
        [_row_out(s_len, D_MODEL, F32, TM), _row_out(s_len, D_MODEL, F32, TM), _row_out(s_len, D_MODEL, BF16, TM)],
        extras=[_rows(pu, TM), _rows(h3, TM), _whole(gains["ple_post_g"]), _whole(g_next)], epilogue=ple_epi)

    sv.update(h1=h1, a_in=a_in, proj=proj, qkv=qkv, attn=attn, attn_s=attn_s, xs=xs, u_c=u_c, z=z, yg=yg, t_glu=t_glu,
              ssm=ssm,
              mixed=mixed, o=o, h2=h2, h3=h3, pu=pu, gt=gt, p_l=p_l, w=gw)
    return h4, n_next, sv


def _vjp(fn, args, cot):
    _, pull = jax.vjp(fn, *args)
    return pull(cot)


def _layer_bwd(s_len, dh4, sv, rope, gw, sp, gains, on_partial=None):
    nt = s_len // TM
    gr = {}
    wg = {}

    def ple_fn(dh, pu, gt, g):
        dpu, dgt, dg = _vjp(lambda a, b, c: _rms(a * _sigmoid(b), c), (pu, gt, g), dh)
        return dpu, dgt, dg

    dpu, dgt, gr["ple_post_g"] = _rowwise(
        "ple_bwd", s_len, TM, [_rows(dh4, TM), _rows(sv["pu"], TM), _rows(sv["gt"], TM), _whole(gains["ple_post_g"])],
        [_row_out(s_len, D_MODEL, BF16, TM), _row_out(s_len, D_MODEL, BF16, TM), _col_out(D_MODEL)], ple_fn)

    def square_w_grad(name, lhs, rhs):
        half = D_MODEL // 2
        (dw,) = _mm(name, (2, 1, nt), (lhs, (TM, half), lambda i, j, k: (k, i)),
                    (rhs, (TM, D_MODEL), lambda i, j, k: (k, 0)), "tn",
                    [((D_MODEL, D_MODEL), BF16, (half, D_MODEL), lambda i, j, k: (i, 0), "row")],
                    acc_shape=(half, D_MODEL))
        return dw

    (wg["ple_w_up"],) = _mm(
        "ple_bwd_wup", (1, 1, nt), (sv["p_l"], (TM, PLE_DIM), lambda i, j, k: (k, 0)),
        (dpu, (TM, D_MODEL), lambda i, j, k: (k, 0)), "tn",
        [((N_DEV, PLE_DIM, D_MODEL // N_DEV), BF16, (N_DEV, PLE_DIM, D_MODEL // N_DEV), lambda i, j, k: (0, 0, 0),
          "row")],
        epilogue=lambda acc: (_column_shards(acc, D_MODEL // N_DEV),), acc_shape=(PLE_DIM, D_MODEL))
    wg["ple_w_gate"] = square_w_grad("ple_bwd_wgate", sv["h3"], dgt)

    def ple_dx_epi(acc, dh, f, g):
        dh3 = dh + acc
        df, dg = _vjp(_rms, (f, g), 0.5 * dh3)
        return dh3, df, dg

    dh3, df2, gr["ffn2_post_g"] = _mm(
        "ple_bwd_dx", (nt, 1, 1), _rows(dgt, TM), _whole(gw["ple_w_gate"]), "nt",
        [_row_out(s_len, D_MODEL, F32, TM), _row_out(s_len, D_MODEL, BF16, TM), _col_out(D_MODEL)],
        extras=[_rows(dh4, TM), _rows(sv["ffn2"]["f"], TM), _whole(gains["ffn2_post_g"])], epilogue=ple_dx_epi)

    def ffn2_final(dn, dh, h, g_pre, o, g_post):
        dx, dg_pre = _vjp(_rms, (h, g_pre), dn)
        dh2 = dh + dx
        do, dg_post = _vjp(_rms, (o, g_post), dh2)
        return dh2, do, dg_pre, dg_post

    wg["ffn2_w_gate"], wg["ffn2_w_up"], wg["ffn2_w_down"], (dh2, do, gr["ffn2_pre_g"], gr["mix_post_g"]) = _ffn_bwd(
        s_len, sv["ffn2"], df2, gw["w_gu2"], gw["w_d2"], ffn2_final,
        [_rows(dh3, FFN_TM), _rows(sv["h2"], FFN_TM), _whole(gains["ffn2_pre_g"]), _rows(sv["o"], FFN_TM),
         _whole(gains["mix_post_g"])],
        [_row_out(s_len, D_MODEL, F32, FFN_TM), _row_out(s_len, D_MODEL, BF16, FFN_TM), _col_out(D_MODEL),
         _col_out(D_MODEL)])

    wg["w_out"] = square_w_grad("w_out_bwd_w", sv["mixed"], do)

    def mixed_epi(acc, attn, ssm, yg, t, g_a, g_s, scr):
        dattn, dg_a = _vjp(_rms, (attn, g_a), acc[:, :D_ATTN])
        dssm, dg_s = _vjp(_rms, (ssm, g_s), acc[:, D_ATTN:])
        sg = _sigmoid(t)
        dt = dssm * yg * sg * (1.0 - sg)
        return ([_to_streams(dattn, d, scr) for d in DILATIONS]
                + [dt, dssm * sg, dg_a, dg_s, jnp.sum(dt, axis=0, keepdims=True)])

    res = _mm(
        "w_out_bwd_x", (nt, 1, 1), _rows(do, TM), _whole(gw["w_out"]), "nt",
        [_stream_out(s_len, F32, TM, d) for d in DILATIONS]
        + [_row_out(s_len, D_SSM, BF16, TM), _row_out(s_len, D_SSM, F32, TM),
           _col_out(D_ATTN), _col_out(D_SSM), _col_out(D_SSM)],
        extras=[_rows(sv["attn"], TM), _rows(sv["ssm"], TM), _rows(sv["yg"], TM), _rows(sv["t_glu"], TM),
                _whole(gains["attn_norm_g"]), _whole(gains["ssm_norm_g"])], epilogue=mixed_epi,
        scratch=[_stream_scratch(TM)])
    dattn_s = dict(zip(DILATIONS, res[:3]))
    dt_glu, dyg_dir, gr["attn_norm_g"], gr["ssm_norm_g"], gr["ssm_b_glu"] = res[3:]

    a_cat, w_b, w_c, d_vec = sp
    ts = SCAN_ROWS
    u_spec = (sv["proj"], (TM, D_SSM), lambda i, *_: (i, 3))
    (wg["ssm_w_glu"],) = _mm(
        "ssm_bwd_wglu", (1, 1, nt), (sv["yg"], (TM, D_SSM), lambda i, j, k: (k, 0)),
        (dt_glu, (TM, D_SSM), lambda i, j, k: (k, 0)), "tn",
        [((D_SSM, D_SSM), BF16, (D_SSM, D_SSM), lambda i, j, k: (0, 0), "row")], acc_shape=(D_SSM, D_SSM))

    def gelu_epi(acc, dy_dir, z, u, dv):
        (dz,) = _vjp(_gelu, (z,), acc + dy_dir)
        return dz, dz * dv, jnp.sum(dz * u, axis=0, keepdims=True)

    dz, du_dir, gr["ssm_d"] = _mm(
        "ssm_bwd_glu", (nt, 1, 1), _rows(dt_glu, TM), _whole(gw["w_glu"]), "nt",
        [_row_out(s_len, D_SSM, BF16, TM), _row_out(s_len, D_SSM, F32, TM), _col_out(D_SSM)],
        extras=[_rows(dyg_dir, TM), _rows(sv["z"], TM), u_spec, _whole(d_vec)], epilogue=gelu_epi)

    x_re, x_im = sv["xs"]
    w_c_half = lambda part: (w_c, (N_STATE, D_SSM), lambda *_: (part, 0))
    w_b_half = lambda part: (w_b, (D_SSM, N_STATE), lambda *_: (0, part))

    n_lb = N_STATE // LANES
    n_chunks = s_len // SCAN_CHUNK
    dz_c = _to_chunk_order(dz)

    def w_c_grad(name, x):
        per = 1024 // LANES
        (dw,) = _mm(name, (N_STATE // 1024, 1, nt), (x, (per, TM, LANES), lambda i, j, k: (i, k, 0)),
                    (dz_c, (TM, D_SSM), lambda i, j, k: (k, 0)), "tn",
                    [((N_STATE, D_SSM), F32, (1024, D_SSM), lambda i, j, k: (i, 0), "row")], acc_shape=(1024, D_SSM),
                    a_prep=_join_lanes)
        return dw

    d_w_c = (w_c_grad("ssm_bwd_wc_re", x_re), w_c_grad("ssm_bwd_wc_im", x_im))
    (dxs,) = _mm("ssm_bwd_dx", (s_len // ts, 1, 1), _rows(dz_c, ts), _whole(w_c), "nt",
                 [_state_out(s_len, 2 * N_STATE, ts)], epilogue=lambda acc: (_lane_blocks(acc),))
    g_re, g_im, da_re, da_im = _scan("ssm_scan_rev", dxs, _conj(a_cat), True, prev_state=(x_re, x_im))
    d_a = jnp.concatenate([da_re.reshape(1, N_STATE), da_im.reshape(1, N_STATE)], axis=1)

    def w_b_grad(name, g):
        (dw,) = _mm(name, (1, 1, nt), (sv["u_c"], (TM, D_SSM), lambda i, j, k: (k, 0)),
                    (g, (n_lb, TM, LANES), lambda i, j, k: (0, k, 0)), "tn",
                    [((D_SSM, N_STATE), F32, (D_SSM, N_STATE), lambda i, j, k: (0, 0), "row")],
                    acc_shape=(D_SSM, N_STATE), b_prep=_join_lanes)
        return dw

    d_w_b = (w_b_grad("ssm_bwd_wb_re", g_re), w_b_grad("ssm_bwd_wb_im", g_im))
    (du_c,) = _mm("ssm_bwd_du", (s_len // ts, 1, 1), _state_in(g_re, ts), w_b_half(0), "nt",
                  [_row_out(s_len, D_SSM, F32, ts)], more=[(_state_in(g_im, ts), w_b_half(1))], a_prep=_join_lanes)
    du = (_from_chunk_order(du_c) + du_dir).astype(BF16)
    sum_tm = 2 * BAND
    n_sum = s_len // sum_tm
    ins = []
    for d in DILATIONS:
        dq_p, dka, dkb, dva, dvb = _attn_bwd(s_len, *sv["qkv"][d], dattn_s[d], *sv["attn_s"][d], d)
        if d == 1:
            nxt = lambda t: (t, (sum_tm, D_ATTN), lambda i: (jnp.minimum(i + 1, n_sum - 1), 0))
            ins += [_rows(dq_p, sum_tm), _rows(dka, sum_tm), _rows(dkb, sum_tm), nxt(dkb), _rows(dva, sum_tm),
                    _rows(dvb, sum_tm), nxt(dvb)]
        else:
            ins += [_stream_in(dq_p, sum_tm, d), _stream_in(dka, sum_tm, d), _stream_in(dkb, sum_tm, d, BAND, n_sum),
                    _stream_in(dva, sum_tm, d), _stream_in(dvb, sum_tm, d, BAND, n_sum)]
    c, s1, s2 = rope
    ins += [_rows(c, sum_tm), _rows(s1, sum_tm), _rows(s2, sum_tm), _rows(du, sum_tm)]

    def qkv_fn(*args):
        i = pl.program_id(0)
        blocks, scr = [b.astype(F32) for b in args[:-1]], args[-1]
        dq_t, dka, dkb, dkb_next, dva, dvb, dvb_next = blocks[:7]
        more = i + 1 < n_sum
        ahead = lambda cur, nxt: jnp.concatenate([cur[BAND:], jnp.where(more, nxt[:BAND], 0.0)], axis=0)
        dk_t = dka + ahead(dkb, dkb_next)
        dv_t = dva + ahead(dvb, dvb_next)
        at = 7
        for d in DILATIONS[1:]:
            dq_p, dka, dkb, dva, dvb = blocks[at:at + 5]
            at += 5
            live = i + BAND // (sum_tm // d) < n_sum
            dq_t = dq_t + _from_streams(dq_p, d, scr)
            dk_t = dk_t + _from_streams(dka + jnp.where(live, dkb, 0.0), d, scr)
            dv_t = dv_t + _from_streams(dva + jnp.where(live, dvb, 0.0), d, scr)
        cc, a1, a2, du_blk = blocks[at:at + 4]
        return (jnp.concatenate([_rope_transposed(dq_t, cc, a1, a2), _rope_transposed(dk_t, cc, a1, a2), dv_t,
                                 du_blk.astype(F32)], axis=1),)

    (dproj,) = _rowwise("attn_bwd_sum", s_len, sum_tm, ins, [_row_out(s_len, 2 * D_MODEL, BF16, sum_tm)], qkv_fn,
                        scratch=[_stream_scratch(sum_tm)])

    (wg["w_in"],) = _mm(
        "w_in_bwd_w", (1, 2, nt), (sv["a_in"], (TM, D_MODEL), lambda i, j, k: (k, 0)),
        (dproj, (TM, D_MODEL), lambda i, j, k: (k, j)), "tn",
        [((N_DEV, D_MODEL, 2 * D_MODEL // N_DEV), BF16, (N_DEV // 2, D_MODEL, 2 * D_MODEL // N_DEV),
          lambda i, j, k: (j, 0, 0), "row")],
        epilogue=lambda acc: (_column_shards(acc, 2 * D_MODEL // N_DEV),), acc_shape=(D_MODEL, D_MODEL))

    tie = on_partial(wg) if on_partial is not None else None
    mix_pre_g = gains["mix_pre_g"] if tie is None else gains["mix_pre_g"] + tie

    def in_epi(acc, dh, h, g_pre, f, g_post):
        dx, dg_pre = _vjp(_rms, (h, g_pre), acc)
        dh1 = dh + dx
        df, dg_post = _vjp(_rms, (f, g_post), 0.5 * dh1)
        return dh1, df, dg_pre, dg_post

    dh1, df1, gr["mix_pre_g"], gr["ffn1_post_g"] = _mm(
        "w_in_bwd_x", (nt, 1, 1), _rows(dproj, TM), _resident(gw["w_in"]), "nt",
        [_row_out(s_len, D_MODEL, F32, TM), _row_out(s_len, D_MODEL, BF16, TM), _col_out(D_MODEL), _col_out(D_MODEL)],
        extras=[_rows(dh2, TM), _rows(sv["h1"], TM), _whole(mix_pre_g), _rows(sv["ffn1"]["f"], TM),
                _whole(gains["ffn1_post_g"])], epilogue=in_epi)

    def ffn1_final(dn, dh, h, g_pre):
        dx, dg_pre = _vjp(_rms, (h, g_pre), dn)
        return dh + dx, dg_pre

    wg["ffn1_w_gate"], wg["ffn1_w_up"], wg["ffn1_w_down"], (dh0, gr["ffn1_pre_g"]) = _ffn_bwd(
        s_len, sv["ffn1"], df1, gw["w_gu1"], gw["w_d1"], ffn1_final,
        [_rows(dh1, FFN_TM), _rows(sv["ffn1"]["h"], FFN_TM), _whole(gains["ffn1_pre_g"])],
        [_row_out(s_len, D_MODEL, F32, FFN_TM), _col_out(D_MODEL)])

    return dh0, wg, gr, (d_a, d_w_b, d_w_c)


def _peers():
    x, y, c = lax.axis_index("x"), lax.axis_index("y"), lax.axis_index("c")
    me = 4 * x + 2 * y + c
    peers = []
    for k in range(1, N_DEV):
        kx, ky, kc = (k >> 2) & 1, (k >> 1) & 1, k & 1
        px, py, pc = x ^ kx, y ^ ky, c ^ kc
        peers.append(((px, py, pc), 4 * px + 2 * py + pc))
    return me, peers


_HBM_SPEC = pl.BlockSpec(memory_space=pltpu.HBM)
_SEM_SPEC = pl.BlockSpec(memory_space=pltpu.SEMAPHORE)
_DATAFLOW = pltpu.SideEffectType.DATAFLOW_SIDE_EFFECTING


def _device_index():
    return 4 * lax.axis_index("x") + 2 * lax.axis_index("y") + lax.axis_index("c")


def _landing(arrays, scatter):
    me = _device_index()
    out = []
    for a, scattered in zip(arrays, scatter):
        own = lax.dynamic_index_in_dim(a, me, 0, keepdims=True) if scattered else a[None]
        buf = lax.empty((N_DEV,) + own.shape[1:], a.dtype)
        out.append(lax.dynamic_update_slice_in_dim(buf, own, me, 0))
    return out


def _split_copies(src_refs, land_refs, send_sems, recv_sems, scatter):
    me, peers = _peers()
    pairs = []
    for t in range(len(src_refs)):
        for k, (peer, peer_id) in enumerate(peers):
            src = src_refs[t].at[peer_id] if scatter[t] else src_refs[t]
            sem = t * (N_DEV - 1) + k
            mk = lambda slot, src=src, t=t, sem=sem, peer=peer: pltpu.make_async_remote_copy(
                src_ref=src, dst_ref=land_refs[t].at[slot], send_sem=send_sems.at[sem], recv_sem=recv_sems.at[sem],
                device_id=peer, device_id_type=pl.DeviceIdType.MESH)
            pairs.append((functools.partial(mk, me), functools.partial(mk, peer_id)))
    return pairs


def _exchange_start(name, arrays, scatter, order_after):
    n = len(arrays)
    landing = _landing(arrays, scatter)

    def body(*refs):
        src_refs, land_refs = refs[:n], refs[n:2 * n]
        send_sems, recv_sems = refs[2 * n + 1], refs[2 * n + 2]
        token_ref = refs[-1]
        for outgoing, _ in _split_copies(src_refs, land_refs, send_sems, recv_sems, scatter):
            outgoing().start()
        token_ref[...] = jnp.zeros_like(token_ref)

    sem_shape = pltpu.SemaphoreType.DMA((n * (N_DEV - 1),))
    thru = [pltpu.HBM(a.shape, a.dtype) for a in list(arrays) + landing]
    hbm = lambda t: pltpu.with_memory_space_constraint(t, pltpu.HBM)
    res = pl.pallas_call(
        body,
        name=name,
        in_specs=[_HBM_SPEC] * (2 * n) + [pl.BlockSpec(memory_space=pl.ANY)],
        out_specs=[_SEM_SPEC, _SEM_SPEC] + [_HBM_SPEC] * (2 * n) + [pl.BlockSpec(memory_space=pltpu.VMEM)],
        out_shape=[sem_shape, sem_shape] + thru + [jax.ShapeDtypeStruct((8, 128), F32)],
        input_output_aliases={i: 2 + i for i in range(2 * n)},
        compiler_params=pltpu.CompilerParams(has_side_effects=_DATAFLOW),
    )(*[hbm(t) for t in list(arrays) + landing], order_after)
    return (res[0], res[1], res[2:2 + n], res[2 + n:2 + 2 * n]), res[-1]


def _exchange_wait(name, handle, scatter, order_after):
    send_sems, recv_sems, sources, landing = handle
    n = len(sources)

    def body(*refs):
        src_refs, land_refs = refs[:n], refs[n:2 * n]
        for outgoing, arrival in _split_copies(src_refs, land_refs, refs[2 * n], refs[2 * n + 1], scatter):
            outgoing().wait_send()
            arrival().wait_recv()

    thru = [pltpu.HBM(a.shape, a.dtype) for a in list(sources) + list(landing)]
    res = pl.pallas_call(
        body,
        name=name,
        in_specs=[_HBM_SPEC] * (2 * n) + [_SEM_SPEC, _SEM_SPEC, pl.BlockSpec(memory_space=pl.ANY)],
        out_specs=[_HBM_SPEC] * (2 * n),
        out_shape=thru,
        input_output_aliases={i: i for i in range(2 * n)},
        compiler_params=pltpu.CompilerParams(has_side_effects=_DATAFLOW),
    )(*sources, *landing, send_sems, recv_sems, order_after)
    return list(res[n:])


def _adam_math(g, w, m, v):
    m = ADAM_B1 * m + (1.0 - ADAM_B1) * g
    v = ADAM_B2 * v + (1.0 - ADAM_B2) * (g * g)
    m_hat = m / (1.0 - ADAM_B1 ** ADAM_STEP)
    v_hat = v / (1.0 - ADAM_B2 ** ADAM_STEP)
    delta = -ADAM_LR * (m_hat / (jnp.sqrt(v_hat) + ADAM_EPS) + ADAM_WD * w)
    return delta, m, v


def _adamw(name, recv, recv_block, recv_map, w, m, v, tr, layer=None, prev=None, after=None):
    def fn(r, wb, mb, vb, *token):
        g = r[0].astype(F32)
        for s in range(1, N_DEV):
            g = g + r[s].astype(F32)
        return (g,) + _adam_math(g, wb, mb, vb) + tuple(jnp.zeros_like(t) for t in token)

    rows, cols = w.shape[-2:]
    if layer is None:
        spec = lambda t: _rows(t, tr)
        out = _row_out(rows, cols, F32, tr)
    else:
        spec = lambda t: (t, (None, tr, cols), lambda i: (layer, i, 0))
        out = (w.shape, F32, (None, tr, cols), lambda i: (layer, i, 0), "row")
    ins = [(recv, recv_block, recv_map), spec(w), spec(m), spec(v)]
    outs = [out] * 4
    if after is not None:
        ins.append(_whole(after))
        outs.append((after.shape, F32, after.shape, lambda i: (0, 0), "row"))
    into = [] if prev is None else [(t, k) for k, t in enumerate(prev)]
    return _rowwise(name, rows, tr, ins, outs, fn, into=into)


def _pack_small(tensors):
    flat = jnp.concatenate([tensors[n].reshape(-1).astype(F32) for n in SMALL])
    padded = -(-flat.shape[0] // SMALL_PAD) * SMALL_PAD
    return jnp.pad(flat, (0, padded - flat.shape[0])).reshape(padded // SMALL_LANES, SMALL_LANES)


def _local_step(x, p, positions, loss_target, w, layer_weights, layer_done=None, layer_partial=None):
    s_len = x.shape[0]
    depth = p.shape[0]
    rope = _rope_tables(positions)
    gains = [{n: w[n][l].reshape(1, -1) for n in SMALL if w[n].ndim == 2 and n != "ssm_log_dt"} for l in range(depth)]
    ssm_args = lambda l: tuple(w[n][l] for n in ("ssm_lam_re", "ssm_lam_im", "ssm_log_dt", "ssm_b_re", "ssm_b_im",
                                                  "ssm_c_re", "ssm_c_im"))
    sps, pulls = [], []
    for l in range(depth):
        (a_cat, w_b, w_c), pull = jax.vjp(_ssm_params, *ssm_args(l))
        w_b = jnp.concatenate([t.astype(BF16) for t in w_b], axis=1)
        w_c = jnp.concatenate([t.astype(BF16) for t in w_c], axis=0)
        sps.append((a_cat, w_b, w_c, w["ssm_d"][l].reshape(1, D_SSM)))
        pulls.append(pull)

    (n,) = _rowwise("pre_norm", s_len, TM, [_rows(x, TM), _whole(gains[0]["ffn1_pre_g"])],
                    [_row_out(s_len, D_MODEL, BF16, TM)], lambda a, g: (_rms(a, g),))
    h = x
    saved = []
    for l in range(depth):
        g_next = gains[l + 1]["ffn1_pre_g"] if l + 1 < depth else gains[l]["ffn1_pre_g"]
        h, n, sv = _layer_fwd(s_len, h, n, p[l], rope, layer_weights(l, h), sps[l], gains[l], g_next)
        saved.append(sv)

    def loss_fn(y, t):
        e = y - t
        return e * (1.0 / D_MODEL), jnp.sum(e * e, axis=0, keepdims=True)

    dh, sq = _rowwise("loss", s_len, TM, [_rows(h, TM), _rows(loss_target, TM)],
                      [_row_out(s_len, D_MODEL, F32, TM), _col_out(D_MODEL)], loss_fn)
    loss = 0.5 * jnp.sum(sq) / D_MODEL

    w_grads, small_grads = [None] * depth, [None] * depth
    tie = None
    for l in reversed(range(depth)):
        g_l = gains[l] if tie is None else dict(gains[l], ple_post_g=gains[l]["ple_post_g"] + tie)
        partial = None if layer_partial is None else functools.partial(layer_partial, l)
        dh, wg, gr, (d_a, d_w_b, d_w_c) = _layer_bwd(s_len, dh, saved[l], rope, saved[l]["w"], sps[l], g_l, partial)
        d_lre, d_lim, d_dt, d_bre, d_bim, d_cre, d_cim = pulls[l]((d_a, d_w_b, d_w_c))
        gr.update(ssm_lam_re=d_lre, ssm_lam_im=d_lim, ssm_log_dt=d_dt, ssm_b_re=d_bre, ssm_b_im=d_bim,
                  ssm_c_re=d_cre, ssm_c_im=d_cim)
        gr = {n: g.reshape(w[n].shape[1:]) for n, g in gr.items()}
        w_grads[l], small_grads[l] = wg, gr
        if layer_done is not None:
            tie = layer_done(l, wg, gr, dh)
    return loss, dh, w_grads, small_grads


GROUP_OF = {"ffn1_w_gate": "g1", "ffn1_w_up": "u1", "ffn2_w_gate": "g2", "ffn2_w_up": "u2", "ffn1_w_down": "d1",
            "ffn2_w_down": "d2", "w_out": "out", "ple_w_gate": "pg", "w_in": "in", "ple_w_up": "pu", "ssm_w_glu": "glu"}
GROUPS = tuple(GROUP_OF[n] for n in SHARDED)
FIRST_PARTS = (("g1", "u1"), ("d1",), ("in", "glu", "out"), ("g2", "u2", "d2", "pu", "pg"))
_COLUMN_VIEWS = {"w_gu1": ("g1", "u1"), "w_gu2": ("g2", "u2"), "w_in": ("in",), "w_pu": ("pu",)}
_ROW_VIEWS = {"d1": "w_d1", "d2": "w_d2", "out": "w_out", "pg": "ple_w_gate", "glu": "w_glu"}


def _shard_groups(w, l):
    return {group: w[n][l].astype(BF16) for n, group in GROUP_OF.items()}


def _gathered_views(gathered, after=None):
    views = {}
    for name, groups in _COLUMN_VIEWS.items():
        if all(g in gathered for g in groups):
            views[name] = _natural_cols("relayout_" + name, [gathered[g] for g in groups], after)
            after = None
    for group, name in _ROW_VIEWS.items():
        if group in gathered:
            g = gathered[group]
            views[name] = g.reshape(N_DEV * g.shape[1], g.shape[2])
    return views


def _grad_groups(wg):
    rows = lambda t: t if t.ndim == 3 else t.reshape(N_DEV, t.shape[0] // N_DEV, t.shape[1])
    return {GROUP_OF[n]: rows(t) for n, t in wg.items()}


def _update_sharded(name, recv_groups, w, m, v, layer, prev, after=None):
    recv = recv_groups[GROUP_OF[name]]
    rows, cols = w.shape[1:]
    tr = rows if rows <= 256 else (rows // 2 if rows % 256 else 256)
    return _adamw(f"adamw_{name}", recv, (N_DEV, tr, cols), lambda i: (0, i, 0), w, m, v, tr, layer, prev, after)


def kernel(x, p, positions, ffn1_pre_g, ffn1_w_gate, ffn1_w_up, ffn1_w_down, ffn1_post_g, mix_pre_g, w_in, attn_norm_g, ssm_lam_re, ssm_lam_im, ssm_log_dt, ssm_b_re, ssm_b_im, ssm_c_re, ssm_c_im, ssm_d, ssm_w_glu, ssm_b_glu, ssm_norm_g, w_out, mix_post_g, ffn2_pre_g, ffn2_w_gate, ffn2_w_up, ffn2_w_down, ffn2_post_g, ple_w_up, ple_w_gate, ple_post_g, loss_target, m_ffn1_pre_g, m_ffn1_w_gate, m_ffn1_w_up, m_ffn1_w_down, m_ffn1_post_g, m_mix_pre_g, m_w_in, m_attn_norm_g, m_ssm_lam_re, m_ssm_lam_im, m_ssm_log_dt, m_ssm_b_re, m_ssm_b_im, m_ssm_c_re, m_ssm_c_im, m_ssm_d, m_ssm_w_glu, m_ssm_b_glu, m_ssm_norm_g, m_w_out, m_mix_post_g, m_ffn2_pre_g, m_ffn2_w_gate, m_ffn2_w_up, m_ffn2_w_down, m_ffn2_post_g, m_ple_w_up, m_ple_w_gate, m_ple_post_g, v_ffn1_pre_g, v_ffn1_w_gate, v_ffn1_w_up, v_ffn1_w_down, v_ffn1_post_g, v_mix_pre_g, v_w_in, v_attn_norm_g, v_ssm_lam_re, v_ssm_lam_im, v_ssm_log_dt, v_ssm_b_re, v_ssm_b_im, v_ssm_c_re, v_ssm_c_im, v_ssm_d, v_ssm_w_glu, v_ssm_b_glu, v_ssm_norm_g, v_w_out, v_mix_post_g, v_ffn2_pre_g, v_ffn2_w_gate, v_ffn2_w_up, v_ffn2_w_down, v_ffn2_post_g, v_ple_w_up, v_ple_w_gate, v_ple_post_g):
    args = dict(locals())
    w = {n: args[n] for n in WEIGHTS}
    mom = {n: args["m_" + n] for n in WEIGHTS}
    var = {n: args["v_" + n] for n in WEIGHTS}
    depth = p.shape[0]

    gathers, scatters, recv, tokens = {}, {}, {}, {}

    def start(name, groups, names, scatter, order_after):
        flags = scatter if isinstance(scatter, tuple) else (scatter,) * len(names)
        handle, token = _exchange_start(name, [groups[n] for n in names], flags, order_after)
        return (handle, names, flags), token

    def finish(name, pending, order_after):
        handle, names, flags = pending
        return dict(zip(names, _exchange_wait(name, handle, flags, order_after)))

    def gather_next(l, order_after):
        if l + 1 >= depth:
            return None
        gathers[l + 1], token = start("all_gather_start", _shard_groups(w, l + 1), GROUPS, False, order_after)
        return token

    def layer_weights(l, h):
        views = {}
        if l > 0:
            got = finish("all_gather_wait", gathers.pop(l), h)
            views.update(_gathered_views(got, gather_next(l, got[GROUPS[0]])))
            return lambda name, after: views[name]

        shards = _shard_groups(w, 0)
        pending, token = start("all_gather_start_0", shards, FIRST_PARTS[0], False, h)
        got = finish("all_gather_wait_0", pending, token)
        waiting = {}
        token = got[FIRST_PARTS[0][0]]
        for i, part in enumerate(FIRST_PARTS[1:], 1):
            waiting[i], token = start(f"all_gather_start_{i}", shards, part, False, token)
        views.update(_gathered_views(got, gather_next(0, token)))

        def weight(name, after):
            for i in sorted(waiting):
                if name not in views:
                    views.update(_gathered_views(finish(f"all_gather_wait_{i}", waiting.pop(i), after)))
            return views[name]

        return weight

    small_l = lambda t, l: _pack_small({n: t[n][l] for n in SMALL})
    scatter_flags = lambda names: tuple(n != "small" for n in names)
    late = ("g1", "u1", "d1", "small")
    early = tuple(g for g in GROUPS if g not in late)

    def layer_partial(l, wg):
        if l > 0:
            return None
        scatters["early"], token = start("reduce_scatter_start_a", _grad_groups(wg), early, scatter_flags(early),
                                         wg["w_in"])
        return token[0, 0]

    def layer_done(l, wg, gr, dh):
        if l + 1 < depth:
            recv[l + 1] = finish("reduce_scatter_wait", scatters.pop(l + 1), dh)
        groups = dict(_grad_groups(wg), small=_pack_small(gr))
        names = late if l == 0 else GROUPS + ("small",)
        scatters[l], tokens[l] = start("reduce_scatter_start_b" if l == 0 else "reduce_scatter_start", groups, names,
                                       scatter_flags(names), dh)
        return tokens[l][0, 0]

    loss, dx, w_grads, small_grads = _local_step(x[0], p[:, 0], positions[0], loss_target[0], w, layer_weights,
                                                 layer_done, layer_partial)
    loss = lax.psum(loss, MESH_AXES)

    results = {n: None for n in SHARDED}
    last = tokens[0]
    for l in reversed(range(1, depth)):
        for n in SHARDED:
            *results[n], last = _update_sharded(n, recv[l], w[n], mom[n], var[n], l, results[n], last)
    recv[0] = dict(finish("reduce_scatter_wait_a", scatters.pop("early"), last))
    recv[0].update(finish("reduce_scatter_wait_b", scatters.pop(0), last))
    for n in SHARDED:
        results[n] = _update_sharded(n, recv[0], w[n], mom[n], var[n], 0, results[n])
    result = dict(results)

    recv_small = jnp.concatenate([recv[l]["small"] for l in range(depth)], axis=1)
    packed = [jnp.concatenate([small_l(t, l) for l in range(depth)], axis=0) for t in (w, mom, var)]
    small_out = _adamw("adamw_small", recv_small, (N_DEV, 8, SMALL_LANES), lambda i: (0, i, 0), *packed, 8)
    for q in range(4):
        flat = small_out[q].reshape(depth, -1)
        off = 0
        for n in SMALL:
            size = math.prod(w[n].shape[1:])
            result.setdefault(n, [None] * 4)[q] = flat[:, off:off + size].reshape(w[n].shape)
            off += size

    outputs = [loss, dx[None]]
    for q in range(4):
        outputs += [result[n][q] for n in WEIGHTS]
    return tuple(outputs)
```
